```python
import jax, jax.numpy as jnp
from jax import lax
import numpy as np

D_MODEL = 2048
BATCH = 8
SEQ = 2048
DEPTH = 2

HEAD_DIM = 64
N_Q_HEADS = 16
N_KV_HEADS = 4
Q_PER_KV = N_Q_HEADS // N_KV_HEADS
ATTN_WIDTH = N_Q_HEADS * HEAD_DIM
KV_WIDTH = N_KV_HEADS * HEAD_DIM
WINDOW = 128
BLOCK = 128
ROPE_THETA = 10000.0
SGU_GROUPS = 8
SGU_GROUP_DIM = 128
SGU_WIDTH = SGU_GROUPS * SGU_GROUP_DIM
CHUNK = 128
N_BRANCHES = 2
D_FF = ((-(-8 * D_MODEL // 3) + 255) // 256) * 256
IN_WIDTH = ATTN_WIDTH + 2 * KV_WIDTH + 2 * SGU_WIDTH + N_BRANCHES * D_MODEL
EPS = 1e-6

kernel_name = "hybrid_gated_swa_sgu_block"


def rms_norm(x, g):
    xf = x.astype(jnp.float32)
    y = xf * lax.rsqrt(jnp.mean(xf * xf, axis=-1, keepdims=True) + EPS)
    return (y * g.astype(jnp.float32)).astype(x.dtype)


def rope_tables(seq):
    pos = jnp.arange(seq, dtype=jnp.float32)
    inv_freq = jnp.power(ROPE_THETA, -jnp.arange(0, HEAD_DIM, 2, dtype=jnp.float32) / HEAD_DIM)
    ang = pos[:, None] * inv_freq[None, :]
    return jnp.cos(ang), jnp.sin(ang)


def apply_rope(x, cos, sin):
    xf = x.astype(jnp.float32)
    half = HEAD_DIM // 2
    x1, x2 = xf[..., :half], xf[..., half:]
    c, s = cos[None, :, None, :], sin[None, :, None, :]
    return jnp.concatenate([x1 * c - x2 * s, x2 * c + x1 * s], axis=-1).astype(x.dtype)


def sliding_window_attention(q, k, v, sinks):
    B, S = q.shape[0], q.shape[1]
    nb = S // BLOCK
    qb = q.reshape(B, nb, BLOCK, N_KV_HEADS, Q_PER_KV, HEAD_DIM)
    kb = k.reshape(B, nb, BLOCK, N_KV_HEADS, HEAD_DIM)
    vb = v.reshape(B, nb, BLOCK, N_KV_HEADS, HEAD_DIM)

    def with_prev(t):
        prev = jnp.pad(t[:, :-1], ((0, 0), (1, 0), (0, 0), (0, 0), (0, 0)))
        return jnp.concatenate([prev, t], axis=2)

    kw, vw = with_prev(kb), with_prev(vb)
    scale = HEAD_DIM ** -0.5
    scores = jnp.einsum('bnqhgd,bnkhd->bnhgqk', qb, kw).astype(jnp.float32) * scale
    q_pos = jnp.arange(BLOCK)[:, None] + BLOCK
    k_pos = jnp.arange(2 * BLOCK)[None, :]
    diff = q_pos - k_pos
    band = (diff >= 0) & (diff < WINDOW)
    valid = (jnp.arange(nb)[:, None, None] > 0) | (k_pos >= BLOCK)[None]
    mask = (band[None] & valid)[None, :, None, None]
    scores = jnp.where(mask, scores, -1e30)
    sink = jnp.broadcast_to(
        sinks.astype(jnp.float32).reshape(N_KV_HEADS, Q_PER_KV)[None, None, :, :, None, None],
        scores.shape[:-1] + (1,))
    probs = jax.nn.softmax(jnp.concatenate([scores, sink], axis=-1), axis=-1)[..., :-1]
    out = jnp.einsum('bnhgqk,bnkhd->bnqhgd', probs.astype(v.dtype), vw)
    return out.reshape(B, S, ATTN_WIDTH)


def chunked_sgu(uv, w_s, b_s, ln_g, ln_b):
    B, S = uv.shape[0], uv.shape[1]
    nc = S // CHUNK
    u, v = uv[..., :SGU_WIDTH], uv[..., SGU_WIDTH:]
    vf = v.astype(jnp.float32).reshape(B, S, SGU_GROUPS, SGU_GROUP_DIM)
    mu = jnp.mean(vf, axis=-1, keepdims=True)
    var = jnp.mean(jnp.square(vf - mu), axis=-1, keepdims=True)
    vn = ((vf - mu) * lax.rsqrt(var + EPS) * ln_g.reshape(SGU_GROUPS, SGU_GROUP_DIM)
          + ln_b.reshape(SGU_GROUPS, SGU_GROUP_DIM)).astype(v.dtype)
    vc = vn.reshape(B, nc, CHUNK, SGU_GROUPS, SGU_GROUP_DIM)
    tri = jnp.tril(jnp.ones((CHUNK, CHUNK), dtype=bool))
    w = jnp.where(tri[None], w_s, jnp.zeros_like(w_s))
    s = jnp.einsum('gij,bnjgd->bnigd', w, vc) + jnp.transpose(b_s)[None, None, :, :, None]
    return u * s.reshape(B, S, SGU_WIDTH)


def _fwd_setup_inputs(seed: int = 0) -> dict:
    key = jax.random.key(seed)
    ks = jax.random.split(key, 17)
    D = D_MODEL

    def nrm(k, shape, scale):
        return jax.random.normal(k, shape, jnp.float32) * scale

    return {
        "x": nrm(ks[0], (BATCH, SEQ, D), 1.0),
        "mix_norm": 1.0 + nrm(ks[1], (DEPTH, D), 0.02),
        "w_in": nrm(ks[2], (DEPTH, D, IN_WIDTH), D ** -0.5),
        "q_norm": 1.0 + nrm(ks[3], (DEPTH, HEAD_DIM), 0.02),
        "k_norm": 1.0 + nrm(ks[4], (DEPTH, HEAD_DIM), 0.02),
        "sinks": nrm(ks[5], (DEPTH, N_Q_HEADS), 0.5),
        "sgu_ln_g": 1.0 + nrm(ks[6], (DEPTH, SGU_WIDTH), 0.02),
        "sgu_ln_b": nrm(ks[7], (DEPTH, SGU_WIDTH), 0.02),
        "w_spatial": nrm(ks[8], (DEPTH, SGU_GROUPS, CHUNK, CHUNK), 0.5 * CHUNK ** -0.5),
        "b_spatial": 1.0 + nrm(ks[9], (DEPTH, SGU_GROUPS, CHUNK), 0.02),
        "w_attn_branch": nrm(ks[10], (DEPTH, ATTN_WIDTH, D), ATTN_WIDTH ** -0.5),
        "w_sgu_branch": nrm(ks[11], (DEPTH, SGU_WIDTH, D), SGU_WIDTH ** -0.5),
        "w_out": nrm(ks[12], (DEPTH, D, D), D ** -0.5),
        "ffn_norm": 1.0 + nrm(ks[13], (DEPTH, D), 0.02),
        "w_gate": nrm(ks[14], (DEPTH, D, D_FF), D ** -0.5),
        "w_up": nrm(ks[15], (DEPTH, D, D_FF), D ** -0.5),
        "w_down": nrm(ks[16], (DEPTH, D_FF, D), D_FF ** -0.5),
    }


def _fwd_reference(x, mix_norm, w_in, q_norm, k_norm, sinks, sgu_ln_g, sgu_ln_b, w_spatial,
              b_spatial, w_attn_branch, w_sgu_branch, w_out, ffn_norm, w_gate, w_up, w_down):
    B, S = x.shape[0], x.shape[1]
    cos, sin = rope_tables(S)
    cuts = [ATTN_WIDTH, ATTN_WIDTH + KV_WIDTH, ATTN_WIDTH + 2 * KV_WIDTH,
            ATTN_WIDTH + 2 * KV_WIDTH + 2 * SGU_WIDTH]
    for l in range(DEPTH):
        h = rms_norm(x, mix_norm[l])
        proj = h @ w_in[l]
        q, k, v, uv, gate_logits = jnp.split(proj, cuts, axis=-1)
        q = apply_rope(rms_norm(q.reshape(B, S, N_Q_HEADS, HEAD_DIM), q_norm[l]), cos, sin)
        k = apply_rope(rms_norm(k.reshape(B, S, N_KV_HEADS, HEAD_DIM), k_norm[l]), cos, sin)
        v = v.reshape(B, S, N_KV_HEADS, HEAD_DIM)
        branch_a = sliding_window_attention(q, k, v, sinks[l]) @ w_attn_branch[l]
        branch_b = chunked_sgu(jax.nn.gelu(uv), w_spatial[l], b_spatial[l],
                               sgu_ln_g[l], sgu_ln_b[l]) @ w_sgu_branch[l]
        gates = jax.nn.sigmoid(gate_logits)
        merged = gates[..., :D_MODEL] * branch_a + gates[..., D_MODEL:] * branch_b
        x = x + merged @ w_out[l]
        h2 = rms_norm(x, ffn_norm[l])
        x = x + (jax.nn.silu(h2 @ w_gate[l]) * (h2 @ w_up[l])) @ w_down[l]
    return x


import jax as _jax
import jax.numpy as _jnp

TWIN_FORMAT = 'train_step'
FWD_PARAMS = ['x', 'mix_norm', 'w_in', 'q_norm', 'k_norm', 'sinks', 'sgu_ln_g', 'sgu_ln_b', 'w_spatial', 'b_spatial', 'w_attn_branch', 'w_sgu_branch', 'w_out', 'ffn_norm', 'w_gate', 'w_up', 'w_down']
TWIN_WEIGHTS = ['mix_norm', 'w_in', 'q_norm', 'k_norm', 'sinks', 'sgu_ln_g', 'sgu_ln_b', 'w_spatial', 'b_spatial', 'w_attn_branch', 'w_sgu_branch', 'w_out', 'ffn_norm', 'w_gate', 'w_up', 'w_down']
TWIN_DIFF_INPUT = 'x'
TWIN_INPUTS = ['x', 'mix_norm', 'w_in', 'q_norm', 'k_norm', 'sinks', 'sgu_ln_g', 'sgu_ln_b', 'w_spatial', 'b_spatial', 'w_attn_branch', 'w_sgu_branch', 'w_out', 'ffn_norm', 'w_gate', 'w_up', 'w_down', 'loss_target', 'm_mix_norm', 'm_w_in', 'm_q_norm', 'm_k_norm', 'm_sinks', 'm_sgu_ln_g', 'm_sgu_ln_b', 'm_w_spatial', 'm_b_spatial', 'm_w_attn_branch', 'm_w_sgu_branch', 'm_w_out', 'm_ffn_norm', 'm_w_gate', 'm_w_up', 'm_w_down', 'v_mix_norm', 'v_w_in', 'v_q_norm', 'v_k_norm', 'v_sinks', 'v_sgu_ln_g', 'v_sgu_ln_b', 'v_w_spatial', 'v_b_spatial', 'v_w_attn_branch', 'v_w_sgu_branch', 'v_w_out', 'v_ffn_norm', 'v_w_gate', 'v_w_up', 'v_w_down']
TWIN_OUTPUTS = ['loss', 'grad_x', 'grad_mix_norm', 'grad_w_in', 'grad_q_norm', 'grad_k_norm', 'grad_sinks', 'grad_sgu_ln_g', 'grad_sgu_ln_b', 'grad_w_spatial', 'grad_b_spatial', 'grad_w_attn_branch', 'grad_w_sgu_branch', 'grad_w_out', 'grad_ffn_norm', 'grad_w_gate', 'grad_w_up', 'grad_w_down', 'delta_mix_norm', 'delta_w_in', 'delta_q_norm', 'delta_k_norm', 'delta_sinks', 'delta_sgu_ln_g', 'delta_sgu_ln_b', 'delta_w_spatial', 'delta_b_spatial', 'delta_w_attn_branch', 'delta_w_sgu_branch', 'delta_w_out', 'delta_ffn_norm', 'delta_w_gate', 'delta_w_up', 'delta_w_down', 'new_m_mix_norm', 'new_m_w_in', 'new_m_q_norm', 'new_m_k_norm', 'new_m_sinks', 'new_m_sgu_ln_g', 'new_m_sgu_ln_b', 'new_m_w_spatial', 'new_m_b_spatial', 'new_m_w_attn_branch', 'new_m_w_sgu_branch', 'new_m_w_out', 'new_m_ffn_norm', 'new_m_w_gate', 'new_m_w_up', 'new_m_w_down', 'new_v_mix_norm', 'new_v_w_in', 'new_v_q_norm', 'new_v_k_norm', 'new_v_sinks', 'new_v_sgu_ln_g', 'new_v_sgu_ln_b', 'new_v_w_spatial', 'new_v_b_spatial', 'new_v_w_attn_branch', 'new_v_w_sgu_branch', 'new_v_w_out', 'new_v_ffn_norm', 'new_v_w_gate', 'new_v_w_up', 'new_v_w_down']
TWIN_LEAF_KINDS = {'loss': 'loss', 'grad_x': 'grad_x', 'grad_mix_norm': 'grad_w', 'grad_w_in': 'grad_w', 'grad_q_norm': 'grad_w', 'grad_k_norm': 'grad_w', 'grad_sinks': 'grad_w', 'grad_sgu_ln_g': 'grad_w', 'grad_sgu_ln_b': 'grad_w', 'grad_w_spatial': 'grad_w', 'grad_b_spatial': 'grad_w', 'grad_w_attn_branch': 'grad_w', 'grad_w_sgu_branch': 'grad_w', 'grad_w_out': 'grad_w', 'grad_ffn_norm': 'grad_w', 'grad_w_gate': 'grad_w', 'grad_w_up': 'grad_w', 'grad_w_down': 'grad_w', 'delta_mix_norm': 'delta_w', 'delta_w_in': 'delta_w', 'delta_q_norm': 'delta_w', 'delta_k_norm': 'delta_w', 'delta_sinks': 'delta_w', 'delta_sgu_ln_g': 'delta_w', 'delta_sgu_ln_b': 'delta_w', 'delta_w_spatial': 'delta_w', 'delta_b_spatial': 'delta_w', 'delta_w_attn_branch': 'delta_w', 'delta_w_sgu_branch': 'delta_w', 'delta_w_out': 'delta_w', 'delta_ffn_norm': 'delta_w', 'delta_w_gate': 'delta_w', 'delta_w_up': 'delta_w', 'delta_w_down': 'delta_w', 'new_m_mix_norm': 'new_m', 'new_m_w_in': 'new_m', 'new_m_q_norm': 'new_m', 'new_m_k_norm': 'new_m', 'new_m_sinks': 'new_m', 'new_m_sgu_ln_g': 'new_m', 'new_m_sgu_ln_b': 'new_m', 'new_m_w_spatial': 'new_m', 'new_m_b_spatial': 'new_m', 'new_m_w_attn_branch': 'new_m', 'new_m_w_sgu_branch': 'new_m', 'new_m_w_out': 'new_m', 'new_m_ffn_norm': 'new_m', 'new_m_w_gate': 'new_m', 'new_m_w_up': 'new_m', 'new_m_w_down': 'new_m', 'new_v_mix_norm': 'new_v', 'new_v_w_in': 'new_v', 'new_v_q_norm': 'new_v', 'new_v_k_norm': 'new_v', 'new_v_sinks': 'new_v', 'new_v_sgu_ln_g': 'new_v', 'new_v_sgu_ln_b': 'new_v', 'new_v_w_spatial': 'new_v', 'new_v_b_spatial': 'new_v', 'new_v_w_attn_branch': 'new_v', 'new_v_w_sgu_branch': 'new_v', 'new_v_w_out': 'new_v', 'new_v_ffn_norm': 'new_v', 'new_v_w_gate': 'new_v', 'new_v_w_up': 'new_v', 'new_v_w_down': 'new_v'}


def _forward(args):
    return _fwd_reference(*[args[k] for k in FWD_PARAMS])


def _output_shape():
    out = _jax.eval_shape(lambda: _forward(_fwd_setup_inputs(0)))
    return out.shape, out.dtype

N_MICROBATCH = 1
ADAM_LR = 0.001
ADAM_B1 = 0.9
ADAM_B2 = 0.999
ADAM_EPS = 1e-08
ADAM_WD = 0.01
ADAM_STEP = 10
PER_EXAMPLE_BATCH_AXIS = {'x': 0, 'loss_target': 0}
SHARED_INPUTS = []
_WEIGHT_DTYPES = {'mix_norm': _jnp.float32, 'w_in': _jnp.float32, 'q_norm': _jnp.float32, 'k_norm': _jnp.float32, 'sinks': _jnp.float32, 'sgu_ln_g': _jnp.float32, 'sgu_ln_b': _jnp.float32, 'w_spatial': _jnp.float32, 'b_spatial': _jnp.float32, 'w_attn_branch': _jnp.float32, 'w_sgu_branch': _jnp.float32, 'w_out': _jnp.float32, 'ffn_norm': _jnp.float32, 'w_gate': _jnp.float32, 'w_up': _jnp.float32, 'w_down': _jnp.float32}
MOMENT_SCALE = {'mix_norm': 1.491189e+00, 'w_in': 8.830164e-02, 'q_norm': 9.797635e-01, 'k_norm': 9.782523e-01, 'sinks': 1.765900e-01, 'sgu_ln_g': 2.405087e-01, 'sgu_ln_b': 5.225274e-02, 'w_spatial': 8.093389e-02, 'b_spatial': 2.037758e+00, 'w_attn_branch': 9.319103e-02, 'w_sgu_branch': 4.267795e-01, 'w_out': 3.821591e-01, 'ffn_norm': 6.151637e+00, 'w_gate': 1.161707e-01, 'w_up': 7.140250e-02, 'w_down': 1.077564e-01}


def _to_microbatches(a, axis):
    t = _jnp.moveaxis(a, axis, 0)
    t = t.reshape((N_MICROBATCH, t.shape[0] // N_MICROBATCH) + t.shape[1:])
    return _jnp.moveaxis(t, 1, axis + 1)


def setup_inputs(seed: int = 0) -> dict:
    inp = _fwd_setup_inputs(seed)
    key = _jax.random.fold_in(_jax.random.key(seed), 7919)
    shape, _ = _output_shape()
    out = dict(inp)
    out["loss_target"] = _jax.random.normal(_jax.random.fold_in(key, 0), shape, _jnp.float32)
    for i, name in enumerate(TWIN_WEIGHTS):
        w = inp[name].astype(_jnp.float32)
        if MOMENT_SCALE is None:
            s = _jnp.sqrt(_jnp.mean(_jnp.square(w)) + 1e-30)
        else:
            s = MOMENT_SCALE[name]
        km, kv = _jax.random.split(_jax.random.fold_in(key, i + 1))
        out[name] = w
        out["m_" + name] = s * _jax.random.normal(km, w.shape, _jnp.float32)
        out["v_" + name] = (s * s) * _jax.random.uniform(kv, w.shape, _jnp.float32, 0.5, 1.5)
    if N_MICROBATCH > 1:
        for name, axis in PER_EXAMPLE_BATCH_AXIS.items():
            out[name] = _to_microbatches(out[name], axis)
    return {'x': out['x'], 'mix_norm': out['mix_norm'], 'w_in': out['w_in'], 'q_norm': out['q_norm'], 'k_norm': out['k_norm'], 'sinks': out['sinks'], 'sgu_ln_g': out['sgu_ln_g'], 'sgu_ln_b': out['sgu_ln_b'], 'w_spatial': out['w_spatial'], 'b_spatial': out['b_spatial'], 'w_attn_branch': out['w_attn_branch'], 'w_sgu_branch': out['w_sgu_branch'], 'w_out': out['w_out'], 'ffn_norm': out['ffn_norm'], 'w_gate': out['w_gate'], 'w_up': out['w_up'], 'w_down': out['w_down'], 'loss_target': out['loss_target'], 'm_mix_norm': out['m_mix_norm'], 'm_w_in': out['m_w_in'], 'm_q_norm': out['m_q_norm'], 'm_k_norm': out['m_k_norm'], 'm_sinks': out['m_sinks'], 'm_sgu_ln_g': out['m_sgu_ln_g'], 'm_sgu_ln_b': out['m_sgu_ln_b'], 'm_w_spatial': out['m_w_spatial'], 'm_b_spatial': out['m_b_spatial'], 'm_w_attn_branch': out['m_w_attn_branch'], 'm_w_sgu_branch': out['m_w_sgu_branch'], 'm_w_out': out['m_w_out'], 'm_ffn_norm': out['m_ffn_norm'], 'm_w_gate': out['m_w_gate'], 'm_w_up': out['m_w_up'], 'm_w_down': out['m_w_down'], 'v_mix_norm': out['v_mix_norm'], 'v_w_in': out['v_w_in'], 'v_q_norm': out['v_q_norm'], 'v_k_norm': out['v_k_norm'], 'v_sinks': out['v_sinks'], 'v_sgu_ln_g': out['v_sgu_ln_g'], 'v_sgu_ln_b': out['v_sgu_ln_b'], 'v_w_spatial': out['v_w_spatial'], 'v_b_spatial': out['v_b_spatial'], 'v_w_attn_branch': out['v_w_attn_branch'], 'v_w_sgu_branch': out['v_w_sgu_branch'], 'v_w_out': out['v_w_out'], 'v_ffn_norm': out['v_ffn_norm'], 'v_w_gate': out['v_w_gate'], 'v_w_up': out['v_w_up'], 'v_w_down': out['v_w_down']}


def _loss(weights, diff, rest, loss_target):
    with _jax.named_scope("forward"):
        args = {**rest, TWIN_DIFF_INPUT: diff, **{k: w.astype(_WEIGHT_DTYPES[k]) for k, w in weights.items()}}
        y = _forward(args)
    with _jax.named_scope("loss_head"):
        err = _jnp.square(y.astype(_jnp.float32) - loss_target)
        return 0.5 * _jnp.sum(_jnp.mean(err, axis=-1)) if err.ndim else 0.5 * err


def _adamw(w, g, m, v):
    m = ADAM_B1 * m + (1.0 - ADAM_B1) * g
    v = ADAM_B2 * v + (1.0 - ADAM_B2) * _jnp.square(g)
    m_hat = m / (1.0 - ADAM_B1 ** ADAM_STEP)
    v_hat = v / (1.0 - ADAM_B2 ** ADAM_STEP)
    delta = -ADAM_LR * (m_hat / (_jnp.sqrt(v_hat) + ADAM_EPS) + ADAM_WD * w)
    return delta, m, v


def reference(x, mix_norm, w_in, q_norm, k_norm, sinks, sgu_ln_g, sgu_ln_b, w_spatial, b_spatial, w_attn_branch, w_sgu_branch, w_out, ffn_norm, w_gate, w_up, w_down, loss_target, m_mix_norm, m_w_in, m_q_norm, m_k_norm, m_sinks, m_sgu_ln_g, m_sgu_ln_b, m_w_spatial, m_b_spatial, m_w_attn_branch, m_w_sgu_branch, m_w_out, m_ffn_norm, m_w_gate, m_w_up, m_w_down, v_mix_norm, v_w_in, v_q_norm, v_k_norm, v_sinks, v_sgu_ln_g, v_sgu_ln_b, v_w_spatial, v_b_spatial, v_w_attn_branch, v_w_sgu_branch, v_w_out, v_ffn_norm, v_w_gate, v_w_up, v_w_down):
    given = dict(x=x, mix_norm=mix_norm, w_in=w_in, q_norm=q_norm, k_norm=k_norm, sinks=sinks, sgu_ln_g=sgu_ln_g, sgu_ln_b=sgu_ln_b, w_spatial=w_spatial, b_spatial=b_spatial, w_attn_branch=w_attn_branch, w_sgu_branch=w_sgu_branch, w_out=w_out, ffn_norm=ffn_norm, w_gate=w_gate, w_up=w_up, w_down=w_down, loss_target=loss_target, m_mix_norm=m_mix_norm, m_w_in=m_w_in, m_q_norm=m_q_norm, m_k_norm=m_k_norm, m_sinks=m_sinks, m_sgu_ln_g=m_sgu_ln_g, m_sgu_ln_b=m_sgu_ln_b, m_w_spatial=m_w_spatial, m_b_spatial=m_b_spatial, m_w_attn_branch=m_w_attn_branch, m_w_sgu_branch=m_w_sgu_branch, m_w_out=m_w_out, m_ffn_norm=m_ffn_norm, m_w_gate=m_w_gate, m_w_up=m_w_up, m_w_down=m_w_down, v_mix_norm=v_mix_norm, v_w_in=v_w_in, v_q_norm=v_q_norm, v_k_norm=v_k_norm, v_sinks=v_sinks, v_sgu_ln_g=v_sgu_ln_g, v_sgu_ln_b=v_sgu_ln_b, v_w_spatial=v_w_spatial, v_b_spatial=v_b_spatial, v_w_attn_branch=v_w_attn_branch, v_w_sgu_branch=v_w_sgu_branch, v_w_out=v_w_out, v_ffn_norm=v_ffn_norm, v_w_gate=v_w_gate, v_w_up=v_w_up, v_w_down=v_w_down)
    weights = {n: given[n] for n in TWIN_WEIGHTS}
    shared = {n: given[n] for n in SHARED_INPUTS}
    per_example = {n: given[n] for n in ['x']}
    grad_fn = _jax.value_and_grad(_loss, argnums=(0, 1))

    def one_microbatch(ex, loss_target):
        ex = dict(ex)
        diff = ex.pop(TWIN_DIFF_INPUT)
        return grad_fn(weights, diff, {**shared, **ex}, loss_target)

    if N_MICROBATCH == 1:
        loss, (grad_w, grad_x) = one_microbatch(per_example, given["loss_target"])
    else:
        def body(carry, xs):
            loss_sum, grad_sum = carry
            l_k, (gw_k, gx_k) = one_microbatch(xs[0], xs[1])
            with _jax.named_scope("update"):
                return (loss_sum + l_k, _jax.tree.map(_jnp.add, grad_sum, gw_k)), gx_k

        init = (_jnp.zeros((), _jnp.float32), _jax.tree.map(_jnp.zeros_like, weights))
        (loss, grad_w), grad_x = _jax.lax.scan(body, init, (per_example, given["loss_target"]))
    with _jax.named_scope("update"):
        delta_w, new_m, new_v = {}, {}, {}
        for n in TWIN_WEIGHTS:
            delta_w[n], new_m[n], new_v[n] = _adamw(weights[n], grad_w[n], given["m_" + n], given["v_" + n])
    return (loss, grad_x, *[grad_w[n] for n in TWIN_WEIGHTS], *[delta_w[n] for n in TWIN_WEIGHTS],
            *[new_m[n] for n in TWIN_WEIGHTS], *[new_v[n] for n in TWIN_WEIGHTS])
```

```python
import functools
import math

import jax
import jax.numpy as jnp
from jax import lax
from jax.experimental import pallas as pl
from jax.experimental.pallas import tpu as pltpu

F32 = jnp.float32
BF16 = jnp.bfloat16
MESH = pl.DeviceIdType.MESH
ANY = pl.BlockSpec(memory_space=pl.ANY)

N_DEV = 8
HEAD_DIM = 64
Q_PER_KV = 4
BLOCK = 128
LANES = 128
ROPE_THETA = 10000.0
EPS = 1e-6
ADAM_LR = 0.001
ADAM_B1 = 0.9
ADAM_B2 = 0.999
ADAM_EPS = 1e-08
ADAM_WD = 0.01
ADAM_STEP = 10
NEG = -1e30
VMEM_LIMIT_BYTES = 56 * 1024 * 1024

NN = ((1,), (0,))
NT = ((1,), (1,))
TN = ((0,), (0,))


def _dot(a, b, dims):
    return lax.dot_general(a, b, (dims, ((), ())), preferred_element_type=F32)


def _params(*sem):
    return pltpu.CompilerParams(dimension_semantics=sem, vmem_limit_bytes=VMEM_LIMIT_BYTES)


def _divisor_tile(n, limit, unit):
    if n <= limit:
        return n
    best = unit
    for t in range(unit, limit + 1, unit):
        if n % t == 0:
            best = t
    assert n % best == 0, (n, limit, unit)
    return best


def _mm(a, b, mode, out_dtype, name, residual=None):
    parts = a.shape[0] if a.ndim == 3 else 1
    a2 = a.shape[-2:]
    if mode == "nn":
        (m, kp), (k2, n) = a2, b.shape
        k, mp = kp * parts, m
    elif mode == "nt":
        (m, kp), (n, k2) = a2, b.shape
        k, mp = kp * parts, m
    else:
        (k, mp), (k2, n) = a2, b.shape
        m, kp = mp * parts, k
    assert k == k2, (name, a.shape, b.shape)
    tm = _divisor_tile(mp, 512, 128)
    tn = _divisor_tile(n, 2048 if mode == "tn" else 512, 128)
    tk = _divisor_tile(kp, 2816, 128)
    nk = k // tk
    kpb, mpb = kp // tk, mp // tm
    dims = {"nn": NN, "nt": NT, "tn": TN}[mode]
    lead = (None,) if a.ndim == 3 else ()
    if mode == "tn":
        a_index = lambda i, j, kk: (i // mpb, kk, i % mpb) if lead else (kk, i)
        a_spec = pl.BlockSpec(lead + (tk, tm), a_index)
    else:
        a_index = lambda i, j, kk: (kk // kpb, i, kk % kpb) if lead else (i, kk)
        a_spec = pl.BlockSpec(lead + (tm, tk), a_index)
    if mode == "nt":
        b_spec = pl.BlockSpec((tn, tk), lambda i, j, kk: (j, kk))
    else:
        b_spec = pl.BlockSpec((tk, tn), lambda i, j, kk: (kk, j))
    o_spec = pl.BlockSpec((tm, tn), lambda i, j, kk: (i, j))
    has_res = residual is not None

    def body(*refs):
        if has_res:
            a_ref, b_ref, r_ref, o_ref, acc_ref = refs
        else:
            a_ref, b_ref, o_ref, acc_ref = refs
            r_ref = None
        kk = pl.program_id(2)
        p = _dot(a_ref[...], b_ref[...], dims)

        def finish(total):
            if has_res:
                total = total + r_ref[...]
            o_ref[...] = total.astype(o_ref.dtype)

        if nk == 1:
            finish(p)
        else:
            @pl.when(kk == 0)
            def _():
                acc_ref[...] = p

            @pl.when(jnp.logical_and(kk > 0, kk < nk - 1))
            def _():
                acc_ref[...] += p

            @pl.when(kk == nk - 1)
            def _():
                finish(acc_ref[...] + p)

    in_specs = [a_spec, b_spec] + ([o_spec] if has_res else [])
    args = (a, b) + ((residual,) if has_res else ())
    acc_shape = (tm, tn) if nk > 1 else (8, LANES)
    return pl.pallas_call(
        body,
        name=name,
        grid=(m // tm, n // tn, nk),
        in_specs=in_specs,
        out_specs=o_spec,
        out_shape=jax.ShapeDtypeStruct((m, n), out_dtype),
        scratch_shapes=[pltpu.VMEM(acc_shape, F32)],
        compiler_params=_params("parallel", "parallel", "arbitrary"),
    )(*args)


def _rmsnorm_fwd(x, g, name):
    t, d = x.shape
    tr = _divisor_tile(t, 256, 8)

    def body(x_ref, g_ref, h_ref):
        xv = x_ref[...]
        rstd = lax.rsqrt(jnp.mean(xv * xv, axis=-1, keepdims=True) + EPS)
        h_ref[...] = (xv * rstd * g_ref[...]).astype(h_ref.dtype)

    return pl.pallas_call(
        body,
        name=name,
        grid=(t // tr,),
        in_specs=[pl.BlockSpec((tr, d), lambda i: (i, 0)), pl.BlockSpec((1, d), lambda i: (0, 0))],
        out_specs=pl.BlockSpec((tr, d), lambda i: (i, 0)),
        out_shape=jax.ShapeDtypeStruct((t, d), BF16),
        compiler_params=_params("parallel"),
    )(x, g)


def _rmsnorm_bwd(x, g, dh, dres, name):
    t, d = x.shape
    tr = _divisor_tile(t, 256, 8)

    def body(x_ref, g_ref, dh_ref, dres_ref, dx_ref, dg_ref):
        i = pl.program_id(0)
        xv = x_ref[...]
        rstd = lax.rsqrt(jnp.mean(xv * xv, axis=-1, keepdims=True) + EPS)
        xh = xv * rstd
        dhv = dh_ref[...]
        dxh = dhv * g_ref[...]
        dx_ref[...] = dres_ref[...] + rstd * (dxh - xh * jnp.mean(dxh * xh, axis=-1, keepdims=True))
        part = jnp.broadcast_to(jnp.sum(dhv * xh, axis=0, keepdims=True), dg_ref.shape)

        @pl.when(i == 0)
        def _():
            dg_ref[...] = part

        @pl.when(i > 0)
        def _():
            dg_ref[...] += part

    row = pl.BlockSpec((tr, d), lambda i: (i, 0))
    return pl.pallas_call(
        body,
        name=name,
        grid=(t // tr,),
        in_specs=[row, pl.BlockSpec((1, d), lambda i: (0, 0)), row, row],
        out_specs=[row, pl.BlockSpec((8, d), lambda i: (0, 0))],
        out_shape=[jax.ShapeDtypeStruct((t, d), F32), jax.ShapeDtypeStruct((8, d), F32)],
        compiler_params=_params("arbitrary"),
    )(x, g, dh, dres)


def _lane(shape):
    return lax.broadcasted_iota(jnp.int32, shape, 1)


def _group_sum64(s):
    w = s.shape[1]
    lane = _lane(s.shape)
    for d in (1, 2, 4, 8, 16, 32):
        up = pltpu.roll(s, w - d, axis=1)
        dn = pltpu.roll(s, d, axis=1)
        s = s + jnp.where((lane & d) == 0, up, dn)
    return s


def _swap32(x):
    w = x.shape[1]
    return jnp.where((_lane(x.shape) & 32) == 0, pltpu.roll(x, w - 32, axis=1), pltpu.roll(x, 32, axis=1))


def _rope(x, c, s):
    return x * c + _swap32(x) * s


def _rope_t(dy, c, s):
    return dy * c + _swap32(dy * s)


def _head_norm(x):
    rstd = lax.rsqrt(_group_sum64(x * x) * (1.0 / HEAD_DIM) + EPS)
    return x * rstd, rstd


def _head_norm_bwd(dxh, xh, rstd):
    return rstd * (dxh - xh * (_group_sum64(dxh * xh) * (1.0 / HEAD_DIM)))


def _roll64(x):
    return pltpu.roll(x, 64, axis=1)


def _attn_specs(wq, wk):
    kb = wq // wk
    prev = lambda i: jnp.maximum(i - 1, 0)
    return dict(
        q=pl.BlockSpec((BLOCK, wq), lambda i: (i, 0)),
        kc=pl.BlockSpec((BLOCK, wk), lambda i: (i, kb)),
        kp=pl.BlockSpec((BLOCK, wk), lambda i: (prev(i), kb)),
        vc=pl.BlockSpec((BLOCK, wk), lambda i: (i, kb + 1)),
        vp=pl.BlockSpec((BLOCK, wk), lambda i: (prev(i), kb + 1)),
        tq=pl.BlockSpec((BLOCK, wq), lambda i: (i, 0)),
        tkp=pl.BlockSpec((BLOCK, wk), lambda i: (prev(i), 0)),
        gq=pl.BlockSpec((1, wq), lambda i: (0, 0)),
        gk=pl.BlockSpec((1, wk), lambda i: (0, 0)),
        sinks=pl.BlockSpec(memory_space=pltpu.SMEM),
    )


def _attn_prologue(i, q_ref, kc_ref, kp_ref, cq_ref, sq_ref, ckp_ref, skp_ref, gq_ref, gk_ref):
    wk = kc_ref.shape[1]
    cq, sq = cq_ref[...], sq_ref[...]
    ck, sk = cq[:, :wk], sq[:, :wk]
    qh, q_rstd = _head_norm(q_ref[...])
    kch, kc_rstd = _head_norm(kc_ref[...])
    kph, kp_rstd = _head_norm(kp_ref[...])
    qn = _rope(qh * gq_ref[...], cq, sq)
    knc = _rope(kch * gk_ref[...], ck, sk)
    knp = _rope(kph * gk_ref[...], ckp_ref[...], skp_ref[...])
    row = lax.broadcasted_iota(jnp.int32, (BLOCK, BLOCK), 0)
    col = lax.broadcasted_iota(jnp.int32, (BLOCK, BLOCK), 1)
    mask_c = col <= row
    mask_p = jnp.logical_and(col > row, i > 0)
    half = (col >= 64).astype(jnp.int32)
    return dict(cq=cq, sq=sq, ck=ck, sk=sk, qh=qh, q_rstd=q_rstd, kch=kch, kc_rstd=kc_rstd, kph=kph,
                kp_rstd=kp_rstd, qn=qn, knc=knc, knp=knp, mask_c=mask_c, mask_p=mask_p, half=half)


def _head_scores(st, t, e, sink, scale):
    g = (2 * t) // Q_PER_KV
    ks, kpar = g // 2, g % 2
    sl = slice(LANES * ks, LANES * ks + LANES)
    mine = st["half"] == e
    qm = jnp.where(mine, st["qn"][:, LANES * t:LANES * t + LANES], 0.0).astype(BF16)
    kc, kp = st["knc"][:, sl], st["knp"][:, sl]
    flip = e != kpar
    if flip:
        kc, kp = _roll64(kc), _roll64(kp)
    kc, kp = kc.astype(BF16), kp.astype(BF16)
    s_c = jnp.where(st["mask_c"], _dot(qm, kc, NT) * scale, NEG)
    s_p = jnp.where(st["mask_p"], _dot(qm, kp, NT) * scale, NEG)
    m = jnp.maximum(jnp.maximum(jnp.max(s_c, axis=1, keepdims=True), jnp.max(s_p, axis=1, keepdims=True)), sink)
    p_c, p_p = jnp.exp(s_c - m), jnp.exp(s_p - m)
    p_s = jnp.exp(sink - m)
    inv = 1.0 / (jnp.sum(p_c, axis=1, keepdims=True) + jnp.sum(p_p, axis=1, keepdims=True) + p_s)
    return dict(sl=sl, mine=mine, flip=flip, qm=qm, kc=kc, kp=kp, pr_c=p_c * inv, pr_p=p_p * inv, pr_s=p_s * inv)


def _attn_fwd(proj, tables, gq, gk, sinks, wq, wk, name):
    t = proj.shape[0]
    nb = t // BLOCK
    sp = _attn_specs(wq, wk)
    scale = HEAD_DIM ** -0.5
    cos_t, sin_t = tables

    def body(sinks_ref, q_ref, kc_ref, kp_ref, vc_ref, vp_ref, cq_ref, sq_ref, ckp_ref, skp_ref, gq_ref, gk_ref,
             o_ref):
        i = pl.program_id(0)
        st = _attn_prologue(i, q_ref, kc_ref, kp_ref, cq_ref, sq_ref, ckp_ref, skp_ref, gq_ref, gk_ref)
        vc_all, vp_all = vc_ref[...], vp_ref[...]
        for ts in range(wq // LANES):
            acc = jnp.zeros((BLOCK, LANES), F32)
            for e in (0, 1):
                hs = _head_scores(st, ts, e, sinks_ref[2 * ts + e], scale)
                vc, vp = vc_all[:, hs["sl"]], vp_all[:, hs["sl"]]
                if hs["flip"]:
                    vc, vp = _roll64(vc), _roll64(vp)
                vc = jnp.where(hs["mine"], vc, 0.0).astype(BF16)
                vp = jnp.where(hs["mine"], vp, 0.0).astype(BF16)
                acc = acc + _dot(hs["pr_c"].astype(BF16), vc, NN) + _dot(hs["pr_p"].astype(BF16), vp, NN)
            o_ref[:, LANES * ts:LANES * ts + LANES] = acc.astype(o_ref.dtype)

    return pl.pallas_call(
        body,
        name=name,
        grid=(nb,),
        in_specs=[sp["sinks"], sp["q"], sp["kc"], sp["kp"], sp["vc"], sp["vp"], sp["tq"], sp["tq"], sp["tkp"],
                  sp["tkp"], sp["gq"], sp["gk"]],
        out_specs=pl.BlockSpec((BLOCK, wq), lambda i: (i, 0)),
        out_shape=jax.ShapeDtypeStruct((t, wq), BF16),
        compiler_params=_params("parallel"),
    )(sinks, proj, proj, proj, proj, proj, cos_t, sin_t, cos_t, sin_t, gq, gk)


def _attn_bwd(proj, dout, tables, gq, gk, sinks, wq, wk, name):
    t = proj.shape[0]
    nb = t // BLOCK
    sp = _attn_specs(wq, wk)
    scale = HEAD_DIM ** -0.5
    cos_t, sin_t = tables

    def body(sinks_ref, q_ref, kc_ref, kp_ref, vc_ref, vp_ref, cq_ref, sq_ref, ckp_ref, skp_ref, gq_ref, gk_ref,
             do_ref, dq_ref, dk_ref, dv_ref, dgq_ref, dgk_ref, dsk_ref, dqn_ref, dknc_ref, dknp_ref, dvc_ref,
             dvp_ref):
        i = pl.program_id(0)
        st = _attn_prologue(i, q_ref, kc_ref, kp_ref, cq_ref, sq_ref, ckp_ref, skp_ref, gq_ref, gk_ref)
        vc_all, vp_all = vc_ref[...], vp_ref[...]
        dknc_ref[...] = jnp.zeros_like(dknc_ref)
        dknp_ref[...] = jnp.zeros_like(dknp_ref)
        dvc_ref[...] = jnp.zeros_like(dvc_ref)
        dvp_ref[...] = jnp.zeros_like(dvp_ref)
        lane8 = _lane((8, LANES))
        dsinks = jnp.zeros((8, LANES), F32)
        for ts in range(wq // LANES):
            dq_acc = jnp.zeros((BLOCK, LANES), F32)
            for e in (0, 1):
                hs = _head_scores(st, ts, e, sinks_ref[2 * ts + e], scale)
                sl, flip = hs["sl"], hs["flip"]
                vc, vp = vc_all[:, sl], vp_all[:, sl]
                if flip:
                    vc, vp = _roll64(vc), _roll64(vp)
                dom = jnp.where(hs["mine"], do_ref[:, LANES * ts:LANES * ts + LANES], 0.0).astype(BF16)
                dp_c = _dot(dom, vc.astype(BF16), NT)
                dp_p = _dot(dom, vp.astype(BF16), NT)
                pr_c, pr_p = hs["pr_c"], hs["pr_p"]
                rs = jnp.sum(pr_c * dp_c, axis=1, keepdims=True) + jnp.sum(pr_p * dp_p, axis=1, keepdims=True)
                ds_c = (pr_c * (dp_c - rs) * scale)
                ds_p = (pr_p * (dp_p - rs) * scale)
                dsink = jnp.sum(-hs["pr_s"] * rs)
                dsinks = dsinks + jnp.where(lane8 == 2 * ts + e, dsink, 0.0)
                dq_acc = dq_acc + jnp.where(
                    hs["mine"], _dot(ds_c.astype(BF16), hs["kc"], NN) + _dot(ds_p.astype(BF16), hs["kp"], NN), 0.0)
                dv_c = _dot(pr_c.T.astype(BF16), dom, NN)
                dv_p = _dot(pr_p.T.astype(BF16), dom, NN)
                dk_c = _dot(ds_c.T.astype(BF16), hs["qm"], NN)
                dk_p = _dot(ds_p.T.astype(BF16), hs["qm"], NN)
                if flip:
                    dv_c, dv_p, dk_c, dk_p = _roll64(dv_c), _roll64(dv_p), _roll64(dk_c), _roll64(dk_p)
                dvc_ref[:, sl] += dv_c
                dvp_ref[:, sl] += dv_p
                dknc_ref[:, sl] += dk_c
                dknp_ref[:, sl] += dk_p
            dqn_ref[:, LANES * ts:LANES * ts + LANES] = dq_acc

        gqv, gkv = gq_ref[...], gk_ref[...]
        dqg = _rope_t(dqn_ref[...], st["cq"], st["sq"])
        dq_ref[...] = _head_norm_bwd(dqg * gqv, st["qh"], st["q_rstd"]).astype(dq_ref.dtype)
        dkcg = _rope_t(dknc_ref[...], st["ck"], st["sk"])
        dkpg = _rope_t(dknp_ref[...], ckp_ref[...], skp_ref[...])
        dk_cur = _head_norm_bwd(dkcg * gkv, st["kch"], st["kc_rstd"])
        dk_prev = _head_norm_bwd(dkpg * gkv, st["kph"], st["kp_rstd"])
        dgq_part = jnp.broadcast_to(jnp.sum(dqg * st["qh"], axis=0, keepdims=True), dgq_ref.shape)
        dgk_part = jnp.broadcast_to(
            jnp.sum(dkcg * st["kch"] + dkpg * st["kph"], axis=0, keepdims=True), dgk_ref.shape)
        cur = pl.ds(pl.multiple_of(i * BLOCK, BLOCK), BLOCK)
        dk_ref[cur, :] = dk_cur
        dv_ref[cur, :] = dvc_ref[...]

        @pl.when(i == 0)
        def _():
            dgq_ref[...] = dgq_part
            dgk_ref[...] = dgk_part
            dsk_ref[...] = dsinks

        @pl.when(i > 0)
        def _():
            before = pl.ds(pl.multiple_of((i - 1) * BLOCK, BLOCK), BLOCK)
            dk_ref[before, :] += dk_prev
            dv_ref[before, :] += dvp_ref[...]
            dgq_ref[...] += dgq_part
            dgk_ref[...] += dgk_part
            dsk_ref[...] += dsinks

    whole = lambda shape: pl.BlockSpec(shape, lambda i: (0, 0))
    return pl.pallas_call(
        body,
        name=name,
        grid=(nb,),
        in_specs=[sp["sinks"], sp["q"], sp["kc"], sp["kp"], sp["vc"], sp["vp"], sp["tq"], sp["tq"], sp["tkp"],
                  sp["tkp"], sp["gq"], sp["gk"], pl.BlockSpec((BLOCK, wq), lambda i: (i, 0))],
        out_specs=[pl.BlockSpec((BLOCK, wq), lambda i: (i, 0)), whole((t, wk)), whole((t, wk)), whole((8, wq)),
                   whole((8, wk)), whole((8, LANES))],
        out_shape=[jax.ShapeDtypeStruct((t, wq), BF16), jax.ShapeDtypeStruct((t, wk), F32),
                   jax.ShapeDtypeStruct((t, wk), F32), jax.ShapeDtypeStruct((8, wq), F32),
                   jax.ShapeDtypeStruct((8, wk), F32), jax.ShapeDtypeStruct((8, LANES), F32)],
        scratch_shapes=[pltpu.VMEM((BLOCK, wq), F32), pltpu.VMEM((BLOCK, wk), F32), pltpu.VMEM((BLOCK, wk), F32),
                        pltpu.VMEM((BLOCK, wk), F32), pltpu.VMEM((BLOCK, wk), F32)],
        compiler_params=_params("arbitrary"),
    )(sinks, proj, proj, proj, proj, proj, cos_t, sin_t, cos_t, sin_t, gq, gk, dout)


_GELU_K = math.sqrt(2.0 / math.pi)
_GELU_A = 0.044715


def _gelu(x):
    return 0.5 * x * (1.0 + jnp.tanh(_GELU_K * (x + _GELU_A * x * x * x)))


def _gelu_grad(x):
    th = jnp.tanh(_GELU_K * (x + _GELU_A * x * x * x))
    return 0.5 * (1.0 + th) + 0.5 * x * (1.0 - th * th) * (_GELU_K * (1.0 + 3.0 * _GELU_A * x * x))


def _group_ln(v):
    mu = jnp.mean(v, axis=1, keepdims=True)
    cen = v - mu
    rstd = lax.rsqrt(jnp.mean(cen * cen, axis=1, keepdims=True) + EPS)
    return cen * rstd, rstd


def _sgu_geometry(off_u, ws):
    cw = math.gcd(off_u, ws)
    return cw, ws // cw, off_u // cw, (off_u + ws) // cw


def _sgu_fwd(proj, ln_g, ln_b, w_s, bt, off_u, ws, name):
    t = proj.shape[0]
    nb = t // BLOCK
    cw, nc, ub, vb = _sgu_geometry(off_u, ws)
    gpc = cw // LANES
    ng = ws // LANES

    def body(u_ref, v_ref, g_ref, b_ref, w_ref, bt_ref, o_ref):
        jc = pl.program_id(0)
        row = lax.broadcasted_iota(jnp.int32, (BLOCK, BLOCK), 0)
        col = lax.broadcasted_iota(jnp.int32, (BLOCK, BLOCK), 1)
        lane_g = _lane((BLOCK, ng))
        for gi in range(gpc):
            sl = slice(LANES * gi, LANES * gi + LANES)
            xh, _ = _group_ln(_gelu(v_ref[:, sl]))
            vn = xh * g_ref[:, sl] + b_ref[:, sl]
            w = jnp.where(row >= col, w_ref[gi], 0.0).astype(BF16)
            bias = jnp.sum(jnp.where(lane_g == jc * gpc + gi, bt_ref[...], 0.0), axis=1, keepdims=True)
            s = _dot(w, vn.astype(BF16), NN) + bias
            o_ref[:, sl] = (_gelu(u_ref[:, sl]) * s).astype(o_ref.dtype)

    return pl.pallas_call(
        body,
        name=name,
        grid=(nc, nb),
        in_specs=[pl.BlockSpec((BLOCK, cw), lambda jc, i: (i, ub + jc)),
                  pl.BlockSpec((BLOCK, cw), lambda jc, i: (i, vb + jc)),
                  pl.BlockSpec((1, cw), lambda jc, i: (0, jc)),
                  pl.BlockSpec((1, cw), lambda jc, i: (0, jc)),
                  pl.BlockSpec((gpc, BLOCK, BLOCK), lambda jc, i: (jc, 0, 0)),
                  pl.BlockSpec((BLOCK, ng), lambda jc, i: (0, 0))],
        out_specs=pl.BlockSpec((BLOCK, cw), lambda jc, i: (i, jc)),
        out_shape=jax.ShapeDtypeStruct((t, ws), BF16),
        compiler_params=_params("parallel", "parallel"),
    )(proj, proj, ln_g, ln_b, w_s, bt)


def _sgu_bwd(proj, dout, ln_g, ln_b, w_s, bt, off_u, ws, name):
    t = proj.shape[0]
    nb = t // BLOCK
    cw, nc, ub, vb = _sgu_geometry(off_u, ws)
    gpc = cw // LANES
    ng = ws // LANES

    def body(u_ref, v_ref, g_ref, b_ref, w_ref, bt_ref, do_ref, du_ref, dv_ref, dg_ref, db_ref, dw_ref, dbs_ref,
             bacc_ref):
        jc = pl.program_id(0)
        i = pl.program_id(1)
        row = lax.broadcasted_iota(jnp.int32, (BLOCK, BLOCK), 0)
        col = lax.broadcasted_iota(jnp.int32, (BLOCK, BLOCK), 1)
        lane_g = _lane((BLOCK, ng))
        tri = row >= col

        @pl.when(i == 0)
        def _():
            dg_ref[...] = jnp.zeros_like(dg_ref)
            db_ref[...] = jnp.zeros_like(db_ref)
            dw_ref[...] = jnp.zeros_like(dw_ref)
            bacc_ref[...] = jnp.zeros_like(bacc_ref)

        for gi in range(gpc):
            sl = slice(LANES * gi, LANES * gi + LANES)
            u_raw, v_raw = u_ref[:, sl], v_ref[:, sl]
            xh, rstd = _group_ln(_gelu(v_raw))
            gam = g_ref[:, sl]
            vn = (xh * gam + b_ref[:, sl]).astype(BF16)
            w = jnp.where(tri, w_ref[gi], 0.0)
            bias = jnp.sum(jnp.where(lane_g == jc * gpc + gi, bt_ref[...], 0.0), axis=1, keepdims=True)
            s = _dot(w.astype(BF16), vn, NN) + bias
            dov = do_ref[:, sl]
            du_ref[:, sl] = (dov * s * _gelu_grad(u_raw)).astype(du_ref.dtype)
            ds = dov * _gelu(u_raw)
            ds16 = ds.astype(BF16)
            dw_ref[gi] += jnp.where(tri, _dot(ds16, vn, NT), 0.0)
            bacc_ref[gi] += ds
            dvn = _dot(w.T.astype(BF16), ds16, NN)
            dg_ref[:, sl] += jnp.broadcast_to(jnp.sum(dvn * xh, axis=0, keepdims=True), (8, LANES))
            db_ref[:, sl] += jnp.broadcast_to(jnp.sum(dvn, axis=0, keepdims=True), (8, LANES))
            dxh = dvn * gam
            dvg = rstd * (dxh - jnp.mean(dxh, axis=1, keepdims=True)
                          - xh * jnp.mean(dxh * xh, axis=1, keepdims=True))
            dv_ref[:, sl] = (dvg * _gelu_grad(v_raw)).astype(dv_ref.dtype)

        @pl.when(i == nb - 1)
        def _():
            for gi in range(gpc):
                dbs_ref[gi] = jnp.broadcast_to(jnp.sum(bacc_ref[gi].T, axis=0, keepdims=True), (8, LANES))

    blk = lambda base: pl.BlockSpec((BLOCK, cw), lambda jc, i: (i, base + jc))
    vec = pl.BlockSpec((1, cw), lambda jc, i: (0, jc))
    acc = pl.BlockSpec((8, cw), lambda jc, i: (0, jc))
    wsp = pl.BlockSpec((gpc, BLOCK, BLOCK), lambda jc, i: (jc, 0, 0))
    return pl.pallas_call(
        body,
        name=name,
        grid=(nc, nb),
        in_specs=[blk(ub), blk(vb), vec, vec, wsp, pl.BlockSpec((BLOCK, ng), lambda jc, i: (0, 0)), blk(0)],
        out_specs=[blk(0), blk(0), acc, acc, wsp, pl.BlockSpec((gpc, 8, LANES), lambda jc, i: (jc, 0, 0))],
        out_shape=[jax.ShapeDtypeStruct((t, ws), BF16), jax.ShapeDtypeStruct((t, ws), BF16),
                   jax.ShapeDtypeStruct((8, ws), F32), jax.ShapeDtypeStruct((8, ws), F32),
                   jax.ShapeDtypeStruct((ng, BLOCK, BLOCK), F32), jax.ShapeDtypeStruct((ng, 8, LANES), F32)],
        scratch_shapes=[pltpu.VMEM((gpc, BLOCK, BLOCK), F32)],
        compiler_params=_params("arbitrary", "arbitrary"),
    )(proj, proj, ln_g, ln_b, w_s, bt, dout)


def _sigmoid(x):
    return 1.0 / (1.0 + jnp.exp(-x))


def _merge_geometry(off_g, d):
    cw = math.gcd(off_g, d)
    return cw, d // cw, off_g // cw, (off_g + d) // cw


def _merge_fwd(a, b, proj, off_g, name):
    t, d = a.shape
    cw, nc, ab, bb = _merge_geometry(off_g, d)
    tr = _divisor_tile(t, 512, 8)

    def body(a_ref, b_ref, la_ref, lb_ref, o_ref):
        o_ref[...] = (_sigmoid(la_ref[...]) * a_ref[...] + _sigmoid(lb_ref[...]) * b_ref[...]).astype(o_ref.dtype)

    blk = lambda base: pl.BlockSpec((tr, cw), lambda i, j: (i, base + j))
    return pl.pallas_call(
        body,
        name=name,
        grid=(t // tr, nc),
        in_specs=[blk(0), blk(0), blk(ab), blk(bb)],
        out_specs=blk(0),
        out_shape=jax.ShapeDtypeStruct((t, d), BF16),
        compiler_params=_params("parallel", "parallel"),
    )(a, b, proj, proj)


def _merge_bwd(a, b, proj, dm, off_g, name):
    t, d = a.shape
    cw, nc, ab, bb = _merge_geometry(off_g, d)
    tr = _divisor_tile(t, 512, 8)

    def body(a_ref, b_ref, la_ref, lb_ref, dm_ref, da_ref, db_ref, dla_ref, dlb_ref):
        dmv = dm_ref[...]
        ga, gb = _sigmoid(la_ref[...]), _sigmoid(lb_ref[...])
        da_ref[...] = (dmv * ga).astype(da_ref.dtype)
        db_ref[...] = (dmv * gb).astype(db_ref.dtype)
        dla_ref[...] = (dmv * a_ref[...] * ga * (1.0 - ga)).astype(dla_ref.dtype)
        dlb_ref[...] = (dmv * b_ref[...] * gb * (1.0 - gb)).astype(dlb_ref.dtype)

    blk = lambda base: pl.BlockSpec((tr, cw), lambda i, j: (i, base + j))
    return pl.pallas_call(
        body,
        name=name,
        grid=(t // tr, nc),
        in_specs=[blk(0), blk(0), blk(ab), blk(bb), blk(0)],
        out_specs=[blk(0)] * 4,
        out_shape=[jax.ShapeDtypeStruct((t, d), BF16)] * 4,
        compiler_params=_params("parallel", "parallel"),
    )(a, b, proj, proj, dm)


def _swiglu_fwd(gu, name):
    t, f2 = gu.shape
    f = f2 // 2
    cw = _divisor_tile(f, 1536, 128)
    nc = f // cw
    tr = _divisor_tile(t, 256, 8)

    def body(g_ref, u_ref, o_ref):
        gv = g_ref[...]
        o_ref[...] = (gv * _sigmoid(gv) * u_ref[...]).astype(o_ref.dtype)

    blk = lambda base: pl.BlockSpec((tr, cw), lambda i, j: (i, base + j))
    return pl.pallas_call(
        body,
        name=name,
        grid=(t // tr, nc),
        in_specs=[blk(0), blk(nc)],
        out_specs=blk(0),
        out_shape=jax.ShapeDtypeStruct((t, f), BF16),
        compiler_params=_params("parallel", "parallel"),
    )(gu, gu)


def _swiglu_bwd(gu, dact, name):
    t, f2 = gu.shape
    f = f2 // 2
    cw = _divisor_tile(f, 1536, 128)
    nc = f // cw
    tr = _divisor_tile(t, 256, 8)

    def body(g_ref, u_ref, da_ref, o_ref):
        gv, dav = g_ref[...], da_ref[...]
        sg = _sigmoid(gv)
        o_ref[0] = (dav * u_ref[...] * (sg + gv * sg * (1.0 - sg))).astype(o_ref.dtype)
        o_ref[1] = (dav * gv * sg).astype(o_ref.dtype)

    blk = lambda base: pl.BlockSpec((tr, cw), lambda i, j: (i, base + j))
    return pl.pallas_call(
        body,
        name=name,
        grid=(t // tr, nc),
        in_specs=[blk(0), blk(nc), blk(0)],
        out_specs=pl.BlockSpec((2, tr, cw), lambda i, j: (0, i, j)),
        out_shape=jax.ShapeDtypeStruct((2, t, f), BF16),
        compiler_params=_params("parallel", "parallel"),
    )(gu, gu, dact)


def _loss_and_grad(y, target, name):
    t, d = y.shape
    tr = _divisor_tile(t, 256, 8)

    def body(y_ref, t_ref, l_ref, dy_ref):
        i = pl.program_id(0)
        err = y_ref[...] - t_ref[...]
        dy_ref[...] = err * (1.0 / d)
        part = jnp.broadcast_to(0.5 * jnp.sum(err * err) * (1.0 / d), l_ref.shape)

        @pl.when(i == 0)
        def _():
            l_ref[...] = part

        @pl.when(i > 0)
        def _():
            l_ref[...] += part

    row = pl.BlockSpec((tr, d), lambda i: (i, 0))
    return pl.pallas_call(
        body,
        name=name,
        grid=(t // tr,),
        in_specs=[row, row],
        out_specs=[pl.BlockSpec((8, LANES), lambda i: (0, 0)), row],
        out_shape=[jax.ShapeDtypeStruct((8, LANES), F32), jax.ShapeDtypeStruct((t, d), F32)],
        compiler_params=_params("arbitrary"),
    )(y, target)


def _adam_math(w, g, m, v):
    m = ADAM_B1 * m + (1.0 - ADAM_B1) * g
    v = ADAM_B2 * v + (1.0 - ADAM_B2) * (g * g)
    m_hat = m / (1.0 - ADAM_B1 ** ADAM_STEP)
    v_hat = v / (1.0 - ADAM_B2 ** ADAM_STEP)
    delta = -ADAM_LR * (m_hat / (jnp.sqrt(v_hat) + ADAM_EPS) + ADAM_WD * w)
    return delta, m, v


def _row_tile(r, c):
    return _divisor_tile(r, max(8, (256 * 1024) // c // 8 * 8), 8)


def _adam(w, g, m, v, name):
    r, c = w.shape
    tr = _row_tile(r, c)

    def body(w_ref, g_ref, m_ref, v_ref, d_ref, nm_ref, nv_ref):
        d_ref[...], nm_ref[...], nv_ref[...] = _adam_math(w_ref[...], g_ref[...], m_ref[...], v_ref[...])

    row = pl.BlockSpec((tr, c), lambda i: (i, 0))
    return pl.pallas_call(
        body,
        name=name,
        grid=(r // tr,),
        in_specs=[row] * 4,
        out_specs=[row] * 3,
        out_shape=[jax.ShapeDtypeStruct((r, c), F32)] * 3,
        compiler_params=_params("parallel"),
    )(w, g, m, v)


def _sum_rows(parts, out_dtype, name):
    r, c = parts[0].shape
    tr = _row_tile(r, c)
    n = len(parts)

    def body(*refs):
        total = refs[0][...].astype(F32)
        for ref in refs[1:n]:
            total = total + ref[...].astype(F32)
        refs[n][...] = total.astype(out_dtype)

    row = pl.BlockSpec((tr, c), lambda i: (i, 0))
    return pl.pallas_call(
        body,
        name=name,
        grid=(r // tr,),
        in_specs=[row] * n,
        out_specs=row,
        out_shape=jax.ShapeDtypeStruct((r, c), out_dtype),
        compiler_params=_params("parallel"),
    )(*parts)


def _small_reduce_adam(gathered, w, m, v, name):
    _, r, c = gathered.shape
    tr = _row_tile(r, c)

    def body(p_ref, w_ref, m_ref, v_ref, g_ref, d_ref, nm_ref, nv_ref):
        g = p_ref[0]
        for j in range(1, N_DEV):
            g = g + p_ref[j]
        g_ref[...] = g
        d_ref[...], nm_ref[...], nv_ref[...] = _adam_math(w_ref[...], g, m_ref[...], v_ref[...])

    row = pl.BlockSpec((tr, c), lambda i: (i, 0))
    return pl.pallas_call(
        body,
        name=name,
        grid=(r // tr,),
        in_specs=[pl.BlockSpec((N_DEV, tr, c), lambda i: (0, i, 0)), row, row, row],
        out_specs=[row] * 4,
        out_shape=[jax.ShapeDtypeStruct((r, c), F32)] * 4,
        compiler_params=_params("parallel"),
    )(gathered, w, m, v)


def _place():
    return lax.axis_index("x"), lax.axis_index("y"), lax.axis_index("c")


def _all_gather(shards, name):
    n = len(shards)

    def body(*refs):
        ins, outs = refs[:n], refs[n:2 * n]
        send_sems, recv_sems, local_sems = refs[2 * n:]
        x, y, c = _place()
        me, sibling = (x, y, c), (x, y, 1 - c)
        chips = [(1 - x, y), (x, 1 - y), (1 - x, 1 - y)]

        def block(a, px, py, pc):
            return outs[a].at[:, pl.ds(4 * px + 2 * py + pc, 1)]

        def copy(a, k, blk, to, src=None):
            return pltpu.make_async_remote_copy(
                src_ref=block(a, *blk) if src is None else src, dst_ref=block(a, *blk),
                send_sem=send_sems.at[a, k], recv_sem=recv_sems.at[a, k], device_id=to, device_id_type=MESH)

        mine = [pltpu.make_async_copy(ins[a], block(a, *me), local_sems.at[a]) for a in range(n)]
        for cp in mine:
            cp.start()
        first = []
        for a in range(n):
            first.append(copy(a, 0, me, sibling, src=ins[a]))
            first += [copy(a, 1 + j, me, (*chip, c), src=ins[a]) for j, chip in enumerate(chips)]
        for cp in first:
            cp.start()
        passed = []
        for j, chip in enumerate(chips):
            for a in range(n):
                copy(a, 1 + j, (*chip, c), me).wait_recv()
                fwd = copy(a, 4 + j, (*chip, c), sibling)
                fwd.start()
                passed.append(fwd)
        for a in range(n):
            copy(a, 0, sibling, me).wait_recv()
            for j, chip in enumerate(chips):
                copy(a, 4 + j, (*chip, 1 - c), me).wait_recv()
        for cp in first + passed:
            cp.wait_send()
        for cp in mine:
            cp.wait()

    return pl.pallas_call(
        body,
        name=name,
        in_specs=[ANY] * n,
        out_specs=[ANY] * n,
        out_shape=[jax.ShapeDtypeStruct((s.shape[0], N_DEV) + s.shape[2:], s.dtype) for s in shards],
        scratch_shapes=[pltpu.SemaphoreType.DMA((n, 7)), pltpu.SemaphoreType.DMA((n, 7)),
                        pltpu.SemaphoreType.DMA((n,))],
    )(*shards)


def _swap_with_sibling(grads, name):
    n = len(grads)

    def body(*refs):
        ins, keep, land = refs[:n], refs[n:2 * n], refs[2 * n:3 * n]
        send_sems, recv_sems, local_sems = refs[3 * n:]
        x, y, c = _place()
        copies = []
        for a in range(n):
            local = pltpu.make_async_copy(ins[a].at[:, :, pl.ds(c, 1)], keep[a], local_sems.at[a])
            remote = pltpu.make_async_remote_copy(
                src_ref=ins[a].at[:, :, pl.ds(1 - c, 1)], dst_ref=land[a], send_sem=send_sems.at[a],
                recv_sem=recv_sems.at[a], device_id=(x, y, 1 - c), device_id_type=MESH)
            local.start()
            remote.start()
            copies.append((local, remote))
        for local, remote in copies:
            local.wait()
            remote.wait()

    half = [jax.ShapeDtypeStruct(g.shape[:2] + (1,) + g.shape[3:], g.dtype) for g in grads]
    outs = pl.pallas_call(
        body,
        name=name,
        in_specs=[ANY] * n,
        out_specs=[ANY] * (2 * n),
        out_shape=half + half,
        scratch_shapes=[pltpu.SemaphoreType.DMA((n,)), pltpu.SemaphoreType.DMA((n,)), pltpu.SemaphoreType.DMA((n,))],
    )(*grads)
    return outs[:n], outs[n:]


def _swap_with_chips(sums, name):
    n = len(sums)

    def body(*refs):
        ins, keep, land = refs[:n], refs[n:2 * n], refs[2 * n:3 * n]
        send_sems, recv_sems, local_sems = refs[3 * n:]
        x, y, c = _place()
        chips = [(1 - x, y), (x, 1 - y), (1 - x, 1 - y)]
        copies = []
        for a in range(n):
            local = pltpu.make_async_copy(ins[a].at[:, pl.ds(2 * x + y, 1)], keep[a], local_sems.at[a])
            local.start()
            copies.append(local)
            for k, (px, py) in enumerate(chips):
                remote = pltpu.make_async_remote_copy(
                    src_ref=ins[a].at[:, pl.ds(2 * px + py, 1)], dst_ref=land[a].at[k], send_sem=send_sems.at[a, k],
                    recv_sem=recv_sems.at[a, k], device_id=(px, py, c), device_id_type=MESH)
                remote.start()
                copies.append(remote)
        for cp in copies:
            cp.wait()

    one = [jax.ShapeDtypeStruct((s.shape[0], 1) + s.shape[2:], s.dtype) for s in sums]
    three = [jax.ShapeDtypeStruct((3, s.shape[0], 1) + s.shape[2:], s.dtype) for s in sums]
    outs = pl.pallas_call(
        body,
        name=name,
        in_specs=[ANY] * n,
        out_specs=[ANY] * (2 * n),
        out_shape=one + three,
        scratch_shapes=[pltpu.SemaphoreType.DMA((n, 3)), pltpu.SemaphoreType.DMA((n, 3)),
                        pltpu.SemaphoreType.DMA((n,))],
    )(*sums)
    return outs[:n], outs[n:]


def _reduce_scatter(grads, tag):
    shaped = []
    for g, p in grads:
        rows, c = g.shape
        shaped.append(g.reshape(p, 4, 2, rows // (8 * p), c))
    mine, theirs = _swap_with_sibling(shaped, f"rs_sibling_{tag}")
    sums = []
    for a, (m, o) in enumerate(zip(mine, theirs)):
        p, _, _, r, c = m.shape
        s = _sum_rows([m.reshape(p * 4 * r, c), o.reshape(p * 4 * r, c)], BF16, f"rs_add_sibling_{tag}_{a}")
        sums.append(s.reshape(p, 4, r, c))
    mine, theirs = _swap_with_chips(sums, f"rs_chips_{tag}")
    out = []
    for a, (m, o) in enumerate(zip(mine, theirs)):
        p, _, r, c = m.shape
        parts = [m.reshape(p * r, c)] + [o[k].reshape(p * r, c) for k in range(3)]
        out.append(_sum_rows(parts, F32, f"rs_add_chips_{tag}_{a}"))
    return out


_SMALL = ("mix_norm", "q_norm", "k_norm", "sinks", "sgu_ln_g", "sgu_ln_b", "w_spatial", "b_spatial", "ffn_norm")


def _pack_rows(a):
    flat = a.reshape(-1)
    pad = (-flat.shape[0]) % LANES
    if pad:
        flat = jnp.pad(flat, (0, pad))
    return flat.reshape(-1, LANES)


def _pack(values):
    rows = jnp.concatenate([_pack_rows(values[k]) for k in _SMALL], axis=0)
    pad = (-rows.shape[0]) % 8
    if pad:
        rows = jnp.pad(rows, ((0, pad), (0, 0)))
    return rows


def _unpack(rows, like):
    out, at = {}, 0
    for k in _SMALL:
        size = like[k].size
        nrows = -(-size // LANES)
        out[k] = rows[at:at + nrows].reshape(-1)[:size].reshape(like[k].shape)
        at += nrows
    return out


def _rope_tables(t, wq):
    pos = jnp.arange(t, dtype=F32)
    inv_freq = jnp.power(ROPE_THETA, -jnp.arange(0, HEAD_DIM, 2, dtype=F32) / HEAD_DIM)
    ang = pos[:, None] * inv_freq[None, :]
    cos, sin = jnp.cos(ang), jnp.sin(ang)
    reps = wq // HEAD_DIM
    return (jnp.tile(jnp.concatenate([cos, cos], axis=1), (1, reps)),
            jnp.tile(jnp.concatenate([-sin, sin], axis=1), (1, reps)))


def kernel(x, mix_norm, w_in, q_norm, k_norm, sinks, sgu_ln_g, sgu_ln_b, w_spatial, b_spatial, w_attn_branch, w_sgu_branch, w_out, ffn_norm, w_gate, w_up, w_down, loss_target, m_mix_norm, m_w_in, m_q_norm, m_k_norm, m_sinks, m_sgu_ln_g, m_sgu_ln_b, m_w_spatial, m_b_spatial, m_w_attn_branch, m_w_sgu_branch, m_w_out, m_ffn_norm, m_w_gate, m_w_up, m_w_down, v_mix_norm, v_w_in, v_q_norm, v_k_norm, v_sinks, v_sgu_ln_g, v_sgu_ln_b, v_w_spatial, v_b_spatial, v_w_attn_branch, v_w_sgu_branch, v_w_out, v_ffn_norm, v_w_gate, v_w_up, v_w_down):
    names = ("mix_norm", "w_in", "q_norm", "k_norm", "sinks", "sgu_ln_g", "sgu_ln_b", "w_spatial", "b_spatial",
             "w_attn_branch", "w_sgu_branch", "w_out", "ffn_norm", "w_gate", "w_up", "w_down")
    weights = dict(zip(names, (mix_norm, w_in, q_norm, k_norm, sinks, sgu_ln_g, sgu_ln_b, w_spatial, b_spatial,
                               w_attn_branch, w_sgu_branch, w_out, ffn_norm, w_gate, w_up, w_down)))
    mom1 = dict(zip(names, (m_mix_norm, m_w_in, m_q_norm, m_k_norm, m_sinks, m_sgu_ln_g, m_sgu_ln_b, m_w_spatial,
                            m_b_spatial, m_w_attn_branch, m_w_sgu_branch, m_w_out, m_ffn_norm, m_w_gate, m_w_up,
                            m_w_down)))
    mom2 = dict(zip(names, (v_mix_norm, v_w_in, v_q_norm, v_k_norm, v_sinks, v_sgu_ln_g, v_sgu_ln_b, v_w_spatial,
                            v_b_spatial, v_w_attn_branch, v_w_sgu_branch, v_w_out, v_ffn_norm, v_w_gate, v_w_up,
                            v_w_down)))
    depth = w_in.shape[0]
    _, t, d = x.shape
    n_q_heads = sinks.shape[1]
    wq = n_q_heads * HEAD_DIM
    wk = wq // Q_PER_KV
    ws = sgu_ln_g.shape[1]
    ng = ws // LANES
    off_u = wq + 2 * wk
    off_g = off_u + 2 * ws
    tables = _rope_tables(t, wq)

    gathered = []
    for l in range(depth):
        shards = [
            w_in[l].T.astype(BF16)[None, None],
            w_attn_branch[l].T.astype(BF16)[None, None],
            w_sgu_branch[l].T.astype(BF16)[None, None],
            w_out[l].astype(BF16)[None, None],
            jnp.stack([w_gate[l].T, w_up[l].T]).astype(BF16)[:, None],
            w_down[l].astype(BF16)[None, None],
        ]
        full = _all_gather(shards, f"all_gather_weights_{l}")
        gathered.append([f.reshape(f.shape[0] * f.shape[1] * f.shape[2], f.shape[3]) for f in full])

    saved = []
    xl = x[0]
    for l in range(depth):
        win_t, wab_t, wsb_t, wo, wgu_t, wd = gathered[l]
        gq = jnp.tile(q_norm[l], n_q_heads)[None]
        gk = jnp.tile(k_norm[l], n_q_heads // Q_PER_KV)[None]
        bt = b_spatial[l].T
        h = _rmsnorm_fwd(xl, mix_norm[l][None], f"mix_norm_fwd_{l}")
        proj = _mm(h, win_t, "nt", F32, f"in_proj_{l}")
        attn = _attn_fwd(proj, tables, gq, gk, sinks[l], wq, wk, f"attn_fwd_{l}")
        sgu = _sgu_fwd(proj, sgu_ln_g[l][None], sgu_ln_b[l][None], w_spatial[l], bt, off_u, ws, f"sgu_fwd_{l}")
        br_a = _mm(attn, wab_t, "nt", F32, f"attn_branch_{l}")
        br_b = _mm(sgu, wsb_t, "nt", F32, f"sgu_branch_{l}")
        merged = _merge_fwd(br_a, br_b, proj, off_g, f"merge_fwd_{l}")
        x1 = _mm(merged, wo, "nn", F32, f"out_proj_{l}", residual=xl)
        h2 = _rmsnorm_fwd(x1, ffn_norm[l][None], f"ffn_norm_fwd_{l}")
        gu = _mm(h2, wgu_t, "nt", F32, f"gate_up_{l}")
        act = _swiglu_fwd(gu, f"swiglu_fwd_{l}")
        x2 = _mm(act, wd, "nn", F32, f"down_proj_{l}", residual=x1)
        saved.append(dict(x0=xl, h=h, proj=proj, attn=attn, sgu=sgu, br_a=br_a, br_b=br_b, merged=merged, x1=x1,
                          h2=h2, gu=gu, act=act, gq=gq, gk=gk, bt=bt))
        xl = x2

    loss_part, dx = _loss_and_grad(xl, loss_target[0], "loss")
    loss = lax.psum(loss_part[0, 0], ("x", "y", "c"))

    big_grads = [None] * depth
    small_grads = [None] * depth
    for l in reversed(range(depth)):
        win_t, wab_t, wsb_t, wo, wgu_t, wd = gathered[l]
        s = saved[l]
        dx16 = dx.astype(BF16)
        dact = _mm(dx16, wd, "nt", F32, f"d_act_{l}")
        g_wd = _mm(s["act"], dx16, "tn", BF16, f"g_w_down_{l}")
        dgu = _swiglu_bwd(s["gu"], dact, f"swiglu_bwd_{l}")
        dh2 = _mm(dgu, wgu_t, "nn", F32, f"d_h2_{l}")
        g_wgu_t = _mm(dgu, s["h2"], "tn", BF16, f"g_w_gate_up_{l}")
        dx1, g_ffn = _rmsnorm_bwd(s["x1"], ffn_norm[l][None], dh2, dx, f"ffn_norm_bwd_{l}")
        dx1_16 = dx1.astype(BF16)
        dmerged = _mm(dx1_16, wo, "nt", F32, f"d_merged_{l}")
        g_wo = _mm(s["merged"], dx1_16, "tn", BF16, f"g_w_out_{l}")
        d_a, d_b, dla, dlb = _merge_bwd(s["br_a"], s["br_b"], s["proj"], dmerged, off_g, f"merge_bwd_{l}")
        dattn = _mm(d_a, wab_t, "nn", F32, f"d_attn_{l}")
        g_wab_t = _mm(d_a, s["attn"], "tn", BF16, f"g_w_attn_branch_{l}")
        dsgu = _mm(d_b, wsb_t, "nn", F32, f"d_sgu_{l}")
        g_wsb_t = _mm(d_b, s["sgu"], "tn", BF16, f"g_w_sgu_branch_{l}")
        dq, dk, dv, g_gq, g_gk, g_sinks = _attn_bwd(s["proj"], dattn, tables, s["gq"], s["gk"], sinks[l], wq, wk,
                                                    f"attn_bwd_{l}")
        du, dvv, g_lng, g_lnb, g_ws, g_bs = _sgu_bwd(s["proj"], dsgu, sgu_ln_g[l][None], sgu_ln_b[l][None],
                                                     w_spatial[l], s["bt"], off_u, ws, f"sgu_bwd_{l}")
        dproj = jnp.concatenate([dq, dk.astype(BF16), dv.astype(BF16), du, dvv, dla, dlb], axis=1)
        dh = _mm(dproj, win_t, "nn", F32, f"d_h_{l}")
        g_win_t = _mm(dproj, s["h"], "tn", BF16, f"g_w_in_{l}")
        dx, g_mix = _rmsnorm_bwd(s["x0"], mix_norm[l][None], dh, dx1, f"mix_norm_bwd_{l}")
        big_grads[l] = [(g_win_t, 1), (g_wab_t, 1), (g_wsb_t, 1), (g_wo, 1), (g_wgu_t, 2), (g_wd, 1)]
        small_grads[l] = dict(
            mix_norm=g_mix[0], q_norm=g_gq[0].reshape(n_q_heads, HEAD_DIM).sum(0),
            k_norm=g_gk[0].reshape(n_q_heads // Q_PER_KV, HEAD_DIM).sum(0), sinks=g_sinks[0, :n_q_heads],
            sgu_ln_g=g_lng[0], sgu_ln_b=g_lnb[0], w_spatial=g_ws, b_spatial=g_bs[:, 0, :], ffn_norm=g_ffn[0])
    grad_x = dx[None]

    out = {k: dict(grad=[], delta=[], m=[], v=[]) for k in names}
    for l in range(depth):
        r_win_t, r_wab_t, r_wsb_t, r_wo, r_wgu_t, r_wd = _reduce_scatter(big_grads[l], str(l))
        f_loc = r_wgu_t.shape[0] // 2
        layer_grads = dict(w_in=r_win_t.T, w_attn_branch=r_wab_t.T, w_sgu_branch=r_wsb_t.T, w_out=r_wo,
                           w_gate=r_wgu_t[:f_loc].T, w_up=r_wgu_t[f_loc:].T, w_down=r_wd)
        for k, g in layer_grads.items():
            delta, nm, nv = _adam(weights[k][l], g, mom1[k][l], mom2[k][l], f"adam_{k}_{l}")
            for key, val in (("grad", g), ("delta", delta), ("m", nm), ("v", nv)):
                out[k][key].append(val)

    layer_like = {k: weights[k][0] for k in _SMALL}
    packed_g = jnp.concatenate([_pack(small_grads[l]) for l in range(depth)], axis=0)
    rows_per_layer = packed_g.shape[0] // depth
    gathered_small = _all_gather([packed_g[None, None]], "all_gather_small_grads")[0][0]
    packed = [jnp.concatenate([_pack({k: src[k][l] for k in _SMALL}) for l in range(depth)], axis=0)
              for src in (weights, mom1, mom2)]
    sg, sd, sm, sv = _small_reduce_adam(gathered_small, *packed, "small_reduce_adam")
    for key, rows in (("grad", sg), ("delta", sd), ("m", sm), ("v", sv)):
        for l in range(depth):
            vals = _unpack(rows[l * rows_per_layer:(l + 1) * rows_per_layer], layer_like)
            for k in _SMALL:
                out[k][key].append(vals[k])

    stacked = {key: [jnp.stack(out[k][key]) for k in names] for key in ("grad", "delta", "m", "v")}
    return (loss, grad_x, *stacked["grad"], *stacked["delta"], *stacked["m"], *stacked["v"])
```

```python
import functools
import math

import jax
import jax.numpy as jnp
from jax import lax
from jax.experimental import pallas as pl
from jax.experimental.pallas import tpu as pltpu

F32 = jnp.float32
BF16 = jnp.bfloat16
MESH = pl.DeviceIdType.MESH
ANY = pl.BlockSpec(memory_space=pl.ANY)

N_DEV = 8
HEAD_DIM = 64
Q_PER_KV = 4
BLOCK = 128
LANES = 128
ROPE_THETA = 10000.0
EPS = 1e-6
ADAM_LR = 0.001
ADAM_B1 = 0.9
ADAM_B2 = 0.999
ADAM_EPS = 1e-08
ADAM_WD = 0.01
ADAM_STEP = 10
NEG = -1e30
VMEM_LIMIT_BYTES = 56 * 1024 * 1024

NN = ((1,), (0,))
NT = ((1,), (1,))
TN = ((0,), (0,))


def _dot(a, b, dims):
    return lax.dot_general(a, b, (dims, ((), ())), preferred_element_type=F32)


def _params(*sem):
    return pltpu.CompilerParams(dimension_semantics=sem, vmem_limit_bytes=VMEM_LIMIT_BYTES)


def _divisor_tile(n, limit, unit):
    if n <= limit:
        return n
    best = unit
    for t in range(unit, limit + 1, unit):
        if n % t == 0:
            best = t
    assert n % best == 0, (n, limit, unit)
    return best


def _mm(a, b, mode, out_dtype, name, residual=None):
    parts = a.shape[0] if a.ndim == 3 else 1
    a2 = a.shape[-2:]
    if mode == "nn":
        (m, kp), (k2, n) = a2, b.shape
        k, mp = kp * parts, m
    elif mode == "nt":
        (m, kp), (n, k2) = a2, b.shape
        k, mp = kp * parts, m
    else:
        (k, mp), (k2, n) = a2, b.shape
        m, kp = mp * parts, k
    assert k == k2, (name, a.shape, b.shape)
    tm = _divisor_tile(mp, 512 if mode == "tn" else 1024, 128)
    tn = _divisor_tile(n, 2048 if mode == "tn" else 512, 128)
    tk = _divisor_tile(kp, 2816, 128)
    nk = k // tk
    kpb, mpb = kp // tk, mp // tm
    dims = {"nn": NN, "nt": NT, "tn": TN}[mode]
    lead = (None,) if a.ndim == 3 else ()
    if mode == "tn":
        a_index = lambda i, j, kk: (i // mpb, kk, i % mpb) if lead else (kk, i)
        a_spec = pl.BlockSpec(lead + (tk, tm), a_index)
    else:
        a_index = lambda i, j, kk: (kk // kpb, i, kk % kpb) if lead else (i, kk)
        a_spec = pl.BlockSpec(lead + (tm, tk), a_index)
    if mode == "nt":
        b_spec = pl.BlockSpec((tn, tk), lambda i, j, kk: (j, kk))
    else:
        b_spec = pl.BlockSpec((tk, tn), lambda i, j, kk: (kk, j))
    o_spec = pl.BlockSpec((tm, tn), lambda i, j, kk: (i, j))
    has_res = residual is not None

    def body(*refs):
        if has_res:
            a_ref, b_ref, r_ref, o_ref, acc_ref = refs
        else:
            a_ref, b_ref, o_ref, acc_ref = refs
            r_ref = None
        kk = pl.program_id(2)
        p = _dot(a_ref[...], b_ref[...], dims)

        def finish(total):
            if has_res:
                total = total + r_ref[...]
            o_ref[...] = total.astype(o_ref.dtype)

        if nk == 1:
            finish(p)
        else:
            @pl.when(kk == 0)
            def _():
                acc_ref[...] = p

            @pl.when(jnp.logical_and(kk > 0, kk < nk - 1))
            def _():
                acc_ref[...] += p

            @pl.when(kk == nk - 1)
            def _():
                finish(acc_ref[...] + p)

    in_specs = [a_spec, b_spec] + ([o_spec] if has_res else [])
    args = (a, b) + ((residual,) if has_res else ())
    acc_shape = (tm, tn) if nk > 1 else (8, LANES)
    return pl.pallas_call(
        body,
        name=name,
        grid=(m // tm, n // tn, nk),
        in_specs=in_specs,
        out_specs=o_spec,
        out_shape=jax.ShapeDtypeStruct((m, n), out_dtype),
        scratch_shapes=[pltpu.VMEM(acc_shape, F32)],
        compiler_params=_params("parallel", "parallel", "arbitrary"),
    )(*args)


def _rmsnorm_fwd(x, g, name):
    t, d = x.shape
    tr = _divisor_tile(t, 256, 8)

    def body(x_ref, g_ref, h_ref):
        xv = x_ref[...]
        rstd = lax.rsqrt(jnp.mean(xv * xv, axis=-1, keepdims=True) + EPS)
        h_ref[...] = (xv * rstd * g_ref[...]).astype(h_ref.dtype)

    return pl.pallas_call(
        body,
        name=name,
        grid=(t // tr,),
        in_specs=[pl.BlockSpec((tr, d), lambda i: (i, 0)), pl.BlockSpec((1, d), lambda i: (0, 0))],
        out_specs=pl.BlockSpec((tr, d), lambda i: (i, 0)),
        out_shape=jax.ShapeDtypeStruct((t, d), BF16),
        compiler_params=_params("parallel"),
    )(x, g)


def _rmsnorm_bwd(x, g, dh, dres, name):
    t, d = x.shape
    tr = _divisor_tile(t, 256, 8)

    def body(x_ref, g_ref, dh_ref, dres_ref, dx_ref, dg_ref):
        i = pl.program_id(0)
        xv = x_ref[...]
        rstd = lax.rsqrt(jnp.mean(xv * xv, axis=-1, keepdims=True) + EPS)
        xh = xv * rstd
        dhv = dh_ref[...]
        dxh = dhv * g_ref[...]
        dx_ref[...] = dres_ref[...] + rstd * (dxh - xh * jnp.mean(dxh * xh, axis=-1, keepdims=True))
        part = jnp.broadcast_to(jnp.sum(dhv * xh, axis=0, keepdims=True), dg_ref.shape)

        @pl.when(i == 0)
        def _():
            dg_ref[...] = part

        @pl.when(i > 0)
        def _():
            dg_ref[...] += part

    row = pl.BlockSpec((tr, d), lambda i: (i, 0))
    return pl.pallas_call(
        body,
        name=name,
        grid=(t // tr,),
        in_specs=[row, pl.BlockSpec((1, d), lambda i: (0, 0)), row, row],
        out_specs=[row, pl.BlockSpec((8, d), lambda i: (0, 0))],
        out_shape=[jax.ShapeDtypeStruct((t, d), F32), jax.ShapeDtypeStruct((8, d), F32)],
        compiler_params=_params("arbitrary"),
    )(x, g, dh, dres)


def _lane(shape):
    return lax.broadcasted_iota(jnp.int32, shape, 1)


def _group_sum64(s):
    row = lax.broadcasted_iota(jnp.int32, (LANES, LANES), 0)
    col = lax.broadcasted_iota(jnp.int32, (LANES, LANES), 1)
    ones = jnp.where((row >= HEAD_DIM) == (col >= HEAD_DIM), 1.0, 0.0).astype(BF16)
    out = []
    for t in range(s.shape[1] // LANES):
        piece = s[:, LANES * t:LANES * t + LANES]
        hi = piece.astype(BF16)
        lo = (piece - hi.astype(F32)).astype(BF16)
        out.append(_dot(hi, ones, NN) + _dot(lo, ones, NN))
    return out[0] if len(out) == 1 else jnp.concatenate(out, axis=1)


def _swap32(x):
    w = x.shape[1]
    return jnp.where((_lane(x.shape) & 32) == 0, pltpu.roll(x, w - 32, axis=1), pltpu.roll(x, 32, axis=1))


def _rope(x, c, s):
    return x * c + _swap32(x) * s


def _rope_t(dy, c, s):
    return dy * c + _swap32(dy * s)


def _head_norm(x):
    rstd = lax.rsqrt(_group_sum64(x * x) * (1.0 / HEAD_DIM) + EPS)
    return x * rstd, rstd


def _head_norm_bwd(dxh, xh, rstd):
    return rstd * (dxh - xh * (_group_sum64(dxh * xh) * (1.0 / HEAD_DIM)))


def _roll64(x):
    return pltpu.roll(x, 64, axis=1)


def _attn_specs(wq, wk):
    kb = wq // wk
    prev = lambda i: jnp.maximum(i - 1, 0)
    return dict(
        q=pl.BlockSpec((BLOCK, wq), lambda i: (i, 0)),
        kc=pl.BlockSpec((BLOCK, wk), lambda i: (i, kb)),
        kp=pl.BlockSpec((BLOCK, wk), lambda i: (prev(i), kb)),
        vc=pl.BlockSpec((BLOCK, wk), lambda i: (i, kb + 1)),
        vp=pl.BlockSpec((BLOCK, wk), lambda i: (prev(i), kb + 1)),
        tq=pl.BlockSpec((BLOCK, wq), lambda i: (i, 0)),
        tkp=pl.BlockSpec((BLOCK, wk), lambda i: (prev(i), 0)),
        gq=pl.BlockSpec((1, wq), lambda i: (0, 0)),
        gk=pl.BlockSpec((1, wk), lambda i: (0, 0)),
        sinks=pl.BlockSpec(memory_space=pltpu.SMEM),
    )


def _attn_prologue(i, q_ref, kc_ref, kp_ref, cq_ref, sq_ref, ckp_ref, skp_ref, gq_ref, gk_ref):
    wk = kc_ref.shape[1]
    cq, sq = cq_ref[...], sq_ref[...]
    ck, sk = cq[:, :wk], sq[:, :wk]
    qh, q_rstd = _head_norm(q_ref[...])
    kch, kc_rstd = _head_norm(kc_ref[...])
    kph, kp_rstd = _head_norm(kp_ref[...])
    qn = _rope(qh * gq_ref[...], cq, sq)
    knc = _rope(kch * gk_ref[...], ck, sk)
    knp = _rope(kph * gk_ref[...], ckp_ref[...], skp_ref[...])
    row = lax.broadcasted_iota(jnp.int32, (BLOCK, BLOCK), 0)
    col = lax.broadcasted_iota(jnp.int32, (BLOCK, BLOCK), 1)
    mask_c = col <= row
    mask_p = jnp.logical_and(col > row, i > 0)
    half = (col >= 64).astype(jnp.int32)
    return dict(cq=cq, sq=sq, ck=ck, sk=sk, qh=qh, q_rstd=q_rstd, kch=kch, kc_rstd=kc_rstd, kph=kph,
                kp_rstd=kp_rstd, qn=qn, knc=knc, knp=knp, mask_c=mask_c, mask_p=mask_p, half=half)


def _head_scores(st, t, e, sink, scale):
    g = (2 * t) // Q_PER_KV
    ks, kpar = g // 2, g % 2
    sl = slice(LANES * ks, LANES * ks + LANES)
    mine = st["half"] == e
    qm = jnp.where(mine, st["qn"][:, LANES * t:LANES * t + LANES], 0.0).astype(BF16)
    kc, kp = st["knc"][:, sl], st["knp"][:, sl]
    flip = e != kpar
    if flip:
        kc, kp = _roll64(kc), _roll64(kp)
    kc, kp = kc.astype(BF16), kp.astype(BF16)
    s_c = jnp.where(st["mask_c"], _dot(qm, kc, NT) * scale, NEG)
    s_p = jnp.where(st["mask_p"], _dot(qm, kp, NT) * scale, NEG)
    m = jnp.maximum(jnp.maximum(jnp.max(s_c, axis=1, keepdims=True), jnp.max(s_p, axis=1, keepdims=True)), sink)
    p_c, p_p = jnp.exp(s_c - m), jnp.exp(s_p - m)
    p_s = jnp.exp(sink - m)
    inv = 1.0 / (jnp.sum(p_c, axis=1, keepdims=True) + jnp.sum(p_p, axis=1, keepdims=True) + p_s)
    return dict(sl=sl, mine=mine, flip=flip, qm=qm, kc=kc, kp=kp, pr_c=p_c * inv, pr_p=p_p * inv, pr_s=p_s * inv)


def _attn_fwd(proj, tables, gq, gk, sinks, wq, wk, name):
    t = proj.shape[0]
    nb = t // BLOCK
    sp = _attn_specs(wq, wk)
    scale = HEAD_DIM ** -0.5
    cos_t, sin_t = tables

    def body(sinks_ref, q_ref, kc_ref, kp_ref, vc_ref, vp_ref, cq_ref, sq_ref, ckp_ref, skp_ref, gq_ref, gk_ref,
             o_ref):
        i = pl.program_id(0)
        st = _attn_prologue(i, q_ref, kc_ref, kp_ref, cq_ref, sq_ref, ckp_ref, skp_ref, gq_ref, gk_ref)
        vc_all, vp_all = vc_ref[...], vp_ref[...]
        for ts in range(wq // LANES):
            acc = jnp.zeros((BLOCK, LANES), F32)
            for e in (0, 1):
                hs = _head_scores(st, ts, e, sinks_ref[2 * ts + e], scale)
                vc, vp = vc_all[:, hs["sl"]], vp_all[:, hs["sl"]]
                if hs["flip"]:
                    vc, vp = _roll64(vc), _roll64(vp)
                vc = jnp.where(hs["mine"], vc, 0.0).astype(BF16)
                vp = jnp.where(hs["mine"], vp, 0.0).astype(BF16)
                acc = acc + _dot(hs["pr_c"].astype(BF16), vc, NN) + _dot(hs["pr_p"].astype(BF16), vp, NN)
            o_ref[:, LANES * ts:LANES * ts + LANES] = acc.astype(o_ref.dtype)

    return pl.pallas_call(
        body,
        name=name,
        grid=(nb,),
        in_specs=[sp["sinks"], sp["q"], sp["kc"], sp["kp"], sp["vc"], sp["vp"], sp["tq"], sp["tq"], sp["tkp"],
                  sp["tkp"], sp["gq"], sp["gk"]],
        out_specs=pl.BlockSpec((BLOCK, wq), lambda i: (i, 0)),
        out_shape=jax.ShapeDtypeStruct((t, wq), BF16),
        compiler_params=_params("parallel"),
    )(sinks, proj, proj, proj, proj, proj, cos_t, sin_t, cos_t, sin_t, gq, gk)


def _attn_bwd(proj, dout, tables, gq, gk, sinks, wq, wk, name):
    t = proj.shape[0]
    nb = t // BLOCK
    sp = _attn_specs(wq, wk)
    scale = HEAD_DIM ** -0.5
    cos_t, sin_t = tables

    def body(sinks_ref, q_ref, kc_ref, kp_ref, vc_ref, vp_ref, cq_ref, sq_ref, ckp_ref, skp_ref, gq_ref, gk_ref,
             do_ref, dq_ref, dk_ref, dv_ref, dgq_ref, dgk_ref, dsk_ref, dqn_ref, dknc_ref, dknp_ref, dvc_ref,
             dvp_ref):
        i = pl.program_id(0)
        st = _attn_prologue(i, q_ref, kc_ref, kp_ref, cq_ref, sq_ref, ckp_ref, skp_ref, gq_ref, gk_ref)
        vc_all, vp_all = vc_ref[...], vp_ref[...]
        dknc_ref[...] = jnp.zeros_like(dknc_ref)
        dknp_ref[...] = jnp.zeros_like(dknp_ref)
        dvc_ref[...] = jnp.zeros_like(dvc_ref)
        dvp_ref[...] = jnp.zeros_like(dvp_ref)
        lane8 = _lane((8, LANES))
        dsinks = jnp.zeros((8, LANES), F32)
        for ts in range(wq // LANES):
            dq_acc = jnp.zeros((BLOCK, LANES), F32)
            for e in (0, 1):
                hs = _head_scores(st, ts, e, sinks_ref[2 * ts + e], scale)
                sl, flip = hs["sl"], hs["flip"]
                vc, vp = vc_all[:, sl], vp_all[:, sl]
                if flip:
                    vc, vp = _roll64(vc), _roll64(vp)
                dom = jnp.where(hs["mine"], do_ref[:, LANES * ts:LANES * ts + LANES], 0.0).astype(BF16)
                dp_c = _dot(dom, vc.astype(BF16), NT)
                dp_p = _dot(dom, vp.astype(BF16), NT)
                pr_c, pr_p = hs["pr_c"], hs["pr_p"]
                rs = jnp.sum(pr_c * dp_c, axis=1, keepdims=True) + jnp.sum(pr_p * dp_p, axis=1, keepdims=True)
                ds_c = (pr_c * (dp_c - rs) * scale)
                ds_p = (pr_p * (dp_p - rs) * scale)
                dsink = jnp.sum(-hs["pr_s"] * rs)
                dsinks = dsinks + jnp.where(lane8 == 2 * ts + e, dsink, 0.0)
                dq_acc = dq_acc + jnp.where(
                    hs["mine"], _dot(ds_c.astype(BF16), hs["kc"], NN) + _dot(ds_p.astype(BF16), hs["kp"], NN), 0.0)
                dv_c = _dot(pr_c.T.astype(BF16), dom, NN)
                dv_p = _dot(pr_p.T.astype(BF16), dom, NN)
                dk_c = _dot(ds_c.T.astype(BF16), hs["qm"], NN)
                dk_p = _dot(ds_p.T.astype(BF16), hs["qm"], NN)
                if flip:
                    dv_c, dv_p, dk_c, dk_p = _roll64(dv_c), _roll64(dv_p), _roll64(dk_c), _roll64(dk_p)
                dvc_ref[:, sl] += dv_c
                dvp_ref[:, sl] += dv_p
                dknc_ref[:, sl] += dk_c
                dknp_ref[:, sl] += dk_p
            dqn_ref[:, LANES * ts:LANES * ts + LANES] = dq_acc

        gqv, gkv = gq_ref[...], gk_ref[...]
        dqg = _rope_t(dqn_ref[...], st["cq"], st["sq"])
        dq_ref[...] = _head_norm_bwd(dqg * gqv, st["qh"], st["q_rstd"]).astype(dq_ref.dtype)
        dkcg = _rope_t(dknc_ref[...], st["ck"], st["sk"])
        dkpg = _rope_t(dknp_ref[...], ckp_ref[...], skp_ref[...])
        dk_cur = _head_norm_bwd(dkcg * gkv, st["kch"], st["kc_rstd"])
        dk_prev = _head_norm_bwd(dkpg * gkv, st["kph"], st["kp_rstd"])
        dgq_part = jnp.broadcast_to(jnp.sum(dqg * st["qh"], axis=0, keepdims=True), dgq_ref.shape)
        dgk_part = jnp.broadcast_to(
            jnp.sum(dkcg * st["kch"] + dkpg * st["kph"], axis=0, keepdims=True), dgk_ref.shape)
        cur = pl.ds(pl.multiple_of(i * BLOCK, BLOCK), BLOCK)
        dk_ref[cur, :] = dk_cur
        dv_ref[cur, :] = dvc_ref[...]

        @pl.when(i == 0)
        def _():
            dgq_ref[...] = dgq_part
            dgk_ref[...] = dgk_part
            dsk_ref[...] = dsinks

        @pl.when(i > 0)
        def _():
            before = pl.ds(pl.multiple_of((i - 1) * BLOCK, BLOCK), BLOCK)
            dk_ref[before, :] += dk_prev
            dv_ref[before, :] += dvp_ref[...]
            dgq_ref[...] += dgq_part
            dgk_ref[...] += dgk_part
            dsk_ref[...] += dsinks

    whole = lambda shape: pl.BlockSpec(shape, lambda i: (0, 0))
    return pl.pallas_call(
        body,
        name=name,
        grid=(nb,),
        in_specs=[sp["sinks"], sp["q"], sp["kc"], sp["kp"], sp["vc"], sp["vp"], sp["tq"], sp["tq"], sp["tkp"],
                  sp["tkp"], sp["gq"], sp["gk"], pl.BlockSpec((BLOCK, wq), lambda i: (i, 0))],
        out_specs=[pl.BlockSpec((BLOCK, wq), lambda i: (i, 0)), whole((t, wk)), whole((t, wk)), whole((8, wq)),
                   whole((8, wk)), whole((8, LANES))],
        out_shape=[jax.ShapeDtypeStruct((t, wq), BF16), jax.ShapeDtypeStruct((t, wk), F32),
                   jax.ShapeDtypeStruct((t, wk), F32), jax.ShapeDtypeStruct((8, wq), F32),
                   jax.ShapeDtypeStruct((8, wk), F32), jax.ShapeDtypeStruct((8, LANES), F32)],
        scratch_shapes=[pltpu.VMEM((BLOCK, wq), F32), pltpu.VMEM((BLOCK, wk), F32), pltpu.VMEM((BLOCK, wk), F32),
                        pltpu.VMEM((BLOCK, wk), F32), pltpu.VMEM((BLOCK, wk), F32)],
        compiler_params=_params("arbitrary"),
    )(sinks, proj, proj, proj, proj, proj, cos_t, sin_t, cos_t, sin_t, gq, gk, dout)


_GELU_K = math.sqrt(2.0 / math.pi)
_GELU_A = 0.044715


def _gelu(x):
    return 0.5 * x * (1.0 + jnp.tanh(_GELU_K * (x + _GELU_A * x * x * x)))


def _gelu_grad(x):
    th = jnp.tanh(_GELU_K * (x + _GELU_A * x * x * x))
    return 0.5 * (1.0 + th) + 0.5 * x * (1.0 - th * th) * (_GELU_K * (1.0 + 3.0 * _GELU_A * x * x))


def _group_ln(v):
    mu = jnp.mean(v, axis=1, keepdims=True)
    cen = v - mu
    rstd = lax.rsqrt(jnp.mean(cen * cen, axis=1, keepdims=True) + EPS)
    return cen * rstd, rstd


def _sgu_geometry(off_u, ws):
    cw = math.gcd(off_u, ws)
    return cw, ws // cw, off_u // cw, (off_u + ws) // cw


def _sgu_fwd(proj, ln_g, ln_b, w_s, bt, off_u, ws, name):
    t = proj.shape[0]
    nb = t // BLOCK
    cw, nc, ub, vb = _sgu_geometry(off_u, ws)
    gpc = cw // LANES
    ng = ws // LANES

    def body(u_ref, v_ref, g_ref, b_ref, w_ref, bt_ref, o_ref):
        jc = pl.program_id(0)
        row = lax.broadcasted_iota(jnp.int32, (BLOCK, BLOCK), 0)
        col = lax.broadcasted_iota(jnp.int32, (BLOCK, BLOCK), 1)
        lane_g = _lane((BLOCK, ng))
        for gi in range(gpc):
            sl = slice(LANES * gi, LANES * gi + LANES)
            xh, _ = _group_ln(_gelu(v_ref[:, sl]))
            vn = xh * g_ref[:, sl] + b_ref[:, sl]
            w = jnp.where(row >= col, w_ref[gi], 0.0).astype(BF16)
            bias = jnp.sum(jnp.where(lane_g == jc * gpc + gi, bt_ref[...], 0.0), axis=1, keepdims=True)
            s = _dot(w, vn.astype(BF16), NN) + bias
            o_ref[:, sl] = (_gelu(u_ref[:, sl]) * s).astype(o_ref.dtype)

    return pl.pallas_call(
        body,
        name=name,
        grid=(nc, nb),
        in_specs=[pl.BlockSpec((BLOCK, cw), lambda jc, i: (i, ub + jc)),
                  pl.BlockSpec((BLOCK, cw), lambda jc, i: (i, vb + jc)),
                  pl.BlockSpec((1, cw), lambda jc, i: (0, jc)),
                  pl.BlockSpec((1, cw), lambda jc, i: (0, jc)),
                  pl.BlockSpec((gpc, BLOCK, BLOCK), lambda jc, i: (jc, 0, 0)),
                  pl.BlockSpec((BLOCK, ng), lambda jc, i: (0, 0))],
        out_specs=pl.BlockSpec((BLOCK, cw), lambda jc, i: (i, jc)),
        out_shape=jax.ShapeDtypeStruct((t, ws), BF16),
        compiler_params=_params("parallel", "parallel"),
    )(proj, proj, ln_g, ln_b, w_s, bt)


def _sgu_bwd(proj, dout, ln_g, ln_b, w_s, bt, off_u, ws, name):
    t = proj.shape[0]
    nb = t // BLOCK
    cw, nc, ub, vb = _sgu_geometry(off_u, ws)
    gpc = cw // LANES
    ng = ws // LANES

    def body(u_ref, v_ref, g_ref, b_ref, w_ref, bt_ref, do_ref, du_ref, dv_ref, dg_ref, db_ref, dw_ref, dbs_ref,
             bacc_ref):
        jc = pl.program_id(0)
        i = pl.program_id(1)
        row = lax.broadcasted_iota(jnp.int32, (BLOCK, BLOCK), 0)
        col = lax.broadcasted_iota(jnp.int32, (BLOCK, BLOCK), 1)
        lane_g = _lane((BLOCK, ng))
        tri = row >= col

        @pl.when(i == 0)
        def _():
            dg_ref[...] = jnp.zeros_like(dg_ref)
            db_ref[...] = jnp.zeros_like(db_ref)
            dw_ref[...] = jnp.zeros_like(dw_ref)
            bacc_ref[...] = jnp.zeros_like(bacc_ref)

        for gi in range(gpc):
            sl = slice(LANES * gi, LANES * gi + LANES)
            u_raw, v_raw = u_ref[:, sl], v_ref[:, sl]
            xh, rstd = _group_ln(_gelu(v_raw))
            gam = g_ref[:, sl]
            vn = (xh * gam + b_ref[:, sl]).astype(BF16)
            w = jnp.where(tri, w_ref[gi], 0.0)
            bias = jnp.sum(jnp.where(lane_g == jc * gpc + gi, bt_ref[...], 0.0), axis=1, keepdims=True)
            s = _dot(w.astype(BF16), vn, NN) + bias
            dov = do_ref[:, sl]
            du_ref[:, sl] = (dov * s * _gelu_grad(u_raw)).astype(du_ref.dtype)
            ds = dov * _gelu(u_raw)
            ds16 = ds.astype(BF16)
            dw_ref[gi] += jnp.where(tri, _dot(ds16, vn, NT), 0.0)
            bacc_ref[gi] += ds
            dvn = _dot(w.T.astype(BF16), ds16, NN)
            dg_ref[:, sl] += jnp.broadcast_to(jnp.sum(dvn * xh, axis=0, keepdims=True), (8, LANES))
            db_ref[:, sl] += jnp.broadcast_to(jnp.sum(dvn, axis=0, keepdims=True), (8, LANES))
            dxh = dvn * gam
            dvg = rstd * (dxh - jnp.mean(dxh, axis=1, keepdims=True)
                          - xh * jnp.mean(dxh * xh, axis=1, keepdims=True))
            dv_ref[:, sl] = (dvg * _gelu_grad(v_raw)).astype(dv_ref.dtype)

        @pl.when(i == nb - 1)
        def _():
            for gi in range(gpc):
                dbs_ref[gi] = jnp.broadcast_to(jnp.sum(bacc_ref[gi].T, axis=0, keepdims=True), (8, LANES))

    blk = lambda base: pl.BlockSpec((BLOCK, cw), lambda jc, i: (i, base + jc))
    vec = pl.BlockSpec((1, cw), lambda jc, i: (0, jc))
    acc = pl.BlockSpec((8, cw), lambda jc, i: (0, jc))
    wsp = pl.BlockSpec((gpc, BLOCK, BLOCK), lambda jc, i: (jc, 0, 0))
    return pl.pallas_call(
        body,
        name=name,
        grid=(nc, nb),
        in_specs=[blk(ub), blk(vb), vec, vec, wsp, pl.BlockSpec((BLOCK, ng), lambda jc, i: (0, 0)), blk(0)],
        out_specs=[blk(0), blk(0), acc, acc, wsp, pl.BlockSpec((gpc, 8, LANES), lambda jc, i: (jc, 0, 0))],
        out_shape=[jax.ShapeDtypeStruct((t, ws), BF16), jax.ShapeDtypeStruct((t, ws), BF16),
                   jax.ShapeDtypeStruct((8, ws), F32), jax.ShapeDtypeStruct((8, ws), F32),
                   jax.ShapeDtypeStruct((ng, BLOCK, BLOCK), F32), jax.ShapeDtypeStruct((ng, 8, LANES), F32)],
        scratch_shapes=[pltpu.VMEM((gpc, BLOCK, BLOCK), F32)],
        compiler_params=_params("arbitrary", "arbitrary"),
    )(proj, proj, ln_g, ln_b, w_s, bt, dout)


def _sigmoid(x):
    return 1.0 / (1.0 + jnp.exp(-x))


def _merge_geometry(off_g, d):
    cw = math.gcd(off_g, d)
    return cw, d // cw, off_g // cw, (off_g + d) // cw


def _merge_fwd(a, b, proj, off_g, name):
    t, d = a.shape
    cw, nc, ab, bb = _merge_geometry(off_g, d)
    tr = _divisor_tile(t, 512, 8)

    def body(a_ref, b_ref, la_ref, lb_ref, o_ref):
        o_ref[...] = (_sigmoid(la_ref[...]) * a_ref[...] + _sigmoid(lb_ref[...]) * b_ref[...]).astype(o_ref.dtype)

    blk = lambda base: pl.BlockSpec((tr, cw), lambda i, j: (i, base + j))
    return pl.pallas_call(
        body,
        name=name,
        grid=(t // tr, nc),
        in_specs=[blk(0), blk(0), blk(ab), blk(bb)],
        out_specs=blk(0),
        out_shape=jax.ShapeDtypeStruct((t, d), BF16),
        compiler_params=_params("parallel", "parallel"),
    )(a, b, proj, proj)


def _merge_bwd(a, b, proj, dm, off_g, name):
    t, d = a.shape
    cw, nc, ab, bb = _merge_geometry(off_g, d)
    tr = _divisor_tile(t, 512, 8)

    def body(a_ref, b_ref, la_ref, lb_ref, dm_ref, da_ref, db_ref, dla_ref, dlb_ref):
        dmv = dm_ref[...]
        ga, gb = _sigmoid(la_ref[...]), _sigmoid(lb_ref[...])
        da_ref[...] = (dmv * ga).astype(da_ref.dtype)
        db_ref[...] = (dmv * gb).astype(db_ref.dtype)
        dla_ref[...] = (dmv * a_ref[...] * ga * (1.0 - ga)).astype(dla_ref.dtype)
        dlb_ref[...] = (dmv * b_ref[...] * gb * (1.0 - gb)).astype(dlb_ref.dtype)

    blk = lambda base: pl.BlockSpec((tr, cw), lambda i, j: (i, base + j))
    return pl.pallas_call(
        body,
        name=name,
        grid=(t // tr, nc),
        in_specs=[blk(0), blk(0), blk(ab), blk(bb), blk(0)],
        out_specs=[blk(0)] * 4,
        out_shape=[jax.ShapeDtypeStruct((t, d), BF16)] * 4,
        compiler_params=_params("parallel", "parallel"),
    )(a, b, proj, proj, dm)


def _swiglu_fwd(gu, name):
    t, f2 = gu.shape
    f = f2 // 2
    cw = _divisor_tile(f, 1536, 128)
    nc = f // cw
    tr = _divisor_tile(t, 256, 8)

    def body(g_ref, u_ref, o_ref):
        gv = g_ref[...]
        o_ref[...] = (gv * _sigmoid(gv) * u_ref[...]).astype(o_ref.dtype)

    blk = lambda base: pl.BlockSpec((tr, cw), lambda i, j: (i, base + j))
    return pl.pallas_call(
        body,
        name=name,
        grid=(t // tr, nc),
        in_specs=[blk(0), blk(nc)],
        out_specs=blk(0),
        out_shape=jax.ShapeDtypeStruct((t, f), BF16),
        compiler_params=_params("parallel", "parallel"),
    )(gu, gu)


def _swiglu_bwd(gu, dact, name):
    t, f2 = gu.shape
    f = f2 // 2
    cw = _divisor_tile(f, 1536, 128)
    nc = f // cw
    tr = _divisor_tile(t, 256, 8)

    def body(g_ref, u_ref, da_ref, o_ref):
        gv, dav = g_ref[...], da_ref[...]
        sg = _sigmoid(gv)
        o_ref[0] = (dav * u_ref[...] * (sg + gv * sg * (1.0 - sg))).astype(o_ref.dtype)
        o_ref[1] = (dav * gv * sg).astype(o_ref.dtype)

    blk = lambda base: pl.BlockSpec((tr, cw), lambda i, j: (i, base + j))
    return pl.pallas_call(
        body,
        name=name,
        grid=(t // tr, nc),
        in_specs=[blk(0), blk(nc), blk(0)],
        out_specs=pl.BlockSpec((2, tr, cw), lambda i, j: (0, i, j)),
        out_shape=jax.ShapeDtypeStruct((2, t, f), BF16),
        compiler_params=_params("parallel", "parallel"),
    )(gu, gu, dact)


def _loss_and_grad(y, target, name):
    t, d = y.shape
    tr = _divisor_tile(t, 256, 8)

    def body(y_ref, t_ref, l_ref, dy_ref):
        i = pl.program_id(0)
        err = y_ref[...] - t_ref[...]
        dy_ref[...] = err * (1.0 / d)
        part = jnp.broadcast_to(0.5 * jnp.sum(err * err) * (1.0 / d), l_ref.shape)

        @pl.when(i == 0)
        def _():
            l_ref[...] = part

        @pl.when(i > 0)
        def _():
            l_ref[...] += part

    row = pl.BlockSpec((tr, d), lambda i: (i, 0))
    return pl.pallas_call(
        body,
        name=name,
        grid=(t // tr,),
        in_specs=[row, row],
        out_specs=[pl.BlockSpec((8, LANES), lambda i: (0, 0)), row],
        out_shape=[jax.ShapeDtypeStruct((8, LANES), F32), jax.ShapeDtypeStruct((t, d), F32)],
        compiler_params=_params("arbitrary"),
    )(y, target)


def _adam_math(w, g, m, v):
    m = ADAM_B1 * m + (1.0 - ADAM_B1) * g
    v = ADAM_B2 * v + (1.0 - ADAM_B2) * (g * g)
    m_hat = m / (1.0 - ADAM_B1 ** ADAM_STEP)
    v_hat = v / (1.0 - ADAM_B2 ** ADAM_STEP)
    delta = -ADAM_LR * (m_hat / (jnp.sqrt(v_hat) + ADAM_EPS) + ADAM_WD * w)
    return delta, m, v


def _row_tile(r, c):
    return _divisor_tile(r, max(8, (256 * 1024) // c // 8 * 8), 8)


def _adam(w, g, m, v, name):
    nl, r, c = w.shape
    tr = _row_tile(r, c)

    def body(w_ref, g_ref, m_ref, v_ref, d_ref, nm_ref, nv_ref):
        d_ref[...], nm_ref[...], nv_ref[...] = _adam_math(w_ref[...], g_ref[...], m_ref[...], v_ref[...])

    row = pl.BlockSpec((None, tr, c), lambda l, i: (l, i, 0))
    return pl.pallas_call(
        body,
        name=name,
        grid=(nl, r // tr),
        in_specs=[row] * 4,
        out_specs=[row] * 3,
        out_shape=[jax.ShapeDtypeStruct((nl, r, c), F32)] * 3,
        compiler_params=_params("parallel", "parallel"),
    )(w, g, m, v)


def _place_shard(x, dev, out_dtype, name):
    p, r, c = x.shape
    tr = _row_tile(r, c)

    def body(dev_ref, x_ref, o_ref):
        o_ref[...] = x_ref[...].astype(o_ref.dtype)

    return pl.pallas_call(
        body,
        name=name,
        grid_spec=pltpu.PrefetchScalarGridSpec(
            num_scalar_prefetch=1,
            grid=(p, r // tr),
            in_specs=[pl.BlockSpec((None, tr, c), lambda pi, i, dev_ref: (pi, i, 0))],
            out_specs=pl.BlockSpec((None, None, tr, c), lambda pi, i, dev_ref: (pi, dev_ref[0], i, 0)),
        ),
        out_shape=jax.ShapeDtypeStruct((p, N_DEV, r, c), out_dtype),
        compiler_params=_params("parallel", "parallel"),
    )(dev, x)


def _sum_sibling(g, land, core, name):
    p, _, _, r, c = g.shape
    tr = _row_tile(r, c)

    def body(core_ref, g_ref, l_ref, o_ref):
        o_ref[...] = (g_ref[...].astype(F32) + l_ref[...].astype(F32)).astype(o_ref.dtype)

    return pl.pallas_call(
        body,
        name=name,
        grid_spec=pltpu.PrefetchScalarGridSpec(
            num_scalar_prefetch=1,
            grid=(p, 4, r // tr),
            in_specs=[pl.BlockSpec((None, None, None, tr, c), lambda pi, q, i, core_ref: (pi, q, core_ref[0], i, 0)),
                      pl.BlockSpec((None, None, None, tr, c), lambda pi, q, i, core_ref: (pi, q, 0, i, 0))],
            out_specs=pl.BlockSpec((None, None, tr, c), lambda pi, q, i, core_ref: (pi, q, i, 0)),
        ),
        out_shape=jax.ShapeDtypeStruct((p, 4, r, c), BF16),
        compiler_params=_params("parallel", "parallel", "parallel"),
    )(core, g, land)


def _sum_chips(s, lands, chip, name):
    p, _, r, c = s.shape
    tr = _row_tile(r, c)

    def body(chip_ref, s_ref, l0_ref, l1_ref, l2_ref, o_ref):
        total = s_ref[...].astype(F32) + l0_ref[...].astype(F32)
        o_ref[...] = total + l1_ref[...].astype(F32) + l2_ref[...].astype(F32)

    land_spec = pl.BlockSpec((None, None, tr, c), lambda pi, i, chip_ref: (pi, 0, i, 0))
    return pl.pallas_call(
        body,
        name=name,
        grid_spec=pltpu.PrefetchScalarGridSpec(
            num_scalar_prefetch=1,
            grid=(p, r // tr),
            in_specs=[pl.BlockSpec((None, None, tr, c), lambda pi, i, chip_ref: (pi, chip_ref[0], i, 0)),
                      land_spec, land_spec, land_spec],
            out_specs=pl.BlockSpec((None, tr, c), lambda pi, i, chip_ref: (pi, i, 0)),
        ),
        out_shape=jax.ShapeDtypeStruct((p, r, c), F32),
        compiler_params=_params("parallel", "parallel"),
    )(chip, s, *lands)


def _small_reduce_adam(gathered, w, m, v, name):
    _, r, c = gathered.shape
    tr = _row_tile(r, c)

    def body(p_ref, w_ref, m_ref, v_ref, g_ref, d_ref, nm_ref, nv_ref):
        g = p_ref[0]
        for j in range(1, N_DEV):
            g = g + p_ref[j]
        g_ref[...] = g
        d_ref[...], nm_ref[...], nv_ref[...] = _adam_math(w_ref[...], g, m_ref[...], v_ref[...])

    row = pl.BlockSpec((tr, c), lambda i: (i, 0))
    return pl.pallas_call(
        body,
        name=name,
        grid=(r // tr,),
        in_specs=[pl.BlockSpec((N_DEV, tr, c), lambda i: (0, i, 0)), row, row, row],
        out_specs=[row] * 4,
        out_shape=[jax.ShapeDtypeStruct((r, c), F32)] * 4,
        compiler_params=_params("parallel"),
    )(gathered, w, m, v)


def _place():
    return lax.axis_index("x"), lax.axis_index("y"), lax.axis_index("c")


def _all_gather(bufs, name):
    n = len(bufs)

    def body(*refs):
        outs = refs[n:2 * n]
        send_sems, recv_sems = refs[2 * n:]
        x, y, c = _place()
        me, sibling = (x, y, c), (x, y, 1 - c)
        chips = [(1 - x, y), (x, 1 - y), (1 - x, 1 - y)]

        def block(a, px, py, pc):
            return outs[a].at[:, pl.ds(4 * px + 2 * py + pc, 1)]

        def copy(a, k, blk, to):
            return pltpu.make_async_remote_copy(
                src_ref=block(a, *blk), dst_ref=block(a, *blk), send_sem=send_sems.at[a, k],
                recv_sem=recv_sems.at[a, k], device_id=to, device_id_type=MESH)

        first = []
        for a in range(n):
            first.append(copy(a, 0, me, sibling))
            first += [copy(a, 1 + j, me, (*chip, c)) for j, chip in enumerate(chips)]
        for cp in first:
            cp.start()
        passed = []
        for j, chip in enumerate(chips):
            for a in range(n):
                copy(a, 1 + j, (*chip, c), me).wait_recv()
                fwd = copy(a, 4 + j, (*chip, c), sibling)
                fwd.start()
                passed.append(fwd)
        for a in range(n):
            copy(a, 0, sibling, me).wait_recv()
            for j, chip in enumerate(chips):
                copy(a, 4 + j, (*chip, 1 - c), me).wait_recv()
        for cp in first + passed:
            cp.wait_send()

    return pl.pallas_call(
        body,
        name=name,
        in_specs=[ANY] * n,
        out_specs=[ANY] * n,
        out_shape=[jax.ShapeDtypeStruct(b.shape, b.dtype) for b in bufs],
        input_output_aliases={a: a for a in range(n)},
        scratch_shapes=[pltpu.SemaphoreType.DMA((n, 7)), pltpu.SemaphoreType.DMA((n, 7))],
    )(*bufs)


def _swap_with_sibling(grads, name):
    n = len(grads)

    def body(*refs):
        ins, land = refs[:n], refs[n:2 * n]
        send_sems, recv_sems = refs[2 * n:]
        x, y, c = _place()
        copies = []
        for a in range(n):
            remote = pltpu.make_async_remote_copy(
                src_ref=ins[a].at[:, :, pl.ds(1 - c, 1)], dst_ref=land[a], send_sem=send_sems.at[a],
                recv_sem=recv_sems.at[a], device_id=(x, y, 1 - c), device_id_type=MESH)
            remote.start()
            copies.append(remote)
        for remote in copies:
            remote.wait()

    return pl.pallas_call(
        body,
        name=name,
        in_specs=[ANY] * n,
        out_specs=[ANY] * n,
        out_shape=[jax.ShapeDtypeStruct(g.shape[:2] + (1,) + g.shape[3:], g.dtype) for g in grads],
        scratch_shapes=[pltpu.SemaphoreType.DMA((n,)), pltpu.SemaphoreType.DMA((n,))],
    )(*grads)


def _swap_with_chips(sums, name):
    n = len(sums)

    def body(*refs):
        ins, land = refs[:n], refs[n:4 * n]
        send_sems, recv_sems = refs[4 * n:]
        x, y, c = _place()
        chips = [(1 - x, y), (x, 1 - y), (1 - x, 1 - y)]
        copies = []
        for a in range(n):
            for k, (px, py) in enumerate(chips):
                remote = pltpu.make_async_remote_copy(
                    src_ref=ins[a].at[:, pl.ds(2 * px + py, 1)], dst_ref=land[3 * a + k],
                    send_sem=send_sems.at[a, k], recv_sem=recv_sems.at[a, k], device_id=(px, py, c),
                    device_id_type=MESH)
                remote.start()
                copies.append(remote)
        for cp in copies:
            cp.wait()

    shapes = []
    for s in sums:
        shapes += [jax.ShapeDtypeStruct((s.shape[0], 1) + s.shape[2:], s.dtype)] * 3
    outs = pl.pallas_call(
        body,
        name=name,
        in_specs=[ANY] * n,
        out_specs=[ANY] * (3 * n),
        out_shape=shapes,
        scratch_shapes=[pltpu.SemaphoreType.DMA((n, 3)), pltpu.SemaphoreType.DMA((n, 3))],
    )(*sums)
    return [outs[3 * a:3 * a + 3] for a in range(n)]


def _reduce_scatter(grads, core, chip, tag):
    shaped = []
    for g, p in grads:
        rows, c = g.shape
        shaped.append(g.reshape(p, 4, 2, rows // (8 * p), c))
    theirs = _swap_with_sibling(shaped, f"rs_sibling_{tag}")
    sums = [_sum_sibling(g, o, core, f"rs_add_sibling_{tag}_{a}") for a, (g, o) in enumerate(zip(shaped, theirs))]
    lands = _swap_with_chips(sums, f"rs_chips_{tag}")
    return [_sum_chips(s, o, chip, f"rs_add_chips_{tag}_{a}") for a, (s, o) in enumerate(zip(sums, lands))]


_SMALL = ("mix_norm", "q_norm", "k_norm", "sinks", "sgu_ln_g", "sgu_ln_b", "w_spatial", "b_spatial", "ffn_norm")


def _pack_rows(a):
    flat = a.reshape(-1)
    pad = (-flat.shape[0]) % LANES
    if pad:
        flat = jnp.pad(flat, (0, pad))
    return flat.reshape(-1, LANES)


def _pack(values):
    rows = jnp.concatenate([_pack_rows(values[k]) for k in _SMALL], axis=0)
    pad = (-rows.shape[0]) % 8
    if pad:
        rows = jnp.pad(rows, ((0, pad), (0, 0)))
    return rows


def _unpack(rows, like):
    out, at = {}, 0
    for k in _SMALL:
        size = like[k].size
        nrows = -(-size // LANES)
        out[k] = rows[at:at + nrows].reshape(-1)[:size].reshape(like[k].shape)
        at += nrows
    return out


def _rope_tables(t, wq):
    pos = jnp.arange(t, dtype=F32)
    inv_freq = jnp.power(ROPE_THETA, -jnp.arange(0, HEAD_DIM, 2, dtype=F32) / HEAD_DIM)
    ang = pos[:, None] * inv_freq[None, :]
    cos, sin = jnp.cos(ang), jnp.sin(ang)
    reps = wq // HEAD_DIM
    return (jnp.tile(jnp.concatenate([cos, cos], axis=1), (1, reps)),
            jnp.tile(jnp.concatenate([-sin, sin], axis=1), (1, reps)))


def kernel(x, mix_norm, w_in, q_norm, k_norm, sinks, sgu_ln_g, sgu_ln_b, w_spatial, b_spatial, w_attn_branch, w_sgu_branch, w_out, ffn_norm, w_gate, w_up, w_down, loss_target, m_mix_norm, m_w_in, m_q_norm, m_k_norm, m_sinks, m_sgu_ln_g, m_sgu_ln_b, m_w_spatial, m_b_spatial, m_w_attn_branch, m_w_sgu_branch, m_w_out, m_ffn_norm, m_w_gate, m_w_up, m_w_down, v_mix_norm, v_w_in, v_q_norm, v_k_norm, v_sinks, v_sgu_ln_g, v_sgu_ln_b, v_w_spatial, v_b_spatial, v_w_attn_branch, v_w_sgu_branch, v_w_out, v_ffn_norm, v_w_gate, v_w_up, v_w_down):
    names = ("mix_norm", "w_in", "q_norm", "k_norm", "sinks", "sgu_ln_g", "sgu_ln_b", "w_spatial", "b_spatial",
             "w_attn_branch", "w_sgu_branch", "w_out", "ffn_norm", "w_gate", "w_up", "w_down")
    weights = dict(zip(names, (mix_norm, w_in, q_norm, k_norm, sinks, sgu_ln_g, sgu_ln_b, w_spatial, b_spatial,
                               w_attn_branch, w_sgu_branch, w_out, ffn_norm, w_gate, w_up, w_down)))
    mom1 = dict(zip(names, (m_mix_norm, m_w_in, m_q_norm, m_k_norm, m_sinks, m_sgu_ln_g, m_sgu_ln_b, m_w_spatial,
                            m_b_spatial, m_w_attn_branch, m_w_sgu_branch, m_w_out, m_ffn_norm, m_w_gate, m_w_up,
                            m_w_down)))
    mom2 = dict(zip(names, (v_mix_norm, v_w_in, v_q_norm, v_k_norm, v_sinks, v_sgu_ln_g, v_sgu_ln_b, v_w_spatial,
                            v_b_spatial, v_w_attn_branch, v_w_sgu_branch, v_w_out, v_ffn_norm, v_w_gate, v_w_up,
                            v_w_down)))
    depth = w_in.shape[0]
    _, t, d = x.shape
    n_q_heads = sinks.shape[1]
    wq = n_q_heads * HEAD_DIM
    wk = wq // Q_PER_KV
    ws = sgu_ln_g.shape[1]
    ng = ws // LANES
    off_u = wq + 2 * wk
    off_g = off_u + 2 * ws
    tables = _rope_tables(t, wq)
    px, py, pc = _place()
    core = pc.astype(jnp.int32)[None]
    chip = (2 * px + py).astype(jnp.int32)[None]
    dev = (4 * px + 2 * py + pc).astype(jnp.int32)[None]

    gathered = []
    for l in range(depth):
        shards = [
            w_in[l].T[None],
            w_attn_branch[l].T[None],
            w_sgu_branch[l].T[None],
            w_out[l][None],
            jnp.stack([w_gate[l].T, w_up[l].T]),
            w_down[l][None],
        ]
        bufs = [_place_shard(s, dev, BF16, f"place_shard_{l}_{a}") for a, s in enumerate(shards)]
        full = _all_gather(bufs, f"all_gather_weights_{l}")
        gathered.append([f.reshape(f.shape[0] * f.shape[1] * f.shape[2], f.shape[3]) for f in full])

    saved = []
    xl = x[0]
    for l in range(depth):
        win_t, wab_t, wsb_t, wo, wgu_t, wd = gathered[l]
        gq = jnp.tile(q_norm[l], n_q_heads)[None]
        gk = jnp.tile(k_norm[l], n_q_heads // Q_PER_KV)[None]
        bt = b_spatial[l].T
        h = _rmsnorm_fwd(xl, mix_norm[l][None], f"mix_norm_fwd_{l}")
        proj = _mm(h, win_t, "nt", F32, f"in_proj_{l}")
        attn = _attn_fwd(proj, tables, gq, gk, sinks[l], wq, wk, f"attn_fwd_{l}")
        sgu = _sgu_fwd(proj, sgu_ln_g[l][None], sgu_ln_b[l][None], w_spatial[l], bt, off_u, ws, f"sgu_fwd_{l}")
        br_a = _mm(attn, wab_t, "nt", F32, f"attn_branch_{l}")
        br_b = _mm(sgu, wsb_t, "nt", F32, f"sgu_branch_{l}")
        merged = _merge_fwd(br_a, br_b, proj, off_g, f"merge_fwd_{l}")
        x1 = _mm(merged, wo, "nn", F32, f"out_proj_{l}", residual=xl)
        h2 = _rmsnorm_fwd(x1, ffn_norm[l][None], f"ffn_norm_fwd_{l}")
        gu = _mm(h2, wgu_t, "nt", F32, f"gate_up_{l}")
        act = _swiglu_fwd(gu, f"swiglu_fwd_{l}")
        x2 = _mm(act, wd, "nn", F32, f"down_proj_{l}", residual=x1)
        saved.append(dict(x0=xl, h=h, proj=proj, attn=attn, sgu=sgu, br_a=br_a, br_b=br_b, merged=merged, x1=x1,
                          h2=h2, gu=gu, act=act, gq=gq, gk=gk, bt=bt))
        xl = x2

    loss_part, dx = _loss_and_grad(xl, loss_target[0], "loss")
    loss = lax.psum(loss_part[0, 0], ("x", "y", "c"))

    big_grads = [None] * depth
    small_grads = [None] * depth
    for l in reversed(range(depth)):
        win_t, wab_t, wsb_t, wo, wgu_t, wd = gathered[l]
        s = saved[l]
        dx16 = dx.astype(BF16)
        dact = _mm(dx16, wd, "nt", F32, f"d_act_{l}")
        g_wd = _mm(s["act"], dx16, "tn", BF16, f"g_w_down_{l}")
        dgu = _swiglu_bwd(s["gu"], dact, f"swiglu_bwd_{l}")
        dh2 = _mm(dgu, wgu_t, "nn", F32, f"d_h2_{l}")
        g_wgu_t = _mm(dgu, s["h2"], "tn", BF16, f"g_w_gate_up_{l}")
        dx1, g_ffn = _rmsnorm_bwd(s["x1"], ffn_norm[l][None], dh2, dx, f"ffn_norm_bwd_{l}")
        dx1_16 = dx1.astype(BF16)
        dmerged = _mm(dx1_16, wo, "nt", F32, f"d_merged_{l}")
        g_wo = _mm(s["merged"], dx1_16, "tn", BF16, f"g_w_out_{l}")
        d_a, d_b, dla, dlb = _merge_bwd(s["br_a"], s["br_b"], s["proj"], dmerged, off_g, f"merge_bwd_{l}")
        dattn = _mm(d_a, wab_t, "nn", F32, f"d_attn_{l}")
        g_wab_t = _mm(d_a, s["attn"], "tn", BF16, f"g_w_attn_branch_{l}")
        dsgu = _mm(d_b, wsb_t, "nn", F32, f"d_sgu_{l}")
        g_wsb_t = _mm(d_b, s["sgu"], "tn", BF16, f"g_w_sgu_branch_{l}")
        dq, dk, dv, g_gq, g_gk, g_sinks = _attn_bwd(s["proj"], dattn, tables, s["gq"], s["gk"], sinks[l], wq, wk,
                                                    f"attn_bwd_{l}")
        du, dvv, g_lng, g_lnb, g_ws, g_bs = _sgu_bwd(s["proj"], dsgu, sgu_ln_g[l][None], sgu_ln_b[l][None],
                                                     w_spatial[l], s["bt"], off_u, ws, f"sgu_bwd_{l}")
        dproj = jnp.concatenate([dq, dk.astype(BF16), dv.astype(BF16), du, dvv, dla, dlb], axis=1)
        dh = _mm(dproj, win_t, "nn", F32, f"d_h_{l}")
        g_win_t = _mm(dproj, s["h"], "tn", BF16, f"g_w_in_{l}")
        dx, g_mix = _rmsnorm_bwd(s["x0"], mix_norm[l][None], dh, dx1, f"mix_norm_bwd_{l}")
        big_grads[l] = [(g_win_t, 1), (g_wab_t, 1), (g_wsb_t, 1), (g_wo, 1), (g_wgu_t, 2), (g_wd, 1)]
        small_grads[l] = dict(
            mix_norm=g_mix[0], q_norm=g_gq[0].reshape(n_q_heads, HEAD_DIM).sum(0),
            k_norm=g_gk[0].reshape(n_q_heads // Q_PER_KV, HEAD_DIM).sum(0), sinks=g_sinks[0, :n_q_heads],
            sgu_ln_g=g_lng[0], sgu_ln_b=g_lnb[0], w_spatial=g_ws, b_spatial=g_bs[:, 0, :], ffn_norm=g_ffn[0])
    grad_x = dx[None]

    reduced = [_reduce_scatter(big_grads[l], core, chip, str(l)) for l in range(depth)]
    layers = range(depth)
    big = dict(
        w_in=jnp.stack([reduced[l][0][0].T for l in layers]),
        w_attn_branch=jnp.stack([reduced[l][1][0].T for l in layers]),
        w_sgu_branch=jnp.stack([reduced[l][2][0].T for l in layers]),
        w_out=jnp.stack([reduced[l][3][0] for l in layers]),
        w_gate=jnp.stack([reduced[l][4][0].T for l in layers]),
        w_up=jnp.stack([reduced[l][4][1].T for l in layers]),
        w_down=jnp.stack([reduced[l][5][0] for l in layers]),
    )
    result = {key: {} for key in ("grad", "delta", "m", "v")}
    for k, g in big.items():
        result["grad"][k] = g
        result["delta"][k], result["m"][k], result["v"][k] = _adam(weights[k], g, mom1[k], mom2[k], f"adam_{k}")

    layer_like = {k: weights[k][0] for k in _SMALL}
    packed_g = jnp.concatenate([_pack(small_grads[l]) for l in layers], axis=0)
    rows_per_layer = packed_g.shape[0] // depth
    small_buf = _place_shard(packed_g[None], dev, F32, "place_small_grads")
    gathered_small = _all_gather([small_buf], "all_gather_small_grads")[0][0]
    packed = [jnp.concatenate([_pack({k: src[k][l] for k in _SMALL}) for l in layers], axis=0)
              for src in (weights, mom1, mom2)]
    small = _small_reduce_adam(gathered_small, *packed, "small_reduce_adam")
    for key, rows in zip(("grad", "delta", "m", "v"), small):
        per_layer = [_unpack(rows[l * rows_per_layer:(l + 1) * rows_per_layer], layer_like) for l in layers]
        for k in _SMALL:
            result[key][k] = jnp.stack([per_layer[l][k] for l in layers])

    return (loss, grad_x, *[result["grad"][k] for k in names], *[result["delta"][k] for k in names],
            *[result["m"][k] for k in names], *[result["v"][k] for k in names])
```

```python
import functools
import math

import jax
import jax.numpy as jnp
from jax import lax
from jax.experimental import pallas as pl
from jax.experimental.pallas import tpu as pltpu

F32 = jnp.float32
BF16 = jnp.bfloat16
MESH = pl.DeviceIdType.MESH
ANY = pl.BlockSpec(memory_space=pl.ANY)

N_DEV = 8
HEAD_DIM = 64
Q_PER_KV = 4
BLOCK = 128
LANES = 128
ROPE_THETA = 10000.0
EPS = 1e-6
ADAM_LR = 0.001
ADAM_B1 = 0.9
ADAM_B2 = 0.999
ADAM_EPS = 1e-08
ADAM_WD = 0.01
ADAM_STEP = 10
NEG = -1e30
VMEM_LIMIT_BYTES = 56 * 1024 * 1024

NN = ((1,), (0,))
NT = ((1,), (1,))
TN = ((0,), (0,))


def _dot(a, b, dims):
    return lax.dot_general(a, b, (dims, ((), ())), preferred_element_type=F32)


def _params(*sem):
    return pltpu.CompilerParams(dimension_semantics=sem, vmem_limit_bytes=VMEM_LIMIT_BYTES)


def _divisor_tile(n, limit, unit):
    if n <= limit:
        return n
    best = unit
    for t in range(unit, limit + 1, unit):
        if n % t == 0:
            best = t
    assert n % best == 0, (n, limit, unit)
    return best


def _mm(a, b, mode, out_dtype, name, residual=None, after=None):
    parts = a.shape[0] if a.ndim == 3 else 1
    a2 = a.shape[-2:]
    if mode == "nn":
        (m, kp), (k2, n) = a2, b.shape
        k, mp = kp * parts, m
    elif mode == "nt":
        (m, kp), (n, k2) = a2, b.shape
        k, mp = kp * parts, m
    else:
        (k, mp), (k2, n) = a2, b.shape
        m, kp = mp * parts, k
    assert k == k2, (name, a.shape, b.shape)
    tm = _divisor_tile(mp, 512 if mode == "tn" else 1024, 128)
    tn = _divisor_tile(n, 2048 if mode == "tn" else 512, 128)
    tk = _divisor_tile(kp, 2816, 128)
    nk = k // tk
    kpb, mpb = kp // tk, mp // tm
    dims = {"nn": NN, "nt": NT, "tn": TN}[mode]
    lead = (None,) if a.ndim == 3 else ()
    if mode == "tn":
        a_index = lambda i, j, kk: (i // mpb, kk, i % mpb) if lead else (kk, i)
        a_spec = pl.BlockSpec(lead + (tk, tm), a_index)
    else:
        a_index = lambda i, j, kk: (kk // kpb, i, kk % kpb) if lead else (i, kk)
        a_spec = pl.BlockSpec(lead + (tm, tk), a_index)
    if mode == "nt":
        b_spec = pl.BlockSpec((tn, tk), lambda i, j, kk: (j, kk))
    else:
        b_spec = pl.BlockSpec((tk, tn), lambda i, j, kk: (kk, j))
    o_spec = pl.BlockSpec((tm, tn), lambda i, j, kk: (i, j))
    has_res = residual is not None

    def body(*refs):
        a_ref, b_ref = refs[:2]
        r_ref = refs[2] if has_res else None
        o_ref, acc_ref = refs[-2:]
        kk = pl.program_id(2)
        p = _dot(a_ref[...], b_ref[...], dims)

        def finish(total):
            if has_res:
                total = total + r_ref[...]
            o_ref[...] = total.astype(o_ref.dtype)

        if nk == 1:
            finish(p)
        else:
            @pl.when(kk == 0)
            def _():
                acc_ref[...] = p

            @pl.when(jnp.logical_and(kk > 0, kk < nk - 1))
            def _():
                acc_ref[...] += p

            @pl.when(kk == nk - 1)
            def _():
                finish(acc_ref[...] + p)

    in_specs = [a_spec, b_spec] + ([o_spec] if has_res else []) + ([ANY] if after is not None else [])
    args = (a, b) + ((residual,) if has_res else ()) + ((after,) if after is not None else ())
    acc_shape = (tm, tn) if nk > 1 else (8, LANES)
    return pl.pallas_call(
        body,
        name=name,
        grid=(m // tm, n // tn, nk),
        in_specs=in_specs,
        out_specs=o_spec,
        out_shape=jax.ShapeDtypeStruct((m, n), out_dtype),
        scratch_shapes=[pltpu.VMEM(acc_shape, F32)],
        compiler_params=_params("parallel", "parallel", "arbitrary"),
    )(*args)


def _rmsnorm_fwd(x, g, name):
    t, d = x.shape
    tr = _divisor_tile(t, 256, 8)

    def body(x_ref, g_ref, h_ref):
        xv = x_ref[...]
        rstd = lax.rsqrt(jnp.mean(xv * xv, axis=-1, keepdims=True) + EPS)
        h_ref[...] = (xv * rstd * g_ref[...]).astype(h_ref.dtype)

    return pl.pallas_call(
        body,
        name=name,
        grid=(t // tr,),
        in_specs=[pl.BlockSpec((tr, d), lambda i: (i, 0)), pl.BlockSpec((1, d), lambda i: (0, 0))],
        out_specs=pl.BlockSpec((tr, d), lambda i: (i, 0)),
        out_shape=jax.ShapeDtypeStruct((t, d), BF16),
        compiler_params=_params("parallel"),
    )(x, g)


def _rmsnorm_bwd(x, g, dh, dres, name):
    t, d = x.shape
    tr = _divisor_tile(t, 256, 8)

    def body(x_ref, g_ref, dh_ref, dres_ref, dx_ref, dg_ref):
        i = pl.program_id(0)
        xv = x_ref[...]
        rstd = lax.rsqrt(jnp.mean(xv * xv, axis=-1, keepdims=True) + EPS)
        xh = xv * rstd
        dhv = dh_ref[...]
        dxh = dhv * g_ref[...]
        dx_ref[...] = dres_ref[...] + rstd * (dxh - xh * jnp.mean(dxh * xh, axis=-1, keepdims=True))
        part = jnp.broadcast_to(jnp.sum(dhv * xh, axis=0, keepdims=True), dg_ref.shape)

        @pl.when(i == 0)
        def _():
            dg_ref[...] = part

        @pl.when(i > 0)
        def _():
            dg_ref[...] += part

    row = pl.BlockSpec((tr, d), lambda i: (i, 0))
    return pl.pallas_call(
        body,
        name=name,
        grid=(t // tr,),
        in_specs=[row, pl.BlockSpec((1, d), lambda i: (0, 0)), row, row],
        out_specs=[row, pl.BlockSpec((8, d), lambda i: (0, 0))],
        out_shape=[jax.ShapeDtypeStruct((t, d), F32), jax.ShapeDtypeStruct((8, d), F32)],
        compiler_params=_params("arbitrary"),
    )(x, g, dh, dres)


def _lane(shape):
    return lax.broadcasted_iota(jnp.int32, shape, 1)


def _group_sum64(s):
    row = lax.broadcasted_iota(jnp.int32, (LANES, LANES), 0)
    col = lax.broadcasted_iota(jnp.int32, (LANES, LANES), 1)
    ones = jnp.where((row >= HEAD_DIM) == (col >= HEAD_DIM), 1.0, 0.0).astype(BF16)
    out = []
    for t in range(s.shape[1] // LANES):
        piece = s[:, LANES * t:LANES * t + LANES]
        hi = piece.astype(BF16)
        lo = (piece - hi.astype(F32)).astype(BF16)
        out.append(_dot(hi, ones, NN) + _dot(lo, ones, NN))
    return out[0] if len(out) == 1 else jnp.concatenate(out, axis=1)


def _swap32(x):
    w = x.shape[1]
    return jnp.where((_lane(x.shape) & 32) == 0, pltpu.roll(x, w - 32, axis=1), pltpu.roll(x, 32, axis=1))


def _rope(x, c, s):
    return x * c + _swap32(x) * s


def _rope_t(dy, c, s):
    return dy * c + _swap32(dy * s)


def _head_norm(x):
    rstd = lax.rsqrt(_group_sum64(x * x) * (1.0 / HEAD_DIM) + EPS)
    return x * rstd, rstd


def _head_norm_bwd(dxh, xh, rstd):
    return rstd * (dxh - xh * (_group_sum64(dxh * xh) * (1.0 / HEAD_DIM)))


def _roll64(x):
    return pltpu.roll(x, 64, axis=1)


def _attn_specs(wq, wk):
    kb = wq // wk
    prev = lambda i: jnp.maximum(i - 1, 0)
    return dict(
        q=pl.BlockSpec((BLOCK, wq), lambda i: (i, 0)),
        kc=pl.BlockSpec((BLOCK, wk), lambda i: (i, kb)),
        kp=pl.BlockSpec((BLOCK, wk), lambda i: (prev(i), kb)),
        vc=pl.BlockSpec((BLOCK, wk), lambda i: (i, kb + 1)),
        vp=pl.BlockSpec((BLOCK, wk), lambda i: (prev(i), kb + 1)),
        tq=pl.BlockSpec((BLOCK, wq), lambda i: (i, 0)),
        tkp=pl.BlockSpec((BLOCK, wk), lambda i: (prev(i), 0)),
        gq=pl.BlockSpec((1, wq), lambda i: (0, 0)),
        gk=pl.BlockSpec((1, wk), lambda i: (0, 0)),
        sinks=pl.BlockSpec(memory_space=pltpu.SMEM),
    )


def _attn_prologue(i, q_ref, kc_ref, kp_ref, cq_ref, sq_ref, ckp_ref, skp_ref, gq_ref, gk_ref):
    wk = kc_ref.shape[1]
    cq, sq = cq_ref[...], sq_ref[...]
    ck, sk = cq[:, :wk], sq[:, :wk]
    qh, q_rstd = _head_norm(q_ref[...])
    kch, kc_rstd = _head_norm(kc_ref[...])
    kph, kp_rstd = _head_norm(kp_ref[...])
    qn = _rope(qh * gq_ref[...], cq, sq)
    knc = _rope(kch * gk_ref[...], ck, sk)
    knp = _rope(kph * gk_ref[...], ckp_ref[...], skp_ref[...])
    row = lax.broadcasted_iota(jnp.int32, (BLOCK, BLOCK), 0)
    col = lax.broadcasted_iota(jnp.int32, (BLOCK, BLOCK), 1)
    mask_c = col <= row
    mask_p = jnp.logical_and(col > row, i > 0)
    half = (col >= 64).astype(jnp.int32)
    return dict(cq=cq, sq=sq, ck=ck, sk=sk, qh=qh, q_rstd=q_rstd, kch=kch, kc_rstd=kc_rstd, kph=kph,
                kp_rstd=kp_rstd, qn=qn, knc=knc, knp=knp, mask_c=mask_c, mask_p=mask_p, half=half)


def _head_scores(st, t, e, sink, scale):
    g = (2 * t) // Q_PER_KV
    ks, kpar = g // 2, g % 2
    sl = slice(LANES * ks, LANES * ks + LANES)
    mine = st["half"] == e
    qm = jnp.where(mine, st["qn"][:, LANES * t:LANES * t + LANES], 0.0).astype(BF16)
    kc, kp = st["knc"][:, sl], st["knp"][:, sl]
    flip = e != kpar
    if flip:
        kc, kp = _roll64(kc), _roll64(kp)
    kc, kp = kc.astype(BF16), kp.astype(BF16)
    s_c = jnp.where(st["mask_c"], _dot(qm, kc, NT) * scale, NEG)
    s_p = jnp.where(st["mask_p"], _dot(qm, kp, NT) * scale, NEG)
    m = jnp.maximum(jnp.maximum(jnp.max(s_c, axis=1, keepdims=True), jnp.max(s_p, axis=1, keepdims=True)), sink)
    p_c, p_p = jnp.exp(s_c - m), jnp.exp(s_p - m)
    p_s = jnp.exp(sink - m)
    inv = 1.0 / (jnp.sum(p_c, axis=1, keepdims=True) + jnp.sum(p_p, axis=1, keepdims=True) + p_s)
    return dict(sl=sl, mine=mine, flip=flip, qm=qm, kc=kc, kp=kp, pr_c=p_c * inv, pr_p=p_p * inv, pr_s=p_s * inv)


def _attn_fwd(proj, tables, gq, gk, sinks, wq, wk, name):
    t = proj.shape[0]
    nb = t // BLOCK
    sp = _attn_specs(wq, wk)
    scale = HEAD_DIM ** -0.5
    cos_t, sin_t = tables

    def body(sinks_ref, q_ref, kc_ref, kp_ref, vc_ref, vp_ref, cq_ref, sq_ref, ckp_ref, skp_ref, gq_ref, gk_ref,
             o_ref):
        i = pl.program_id(0)
        st = _attn_prologue(i, q_ref, kc_ref, kp_ref, cq_ref, sq_ref, ckp_ref, skp_ref, gq_ref, gk_ref)
        vc_all, vp_all = vc_ref[...], vp_ref[...]
        for ts in range(wq // LANES):
            acc = jnp.zeros((BLOCK, LANES), F32)
            for e in (0, 1):
                hs = _head_scores(st, ts, e, sinks_ref[2 * ts + e], scale)
                vc, vp = vc_all[:, hs["sl"]], vp_all[:, hs["sl"]]
                if hs["flip"]:
                    vc, vp = _roll64(vc), _roll64(vp)
                vc = jnp.where(hs["mine"], vc, 0.0).astype(BF16)
                vp = jnp.where(hs["mine"], vp, 0.0).astype(BF16)
                acc = acc + _dot(hs["pr_c"].astype(BF16), vc, NN) + _dot(hs["pr_p"].astype(BF16), vp, NN)
            o_ref[:, LANES * ts:LANES * ts + LANES] = acc.astype(o_ref.dtype)

    return pl.pallas_call(
        body,
        name=name,
        grid=(nb,),
        in_specs=[sp["sinks"], sp["q"], sp["kc"], sp["kp"], sp["vc"], sp["vp"], sp["tq"], sp["tq"], sp["tkp"],
                  sp["tkp"], sp["gq"], sp["gk"]],
        out_specs=pl.BlockSpec((BLOCK, wq), lambda i: (i, 0)),
        out_shape=jax.ShapeDtypeStruct((t, wq), BF16),
        compiler_params=_params("parallel"),
    )(sinks, proj, proj, proj, proj, proj, cos_t, sin_t, cos_t, sin_t, gq, gk)


def _attn_bwd(proj, dout, tables, gq, gk, sinks, wq, wk, name):
    t = proj.shape[0]
    nb = t // BLOCK
    sp = _attn_specs(wq, wk)
    scale = HEAD_DIM ** -0.5
    cos_t, sin_t = tables

    def body(sinks_ref, q_ref, kc_ref, kp_ref, vc_ref, vp_ref, cq_ref, sq_ref, ckp_ref, skp_ref, gq_ref, gk_ref,
             do_ref, dq_ref, dk_ref, dv_ref, dgq_ref, dgk_ref, dsk_ref, dqn_ref, dknc_ref, dknp_ref, dvc_ref,
             dvp_ref):
        i = pl.program_id(0)
        st = _attn_prologue(i, q_ref, kc_ref, kp_ref, cq_ref, sq_ref, ckp_ref, skp_ref, gq_ref, gk_ref)
        vc_all, vp_all = vc_ref[...], vp_ref[...]
        dknc_ref[...] = jnp.zeros_like(dknc_ref)
        dknp_ref[...] = jnp.zeros_like(dknp_ref)
        dvc_ref[...] = jnp.zeros_like(dvc_ref)
        dvp_ref[...] = jnp.zeros_like(dvp_ref)
        lane8 = _lane((8, LANES))
        dsinks = jnp.zeros((8, LANES), F32)
        for ts in range(wq // LANES):
            dq_acc = jnp.zeros((BLOCK, LANES), F32)
            for e in (0, 1):
                hs = _head_scores(st, ts, e, sinks_ref[2 * ts + e], scale)
                sl, flip = hs["sl"], hs["flip"]
                vc, vp = vc_all[:, sl], vp_all[:, sl]
                if flip:
                    vc, vp = _roll64(vc), _roll64(vp)
                dom = jnp.where(hs["mine"], do_ref[:, LANES * ts:LANES * ts + LANES], 0.0).astype(BF16)
                dp_c = _dot(dom, vc.astype(BF16), NT)
                dp_p = _dot(dom, vp.astype(BF16), NT)
                pr_c, pr_p = hs["pr_c"], hs["pr_p"]
                rs = jnp.sum(pr_c * dp_c, axis=1, keepdims=True) + jnp.sum(pr_p * dp_p, axis=1, keepdims=True)
                ds_c = (pr_c * (dp_c - rs) * scale)
                ds_p = (pr_p * (dp_p - rs) * scale)
                dsink = jnp.sum(-hs["pr_s"] * rs)
                dsinks = dsinks + jnp.where(lane8 == 2 * ts + e, dsink, 0.0)
                dq_acc = dq_acc + jnp.where(
                    hs["mine"], _dot(ds_c.astype(BF16), hs["kc"], NN) + _dot(ds_p.astype(BF16), hs["kp"], NN), 0.0)
                dv_c = _dot(pr_c.T.astype(BF16), dom, NN)
                dv_p = _dot(pr_p.T.astype(BF16), dom, NN)
                dk_c = _dot(ds_c.T.astype(BF16), hs["qm"], NN)
                dk_p = _dot(ds_p.T.astype(BF16), hs["qm"], NN)
                if flip:
                    dv_c, dv_p, dk_c, dk_p = _roll64(dv_c), _roll64(dv_p), _roll64(dk_c), _roll64(dk_p)
                dvc_ref[:, sl] += dv_c
                dvp_ref[:, sl] += dv_p
                dknc_ref[:, sl] += dk_c
                dknp_ref[:, sl] += dk_p
            dqn_ref[:, LANES * ts:LANES * ts + LANES] = dq_acc

        gqv, gkv = gq_ref[...], gk_ref[...]
        dqg = _rope_t(dqn_ref[...], st["cq"], st["sq"])
        dq_ref[...] = _head_norm_bwd(dqg * gqv, st["qh"], st["q_rstd"]).astype(dq_ref.dtype)
        dkcg = _rope_t(dknc_ref[...], st["ck"], st["sk"])
        dkpg = _rope_t(dknp_ref[...], ckp_ref[...], skp_ref[...])
        dk_cur = _head_norm_bwd(dkcg * gkv, st["kch"], st["kc_rstd"])
        dk_prev = _head_norm_bwd(dkpg * gkv, st["kph"], st["kp_rstd"])
        dgq_part = jnp.broadcast_to(jnp.sum(dqg * st["qh"], axis=0, keepdims=True), dgq_ref.shape)
        dgk_part = jnp.broadcast_to(
            jnp.sum(dkcg * st["kch"] + dkpg * st["kph"], axis=0, keepdims=True), dgk_ref.shape)
        cur = pl.ds(pl.multiple_of(i * BLOCK, BLOCK), BLOCK)
        dk_ref[cur, :] = dk_cur
        dv_ref[cur, :] = dvc_ref[...]

        @pl.when(i == 0)
        def _():
            dgq_ref[...] = dgq_part
            dgk_ref[...] = dgk_part
            dsk_ref[...] = dsinks

        @pl.when(i > 0)
        def _():
            before = pl.ds(pl.multiple_of((i - 1) * BLOCK, BLOCK), BLOCK)
            dk_ref[before, :] += dk_prev
            dv_ref[before, :] += dvp_ref[...]
            dgq_ref[...] += dgq_part
            dgk_ref[...] += dgk_part
            dsk_ref[...] += dsinks

    whole = lambda shape: pl.BlockSpec(shape, lambda i: (0, 0))
    return pl.pallas_call(
        body,
        name=name,
        grid=(nb,),
        in_specs=[sp["sinks"], sp["q"], sp["kc"], sp["kp"], sp["vc"], sp["vp"], sp["tq"], sp["tq"], sp["tkp"],
                  sp["tkp"], sp["gq"], sp["gk"], pl.BlockSpec((BLOCK, wq), lambda i: (i, 0))],
        out_specs=[pl.BlockSpec((BLOCK, wq), lambda i: (i, 0)), whole((t, wk)), whole((t, wk)), whole((8, wq)),
                   whole((8, wk)), whole((8, LANES))],
        out_shape=[jax.ShapeDtypeStruct((t, wq), BF16), jax.ShapeDtypeStruct((t, wk), F32),
                   jax.ShapeDtypeStruct((t, wk), F32), jax.ShapeDtypeStruct((8, wq), F32),
                   jax.ShapeDtypeStruct((8, wk), F32), jax.ShapeDtypeStruct((8, LANES), F32)],
        scratch_shapes=[pltpu.VMEM((BLOCK, wq), F32), pltpu.VMEM((BLOCK, wk), F32), pltpu.VMEM((BLOCK, wk), F32),
                        pltpu.VMEM((BLOCK, wk), F32), pltpu.VMEM((BLOCK, wk), F32)],
        compiler_params=_params("arbitrary"),
    )(sinks, proj, proj, proj, proj, proj, cos_t, sin_t, cos_t, sin_t, gq, gk, dout)


_GELU_K = math.sqrt(2.0 / math.pi)
_GELU_A = 0.044715


def _gelu(x):
    return 0.5 * x * (1.0 + jnp.tanh(_GELU_K * (x + _GELU_A * x * x * x)))


def _gelu_grad(x):
    th = jnp.tanh(_GELU_K * (x + _GELU_A * x * x * x))
    return 0.5 * (1.0 + th) + 0.5 * x * (1.0 - th * th) * (_GELU_K * (1.0 + 3.0 * _GELU_A * x * x))


def _group_ln(v):
    mu = jnp.mean(v, axis=1, keepdims=True)
    cen = v - mu
    rstd = lax.rsqrt(jnp.mean(cen * cen, axis=1, keepdims=True) + EPS)
    return cen * rstd, rstd


def _sgu_geometry(off_u, ws):
    cw = math.gcd(off_u, ws)
    return cw, ws // cw, off_u // cw, (off_u + ws) // cw


def _sgu_fwd(proj, ln_g, ln_b, w_s, bt, off_u, ws, name):
    t = proj.shape[0]
    nb = t // BLOCK
    cw, nc, ub, vb = _sgu_geometry(off_u, ws)
    gpc = cw // LANES
    ng = ws // LANES

    def body(u_ref, v_ref, g_ref, b_ref, w_ref, bt_ref, o_ref):
        jc = pl.program_id(0)
        row = lax.broadcasted_iota(jnp.int32, (BLOCK, BLOCK), 0)
        col = lax.broadcasted_iota(jnp.int32, (BLOCK, BLOCK), 1)
        lane_g = _lane((BLOCK, ng))
        for gi in range(gpc):
            sl = slice(LANES * gi, LANES * gi + LANES)
            xh, _ = _group_ln(_gelu(v_ref[:, sl]))
            vn = xh * g_ref[:, sl] + b_ref[:, sl]
            w = jnp.where(row >= col, w_ref[gi], 0.0).astype(BF16)
            bias = jnp.sum(jnp.where(lane_g == jc * gpc + gi, bt_ref[...], 0.0), axis=1, keepdims=True)
            s = _dot(w, vn.astype(BF16), NN) + bias
            o_ref[:, sl] = (_gelu(u_ref[:, sl]) * s).astype(o_ref.dtype)

    return pl.pallas_call(
        body,
        name=name,
        grid=(nc, nb),
        in_specs=[pl.BlockSpec((BLOCK, cw), lambda jc, i: (i, ub + jc)),
                  pl.BlockSpec((BLOCK, cw), lambda jc, i: (i, vb + jc)),
                  pl.BlockSpec((1, cw), lambda jc, i: (0, jc)),
                  pl.BlockSpec((1, cw), lambda jc, i: (0, jc)),
                  pl.BlockSpec((gpc, BLOCK, BLOCK), lambda jc, i: (jc, 0, 0)),
                  pl.BlockSpec((BLOCK, ng), lambda jc, i: (0, 0))],
        out_specs=pl.BlockSpec((BLOCK, cw), lambda jc, i: (i, jc)),
        out_shape=jax.ShapeDtypeStruct((t, ws), BF16),
        compiler_params=_params("parallel", "parallel"),
    )(proj, proj, ln_g, ln_b, w_s, bt)


def _sgu_bwd(proj, dout, ln_g, ln_b, w_s, bt, off_u, ws, name):
    t = proj.shape[0]
    nb = t // BLOCK
    cw, nc, ub, vb = _sgu_geometry(off_u, ws)
    gpc = cw // LANES
    ng = ws // LANES

    def body(u_ref, v_ref, g_ref, b_ref, w_ref, bt_ref, do_ref, du_ref, dv_ref, dg_ref, db_ref, dw_ref, dbs_ref,
             bacc_ref):
        jc = pl.program_id(0)
        i = pl.program_id(1)
        row = lax.broadcasted_iota(jnp.int32, (BLOCK, BLOCK), 0)
        col = lax.broadcasted_iota(jnp.int32, (BLOCK, BLOCK), 1)
        lane_g = _lane((BLOCK, ng))
        tri = row >= col

        @pl.when(i == 0)
        def _():
            dg_ref[...] = jnp.zeros_like(dg_ref)
            db_ref[...] = jnp.zeros_like(db_ref)
            dw_ref[...] = jnp.zeros_like(dw_ref)
            bacc_ref[...] = jnp.zeros_like(bacc_ref)

        for gi in range(gpc):
            sl = slice(LANES * gi, LANES * gi + LANES)
            u_raw, v_raw = u_ref[:, sl], v_ref[:, sl]
            xh, rstd = _group_ln(_gelu(v_raw))
            gam = g_ref[:, sl]
            vn = (xh * gam + b_ref[:, sl]).astype(BF16)
            w = jnp.where(tri, w_ref[gi], 0.0)
            bias = jnp.sum(jnp.where(lane_g == jc * gpc + gi, bt_ref[...], 0.0), axis=1, keepdims=True)
            s = _dot(w.astype(BF16), vn, NN) + bias
            dov = do_ref[:, sl]
            du_ref[:, sl] = (dov * s * _gelu_grad(u_raw)).astype(du_ref.dtype)
            ds = dov * _gelu(u_raw)
            ds16 = ds.astype(BF16)
            dw_ref[gi] += jnp.where(tri, _dot(ds16, vn, NT), 0.0)
            bacc_ref[gi] += ds
            dvn = _dot(w.T.astype(BF16), ds16, NN)
            dg_ref[:, sl] += jnp.broadcast_to(jnp.sum(dvn * xh, axis=0, keepdims=True), (8, LANES))
            db_ref[:, sl] += jnp.broadcast_to(jnp.sum(dvn, axis=0, keepdims=True), (8, LANES))
            dxh = dvn * gam
            dvg = rstd * (dxh - jnp.mean(dxh, axis=1, keepdims=True)
                          - xh * jnp.mean(dxh * xh, axis=1, keepdims=True))
            dv_ref[:, sl] = (dvg * _gelu_grad(v_raw)).astype(dv_ref.dtype)

        @pl.when(i == nb - 1)
        def _():
            for gi in range(gpc):
                dbs_ref[gi] = jnp.broadcast_to(jnp.sum(bacc_ref[gi].T, axis=0, keepdims=True), (8, LANES))

    blk = lambda base: pl.BlockSpec((BLOCK, cw), lambda jc, i: (i, base + jc))
    vec = pl.BlockSpec((1, cw), lambda jc, i: (0, jc))
    acc = pl.BlockSpec((8, cw), lambda jc, i: (0, jc))
    wsp = pl.BlockSpec((gpc, BLOCK, BLOCK), lambda jc, i: (jc, 0, 0))
    return pl.pallas_call(
        body,
        name=name,
        grid=(nc, nb),
        in_specs=[blk(ub), blk(vb), vec, vec, wsp, pl.BlockSpec((BLOCK, ng), lambda jc, i: (0, 0)), blk(0)],
        out_specs=[blk(0), blk(0), acc, acc, wsp, pl.BlockSpec((gpc, 8, LANES), lambda jc, i: (jc, 0, 0))],
        out_shape=[jax.ShapeDtypeStruct((t, ws), BF16), jax.ShapeDtypeStruct((t, ws), BF16),
                   jax.ShapeDtypeStruct((8, ws), F32), jax.ShapeDtypeStruct((8, ws), F32),
                   jax.ShapeDtypeStruct((ng, BLOCK, BLOCK), F32), jax.ShapeDtypeStruct((ng, 8, LANES), F32)],
        scratch_shapes=[pltpu.VMEM((gpc, BLOCK, BLOCK), F32)],
        compiler_params=_params("arbitrary", "arbitrary"),
    )(proj, proj, ln_g, ln_b, w_s, bt, dout)


def _sigmoid(x):
    return 1.0 / (1.0 + jnp.exp(-x))


def _merge_geometry(off_g, d):
    cw = math.gcd(off_g, d)
    return cw, d // cw, off_g // cw, (off_g + d) // cw


def _merge_fwd(a, b, proj, off_g, name):
    t, d = a.shape
    cw, nc, ab, bb = _merge_geometry(off_g, d)
    tr = _divisor_tile(t, 512, 8)

    def body(a_ref, b_ref, la_ref, lb_ref, o_ref):
        o_ref[...] = (_sigmoid(la_ref[...]) * a_ref[...] + _sigmoid(lb_ref[...]) * b_ref[...]).astype(o_ref.dtype)

    blk = lambda base: pl.BlockSpec((tr, cw), lambda i, j: (i, base + j))
    return pl.pallas_call(
        body,
        name=name,
        grid=(t // tr, nc),
        in_specs=[blk(0), blk(0), blk(ab), blk(bb)],
        out_specs=blk(0),
        out_shape=jax.ShapeDtypeStruct((t, d), BF16),
        compiler_params=_params("parallel", "parallel"),
    )(a, b, proj, proj)


def _merge_bwd(a, b, proj, dm, off_g, name):
    t, d = a.shape
    cw, nc, ab, bb = _merge_geometry(off_g, d)
    tr = _divisor_tile(t, 512, 8)

    def body(a_ref, b_ref, la_ref, lb_ref, dm_ref, da_ref, db_ref, dla_ref, dlb_ref):
        dmv = dm_ref[...]
        ga, gb = _sigmoid(la_ref[...]), _sigmoid(lb_ref[...])
        da_ref[...] = (dmv * ga).astype(da_ref.dtype)
        db_ref[...] = (dmv * gb).astype(db_ref.dtype)
        dla_ref[...] = (dmv * a_ref[...] * ga * (1.0 - ga)).astype(dla_ref.dtype)
        dlb_ref[...] = (dmv * b_ref[...] * gb * (1.0 - gb)).astype(dlb_ref.dtype)

    blk = lambda base: pl.BlockSpec((tr, cw), lambda i, j: (i, base + j))
    return pl.pallas_call(
        body,
        name=name,
        grid=(t // tr, nc),
        in_specs=[blk(0), blk(0), blk(ab), blk(bb), blk(0)],
        out_specs=[blk(0)] * 4,
        out_shape=[jax.ShapeDtypeStruct((t, d), BF16)] * 4,
        compiler_params=_params("parallel", "parallel"),
    )(a, b, proj, proj, dm)


def _swiglu_fwd(gu, name):
    t, f2 = gu.shape
    f = f2 // 2
    cw = _divisor_tile(f, 1536, 128)
    nc = f // cw
    tr = _divisor_tile(t, 256, 8)

    def body(g_ref, u_ref, o_ref):
        gv = g_ref[...]
        o_ref[...] = (gv * _sigmoid(gv) * u_ref[...]).astype(o_ref.dtype)

    blk = lambda base: pl.BlockSpec((tr, cw), lambda i, j: (i, base + j))
    return pl.pallas_call(
        body,
        name=name,
        grid=(t // tr, nc),
        in_specs=[blk(0), blk(nc)],
        out_specs=blk(0),
        out_shape=jax.ShapeDtypeStruct((t, f), BF16),
        compiler_params=_params("parallel", "parallel"),
    )(gu, gu)


def _swiglu_bwd(gu, dact, name):
    t, f2 = gu.shape
    f = f2 // 2
    cw = _divisor_tile(f, 1536, 128)
    nc = f // cw
    tr = _divisor_tile(t, 256, 8)

    def body(g_ref, u_ref, da_ref, o_ref):
        gv, dav = g_ref[...], da_ref[...]
        sg = _sigmoid(gv)
        o_ref[0] = (dav * u_ref[...] * (sg + gv * sg * (1.0 - sg))).astype(o_ref.dtype)
        o_ref[1] = (dav * gv * sg).astype(o_ref.dtype)

    blk = lambda base: pl.BlockSpec((tr, cw), lambda i, j: (i, base + j))
    return pl.pallas_call(
        body,
        name=name,
        grid=(t // tr, nc),
        in_specs=[blk(0), blk(nc), blk(0)],
        out_specs=pl.BlockSpec((2, tr, cw), lambda i, j: (0, i, j)),
        out_shape=jax.ShapeDtypeStruct((2, t, f), BF16),
        compiler_params=_params("parallel", "parallel"),
    )(gu, gu, dact)


def _loss_and_grad(y, target, name):
    t, d = y.shape
    tr = _divisor_tile(t, 256, 8)

    def body(y_ref, t_ref, l_ref, dy_ref):
        i = pl.program_id(0)
        err = y_ref[...] - t_ref[...]
        dy_ref[...] = err * (1.0 / d)
        part = jnp.broadcast_to(0.5 * jnp.sum(err * err) * (1.0 / d), l_ref.shape)

        @pl.when(i == 0)
        def _():
            l_ref[...] = part

        @pl.when(i > 0)
        def _():
            l_ref[...] += part

    row = pl.BlockSpec((tr, d), lambda i: (i, 0))
    return pl.pallas_call(
        body,
        name=name,
        grid=(t // tr,),
        in_specs=[row, row],
        out_specs=[pl.BlockSpec((8, LANES), lambda i: (0, 0)), row],
        out_shape=[jax.ShapeDtypeStruct((8, LANES), F32), jax.ShapeDtypeStruct((t, d), F32)],
        compiler_params=_params("arbitrary"),
    )(y, target)


def _adam_math(w, g, m, v):
    m = ADAM_B1 * m + (1.0 - ADAM_B1) * g
    v = ADAM_B2 * v + (1.0 - ADAM_B2) * (g * g)
    m_hat = m / (1.0 - ADAM_B1 ** ADAM_STEP)
    v_hat = v / (1.0 - ADAM_B2 ** ADAM_STEP)
    delta = -ADAM_LR * (m_hat / (jnp.sqrt(v_hat) + ADAM_EPS) + ADAM_WD * w)
    return delta, m, v


def _row_tile(r, c):
    return _divisor_tile(r, max(8, (256 * 1024) // c // 8 * 8), 8)


def _adam(w, g, m, v, name):
    nl, r, c = w.shape
    tr = _row_tile(r, c)

    def body(w_ref, g_ref, m_ref, v_ref, d_ref, nm_ref, nv_ref):
        d_ref[...], nm_ref[...], nv_ref[...] = _adam_math(w_ref[...], g_ref[...], m_ref[...], v_ref[...])

    row = pl.BlockSpec((None, tr, c), lambda l, i: (l, i, 0))
    return pl.pallas_call(
        body,
        name=name,
        grid=(nl, r // tr),
        in_specs=[row] * 4,
        out_specs=[row] * 3,
        out_shape=[jax.ShapeDtypeStruct((nl, r, c), F32)] * 3,
        compiler_params=_params("parallel", "parallel"),
    )(w, g, m, v)


def _place_shard(x, dev, out_dtype, name, after=None):
    p, r, c = x.shape
    tr = _row_tile(r, c)

    def body(dev_ref, x_ref, *rest):
        o_ref = rest[-1]
        o_ref[...] = x_ref[...].astype(o_ref.dtype)

    return pl.pallas_call(
        body,
        name=name,
        grid_spec=pltpu.PrefetchScalarGridSpec(
            num_scalar_prefetch=1,
            grid=(p, r // tr),
            in_specs=[pl.BlockSpec((None, tr, c), lambda pi, i, dev_ref: (pi, i, 0))]
            + ([ANY] if after is not None else []),
            out_specs=pl.BlockSpec((None, None, tr, c), lambda pi, i, dev_ref: (pi, dev_ref[0], i, 0)),
        ),
        out_shape=jax.ShapeDtypeStruct((p, N_DEV, r, c), out_dtype),
        compiler_params=_params("parallel", "parallel"),
    )(dev, x, *(() if after is None else (after,)))


def _sum_sibling(g, land, core, name):
    p, _, _, r, c = g.shape
    tr = _row_tile(r, c)

    def body(core_ref, g_ref, l_ref, o_ref):
        o_ref[...] = (g_ref[...].astype(F32) + l_ref[...].astype(F32)).astype(o_ref.dtype)

    return pl.pallas_call(
        body,
        name=name,
        grid_spec=pltpu.PrefetchScalarGridSpec(
            num_scalar_prefetch=1,
            grid=(p, 4, r // tr),
            in_specs=[pl.BlockSpec((None, None, None, tr, c), lambda pi, q, i, core_ref: (pi, q, core_ref[0], i, 0)),
                      pl.BlockSpec((None, None, None, tr, c), lambda pi, q, i, core_ref: (pi, q, 0, i, 0))],
            out_specs=pl.BlockSpec((None, None, tr, c), lambda pi, q, i, core_ref: (pi, q, i, 0)),
        ),
        out_shape=jax.ShapeDtypeStruct((p, 4, r, c), BF16),
        compiler_params=_params("parallel", "parallel", "parallel"),
    )(core, g, land)


def _sum_chips(s, lands, chip, name):
    p, _, r, c = s.shape
    tr = _row_tile(r, c)

    def body(chip_ref, s_ref, l0_ref, l1_ref, l2_ref, o_ref):
        total = s_ref[...].astype(F32) + l0_ref[...].astype(F32)
        o_ref[...] = total + l1_ref[...].astype(F32) + l2_ref[...].astype(F32)

    land_spec = pl.BlockSpec((None, None, tr, c), lambda pi, i, chip_ref: (pi, 0, i, 0))
    return pl.pallas_call(
        body,
        name=name,
        grid_spec=pltpu.PrefetchScalarGridSpec(
            num_scalar_prefetch=1,
            grid=(p, r // tr),
            in_specs=[pl.BlockSpec((None, None, tr, c), lambda pi, i, chip_ref: (pi, chip_ref[0], i, 0)),
                      land_spec, land_spec, land_spec],
            out_specs=pl.BlockSpec((None, tr, c), lambda pi, i, chip_ref: (pi, i, 0)),
        ),
        out_shape=jax.ShapeDtypeStruct((p, r, c), F32),
        compiler_params=_params("parallel", "parallel"),
    )(chip, s, *lands)


def _small_reduce_adam(gathered, w, m, v, name):
    _, r, c = gathered.shape
    tr = _row_tile(r, c)

    def body(p_ref, w_ref, m_ref, v_ref, g_ref, d_ref, nm_ref, nv_ref):
        g = p_ref[0]
        for j in range(1, N_DEV):
            g = g + p_ref[j]
        g_ref[...] = g
        d_ref[...], nm_ref[...], nv_ref[...] = _adam_math(w_ref[...], g, m_ref[...], v_ref[...])

    row = pl.BlockSpec((tr, c), lambda i: (i, 0))
    return pl.pallas_call(
        body,
        name=name,
        grid=(r // tr,),
        in_specs=[pl.BlockSpec((N_DEV, tr, c), lambda i: (0, i, 0)), row, row, row],
        out_specs=[row] * 4,
        out_shape=[jax.ShapeDtypeStruct((r, c), F32)] * 4,
        compiler_params=_params("parallel"),
    )(gathered, w, m, v)


def _place():
    return lax.axis_index("x"), lax.axis_index("y"), lax.axis_index("c")


def _all_gather(bufs, name):
    n = len(bufs)

    def body(*refs):
        outs = refs[n:2 * n]
        send_sems, recv_sems = refs[2 * n:]
        x, y, c = _place()
        me, sibling = (x, y, c), (x, y, 1 - c)
        chips = [(1 - x, y), (x, 1 - y), (1 - x, 1 - y)]

        def block(a, px, py, pc):
            return outs[a].at[:, pl.ds(4 * px + 2 * py + pc, 1)]

        def copy(a, k, blk, to):
            return pltpu.make_async_remote_copy(
                src_ref=block(a, *blk), dst_ref=block(a, *blk), send_sem=send_sems.at[a, k],
                recv_sem=recv_sems.at[a, k], device_id=to, device_id_type=MESH)

        first = []
        for a in range(n):
            first.append(copy(a, 0, me, sibling))
            first += [copy(a, 1 + j, me, (*chip, c)) for j, chip in enumerate(chips)]
        for cp in first:
            cp.start()
        passed = []
        for j, chip in enumerate(chips):
            for a in range(n):
                copy(a, 1 + j, (*chip, c), me).wait_recv()
                fwd = copy(a, 4 + j, (*chip, c), sibling)
                fwd.start()
                passed.append(fwd)
        for a in range(n):
            copy(a, 0, sibling, me).wait_recv()
            for j, chip in enumerate(chips):
                copy(a, 4 + j, (*chip, 1 - c), me).wait_recv()
        for cp in first + passed:
            cp.wait_send()

    return pl.pallas_call(
        body,
        name=name,
        in_specs=[ANY] * n,
        out_specs=[ANY] * n,
        out_shape=[jax.ShapeDtypeStruct(b.shape, b.dtype) for b in bufs],
        input_output_aliases={a: a for a in range(n)},
        scratch_shapes=[pltpu.SemaphoreType.DMA((n, 7)), pltpu.SemaphoreType.DMA((n, 7))],
    )(*bufs)


HBM = pl.BlockSpec(memory_space=pltpu.HBM)
SEM = pl.BlockSpec(memory_space=pltpu.SEMAPHORE)
TOKEN = pl.BlockSpec(memory_space=pltpu.VMEM)
EFFECT = pltpu.SideEffectType.DATAFLOW_SIDE_EFFECTING


def _in_hbm(a):
    return pltpu.with_memory_space_constraint(a, pltpu.HBM)


def _gather_start(chunks, name):
    flat = [b for chunk in chunks for b in chunk]
    n, nch = len(flat), len(chunks)

    def body(*refs):
        sems, outs, token = refs[n:n + 2 * nch], refs[n + 2 * nch:2 * n + 2 * nch], refs[2 * n + 2 * nch]
        x, y, c = _place()
        targets = [(x, y, 1 - c), (1 - x, y, c), (x, 1 - y, c), (1 - x, 1 - y, c)]
        a = 0
        for ci, chunk in enumerate(chunks):
            for k in range(len(chunk)):
                mine = outs[a].at[:, pl.ds(4 * x + 2 * y + c, 1)]
                for ti, to in enumerate(targets):
                    pltpu.make_async_remote_copy(
                        src_ref=mine, dst_ref=mine, send_sem=sems[2 * ci].at[4 * k + ti],
                        recv_sem=sems[2 * ci + 1].at[4 * k + ti], device_id=to, device_id_type=MESH).start()
                a += 1
        token[...] = jnp.zeros_like(token)

    sem_shapes = []
    for chunk in chunks:
        sem_shapes += [pltpu.SemaphoreType.DMA((4 * len(chunk),))] * 2
    outs = pl.pallas_call(
        body,
        name=name,
        in_specs=[HBM] * n,
        out_specs=[SEM] * (2 * nch) + [HBM] * n + [TOKEN],
        out_shape=sem_shapes + [pltpu.HBM(b.shape, b.dtype) for b in flat] + [jax.ShapeDtypeStruct((8, LANES), F32)],
        input_output_aliases={i: 2 * nch + i for i in range(n)},
        compiler_params=pltpu.CompilerParams(has_side_effects=EFFECT),
    )(*[_in_hbm(b) for b in flat])
    result, a = [], 2 * nch
    for ci, chunk in enumerate(chunks):
        result.append((outs[2 * ci], outs[2 * ci + 1], list(outs[a:a + len(chunk)])))
        a += len(chunk)
    return result, outs[-1]


def _gather_wait(send_sems, recv_sems, bufs, after, name):
    n = len(bufs)

    def body(*refs):
        ins, ssem, rsem = refs[:n], refs[n], refs[n + 1]
        x, y, c = _place()
        sources = [(x, y, 1 - c), (1 - x, y, c), (x, 1 - y, c), (1 - x, 1 - y, c)]
        for k in range(n):
            for ti, (px, py, pc) in enumerate(sources):
                theirs = ins[k].at[:, pl.ds(4 * px + 2 * py + pc, 1)]
                cp = pltpu.make_async_remote_copy(
                    src_ref=theirs, dst_ref=theirs, send_sem=ssem.at[4 * k + ti], recv_sem=rsem.at[4 * k + ti],
                    device_id=(px, py, pc), device_id_type=MESH)
                cp.wait_send()
                cp.wait_recv()

    return pl.pallas_call(
        body,
        name=name,
        in_specs=[HBM] * n + [SEM, SEM, ANY],
        out_specs=[HBM] * n,
        out_shape=[pltpu.HBM(b.shape, b.dtype) for b in bufs],
        input_output_aliases={i: i for i in range(n)},
        compiler_params=pltpu.CompilerParams(has_side_effects=EFFECT),
    )(*bufs, send_sems, recv_sems, after)


def _gather_forward(bufs, name):
    n = len(bufs)

    def body(*refs):
        outs = refs[n:2 * n]
        send_sems, recv_sems = refs[2 * n:]
        x, y, c = _place()
        chips = [(1 - x, y), (x, 1 - y), (1 - x, 1 - y)]
        copies = []
        for a in range(n):
            for j, (px, py) in enumerate(chips):
                got = outs[a].at[:, pl.ds(4 * px + 2 * py + c, 1)]
                cp = pltpu.make_async_remote_copy(
                    src_ref=got, dst_ref=got, send_sem=send_sems.at[a, j], recv_sem=recv_sems.at[a, j],
                    device_id=(x, y, 1 - c), device_id_type=MESH)
                cp.start()
                copies.append(cp)
        for a in range(n):
            for j, (px, py) in enumerate(chips):
                coming = outs[a].at[:, pl.ds(4 * px + 2 * py + 1 - c, 1)]
                pltpu.make_async_remote_copy(
                    src_ref=coming, dst_ref=coming, send_sem=send_sems.at[a, j], recv_sem=recv_sems.at[a, j],
                    device_id=(x, y, 1 - c), device_id_type=MESH).wait_recv()
        for cp in copies:
            cp.wait_send()

    return pl.pallas_call(
        body,
        name=name,
        in_specs=[ANY] * n,
        out_specs=[ANY] * n,
        out_shape=[jax.ShapeDtypeStruct(b.shape, b.dtype) for b in bufs],
        input_output_aliases={a: a for a in range(n)},
        scratch_shapes=[pltpu.SemaphoreType.DMA((n, 3)), pltpu.SemaphoreType.DMA((n, 3))],
    )(*bufs)


def _chips_start(sums, name):
    n = len(sums)

    def body(*refs):
        ssem, rsem = refs[4 * n], refs[4 * n + 1]
        src, land = refs[4 * n + 2:5 * n + 2], refs[5 * n + 2:8 * n + 2]
        token = refs[8 * n + 2]
        x, y, c = _place()
        chips = [(1 - x, y), (x, 1 - y), (1 - x, 1 - y)]
        for a in range(n):
            for k, (px, py) in enumerate(chips):
                pltpu.make_async_remote_copy(
                    src_ref=src[a].at[:, pl.ds(2 * px + py, 1)], dst_ref=land[3 * a + k], send_sem=ssem.at[3 * a + k],
                    recv_sem=rsem.at[3 * a + k], device_id=(px, py, c), device_id_type=MESH).start()
        token[...] = jnp.zeros_like(token)

    lands = []
    for s in sums:
        lands += [lax.empty((s.shape[0], 1) + s.shape[2:], s.dtype) for _ in range(3)]
    outs = pl.pallas_call(
        body,
        name=name,
        in_specs=[HBM] * (4 * n),
        out_specs=[SEM, SEM] + [HBM] * (4 * n) + [TOKEN],
        out_shape=[pltpu.SemaphoreType.DMA((3 * n,))] * 2 + [pltpu.HBM(b.shape, b.dtype) for b in list(sums) + lands]
        + [jax.ShapeDtypeStruct((8, LANES), F32)],
        input_output_aliases={i: 2 + i for i in range(4 * n)},
        compiler_params=pltpu.CompilerParams(has_side_effects=EFFECT),
    )(*[_in_hbm(b) for b in list(sums) + lands])
    return outs[0], outs[1], list(outs[2:2 + n]), list(outs[2 + n:2 + 4 * n]), outs[-1]


def _chips_wait(send_sems, recv_sems, sums, lands, after, name):
    n = len(sums)

    def body(*refs):
        src, land = refs[:n], refs[n:4 * n]
        ssem, rsem = refs[4 * n], refs[4 * n + 1]
        x, y, c = _place()
        chips = [(1 - x, y), (x, 1 - y), (1 - x, 1 - y)]
        for a in range(n):
            for k, (px, py) in enumerate(chips):
                cp = pltpu.make_async_remote_copy(
                    src_ref=src[a].at[:, pl.ds(2 * px + py, 1)], dst_ref=land[3 * a + k], send_sem=ssem.at[3 * a + k],
                    recv_sem=rsem.at[3 * a + k], device_id=(px, py, c), device_id_type=MESH)
                cp.wait_send()
                cp.wait_recv()

    both = list(sums) + list(lands)
    outs = pl.pallas_call(
        body,
        name=name,
        in_specs=[HBM] * (4 * n) + [SEM, SEM, ANY],
        out_specs=[HBM] * (4 * n),
        out_shape=[pltpu.HBM(b.shape, b.dtype) for b in both],
        input_output_aliases={i: i for i in range(4 * n)},
        compiler_params=pltpu.CompilerParams(has_side_effects=EFFECT),
    )(*both, send_sems, recv_sems, after)
    return list(outs[:n]), [list(outs[n + 3 * a:n + 3 * a + 3]) for a in range(n)]


def _swap_with_sibling(grads, name):
    n = len(grads)

    def body(*refs):
        ins, land = refs[:n], refs[n:2 * n]
        send_sems, recv_sems = refs[2 * n:]
        x, y, c = _place()
        copies = []
        for a in range(n):
            remote = pltpu.make_async_remote_copy(
                src_ref=ins[a].at[:, :, pl.ds(1 - c, 1)], dst_ref=land[a], send_sem=send_sems.at[a],
                recv_sem=recv_sems.at[a], device_id=(x, y, 1 - c), device_id_type=MESH)
            remote.start()
            copies.append(remote)
        for remote in copies:
            remote.wait()

    return pl.pallas_call(
        body,
        name=name,
        in_specs=[ANY] * n,
        out_specs=[ANY] * n,
        out_shape=[jax.ShapeDtypeStruct(g.shape[:2] + (1,) + g.shape[3:], g.dtype) for g in grads],
        scratch_shapes=[pltpu.SemaphoreType.DMA((n,)), pltpu.SemaphoreType.DMA((n,))],
    )(*grads)


_SMALL = ("mix_norm", "q_norm", "k_norm", "sinks", "sgu_ln_g", "sgu_ln_b", "w_spatial", "b_spatial", "ffn_norm")


def _pack_rows(a):
    flat = a.reshape(-1)
    pad = (-flat.shape[0]) % LANES
    if pad:
        flat = jnp.pad(flat, (0, pad))
    return flat.reshape(-1, LANES)


def _pack(values):
    rows = jnp.concatenate([_pack_rows(values[k]) for k in _SMALL], axis=0)
    pad = (-rows.shape[0]) % 8
    if pad:
        rows = jnp.pad(rows, ((0, pad), (0, 0)))
    return rows


def _unpack(rows, like):
    out, at = {}, 0
    for k in _SMALL:
        size = like[k].size
        nrows = -(-size // LANES)
        out[k] = rows[at:at + nrows].reshape(-1)[:size].reshape(like[k].shape)
        at += nrows
    return out


def _rope_tables(t, wq):
    pos = jnp.arange(t, dtype=F32)
    inv_freq = jnp.power(ROPE_THETA, -jnp.arange(0, HEAD_DIM, 2, dtype=F32) / HEAD_DIM)
    ang = pos[:, None] * inv_freq[None, :]
    cos, sin = jnp.cos(ang), jnp.sin(ang)
    reps = wq // HEAD_DIM
    return (jnp.tile(jnp.concatenate([cos, cos], axis=1), (1, reps)),
            jnp.tile(jnp.concatenate([-sin, sin], axis=1), (1, reps)))


def kernel(x, mix_norm, w_in, q_norm, k_norm, sinks, sgu_ln_g, sgu_ln_b, w_spatial, b_spatial, w_attn_branch, w_sgu_branch, w_out, ffn_norm, w_gate, w_up, w_down, loss_target, m_mix_norm, m_w_in, m_q_norm, m_k_norm, m_sinks, m_sgu_ln_g, m_sgu_ln_b, m_w_spatial, m_b_spatial, m_w_attn_branch, m_w_sgu_branch, m_w_out, m_ffn_norm, m_w_gate, m_w_up, m_w_down, v_mix_norm, v_w_in, v_q_norm, v_k_norm, v_sinks, v_sgu_ln_g, v_sgu_ln_b, v_w_spatial, v_b_spatial, v_w_attn_branch, v_w_sgu_branch, v_w_out, v_ffn_norm, v_w_gate, v_w_up, v_w_down):
    names = ("mix_norm", "w_in", "q_norm", "k_norm", "sinks", "sgu_ln_g", "sgu_ln_b", "w_spatial", "b_spatial",
             "w_attn_branch", "w_sgu_branch", "w_out", "ffn_norm", "w_gate", "w_up", "w_down")
    weights = dict(zip(names, (mix_norm, w_in, q_norm, k_norm, sinks, sgu_ln_g, sgu_ln_b, w_spatial, b_spatial,
                               w_attn_branch, w_sgu_branch, w_out, ffn_norm, w_gate, w_up, w_down)))
    mom1 = dict(zip(names, (m_mix_norm, m_w_in, m_q_norm, m_k_norm, m_sinks, m_sgu_ln_g, m_sgu_ln_b, m_w_spatial,
                            m_b_spatial, m_w_attn_branch, m_w_sgu_branch, m_w_out, m_ffn_norm, m_w_gate, m_w_up,
                            m_w_down)))
    mom2 = dict(zip(names, (v_mix_norm, v_w_in, v_q_norm, v_k_norm, v_sinks, v_sgu_ln_g, v_sgu_ln_b, v_w_spatial,
                            v_b_spatial, v_w_attn_branch, v_w_sgu_branch, v_w_out, v_ffn_norm, v_w_gate, v_w_up,
                            v_w_down)))
    depth = w_in.shape[0]
    _, t, d = x.shape
    n_q_heads = sinks.shape[1]
    wq = n_q_heads * HEAD_DIM
    wk = wq // Q_PER_KV
    ws = sgu_ln_g.shape[1]
    ng = ws // LANES
    off_u = wq + 2 * wk
    off_g = off_u + 2 * ws
    tables = _rope_tables(t, wq)
    px, py, pc = _place()
    core = pc.astype(jnp.int32)[None]
    chip = (2 * px + py).astype(jnp.int32)[None]
    dev = (4 * px + 2 * py + pc).astype(jnp.int32)[None]

    layers = range(depth)
    chunks = ((0,), (1, 2, 3), (4,), (5,))
    pending, token = [], None
    for l in layers:
        shards = [
            w_in[l].T[None],
            w_attn_branch[l].T[None],
            w_sgu_branch[l].T[None],
            w_out[l][None],
            jnp.stack([w_gate[l].T, w_up[l].T]),
            w_down[l][None],
        ]
        bufs = [_place_shard(s, dev, BF16, f"place_shard_{l}_{a}", after=token if a == 0 else None)
                for a, s in enumerate(shards)]
        started, token = _gather_start([[bufs[a] for a in chunk] for chunk in chunks], f"gather_start_{l}")
        pending.append(started)

    def gathered(l, ci, after):
        send_sems, recv_sems, bufs = pending[l][ci]
        bufs = _gather_wait(send_sems, recv_sems, bufs, after, f"gather_wait_{l}_{ci}")
        bufs = _gather_forward(bufs, f"gather_forward_{l}_{ci}")
        return [f.reshape(f.shape[0] * f.shape[1] * f.shape[2], f.shape[3]) for f in bufs]

    saved = []
    xl = x[0]
    for l in layers:
        gq = jnp.tile(q_norm[l], n_q_heads)[None]
        gk = jnp.tile(k_norm[l], n_q_heads // Q_PER_KV)[None]
        bt = b_spatial[l].T
        h = _rmsnorm_fwd(xl, mix_norm[l][None], f"mix_norm_fwd_{l}")
        (win_t,) = gathered(l, 0, token if l == 0 else h)
        proj = _mm(h, win_t, "nt", F32, f"in_proj_{l}")
        attn = _attn_fwd(proj, tables, gq, gk, sinks[l], wq, wk, f"attn_fwd_{l}")
        sgu = _sgu_fwd(proj, sgu_ln_g[l][None], sgu_ln_b[l][None], w_spatial[l], bt, off_u, ws, f"sgu_fwd_{l}")
        wab_t, wsb_t, wo = gathered(l, 1, sgu)
        br_a = _mm(attn, wab_t, "nt", F32, f"attn_branch_{l}")
        br_b = _mm(sgu, wsb_t, "nt", F32, f"sgu_branch_{l}")
        merged = _merge_fwd(br_a, br_b, proj, off_g, f"merge_fwd_{l}")
        x1 = _mm(merged, wo, "nn", F32, f"out_proj_{l}", residual=xl)
        h2 = _rmsnorm_fwd(x1, ffn_norm[l][None], f"ffn_norm_fwd_{l}")
        (wgu_t,) = gathered(l, 2, h2)
        gu = _mm(h2, wgu_t, "nt", F32, f"gate_up_{l}")
        act = _swiglu_fwd(gu, f"swiglu_fwd_{l}")
        (wd,) = gathered(l, 3, act)
        x2 = _mm(act, wd, "nn", F32, f"down_proj_{l}", residual=x1)
        saved.append(dict(x0=xl, h=h, proj=proj, attn=attn, sgu=sgu, br_a=br_a, br_b=br_b, merged=merged, x1=x1,
                          h2=h2, gu=gu, act=act, gq=gq, gk=gk, bt=bt, win_t=win_t, wab_t=wab_t, wsb_t=wsb_t, wo=wo,
                          wgu_t=wgu_t, wd=wd))
        xl = x2

    loss_part, dx = _loss_and_grad(xl, loss_target[0], "loss")
    loss = lax.psum(loss_part[0, 0], ("x", "y", "c"))

    def scatter_start(grads, tag):
        shaped = []
        for g, p in grads:
            rows, c = g.shape
            shaped.append(g.reshape(p, 4, 2, rows // (8 * p), c))
        theirs = _swap_with_sibling(shaped, f"rs_sibling_{tag}")
        sums = [_sum_sibling(g, o, core, f"rs_add_sibling_{tag}_{a}") for a, (g, o) in enumerate(zip(shaped, theirs))]
        send_sems, recv_sems, sums, lands, tok = _chips_start(sums, f"rs_chips_start_{tag}")
        return (send_sems, recv_sems, sums, lands), tok

    def scatter_finish(state, after, tag):
        send_sems, recv_sems, sums, lands = state
        sums, lands = _chips_wait(send_sems, recv_sems, sums, lands, after, f"rs_chips_wait_{tag}")
        return [_sum_chips(s, o, chip, f"rs_add_chips_{tag}_{a}") for a, (s, o) in enumerate(zip(sums, lands))]

    in_flight = [dict() for _ in layers]
    small_grads = [None] * depth
    tok = None
    for l in reversed(layers):
        s = saved[l]
        dx16 = dx.astype(BF16)
        dact = _mm(dx16, s["wd"], "nt", F32, f"d_act_{l}", after=tok)
        g_wd = _mm(s["act"], dx16, "tn", BF16, f"g_w_down_{l}")
        in_flight[l]["down"], tok = scatter_start([(g_wd, 1)], f"{l}_down")
        dgu = _swiglu_bwd(s["gu"], dact, f"swiglu_bwd_{l}")
        dh2 = _mm(dgu, s["wgu_t"], "nn", F32, f"d_h2_{l}", after=tok)
        g_wgu_t = _mm(dgu, s["h2"], "tn", BF16, f"g_w_gate_up_{l}")
        in_flight[l]["gate_up"], tok = scatter_start([(g_wgu_t, 2)], f"{l}_gate_up")
        dx1, g_ffn = _rmsnorm_bwd(s["x1"], ffn_norm[l][None], dh2, dx, f"ffn_norm_bwd_{l}")
        dx1_16 = dx1.astype(BF16)
        dmerged = _mm(dx1_16, s["wo"], "nt", F32, f"d_merged_{l}", after=tok)
        g_wo = _mm(s["merged"], dx1_16, "tn", BF16, f"g_w_out_{l}")
        d_a, d_b, dla, dlb = _merge_bwd(s["br_a"], s["br_b"], s["proj"], dmerged, off_g, f"merge_bwd_{l}")
        dattn = _mm(d_a, s["wab_t"], "nn", F32, f"d_attn_{l}")
        g_wab_t = _mm(d_a, s["attn"], "tn", BF16, f"g_w_attn_branch_{l}")
        dsgu = _mm(d_b, s["wsb_t"], "nn", F32, f"d_sgu_{l}")
        g_wsb_t = _mm(d_b, s["sgu"], "tn", BF16, f"g_w_sgu_branch_{l}")
        in_flight[l]["mix"], tok = scatter_start([(g_wab_t, 1), (g_wsb_t, 1), (g_wo, 1)], f"{l}_mix")
        dq, dk, dv, g_gq, g_gk, g_sinks = _attn_bwd(s["proj"], dattn, tables, s["gq"], s["gk"], sinks[l], wq, wk,
                                                    f"attn_bwd_{l}")
        du, dvv, g_lng, g_lnb, g_ws, g_bs = _sgu_bwd(s["proj"], dsgu, sgu_ln_g[l][None], sgu_ln_b[l][None],
                                                     w_spatial[l], s["bt"], off_u, ws, f"sgu_bwd_{l}")
        dproj = jnp.concatenate([dq, dk.astype(BF16), dv.astype(BF16), du, dvv, dla, dlb], axis=1)
        dh = _mm(dproj, s["win_t"], "nn", F32, f"d_h_{l}", after=tok)
        g_win_t = _mm(dproj, s["h"], "tn", BF16, f"g_w_in_{l}")
        in_flight[l]["in"], tok = scatter_start([(g_win_t, 1)], f"{l}_in")
        dx, g_mix = _rmsnorm_bwd(s["x0"], mix_norm[l][None], dh, dx1, f"mix_norm_bwd_{l}")
        small_grads[l] = dict(
            mix_norm=g_mix[0], q_norm=g_gq[0].reshape(n_q_heads, HEAD_DIM).sum(0),
            k_norm=g_gk[0].reshape(n_q_heads // Q_PER_KV, HEAD_DIM).sum(0), sinks=g_sinks[0, :n_q_heads],
            sgu_ln_g=g_lng[0], sgu_ln_b=g_lnb[0], w_spatial=g_ws, b_spatial=g_bs[:, 0, :], ffn_norm=g_ffn[0])
    grad_x = dx[None]

    result = {key: {} for key in ("grad", "delta", "m", "v")}
    layer_like = {k: weights[k][0] for k in _SMALL}
    packed_g = jnp.concatenate([_pack(small_grads[l]) for l in layers], axis=0)
    rows_per_layer = packed_g.shape[0] // depth
    small_buf = _place_shard(packed_g[None], dev, F32, "place_small_grads")
    gathered_small = _all_gather([small_buf], "all_gather_small_grads")[0][0]
    packed = [jnp.concatenate([_pack({k: src[k][l] for k in _SMALL}) for l in layers], axis=0)
              for src in (weights, mom1, mom2)]
    small = _small_reduce_adam(gathered_small, *packed, "small_reduce_adam")
    for key, rows in zip(("grad", "delta", "m", "v"), small):
        per_layer = [_unpack(rows[l * rows_per_layer:(l + 1) * rows_per_layer], layer_like) for l in layers]
        for k in _SMALL:
            result[key][k] = jnp.stack([per_layer[l][k] for l in layers])

    def update(k, g):
        result["grad"][k] = g
        result["delta"][k], result["m"][k], result["v"][k] = _adam(weights[k], g, mom1[k], mom2[k], f"adam_{k}")

    done = {name: [scatter_finish(in_flight[l][name], small[0], f"{l}_{name}") for l in layers]
            for name in ("down", "gate_up", "mix")}
    update("w_down", jnp.stack([done["down"][l][0][0] for l in layers]))
    update("w_gate", jnp.stack([done["gate_up"][l][0][0].T for l in layers]))
    update("w_up", jnp.stack([done["gate_up"][l][0][1].T for l in layers]))
    update("w_attn_branch", jnp.stack([done["mix"][l][0][0].T for l in layers]))
    update("w_sgu_branch", jnp.stack([done["mix"][l][1][0].T for l in layers]))
    update("w_out", jnp.stack([done["mix"][l][2][0] for l in layers]))
    last = [scatter_finish(in_flight[l]["in"], result["v"]["w_out"], f"{l}_in") for l in layers]
    update("w_in", jnp.stack([last[l][0][0].T for l in layers]))

    return (loss, grad_x, *[result["grad"][k] for k in names], *[result["delta"][k] for k in names],
            *[result["m"][k] for k in names], *[result["v"][k] for k in names])
```

```python
import functools
import math

import jax
import jax.numpy as jnp
from jax import lax
from jax.experimental import pallas as pl
from jax.experimental.pallas import tpu as pltpu

F32 = jnp.float32
BF16 = jnp.bfloat16
MESH = pl.DeviceIdType.MESH
ANY = pl.BlockSpec(memory_space=pl.ANY)

N_DEV = 8
HEAD_DIM = 64
Q_PER_KV = 4
BLOCK = 128
LANES = 128
ROPE_THETA = 10000.0
EPS = 1e-6
ADAM_LR = 0.001
ADAM_B1 = 0.9
ADAM_B2 = 0.999
ADAM_EPS = 1e-08
ADAM_WD = 0.01
ADAM_STEP = 10
NEG = -1e30
VMEM_LIMIT_BYTES = 56 * 1024 * 1024

NN = ((1,), (0,))
NT = ((1,), (1,))
TN = ((0,), (0,))


def _dot(a, b, dims):
    return lax.dot_general(a, b, (dims, ((), ())), preferred_element_type=F32)


def _params(*sem):
    return pltpu.CompilerParams(dimension_semantics=sem, vmem_limit_bytes=VMEM_LIMIT_BYTES)


def _divisor_tile(n, limit, unit):
    if n <= limit:
        return n
    best = unit
    for t in range(unit, limit + 1, unit):
        if n % t == 0:
            best = t
    assert n % best == 0, (n, limit, unit)
    return best


def _mm(a, b, mode, out_dtype, name, residual=None, after=None):
    parts = a.shape[0] if a.ndim == 3 else 1
    a2 = a.shape[-2:]
    if mode == "nn":
        (m, kp), (k2, n) = a2, b.shape
        k, mp = kp * parts, m
    elif mode == "nt":
        (m, kp), (n, k2) = a2, b.shape
        k, mp = kp * parts, m
    else:
        (k, mp), (k2, n) = a2, b.shape
        m, kp = mp * parts, k
    assert k == k2, (name, a.shape, b.shape)
    tk = _divisor_tile(kp, 2816, 128)
    nk = k // tk
    tm = _divisor_tile(mp, 512 if mode == "tn" else 1024, 128)
    tn = _divisor_tile(n, 2048 if mode == "tn" else (1024 if nk > 1 else 512), 128)
    kpb, mpb = kp // tk, mp // tm
    dims = {"nn": NN, "nt": NT, "tn": TN}[mode]
    lead = (None,) if a.ndim == 3 else ()
    if mode == "tn":
        a_index = lambda i, j, kk: (i // mpb, kk, i % mpb) if lead else (kk, i)
        a_spec = pl.BlockSpec(lead + (tk, tm), a_index)
    else:
        a_index = lambda i, j, kk: (kk // kpb, i, kk % kpb) if lead else (i, kk)
        a_spec = pl.BlockSpec(lead + (tm, tk), a_index)
    if mode == "nt":
        b_spec = pl.BlockSpec((tn, tk), lambda i, j, kk: (j, kk))
    else:
        b_spec = pl.BlockSpec((tk, tn), lambda i, j, kk: (kk, j))
    o_spec = pl.BlockSpec((tm, tn), lambda i, j, kk: (i, j))
    has_res = residual is not None

    def body(*refs):
        a_ref, b_ref = refs[:2]
        r_ref = refs[2] if has_res else None
        o_ref, acc_ref = refs[-2:]
        kk = pl.program_id(2)
        p = _dot(a_ref[...], b_ref[...], dims)

        def finish(total):
            if has_res:
                total = total + r_ref[...]
            o_ref[...] = total.astype(o_ref.dtype)

        if nk == 1:
            finish(p)
        else:
            @pl.when(kk == 0)
            def _():
                acc_ref[...] = p

            @pl.when(jnp.logical_and(kk > 0, kk < nk - 1))
            def _():
                acc_ref[...] += p

            @pl.when(kk == nk - 1)
            def _():
                finish(acc_ref[...] + p)

    in_specs = [a_spec, b_spec] + ([o_spec] if has_res else []) + ([ANY] if after is not None else [])
    args = (a, b) + ((residual,) if has_res else ()) + ((after,) if after is not None else ())
    acc_shape = (tm, tn) if nk > 1 else (8, LANES)
    return pl.pallas_call(
        body,
        name=name,
        grid=(m // tm, n // tn, nk),
        in_specs=in_specs,
        out_specs=o_spec,
        out_shape=jax.ShapeDtypeStruct((m, n), out_dtype),
        scratch_shapes=[pltpu.VMEM(acc_shape, F32)],
        compiler_params=_params("parallel", "parallel", "arbitrary"),
    )(*args)


def _rmsnorm_fwd(x, g, name, after=None):
    t, d = x.shape
    tr = _divisor_tile(t, 256, 8)

    def body(x_ref, g_ref, *rest):
        h_ref = rest[-1]
        xv = x_ref[...]
        rstd = lax.rsqrt(jnp.mean(xv * xv, axis=-1, keepdims=True) + EPS)
        h_ref[...] = (xv * rstd * g_ref[...]).astype(h_ref.dtype)

    return pl.pallas_call(
        body,
        name=name,
        grid=(t // tr,),
        in_specs=[pl.BlockSpec((tr, d), lambda i: (i, 0)), pl.BlockSpec((1, d), lambda i: (0, 0))]
        + ([ANY] if after is not None else []),
        out_specs=pl.BlockSpec((tr, d), lambda i: (i, 0)),
        out_shape=jax.ShapeDtypeStruct((t, d), BF16),
        compiler_params=_params("parallel"),
    )(x, g, *(() if after is None else (after,)))


def _rmsnorm_bwd(x, g, dh, dres, name):
    t, d = x.shape
    tr = _divisor_tile(t, 256, 8)

    def body(x_ref, g_ref, dh_ref, dres_ref, dx_ref, dg_ref):
        i = pl.program_id(0)
        xv = x_ref[...]
        rstd = lax.rsqrt(jnp.mean(xv * xv, axis=-1, keepdims=True) + EPS)
        xh = xv * rstd
        dhv = dh_ref[...]
        dxh = dhv * g_ref[...]
        dx_ref[...] = dres_ref[...] + rstd * (dxh - xh * jnp.mean(dxh * xh, axis=-1, keepdims=True))
        part = jnp.broadcast_to(jnp.sum(dhv * xh, axis=0, keepdims=True), dg_ref.shape)

        @pl.when(i == 0)
        def _():
            dg_ref[...] = part

        @pl.when(i > 0)
        def _():
            dg_ref[...] += part

    row = pl.BlockSpec((tr, d), lambda i: (i, 0))
    return pl.pallas_call(
        body,
        name=name,
        grid=(t // tr,),
        in_specs=[row, pl.BlockSpec((1, d), lambda i: (0, 0)), row, row],
        out_specs=[row, pl.BlockSpec((8, d), lambda i: (0, 0))],
        out_shape=[jax.ShapeDtypeStruct((t, d), F32), jax.ShapeDtypeStruct((8, d), F32)],
        compiler_params=_params("arbitrary"),
    )(x, g, dh, dres)


def _lane(shape):
    return lax.broadcasted_iota(jnp.int32, shape, 1)


def _group_sum64(s):
    row = lax.broadcasted_iota(jnp.int32, (LANES, LANES), 0)
    col = lax.broadcasted_iota(jnp.int32, (LANES, LANES), 1)
    ones = jnp.where((row >= HEAD_DIM) == (col >= HEAD_DIM), 1.0, 0.0).astype(BF16)
    out = []
    for t in range(s.shape[1] // LANES):
        piece = s[:, LANES * t:LANES * t + LANES]
        hi = piece.astype(BF16)
        lo = (piece - hi.astype(F32)).astype(BF16)
        out.append(_dot(hi, ones, NN) + _dot(lo, ones, NN))
    return out[0] if len(out) == 1 else jnp.concatenate(out, axis=1)


def _swap32(x):
    w = x.shape[1]
    return jnp.where((_lane(x.shape) & 32) == 0, pltpu.roll(x, w - 32, axis=1), pltpu.roll(x, 32, axis=1))


def _rope(x, c, s):
    return x * c + _swap32(x) * s


def _rope_t(dy, c, s):
    return dy * c + _swap32(dy * s)


def _head_norm(x):
    rstd = lax.rsqrt(_group_sum64(x * x) * (1.0 / HEAD_DIM) + EPS)
    return x * rstd, rstd


def _head_norm_bwd(dxh, xh, rstd):
    return rstd * (dxh - xh * (_group_sum64(dxh * xh) * (1.0 / HEAD_DIM)))


def _roll64(x):
    return pltpu.roll(x, 64, axis=1)


def _attn_specs(wq, wk):
    kb = wq // wk
    prev = lambda i: jnp.maximum(i - 1, 0)
    return dict(
        q=pl.BlockSpec((BLOCK, wq), lambda i: (i, 0)),
        kc=pl.BlockSpec((BLOCK, wk), lambda i: (i, kb)),
        kp=pl.BlockSpec((BLOCK, wk), lambda i: (prev(i), kb)),
        vc=pl.BlockSpec((BLOCK, wk), lambda i: (i, kb + 1)),
        vp=pl.BlockSpec((BLOCK, wk), lambda i: (prev(i), kb + 1)),
        tq=pl.BlockSpec((BLOCK, wq), lambda i: (i, 0)),
        tkp=pl.BlockSpec((BLOCK, wk), lambda i: (prev(i), 0)),
        gq=pl.BlockSpec((1, wq), lambda i: (0, 0)),
        gk=pl.BlockSpec((1, wk), lambda i: (0, 0)),
        sinks=pl.BlockSpec(memory_space=pltpu.SMEM),
    )


def _attn_prologue(i, q_ref, kc_ref, kp_ref, cq_ref, sq_ref, ckp_ref, skp_ref, gq_ref, gk_ref):
    wk = kc_ref.shape[1]
    cq, sq = cq_ref[...], sq_ref[...]
    ck, sk = cq[:, :wk], sq[:, :wk]
    qh, q_rstd = _head_norm(q_ref[...])
    kch, kc_rstd = _head_norm(kc_ref[...])
    kph, kp_rstd = _head_norm(kp_ref[...])
    qn = _rope(qh * gq_ref[...], cq, sq)
    knc = _rope(kch * gk_ref[...], ck, sk)
    knp = _rope(kph * gk_ref[...], ckp_ref[...], skp_ref[...])
    row = lax.broadcasted_iota(jnp.int32, (BLOCK, BLOCK), 0)
    col = lax.broadcasted_iota(jnp.int32, (BLOCK, BLOCK), 1)
    mask_c = col <= row
    mask_p = jnp.logical_and(col > row, i > 0)
    half = (col >= 64).astype(jnp.int32)
    return dict(cq=cq, sq=sq, ck=ck, sk=sk, qh=qh, q_rstd=q_rstd, kch=kch, kc_rstd=kc_rstd, kph=kph,
                kp_rstd=kp_rstd, qn=qn, knc=knc, knp=knp, mask_c=mask_c, mask_p=mask_p, half=half)


def _head_scores(st, t, e, sink, scale):
    g = (2 * t) // Q_PER_KV
    ks, kpar = g // 2, g % 2
    sl = slice(LANES * ks, LANES * ks + LANES)
    mine = st["half"] == e
    qm = jnp.where(mine, st["qn"][:, LANES * t:LANES * t + LANES], 0.0).astype(BF16)
    kc, kp = st["knc"][:, sl], st["knp"][:, sl]
    flip = e != kpar
    if flip:
        kc, kp = _roll64(kc), _roll64(kp)
    kc, kp = kc.astype(BF16), kp.astype(BF16)
    s_c = jnp.where(st["mask_c"], _dot(qm, kc, NT) * scale, NEG)
    s_p = jnp.where(st["mask_p"], _dot(qm, kp, NT) * scale, NEG)
    m = jnp.maximum(jnp.maximum(jnp.max(s_c, axis=1, keepdims=True), jnp.max(s_p, axis=1, keepdims=True)), sink)
    p_c, p_p = jnp.exp(s_c - m), jnp.exp(s_p - m)
    p_s = jnp.exp(sink - m)
    inv = 1.0 / (jnp.sum(p_c, axis=1, keepdims=True) + jnp.sum(p_p, axis=1, keepdims=True) + p_s)
    return dict(sl=sl, mine=mine, flip=flip, qm=qm, kc=kc, kp=kp, pr_c=p_c * inv, pr_p=p_p * inv, pr_s=p_s * inv)


def _attn_fwd(proj, tables, gq, gk, sinks, wq, wk, name, after=None):
    t = proj.shape[0]
    nb = t // BLOCK
    sp = _attn_specs(wq, wk)
    scale = HEAD_DIM ** -0.5
    cos_t, sin_t = tables

    def body(sinks_ref, q_ref, kc_ref, kp_ref, vc_ref, vp_ref, cq_ref, sq_ref, ckp_ref, skp_ref, gq_ref, gk_ref,
             *rest):
        o_ref = rest[-1]
        i = pl.program_id(0)
        st = _attn_prologue(i, q_ref, kc_ref, kp_ref, cq_ref, sq_ref, ckp_ref, skp_ref, gq_ref, gk_ref)
        vc_all, vp_all = vc_ref[...], vp_ref[...]
        for ts in range(wq // LANES):
            acc = jnp.zeros((BLOCK, LANES), F32)
            for e in (0, 1):
                hs = _head_scores(st, ts, e, sinks_ref[2 * ts + e], scale)
                vc, vp = vc_all[:, hs["sl"]], vp_all[:, hs["sl"]]
                if hs["flip"]:
                    vc, vp = _roll64(vc), _roll64(vp)
                vc = jnp.where(hs["mine"], vc, 0.0).astype(BF16)
                vp = jnp.where(hs["mine"], vp, 0.0).astype(BF16)
                acc = acc + _dot(hs["pr_c"].astype(BF16), vc, NN) + _dot(hs["pr_p"].astype(BF16), vp, NN)
            o_ref[:, LANES * ts:LANES * ts + LANES] = acc.astype(o_ref.dtype)

    return pl.pallas_call(
        body,
        name=name,
        grid=(nb,),
        in_specs=[sp["sinks"], sp["q"], sp["kc"], sp["kp"], sp["vc"], sp["vp"], sp["tq"], sp["tq"], sp["tkp"],
                  sp["tkp"], sp["gq"], sp["gk"]] + ([ANY] if after is not None else []),
        out_specs=pl.BlockSpec((BLOCK, wq), lambda i: (i, 0)),
        out_shape=jax.ShapeDtypeStruct((t, wq), BF16),
        compiler_params=_params("parallel"),
    )(sinks, proj, proj, proj, proj, proj, cos_t, sin_t, cos_t, sin_t, gq, gk, *(() if after is None else (after,)))


def _attn_bwd(proj, dout, tables, gq, gk, sinks, wq, wk, name):
    t = proj.shape[0]
    nb = t // BLOCK
    sp = _attn_specs(wq, wk)
    scale = HEAD_DIM ** -0.5
    cos_t, sin_t = tables

    def body(sinks_ref, q_ref, kc_ref, kp_ref, vc_ref, vp_ref, cq_ref, sq_ref, ckp_ref, skp_ref, gq_ref, gk_ref,
             do_ref, dq_ref, dk_ref, dv_ref, dgq_ref, dgk_ref, dsk_ref, dqn_ref, dknc_ref, dknp_ref, dvc_ref,
             dvp_ref):
        i = pl.program_id(0)
        st = _attn_prologue(i, q_ref, kc_ref, kp_ref, cq_ref, sq_ref, ckp_ref, skp_ref, gq_ref, gk_ref)
        vc_all, vp_all = vc_ref[...], vp_ref[...]
        dknc_ref[...] = jnp.zeros_like(dknc_ref)
        dknp_ref[...] = jnp.zeros_like(dknp_ref)
        dvc_ref[...] = jnp.zeros_like(dvc_ref)
        dvp_ref[...] = jnp.zeros_like(dvp_ref)
        lane8 = _lane((8, LANES))
        dsinks = jnp.zeros((8, LANES), F32)
        for ts in range(wq // LANES):
            dq_acc = jnp.zeros((BLOCK, LANES), F32)
            for e in (0, 1):
                hs = _head_scores(st, ts, e, sinks_ref[2 * ts + e], scale)
                sl, flip = hs["sl"], hs["flip"]
                vc, vp = vc_all[:, sl], vp_all[:, sl]
                if flip:
                    vc, vp = _roll64(vc), _roll64(vp)
                dom = jnp.where(hs["mine"], do_ref[:, LANES * ts:LANES * ts + LANES], 0.0).astype(BF16)
                dp_c = _dot(dom, vc.astype(BF16), NT)
                dp_p = _dot(dom, vp.astype(BF16), NT)
                pr_c, pr_p = hs["pr_c"], hs["pr_p"]
                rs = jnp.sum(pr_c * dp_c, axis=1, keepdims=True) + jnp.sum(pr_p * dp_p, axis=1, keepdims=True)
                ds_c = (pr_c * (dp_c - rs) * scale)
                ds_p = (pr_p * (dp_p - rs) * scale)
                dsink = jnp.sum(-hs["pr_s"] * rs)
                dsinks = dsinks + jnp.where(lane8 == 2 * ts + e, dsink, 0.0)
                dq_acc = dq_acc + jnp.where(
                    hs["mine"], _dot(ds_c.astype(BF16), hs["kc"], NN) + _dot(ds_p.astype(BF16), hs["kp"], NN), 0.0)
                dv_c = _dot(pr_c.T.astype(BF16), dom, NN)
                dv_p = _dot(pr_p.T.astype(BF16), dom, NN)
                dk_c = _dot(ds_c.T.astype(BF16), hs["qm"], NN)
                dk_p = _dot(ds_p.T.astype(BF16), hs["qm"], NN)
                if flip:
                    dv_c, dv_p, dk_c, dk_p = _roll64(dv_c), _roll64(dv_p), _roll64(dk_c), _roll64(dk_p)
                dvc_ref[:, sl] += dv_c
                dvp_ref[:, sl] += dv_p
                dknc_ref[:, sl] += dk_c
                dknp_ref[:, sl] += dk_p
            dqn_ref[:, LANES * ts:LANES * ts + LANES] = dq_acc

        gqv, gkv = gq_ref[...], gk_ref[...]
        dqg = _rope_t(dqn_ref[...], st["cq"], st["sq"])
        dq_ref[...] = _head_norm_bwd(dqg * gqv, st["qh"], st["q_rstd"]).astype(dq_ref.dtype)
        dkcg = _rope_t(dknc_ref[...], st["ck"], st["sk"])
        dkpg = _rope_t(dknp_ref[...], ckp_ref[...], skp_ref[...])
        dk_cur = _head_norm_bwd(dkcg * gkv, st["kch"], st["kc_rstd"])
        dk_prev = _head_norm_bwd(dkpg * gkv, st["kph"], st["kp_rstd"])
        dgq_part = jnp.broadcast_to(jnp.sum(dqg * st["qh"], axis=0, keepdims=True), dgq_ref.shape)
        dgk_part = jnp.broadcast_to(
            jnp.sum(dkcg * st["kch"] + dkpg * st["kph"], axis=0, keepdims=True), dgk_ref.shape)
        cur = pl.ds(pl.multiple_of(i * BLOCK, BLOCK), BLOCK)
        dk_ref[cur, :] = dk_cur
        dv_ref[cur, :] = dvc_ref[...]

        @pl.when(i == 0)
        def _():
            dgq_ref[...] = dgq_part
            dgk_ref[...] = dgk_part
            dsk_ref[...] = dsinks

        @pl.when(i > 0)
        def _():
            before = pl.ds(pl.multiple_of((i - 1) * BLOCK, BLOCK), BLOCK)
            dk_ref[before, :] += dk_prev
            dv_ref[before, :] += dvp_ref[...]
            dgq_ref[...] += dgq_part
            dgk_ref[...] += dgk_part
            dsk_ref[...] += dsinks

    whole = lambda shape: pl.BlockSpec(shape, lambda i: (0, 0))
    return pl.pallas_call(
        body,
        name=name,
        grid=(nb,),
        in_specs=[sp["sinks"], sp["q"], sp["kc"], sp["kp"], sp["vc"], sp["vp"], sp["tq"], sp["tq"], sp["tkp"],
                  sp["tkp"], sp["gq"], sp["gk"], pl.BlockSpec((BLOCK, wq), lambda i: (i, 0))],
        out_specs=[pl.BlockSpec((BLOCK, wq), lambda i: (i, 0)), whole((t, wk)), whole((t, wk)), whole((8, wq)),
                   whole((8, wk)), whole((8, LANES))],
        out_shape=[jax.ShapeDtypeStruct((t, wq), BF16), jax.ShapeDtypeStruct((t, wk), F32),
                   jax.ShapeDtypeStruct((t, wk), F32), jax.ShapeDtypeStruct((8, wq), F32),
                   jax.ShapeDtypeStruct((8, wk), F32), jax.ShapeDtypeStruct((8, LANES), F32)],
        scratch_shapes=[pltpu.VMEM((BLOCK, wq), F32), pltpu.VMEM((BLOCK, wk), F32), pltpu.VMEM((BLOCK, wk), F32),
                        pltpu.VMEM((BLOCK, wk), F32), pltpu.VMEM((BLOCK, wk), F32)],
        compiler_params=_params("arbitrary"),
    )(sinks, proj, proj, proj, proj, proj, cos_t, sin_t, cos_t, sin_t, gq, gk, dout)


_GELU_K = math.sqrt(2.0 / math.pi)
_GELU_A = 0.044715


def _gelu(x):
    return 0.5 * x * (1.0 + jnp.tanh(_GELU_K * (x + _GELU_A * x * x * x)))


def _gelu_grad(x):
    th = jnp.tanh(_GELU_K * (x + _GELU_A * x * x * x))
    return 0.5 * (1.0 + th) + 0.5 * x * (1.0 - th * th) * (_GELU_K * (1.0 + 3.0 * _GELU_A * x * x))


def _group_ln(v):
    mu = jnp.mean(v, axis=1, keepdims=True)
    cen = v - mu
    rstd = lax.rsqrt(jnp.mean(cen * cen, axis=1, keepdims=True) + EPS)
    return cen * rstd, rstd


def _sgu_geometry(off_u, ws):
    cw = math.gcd(off_u, ws)
    return cw, ws // cw, off_u // cw, (off_u + ws) // cw


def _sgu_fwd(proj, ln_g, ln_b, w_s, bt, off_u, ws, name):
    t = proj.shape[0]
    nb = t // BLOCK
    cw, nc, ub, vb = _sgu_geometry(off_u, ws)
    gpc = cw // LANES
    ng = ws // LANES

    def body(u_ref, v_ref, g_ref, b_ref, w_ref, bt_ref, o_ref):
        jc = pl.program_id(0)
        row = lax.broadcasted_iota(jnp.int32, (BLOCK, BLOCK), 0)
        col = lax.broadcasted_iota(jnp.int32, (BLOCK, BLOCK), 1)
        lane_g = _lane((BLOCK, ng))
        for gi in range(gpc):
            sl = slice(LANES * gi, LANES * gi + LANES)
            xh, _ = _group_ln(_gelu(v_ref[:, sl]))
            vn = xh * g_ref[:, sl] + b_ref[:, sl]
            w = jnp.where(row >= col, w_ref[gi], 0.0).astype(BF16)
            bias = jnp.sum(jnp.where(lane_g == jc * gpc + gi, bt_ref[...], 0.0), axis=1, keepdims=True)
            s = _dot(w, vn.astype(BF16), NN) + bias
            o_ref[:, sl] = (_gelu(u_ref[:, sl]) * s).astype(o_ref.dtype)

    return pl.pallas_call(
        body,
        name=name,
        grid=(nc, nb),
        in_specs=[pl.BlockSpec((BLOCK, cw), lambda jc, i: (i, ub + jc)),
                  pl.BlockSpec((BLOCK, cw), lambda jc, i: (i, vb + jc)),
                  pl.BlockSpec((1, cw), lambda jc, i: (0, jc)),
                  pl.BlockSpec((1, cw), lambda jc, i: (0, jc)),
                  pl.BlockSpec((gpc, BLOCK, BLOCK), lambda jc, i: (jc, 0, 0)),
                  pl.BlockSpec((BLOCK, ng), lambda jc, i: (0, 0))],
        out_specs=pl.BlockSpec((BLOCK, cw), lambda jc, i: (i, jc)),
        out_shape=jax.ShapeDtypeStruct((t, ws), BF16),
        compiler_params=_params("parallel", "parallel"),
    )(proj, proj, ln_g, ln_b, w_s, bt)


def _sgu_bwd(proj, dout, ln_g, ln_b, w_s, bt, off_u, ws, name):
    t = proj.shape[0]
    nb = t // BLOCK
    cw, nc, ub, vb = _sgu_geometry(off_u, ws)
    gpc = cw // LANES
    ng = ws // LANES

    def body(u_ref, v_ref, g_ref, b_ref, w_ref, bt_ref, do_ref, du_ref, dv_ref, dg_ref, db_ref, dw_ref, dbs_ref,
             bacc_ref):
        jc = pl.program_id(0)
        i = pl.program_id(1)
        row = lax.broadcasted_iota(jnp.int32, (BLOCK, BLOCK), 0)
        col = lax.broadcasted_iota(jnp.int32, (BLOCK, BLOCK), 1)
        lane_g = _lane((BLOCK, ng))
        tri = row >= col

        @pl.when(i == 0)
        def _():
            dg_ref[...] = jnp.zeros_like(dg_ref)
            db_ref[...] = jnp.zeros_like(db_ref)
            dw_ref[...] = jnp.zeros_like(dw_ref)
            bacc_ref[...] = jnp.zeros_like(bacc_ref)

        for gi in range(gpc):
            sl = slice(LANES * gi, LANES * gi + LANES)
            u_raw, v_raw = u_ref[:, sl], v_ref[:, sl]
            xh, rstd = _group_ln(_gelu(v_raw))
            gam = g_ref[:, sl]
            vn = (xh * gam + b_ref[:, sl]).astype(BF16)
            w = jnp.where(tri, w_ref[gi], 0.0)
            bias = jnp.sum(jnp.where(lane_g == jc * gpc + gi, bt_ref[...], 0.0), axis=1, keepdims=True)
            s = _dot(w.astype(BF16), vn, NN) + bias
            dov = do_ref[:, sl]
            du_ref[:, sl] = (dov * s * _gelu_grad(u_raw)).astype(du_ref.dtype)
            ds = dov * _gelu(u_raw)
            ds16 = ds.astype(BF16)
            dw_ref[gi] += jnp.where(tri, _dot(ds16, vn, NT), 0.0)
            bacc_ref[gi] += ds
            dvn = _dot(w.T.astype(BF16), ds16, NN)
            dg_ref[:, sl] += jnp.broadcast_to(jnp.sum(dvn * xh, axis=0, keepdims=True), (8, LANES))
            db_ref[:, sl] += jnp.broadcast_to(jnp.sum(dvn, axis=0, keepdims=True), (8, LANES))
            dxh = dvn * gam
            dvg = rstd * (dxh - jnp.mean(dxh, axis=1, keepdims=True)
                          - xh * jnp.mean(dxh * xh, axis=1, keepdims=True))
            dv_ref[:, sl] = (dvg * _gelu_grad(v_raw)).astype(dv_ref.dtype)

        @pl.when(i == nb - 1)
        def _():
            for gi in range(gpc):
                dbs_ref[gi] = jnp.broadcast_to(jnp.sum(bacc_ref[gi].T, axis=0, keepdims=True), (8, LANES))

    blk = lambda base: pl.BlockSpec((BLOCK, cw), lambda jc, i: (i, base + jc))
    vec = pl.BlockSpec((1, cw), lambda jc, i: (0, jc))
    acc = pl.BlockSpec((8, cw), lambda jc, i: (0, jc))
    wsp = pl.BlockSpec((gpc, BLOCK, BLOCK), lambda jc, i: (jc, 0, 0))
    return pl.pallas_call(
        body,
        name=name,
        grid=(nc, nb),
        in_specs=[blk(ub), blk(vb), vec, vec, wsp, pl.BlockSpec((BLOCK, ng), lambda jc, i: (0, 0)), blk(0)],
        out_specs=[blk(0), blk(0), acc, acc, wsp, pl.BlockSpec((gpc, 8, LANES), lambda jc, i: (jc, 0, 0))],
        out_shape=[jax.ShapeDtypeStruct((t, ws), BF16), jax.ShapeDtypeStruct((t, ws), BF16),
                   jax.ShapeDtypeStruct((8, ws), F32), jax.ShapeDtypeStruct((8, ws), F32),
                   jax.ShapeDtypeStruct((ng, BLOCK, BLOCK), F32), jax.ShapeDtypeStruct((ng, 8, LANES), F32)],
        scratch_shapes=[pltpu.VMEM((gpc, BLOCK, BLOCK), F32)],
        compiler_params=_params("arbitrary", "arbitrary"),
    )(proj, proj, ln_g, ln_b, w_s, bt, dout)


def _sigmoid(x):
    return 1.0 / (1.0 + jnp.exp(-x))


def _merge_geometry(off_g, d):
    cw = math.gcd(off_g, d)
    return cw, d // cw, off_g // cw, (off_g + d) // cw


def _merge_fwd(a, b, proj, off_g, name):
    t, d = a.shape
    cw, nc, ab, bb = _merge_geometry(off_g, d)
    tr = _divisor_tile(t, 512, 8)

    def body(a_ref, b_ref, la_ref, lb_ref, o_ref):
        o_ref[...] = (_sigmoid(la_ref[...]) * a_ref[...] + _sigmoid(lb_ref[...]) * b_ref[...]).astype(o_ref.dtype)

    blk = lambda base: pl.BlockSpec((tr, cw), lambda i, j: (i, base + j))
    return pl.pallas_call(
        body,
        name=name,
        grid=(t // tr, nc),
        in_specs=[blk(0), blk(0), blk(ab), blk(bb)],
        out_specs=blk(0),
        out_shape=jax.ShapeDtypeStruct((t, d), BF16),
        compiler_params=_params("parallel", "parallel"),
    )(a, b, proj, proj)


def _merge_bwd(a, b, proj, dm, off_g, name):
    t, d = a.shape
    cw, nc, ab, bb = _merge_geometry(off_g, d)
    tr = _divisor_tile(t, 512, 8)

    def body(a_ref, b_ref, la_ref, lb_ref, dm_ref, da_ref, db_ref, dla_ref, dlb_ref):
        dmv = dm_ref[...]
        ga, gb = _sigmoid(la_ref[...]), _sigmoid(lb_ref[...])
        da_ref[...] = (dmv * ga).astype(da_ref.dtype)
        db_ref[...] = (dmv * gb).astype(db_ref.dtype)
        dla_ref[...] = (dmv * a_ref[...] * ga * (1.0 - ga)).astype(dla_ref.dtype)
        dlb_ref[...] = (dmv * b_ref[...] * gb * (1.0 - gb)).astype(dlb_ref.dtype)

    blk = lambda base: pl.BlockSpec((tr, cw), lambda i, j: (i, base + j))
    return pl.pallas_call(
        body,
        name=name,
        grid=(t // tr, nc),
        in_specs=[blk(0), blk(0), blk(ab), blk(bb), blk(0)],
        out_specs=[blk(0)] * 4,
        out_shape=[jax.ShapeDtypeStruct((t, d), BF16)] * 4,
        compiler_params=_params("parallel", "parallel"),
    )(a, b, proj, proj, dm)


def _swiglu_fwd(gu, name, after=None):
    t, f2 = gu.shape
    f = f2 // 2
    cw = _divisor_tile(f, 1536, 128)
    nc = f // cw
    tr = _divisor_tile(t, 256, 8)

    def body(g_ref, u_ref, *rest):
        o_ref = rest[-1]
        gv = g_ref[...]
        o_ref[...] = (gv * _sigmoid(gv) * u_ref[...]).astype(o_ref.dtype)

    blk = lambda base: pl.BlockSpec((tr, cw), lambda i, j: (i, base + j))
    return pl.pallas_call(
        body,
        name=name,
        grid=(t // tr, nc),
        in_specs=[blk(0), blk(nc)] + ([ANY] if after is not None else []),
        out_specs=blk(0),
        out_shape=jax.ShapeDtypeStruct((t, f), BF16),
        compiler_params=_params("parallel", "parallel"),
    )(gu, gu, *(() if after is None else (after,)))


def _swiglu_bwd(gu, dact, name):
    t, f2 = gu.shape
    f = f2 // 2
    cw = _divisor_tile(f, 1536, 128)
    nc = f // cw
    tr = _divisor_tile(t, 256, 8)

    def body(g_ref, u_ref, da_ref, o_ref):
        gv, dav = g_ref[...], da_ref[...]
        sg = _sigmoid(gv)
        o_ref[0] = (dav * u_ref[...] * (sg + gv * sg * (1.0 - sg))).astype(o_ref.dtype)
        o_ref[1] = (dav * gv * sg).astype(o_ref.dtype)

    blk = lambda base: pl.BlockSpec((tr, cw), lambda i, j: (i, base + j))
    return pl.pallas_call(
        body,
        name=name,
        grid=(t // tr, nc),
        in_specs=[blk(0), blk(nc), blk(0)],
        out_specs=pl.BlockSpec((2, tr, cw), lambda i, j: (0, i, j)),
        out_shape=jax.ShapeDtypeStruct((2, t, f), BF16),
        compiler_params=_params("parallel", "parallel"),
    )(gu, gu, dact)


def _loss_and_grad(y, target, name):
    t, d = y.shape
    tr = _divisor_tile(t, 256, 8)

    def body(y_ref, t_ref, l_ref, dy_ref):
        i = pl.program_id(0)
        err = y_ref[...] - t_ref[...]
        dy_ref[...] = err * (1.0 / d)
        part = jnp.broadcast_to(0.5 * jnp.sum(err * err) * (1.0 / d), l_ref.shape)

        @pl.when(i == 0)
        def _():
            l_ref[...] = part

        @pl.when(i > 0)
        def _():
            l_ref[...] += part

    row = pl.BlockSpec((tr, d), lambda i: (i, 0))
    return pl.pallas_call(
        body,
        name=name,
        grid=(t // tr,),
        in_specs=[row, row],
        out_specs=[pl.BlockSpec((8, LANES), lambda i: (0, 0)), row],
        out_shape=[jax.ShapeDtypeStruct((8, LANES), F32), jax.ShapeDtypeStruct((t, d), F32)],
        compiler_params=_params("arbitrary"),
    )(y, target)


def _adam_math(w, g, m, v):
    m = ADAM_B1 * m + (1.0 - ADAM_B1) * g
    v = ADAM_B2 * v + (1.0 - ADAM_B2) * (g * g)
    m_hat = m / (1.0 - ADAM_B1 ** ADAM_STEP)
    v_hat = v / (1.0 - ADAM_B2 ** ADAM_STEP)
    delta = -ADAM_LR * (m_hat / (jnp.sqrt(v_hat) + ADAM_EPS) + ADAM_WD * w)
    return delta, m, v


def _row_tile(r, c, elems=512 * 1024):
    return _divisor_tile(r, max(8, elems // c // 8 * 8), 8)


def _adam(w, grads, m, v, name):
    nl, r, c = w.shape
    tr = _row_tile(r, c, 256 * 1024)
    nb = r // tr

    def body(*refs):
        w_ref, m_ref, v_ref = refs[:3]
        g_refs = refs[3:3 + nl]
        g_ref, d_ref, nm_ref, nv_ref = refs[3 + nl:]
        layer = pl.program_id(0)
        g = g_refs[0][...]
        for li in range(1, nl):
            g = jnp.where(layer == li, g_refs[li][...], g)
        g_ref[...] = g
        d_ref[...], nm_ref[...], nv_ref[...] = _adam_math(w_ref[...], g, m_ref[...], v_ref[...])

    def grad_spec(li, p):
        return pl.BlockSpec(
            (None, tr, c), lambda l, i: (p, jnp.where(l < li, 0, jnp.where(l > li, nb - 1, i)), 0))

    row = pl.BlockSpec((None, tr, c), lambda l, i: (l, i, 0))
    return pl.pallas_call(
        body,
        name=name,
        grid=(nl, nb),
        in_specs=[row] * 3 + [grad_spec(li, p) for li, (_, p) in enumerate(grads)],
        out_specs=[row] * 4,
        out_shape=[jax.ShapeDtypeStruct((nl, r, c), F32)] * 4,
        compiler_params=_params("arbitrary", "arbitrary"),
    )(w, m, v, *[g for g, _ in grads])


def _place_shard(parts, layer, dev, out_dtype, name, after=None):
    p = len(parts)
    _, r, c = parts[0].shape
    tr = _row_tile(r, c)

    def body(dev_ref, *refs):
        o_ref = refs[-1]
        x = refs[0][...]
        for pi in range(1, p):
            x = jnp.where(pl.program_id(0) == pi, refs[pi][...], x)
        o_ref[...] = x.astype(o_ref.dtype)

    return pl.pallas_call(
        body,
        name=name,
        grid_spec=pltpu.PrefetchScalarGridSpec(
            num_scalar_prefetch=1,
            grid=(p, r // tr),
            in_specs=[pl.BlockSpec((None, tr, c), lambda pi, i, dev_ref: (layer, i, 0))] * p
            + ([ANY] if after is not None else []),
            out_specs=pl.BlockSpec((None, None, tr, c), lambda pi, i, dev_ref: (pi, dev_ref[0], i, 0)),
        ),
        out_shape=jax.ShapeDtypeStruct((p, N_DEV, r, c), out_dtype),
        compiler_params=_params("parallel", "parallel"),
    )(dev, *parts, *(() if after is None else (after,)))


def _sum_sibling(g, land, core, name):
    p, _, _, r, c = g.shape
    tr = _row_tile(r, c)

    def body(core_ref, g_ref, l_ref, o_ref):
        o_ref[...] = (g_ref[...].astype(F32) + l_ref[...].astype(F32)).astype(o_ref.dtype)

    return pl.pallas_call(
        body,
        name=name,
        grid_spec=pltpu.PrefetchScalarGridSpec(
            num_scalar_prefetch=1,
            grid=(p, 4, r // tr),
            in_specs=[pl.BlockSpec((None, None, None, tr, c), lambda pi, q, i, core_ref: (pi, q, core_ref[0], i, 0)),
                      pl.BlockSpec((None, None, None, tr, c), lambda pi, q, i, core_ref: (pi, q, 0, i, 0))],
            out_specs=pl.BlockSpec((None, None, tr, c), lambda pi, q, i, core_ref: (pi, q, i, 0)),
        ),
        out_shape=jax.ShapeDtypeStruct((p, 4, r, c), BF16),
        compiler_params=_params("parallel", "parallel", "parallel"),
    )(core, g, land)


def _sum_chips(s, lands, chip, name):
    p, _, r, c = s.shape
    tr = _row_tile(r, c)

    def body(chip_ref, s_ref, l0_ref, l1_ref, l2_ref, o_ref):
        total = s_ref[...].astype(F32) + l0_ref[...].astype(F32)
        o_ref[...] = total + l1_ref[...].astype(F32) + l2_ref[...].astype(F32)

    land_spec = pl.BlockSpec((None, None, tr, c), lambda pi, i, chip_ref: (pi, 0, i, 0))
    return pl.pallas_call(
        body,
        name=name,
        grid_spec=pltpu.PrefetchScalarGridSpec(
            num_scalar_prefetch=1,
            grid=(p, r // tr),
            in_specs=[pl.BlockSpec((None, None, tr, c), lambda pi, i, chip_ref: (pi, chip_ref[0], i, 0)),
                      land_spec, land_spec, land_spec],
            out_specs=pl.BlockSpec((None, tr, c), lambda pi, i, chip_ref: (pi, i, 0)),
        ),
        out_shape=jax.ShapeDtypeStruct((p, r, c), F32),
        compiler_params=_params("parallel", "parallel"),
    )(chip, s, *lands)


def _small_reduce_adam(gathered, w, m, v, name):
    _, r, c = gathered.shape
    tr = _row_tile(r, c)

    def body(p_ref, w_ref, m_ref, v_ref, g_ref, d_ref, nm_ref, nv_ref):
        g = p_ref[0]
        for j in range(1, N_DEV):
            g = g + p_ref[j]
        g_ref[...] = g
        d_ref[...], nm_ref[...], nv_ref[...] = _adam_math(w_ref[...], g, m_ref[...], v_ref[...])

    row = pl.BlockSpec((tr, c), lambda i: (i, 0))
    return pl.pallas_call(
        body,
        name=name,
        grid=(r // tr,),
        in_specs=[pl.BlockSpec((N_DEV, tr, c), lambda i: (0, i, 0)), row, row, row],
        out_specs=[row] * 4,
        out_shape=[jax.ShapeDtypeStruct((r, c), F32)] * 4,
        compiler_params=_params("parallel"),
    )(gathered, w, m, v)


def _place():
    return lax.axis_index("x"), lax.axis_index("y"), lax.axis_index("c")


def _all_gather(bufs, name):
    n = len(bufs)

    def body(*refs):
        outs = refs[n:2 * n]
        send_sems, recv_sems = refs[2 * n:]
        x, y, c = _place()
        me, sibling = (x, y, c), (x, y, 1 - c)
        chips = [(1 - x, y), (x, 1 - y), (1 - x, 1 - y)]

        def block(a, px, py, pc):
            return outs[a].at[:, pl.ds(4 * px + 2 * py + pc, 1)]

        def copy(a, k, blk, to):
            return pltpu.make_async_remote_copy(
                src_ref=block(a, *blk), dst_ref=block(a, *blk), send_sem=send_sems.at[a, k],
                recv_sem=recv_sems.at[a, k], device_id=to, device_id_type=MESH)

        first = []
        for a in range(n):
            first.append(copy(a, 0, me, sibling))
            first += [copy(a, 1 + j, me, (*chip, c)) for j, chip in enumerate(chips)]
        for cp in first:
            cp.start()
        passed = []
        for j, chip in enumerate(chips):
            for a in range(n):
                copy(a, 1 + j, (*chip, c), me).wait_recv()
                fwd = copy(a, 4 + j, (*chip, c), sibling)
                fwd.start()
                passed.append(fwd)
        for a in range(n):
            copy(a, 0, sibling, me).wait_recv()
            for j, chip in enumerate(chips):
                copy(a, 4 + j, (*chip, 1 - c), me).wait_recv()
        for cp in first + passed:
            cp.wait_send()

    return pl.pallas_call(
        body,
        name=name,
        in_specs=[ANY] * n,
        out_specs=[ANY] * n,
        out_shape=[jax.ShapeDtypeStruct(b.shape, b.dtype) for b in bufs],
        input_output_aliases={a: a for a in range(n)},
        scratch_shapes=[pltpu.SemaphoreType.DMA((n, 7)), pltpu.SemaphoreType.DMA((n, 7))],
    )(*bufs)


HBM = pl.BlockSpec(memory_space=pltpu.HBM)
SEM = pl.BlockSpec(memory_space=pltpu.SEMAPHORE)
TOKEN = pl.BlockSpec(memory_space=pltpu.VMEM)
EFFECT = pltpu.SideEffectType.DATAFLOW_SIDE_EFFECTING


def _in_hbm(a):
    return pltpu.with_memory_space_constraint(a, pltpu.HBM)


def _gather_start(chunks, name):
    flat = [b for chunk in chunks for b in chunk]
    n, nch = len(flat), len(chunks)

    def body(*refs):
        sems, outs, token = refs[n:n + 2 * nch], refs[n + 2 * nch:2 * n + 2 * nch], refs[2 * n + 2 * nch]
        x, y, c = _place()
        targets = [(x, y, 1 - c), (1 - x, y, c), (x, 1 - y, c), (1 - x, 1 - y, c)]
        a = 0
        for ci, chunk in enumerate(chunks):
            for k in range(len(chunk)):
                mine = outs[a].at[:, pl.ds(4 * x + 2 * y + c, 1)]
                for ti, to in enumerate(targets):
                    pltpu.make_async_remote_copy(
                        src_ref=mine, dst_ref=mine, send_sem=sems[2 * ci].at[4 * k + ti],
                        recv_sem=sems[2 * ci + 1].at[4 * k + ti], device_id=to, device_id_type=MESH).start()
                a += 1
        token[...] = jnp.zeros_like(token)

    sem_shapes = []
    for chunk in chunks:
        sem_shapes += [pltpu.SemaphoreType.DMA((4 * len(chunk),))] * 2
    outs = pl.pallas_call(
        body,
        name=name,
        in_specs=[HBM] * n,
        out_specs=[SEM] * (2 * nch) + [HBM] * n + [TOKEN],
        out_shape=sem_shapes + [pltpu.HBM(b.shape, b.dtype) for b in flat] + [jax.ShapeDtypeStruct((8, LANES), F32)],
        input_output_aliases={i: 2 * nch + i for i in range(n)},
        compiler_params=pltpu.CompilerParams(has_side_effects=EFFECT),
    )(*[_in_hbm(b) for b in flat])
    result, a = [], 2 * nch
    for ci, chunk in enumerate(chunks):
        result.append((outs[2 * ci], outs[2 * ci + 1], list(outs[a:a + len(chunk)])))
        a += len(chunk)
    return result, outs[-1]


def _gather_wait(send_sems, recv_sems, bufs, after, name):
    n = len(bufs)

    def body(*refs):
        ins, ssem, rsem = refs[:n], refs[n], refs[n + 1]
        x, y, c = _place()
        sources = [(x, y, 1 - c), (1 - x, y, c), (x, 1 - y, c), (1 - x, 1 - y, c)]
        for k in range(n):
            for ti, (px, py, pc) in enumerate(sources):
                theirs = ins[k].at[:, pl.ds(4 * px + 2 * py + pc, 1)]
                cp = pltpu.make_async_remote_copy(
                    src_ref=theirs, dst_ref=theirs, send_sem=ssem.at[4 * k + ti], recv_sem=rsem.at[4 * k + ti],
                    device_id=(px, py, pc), device_id_type=MESH)
                cp.wait_send()
                cp.wait_recv()

    return pl.pallas_call(
        body,
        name=name,
        in_specs=[HBM] * n + [SEM, SEM, ANY],
        out_specs=[HBM] * n,
        out_shape=[pltpu.HBM(b.shape, b.dtype) for b in bufs],
        input_output_aliases={i: i for i in range(n)},
        compiler_params=pltpu.CompilerParams(has_side_effects=EFFECT),
    )(*bufs, send_sems, recv_sems, after)


def _forward_start(bufs, name):
    n = len(bufs)

    def body(*refs):
        ssem, rsem = refs[n], refs[n + 1]
        outs, token = refs[n + 2:2 * n + 2], refs[2 * n + 2]
        x, y, c = _place()
        chips = [(1 - x, y), (x, 1 - y), (1 - x, 1 - y)]
        for a in range(n):
            for j, (px, py) in enumerate(chips):
                got = outs[a].at[:, pl.ds(4 * px + 2 * py + c, 1)]
                pltpu.make_async_remote_copy(
                    src_ref=got, dst_ref=got, send_sem=ssem.at[3 * a + j], recv_sem=rsem.at[3 * a + j],
                    device_id=(x, y, 1 - c), device_id_type=MESH).start()
        token[...] = jnp.zeros_like(token)

    outs = pl.pallas_call(
        body,
        name=name,
        in_specs=[HBM] * n,
        out_specs=[SEM, SEM] + [HBM] * n + [TOKEN],
        out_shape=[pltpu.SemaphoreType.DMA((3 * n,))] * 2 + [pltpu.HBM(b.shape, b.dtype) for b in bufs]
        + [jax.ShapeDtypeStruct((8, LANES), F32)],
        input_output_aliases={i: 2 + i for i in range(n)},
        compiler_params=pltpu.CompilerParams(has_side_effects=EFFECT),
    )(*[_in_hbm(b) for b in bufs])
    return outs[0], outs[1], list(outs[2:2 + n]), outs[-1]


def _forward_wait(send_sems, recv_sems, bufs, after, name):
    n = len(bufs)

    def body(*refs):
        ins, ssem, rsem = refs[:n], refs[n], refs[n + 1]
        x, y, c = _place()
        chips = [(1 - x, y), (x, 1 - y), (1 - x, 1 - y)]
        for a in range(n):
            for j, (px, py) in enumerate(chips):
                coming = ins[a].at[:, pl.ds(4 * px + 2 * py + 1 - c, 1)]
                cp = pltpu.make_async_remote_copy(
                    src_ref=coming, dst_ref=coming, send_sem=ssem.at[3 * a + j], recv_sem=rsem.at[3 * a + j],
                    device_id=(x, y, 1 - c), device_id_type=MESH)
                cp.wait_send()
                cp.wait_recv()

    return pl.pallas_call(
        body,
        name=name,
        in_specs=[HBM] * n + [SEM, SEM, ANY],
        out_specs=[HBM] * n,
        out_shape=[pltpu.HBM(b.shape, b.dtype) for b in bufs],
        input_output_aliases={i: i for i in range(n)},
        compiler_params=pltpu.CompilerParams(has_side_effects=EFFECT),
    )(*bufs, send_sems, recv_sems, after)


def _chips_start(sums, name):
    n = len(sums)

    def body(*refs):
        ssem, rsem = refs[4 * n], refs[4 * n + 1]
        src, land = refs[4 * n + 2:5 * n + 2], refs[5 * n + 2:8 * n + 2]
        token = refs[8 * n + 2]
        x, y, c = _place()
        chips = [(1 - x, y), (x, 1 - y), (1 - x, 1 - y)]
        for a in range(n):
            for k, (px, py) in enumerate(chips):
                pltpu.make_async_remote_copy(
                    src_ref=src[a].at[:, pl.ds(2 * px + py, 1)], dst_ref=land[3 * a + k], send_sem=ssem.at[3 * a + k],
                    recv_sem=rsem.at[3 * a + k], device_id=(px, py, c), device_id_type=MESH).start()
        token[...] = jnp.zeros_like(token)

    lands = []
    for s in sums:
        lands += [lax.empty((s.shape[0], 1) + s.shape[2:], s.dtype) for _ in range(3)]
    outs = pl.pallas_call(
        body,
        name=name,
        in_specs=[HBM] * (4 * n),
        out_specs=[SEM, SEM] + [HBM] * (4 * n) + [TOKEN],
        out_shape=[pltpu.SemaphoreType.DMA((3 * n,))] * 2 + [pltpu.HBM(b.shape, b.dtype) for b in list(sums) + lands]
        + [jax.ShapeDtypeStruct((8, LANES), F32)],
        input_output_aliases={i: 2 + i for i in range(4 * n)},
        compiler_params=pltpu.CompilerParams(has_side_effects=EFFECT),
    )(*[_in_hbm(b) for b in list(sums) + lands])
    return outs[0], outs[1], list(outs[2:2 + n]), list(outs[2 + n:2 + 4 * n]), outs[-1]


def _chips_wait(send_sems, recv_sems, sums, lands, after, name):
    n = len(sums)

    def body(*refs):
        src, land = refs[:n], refs[n:4 * n]
        ssem, rsem = refs[4 * n], refs[4 * n + 1]
        x, y, c = _place()
        chips = [(1 - x, y), (x, 1 - y), (1 - x, 1 - y)]
        for a in range(n):
            for k, (px, py) in enumerate(chips):
                cp = pltpu.make_async_remote_copy(
                    src_ref=src[a].at[:, pl.ds(2 * px + py, 1)], dst_ref=land[3 * a + k], send_sem=ssem.at[3 * a + k],
                    recv_sem=rsem.at[3 * a + k], device_id=(px, py, c), device_id_type=MESH)
                cp.wait_send()
                cp.wait_recv()

    both = list(sums) + list(lands)
    outs = pl.pallas_call(
        body,
        name=name,
        in_specs=[HBM] * (4 * n) + [SEM, SEM, ANY],
        out_specs=[HBM] * (4 * n),
        out_shape=[pltpu.HBM(b.shape, b.dtype) for b in both],
        input_output_aliases={i: i for i in range(4 * n)},
        compiler_params=pltpu.CompilerParams(has_side_effects=EFFECT),
    )(*both, send_sems, recv_sems, after)
    return list(outs[:n]), [list(outs[n + 3 * a:n + 3 * a + 3]) for a in range(n)]


def _sibling_start(grads, name):
    n = len(grads)

    def body(*refs):
        ssem, rsem = refs[2 * n], refs[2 * n + 1]
        src, land = refs[2 * n + 2:3 * n + 2], refs[3 * n + 2:4 * n + 2]
        token = refs[4 * n + 2]
        x, y, c = _place()
        for a in range(n):
            pltpu.make_async_remote_copy(
                src_ref=src[a].at[:, :, pl.ds(1 - c, 1)], dst_ref=land[a], send_sem=ssem.at[a], recv_sem=rsem.at[a],
                device_id=(x, y, 1 - c), device_id_type=MESH).start()
        token[...] = jnp.zeros_like(token)

    lands = [lax.empty(g.shape[:2] + (1,) + g.shape[3:], g.dtype) for g in grads]
    both = list(grads) + lands
    outs = pl.pallas_call(
        body,
        name=name,
        in_specs=[HBM] * (2 * n),
        out_specs=[SEM, SEM] + [HBM] * (2 * n) + [TOKEN],
        out_shape=[pltpu.SemaphoreType.DMA((n,))] * 2 + [pltpu.HBM(b.shape, b.dtype) for b in both]
        + [jax.ShapeDtypeStruct((8, LANES), F32)],
        input_output_aliases={i: 2 + i for i in range(2 * n)},
        compiler_params=pltpu.CompilerParams(has_side_effects=EFFECT),
    )(*[_in_hbm(b) for b in both])
    return outs[0], outs[1], list(outs[2:2 + n]), list(outs[2 + n:2 + 2 * n]), outs[-1]


def _sibling_wait(send_sems, recv_sems, grads, lands, after, name):
    n = len(grads)

    def body(*refs):
        src, land = refs[:n], refs[n:2 * n]
        ssem, rsem = refs[2 * n], refs[2 * n + 1]
        x, y, c = _place()
        for a in range(n):
            cp = pltpu.make_async_remote_copy(
                src_ref=src[a].at[:, :, pl.ds(1 - c, 1)], dst_ref=land[a], send_sem=ssem.at[a], recv_sem=rsem.at[a],
                device_id=(x, y, 1 - c), device_id_type=MESH)
            cp.wait_send()
            cp.wait_recv()

    both = list(grads) + list(lands)
    outs = pl.pallas_call(
        body,
        name=name,
        in_specs=[HBM] * (2 * n) + [SEM, SEM, ANY],
        out_specs=[HBM] * (2 * n),
        out_shape=[pltpu.HBM(b.shape, b.dtype) for b in both],
        input_output_aliases={i: i for i in range(2 * n)},
        compiler_params=pltpu.CompilerParams(has_side_effects=EFFECT),
    )(*both, send_sems, recv_sems, after)
    return list(outs[:n]), list(outs[n:])


_SMALL = ("mix_norm", "q_norm", "k_norm", "sinks", "sgu_ln_g", "sgu_ln_b", "w_spatial", "b_spatial", "ffn_norm")


def _pack_rows(a):
    flat = a.reshape(-1)
    pad = (-flat.shape[0]) % LANES
    if pad:
        flat = jnp.pad(flat, (0, pad))
    return flat.reshape(-1, LANES)


def _pack(values):
    rows = jnp.concatenate([_pack_rows(values[k]) for k in _SMALL], axis=0)
    pad = (-rows.shape[0]) % 8
    if pad:
        rows = jnp.pad(rows, ((0, pad), (0, 0)))
    return rows


def _unpack(rows, like):
    out, at = {}, 0
    for k in _SMALL:
        size = like[k].size
        nrows = -(-size // LANES)
        out[k] = rows[at:at + nrows].reshape(-1)[:size].reshape(like[k].shape)
        at += nrows
    return out


def _rope_tables(t, wq):
    pos = jnp.arange(t, dtype=F32)
    inv_freq = jnp.power(ROPE_THETA, -jnp.arange(0, HEAD_DIM, 2, dtype=F32) / HEAD_DIM)
    ang = pos[:, None] * inv_freq[None, :]
    cos, sin = jnp.cos(ang), jnp.sin(ang)
    reps = wq // HEAD_DIM
    return (jnp.tile(jnp.concatenate([cos, cos], axis=1), (1, reps)),
            jnp.tile(jnp.concatenate([-sin, sin], axis=1), (1, reps)))


def kernel(x, mix_norm, w_in, q_norm, k_norm, sinks, sgu_ln_g, sgu_ln_b, w_spatial, b_spatial, w_attn_branch, w_sgu_branch, w_out, ffn_norm, w_gate, w_up, w_down, loss_target, m_mix_norm, m_w_in, m_q_norm, m_k_norm, m_sinks, m_sgu_ln_g, m_sgu_ln_b, m_w_spatial, m_b_spatial, m_w_attn_branch, m_w_sgu_branch, m_w_out, m_ffn_norm, m_w_gate, m_w_up, m_w_down, v_mix_norm, v_w_in, v_q_norm, v_k_norm, v_sinks, v_sgu_ln_g, v_sgu_ln_b, v_w_spatial, v_b_spatial, v_w_attn_branch, v_w_sgu_branch, v_w_out, v_ffn_norm, v_w_gate, v_w_up, v_w_down):
    names = ("mix_norm", "w_in", "q_norm", "k_norm", "sinks", "sgu_ln_g", "sgu_ln_b", "w_spatial", "b_spatial",
             "w_attn_branch", "w_sgu_branch", "w_out", "ffn_norm", "w_gate", "w_up", "w_down")
    weights = dict(zip(names, (mix_norm, w_in, q_norm, k_norm, sinks, sgu_ln_g, sgu_ln_b, w_spatial, b_spatial,
                               w_attn_branch, w_sgu_branch, w_out, ffn_norm, w_gate, w_up, w_down)))
    mom1 = dict(zip(names, (m_mix_norm, m_w_in, m_q_norm, m_k_norm, m_sinks, m_sgu_ln_g, m_sgu_ln_b, m_w_spatial,
                            m_b_spatial, m_w_attn_branch, m_w_sgu_branch, m_w_out, m_ffn_norm, m_w_gate, m_w_up,
                            m_w_down)))
    mom2 = dict(zip(names, (v_mix_norm, v_w_in, v_q_norm, v_k_norm, v_sinks, v_sgu_ln_g, v_sgu_ln_b, v_w_spatial,
                            v_b_spatial, v_w_attn_branch, v_w_sgu_branch, v_w_out, v_ffn_norm, v_w_gate, v_w_up,
                            v_w_down)))
    depth = w_in.shape[0]
    _, t, d = x.shape
    n_q_heads = sinks.shape[1]
    wq = n_q_heads * HEAD_DIM
    wk = wq // Q_PER_KV
    ws = sgu_ln_g.shape[1]
    ng = ws // LANES
    off_u = wq + 2 * wk
    off_g = off_u + 2 * ws
    tables = _rope_tables(t, wq)
    px, py, pc = _place()
    core = pc.astype(jnp.int32)[None]
    chip = (2 * px + py).astype(jnp.int32)[None]
    dev = (4 * px + 2 * py + pc).astype(jnp.int32)[None]

    layers = range(depth)
    chunks = ((0,), (1, 2, 3), (4,), (5,))
    pending, token = [], None
    sources = [[jnp.swapaxes(w_in, 1, 2)], [jnp.swapaxes(w_attn_branch, 1, 2)], [jnp.swapaxes(w_sgu_branch, 1, 2)],
               [w_out], [jnp.swapaxes(w_gate, 1, 2), jnp.swapaxes(w_up, 1, 2)], [w_down]]
    for l in layers:
        bufs = [_place_shard(parts, l, dev, BF16, f"place_shard_{l}_{a}", after=token if a == 0 else None)
                for a, parts in enumerate(sources)]
        started, token = _gather_start([[bufs[a] for a in chunk] for chunk in chunks], f"gather_start_{l}")
        pending.append(started)

    passing = {}

    def arrive(l, ci, after):
        send_sems, recv_sems, bufs = pending[l][ci]
        bufs = _gather_wait(send_sems, recv_sems, bufs, after, f"gather_wait_{l}_{ci}")
        passing[l, ci] = _forward_start(bufs, f"forward_start_{l}_{ci}")
        return passing[l, ci][3]

    def ready(l, ci, after):
        send_sems, recv_sems, bufs, _ = passing.pop((l, ci))
        bufs = _forward_wait(send_sems, recv_sems, bufs, after, f"forward_wait_{l}_{ci}")
        return [f.reshape(f.shape[0] * f.shape[1] * f.shape[2], f.shape[3]) for f in bufs]

    saved = []
    xl = x[0]
    going = arrive(0, 0, token)
    for l in layers:
        gq = jnp.tile(q_norm[l], n_q_heads)[None]
        gk = jnp.tile(k_norm[l], n_q_heads // Q_PER_KV)[None]
        bt = b_spatial[l].T
        h = _rmsnorm_fwd(xl, mix_norm[l][None], f"mix_norm_fwd_{l}", after=going)
        (win_t,) = ready(l, 0, h)
        proj = _mm(h, win_t, "nt", F32, f"in_proj_{l}")
        going = arrive(l, 1, proj)
        attn = _attn_fwd(proj, tables, gq, gk, sinks[l], wq, wk, f"attn_fwd_{l}", after=going)
        sgu = _sgu_fwd(proj, sgu_ln_g[l][None], sgu_ln_b[l][None], w_spatial[l], bt, off_u, ws, f"sgu_fwd_{l}")
        wab_t, wsb_t, wo = ready(l, 1, sgu)
        br_a = _mm(attn, wab_t, "nt", F32, f"attn_branch_{l}")
        br_b = _mm(sgu, wsb_t, "nt", F32, f"sgu_branch_{l}")
        merged = _merge_fwd(br_a, br_b, proj, off_g, f"merge_fwd_{l}")
        x1 = _mm(merged, wo, "nn", F32, f"out_proj_{l}", residual=xl)
        going = arrive(l, 2, x1)
        h2 = _rmsnorm_fwd(x1, ffn_norm[l][None], f"ffn_norm_fwd_{l}", after=going)
        (wgu_t,) = ready(l, 2, h2)
        gu = _mm(h2, wgu_t, "nt", F32, f"gate_up_{l}")
        going = arrive(l, 3, gu)
        if l + 1 < depth:
            going = arrive(l + 1, 0, going)
        act = _swiglu_fwd(gu, f"swiglu_fwd_{l}", after=going)
        (wd,) = ready(l, 3, act)
        x2 = _mm(act, wd, "nn", F32, f"down_proj_{l}", residual=x1)
        saved.append(dict(x0=xl, h=h, proj=proj, attn=attn, sgu=sgu, br_a=br_a, br_b=br_b, merged=merged, x1=x1,
                          h2=h2, gu=gu, act=act, gq=gq, gk=gk, bt=bt, win_t=win_t, wab_t=wab_t, wsb_t=wsb_t, wo=wo,
                          wgu_t=wgu_t, wd=wd))
        xl = x2

    loss_part, dx = _loss_and_grad(xl, loss_target[0], "loss")
    loss = lax.psum(loss_part[0, 0], ("x", "y", "c"))

    def sibling_start(grads, tag):
        shaped = []
        for g, p in grads:
            rows, c = g.shape
            shaped.append(g.reshape(p, 4, 2, rows // (8 * p), c))
        send_sems, recv_sems, shaped, lands, tok = _sibling_start(shaped, f"rs_sibling_start_{tag}")
        return (send_sems, recv_sems, shaped, lands, tag), tok

    def chips_start(state, after):
        send_sems, recv_sems, shaped, lands, tag = state
        shaped, lands = _sibling_wait(send_sems, recv_sems, shaped, lands, after, f"rs_sibling_wait_{tag}")
        sums = [_sum_sibling(g, o, core, f"rs_add_sibling_{tag}_{a}") for a, (g, o) in enumerate(zip(shaped, lands))]
        send_sems, recv_sems, sums, lands, tok = _chips_start(sums, f"rs_chips_start_{tag}")
        return (send_sems, recv_sems, sums, lands, tag), tok

    def scatter_finish(state, after):
        send_sems, recv_sems, sums, lands, tag = state
        sums, lands = _chips_wait(send_sems, recv_sems, sums, lands, after, f"rs_chips_wait_{tag}")
        return [_sum_chips(s, o, chip, f"rs_add_chips_{tag}_{a}") for a, (s, o) in enumerate(zip(sums, lands))]

    in_flight = [dict() for _ in layers]
    small_grads = [None] * depth
    tok, swap_in = None, None
    for l in reversed(layers):
        s = saved[l]
        dx16 = dx.astype(BF16)
        dact = _mm(dx16, s["wd"], "nt", F32, f"d_act_{l}", after=tok)
        if swap_in is not None:
            in_flight[l + 1]["in"], tok = chips_start(swap_in, dact)
        g_wd = _mm(s["act"], dx16, "tn", BF16, f"g_w_down_{l}", after=tok)
        swap, tok_s = sibling_start([(g_wd, 1)], f"{l}_down")
        dgu = _swiglu_bwd(s["gu"], dact, f"swiglu_bwd_{l}")
        dh2 = _mm(dgu, s["wgu_t"], "nn", F32, f"d_h2_{l}", after=tok_s)
        in_flight[l]["down"], tok = chips_start(swap, dh2)
        g_wgu_t = _mm(dgu, s["h2"], "tn", BF16, f"g_w_gate_up_{l}", after=tok)
        swap, tok_s = sibling_start([(g_wgu_t, 2)], f"{l}_gate_up")
        dx1, g_ffn = _rmsnorm_bwd(s["x1"], ffn_norm[l][None], dh2, dx, f"ffn_norm_bwd_{l}")
        dx1_16 = dx1.astype(BF16)
        dmerged = _mm(dx1_16, s["wo"], "nt", F32, f"d_merged_{l}", after=tok_s)
        in_flight[l]["gate_up"], tok = chips_start(swap, dmerged)
        g_wo = _mm(s["merged"], dx1_16, "tn", BF16, f"g_w_out_{l}", after=tok)
        d_a, d_b, dla, dlb = _merge_bwd(s["br_a"], s["br_b"], s["proj"], dmerged, off_g, f"merge_bwd_{l}")
        dattn = _mm(d_a, s["wab_t"], "nn", F32, f"d_attn_{l}")
        g_wab_t = _mm(d_a, s["attn"], "tn", BF16, f"g_w_attn_branch_{l}")
        dsgu = _mm(d_b, s["wsb_t"], "nn", F32, f"d_sgu_{l}")
        g_wsb_t = _mm(d_b, s["sgu"], "tn", BF16, f"g_w_sgu_branch_{l}")
        swap, tok_s = sibling_start([(g_wab_t, 1), (g_wsb_t, 1), (g_wo, 1)], f"{l}_mix")
        dq, dk, dv, g_gq, g_gk, g_sinks = _attn_bwd(s["proj"], dattn, tables, s["gq"], s["gk"], sinks[l], wq, wk,
                                                    f"attn_bwd_{l}")
        du, dvv, g_lng, g_lnb, g_ws, g_bs = _sgu_bwd(s["proj"], dsgu, sgu_ln_g[l][None], sgu_ln_b[l][None],
                                                     w_spatial[l], s["bt"], off_u, ws, f"sgu_bwd_{l}")
        dproj = jnp.concatenate([dq, dk.astype(BF16), dv.astype(BF16), du, dvv, dla, dlb], axis=1)
        dh = _mm(dproj, s["win_t"], "nn", F32, f"d_h_{l}", after=tok_s)
        in_flight[l]["mix"], tok = chips_start(swap, dh)
        g_win_t = _mm(dproj, s["h"], "tn", BF16, f"g_w_in_{l}", after=tok)
        swap_in, tok = sibling_start([(g_win_t, 1)], f"{l}_in")
        dx, g_mix = _rmsnorm_bwd(s["x0"], mix_norm[l][None], dh, dx1, f"mix_norm_bwd_{l}")
        if l == 0:
            in_flight[0]["in"], _ = chips_start(swap_in, dx)
        small_grads[l] = dict(
            mix_norm=g_mix[0], q_norm=g_gq[0].reshape(n_q_heads, HEAD_DIM).sum(0),
            k_norm=g_gk[0].reshape(n_q_heads // Q_PER_KV, HEAD_DIM).sum(0), sinks=g_sinks[0, :n_q_heads],
            sgu_ln_g=g_lng[0], sgu_ln_b=g_lnb[0], w_spatial=g_ws, b_spatial=g_bs[:, 0, :], ffn_norm=g_ffn[0])
    grad_x = dx[None]

    result = {key: {} for key in ("grad", "delta", "m", "v")}
    layer_like = {k: weights[k][0] for k in _SMALL}
    packed_g = jnp.concatenate([_pack(small_grads[l]) for l in layers], axis=0)
    rows_per_layer = packed_g.shape[0] // depth
    small_buf = _place_shard([packed_g[None]], 0, dev, F32, "place_small_grads")
    gathered_small = _all_gather([small_buf], "all_gather_small_grads")[0][0]
    packed = [jnp.concatenate([_pack({k: src[k][l] for k in _SMALL}) for l in layers], axis=0)
              for src in (weights, mom1, mom2)]
    small = _small_reduce_adam(gathered_small, *packed, "small_reduce_adam")
    for key, rows in zip(("grad", "delta", "m", "v"), small):
        per_layer = [_unpack(rows[l * rows_per_layer:(l + 1) * rows_per_layer], layer_like) for l in layers]
        for k in _SMALL:
            result[key][k] = jnp.stack([per_layer[l][k] for l in layers])

    def update(k, grads, transposed):
        view = (lambda a: jnp.swapaxes(a, 1, 2)) if transposed else (lambda a: a)
        outs = _adam(view(weights[k]), grads, view(mom1[k]), view(mom2[k]), f"adam_{k}")
        for key, val in zip(("grad", "delta", "m", "v"), outs):
            result[key][k] = view(val)

    done = {name: [scatter_finish(in_flight[l][name], small[0]) for l in layers] for name in ("down", "gate_up", "mix")}
    update("w_down", [(done["down"][l][0], 0) for l in layers], False)
    update("w_gate", [(done["gate_up"][l][0], 0) for l in layers], True)
    update("w_up", [(done["gate_up"][l][0], 1) for l in layers], True)
    update("w_attn_branch", [(done["mix"][l][0], 0) for l in layers], True)
    update("w_sgu_branch", [(done["mix"][l][1], 0) for l in layers], True)
    update("w_out", [(done["mix"][l][2], 0) for l in layers], False)
    last = [scatter_finish(in_flight[l]["in"], result["v"]["w_out"]) for l in layers]
    update("w_in", [(last[l][0], 0) for l in layers], True)

    return (loss, grad_x, *[result["grad"][k] for k in names], *[result["delta"][k] for k in names],
            *[result["m"][k] for k in names], *[result["v"][k] for k in names])
```

```python
import functools
import math

import jax
import jax.numpy as jnp
from jax import lax
from jax.experimental import pallas as pl
from jax.experimental.pallas import tpu as pltpu

F32 = jnp.float32
BF16 = jnp.bfloat16
MESH = pl.DeviceIdType.MESH
ANY = pl.BlockSpec(memory_space=pl.ANY)

N_DEV = 8
HEAD_DIM = 64
Q_PER_KV = 4
BLOCK = 128
LANES = 128
ROPE_THETA = 10000.0
EPS = 1e-6
ADAM_LR = 0.001
ADAM_B1 = 0.9
ADAM_B2 = 0.999
ADAM_EPS = 1e-08
ADAM_WD = 0.01
ADAM_STEP = 10
NEG = -1e30
VMEM_LIMIT_BYTES = 56 * 1024 * 1024

NN = ((1,), (0,))
NT = ((1,), (1,))
TN = ((0,), (0,))


def _dot(a, b, dims):
    return lax.dot_general(a, b, (dims, ((), ())), preferred_element_type=F32)


def _params(*sem):
    return pltpu.CompilerParams(dimension_semantics=sem, vmem_limit_bytes=VMEM_LIMIT_BYTES)


def _divisor_tile(n, limit, unit):
    if n <= limit:
        return n
    best = unit
    for t in range(unit, limit + 1, unit):
        if n % t == 0:
            best = t
    assert n % best == 0, (n, limit, unit)
    return best


def _mm(a, b, mode, out_dtype, name, residual=None, after=None):
    parts = a.shape[0] if a.ndim == 3 else 1
    a2 = a.shape[-2:]
    if mode == "nn":
        (m, kp), (k2, n) = a2, b.shape
        k, mp = kp * parts, m
    elif mode == "nt":
        (m, kp), (n, k2) = a2, b.shape
        k, mp = kp * parts, m
    else:
        (k, mp), (k2, n) = a2, b.shape
        m, kp = mp * parts, k
    assert k == k2, (name, a.shape, b.shape)
    tk = _divisor_tile(kp, 2816, 128)
    nk = k // tk
    tm = _divisor_tile(mp, 512 if mode == "tn" else 1024, 128)
    tn = _divisor_tile(n, 2048 if mode == "tn" else (1024 if nk > 1 else 512), 128)
    kpb, mpb = kp // tk, mp // tm
    dims = {"nn": NN, "nt": NT, "tn": TN}[mode]
    lead = (None,) if a.ndim == 3 else ()
    if mode == "tn":
        a_index = lambda i, j, kk: (i // mpb, kk, i % mpb) if lead else (kk, i)
        a_spec = pl.BlockSpec(lead + (tk, tm), a_index)
    else:
        a_index = lambda i, j, kk: (kk // kpb, i, kk % kpb) if lead else (i, kk)
        a_spec = pl.BlockSpec(lead + (tm, tk), a_index)
    if mode == "nt":
        b_spec = pl.BlockSpec((tn, tk), lambda i, j, kk: (j, kk))
    else:
        b_spec = pl.BlockSpec((tk, tn), lambda i, j, kk: (kk, j))
    o_spec = pl.BlockSpec((tm, tn), lambda i, j, kk: (i, j))
    has_res = residual is not None

    def body(*refs):
        a_ref, b_ref = refs[:2]
        r_ref = refs[2] if has_res else None
        o_ref, acc_ref = refs[-2:]
        kk = pl.program_id(2)
        p = _dot(a_ref[...], b_ref[...], dims)

        def finish(total):
            if has_res:
                total = total + r_ref[...]
            o_ref[...] = total.astype(o_ref.dtype)

        if nk == 1:
            finish(p)
        else:
            @pl.when(kk == 0)
            def _():
                acc_ref[...] = p

            @pl.when(jnp.logical_and(kk > 0, kk < nk - 1))
            def _():
                acc_ref[...] += p

            @pl.when(kk == nk - 1)
            def _():
                finish(acc_ref[...] + p)

    in_specs = [a_spec, b_spec] + ([o_spec] if has_res else []) + ([ANY] if after is not None else [])
    args = (a, b) + ((residual,) if has_res else ()) + ((after,) if after is not None else ())
    acc_shape = (tm, tn) if nk > 1 else (8, LANES)
    return pl.pallas_call(
        body,
        name=name,
        grid=(m // tm, n // tn, nk),
        in_specs=in_specs,
        out_specs=o_spec,
        out_shape=jax.ShapeDtypeStruct((m, n), out_dtype),
        scratch_shapes=[pltpu.VMEM(acc_shape, F32)],
        compiler_params=_params("parallel", "parallel", "arbitrary"),
    )(*args)


def _rmsnorm_fwd(x, g, name, after=None):
    t, d = x.shape
    tr = _divisor_tile(t, 256, 8)

    def body(x_ref, g_ref, *rest):
        h_ref = rest[-1]
        xv = x_ref[...]
        rstd = lax.rsqrt(jnp.mean(xv * xv, axis=-1, keepdims=True) + EPS)
        h_ref[...] = (xv * rstd * g_ref[...]).astype(h_ref.dtype)

    return pl.pallas_call(
        body,
        name=name,
        grid=(t // tr,),
        in_specs=[pl.BlockSpec((tr, d), lambda i: (i, 0)), pl.BlockSpec((1, d), lambda i: (0, 0))]
        + ([ANY] if after is not None else []),
        out_specs=pl.BlockSpec((tr, d), lambda i: (i, 0)),
        out_shape=jax.ShapeDtypeStruct((t, d), BF16),
        compiler_params=_params("parallel"),
    )(x, g, *(() if after is None else (after,)))


def _rmsnorm_bwd(x, g, dh, dres, name):
    t, d = x.shape
    tr = _divisor_tile(t, 256, 8)

    def body(x_ref, g_ref, dh_ref, dres_ref, dx_ref, dx16_ref, dg_ref):
        i = pl.program_id(0)
        xv = x_ref[...]
        rstd = lax.rsqrt(jnp.mean(xv * xv, axis=-1, keepdims=True) + EPS)
        xh = xv * rstd
        dhv = dh_ref[...]
        dxh = dhv * g_ref[...]
        dx = dres_ref[...] + rstd * (dxh - xh * jnp.mean(dxh * xh, axis=-1, keepdims=True))
        dx_ref[...] = dx
        dx16_ref[...] = dx.astype(dx16_ref.dtype)
        part = jnp.broadcast_to(jnp.sum(dhv * xh, axis=0, keepdims=True), dg_ref.shape)

        @pl.when(i == 0)
        def _():
            dg_ref[...] = part

        @pl.when(i > 0)
        def _():
            dg_ref[...] += part

    row = pl.BlockSpec((tr, d), lambda i: (i, 0))
    return pl.pallas_call(
        body,
        name=name,
        grid=(t // tr,),
        in_specs=[row, pl.BlockSpec((1, d), lambda i: (0, 0)), row, row],
        out_specs=[row, row, pl.BlockSpec((8, d), lambda i: (0, 0))],
        out_shape=[jax.ShapeDtypeStruct((t, d), F32), jax.ShapeDtypeStruct((t, d), BF16),
                   jax.ShapeDtypeStruct((8, d), F32)],
        compiler_params=_params("arbitrary"),
    )(x, g, dh, dres)


def _lane(shape):
    return lax.broadcasted_iota(jnp.int32, shape, 1)


def _group_sum64(s):
    row = lax.broadcasted_iota(jnp.int32, (LANES, LANES), 0)
    col = lax.broadcasted_iota(jnp.int32, (LANES, LANES), 1)
    ones = jnp.where((row >= HEAD_DIM) == (col >= HEAD_DIM), 1.0, 0.0).astype(BF16)
    out = []
    for t in range(s.shape[1] // LANES):
        piece = s[:, LANES * t:LANES * t + LANES]
        hi = piece.astype(BF16)
        lo = (piece - hi.astype(F32)).astype(BF16)
        out.append(_dot(hi, ones, NN) + _dot(lo, ones, NN))
    return out[0] if len(out) == 1 else jnp.concatenate(out, axis=1)


def _swap32(x):
    w = x.shape[1]
    return jnp.where((_lane(x.shape) & 32) == 0, pltpu.roll(x, w - 32, axis=1), pltpu.roll(x, 32, axis=1))


def _rope(x, c, s):
    return x * c + _swap32(x) * s


def _rope_t(dy, c, s):
    return dy * c + _swap32(dy * s)


def _head_norm(x):
    rstd = lax.rsqrt(_group_sum64(x * x) * (1.0 / HEAD_DIM) + EPS)
    return x * rstd, rstd


def _head_norm_bwd(dxh, xh, rstd):
    return rstd * (dxh - xh * (_group_sum64(dxh * xh) * (1.0 / HEAD_DIM)))


def _roll64(x):
    return pltpu.roll(x, 64, axis=1)


def _attn_specs(wq, wk):
    kb = wq // wk
    prev = lambda i: jnp.maximum(i - 1, 0)
    return dict(
        q=pl.BlockSpec((BLOCK, wq), lambda i: (i, 0)),
        kc=pl.BlockSpec((BLOCK, wk), lambda i: (i, kb)),
        kp=pl.BlockSpec((BLOCK, wk), lambda i: (prev(i), kb)),
        vc=pl.BlockSpec((BLOCK, wk), lambda i: (i, kb + 1)),
        vp=pl.BlockSpec((BLOCK, wk), lambda i: (prev(i), kb + 1)),
        tq=pl.BlockSpec((BLOCK, wq), lambda i: (i, 0)),
        tkp=pl.BlockSpec((BLOCK, wk), lambda i: (prev(i), 0)),
        gq=pl.BlockSpec((1, wq), lambda i: (0, 0)),
        gk=pl.BlockSpec((1, wk), lambda i: (0, 0)),
        sinks=pl.BlockSpec(memory_space=pltpu.SMEM),
    )


def _attn_prologue(i, q_ref, kc_ref, kp_ref, cq_ref, sq_ref, ckp_ref, skp_ref, gq_ref, gk_ref):
    wk = kc_ref.shape[1]
    cq, sq = cq_ref[...], sq_ref[...]
    ck, sk = cq[:, :wk], sq[:, :wk]
    qh, q_rstd = _head_norm(q_ref[...])
    kch, kc_rstd = _head_norm(kc_ref[...])
    kph, kp_rstd = _head_norm(kp_ref[...])
    qn = _rope(qh * gq_ref[...], cq, sq)
    knc = _rope(kch * gk_ref[...], ck, sk)
    knp = _rope(kph * gk_ref[...], ckp_ref[...], skp_ref[...])
    row = lax.broadcasted_iota(jnp.int32, (BLOCK, BLOCK), 0)
    col = lax.broadcasted_iota(jnp.int32, (BLOCK, BLOCK), 1)
    mask_c = col <= row
    mask_p = jnp.logical_and(col > row, i > 0)
    half = (col >= 64).astype(jnp.int32)
    return dict(cq=cq, sq=sq, ck=ck, sk=sk, qh=qh, q_rstd=q_rstd, kch=kch, kc_rstd=kc_rstd, kph=kph,
                kp_rstd=kp_rstd, qn=qn, knc=knc, knp=knp, mask_c=mask_c, mask_p=mask_p, half=half)


def _head_scores(st, t, e, sink, scale):
    g = (2 * t) // Q_PER_KV
    ks, kpar = g // 2, g % 2
    sl = slice(LANES * ks, LANES * ks + LANES)
    mine = st["half"] == e
    qm = jnp.where(mine, st["qn"][:, LANES * t:LANES * t + LANES], 0.0).astype(BF16)
    kc, kp = st["knc"][:, sl], st["knp"][:, sl]
    flip = e != kpar
    if flip:
        kc, kp = _roll64(kc), _roll64(kp)
    kc, kp = kc.astype(BF16), kp.astype(BF16)
    s_c = jnp.where(st["mask_c"], _dot(qm, kc, NT) * scale, NEG)
    s_p = jnp.where(st["mask_p"], _dot(qm, kp, NT) * scale, NEG)
    m = jnp.maximum(jnp.maximum(jnp.max(s_c, axis=1, keepdims=True), jnp.max(s_p, axis=1, keepdims=True)), sink)
    p_c, p_p = jnp.exp(s_c - m), jnp.exp(s_p - m)
    p_s = jnp.exp(sink - m)
    inv = 1.0 / (jnp.sum(p_c, axis=1, keepdims=True) + jnp.sum(p_p, axis=1, keepdims=True) + p_s)
    return dict(sl=sl, mine=mine, flip=flip, qm=qm, kc=kc, kp=kp, pr_c=p_c * inv, pr_p=p_p * inv, pr_s=p_s * inv)


def _attn_fwd(proj, tables, gq, gk, sinks, wq, wk, name, after=None):
    t = proj.shape[0]
    nb = t // BLOCK
    sp = _attn_specs(wq, wk)
    scale = HEAD_DIM ** -0.5
    cos_t, sin_t = tables

    def body(sinks_ref, q_ref, kc_ref, kp_ref, vc_ref, vp_ref, cq_ref, sq_ref, ckp_ref, skp_ref, gq_ref, gk_ref,
             *rest):
        o_ref = rest[-1]
        i = pl.program_id(0)
        st = _attn_prologue(i, q_ref, kc_ref, kp_ref, cq_ref, sq_ref, ckp_ref, skp_ref, gq_ref, gk_ref)
        vc_all, vp_all = vc_ref[...], vp_ref[...]
        for ts in range(wq // LANES):
            acc = jnp.zeros((BLOCK, LANES), F32)
            for e in (0, 1):
                hs = _head_scores(st, ts, e, sinks_ref[2 * ts + e], scale)
                vc, vp = vc_all[:, hs["sl"]], vp_all[:, hs["sl"]]
                if hs["flip"]:
                    vc, vp = _roll64(vc), _roll64(vp)
                vc = jnp.where(hs["mine"], vc, 0.0).astype(BF16)
                vp = jnp.where(hs["mine"], vp, 0.0).astype(BF16)
                acc = acc + _dot(hs["pr_c"].astype(BF16), vc, NN) + _dot(hs["pr_p"].astype(BF16), vp, NN)
            o_ref[:, LANES * ts:LANES * ts + LANES] = acc.astype(o_ref.dtype)

    return pl.pallas_call(
        body,
        name=name,
        grid=(nb,),
        in_specs=[sp["sinks"], sp["q"], sp["kc"], sp["kp"], sp["vc"], sp["vp"], sp["tq"], sp["tq"], sp["tkp"],
                  sp["tkp"], sp["gq"], sp["gk"]] + ([ANY] if after is not None else []),
        out_specs=pl.BlockSpec((BLOCK, wq), lambda i: (i, 0)),
        out_shape=jax.ShapeDtypeStruct((t, wq), BF16),
        compiler_params=_params("parallel"),
    )(sinks, proj, proj, proj, proj, proj, cos_t, sin_t, cos_t, sin_t, gq, gk, *(() if after is None else (after,)))


def _attn_bwd(proj, dout, tables, gq, gk, sinks, wq, wk, name):
    t = proj.shape[0]
    nb = t // BLOCK
    sp = _attn_specs(wq, wk)
    scale = HEAD_DIM ** -0.5
    cos_t, sin_t = tables

    def body(sinks_ref, q_ref, kc_ref, kp_ref, vc_ref, vp_ref, cq_ref, sq_ref, ckp_ref, skp_ref, gq_ref, gk_ref,
             do_ref, dq_ref, dk_ref, dv_ref, dgq_ref, dgk_ref, dsk_ref, dqn_ref, dknc_ref, dknp_ref, dvc_ref,
             dvp_ref):
        i = pl.program_id(0)
        st = _attn_prologue(i, q_ref, kc_ref, kp_ref, cq_ref, sq_ref, ckp_ref, skp_ref, gq_ref, gk_ref)
        vc_all, vp_all = vc_ref[...], vp_ref[...]
        dknc_ref[...] = jnp.zeros_like(dknc_ref)
        dknp_ref[...] = jnp.zeros_like(dknp_ref)
        dvc_ref[...] = jnp.zeros_like(dvc_ref)
        dvp_ref[...] = jnp.zeros_like(dvp_ref)
        lane8 = _lane((8, LANES))
        dsinks = jnp.zeros((8, LANES), F32)
        for ts in range(wq // LANES):
            dq_acc = jnp.zeros((BLOCK, LANES), F32)
            for e in (0, 1):
                hs = _head_scores(st, ts, e, sinks_ref[2 * ts + e], scale)
                sl, flip = hs["sl"], hs["flip"]
                vc, vp = vc_all[:, sl], vp_all[:, sl]
                if flip:
                    vc, vp = _roll64(vc), _roll64(vp)
                dom = jnp.where(hs["mine"], do_ref[:, LANES * ts:LANES * ts + LANES], 0.0).astype(BF16)
                dp_c = _dot(dom, vc.astype(BF16), NT)
                dp_p = _dot(dom, vp.astype(BF16), NT)
                pr_c, pr_p = hs["pr_c"], hs["pr_p"]
                rs = jnp.sum(pr_c * dp_c, axis=1, keepdims=True) + jnp.sum(pr_p * dp_p, axis=1, keepdims=True)
                ds_c = (pr_c * (dp_c - rs) * scale)
                ds_p = (pr_p * (dp_p - rs) * scale)
                dsink = jnp.sum(-hs["pr_s"] * rs)
                dsinks = dsinks + jnp.where(lane8 == 2 * ts + e, dsink, 0.0)
                dq_acc = dq_acc + jnp.where(
                    hs["mine"], _dot(ds_c.astype(BF16), hs["kc"], NN) + _dot(ds_p.astype(BF16), hs["kp"], NN), 0.0)
                dv_c = _dot(pr_c.T.astype(BF16), dom, NN)
                dv_p = _dot(pr_p.T.astype(BF16), dom, NN)
                dk_c = _dot(ds_c.T.astype(BF16), hs["qm"], NN)
                dk_p = _dot(ds_p.T.astype(BF16), hs["qm"], NN)
                if flip:
                    dv_c, dv_p, dk_c, dk_p = _roll64(dv_c), _roll64(dv_p), _roll64(dk_c), _roll64(dk_p)
                dvc_ref[:, sl] += dv_c
                dvp_ref[:, sl] += dv_p
                dknc_ref[:, sl] += dk_c
                dknp_ref[:, sl] += dk_p
            dqn_ref[:, LANES * ts:LANES * ts + LANES] = dq_acc

        gqv, gkv = gq_ref[...], gk_ref[...]
        dqg = _rope_t(dqn_ref[...], st["cq"], st["sq"])
        dq_ref[...] = _head_norm_bwd(dqg * gqv, st["qh"], st["q_rstd"]).astype(dq_ref.dtype)
        dkcg = _rope_t(dknc_ref[...], st["ck"], st["sk"])
        dkpg = _rope_t(dknp_ref[...], ckp_ref[...], skp_ref[...])
        dk_cur = _head_norm_bwd(dkcg * gkv, st["kch"], st["kc_rstd"])
        dk_prev = _head_norm_bwd(dkpg * gkv, st["kph"], st["kp_rstd"])
        dgq_part = jnp.broadcast_to(jnp.sum(dqg * st["qh"], axis=0, keepdims=True), dgq_ref.shape)
        dgk_part = jnp.broadcast_to(
            jnp.sum(dkcg * st["kch"] + dkpg * st["kph"], axis=0, keepdims=True), dgk_ref.shape)
        cur = pl.ds(pl.multiple_of(i * BLOCK, BLOCK), BLOCK)
        dk_ref[cur, :] = dk_cur
        dv_ref[cur, :] = dvc_ref[...]

        @pl.when(i == 0)
        def _():
            dgq_ref[...] = dgq_part
            dgk_ref[...] = dgk_part
            dsk_ref[...] = dsinks

        @pl.when(i > 0)
        def _():
            before = pl.ds(pl.multiple_of((i - 1) * BLOCK, BLOCK), BLOCK)
            dk_ref[before, :] += dk_prev
            dv_ref[before, :] += dvp_ref[...]
            dgq_ref[...] += dgq_part
            dgk_ref[...] += dgk_part
            dsk_ref[...] += dsinks

    whole = lambda shape: pl.BlockSpec(shape, lambda i: (0, 0))
    return pl.pallas_call(
        body,
        name=name,
        grid=(nb,),
        in_specs=[sp["sinks"], sp["q"], sp["kc"], sp["kp"], sp["vc"], sp["vp"], sp["tq"], sp["tq"], sp["tkp"],
                  sp["tkp"], sp["gq"], sp["gk"], pl.BlockSpec((BLOCK, wq), lambda i: (i, 0))],
        out_specs=[pl.BlockSpec((BLOCK, wq), lambda i: (i, 0)), whole((t, wk)), whole((t, wk)), whole((8, wq)),
                   whole((8, wk)), whole((8, LANES))],
        out_shape=[jax.ShapeDtypeStruct((t, wq), BF16), jax.ShapeDtypeStruct((t, wk), F32),
                   jax.ShapeDtypeStruct((t, wk), F32), jax.ShapeDtypeStruct((8, wq), F32),
                   jax.ShapeDtypeStruct((8, wk), F32), jax.ShapeDtypeStruct((8, LANES), F32)],
        scratch_shapes=[pltpu.VMEM((BLOCK, wq), F32), pltpu.VMEM((BLOCK, wk), F32), pltpu.VMEM((BLOCK, wk), F32),
                        pltpu.VMEM((BLOCK, wk), F32), pltpu.VMEM((BLOCK, wk), F32)],
        compiler_params=_params("arbitrary"),
    )(sinks, proj, proj, proj, proj, proj, cos_t, sin_t, cos_t, sin_t, gq, gk, dout)


_GELU_K = math.sqrt(2.0 / math.pi)
_GELU_A = 0.044715


def _gelu(x):
    return 0.5 * x * (1.0 + jnp.tanh(_GELU_K * (x + _GELU_A * x * x * x)))


def _gelu_grad(x):
    th = jnp.tanh(_GELU_K * (x + _GELU_A * x * x * x))
    return 0.5 * (1.0 + th) + 0.5 * x * (1.0 - th * th) * (_GELU_K * (1.0 + 3.0 * _GELU_A * x * x))


def _group_ln(v):
    mu = jnp.mean(v, axis=1, keepdims=True)
    cen = v - mu
    rstd = lax.rsqrt(jnp.mean(cen * cen, axis=1, keepdims=True) + EPS)
    return cen * rstd, rstd


def _sgu_geometry(off_u, ws):
    cw = math.gcd(off_u, ws)
    return cw, ws // cw, off_u // cw, (off_u + ws) // cw


def _sgu_fwd(proj, ln_g, ln_b, w_s, bt, off_u, ws, name):
    t = proj.shape[0]
    nb = t // BLOCK
    cw, nc, ub, vb = _sgu_geometry(off_u, ws)
    gpc = cw // LANES
    ng = ws // LANES

    def body(u_ref, v_ref, g_ref, b_ref, w_ref, bt_ref, o_ref):
        jc = pl.program_id(0)
        row = lax.broadcasted_iota(jnp.int32, (BLOCK, BLOCK), 0)
        col = lax.broadcasted_iota(jnp.int32, (BLOCK, BLOCK), 1)
        lane_g = _lane((BLOCK, ng))
        for gi in range(gpc):
            sl = slice(LANES * gi, LANES * gi + LANES)
            xh, _ = _group_ln(_gelu(v_ref[:, sl]))
            vn = xh * g_ref[:, sl] + b_ref[:, sl]
            w = jnp.where(row >= col, w_ref[gi], 0.0).astype(BF16)
            bias = jnp.sum(jnp.where(lane_g == jc * gpc + gi, bt_ref[...], 0.0), axis=1, keepdims=True)
            s = _dot(w, vn.astype(BF16), NN) + bias
            o_ref[:, sl] = (_gelu(u_ref[:, sl]) * s).astype(o_ref.dtype)

    return pl.pallas_call(
        body,
        name=name,
        grid=(nc, nb),
        in_specs=[pl.BlockSpec((BLOCK, cw), lambda jc, i: (i, ub + jc)),
                  pl.BlockSpec((BLOCK, cw), lambda jc, i: (i, vb + jc)),
                  pl.BlockSpec((1, cw), lambda jc, i: (0, jc)),
                  pl.BlockSpec((1, cw), lambda jc, i: (0, jc)),
                  pl.BlockSpec((gpc, BLOCK, BLOCK), lambda jc, i: (jc, 0, 0)),
                  pl.BlockSpec((BLOCK, ng), lambda jc, i: (0, 0))],
        out_specs=pl.BlockSpec((BLOCK, cw), lambda jc, i: (i, jc)),
        out_shape=jax.ShapeDtypeStruct((t, ws), BF16),
        compiler_params=_params("parallel", "parallel"),
    )(proj, proj, ln_g, ln_b, w_s, bt)


def _sgu_bwd(proj, dout, ln_g, ln_b, w_s, bt, off_u, ws, name):
    t = proj.shape[0]
    nb = t // BLOCK
    cw, nc, ub, vb = _sgu_geometry(off_u, ws)
    gpc = cw // LANES
    ng = ws // LANES

    def body(u_ref, v_ref, g_ref, b_ref, w_ref, bt_ref, do_ref, du_ref, dv_ref, dg_ref, db_ref, dw_ref, dbs_ref,
             bacc_ref):
        jc = pl.program_id(0)
        i = pl.program_id(1)
        row = lax.broadcasted_iota(jnp.int32, (BLOCK, BLOCK), 0)
        col = lax.broadcasted_iota(jnp.int32, (BLOCK, BLOCK), 1)
        lane_g = _lane((BLOCK, ng))
        tri = row >= col

        @pl.when(i == 0)
        def _():
            dg_ref[...] = jnp.zeros_like(dg_ref)
            db_ref[...] = jnp.zeros_like(db_ref)
            dw_ref[...] = jnp.zeros_like(dw_ref)
            bacc_ref[...] = jnp.zeros_like(bacc_ref)

        for gi in range(gpc):
            sl = slice(LANES * gi, LANES * gi + LANES)
            u_raw, v_raw = u_ref[:, sl], v_ref[:, sl]
            xh, rstd = _group_ln(_gelu(v_raw))
            gam = g_ref[:, sl]
            vn = (xh * gam + b_ref[:, sl]).astype(BF16)
            w = jnp.where(tri, w_ref[gi], 0.0)
            bias = jnp.sum(jnp.where(lane_g == jc * gpc + gi, bt_ref[...], 0.0), axis=1, keepdims=True)
            s = _dot(w.astype(BF16), vn, NN) + bias
            dov = do_ref[:, sl]
            du_ref[:, sl] = (dov * s * _gelu_grad(u_raw)).astype(du_ref.dtype)
            ds = dov * _gelu(u_raw)
            ds16 = ds.astype(BF16)
            dw_ref[gi] += jnp.where(tri, _dot(ds16, vn, NT), 0.0)
            bacc_ref[gi] += ds
            dvn = _dot(w.T.astype(BF16), ds16, NN)
            dg_ref[:, sl] += jnp.broadcast_to(jnp.sum(dvn * xh, axis=0, keepdims=True), (8, LANES))
            db_ref[:, sl] += jnp.broadcast_to(jnp.sum(dvn, axis=0, keepdims=True), (8, LANES))
            dxh = dvn * gam
            dvg = rstd * (dxh - jnp.mean(dxh, axis=1, keepdims=True)
                          - xh * jnp.mean(dxh * xh, axis=1, keepdims=True))
            dv_ref[:, sl] = (dvg * _gelu_grad(v_raw)).astype(dv_ref.dtype)

        @pl.when(i == nb - 1)
        def _():
            for gi in range(gpc):
                dbs_ref[gi] = jnp.broadcast_to(jnp.sum(bacc_ref[gi].T, axis=0, keepdims=True), (8, LANES))

    blk = lambda base: pl.BlockSpec((BLOCK, cw), lambda jc, i: (i, base + jc))
    vec = pl.BlockSpec((1, cw), lambda jc, i: (0, jc))
    acc = pl.BlockSpec((8, cw), lambda jc, i: (0, jc))
    wsp = pl.BlockSpec((gpc, BLOCK, BLOCK), lambda jc, i: (jc, 0, 0))
    return pl.pallas_call(
        body,
        name=name,
        grid=(nc, nb),
        in_specs=[blk(ub), blk(vb), vec, vec, wsp, pl.BlockSpec((BLOCK, ng), lambda jc, i: (0, 0)), blk(0)],
        out_specs=[blk(0), blk(0), acc, acc, wsp, pl.BlockSpec((gpc, 8, LANES), lambda jc, i: (jc, 0, 0))],
        out_shape=[jax.ShapeDtypeStruct((t, ws), BF16), jax.ShapeDtypeStruct((t, ws), BF16),
                   jax.ShapeDtypeStruct((8, ws), F32), jax.ShapeDtypeStruct((8, ws), F32),
                   jax.ShapeDtypeStruct((ng, BLOCK, BLOCK), F32), jax.ShapeDtypeStruct((ng, 8, LANES), F32)],
        scratch_shapes=[pltpu.VMEM((gpc, BLOCK, BLOCK), F32)],
        compiler_params=_params("arbitrary", "arbitrary"),
    )(proj, proj, ln_g, ln_b, w_s, bt, dout)


def _sigmoid(x):
    return 1.0 / (1.0 + jnp.exp(-x))


def _merge_geometry(off_g, d):
    cw = math.gcd(off_g, d)
    return cw, d // cw, off_g // cw, (off_g + d) // cw


def _merge_fwd(a, b, proj, off_g, name):
    t, d = a.shape
    cw, nc, ab, bb = _merge_geometry(off_g, d)
    tr = _divisor_tile(t, 512, 8)

    def body(a_ref, b_ref, la_ref, lb_ref, o_ref):
        o_ref[...] = (_sigmoid(la_ref[...]) * a_ref[...] + _sigmoid(lb_ref[...]) * b_ref[...]).astype(o_ref.dtype)

    blk = lambda base: pl.BlockSpec((tr, cw), lambda i, j: (i, base + j))
    return pl.pallas_call(
        body,
        name=name,
        grid=(t // tr, nc),
        in_specs=[blk(0), blk(0), blk(ab), blk(bb)],
        out_specs=blk(0),
        out_shape=jax.ShapeDtypeStruct((t, d), BF16),
        compiler_params=_params("parallel", "parallel"),
    )(a, b, proj, proj)


def _merge_bwd(a, b, proj, dm, off_g, name):
    t, d = a.shape
    cw, nc, ab, bb = _merge_geometry(off_g, d)
    tr = _divisor_tile(t, 512, 8)

    def body(a_ref, b_ref, la_ref, lb_ref, dm_ref, da_ref, db_ref, dla_ref, dlb_ref):
        dmv = dm_ref[...]
        ga, gb = _sigmoid(la_ref[...]), _sigmoid(lb_ref[...])
        da_ref[...] = (dmv * ga).astype(da_ref.dtype)
        db_ref[...] = (dmv * gb).astype(db_ref.dtype)
        dla_ref[...] = (dmv * a_ref[...] * ga * (1.0 - ga)).astype(dla_ref.dtype)
        dlb_ref[...] = (dmv * b_ref[...] * gb * (1.0 - gb)).astype(dlb_ref.dtype)

    blk = lambda base: pl.BlockSpec((tr, cw), lambda i, j: (i, base + j))
    return pl.pallas_call(
        body,
        name=name,
        grid=(t // tr, nc),
        in_specs=[blk(0), blk(0), blk(ab), blk(bb), blk(0)],
        out_specs=[blk(0)] * 4,
        out_shape=[jax.ShapeDtypeStruct((t, d), BF16)] * 4,
        compiler_params=_params("parallel", "parallel"),
    )(a, b, proj, proj, dm)


def _swiglu_fwd(gu, name, after=None):
    t, f2 = gu.shape
    f = f2 // 2
    cw = _divisor_tile(f, 1536, 128)
    nc = f // cw
    tr = _divisor_tile(t, 256, 8)

    def body(g_ref, u_ref, *rest):
        o_ref = rest[-1]
        gv = g_ref[...]
        o_ref[...] = (gv * _sigmoid(gv) * u_ref[...]).astype(o_ref.dtype)

    blk = lambda base: pl.BlockSpec((tr, cw), lambda i, j: (i, base + j))
    return pl.pallas_call(
        body,
        name=name,
        grid=(t // tr, nc),
        in_specs=[blk(0), blk(nc)] + ([ANY] if after is not None else []),
        out_specs=blk(0),
        out_shape=jax.ShapeDtypeStruct((t, f), BF16),
        compiler_params=_params("parallel", "parallel"),
    )(gu, gu, *(() if after is None else (after,)))


def _swiglu_bwd(gu, dact, name):
    t, f2 = gu.shape
    f = f2 // 2
    cw = _divisor_tile(f, 1536, 128)
    nc = f // cw
    tr = _divisor_tile(t, 256, 8)

    def body(g_ref, u_ref, da_ref, o_ref):
        gv, dav = g_ref[...], da_ref[...]
        sg = _sigmoid(gv)
        o_ref[0] = (dav * u_ref[...] * (sg + gv * sg * (1.0 - sg))).astype(o_ref.dtype)
        o_ref[1] = (dav * gv * sg).astype(o_ref.dtype)

    blk = lambda base: pl.BlockSpec((tr, cw), lambda i, j: (i, base + j))
    return pl.pallas_call(
        body,
        name=name,
        grid=(t // tr, nc),
        in_specs=[blk(0), blk(nc), blk(0)],
        out_specs=pl.BlockSpec((2, tr, cw), lambda i, j: (0, i, j)),
        out_shape=jax.ShapeDtypeStruct((2, t, f), BF16),
        compiler_params=_params("parallel", "parallel"),
    )(gu, gu, dact)


def _loss_and_grad(y, target, name):
    t, d = y.shape
    tr = _divisor_tile(t, 256, 8)

    def body(y_ref, t_ref, l_ref, dy_ref, dy16_ref):
        i = pl.program_id(0)
        err = y_ref[...] - t_ref[...]
        dy_ref[...] = err * (1.0 / d)
        dy16_ref[...] = (err * (1.0 / d)).astype(dy16_ref.dtype)
        part = jnp.broadcast_to(0.5 * jnp.sum(err * err) * (1.0 / d), l_ref.shape)

        @pl.when(i == 0)
        def _():
            l_ref[...] = part

        @pl.when(i > 0)
        def _():
            l_ref[...] += part

    row = pl.BlockSpec((tr, d), lambda i: (i, 0))
    return pl.pallas_call(
        body,
        name=name,
        grid=(t // tr,),
        in_specs=[row, row],
        out_specs=[pl.BlockSpec((8, LANES), lambda i: (0, 0)), row, row],
        out_shape=[jax.ShapeDtypeStruct((8, LANES), F32), jax.ShapeDtypeStruct((t, d), F32),
                   jax.ShapeDtypeStruct((t, d), BF16)],
        compiler_params=_params("arbitrary"),
    )(y, target)


def _adam_math(w, g, m, v):
    m = ADAM_B1 * m + (1.0 - ADAM_B1) * g
    v = ADAM_B2 * v + (1.0 - ADAM_B2) * (g * g)
    m_hat = m / (1.0 - ADAM_B1 ** ADAM_STEP)
    v_hat = v / (1.0 - ADAM_B2 ** ADAM_STEP)
    delta = -ADAM_LR * (m_hat / (jnp.sqrt(v_hat) + ADAM_EPS) + ADAM_WD * w)
    return delta, m, v


def _row_tile(r, c, elems=512 * 1024):
    return _divisor_tile(r, max(8, elems // c // 8 * 8), 8)


def _adam(w, grads, m, v, chip, name):
    nl, r, c = w.shape
    tr = _row_tile(r, c, 256 * 1024)
    nb = r // tr
    counts = [len(terms) for terms, _ in grads]

    def body(chip_ref, *refs):
        w_ref, m_ref, v_ref = refs[:3]
        g_ref, d_ref, nm_ref, nv_ref = refs[-4:]
        layer = pl.program_id(0)
        g, at = None, 3
        for li, n in enumerate(counts):
            total = refs[at][...].astype(F32)
            for ref in refs[at + 1:at + n]:
                total = total + ref[...].astype(F32)
            g = total if g is None else jnp.where(layer == li, total, g)
            at += n
        g_ref[...] = g
        d_ref[...], nm_ref[...], nv_ref[...] = _adam_math(w_ref[...], g, m_ref[...], v_ref[...])

    def term_spec(li, p, by_owner):
        def index(l, i, chip_ref):
            rows = jnp.where(l < li, 0, jnp.where(l > li, nb - 1, i))
            return (p, chip_ref[0] if by_owner else 0, rows, 0)
        return pl.BlockSpec((None, None, tr, c), index)

    row = pl.BlockSpec((None, tr, c), lambda l, i, chip_ref: (l, i, 0))
    specs, arrays = [], []
    for li, (terms, p) in enumerate(grads):
        for term in terms:
            specs.append(term_spec(li, p, term.shape[1] == 4))
            arrays.append(term)
    return pl.pallas_call(
        body,
        name=name,
        grid_spec=pltpu.PrefetchScalarGridSpec(
            num_scalar_prefetch=1, grid=(nl, nb), in_specs=[row] * 3 + specs, out_specs=[row] * 4),
        out_shape=[jax.ShapeDtypeStruct((nl, r, c), F32)] * 4,
        compiler_params=_params("arbitrary", "arbitrary"),
    )(chip, w, m, v, *arrays)


def _place_shard(parts, layer, dev, out_dtype, name, after=None):
    p = len(parts)
    _, r, c = parts[0].shape
    tr = _row_tile(r, c)

    def body(dev_ref, *refs):
        o_ref = refs[-1]
        x = refs[0][...]
        for pi in range(1, p):
            x = jnp.where(pl.program_id(0) == pi, refs[pi][...], x)
        o_ref[...] = x.astype(o_ref.dtype)

    return pl.pallas_call(
        body,
        name=name,
        grid_spec=pltpu.PrefetchScalarGridSpec(
            num_scalar_prefetch=1,
            grid=(p, r // tr),
            in_specs=[pl.BlockSpec((None, tr, c), lambda pi, i, dev_ref: (layer, i, 0))] * p
            + ([ANY] if after is not None else []),
            out_specs=pl.BlockSpec((None, None, tr, c), lambda pi, i, dev_ref: (pi, dev_ref[0], i, 0)),
        ),
        out_shape=jax.ShapeDtypeStruct((p, N_DEV, r, c), out_dtype),
        compiler_params=_params("parallel", "parallel"),
    )(dev, *parts, *(() if after is None else (after,)))


def _sum_sibling(g, land, core, name):
    p, _, _, r, c = g.shape
    tr = _row_tile(r, c)

    def body(core_ref, g_ref, l_ref, o_ref):
        o_ref[...] = (g_ref[...].astype(F32) + l_ref[...].astype(F32)).astype(o_ref.dtype)

    return pl.pallas_call(
        body,
        name=name,
        grid_spec=pltpu.PrefetchScalarGridSpec(
            num_scalar_prefetch=1,
            grid=(p, 4, r // tr),
            in_specs=[pl.BlockSpec((None, None, None, tr, c), lambda pi, q, i, core_ref: (pi, q, core_ref[0], i, 0)),
                      pl.BlockSpec((None, None, None, tr, c), lambda pi, q, i, core_ref: (pi, q, 0, i, 0))],
            out_specs=pl.BlockSpec((None, None, tr, c), lambda pi, q, i, core_ref: (pi, q, i, 0)),
        ),
        out_shape=jax.ShapeDtypeStruct((p, 4, r, c), BF16),
        compiler_params=_params("parallel", "parallel", "parallel"),
    )(core, g, land)


def _sum_chips(s, lands, chip, name):
    p, _, r, c = s.shape
    tr = _row_tile(r, c)

    def body(chip_ref, s_ref, l0_ref, l1_ref, l2_ref, o_ref):
        total = s_ref[...].astype(F32) + l0_ref[...].astype(F32)
        o_ref[...] = total + l1_ref[...].astype(F32) + l2_ref[...].astype(F32)

    land_spec = pl.BlockSpec((None, None, tr, c), lambda pi, i, chip_ref: (pi, 0, i, 0))
    return pl.pallas_call(
        body,
        name=name,
        grid_spec=pltpu.PrefetchScalarGridSpec(
            num_scalar_prefetch=1,
            grid=(p, r // tr),
            in_specs=[pl.BlockSpec((None, None, tr, c), lambda pi, i, chip_ref: (pi, chip_ref[0], i, 0)),
                      land_spec, land_spec, land_spec],
            out_specs=pl.BlockSpec((None, tr, c), lambda pi, i, chip_ref: (pi, i, 0)),
        ),
        out_shape=jax.ShapeDtypeStruct((p, r, c), F32),
        compiler_params=_params("parallel", "parallel"),
    )(chip, s, *lands)


def _small_reduce_adam(gathered, w, m, v, name):
    _, r, c = gathered.shape
    tr = _row_tile(r, c)

    def body(p_ref, w_ref, m_ref, v_ref, g_ref, d_ref, nm_ref, nv_ref):
        g = p_ref[0]
        for j in range(1, N_DEV):
            g = g + p_ref[j]
        g_ref[...] = g
        d_ref[...], nm_ref[...], nv_ref[...] = _adam_math(w_ref[...], g, m_ref[...], v_ref[...])

    row = pl.BlockSpec((tr, c), lambda i: (i, 0))
    return pl.pallas_call(
        body,
        name=name,
        grid=(r // tr,),
        in_specs=[pl.BlockSpec((N_DEV, tr, c), lambda i: (0, i, 0)), row, row, row],
        out_specs=[row] * 4,
        out_shape=[jax.ShapeDtypeStruct((r, c), F32)] * 4,
        compiler_params=_params("parallel"),
    )(gathered, w, m, v)


def _place():
    return lax.axis_index("x"), lax.axis_index("y"), lax.axis_index("c")


def _all_gather(bufs, name):
    n = len(bufs)

    def body(*refs):
        outs = refs[n:2 * n]
        send_sems, recv_sems = refs[2 * n:]
        x, y, c = _place()
        me, sibling = (x, y, c), (x, y, 1 - c)
        chips = [(1 - x, y), (x, 1 - y), (1 - x, 1 - y)]

        def block(a, px, py, pc):
            return outs[a].at[:, pl.ds(4 * px + 2 * py + pc, 1)]

        def copy(a, k, blk, to):
            return pltpu.make_async_remote_copy(
                src_ref=block(a, *blk), dst_ref=block(a, *blk), send_sem=send_sems.at[a, k],
                recv_sem=recv_sems.at[a, k], device_id=to, device_id_type=MESH)

        first = []
        for a in range(n):
            first.append(copy(a, 0, me, sibling))
            first += [copy(a, 1 + j, me, (*chip, c)) for j, chip in enumerate(chips)]
        for cp in first:
            cp.start()
        passed = []
        for j, chip in enumerate(chips):
            for a in range(n):
                copy(a, 1 + j, (*chip, c), me).wait_recv()
                fwd = copy(a, 4 + j, (*chip, c), sibling)
                fwd.start()
                passed.append(fwd)
        for a in range(n):
            copy(a, 0, sibling, me).wait_recv()
            for j, chip in enumerate(chips):
                copy(a, 4 + j, (*chip, 1 - c), me).wait_recv()
        for cp in first + passed:
            cp.wait_send()

    return pl.pallas_call(
        body,
        name=name,
        in_specs=[ANY] * n,
        out_specs=[ANY] * n,
        out_shape=[jax.ShapeDtypeStruct(b.shape, b.dtype) for b in bufs],
        input_output_aliases={a: a for a in range(n)},
        scratch_shapes=[pltpu.SemaphoreType.DMA((n, 7)), pltpu.SemaphoreType.DMA((n, 7))],
    )(*bufs)


HBM = pl.BlockSpec(memory_space=pltpu.HBM)
SEM = pl.BlockSpec(memory_space=pltpu.SEMAPHORE)
TOKEN = pl.BlockSpec(memory_space=pltpu.VMEM)
EFFECT = pltpu.SideEffectType.DATAFLOW_SIDE_EFFECTING


def _in_hbm(a):
    return pltpu.with_memory_space_constraint(a, pltpu.HBM)


def _gather_start(chunks, name):
    flat = [b for chunk in chunks for b in chunk]
    n, nch = len(flat), len(chunks)

    def body(*refs):
        sems, outs, token = refs[n:n + 2 * nch], refs[n + 2 * nch:2 * n + 2 * nch], refs[2 * n + 2 * nch]
        x, y, c = _place()
        targets = [(x, y, 1 - c), (1 - x, y, c), (x, 1 - y, c), (1 - x, 1 - y, c)]
        a = 0
        for ci, chunk in enumerate(chunks):
            for k in range(len(chunk)):
                mine = outs[a].at[:, pl.ds(4 * x + 2 * y + c, 1)]
                for ti, to in enumerate(targets):
                    pltpu.make_async_remote_copy(
                        src_ref=mine, dst_ref=mine, send_sem=sems[2 * ci].at[4 * k + ti],
                        recv_sem=sems[2 * ci + 1].at[4 * k + ti], device_id=to, device_id_type=MESH).start()
                a += 1
        token[...] = jnp.zeros_like(token)

    sem_shapes = []
    for chunk in chunks:
        sem_shapes += [pltpu.SemaphoreType.DMA((4 * len(chunk),))] * 2
    outs = pl.pallas_call(
        body,
        name=name,
        in_specs=[HBM] * n,
        out_specs=[SEM] * (2 * nch) + [HBM] * n + [TOKEN],
        out_shape=sem_shapes + [pltpu.HBM(b.shape, b.dtype) for b in flat] + [jax.ShapeDtypeStruct((8, LANES), F32)],
        input_output_aliases={i: 2 * nch + i for i in range(n)},
        compiler_params=pltpu.CompilerParams(has_side_effects=EFFECT),
    )(*[_in_hbm(b) for b in flat])
    result, a = [], 2 * nch
    for ci, chunk in enumerate(chunks):
        result.append((outs[2 * ci], outs[2 * ci + 1], list(outs[a:a + len(chunk)])))
        a += len(chunk)
    return result, outs[-1]


def _gather_wait(send_sems, recv_sems, bufs, after, name):
    n = len(bufs)

    def body(*refs):
        ins, ssem, rsem = refs[:n], refs[n], refs[n + 1]
        x, y, c = _place()
        sources = [(x, y, 1 - c), (1 - x, y, c), (x, 1 - y, c), (1 - x, 1 - y, c)]
        for k in range(n):
            for ti, (px, py, pc) in enumerate(sources):
                theirs = ins[k].at[:, pl.ds(4 * px + 2 * py + pc, 1)]
                cp = pltpu.make_async_remote_copy(
                    src_ref=theirs, dst_ref=theirs, send_sem=ssem.at[4 * k + ti], recv_sem=rsem.at[4 * k + ti],
                    device_id=(px, py, pc), device_id_type=MESH)
                cp.wait_send()
                cp.wait_recv()

    return pl.pallas_call(
        body,
        name=name,
        in_specs=[HBM] * n + [SEM, SEM, ANY],
        out_specs=[HBM] * n,
        out_shape=[pltpu.HBM(b.shape, b.dtype) for b in bufs],
        input_output_aliases={i: i for i in range(n)},
        compiler_params=pltpu.CompilerParams(has_side_effects=EFFECT),
    )(*bufs, send_sems, recv_sems, after)


def _forward_start(bufs, name):
    n = len(bufs)

    def body(*refs):
        ssem, rsem = refs[n], refs[n + 1]
        outs, token = refs[n + 2:2 * n + 2], refs[2 * n + 2]
        x, y, c = _place()
        chips = [(1 - x, y), (x, 1 - y), (1 - x, 1 - y)]
        for a in range(n):
            for j, (px, py) in enumerate(chips):
                got = outs[a].at[:, pl.ds(4 * px + 2 * py + c, 1)]
                pltpu.make_async_remote_copy(
                    src_ref=got, dst_ref=got, send_sem=ssem.at[3 * a + j], recv_sem=rsem.at[3 * a + j],
                    device_id=(x, y, 1 - c), device_id_type=MESH).start()
        token[...] = jnp.zeros_like(token)

    outs = pl.pallas_call(
        body,
        name=name,
        in_specs=[HBM] * n,
        out_specs=[SEM, SEM] + [HBM] * n + [TOKEN],
        out_shape=[pltpu.SemaphoreType.DMA((3 * n,))] * 2 + [pltpu.HBM(b.shape, b.dtype) for b in bufs]
        + [jax.ShapeDtypeStruct((8, LANES), F32)],
        input_output_aliases={i: 2 + i for i in range(n)},
        compiler_params=pltpu.CompilerParams(has_side_effects=EFFECT),
    )(*[_in_hbm(b) for b in bufs])
    return outs[0], outs[1], list(outs[2:2 + n]), outs[-1]


def _forward_wait(send_sems, recv_sems, bufs, after, name):
    n = len(bufs)

    def body(*refs):
        ins, ssem, rsem = refs[:n], refs[n], refs[n + 1]
        x, y, c = _place()
        chips = [(1 - x, y), (x, 1 - y), (1 - x, 1 - y)]
        for a in range(n):
            for j, (px, py) in enumerate(chips):
                coming = ins[a].at[:, pl.ds(4 * px + 2 * py + 1 - c, 1)]
                cp = pltpu.make_async_remote_copy(
                    src_ref=coming, dst_ref=coming, send_sem=ssem.at[3 * a + j], recv_sem=rsem.at[3 * a + j],
                    device_id=(x, y, 1 - c), device_id_type=MESH)
                cp.wait_send()
                cp.wait_recv()

    return pl.pallas_call(
        body,
        name=name,
        in_specs=[HBM] * n + [SEM, SEM, ANY],
        out_specs=[HBM] * n,
        out_shape=[pltpu.HBM(b.shape, b.dtype) for b in bufs],
        input_output_aliases={i: i for i in range(n)},
        compiler_params=pltpu.CompilerParams(has_side_effects=EFFECT),
    )(*bufs, send_sems, recv_sems, after)


def _chips_start(sums, name):
    n = len(sums)

    def body(*refs):
        ssem, rsem = refs[4 * n], refs[4 * n + 1]
        src, land = refs[4 * n + 2:5 * n + 2], refs[5 * n + 2:8 * n + 2]
        token = refs[8 * n + 2]
        x, y, c = _place()
        chips = [(1 - x, y), (x, 1 - y), (1 - x, 1 - y)]
        for a in range(n):
            for k, (px, py) in enumerate(chips):
                pltpu.make_async_remote_copy(
                    src_ref=src[a].at[:, pl.ds(2 * px + py, 1)], dst_ref=land[3 * a + k], send_sem=ssem.at[3 * a + k],
                    recv_sem=rsem.at[3 * a + k], device_id=(px, py, c), device_id_type=MESH).start()
        token[...] = jnp.zeros_like(token)

    lands = []
    for s in sums:
        lands += [lax.empty((s.shape[0], 1) + s.shape[2:], s.dtype) for _ in range(3)]
    outs = pl.pallas_call(
        body,
        name=name,
        in_specs=[HBM] * (4 * n),
        out_specs=[SEM, SEM] + [HBM] * (4 * n) + [TOKEN],
        out_shape=[pltpu.SemaphoreType.DMA((3 * n,))] * 2 + [pltpu.HBM(b.shape, b.dtype) for b in list(sums) + lands]
        + [jax.ShapeDtypeStruct((8, LANES), F32)],
        input_output_aliases={i: 2 + i for i in range(4 * n)},
        compiler_params=pltpu.CompilerParams(has_side_effects=EFFECT),
    )(*[_in_hbm(b) for b in list(sums) + lands])
    return outs[0], outs[1], list(outs[2:2 + n]), list(outs[2 + n:2 + 4 * n]), outs[-1]


def _chips_wait(send_sems, recv_sems, sums, lands, after, name):
    n = len(sums)

    def body(*refs):
        src, land = refs[:n], refs[n:4 * n]
        ssem, rsem = refs[4 * n], refs[4 * n + 1]
        x, y, c = _place()
        chips = [(1 - x, y), (x, 1 - y), (1 - x, 1 - y)]
        for a in range(n):
            for k, (px, py) in enumerate(chips):
                cp = pltpu.make_async_remote_copy(
                    src_ref=src[a].at[:, pl.ds(2 * px + py, 1)], dst_ref=land[3 * a + k], send_sem=ssem.at[3 * a + k],
                    recv_sem=rsem.at[3 * a + k], device_id=(px, py, c), device_id_type=MESH)
                cp.wait_send()
                cp.wait_recv()

    both = list(sums) + list(lands)
    outs = pl.pallas_call(
        body,
        name=name,
        in_specs=[HBM] * (4 * n) + [SEM, SEM, ANY],
        out_specs=[HBM] * (4 * n),
        out_shape=[pltpu.HBM(b.shape, b.dtype) for b in both],
        input_output_aliases={i: i for i in range(4 * n)},
        compiler_params=pltpu.CompilerParams(has_side_effects=EFFECT),
    )(*both, send_sems, recv_sems, after)
    return list(outs[:n]), [list(outs[n + 3 * a:n + 3 * a + 3]) for a in range(n)]


def _sibling_start(grads, name):
    n = len(grads)

    def body(*refs):
        ssem, rsem = refs[2 * n], refs[2 * n + 1]
        src, land = refs[2 * n + 2:3 * n + 2], refs[3 * n + 2:4 * n + 2]
        token = refs[4 * n + 2]
        x, y, c = _place()
        for a in range(n):
            pltpu.make_async_remote_copy(
                src_ref=src[a].at[:, :, pl.ds(1 - c, 1)], dst_ref=land[a], send_sem=ssem.at[a], recv_sem=rsem.at[a],
                device_id=(x, y, 1 - c), device_id_type=MESH).start()
        token[...] = jnp.zeros_like(token)

    lands = [lax.empty(g.shape[:2] + (1,) + g.shape[3:], g.dtype) for g in grads]
    both = list(grads) + lands
    outs = pl.pallas_call(
        body,
        name=name,
        in_specs=[HBM] * (2 * n),
        out_specs=[SEM, SEM] + [HBM] * (2 * n) + [TOKEN],
        out_shape=[pltpu.SemaphoreType.DMA((n,))] * 2 + [pltpu.HBM(b.shape, b.dtype) for b in both]
        + [jax.ShapeDtypeStruct((8, LANES), F32)],
        input_output_aliases={i: 2 + i for i in range(2 * n)},
        compiler_params=pltpu.CompilerParams(has_side_effects=EFFECT),
    )(*[_in_hbm(b) for b in both])
    return outs[0], outs[1], list(outs[2:2 + n]), list(outs[2 + n:2 + 2 * n]), outs[-1]


def _sibling_wait(send_sems, recv_sems, grads, lands, after, name):
    n = len(grads)

    def body(*refs):
        src, land = refs[:n], refs[n:2 * n]
        ssem, rsem = refs[2 * n], refs[2 * n + 1]
        x, y, c = _place()
        for a in range(n):
            cp = pltpu.make_async_remote_copy(
                src_ref=src[a].at[:, :, pl.ds(1 - c, 1)], dst_ref=land[a], send_sem=ssem.at[a], recv_sem=rsem.at[a],
                device_id=(x, y, 1 - c), device_id_type=MESH)
            cp.wait_send()
            cp.wait_recv()

    both = list(grads) + list(lands)
    outs = pl.pallas_call(
        body,
        name=name,
        in_specs=[HBM] * (2 * n) + [SEM, SEM, ANY],
        out_specs=[HBM] * (2 * n),
        out_shape=[pltpu.HBM(b.shape, b.dtype) for b in both],
        input_output_aliases={i: i for i in range(2 * n)},
        compiler_params=pltpu.CompilerParams(has_side_effects=EFFECT),
    )(*both, send_sems, recv_sems, after)
    return list(outs[:n]), list(outs[n:])


_SMALL = ("mix_norm", "q_norm", "k_norm", "sinks", "sgu_ln_g", "sgu_ln_b", "w_spatial", "b_spatial", "ffn_norm")


def _pack_rows(a):
    flat = a.reshape(-1)
    pad = (-flat.shape[0]) % LANES
    if pad:
        flat = jnp.pad(flat, (0, pad))
    return flat.reshape(-1, LANES)


def _pack(values):
    rows = jnp.concatenate([_pack_rows(values[k]) for k in _SMALL], axis=0)
    pad = (-rows.shape[0]) % 8
    if pad:
        rows = jnp.pad(rows, ((0, pad), (0, 0)))
    return rows


def _unpack(rows, like):
    out, at = {}, 0
    for k in _SMALL:
        size = like[k].size
        nrows = -(-size // LANES)
        out[k] = rows[at:at + nrows].reshape(-1)[:size].reshape(like[k].shape)
        at += nrows
    return out


def _rope_tables(t, wq):
    pos = jnp.arange(t, dtype=F32)
    inv_freq = jnp.power(ROPE_THETA, -jnp.arange(0, HEAD_DIM, 2, dtype=F32) / HEAD_DIM)
    ang = pos[:, None] * inv_freq[None, :]
    cos, sin = jnp.cos(ang), jnp.sin(ang)
    reps = wq // HEAD_DIM
    return (jnp.tile(jnp.concatenate([cos, cos], axis=1), (1, reps)),
            jnp.tile(jnp.concatenate([-sin, sin], axis=1), (1, reps)))


def kernel(x, mix_norm, w_in, q_norm, k_norm, sinks, sgu_ln_g, sgu_ln_b, w_spatial, b_spatial, w_attn_branch, w_sgu_branch, w_out, ffn_norm, w_gate, w_up, w_down, loss_target, m_mix_norm, m_w_in, m_q_norm, m_k_norm, m_sinks, m_sgu_ln_g, m_sgu_ln_b, m_w_spatial, m_b_spatial, m_w_attn_branch, m_w_sgu_branch, m_w_out, m_ffn_norm, m_w_gate, m_w_up, m_w_down, v_mix_norm, v_w_in, v_q_norm, v_k_norm, v_sinks, v_sgu_ln_g, v_sgu_ln_b, v_w_spatial, v_b_spatial, v_w_attn_branch, v_w_sgu_branch, v_w_out, v_ffn_norm, v_w_gate, v_w_up, v_w_down):
    names = ("mix_norm", "w_in", "q_norm", "k_norm", "sinks", "sgu_ln_g", "sgu_ln_b", "w_spatial", "b_spatial",
             "w_attn_branch", "w_sgu_branch", "w_out", "ffn_norm", "w_gate", "w_up", "w_down")
    weights = dict(zip(names, (mix_norm, w_in, q_norm, k_norm, sinks, sgu_ln_g, sgu_ln_b, w_spatial, b_spatial,
                               w_attn_branch, w_sgu_branch, w_out, ffn_norm, w_gate, w_up, w_down)))
    mom1 = dict(zip(names, (m_mix_norm, m_w_in, m_q_norm, m_k_norm, m_sinks, m_sgu_ln_g, m_sgu_ln_b, m_w_spatial,
                            m_b_spatial, m_w_attn_branch, m_w_sgu_branch, m_w_out, m_ffn_norm, m_w_gate, m_w_up,
                            m_w_down)))
    mom2 = dict(zip(names, (v_mix_norm, v_w_in, v_q_norm, v_k_norm, v_sinks, v_sgu_ln_g, v_sgu_ln_b, v_w_spatial,
                            v_b_spatial, v_w_attn_branch, v_w_sgu_branch, v_w_out, v_ffn_norm, v_w_gate, v_w_up,
                            v_w_down)))
    depth = w_in.shape[0]
    _, t, d = x.shape
    n_q_heads = sinks.shape[1]
    wq = n_q_heads * HEAD_DIM
    wk = wq // Q_PER_KV
    ws = sgu_ln_g.shape[1]
    ng = ws // LANES
    off_u = wq + 2 * wk
    off_g = off_u + 2 * ws
    tables = _rope_tables(t, wq)
    px, py, pc = _place()
    core = pc.astype(jnp.int32)[None]
    chip = (2 * px + py).astype(jnp.int32)[None]
    dev = (4 * px + 2 * py + pc).astype(jnp.int32)[None]

    layers = range(depth)
    chunks = ((0,), (1, 2, 3), (4,), (5,))
    pending, token = [], None
    sources = [[jnp.swapaxes(w_in, 1, 2)], [jnp.swapaxes(w_attn_branch, 1, 2)], [jnp.swapaxes(w_sgu_branch, 1, 2)],
               [w_out], [jnp.swapaxes(w_gate, 1, 2), jnp.swapaxes(w_up, 1, 2)], [w_down]]
    for l in layers:
        pending.append([])
        for ci, chunk in enumerate(chunks):
            bufs = [_place_shard(sources[a], l, dev, BF16, f"place_shard_{l}_{a}", after=token if a == chunk[0] else None)
                    for a in chunk]
            started, token = _gather_start([bufs], f"gather_start_{l}_{ci}")
            pending[l].append(started[0])

    passing = {}

    def arrive(l, ci, after):
        send_sems, recv_sems, bufs = pending[l][ci]
        bufs = _gather_wait(send_sems, recv_sems, bufs, after, f"gather_wait_{l}_{ci}")
        passing[l, ci] = _forward_start(bufs, f"forward_start_{l}_{ci}")
        return passing[l, ci][3]

    def ready(l, ci, after):
        send_sems, recv_sems, bufs, _ = passing.pop((l, ci))
        bufs = _forward_wait(send_sems, recv_sems, bufs, after, f"forward_wait_{l}_{ci}")
        return [f.reshape(f.shape[0] * f.shape[1] * f.shape[2], f.shape[3]) for f in bufs]

    saved = []
    xl = x[0]
    going = arrive(0, 0, token)
    for l in layers:
        gq = jnp.tile(q_norm[l], n_q_heads)[None]
        gk = jnp.tile(k_norm[l], n_q_heads // Q_PER_KV)[None]
        bt = b_spatial[l].T
        h = _rmsnorm_fwd(xl, mix_norm[l][None], f"mix_norm_fwd_{l}", after=going)
        (win_t,) = ready(l, 0, h)
        proj = _mm(h, win_t, "nt", F32, f"in_proj_{l}")
        going = arrive(l, 1, proj)
        attn = _attn_fwd(proj, tables, gq, gk, sinks[l], wq, wk, f"attn_fwd_{l}", after=going)
        sgu = _sgu_fwd(proj, sgu_ln_g[l][None], sgu_ln_b[l][None], w_spatial[l], bt, off_u, ws, f"sgu_fwd_{l}")
        wab_t, wsb_t, wo = ready(l, 1, sgu)
        br_a = _mm(attn, wab_t, "nt", F32, f"attn_branch_{l}")
        br_b = _mm(sgu, wsb_t, "nt", F32, f"sgu_branch_{l}")
        merged = _merge_fwd(br_a, br_b, proj, off_g, f"merge_fwd_{l}")
        x1 = _mm(merged, wo, "nn", F32, f"out_proj_{l}", residual=xl)
        going = arrive(l, 2, x1)
        h2 = _rmsnorm_fwd(x1, ffn_norm[l][None], f"ffn_norm_fwd_{l}", after=going)
        (wgu_t,) = ready(l, 2, h2)
        gu = _mm(h2, wgu_t, "nt", F32, f"gate_up_{l}")
        going = arrive(l, 3, gu)
        act = _swiglu_fwd(gu, f"swiglu_fwd_{l}", after=going)
        (wd,) = ready(l, 3, act)
        x2 = _mm(act, wd, "nn", F32, f"down_proj_{l}", residual=x1)
        if l + 1 < depth:
            going = arrive(l + 1, 0, x2)
        saved.append(dict(x0=xl, h=h, proj=proj, attn=attn, sgu=sgu, br_a=br_a, br_b=br_b, merged=merged, x1=x1,
                          h2=h2, gu=gu, act=act, gq=gq, gk=gk, bt=bt, win_t=win_t, wab_t=wab_t, wsb_t=wsb_t, wo=wo,
                          wgu_t=wgu_t, wd=wd))
        xl = x2

    loss_part, dx, dx16 = _loss_and_grad(xl, loss_target[0], "loss")
    loss = lax.psum(loss_part[0, 0], ("x", "y", "c"))

    def sibling_start(grads, tag):
        shaped = []
        for g, p in grads:
            rows, c = g.shape
            shaped.append(g.reshape(p, 4, 2, rows // (8 * p), c))
        send_sems, recv_sems, shaped, lands, tok = _sibling_start(shaped, f"rs_sibling_start_{tag}")
        return (send_sems, recv_sems, shaped, lands, tag), tok

    def chips_start(state, after):
        send_sems, recv_sems, shaped, lands, tag = state
        shaped, lands = _sibling_wait(send_sems, recv_sems, shaped, lands, after, f"rs_sibling_wait_{tag}")
        sums = [_sum_sibling(g, o, core, f"rs_add_sibling_{tag}_{a}") for a, (g, o) in enumerate(zip(shaped, lands))]
        send_sems, recv_sems, sums, lands, tok = _chips_start(sums, f"rs_chips_start_{tag}")
        return (send_sems, recv_sems, sums, lands, tag), tok

    def scatter_finish(state, after):
        send_sems, recv_sems, sums, lands, tag = state
        sums, lands = _chips_wait(send_sems, recv_sems, sums, lands, after, f"rs_chips_wait_{tag}")
        return [[s] + o for s, o in zip(sums, lands)]

    in_flight = [dict() for _ in layers]
    small_grads = [None] * depth
    tok, swap_in = None, None
    for l in reversed(layers):
        s = saved[l]
        dact = _mm(dx16, s["wd"], "nt", F32, f"d_act_{l}", after=tok)
        if swap_in is not None:
            in_flight[l + 1]["in"], tok = chips_start(swap_in, dact)
        g_wd = _mm(s["act"], dx16, "tn", BF16, f"g_w_down_{l}", after=tok)
        swap, tok_s = sibling_start([(g_wd, 1)], f"{l}_down")
        dgu = _swiglu_bwd(s["gu"], dact, f"swiglu_bwd_{l}")
        dh2 = _mm(dgu, s["wgu_t"], "nn", F32, f"d_h2_{l}", after=tok_s)
        in_flight[l]["down"], tok = chips_start(swap, dh2)
        g_wgu_t = _mm(dgu, s["h2"], "tn", BF16, f"g_w_gate_up_{l}", after=tok)
        swap, tok_s = sibling_start([(g_wgu_t, 2)], f"{l}_gate_up")
        dx1, dx1_16, g_ffn = _rmsnorm_bwd(s["x1"], ffn_norm[l][None], dh2, dx, f"ffn_norm_bwd_{l}")
        dmerged = _mm(dx1_16, s["wo"], "nt", F32, f"d_merged_{l}", after=tok_s)
        in_flight[l]["gate_up"], tok = chips_start(swap, dmerged)
        g_wo = _mm(s["merged"], dx1_16, "tn", BF16, f"g_w_out_{l}", after=tok)
        d_a, d_b, dla, dlb = _merge_bwd(s["br_a"], s["br_b"], s["proj"], dmerged, off_g, f"merge_bwd_{l}")
        dattn = _mm(d_a, s["wab_t"], "nn", F32, f"d_attn_{l}")
        g_wab_t = _mm(d_a, s["attn"], "tn", BF16, f"g_w_attn_branch_{l}")
        dsgu = _mm(d_b, s["wsb_t"], "nn", F32, f"d_sgu_{l}")
        g_wsb_t = _mm(d_b, s["sgu"], "tn", BF16, f"g_w_sgu_branch_{l}")
        swap, tok_s = sibling_start([(g_wab_t, 1), (g_wsb_t, 1), (g_wo, 1)], f"{l}_mix")
        dq, dk, dv, g_gq, g_gk, g_sinks = _attn_bwd(s["proj"], dattn, tables, s["gq"], s["gk"], sinks[l], wq, wk,
                                                    f"attn_bwd_{l}")
        du, dvv, g_lng, g_lnb, g_ws, g_bs = _sgu_bwd(s["proj"], dsgu, sgu_ln_g[l][None], sgu_ln_b[l][None],
                                                     w_spatial[l], s["bt"], off_u, ws, f"sgu_bwd_{l}")
        dproj = jnp.concatenate([dq, dk.astype(BF16), dv.astype(BF16), du, dvv, dla, dlb], axis=1)
        dh = _mm(dproj, s["win_t"], "nn", F32, f"d_h_{l}", after=tok_s)
        in_flight[l]["mix"], tok = chips_start(swap, dh)
        g_win_t = _mm(dproj, s["h"], "tn", BF16, f"g_w_in_{l}", after=tok)
        swap_in, tok = sibling_start([(g_win_t, 1)], f"{l}_in")
        dx, dx16, g_mix = _rmsnorm_bwd(s["x0"], mix_norm[l][None], dh, dx1, f"mix_norm_bwd_{l}")
        small_grads[l] = dict(
            mix_norm=g_mix[0], q_norm=g_gq[0].reshape(n_q_heads, HEAD_DIM).sum(0),
            k_norm=g_gk[0].reshape(n_q_heads // Q_PER_KV, HEAD_DIM).sum(0), sinks=g_sinks[0, :n_q_heads],
            sgu_ln_g=g_lng[0], sgu_ln_b=g_lnb[0], w_spatial=g_ws, b_spatial=g_bs[:, 0, :], ffn_norm=g_ffn[0])
    grad_x = dx[None]

    result = {key: {} for key in ("grad", "delta", "m", "v")}
    layer_like = {k: weights[k][0] for k in _SMALL}
    packed_g = jnp.concatenate([_pack(small_grads[l]) for l in layers], axis=0)
    rows_per_layer = packed_g.shape[0] // depth
    small_buf = _place_shard([packed_g[None]], 0, dev, F32, "place_small_grads", after=tok)
    (small_started,), tok = _gather_start([[small_buf]], "gather_start_small")
    in_flight[0]["in"], tok = chips_start(swap_in, tok)

    def update(k, grads, transposed):
        view = (lambda a: jnp.swapaxes(a, 1, 2)) if transposed else (lambda a: a)
        outs = _adam(view(weights[k]), grads, view(mom1[k]), view(mom2[k]), chip, f"adam_{k}")
        for key, val in zip(("grad", "delta", "m", "v"), outs):
            result[key][k] = view(val)
        return outs[3]

    def plain(terms, tag):
        s, lands = terms[0], terms[1:]
        g = _sum_chips(s, lands, chip, f"rs_add_chips_{tag}")
        return [jnp.swapaxes(g, 1, 2)[:, None]]

    down = [scatter_finish(in_flight[l]["down"], tok) for l in reversed(layers)][::-1]
    tok = update("w_down", [(down[l][0], 0) for l in layers], False)
    gate_up = [scatter_finish(in_flight[l]["gate_up"], tok) for l in reversed(layers)][::-1]
    tok = update("w_gate", [(gate_up[l][0], 0) for l in layers], True)
    tok = update("w_up", [(gate_up[l][0], 1) for l in layers], True)
    mix = [scatter_finish(in_flight[l]["mix"], tok) for l in reversed(layers)][::-1]
    tok = update("w_out", [(mix[l][2], 0) for l in layers], False)
    update("w_attn_branch", [(plain(mix[l][0], f"{l}_attn_branch"), 0) for l in layers], False)
    tok = update("w_sgu_branch", [(plain(mix[l][1], f"{l}_sgu_branch"), 0) for l in layers], False)

    send_sems, recv_sems, small_bufs = small_started
    small_bufs = _gather_wait(send_sems, recv_sems, small_bufs, tok, "gather_wait_small")
    send_sems, recv_sems, small_bufs, tok = _forward_start(small_bufs, "forward_start_small")
    packed = [jnp.concatenate([_pack({k: src[k][l] for k in _SMALL}) for l in layers], axis=0)
              for src in (weights, mom1, mom2)]
    (gathered_small,) = _forward_wait(send_sems, recv_sems, small_bufs, packed[0], "forward_wait_small")
    small = _small_reduce_adam(gathered_small[0], *packed, "small_reduce_adam")
    for key, rows in zip(("grad", "delta", "m", "v"), small):
        per_layer = [_unpack(rows[l * rows_per_layer:(l + 1) * rows_per_layer], layer_like) for l in layers]
        for k in _SMALL:
            result[key][k] = jnp.stack([per_layer[l][k] for l in layers])

    last = [scatter_finish(in_flight[l]["in"], small[0]) for l in reversed(layers)][::-1]
    update("w_in", [(last[l][0], 0) for l in layers], True)

    return (loss, grad_x, *[result["grad"][k] for k in names], *[result["delta"][k] for k in names],
            *[result["m"][k] for k in names], *[result["v"][k] for k in names])
```

```python
import functools
import math

import jax
import jax.numpy as jnp
from jax import lax
from jax.experimental import pallas as pl
from jax.experimental.pallas import tpu as pltpu

F32 = jnp.float32
BF16 = jnp.bfloat16
MESH = pl.DeviceIdType.MESH
ANY = pl.BlockSpec(memory_space=pl.ANY)

N_DEV = 8
HEAD_DIM = 64
Q_PER_KV = 4
BLOCK = 128
LANES = 128
ROPE_THETA = 10000.0
EPS = 1e-6
ADAM_LR = 0.001
ADAM_B1 = 0.9
ADAM_B2 = 0.999
ADAM_EPS = 1e-08
ADAM_WD = 0.01
ADAM_STEP = 10
NEG = -1e30
VMEM_LIMIT_BYTES = 56 * 1024 * 1024

NN = ((1,), (0,))
NT = ((1,), (1,))
TN = ((0,), (0,))


def _dot(a, b, dims):
    return lax.dot_general(a, b, (dims, ((), ())), preferred_element_type=F32)


def _params(*sem):
    return pltpu.CompilerParams(dimension_semantics=sem, vmem_limit_bytes=VMEM_LIMIT_BYTES)


def _divisor_tile(n, limit, unit):
    if n <= limit:
        return n
    best = unit
    for t in range(unit, limit + 1, unit):
        if n % t == 0:
            best = t
    assert n % best == 0, (n, limit, unit)
    return best


def _mm(a, b, mode, out_dtype, name, residual=None, after=None):
    parts = a.shape[0] if a.ndim == 3 else 1
    a2 = a.shape[-2:]
    if mode == "nn":
        (m, kp), (k2, n) = a2, b.shape
        k, mp = kp * parts, m
    elif mode == "nt":
        (m, kp), (n, k2) = a2, b.shape
        k, mp = kp * parts, m
    else:
        (k, mp), (k2, n) = a2, b.shape
        m, kp = mp * parts, k
    assert k == k2, (name, a.shape, b.shape)
    tk = _divisor_tile(kp, 2816, 128)
    nk = k // tk
    tm = _divisor_tile(mp, 512 if mode == "tn" else 1024, 128)
    tn = _divisor_tile(n, 2048 if mode == "tn" else (1024 if nk > 1 else 512), 128)
    kpb, mpb = kp // tk, mp // tm
    dims = {"nn": NN, "nt": NT, "tn": TN}[mode]
    lead = (None,) if a.ndim == 3 else ()
    if mode == "tn":
        a_index = lambda i, j, kk: (i // mpb, kk, i % mpb) if lead else (kk, i)
        a_spec = pl.BlockSpec(lead + (tk, tm), a_index)
    else:
        a_index = lambda i, j, kk: (kk // kpb, i, kk % kpb) if lead else (i, kk)
        a_spec = pl.BlockSpec(lead + (tm, tk), a_index)
    if mode == "nt":
        b_spec = pl.BlockSpec((tn, tk), lambda i, j, kk: (j, kk))
    else:
        b_spec = pl.BlockSpec((tk, tn), lambda i, j, kk: (kk, j))
    o_spec = pl.BlockSpec((tm, tn), lambda i, j, kk: (i, j))
    has_res = residual is not None

    def body(*refs):
        a_ref, b_ref = refs[:2]
        r_ref = refs[2] if has_res else None
        o_ref, acc_ref = refs[-2:]
        kk = pl.program_id(2)
        p = _dot(a_ref[...], b_ref[...], dims)

        def finish(total):
            if has_res:
                total = total + r_ref[...]
            o_ref[...] = total.astype(o_ref.dtype)

        if nk == 1:
            finish(p)
        else:
            @pl.when(kk == 0)
            def _():
                acc_ref[...] = p

            @pl.when(jnp.logical_and(kk > 0, kk < nk - 1))
            def _():
                acc_ref[...] += p

            @pl.when(kk == nk - 1)
            def _():
                finish(acc_ref[...] + p)

    in_specs = [a_spec, b_spec] + ([o_spec] if has_res else []) + ([ANY] if after is not None else [])
    args = (a, b) + ((residual,) if has_res else ()) + ((after,) if after is not None else ())
    acc_shape = (tm, tn) if nk > 1 else (8, LANES)
    return pl.pallas_call(
        body,
        name=name,
        grid=(m // tm, n // tn, nk),
        in_specs=in_specs,
        out_specs=o_spec,
        out_shape=jax.ShapeDtypeStruct((m, n), out_dtype),
        scratch_shapes=[pltpu.VMEM(acc_shape, F32)],
        compiler_params=_params("parallel", "parallel", "arbitrary"),
    )(*args)


def _rmsnorm_fwd(x, g, name, after=None):
    t, d = x.shape
    tr = _divisor_tile(t, 256, 8)

    def body(x_ref, g_ref, *rest):
        h_ref = rest[-1]
        xv = x_ref[...]
        rstd = lax.rsqrt(jnp.mean(xv * xv, axis=-1, keepdims=True) + EPS)
        h_ref[...] = (xv * rstd * g_ref[...]).astype(h_ref.dtype)

    return pl.pallas_call(
        body,
        name=name,
        grid=(t // tr,),
        in_specs=[pl.BlockSpec((tr, d), lambda i: (i, 0)), pl.BlockSpec((1, d), lambda i: (0, 0))]
        + ([ANY] if after is not None else []),
        out_specs=pl.BlockSpec((tr, d), lambda i: (i, 0)),
        out_shape=jax.ShapeDtypeStruct((t, d), BF16),
        compiler_params=_params("parallel"),
    )(x, g, *(() if after is None else (after,)))


def _rmsnorm_bwd(x, g, dh, dres, name):
    t, d = x.shape
    tr = _divisor_tile(t, 256, 8)

    def body(x_ref, g_ref, dh_ref, dres_ref, dx_ref, dx16_ref, dg_ref):
        i = pl.program_id(0)
        xv = x_ref[...]
        rstd = lax.rsqrt(jnp.mean(xv * xv, axis=-1, keepdims=True) + EPS)
        xh = xv * rstd
        dhv = dh_ref[...]
        dxh = dhv * g_ref[...]
        dx = dres_ref[...] + rstd * (dxh - xh * jnp.mean(dxh * xh, axis=-1, keepdims=True))
        dx_ref[...] = dx
        dx16_ref[...] = dx.astype(dx16_ref.dtype)
        part = jnp.broadcast_to(jnp.sum(dhv * xh, axis=0, keepdims=True), dg_ref.shape)

        @pl.when(i == 0)
        def _():
            dg_ref[...] = part

        @pl.when(i > 0)
        def _():
            dg_ref[...] += part

    row = pl.BlockSpec((tr, d), lambda i: (i, 0))
    return pl.pallas_call(
        body,
        name=name,
        grid=(t // tr,),
        in_specs=[row, pl.BlockSpec((1, d), lambda i: (0, 0)), row, row],
        out_specs=[row, row, pl.BlockSpec((8, d), lambda i: (0, 0))],
        out_shape=[jax.ShapeDtypeStruct((t, d), F32), jax.ShapeDtypeStruct((t, d), BF16),
                   jax.ShapeDtypeStruct((8, d), F32)],
        compiler_params=_params("arbitrary"),
    )(x, g, dh, dres)


def _lane(shape):
    return lax.broadcasted_iota(jnp.int32, shape, 1)


def _group_sum64(s):
    row = lax.broadcasted_iota(jnp.int32, (LANES, LANES), 0)
    col = lax.broadcasted_iota(jnp.int32, (LANES, LANES), 1)
    ones = jnp.where((row >= HEAD_DIM) == (col >= HEAD_DIM), 1.0, 0.0).astype(BF16)
    out = []
    for t in range(s.shape[1] // LANES):
        piece = s[:, LANES * t:LANES * t + LANES]
        hi = piece.astype(BF16)
        lo = (piece - hi.astype(F32)).astype(BF16)
        out.append(_dot(hi, ones, NN) + _dot(lo, ones, NN))
    return out[0] if len(out) == 1 else jnp.concatenate(out, axis=1)


def _swap32(x):
    w = x.shape[1]
    return jnp.where((_lane(x.shape) & 32) == 0, pltpu.roll(x, w - 32, axis=1), pltpu.roll(x, 32, axis=1))


def _rope(x, c, s):
    return x * c + _swap32(x) * s


def _rope_t(dy, c, s):
    return dy * c + _swap32(dy * s)


def _head_norm(x):
    rstd = lax.rsqrt(_group_sum64(x * x) * (1.0 / HEAD_DIM) + EPS)
    return x * rstd, rstd


def _head_norm_bwd(dxh, xh, rstd):
    return rstd * (dxh - xh * (_group_sum64(dxh * xh) * (1.0 / HEAD_DIM)))


def _roll64(x):
    return pltpu.roll(x, 64, axis=1)


def _attn_specs(wq, wk):
    kb = wq // wk
    prev = lambda i: jnp.maximum(i - 1, 0)
    return dict(
        q=pl.BlockSpec((BLOCK, wq), lambda i: (i, 0)),
        kc=pl.BlockSpec((BLOCK, wk), lambda i: (i, kb)),
        kp=pl.BlockSpec((BLOCK, wk), lambda i: (prev(i), kb)),
        vc=pl.BlockSpec((BLOCK, wk), lambda i: (i, kb + 1)),
        vp=pl.BlockSpec((BLOCK, wk), lambda i: (prev(i), kb + 1)),
        tq=pl.BlockSpec((BLOCK, wq), lambda i: (i, 0)),
        tkp=pl.BlockSpec((BLOCK, wk), lambda i: (prev(i), 0)),
        gq=pl.BlockSpec((1, wq), lambda i: (0, 0)),
        gk=pl.BlockSpec((1, wk), lambda i: (0, 0)),
        sinks=pl.BlockSpec(memory_space=pltpu.SMEM),
    )


def _attn_prologue(i, q_ref, kc_ref, kp_ref, cq_ref, sq_ref, ckp_ref, skp_ref, gq_ref, gk_ref):
    wk = kc_ref.shape[1]
    cq, sq = cq_ref[...], sq_ref[...]
    ck, sk = cq[:, :wk], sq[:, :wk]
    qh, q_rstd = _head_norm(q_ref[...])
    kch, kc_rstd = _head_norm(kc_ref[...])
    kph, kp_rstd = _head_norm(kp_ref[...])
    qn = _rope(qh * gq_ref[...], cq, sq)
    knc = _rope(kch * gk_ref[...], ck, sk)
    knp = _rope(kph * gk_ref[...], ckp_ref[...], skp_ref[...])
    row = lax.broadcasted_iota(jnp.int32, (BLOCK, BLOCK), 0)
    col = lax.broadcasted_iota(jnp.int32, (BLOCK, BLOCK), 1)
    mask_c = col <= row
    mask_p = jnp.logical_and(col > row, i > 0)
    half = (col >= 64).astype(jnp.int32)
    return dict(cq=cq, sq=sq, ck=ck, sk=sk, qh=qh, q_rstd=q_rstd, kch=kch, kc_rstd=kc_rstd, kph=kph,
                kp_rstd=kp_rstd, qn=qn, knc=knc, knp=knp, mask_c=mask_c, mask_p=mask_p, half=half)


def _head_scores(st, t, e, sink, scale):
    g = (2 * t) // Q_PER_KV
    ks, kpar = g // 2, g % 2
    sl = slice(LANES * ks, LANES * ks + LANES)
    mine = st["half"] == e
    qm = jnp.where(mine, st["qn"][:, LANES * t:LANES * t + LANES], 0.0).astype(BF16)
    kc, kp = st["knc"][:, sl], st["knp"][:, sl]
    flip = e != kpar
    if flip:
        kc, kp = _roll64(kc), _roll64(kp)
    kc, kp = kc.astype(BF16), kp.astype(BF16)
    s_c = jnp.where(st["mask_c"], _dot(qm, kc, NT) * scale, NEG)
    s_p = jnp.where(st["mask_p"], _dot(qm, kp, NT) * scale, NEG)
    m = jnp.maximum(jnp.maximum(jnp.max(s_c, axis=1, keepdims=True), jnp.max(s_p, axis=1, keepdims=True)), sink)
    p_c, p_p = jnp.exp(s_c - m), jnp.exp(s_p - m)
    p_s = jnp.exp(sink - m)
    inv = 1.0 / (jnp.sum(p_c, axis=1, keepdims=True) + jnp.sum(p_p, axis=1, keepdims=True) + p_s)
    return dict(sl=sl, mine=mine, flip=flip, qm=qm, kc=kc, kp=kp, pr_c=p_c * inv, pr_p=p_p * inv, pr_s=p_s * inv)


def _attn_fwd(proj, tables, gq, gk, sinks, wq, wk, name, after=None):
    t = proj.shape[0]
    nb = t // BLOCK
    sp = _attn_specs(wq, wk)
    scale = HEAD_DIM ** -0.5
    cos_t, sin_t = tables

    def body(sinks_ref, q_ref, kc_ref, kp_ref, vc_ref, vp_ref, cq_ref, sq_ref, ckp_ref, skp_ref, gq_ref, gk_ref,
             *rest):
        o_ref = rest[-1]
        i = pl.program_id(0)
        st = _attn_prologue(i, q_ref, kc_ref, kp_ref, cq_ref, sq_ref, ckp_ref, skp_ref, gq_ref, gk_ref)
        vc_all, vp_all = vc_ref[...], vp_ref[...]
        for ts in range(wq // LANES):
            acc = jnp.zeros((BLOCK, LANES), F32)
            for e in (0, 1):
                hs = _head_scores(st, ts, e, sinks_ref[2 * ts + e], scale)
                vc, vp = vc_all[:, hs["sl"]], vp_all[:, hs["sl"]]
                if hs["flip"]:
                    vc, vp = _roll64(vc), _roll64(vp)
                vc = jnp.where(hs["mine"], vc, 0.0).astype(BF16)
                vp = jnp.where(hs["mine"], vp, 0.0).astype(BF16)
                acc = acc + _dot(hs["pr_c"].astype(BF16), vc, NN) + _dot(hs["pr_p"].astype(BF16), vp, NN)
            o_ref[:, LANES * ts:LANES * ts + LANES] = acc.astype(o_ref.dtype)

    return pl.pallas_call(
        body,
        name=name,
        grid=(nb,),
        in_specs=[sp["sinks"], sp["q"], sp["kc"], sp["kp"], sp["vc"], sp["vp"], sp["tq"], sp["tq"], sp["tkp"],
                  sp["tkp"], sp["gq"], sp["gk"]] + ([ANY] if after is not None else []),
        out_specs=pl.BlockSpec((BLOCK, wq), lambda i: (i, 0)),
        out_shape=jax.ShapeDtypeStruct((t, wq), BF16),
        compiler_params=_params("parallel"),
    )(sinks, proj, proj, proj, proj, proj, cos_t, sin_t, cos_t, sin_t, gq, gk, *(() if after is None else (after,)))


def _attn_bwd(proj, dout, tables, gq, gk, sinks, wq, wk, name):
    t = proj.shape[0]
    nb = t // BLOCK
    sp = _attn_specs(wq, wk)
    scale = HEAD_DIM ** -0.5
    cos_t, sin_t = tables

    def body(sinks_ref, q_ref, kc_ref, kp_ref, vc_ref, vp_ref, cq_ref, sq_ref, ckp_ref, skp_ref, gq_ref, gk_ref,
             do_ref, dq_ref, dk_ref, dv_ref, dgq_ref, dgk_ref, dsk_ref, dqn_ref, dknc_ref, dknp_ref, dvc_ref,
             dvp_ref):
        i = pl.program_id(0)
        st = _attn_prologue(i, q_ref, kc_ref, kp_ref, cq_ref, sq_ref, ckp_ref, skp_ref, gq_ref, gk_ref)
        vc_all, vp_all = vc_ref[...], vp_ref[...]
        dknc_ref[...] = jnp.zeros_like(dknc_ref)
        dknp_ref[...] = jnp.zeros_like(dknp_ref)
        dvc_ref[...] = jnp.zeros_like(dvc_ref)
        dvp_ref[...] = jnp.zeros_like(dvp_ref)
        lane8 = _lane((8, LANES))
        dsinks = jnp.zeros((8, LANES), F32)
        for ts in range(wq // LANES):
            dq_acc = jnp.zeros((BLOCK, LANES), F32)
            for e in (0, 1):
                hs = _head_scores(st, ts, e, sinks_ref[2 * ts + e], scale)
                sl, flip = hs["sl"], hs["flip"]
                vc, vp = vc_all[:, sl], vp_all[:, sl]
                if flip:
                    vc, vp = _roll64(vc), _roll64(vp)
                dom = jnp.where(hs["mine"], do_ref[:, LANES * ts:LANES * ts + LANES], 0.0).astype(BF16)
                dp_c = _dot(dom, vc.astype(BF16), NT)
                dp_p = _dot(dom, vp.astype(BF16), NT)
                pr_c, pr_p = hs["pr_c"], hs["pr_p"]
                rs = jnp.sum(pr_c * dp_c, axis=1, keepdims=True) + jnp.sum(pr_p * dp_p, axis=1, keepdims=True)
                ds_c = (pr_c * (dp_c - rs) * scale)
                ds_p = (pr_p * (dp_p - rs) * scale)
                dsink = jnp.sum(-hs["pr_s"] * rs)
                dsinks = dsinks + jnp.where(lane8 == 2 * ts + e, dsink, 0.0)
                dq_acc = dq_acc + jnp.where(
                    hs["mine"], _dot(ds_c.astype(BF16), hs["kc"], NN) + _dot(ds_p.astype(BF16), hs["kp"], NN), 0.0)
                dv_c = _dot(pr_c.T.astype(BF16), dom, NN)
                dv_p = _dot(pr_p.T.astype(BF16), dom, NN)
                dk_c = _dot(ds_c.T.astype(BF16), hs["qm"], NN)
                dk_p = _dot(ds_p.T.astype(BF16), hs["qm"], NN)
                if flip:
                    dv_c, dv_p, dk_c, dk_p = _roll64(dv_c), _roll64(dv_p), _roll64(dk_c), _roll64(dk_p)
                dvc_ref[:, sl] += dv_c
                dvp_ref[:, sl] += dv_p
                dknc_ref[:, sl] += dk_c
                dknp_ref[:, sl] += dk_p
            dqn_ref[:, LANES * ts:LANES * ts + LANES] = dq_acc

        gqv, gkv = gq_ref[...], gk_ref[...]
        dqg = _rope_t(dqn_ref[...], st["cq"], st["sq"])
        dq_ref[...] = _head_norm_bwd(dqg * gqv, st["qh"], st["q_rstd"]).astype(dq_ref.dtype)
        dkcg = _rope_t(dknc_ref[...], st["ck"], st["sk"])
        dkpg = _rope_t(dknp_ref[...], ckp_ref[...], skp_ref[...])
        dk_cur = _head_norm_bwd(dkcg * gkv, st["kch"], st["kc_rstd"])
        dk_prev = _head_norm_bwd(dkpg * gkv, st["kph"], st["kp_rstd"])
        dgq_part = jnp.broadcast_to(jnp.sum(dqg * st["qh"], axis=0, keepdims=True), dgq_ref.shape)
        dgk_part = jnp.broadcast_to(
            jnp.sum(dkcg * st["kch"] + dkpg * st["kph"], axis=0, keepdims=True), dgk_ref.shape)
        cur = pl.ds(pl.multiple_of(i * BLOCK, BLOCK), BLOCK)
        dk_ref[cur, :] = dk_cur
        dv_ref[cur, :] = dvc_ref[...]

        @pl.when(i == 0)
        def _():
            dgq_ref[...] = dgq_part
            dgk_ref[...] = dgk_part
            dsk_ref[...] = dsinks

        @pl.when(i > 0)
        def _():
            before = pl.ds(pl.multiple_of((i - 1) * BLOCK, BLOCK), BLOCK)
            dk_ref[before, :] += dk_prev
            dv_ref[before, :] += dvp_ref[...]
            dgq_ref[...] += dgq_part
            dgk_ref[...] += dgk_part
            dsk_ref[...] += dsinks

    whole = lambda shape: pl.BlockSpec(shape, lambda i: (0, 0))
    return pl.pallas_call(
        body,
        name=name,
        grid=(nb,),
        in_specs=[sp["sinks"], sp["q"], sp["kc"], sp["kp"], sp["vc"], sp["vp"], sp["tq"], sp["tq"], sp["tkp"],
                  sp["tkp"], sp["gq"], sp["gk"], pl.BlockSpec((BLOCK, wq), lambda i: (i, 0))],
        out_specs=[pl.BlockSpec((BLOCK, wq), lambda i: (i, 0)), whole((t, wk)), whole((t, wk)), whole((8, wq)),
                   whole((8, wk)), whole((8, LANES))],
        out_shape=[jax.ShapeDtypeStruct((t, wq), BF16), jax.ShapeDtypeStruct((t, wk), F32),
                   jax.ShapeDtypeStruct((t, wk), F32), jax.ShapeDtypeStruct((8, wq), F32),
                   jax.ShapeDtypeStruct((8, wk), F32), jax.ShapeDtypeStruct((8, LANES), F32)],
        scratch_shapes=[pltpu.VMEM((BLOCK, wq), F32), pltpu.VMEM((BLOCK, wk), F32), pltpu.VMEM((BLOCK, wk), F32),
                        pltpu.VMEM((BLOCK, wk), F32), pltpu.VMEM((BLOCK, wk), F32)],
        compiler_params=_params("arbitrary"),
    )(sinks, proj, proj, proj, proj, proj, cos_t, sin_t, cos_t, sin_t, gq, gk, dout)


_GELU_K = math.sqrt(2.0 / math.pi)
_GELU_A = 0.044715


def _gelu(x):
    return 0.5 * x * (1.0 + jnp.tanh(_GELU_K * (x + _GELU_A * x * x * x)))


def _gelu_grad(x):
    th = jnp.tanh(_GELU_K * (x + _GELU_A * x * x * x))
    return 0.5 * (1.0 + th) + 0.5 * x * (1.0 - th * th) * (_GELU_K * (1.0 + 3.0 * _GELU_A * x * x))


def _group_ln(v):
    mu = jnp.mean(v, axis=1, keepdims=True)
    cen = v - mu
    rstd = lax.rsqrt(jnp.mean(cen * cen, axis=1, keepdims=True) + EPS)
    return cen * rstd, rstd


def _sgu_geometry(off_u, ws):
    cw = math.gcd(off_u, ws)
    return cw, ws // cw, off_u // cw, (off_u + ws) // cw


def _sgu_fwd(proj, ln_g, ln_b, w_s, bt, off_u, ws, name):
    t = proj.shape[0]
    nb = t // BLOCK
    cw, nc, ub, vb = _sgu_geometry(off_u, ws)
    gpc = cw // LANES
    ng = ws // LANES

    def body(u_ref, v_ref, g_ref, b_ref, w_ref, bt_ref, o_ref):
        jc = pl.program_id(0)
        row = lax.broadcasted_iota(jnp.int32, (BLOCK, BLOCK), 0)
        col = lax.broadcasted_iota(jnp.int32, (BLOCK, BLOCK), 1)
        lane_g = _lane((BLOCK, ng))
        for gi in range(gpc):
            sl = slice(LANES * gi, LANES * gi + LANES)
            xh, _ = _group_ln(_gelu(v_ref[:, sl]))
            vn = xh * g_ref[:, sl] + b_ref[:, sl]
            w = jnp.where(row >= col, w_ref[gi], 0.0).astype(BF16)
            bias = jnp.sum(jnp.where(lane_g == jc * gpc + gi, bt_ref[...], 0.0), axis=1, keepdims=True)
            s = _dot(w, vn.astype(BF16), NN) + bias
            o_ref[:, sl] = (_gelu(u_ref[:, sl]) * s).astype(o_ref.dtype)

    return pl.pallas_call(
        body,
        name=name,
        grid=(nc, nb),
        in_specs=[pl.BlockSpec((BLOCK, cw), lambda jc, i: (i, ub + jc)),
                  pl.BlockSpec((BLOCK, cw), lambda jc, i: (i, vb + jc)),
                  pl.BlockSpec((1, cw), lambda jc, i: (0, jc)),
                  pl.BlockSpec((1, cw), lambda jc, i: (0, jc)),
                  pl.BlockSpec((gpc, BLOCK, BLOCK), lambda jc, i: (jc, 0, 0)),
                  pl.BlockSpec((BLOCK, ng), lambda jc, i: (0, 0))],
        out_specs=pl.BlockSpec((BLOCK, cw), lambda jc, i: (i, jc)),
        out_shape=jax.ShapeDtypeStruct((t, ws), BF16),
        compiler_params=_params("parallel", "parallel"),
    )(proj, proj, ln_g, ln_b, w_s, bt)


def _sgu_bwd(proj, dout, ln_g, ln_b, w_s, bt, off_u, ws, name):
    t = proj.shape[0]
    nb = t // BLOCK
    cw, nc, ub, vb = _sgu_geometry(off_u, ws)
    gpc = cw // LANES
    ng = ws // LANES

    def body(u_ref, v_ref, g_ref, b_ref, w_ref, bt_ref, do_ref, du_ref, dv_ref, dg_ref, db_ref, dw_ref, dbs_ref,
             bacc_ref):
        jc = pl.program_id(0)
        i = pl.program_id(1)
        row = lax.broadcasted_iota(jnp.int32, (BLOCK, BLOCK), 0)
        col = lax.broadcasted_iota(jnp.int32, (BLOCK, BLOCK), 1)
        lane_g = _lane((BLOCK, ng))
        tri = row >= col

        @pl.when(i == 0)
        def _():
            dg_ref[...] = jnp.zeros_like(dg_ref)
            db_ref[...] = jnp.zeros_like(db_ref)
            dw_ref[...] = jnp.zeros_like(dw_ref)
            bacc_ref[...] = jnp.zeros_like(bacc_ref)

        for gi in range(gpc):
            sl = slice(LANES * gi, LANES * gi + LANES)
            u_raw, v_raw = u_ref[:, sl], v_ref[:, sl]
            xh, rstd = _group_ln(_gelu(v_raw))
            gam = g_ref[:, sl]
            vn = (xh * gam + b_ref[:, sl]).astype(BF16)
            w = jnp.where(tri, w_ref[gi], 0.0)
            bias = jnp.sum(jnp.where(lane_g == jc * gpc + gi, bt_ref[...], 0.0), axis=1, keepdims=True)
            s = _dot(w.astype(BF16), vn, NN) + bias
            dov = do_ref[:, sl]
            du_ref[:, sl] = (dov * s * _gelu_grad(u_raw)).astype(du_ref.dtype)
            ds = dov * _gelu(u_raw)
            ds16 = ds.astype(BF16)
            dw_ref[gi] += jnp.where(tri, _dot(ds16, vn, NT), 0.0)
            bacc_ref[gi] += ds
            dvn = _dot(w.T.astype(BF16), ds16, NN)
            dg_ref[:, sl] += jnp.broadcast_to(jnp.sum(dvn * xh, axis=0, keepdims=True), (8, LANES))
            db_ref[:, sl] += jnp.broadcast_to(jnp.sum(dvn, axis=0, keepdims=True), (8, LANES))
            dxh = dvn * gam
            dvg = rstd * (dxh - jnp.mean(dxh, axis=1, keepdims=True)
                          - xh * jnp.mean(dxh * xh, axis=1, keepdims=True))
            dv_ref[:, sl] = (dvg * _gelu_grad(v_raw)).astype(dv_ref.dtype)

        @pl.when(i == nb - 1)
        def _():
            for gi in range(gpc):
                dbs_ref[gi] = jnp.broadcast_to(jnp.sum(bacc_ref[gi].T, axis=0, keepdims=True), (8, LANES))

    blk = lambda base: pl.BlockSpec((BLOCK, cw), lambda jc, i: (i, base + jc))
    vec = pl.BlockSpec((1, cw), lambda jc, i: (0, jc))
    acc = pl.BlockSpec((8, cw), lambda jc, i: (0, jc))
    wsp = pl.BlockSpec((gpc, BLOCK, BLOCK), lambda jc, i: (jc, 0, 0))
    return pl.pallas_call(
        body,
        name=name,
        grid=(nc, nb),
        in_specs=[blk(ub), blk(vb), vec, vec, wsp, pl.BlockSpec((BLOCK, ng), lambda jc, i: (0, 0)), blk(0)],
        out_specs=[blk(0), blk(0), acc, acc, wsp, pl.BlockSpec((gpc, 8, LANES), lambda jc, i: (jc, 0, 0))],
        out_shape=[jax.ShapeDtypeStruct((t, ws), BF16), jax.ShapeDtypeStruct((t, ws), BF16),
                   jax.ShapeDtypeStruct((8, ws), F32), jax.ShapeDtypeStruct((8, ws), F32),
                   jax.ShapeDtypeStruct((ng, BLOCK, BLOCK), F32), jax.ShapeDtypeStruct((ng, 8, LANES), F32)],
        scratch_shapes=[pltpu.VMEM((gpc, BLOCK, BLOCK), F32)],
        compiler_params=_params("arbitrary", "arbitrary"),
    )(proj, proj, ln_g, ln_b, w_s, bt, dout)


def _sigmoid(x):
    return 1.0 / (1.0 + jnp.exp(-x))


def _merge_geometry(off_g, d):
    cw = math.gcd(off_g, d)
    return cw, d // cw, off_g // cw, (off_g + d) // cw


def _merge_fwd(a, b, proj, off_g, name):
    t, d = a.shape
    cw, nc, ab, bb = _merge_geometry(off_g, d)
    tr = _divisor_tile(t, 512, 8)

    def body(a_ref, b_ref, la_ref, lb_ref, o_ref):
        o_ref[...] = (_sigmoid(la_ref[...]) * a_ref[...] + _sigmoid(lb_ref[...]) * b_ref[...]).astype(o_ref.dtype)

    blk = lambda base: pl.BlockSpec((tr, cw), lambda i, j: (i, base + j))
    return pl.pallas_call(
        body,
        name=name,
        grid=(t // tr, nc),
        in_specs=[blk(0), blk(0), blk(ab), blk(bb)],
        out_specs=blk(0),
        out_shape=jax.ShapeDtypeStruct((t, d), BF16),
        compiler_params=_params("parallel", "parallel"),
    )(a, b, proj, proj)


def _merge_bwd(a, b, proj, dm, off_g, name):
    t, d = a.shape
    cw, nc, ab, bb = _merge_geometry(off_g, d)
    tr = _divisor_tile(t, 512, 8)

    def body(a_ref, b_ref, la_ref, lb_ref, dm_ref, da_ref, db_ref, dla_ref, dlb_ref):
        dmv = dm_ref[...]
        ga, gb = _sigmoid(la_ref[...]), _sigmoid(lb_ref[...])
        da_ref[...] = (dmv * ga).astype(da_ref.dtype)
        db_ref[...] = (dmv * gb).astype(db_ref.dtype)
        dla_ref[...] = (dmv * a_ref[...] * ga * (1.0 - ga)).astype(dla_ref.dtype)
        dlb_ref[...] = (dmv * b_ref[...] * gb * (1.0 - gb)).astype(dlb_ref.dtype)

    blk = lambda base: pl.BlockSpec((tr, cw), lambda i, j: (i, base + j))
    return pl.pallas_call(
        body,
        name=name,
        grid=(t // tr, nc),
        in_specs=[blk(0), blk(0), blk(ab), blk(bb), blk(0)],
        out_specs=[blk(0)] * 4,
        out_shape=[jax.ShapeDtypeStruct((t, d), BF16)] * 4,
        compiler_params=_params("parallel", "parallel"),
    )(a, b, proj, proj, dm)


def _gate_up_fwd(h2, wgu_t, name, after=None):
    t, d = h2.shape
    f = wgu_t.shape[0] // 2
    tm = _divisor_tile(t, 1024, 128)
    tn = _divisor_tile(f, 512, 128)
    nb = f // tn

    def body(a_ref, bg_ref, bu_ref, *rest):
        gu_ref, act_ref = rest[-2:]
        av = a_ref[...]
        gv = _dot(av, bg_ref[...], NT)
        uv = _dot(av, bu_ref[...], NT)
        gu_ref[0] = gv
        gu_ref[1] = uv
        act_ref[...] = (gv * _sigmoid(gv) * uv).astype(act_ref.dtype)

    return pl.pallas_call(
        body,
        name=name,
        grid=(t // tm, nb),
        in_specs=[pl.BlockSpec((tm, d), lambda i, j: (i, 0)), pl.BlockSpec((tn, d), lambda i, j: (j, 0)),
                  pl.BlockSpec((tn, d), lambda i, j: (j + nb, 0))] + ([ANY] if after is not None else []),
        out_specs=[pl.BlockSpec((2, tm, tn), lambda i, j: (0, i, j)), pl.BlockSpec((tm, tn), lambda i, j: (i, j))],
        out_shape=[jax.ShapeDtypeStruct((2, t, f), F32), jax.ShapeDtypeStruct((t, f), BF16)],
        compiler_params=_params("parallel", "parallel"),
    )(h2, wgu_t, wgu_t, *(() if after is None else (after,)))


def _gate_up_bwd(dx16, wd, gu, name, after=None):
    t, d = dx16.shape
    f = wd.shape[0]
    tm = _divisor_tile(t, 1024, 128)
    tn = _divisor_tile(f, 512, 128)

    def body(a_ref, b_ref, gu_ref, *rest):
        o_ref = rest[-1]
        dav = _dot(a_ref[...], b_ref[...], NT)
        gv = gu_ref[0]
        sg = _sigmoid(gv)
        o_ref[0] = (dav * gu_ref[1] * (sg + gv * sg * (1.0 - sg))).astype(o_ref.dtype)
        o_ref[1] = (dav * gv * sg).astype(o_ref.dtype)

    pair = pl.BlockSpec((2, tm, tn), lambda i, j: (0, i, j))
    return pl.pallas_call(
        body,
        name=name,
        grid=(t // tm, f // tn),
        in_specs=[pl.BlockSpec((tm, d), lambda i, j: (i, 0)), pl.BlockSpec((tn, d), lambda i, j: (j, 0)), pair]
        + ([ANY] if after is not None else []),
        out_specs=pair,
        out_shape=jax.ShapeDtypeStruct((2, t, f), BF16),
        compiler_params=_params("parallel", "parallel"),
    )(dx16, wd, gu, *(() if after is None else (after,)))


def _loss_and_grad(y, target, name):
    t, d = y.shape
    tr = _divisor_tile(t, 256, 8)

    def body(y_ref, t_ref, l_ref, dy_ref, dy16_ref):
        i = pl.program_id(0)
        err = y_ref[...] - t_ref[...]
        dy_ref[...] = err * (1.0 / d)
        dy16_ref[...] = (err * (1.0 / d)).astype(dy16_ref.dtype)
        part = jnp.broadcast_to(0.5 * jnp.sum(err * err) * (1.0 / d), l_ref.shape)

        @pl.when(i == 0)
        def _():
            l_ref[...] = part

        @pl.when(i > 0)
        def _():
            l_ref[...] += part

    row = pl.BlockSpec((tr, d), lambda i: (i, 0))
    return pl.pallas_call(
        body,
        name=name,
        grid=(t // tr,),
        in_specs=[row, row],
        out_specs=[pl.BlockSpec((8, LANES), lambda i: (0, 0)), row, row],
        out_shape=[jax.ShapeDtypeStruct((8, LANES), F32), jax.ShapeDtypeStruct((t, d), F32),
                   jax.ShapeDtypeStruct((t, d), BF16)],
        compiler_params=_params("arbitrary"),
    )(y, target)


def _adam_math(w, g, m, v):
    m = ADAM_B1 * m + (1.0 - ADAM_B1) * g
    v = ADAM_B2 * v + (1.0 - ADAM_B2) * (g * g)
    m_hat = m / (1.0 - ADAM_B1 ** ADAM_STEP)
    v_hat = v / (1.0 - ADAM_B2 ** ADAM_STEP)
    delta = -ADAM_LR * (m_hat / (jnp.sqrt(v_hat) + ADAM_EPS) + ADAM_WD * w)
    return delta, m, v


def _row_tile(r, c, elems=512 * 1024):
    return _divisor_tile(r, max(8, elems // c // 8 * 8), 8)


def _adam(w, grads, m, v, chip, name, after=None):
    nl, r, c = w.shape
    tr = _row_tile(r, c, 256 * 1024)
    nb = r // tr
    counts = [len(terms) for terms, _ in grads]

    def body(chip_ref, *refs):
        w_ref, m_ref, v_ref = refs[:3]
        g_ref, d_ref, nm_ref, nv_ref = refs[-4:]
        layer = pl.program_id(0)
        g, at = None, 3
        for li, n in enumerate(counts):
            total = refs[at][...].astype(F32)
            for ref in refs[at + 1:at + n]:
                total = total + ref[...].astype(F32)
            g = total if g is None else jnp.where(layer == li, total, g)
            at += n
        g_ref[...] = g
        d_ref[...], nm_ref[...], nv_ref[...] = _adam_math(w_ref[...], g, m_ref[...], v_ref[...])

    def term_spec(li, p, by_owner):
        def index(l, i, chip_ref):
            rows = jnp.where(l < li, 0, jnp.where(l > li, nb - 1, i))
            return (p, chip_ref[0] if by_owner else 0, rows, 0)
        return pl.BlockSpec((None, None, tr, c), index)

    row = pl.BlockSpec((None, tr, c), lambda l, i, chip_ref: (l, i, 0))
    specs, arrays = [], []
    for li, (terms, p) in enumerate(grads):
        for term in terms:
            specs.append(term_spec(li, p, term.shape[1] == 4))
            arrays.append(term)
    return pl.pallas_call(
        body,
        name=name,
        grid_spec=pltpu.PrefetchScalarGridSpec(
            num_scalar_prefetch=1, grid=(nl, nb),
            in_specs=[row] * 3 + specs + ([ANY] if after is not None else []), out_specs=[row] * 4),
        out_shape=[jax.ShapeDtypeStruct((nl, r, c), F32)] * 4,
        compiler_params=_params("arbitrary", "arbitrary"),
    )(chip, w, m, v, *arrays, *(() if after is None else (after,)))


def _place_shard(parts, layer, dev, out_dtype, name, after=None):
    p = len(parts)
    _, r, c = parts[0].shape
    tr = _row_tile(r, c)

    def body(dev_ref, *refs):
        o_ref = refs[-1]
        x = refs[0][...]
        for pi in range(1, p):
            x = jnp.where(pl.program_id(0) == pi, refs[pi][...], x)
        o_ref[...] = x.astype(o_ref.dtype)

    return pl.pallas_call(
        body,
        name=name,
        grid_spec=pltpu.PrefetchScalarGridSpec(
            num_scalar_prefetch=1,
            grid=(p, r // tr),
            in_specs=[pl.BlockSpec((None, tr, c), lambda pi, i, dev_ref: (layer, i, 0))] * p
            + ([ANY] if after is not None else []),
            out_specs=pl.BlockSpec((None, None, tr, c), lambda pi, i, dev_ref: (pi, dev_ref[0], i, 0)),
        ),
        out_shape=jax.ShapeDtypeStruct((p, N_DEV, r, c), out_dtype),
        compiler_params=_params("parallel", "parallel"),
    )(dev, *parts, *(() if after is None else (after,)))


def _sum_sibling(g, land, core, name):
    p, _, _, r, c = g.shape
    tr = _row_tile(r, c, 1024 * 1024)

    def body(core_ref, g_ref, l_ref, o_ref):
        o_ref[...] = (g_ref[...].astype(F32) + l_ref[...].astype(F32)).astype(o_ref.dtype)

    return pl.pallas_call(
        body,
        name=name,
        grid_spec=pltpu.PrefetchScalarGridSpec(
            num_scalar_prefetch=1,
            grid=(p, 4, r // tr),
            in_specs=[pl.BlockSpec((None, None, None, tr, c), lambda pi, q, i, core_ref: (pi, q, core_ref[0], i, 0)),
                      pl.BlockSpec((None, None, None, tr, c), lambda pi, q, i, core_ref: (pi, q, 0, i, 0))],
            out_specs=pl.BlockSpec((None, None, tr, c), lambda pi, q, i, core_ref: (pi, q, i, 0)),
        ),
        out_shape=jax.ShapeDtypeStruct((p, 4, r, c), BF16),
        compiler_params=_params("parallel", "parallel", "parallel"),
    )(core, g, land)


def _sum_chips(s, lands, chip, name):
    p, _, r, c = s.shape
    tr = _row_tile(r, c)

    def body(chip_ref, s_ref, l0_ref, l1_ref, l2_ref, o_ref):
        total = s_ref[...].astype(F32) + l0_ref[...].astype(F32)
        o_ref[...] = total + l1_ref[...].astype(F32) + l2_ref[...].astype(F32)

    land_spec = pl.BlockSpec((None, None, tr, c), lambda pi, i, chip_ref: (pi, 0, i, 0))
    return pl.pallas_call(
        body,
        name=name,
        grid_spec=pltpu.PrefetchScalarGridSpec(
            num_scalar_prefetch=1,
            grid=(p, r // tr),
            in_specs=[pl.BlockSpec((None, None, tr, c), lambda pi, i, chip_ref: (pi, chip_ref[0], i, 0)),
                      land_spec, land_spec, land_spec],
            out_specs=pl.BlockSpec((None, tr, c), lambda pi, i, chip_ref: (pi, i, 0)),
        ),
        out_shape=jax.ShapeDtypeStruct((p, r, c), F32),
        compiler_params=_params("parallel", "parallel"),
    )(chip, s, *lands)


def _small_reduce_adam(gathered, w, m, v, name):
    _, r, c = gathered.shape
    tr = _row_tile(r, c)

    def body(p_ref, w_ref, m_ref, v_ref, g_ref, d_ref, nm_ref, nv_ref):
        g = p_ref[0]
        for j in range(1, N_DEV):
            g = g + p_ref[j]
        g_ref[...] = g
        d_ref[...], nm_ref[...], nv_ref[...] = _adam_math(w_ref[...], g, m_ref[...], v_ref[...])

    row = pl.BlockSpec((tr, c), lambda i: (i, 0))
    return pl.pallas_call(
        body,
        name=name,
        grid=(r // tr,),
        in_specs=[pl.BlockSpec((N_DEV, tr, c), lambda i: (0, i, 0)), row, row, row],
        out_specs=[row] * 4,
        out_shape=[jax.ShapeDtypeStruct((r, c), F32)] * 4,
        compiler_params=_params("parallel"),
    )(gathered, w, m, v)


def _place():
    return lax.axis_index("x"), lax.axis_index("y"), lax.axis_index("c")


def _all_gather(bufs, name):
    n = len(bufs)

    def body(*refs):
        outs = refs[n:2 * n]
        send_sems, recv_sems = refs[2 * n:]
        x, y, c = _place()
        me, sibling = (x, y, c), (x, y, 1 - c)
        chips = [(1 - x, y), (x, 1 - y), (1 - x, 1 - y)]

        def block(a, px, py, pc):
            return outs[a].at[:, pl.ds(4 * px + 2 * py + pc, 1)]

        def copy(a, k, blk, to):
            return pltpu.make_async_remote_copy(
                src_ref=block(a, *blk), dst_ref=block(a, *blk), send_sem=send_sems.at[a, k],
                recv_sem=recv_sems.at[a, k], device_id=to, device_id_type=MESH)

        first = []
        for a in range(n):
            first.append(copy(a, 0, me, sibling))
            first += [copy(a, 1 + j, me, (*chip, c)) for j, chip in enumerate(chips)]
        for cp in first:
            cp.start()
        passed = []
        for j, chip in enumerate(chips):
            for a in range(n):
                copy(a, 1 + j, (*chip, c), me).wait_recv()
                fwd = copy(a, 4 + j, (*chip, c), sibling)
                fwd.start()
                passed.append(fwd)
        for a in range(n):
            copy(a, 0, sibling, me).wait_recv()
            for j, chip in enumerate(chips):
                copy(a, 4 + j, (*chip, 1 - c), me).wait_recv()
        for cp in first + passed:
            cp.wait_send()

    return pl.pallas_call(
        body,
        name=name,
        in_specs=[ANY] * n,
        out_specs=[ANY] * n,
        out_shape=[jax.ShapeDtypeStruct(b.shape, b.dtype) for b in bufs],
        input_output_aliases={a: a for a in range(n)},
        scratch_shapes=[pltpu.SemaphoreType.DMA((n, 7)), pltpu.SemaphoreType.DMA((n, 7))],
    )(*bufs)


HBM = pl.BlockSpec(memory_space=pltpu.HBM)
SEM = pl.BlockSpec(memory_space=pltpu.SEMAPHORE)
TOKEN = pl.BlockSpec(memory_space=pltpu.VMEM)
EFFECT = pltpu.SideEffectType.DATAFLOW_SIDE_EFFECTING


def _in_hbm(a):
    return pltpu.with_memory_space_constraint(a, pltpu.HBM)


def _gather_start(chunks, name):
    flat = [b for chunk in chunks for b in chunk]
    n, nch = len(flat), len(chunks)

    def body(*refs):
        sems, outs, token = refs[n:n + 2 * nch], refs[n + 2 * nch:2 * n + 2 * nch], refs[2 * n + 2 * nch]
        x, y, c = _place()
        targets = [(x, y, 1 - c), (1 - x, y, c), (x, 1 - y, c), (1 - x, 1 - y, c)]
        a = 0
        for ci, chunk in enumerate(chunks):
            for k in range(len(chunk)):
                mine = outs[a].at[:, pl.ds(4 * x + 2 * y + c, 1)]
                for ti, to in enumerate(targets):
                    pltpu.make_async_remote_copy(
                        src_ref=mine, dst_ref=mine, send_sem=sems[2 * ci].at[4 * k + ti],
                        recv_sem=sems[2 * ci + 1].at[4 * k + ti], device_id=to, device_id_type=MESH).start()
                a += 1
        token[...] = jnp.zeros_like(token)

    sem_shapes = []
    for chunk in chunks:
        sem_shapes += [pltpu.SemaphoreType.DMA((4 * len(chunk),))] * 2
    outs = pl.pallas_call(
        body,
        name=name,
        in_specs=[HBM] * n,
        out_specs=[SEM] * (2 * nch) + [HBM] * n + [TOKEN],
        out_shape=sem_shapes + [pltpu.HBM(b.shape, b.dtype) for b in flat] + [jax.ShapeDtypeStruct((8, LANES), F32)],
        input_output_aliases={i: 2 * nch + i for i in range(n)},
        compiler_params=pltpu.CompilerParams(has_side_effects=EFFECT),
    )(*[_in_hbm(b) for b in flat])
    result, a = [], 2 * nch
    for ci, chunk in enumerate(chunks):
        result.append((outs[2 * ci], outs[2 * ci + 1], list(outs[a:a + len(chunk)])))
        a += len(chunk)
    return result, outs[-1]


def _gather_wait(send_sems, recv_sems, bufs, after, name):
    n = len(bufs)

    def body(*refs):
        ins, ssem, rsem = refs[:n], refs[n], refs[n + 1]
        x, y, c = _place()
        sources = [(x, y, 1 - c), (1 - x, y, c), (x, 1 - y, c), (1 - x, 1 - y, c)]
        for k in range(n):
            for ti, (px, py, pc) in enumerate(sources):
                theirs = ins[k].at[:, pl.ds(4 * px + 2 * py + pc, 1)]
                cp = pltpu.make_async_remote_copy(
                    src_ref=theirs, dst_ref=theirs, send_sem=ssem.at[4 * k + ti], recv_sem=rsem.at[4 * k + ti],
                    device_id=(px, py, pc), device_id_type=MESH)
                cp.wait_send()
                cp.wait_recv()

    return pl.pallas_call(
        body,
        name=name,
        in_specs=[HBM] * n + [SEM, SEM, ANY],
        out_specs=[HBM] * n,
        out_shape=[pltpu.HBM(b.shape, b.dtype) for b in bufs],
        input_output_aliases={i: i for i in range(n)},
        compiler_params=pltpu.CompilerParams(has_side_effects=EFFECT),
    )(*bufs, send_sems, recv_sems, after)


def _forward_start(bufs, name):
    n = len(bufs)

    def body(*refs):
        ssem, rsem = refs[n], refs[n + 1]
        outs, token = refs[n + 2:2 * n + 2], refs[2 * n + 2]
        x, y, c = _place()
        chips = [(1 - x, y), (x, 1 - y), (1 - x, 1 - y)]
        for a in range(n):
            for j, (px, py) in enumerate(chips):
                got = outs[a].at[:, pl.ds(4 * px + 2 * py + c, 1)]
                pltpu.make_async_remote_copy(
                    src_ref=got, dst_ref=got, send_sem=ssem.at[3 * a + j], recv_sem=rsem.at[3 * a + j],
                    device_id=(x, y, 1 - c), device_id_type=MESH).start()
        token[...] = jnp.zeros_like(token)

    outs = pl.pallas_call(
        body,
        name=name,
        in_specs=[HBM] * n,
        out_specs=[SEM, SEM] + [HBM] * n + [TOKEN],
        out_shape=[pltpu.SemaphoreType.DMA((3 * n,))] * 2 + [pltpu.HBM(b.shape, b.dtype) for b in bufs]
        + [jax.ShapeDtypeStruct((8, LANES), F32)],
        input_output_aliases={i: 2 + i for i in range(n)},
        compiler_params=pltpu.CompilerParams(has_side_effects=EFFECT),
    )(*[_in_hbm(b) for b in bufs])
    return outs[0], outs[1], list(outs[2:2 + n]), outs[-1]


def _forward_wait(send_sems, recv_sems, bufs, after, name):
    n = len(bufs)

    def body(*refs):
        ins, ssem, rsem = refs[:n], refs[n], refs[n + 1]
        x, y, c = _place()
        chips = [(1 - x, y), (x, 1 - y), (1 - x, 1 - y)]
        for a in range(n):
            for j, (px, py) in enumerate(chips):
                coming = ins[a].at[:, pl.ds(4 * px + 2 * py + 1 - c, 1)]
                cp = pltpu.make_async_remote_copy(
                    src_ref=coming, dst_ref=coming, send_sem=ssem.at[3 * a + j], recv_sem=rsem.at[3 * a + j],
                    device_id=(x, y, 1 - c), device_id_type=MESH)
                cp.wait_send()
                cp.wait_recv()

    return pl.pallas_call(
        body,
        name=name,
        in_specs=[HBM] * n + [SEM, SEM, ANY],
        out_specs=[HBM] * n,
        out_shape=[pltpu.HBM(b.shape, b.dtype) for b in bufs],
        input_output_aliases={i: i for i in range(n)},
        compiler_params=pltpu.CompilerParams(has_side_effects=EFFECT),
    )(*bufs, send_sems, recv_sems, after)


def _chips_start(sums, name):
    n = len(sums)

    def body(*refs):
        ssem, rsem = refs[4 * n], refs[4 * n + 1]
        src, land = refs[4 * n + 2:5 * n + 2], refs[5 * n + 2:8 * n + 2]
        token = refs[8 * n + 2]
        x, y, c = _place()
        chips = [(1 - x, y), (x, 1 - y), (1 - x, 1 - y)]
        for a in range(n):
            for k, (px, py) in enumerate(chips):
                pltpu.make_async_remote_copy(
                    src_ref=src[a].at[:, pl.ds(2 * px + py, 1)], dst_ref=land[3 * a + k], send_sem=ssem.at[3 * a + k],
                    recv_sem=rsem.at[3 * a + k], device_id=(px, py, c), device_id_type=MESH).start()
        token[...] = jnp.zeros_like(token)

    lands = []
    for s in sums:
        lands += [lax.empty((s.shape[0], 1) + s.shape[2:], s.dtype) for _ in range(3)]
    outs = pl.pallas_call(
        body,
        name=name,
        in_specs=[HBM] * (4 * n),
        out_specs=[SEM, SEM] + [HBM] * (4 * n) + [TOKEN],
        out_shape=[pltpu.SemaphoreType.DMA((3 * n,))] * 2 + [pltpu.HBM(b.shape, b.dtype) for b in list(sums) + lands]
        + [jax.ShapeDtypeStruct((8, LANES), F32)],
        input_output_aliases={i: 2 + i for i in range(4 * n)},
        compiler_params=pltpu.CompilerParams(has_side_effects=EFFECT),
    )(*[_in_hbm(b) for b in list(sums) + lands])
    return outs[0], outs[1], list(outs[2:2 + n]), list(outs[2 + n:2 + 4 * n]), outs[-1]


def _chips_wait(send_sems, recv_sems, sums, lands, after, name):
    n = len(sums)

    def body(*refs):
        src, land = refs[:n], refs[n:4 * n]
        ssem, rsem = refs[4 * n], refs[4 * n + 1]
        x, y, c = _place()
        chips = [(1 - x, y), (x, 1 - y), (1 - x, 1 - y)]
        for a in range(n):
            for k, (px, py) in enumerate(chips):
                cp = pltpu.make_async_remote_copy(
                    src_ref=src[a].at[:, pl.ds(2 * px + py, 1)], dst_ref=land[3 * a + k], send_sem=ssem.at[3 * a + k],
                    recv_sem=rsem.at[3 * a + k], device_id=(px, py, c), device_id_type=MESH)
                cp.wait_send()
                cp.wait_recv()

    both = list(sums) + list(lands)
    outs = pl.pallas_call(
        body,
        name=name,
        in_specs=[HBM] * (4 * n) + [SEM, SEM, ANY],
        out_specs=[HBM] * (4 * n),
        out_shape=[pltpu.HBM(b.shape, b.dtype) for b in both],
        input_output_aliases={i: i for i in range(4 * n)},
        compiler_params=pltpu.CompilerParams(has_side_effects=EFFECT),
    )(*both, send_sems, recv_sems, after)
    return list(outs[:n]), [list(outs[n + 3 * a:n + 3 * a + 3]) for a in range(n)]


def _sibling_start(grads, name):
    n = len(grads)

    def body(*refs):
        ssem, rsem = refs[2 * n], refs[2 * n + 1]
        src, land = refs[2 * n + 2:3 * n + 2], refs[3 * n + 2:4 * n + 2]
        token = refs[4 * n + 2]
        x, y, c = _place()
        for a in range(n):
            pltpu.make_async_remote_copy(
                src_ref=src[a].at[:, :, pl.ds(1 - c, 1)], dst_ref=land[a], send_sem=ssem.at[a], recv_sem=rsem.at[a],
                device_id=(x, y, 1 - c), device_id_type=MESH).start()
        token[...] = jnp.zeros_like(token)

    lands = [lax.empty(g.shape[:2] + (1,) + g.shape[3:], g.dtype) for g in grads]
    both = list(grads) + lands
    outs = pl.pallas_call(
        body,
        name=name,
        in_specs=[HBM] * (2 * n),
        out_specs=[SEM, SEM] + [HBM] * (2 * n) + [TOKEN],
        out_shape=[pltpu.SemaphoreType.DMA((n,))] * 2 + [pltpu.HBM(b.shape, b.dtype) for b in both]
        + [jax.ShapeDtypeStruct((8, LANES), F32)],
        input_output_aliases={i: 2 + i for i in range(2 * n)},
        compiler_params=pltpu.CompilerParams(has_side_effects=EFFECT),
    )(*[_in_hbm(b) for b in both])
    return outs[0], outs[1], list(outs[2:2 + n]), list(outs[2 + n:2 + 2 * n]), outs[-1]


def _sibling_wait(send_sems, recv_sems, grads, lands, after, name):
    n = len(grads)

    def body(*refs):
        src, land = refs[:n], refs[n:2 * n]
        ssem, rsem = refs[2 * n], refs[2 * n + 1]
        x, y, c = _place()
        for a in range(n):
            cp = pltpu.make_async_remote_copy(
                src_ref=src[a].at[:, :, pl.ds(1 - c, 1)], dst_ref=land[a], send_sem=ssem.at[a], recv_sem=rsem.at[a],
                device_id=(x, y, 1 - c), device_id_type=MESH)
            cp.wait_send()
            cp.wait_recv()

    both = list(grads) + list(lands)
    outs = pl.pallas_call(
        body,
        name=name,
        in_specs=[HBM] * (2 * n) + [SEM, SEM, ANY],
        out_specs=[HBM] * (2 * n),
        out_shape=[pltpu.HBM(b.shape, b.dtype) for b in both],
        input_output_aliases={i: i for i in range(2 * n)},
        compiler_params=pltpu.CompilerParams(has_side_effects=EFFECT),
    )(*both, send_sems, recv_sems, after)
    return list(outs[:n]), list(outs[n:])


_SMALL = ("mix_norm", "q_norm", "k_norm", "sinks", "sgu_ln_g", "sgu_ln_b", "w_spatial", "b_spatial", "ffn_norm")


def _pack_rows(a):
    flat = a.reshape(-1)
    pad = (-flat.shape[0]) % LANES
    if pad:
        flat = jnp.pad(flat, (0, pad))
    return flat.reshape(-1, LANES)


def _pack(values):
    rows = jnp.concatenate([_pack_rows(values[k]) for k in _SMALL], axis=0)
    pad = (-rows.shape[0]) % 8
    if pad:
        rows = jnp.pad(rows, ((0, pad), (0, 0)))
    return rows


def _unpack(rows, like):
    out, at = {}, 0
    for k in _SMALL:
        size = like[k].size
        nrows = -(-size // LANES)
        out[k] = rows[at:at + nrows].reshape(-1)[:size].reshape(like[k].shape)
        at += nrows
    return out


def _rope_tables(t, wq):
    pos = jnp.arange(t, dtype=F32)
    inv_freq = jnp.power(ROPE_THETA, -jnp.arange(0, HEAD_DIM, 2, dtype=F32) / HEAD_DIM)
    ang = pos[:, None] * inv_freq[None, :]
    cos, sin = jnp.cos(ang), jnp.sin(ang)
    reps = wq // HEAD_DIM
    return (jnp.tile(jnp.concatenate([cos, cos], axis=1), (1, reps)),
            jnp.tile(jnp.concatenate([-sin, sin], axis=1), (1, reps)))


def kernel(x, mix_norm, w_in, q_norm, k_norm, sinks, sgu_ln_g, sgu_ln_b, w_spatial, b_spatial, w_attn_branch, w_sgu_branch, w_out, ffn_norm, w_gate, w_up, w_down, loss_target, m_mix_norm, m_w_in, m_q_norm, m_k_norm, m_sinks, m_sgu_ln_g, m_sgu_ln_b, m_w_spatial, m_b_spatial, m_w_attn_branch, m_w_sgu_branch, m_w_out, m_ffn_norm, m_w_gate, m_w_up, m_w_down, v_mix_norm, v_w_in, v_q_norm, v_k_norm, v_sinks, v_sgu_ln_g, v_sgu_ln_b, v_w_spatial, v_b_spatial, v_w_attn_branch, v_w_sgu_branch, v_w_out, v_ffn_norm, v_w_gate, v_w_up, v_w_down):
    names = ("mix_norm", "w_in", "q_norm", "k_norm", "sinks", "sgu_ln_g", "sgu_ln_b", "w_spatial", "b_spatial",
             "w_attn_branch", "w_sgu_branch", "w_out", "ffn_norm", "w_gate", "w_up", "w_down")
    weights = dict(zip(names, (mix_norm, w_in, q_norm, k_norm, sinks, sgu_ln_g, sgu_ln_b, w_spatial, b_spatial,
                               w_attn_branch, w_sgu_branch, w_out, ffn_norm, w_gate, w_up, w_down)))
    mom1 = dict(zip(names, (m_mix_norm, m_w_in, m_q_norm, m_k_norm, m_sinks, m_sgu_ln_g, m_sgu_ln_b, m_w_spatial,
                            m_b_spatial, m_w_attn_branch, m_w_sgu_branch, m_w_out, m_ffn_norm, m_w_gate, m_w_up,
                            m_w_down)))
    mom2 = dict(zip(names, (v_mix_norm, v_w_in, v_q_norm, v_k_norm, v_sinks, v_sgu_ln_g, v_sgu_ln_b, v_w_spatial,
                            v_b_spatial, v_w_attn_branch, v_w_sgu_branch, v_w_out, v_ffn_norm, v_w_gate, v_w_up,
                            v_w_down)))
    depth = w_in.shape[0]
    _, t, d = x.shape
    n_q_heads = sinks.shape[1]
    wq = n_q_heads * HEAD_DIM
    wk = wq // Q_PER_KV
    ws = sgu_ln_g.shape[1]
    ng = ws // LANES
    off_u = wq + 2 * wk
    off_g = off_u + 2 * ws
    tables = _rope_tables(t, wq)
    px, py, pc = _place()
    core = pc.astype(jnp.int32)[None]
    chip = (2 * px + py).astype(jnp.int32)[None]
    dev = (4 * px + 2 * py + pc).astype(jnp.int32)[None]

    layers = range(depth)
    chunks = ((0,), (1, 2, 3), (4,), (5,))
    pending, token = [], None
    sources = [[jnp.swapaxes(w_in, 1, 2)], [jnp.swapaxes(w_attn_branch, 1, 2)], [jnp.swapaxes(w_sgu_branch, 1, 2)],
               [w_out], [jnp.swapaxes(w_gate, 1, 2), jnp.swapaxes(w_up, 1, 2)], [w_down]]
    for l in layers:
        pending.append([])
        for ci, chunk in enumerate(chunks):
            bufs = [_place_shard(sources[a], l, dev, BF16, f"place_shard_{l}_{a}", after=token if a == chunk[0] else None)
                    for a in chunk]
            started, token = _gather_start([bufs], f"gather_start_{l}_{ci}")
            pending[l].append(started[0])

    passing = {}

    def arrive(l, ci, after):
        send_sems, recv_sems, bufs = pending[l][ci]
        bufs = _gather_wait(send_sems, recv_sems, bufs, after, f"gather_wait_{l}_{ci}")
        passing[l, ci] = _forward_start(bufs, f"forward_start_{l}_{ci}")
        return passing[l, ci][3]

    def ready(l, ci, after):
        send_sems, recv_sems, bufs, _ = passing.pop((l, ci))
        bufs = _forward_wait(send_sems, recv_sems, bufs, after, f"forward_wait_{l}_{ci}")
        return [f.reshape(f.shape[0] * f.shape[1] * f.shape[2], f.shape[3]) for f in bufs]

    saved = []
    xl = x[0]
    going = arrive(0, 0, token)
    for l in layers:
        gq = jnp.tile(q_norm[l], n_q_heads)[None]
        gk = jnp.tile(k_norm[l], n_q_heads // Q_PER_KV)[None]
        bt = b_spatial[l].T
        h = _rmsnorm_fwd(xl, mix_norm[l][None], f"mix_norm_fwd_{l}", after=going)
        (win_t,) = ready(l, 0, h)
        proj = _mm(h, win_t, "nt", F32, f"in_proj_{l}")
        going = arrive(l, 1, proj)
        attn = _attn_fwd(proj, tables, gq, gk, sinks[l], wq, wk, f"attn_fwd_{l}", after=going)
        sgu = _sgu_fwd(proj, sgu_ln_g[l][None], sgu_ln_b[l][None], w_spatial[l], bt, off_u, ws, f"sgu_fwd_{l}")
        wab_t, wsb_t, wo = ready(l, 1, sgu)
        br_a = _mm(attn, wab_t, "nt", F32, f"attn_branch_{l}")
        br_b = _mm(sgu, wsb_t, "nt", F32, f"sgu_branch_{l}")
        merged = _merge_fwd(br_a, br_b, proj, off_g, f"merge_fwd_{l}")
        x1 = _mm(merged, wo, "nn", F32, f"out_proj_{l}", residual=xl)
        going = arrive(l, 2, x1)
        h2 = _rmsnorm_fwd(x1, ffn_norm[l][None], f"ffn_norm_fwd_{l}", after=going)
        (wgu_t,) = ready(l, 2, h2)
        going = arrive(l, 3, h2) if l > 0 else None
        gu, act = _gate_up_fwd(h2, wgu_t, f"gate_up_{l}", after=going)
        if l == 0:
            arrive(l, 3, act)
        (wd,) = ready(l, 3, act)
        x2 = _mm(act, wd, "nn", F32, f"down_proj_{l}", residual=x1)
        if l + 1 < depth:
            going = arrive(l + 1, 0, x2)
        saved.append(dict(x0=xl, h=h, proj=proj, attn=attn, sgu=sgu, br_a=br_a, br_b=br_b, merged=merged, x1=x1,
                          h2=h2, gu=gu, act=act, gq=gq, gk=gk, bt=bt, win_t=win_t, wab_t=wab_t, wsb_t=wsb_t, wo=wo,
                          wgu_t=wgu_t, wd=wd))
        xl = x2

    loss_part, dx, dx16 = _loss_and_grad(xl, loss_target[0], "loss")
    loss = lax.psum(loss_part[0, 0], ("x", "y", "c"))

    def sibling_start(grads, tag):
        shaped = []
        for g, p in grads:
            rows, c = g.shape
            shaped.append(g.reshape(p, 4, 2, rows // (8 * p), c))
        send_sems, recv_sems, shaped, lands, tok = _sibling_start(shaped, f"rs_sibling_start_{tag}")
        return (send_sems, recv_sems, shaped, lands, tag), tok

    def chips_start(state, after):
        send_sems, recv_sems, shaped, lands, tag = state
        shaped, lands = _sibling_wait(send_sems, recv_sems, shaped, lands, after, f"rs_sibling_wait_{tag}")
        sums = [_sum_sibling(g, o, core, f"rs_add_sibling_{tag}_{a}") for a, (g, o) in enumerate(zip(shaped, lands))]
        send_sems, recv_sems, sums, lands, tok = _chips_start(sums, f"rs_chips_start_{tag}")
        return (send_sems, recv_sems, sums, lands, tag), tok

    def scatter_finish(state, after):
        send_sems, recv_sems, sums, lands, tag = state
        sums, lands = _chips_wait(send_sems, recv_sems, sums, lands, after, f"rs_chips_wait_{tag}")
        return [[s] + o for s, o in zip(sums, lands)]

    in_flight = [dict() for _ in layers]
    small_grads = [None] * depth
    tok, swap_in = None, None
    for l in reversed(layers):
        s = saved[l]
        dgu = _gate_up_bwd(dx16, s["wd"], s["gu"], f"d_gate_up_{l}", after=tok)
        if swap_in is not None:
            in_flight[l + 1]["in"], tok = chips_start(swap_in, dgu)
        g_wd = _mm(s["act"], dx16, "tn", BF16, f"g_w_down_{l}", after=tok)
        swap, tok_s = sibling_start([(g_wd, 1)], f"{l}_down")
        dh2 = _mm(dgu, s["wgu_t"], "nn", F32, f"d_h2_{l}", after=tok_s)
        in_flight[l]["down"], tok = chips_start(swap, dh2)
        g_wgu_t = _mm(dgu, s["h2"], "tn", BF16, f"g_w_gate_up_{l}", after=tok)
        swap, tok_s = sibling_start([(g_wgu_t, 2)], f"{l}_gate_up")
        dx1, dx1_16, g_ffn = _rmsnorm_bwd(s["x1"], ffn_norm[l][None], dh2, dx, f"ffn_norm_bwd_{l}")
        dmerged = _mm(dx1_16, s["wo"], "nt", F32, f"d_merged_{l}", after=tok_s)
        in_flight[l]["gate_up"], tok = chips_start(swap, dmerged)
        g_wo = _mm(s["merged"], dx1_16, "tn", BF16, f"g_w_out_{l}", after=tok)
        d_a, d_b, dla, dlb = _merge_bwd(s["br_a"], s["br_b"], s["proj"], dmerged, off_g, f"merge_bwd_{l}")
        dattn = _mm(d_a, s["wab_t"], "nn", F32, f"d_attn_{l}")
        g_wab_t = _mm(d_a, s["attn"], "tn", BF16, f"g_w_attn_branch_{l}")
        dsgu = _mm(d_b, s["wsb_t"], "nn", F32, f"d_sgu_{l}")
        g_wsb_t = _mm(d_b, s["sgu"], "tn", BF16, f"g_w_sgu_branch_{l}")
        swap, tok_s = sibling_start([(g_wab_t, 1), (g_wsb_t, 1), (g_wo, 1)], f"{l}_mix")
        dq, dk, dv, g_gq, g_gk, g_sinks = _attn_bwd(s["proj"], dattn, tables, s["gq"], s["gk"], sinks[l], wq, wk,
                                                    f"attn_bwd_{l}")
        du, dvv, g_lng, g_lnb, g_ws, g_bs = _sgu_bwd(s["proj"], dsgu, sgu_ln_g[l][None], sgu_ln_b[l][None],
                                                     w_spatial[l], s["bt"], off_u, ws, f"sgu_bwd_{l}")
        dproj = jnp.concatenate([dq, dk.astype(BF16), dv.astype(BF16), du, dvv, dla, dlb], axis=1)
        dh = _mm(dproj, s["win_t"], "nn", F32, f"d_h_{l}", after=tok_s)
        in_flight[l]["mix"], tok = chips_start(swap, dh)
        g_win_t = _mm(dproj, s["h"], "tn", BF16, f"g_w_in_{l}", after=tok)
        swap_in, tok = sibling_start([(g_win_t, 1)], f"{l}_in")
        dx, dx16, g_mix = _rmsnorm_bwd(s["x0"], mix_norm[l][None], dh, dx1, f"mix_norm_bwd_{l}")
        small_grads[l] = dict(
            mix_norm=g_mix[0], q_norm=g_gq[0].reshape(n_q_heads, HEAD_DIM).sum(0),
            k_norm=g_gk[0].reshape(n_q_heads // Q_PER_KV, HEAD_DIM).sum(0), sinks=g_sinks[0, :n_q_heads],
            sgu_ln_g=g_lng[0], sgu_ln_b=g_lnb[0], w_spatial=g_ws, b_spatial=g_bs[:, 0, :], ffn_norm=g_ffn[0])
    grad_x = dx[None]

    result = {key: {} for key in ("grad", "delta", "m", "v")}
    layer_like = {k: weights[k][0] for k in _SMALL}
    packed_g = jnp.concatenate([_pack(small_grads[l]) for l in layers], axis=0)
    rows_per_layer = packed_g.shape[0] // depth
    small_buf = _place_shard([packed_g[None]], 0, dev, F32, "place_small_grads", after=tok)
    (small_started,), tok = _gather_start([[small_buf]], "gather_start_small")
    in_flight[0]["in"], tok = chips_start(swap_in, tok)

    def update(k, grads, transposed, after):
        view = (lambda a: jnp.swapaxes(a, 1, 2)) if transposed else (lambda a: a)
        outs = _adam(view(weights[k]), grads, view(mom1[k]), view(mom2[k]), chip, f"adam_{k}", after=after)
        for key, val in zip(("grad", "delta", "m", "v"), outs):
            result[key][k] = view(val)
        return outs[3]

    def plain(terms, tag):
        s, lands = terms[0], terms[1:]
        g = _sum_chips(s, lands, chip, f"rs_add_chips_{tag}")
        return [jnp.swapaxes(g, 1, 2)[:, None]]

    down = [scatter_finish(in_flight[l]["down"], tok) for l in reversed(layers)][::-1]
    tok = update("w_down", [(down[l][0], 0) for l in layers], False, None)
    gate_up = [scatter_finish(in_flight[l]["gate_up"], tok) for l in reversed(layers)][::-1]
    tok = update("w_gate", [(gate_up[l][0], 0) for l in layers], True, None)
    tok = update("w_up", [(gate_up[l][0], 1) for l in layers], True, tok)
    mix = [scatter_finish(in_flight[l]["mix"], tok) for l in reversed(layers)][::-1]
    tok = update("w_out", [(mix[l][2], 0) for l in layers], False, None)
    tok = update("w_attn_branch", [(plain(mix[l][0], f"{l}_attn_branch"), 0) for l in layers], False, tok)
    tok = update("w_sgu_branch", [(plain(mix[l][1], f"{l}_sgu_branch"), 0) for l in layers], False, tok)

    send_sems, recv_sems, small_bufs = small_started
    small_bufs = _gather_wait(send_sems, recv_sems, small_bufs, tok, "gather_wait_small")
    send_sems, recv_sems, small_bufs, tok = _forward_start(small_bufs, "forward_start_small")
    packed = [jnp.concatenate([_pack({k: src[k][l] for k in _SMALL}) for l in layers], axis=0)
              for src in (weights, mom1, mom2)]
    (gathered_small,) = _forward_wait(send_sems, recv_sems, small_bufs, packed[0], "forward_wait_small")
    small = _small_reduce_adam(gathered_small[0], *packed, "small_reduce_adam")
    for key, rows in zip(("grad", "delta", "m", "v"), small):
        per_layer = [_unpack(rows[l * rows_per_layer:(l + 1) * rows_per_layer], layer_like) for l in layers]
        for k in _SMALL:
            result[key][k] = jnp.stack([per_layer[l][k] for l in layers])

    last = [scatter_finish(in_flight[l]["in"], small[0]) for l in reversed(layers)][::-1]
    update("w_in", [(last[l][0], 0) for l in layers], True, None)

    return (loss, grad_x, *[result["grad"][k] for k in names], *[result["delta"][k] for k in names],
            *[result["m"][k] for k in names], *[result["v"][k] for k in names])
```

```python
import functools
import math

import jax
import jax.numpy as jnp
from jax import lax
from jax.experimental import pallas as pl
from jax.experimental.pallas import tpu as pltpu

F32 = jnp.float32
BF16 = jnp.bfloat16
MESH = pl.DeviceIdType.MESH
ANY = pl.BlockSpec(memory_space=pl.ANY)

N_DEV = 8
HEAD_DIM = 64
Q_PER_KV = 4
BLOCK = 128
LANES = 128
ROPE_THETA = 10000.0
EPS = 1e-6
ADAM_LR = 0.001
ADAM_B1 = 0.9
ADAM_B2 = 0.999
ADAM_EPS = 1e-08
ADAM_WD = 0.01
ADAM_STEP = 10
NEG = -1e30
VMEM_LIMIT_BYTES = 56 * 1024 * 1024

NN = ((1,), (0,))
NT = ((1,), (1,))
TN = ((0,), (0,))


def _dot(a, b, dims):
    return lax.dot_general(a, b, (dims, ((), ())), preferred_element_type=F32)


def _params(*sem):
    return pltpu.CompilerParams(dimension_semantics=sem, vmem_limit_bytes=VMEM_LIMIT_BYTES)


def _divisor_tile(n, limit, unit):
    if n <= limit:
        return n
    best = unit
    for t in range(unit, limit + 1, unit):
        if n % t == 0:
            best = t
    assert n % best == 0, (n, limit, unit)
    return best


def _mm(a, b, mode, out_dtype, name, residual=None, after=None):
    parts = a.shape[0] if a.ndim == 3 else 1
    a2 = a.shape[-2:]
    if mode == "nn":
        (m, kp), (k2, n) = a2, b.shape
        k, mp = kp * parts, m
    elif mode == "nt":
        (m, kp), (n, k2) = a2, b.shape
        k, mp = kp * parts, m
    else:
        (k, mp), (k2, n) = a2, b.shape
        m, kp = mp * parts, k
    assert k == k2, (name, a.shape, b.shape)
    tk = _divisor_tile(kp, 2816, 128)
    nk = k // tk
    tm = _divisor_tile(mp, 512 if mode == "tn" else 1024, 128)
    tn = _divisor_tile(n, 2048 if mode == "tn" else (1024 if nk > 1 else 512), 128)
    kpb, mpb = kp // tk, mp // tm
    dims = {"nn": NN, "nt": NT, "tn": TN}[mode]
    lead = (None,) if a.ndim == 3 else ()
    if mode == "tn":
        a_index = lambda i, j, kk: (i // mpb, kk, i % mpb) if lead else (kk, i)
        a_spec = pl.BlockSpec(lead + (tk, tm), a_index)
    else:
        a_index = lambda i, j, kk: (kk // kpb, i, kk % kpb) if lead else (i, kk)
        a_spec = pl.BlockSpec(lead + (tm, tk), a_index)
    if mode == "nt":
        b_spec = pl.BlockSpec((tn, tk), lambda i, j, kk: (j, kk))
    else:
        b_spec = pl.BlockSpec((tk, tn), lambda i, j, kk: (kk, j))
    o_spec = pl.BlockSpec((tm, tn), lambda i, j, kk: (i, j))
    has_res = residual is not None

    def body(*refs):
        a_ref, b_ref = refs[:2]
        r_ref = refs[2] if has_res else None
        o_ref, acc_ref = refs[-2:]
        kk = pl.program_id(2)
        p = _dot(a_ref[...], b_ref[...], dims)

        def finish(total):
            if has_res:
                total = total + r_ref[...]
            o_ref[...] = total.astype(o_ref.dtype)

        if nk == 1:
            finish(p)
        else:
            @pl.when(kk == 0)
            def _():
                acc_ref[...] = p

            @pl.when(jnp.logical_and(kk > 0, kk < nk - 1))
            def _():
                acc_ref[...] += p

            @pl.when(kk == nk - 1)
            def _():
                finish(acc_ref[...] + p)

    in_specs = [a_spec, b_spec] + ([o_spec] if has_res else []) + ([ANY] if after is not None else [])
    args = (a, b) + ((residual,) if has_res else ()) + ((after,) if after is not None else ())
    acc_shape = (tm, tn) if nk > 1 else (8, LANES)
    return pl.pallas_call(
        body,
        name=name,
        grid=(m // tm, n // tn, nk),
        in_specs=in_specs,
        out_specs=o_spec,
        out_shape=jax.ShapeDtypeStruct((m, n), out_dtype),
        scratch_shapes=[pltpu.VMEM(acc_shape, F32)],
        compiler_params=_params("parallel", "parallel", "arbitrary"),
    )(*args)


def _rmsnorm_fwd(x, g, name, after=None):
    t, d = x.shape
    tr = _divisor_tile(t, 256, 8)

    def body(x_ref, g_ref, *rest):
        h_ref = rest[-1]
        xv = x_ref[...]
        rstd = lax.rsqrt(jnp.mean(xv * xv, axis=-1, keepdims=True) + EPS)
        h_ref[...] = (xv * rstd * g_ref[...]).astype(h_ref.dtype)

    return pl.pallas_call(
        body,
        name=name,
        grid=(t // tr,),
        in_specs=[pl.BlockSpec((tr, d), lambda i: (i, 0)), pl.BlockSpec((1, d), lambda i: (0, 0))]
        + ([ANY] if after is not None else []),
        out_specs=pl.BlockSpec((tr, d), lambda i: (i, 0)),
        out_shape=jax.ShapeDtypeStruct((t, d), BF16),
        compiler_params=_params("parallel"),
    )(x, g, *(() if after is None else (after,)))


def _rmsnorm_bwd(x, g, dh, dres, name, after=None):
    t, d = x.shape
    tr = _divisor_tile(t, 256, 8)

    def body(x_ref, g_ref, dh_ref, dres_ref, *rest):
        dx_ref, dx16_ref, dg_ref = rest[-3:]
        i = pl.program_id(0)
        xv = x_ref[...]
        rstd = lax.rsqrt(jnp.mean(xv * xv, axis=-1, keepdims=True) + EPS)
        xh = xv * rstd
        dhv = dh_ref[...]
        dxh = dhv * g_ref[...]
        dx = dres_ref[...] + rstd * (dxh - xh * jnp.mean(dxh * xh, axis=-1, keepdims=True))
        dx_ref[...] = dx
        dx16_ref[...] = dx.astype(dx16_ref.dtype)
        part = jnp.broadcast_to(jnp.sum(dhv * xh, axis=0, keepdims=True), dg_ref.shape)

        @pl.when(i == 0)
        def _():
            dg_ref[...] = part

        @pl.when(i > 0)
        def _():
            dg_ref[...] += part

    row = pl.BlockSpec((tr, d), lambda i: (i, 0))
    return pl.pallas_call(
        body,
        name=name,
        grid=(t // tr,),
        in_specs=[row, pl.BlockSpec((1, d), lambda i: (0, 0)), row, row] + ([ANY] if after is not None else []),
        out_specs=[row, row, pl.BlockSpec((8, d), lambda i: (0, 0))],
        out_shape=[jax.ShapeDtypeStruct((t, d), F32), jax.ShapeDtypeStruct((t, d), BF16),
                   jax.ShapeDtypeStruct((8, d), F32)],
        compiler_params=_params("arbitrary"),
    )(x, g, dh, dres, *(() if after is None else (after,)))


def _lane(shape):
    return lax.broadcasted_iota(jnp.int32, shape, 1)


def _group_sum64(s):
    row = lax.broadcasted_iota(jnp.int32, (LANES, LANES), 0)
    col = lax.broadcasted_iota(jnp.int32, (LANES, LANES), 1)
    ones = jnp.where((row >= HEAD_DIM) == (col >= HEAD_DIM), 1.0, 0.0).astype(BF16)
    out = []
    for t in range(s.shape[1] // LANES):
        piece = s[:, LANES * t:LANES * t + LANES]
        hi = piece.astype(BF16)
        lo = (piece - hi.astype(F32)).astype(BF16)
        out.append(_dot(hi, ones, NN) + _dot(lo, ones, NN))
    return out[0] if len(out) == 1 else jnp.concatenate(out, axis=1)


def _swap32(x):
    w = x.shape[1]
    return jnp.where((_lane(x.shape) & 32) == 0, pltpu.roll(x, w - 32, axis=1), pltpu.roll(x, 32, axis=1))


def _rope(x, c, s):
    return x * c + _swap32(x) * s


def _rope_t(dy, c, s):
    return dy * c + _swap32(dy * s)


def _head_norm(x):
    rstd = lax.rsqrt(_group_sum64(x * x) * (1.0 / HEAD_DIM) + EPS)
    return x * rstd, rstd


def _head_norm_bwd(dxh, xh, rstd):
    return rstd * (dxh - xh * (_group_sum64(dxh * xh) * (1.0 / HEAD_DIM)))


def _roll64(x):
    return pltpu.roll(x, 64, axis=1)


def _attn_specs(wq, wk):
    kb = wq // wk
    prev = lambda i: jnp.maximum(i - 1, 0)
    return dict(
        q=pl.BlockSpec((BLOCK, wq), lambda i: (i, 0)),
        kc=pl.BlockSpec((BLOCK, wk), lambda i: (i, kb)),
        kp=pl.BlockSpec((BLOCK, wk), lambda i: (prev(i), kb)),
        vc=pl.BlockSpec((BLOCK, wk), lambda i: (i, kb + 1)),
        vp=pl.BlockSpec((BLOCK, wk), lambda i: (prev(i), kb + 1)),
        tq=pl.BlockSpec((BLOCK, wq), lambda i: (i, 0)),
        tkp=pl.BlockSpec((BLOCK, wk), lambda i: (prev(i), 0)),
        gq=pl.BlockSpec((1, wq), lambda i: (0, 0)),
        gk=pl.BlockSpec((1, wk), lambda i: (0, 0)),
        sinks=pl.BlockSpec(memory_space=pltpu.SMEM),
    )


def _attn_prologue(i, q_ref, kc_ref, kp_ref, cq_ref, sq_ref, ckp_ref, skp_ref, gq_ref, gk_ref):
    wk = kc_ref.shape[1]
    cq, sq = cq_ref[...], sq_ref[...]
    ck, sk = cq[:, :wk], sq[:, :wk]
    qh, q_rstd = _head_norm(q_ref[...])
    kch, kc_rstd = _head_norm(kc_ref[...])
    kph, kp_rstd = _head_norm(kp_ref[...])
    qn = _rope(qh * gq_ref[...], cq, sq)
    knc = _rope(kch * gk_ref[...], ck, sk)
    knp = _rope(kph * gk_ref[...], ckp_ref[...], skp_ref[...])
    row = lax.broadcasted_iota(jnp.int32, (BLOCK, BLOCK), 0)
    col = lax.broadcasted_iota(jnp.int32, (BLOCK, BLOCK), 1)
    mask_c = col <= row
    mask_p = jnp.logical_and(col > row, i > 0)
    half = (col >= 64).astype(jnp.int32)
    return dict(cq=cq, sq=sq, ck=ck, sk=sk, qh=qh, q_rstd=q_rstd, kch=kch, kc_rstd=kc_rstd, kph=kph,
                kp_rstd=kp_rstd, qn=qn, knc=knc, knp=knp, mask_c=mask_c, mask_p=mask_p, half=half)


def _head_scores(st, t, e, sink, scale):
    g = (2 * t) // Q_PER_KV
    ks, kpar = g // 2, g % 2
    sl = slice(LANES * ks, LANES * ks + LANES)
    mine = st["half"] == e
    qm = jnp.where(mine, st["qn"][:, LANES * t:LANES * t + LANES], 0.0).astype(BF16)
    kc, kp = st["knc"][:, sl], st["knp"][:, sl]
    flip = e != kpar
    if flip:
        kc, kp = _roll64(kc), _roll64(kp)
    kc, kp = kc.astype(BF16), kp.astype(BF16)
    s_c = jnp.where(st["mask_c"], _dot(qm, kc, NT) * scale, NEG)
    s_p = jnp.where(st["mask_p"], _dot(qm, kp, NT) * scale, NEG)
    m = jnp.maximum(jnp.maximum(jnp.max(s_c, axis=1, keepdims=True), jnp.max(s_p, axis=1, keepdims=True)), sink)
    p_c, p_p = jnp.exp(s_c - m), jnp.exp(s_p - m)
    p_s = jnp.exp(sink - m)
    inv = 1.0 / (jnp.sum(p_c, axis=1, keepdims=True) + jnp.sum(p_p, axis=1, keepdims=True) + p_s)
    return dict(sl=sl, mine=mine, flip=flip, qm=qm, kc=kc, kp=kp, pr_c=p_c * inv, pr_p=p_p * inv, pr_s=p_s * inv)


def _attn_fwd(proj, tables, gq, gk, sinks, wq, wk, name, after=None):
    t = proj.shape[0]
    nb = t // BLOCK
    sp = _attn_specs(wq, wk)
    scale = HEAD_DIM ** -0.5
    cos_t, sin_t = tables

    def body(sinks_ref, q_ref, kc_ref, kp_ref, vc_ref, vp_ref, cq_ref, sq_ref, ckp_ref, skp_ref, gq_ref, gk_ref,
             *rest):
        o_ref = rest[-1]
        i = pl.program_id(0)
        st = _attn_prologue(i, q_ref, kc_ref, kp_ref, cq_ref, sq_ref, ckp_ref, skp_ref, gq_ref, gk_ref)
        vc_all, vp_all = vc_ref[...], vp_ref[...]
        for ts in range(wq // LANES):
            acc = jnp.zeros((BLOCK, LANES), F32)
            for e in (0, 1):
                hs = _head_scores(st, ts, e, sinks_ref[2 * ts + e], scale)
                vc, vp = vc_all[:, hs["sl"]], vp_all[:, hs["sl"]]
                if hs["flip"]:
                    vc, vp = _roll64(vc), _roll64(vp)
                vc = jnp.where(hs["mine"], vc, 0.0).astype(BF16)
                vp = jnp.where(hs["mine"], vp, 0.0).astype(BF16)
                acc = acc + _dot(hs["pr_c"].astype(BF16), vc, NN) + _dot(hs["pr_p"].astype(BF16), vp, NN)
            o_ref[:, LANES * ts:LANES * ts + LANES] = acc.astype(o_ref.dtype)

    return pl.pallas_call(
        body,
        name=name,
        grid=(nb,),
        in_specs=[sp["sinks"], sp["q"], sp["kc"], sp["kp"], sp["vc"], sp["vp"], sp["tq"], sp["tq"], sp["tkp"],
                  sp["tkp"], sp["gq"], sp["gk"]] + ([ANY] if after is not None else []),
        out_specs=pl.BlockSpec((BLOCK, wq), lambda i: (i, 0)),
        out_shape=jax.ShapeDtypeStruct((t, wq), BF16),
        compiler_params=_params("parallel"),
    )(sinks, proj, proj, proj, proj, proj, cos_t, sin_t, cos_t, sin_t, gq, gk, *(() if after is None else (after,)))


def _attn_bwd(proj, dout, tables, gq, gk, sinks, wq, wk, name, after=None):
    t = proj.shape[0]
    nb = t // BLOCK
    sp = _attn_specs(wq, wk)
    scale = HEAD_DIM ** -0.5
    cos_t, sin_t = tables

    def body(sinks_ref, q_ref, kc_ref, kp_ref, vc_ref, vp_ref, cq_ref, sq_ref, ckp_ref, skp_ref, gq_ref, gk_ref,
             do_ref, *rest):
        dq_ref, dk_ref, dv_ref, dgq_ref, dgk_ref, dsk_ref, dqn_ref, dknc_ref, dknp_ref, dvc_ref, dvp_ref = rest[-11:]
        i = pl.program_id(0)
        st = _attn_prologue(i, q_ref, kc_ref, kp_ref, cq_ref, sq_ref, ckp_ref, skp_ref, gq_ref, gk_ref)
        vc_all, vp_all = vc_ref[...], vp_ref[...]
        dknc_ref[...] = jnp.zeros_like(dknc_ref)
        dknp_ref[...] = jnp.zeros_like(dknp_ref)
        dvc_ref[...] = jnp.zeros_like(dvc_ref)
        dvp_ref[...] = jnp.zeros_like(dvp_ref)
        lane8 = _lane((8, LANES))
        dsinks = jnp.zeros((8, LANES), F32)
        for ts in range(wq // LANES):
            dq_acc = jnp.zeros((BLOCK, LANES), F32)
            for e in (0, 1):
                hs = _head_scores(st, ts, e, sinks_ref[2 * ts + e], scale)
                sl, flip = hs["sl"], hs["flip"]
                vc, vp = vc_all[:, sl], vp_all[:, sl]
                if flip:
                    vc, vp = _roll64(vc), _roll64(vp)
                dom = jnp.where(hs["mine"], do_ref[:, LANES * ts:LANES * ts + LANES], 0.0).astype(BF16)
                dp_c = _dot(dom, vc.astype(BF16), NT)
                dp_p = _dot(dom, vp.astype(BF16), NT)
                pr_c, pr_p = hs["pr_c"], hs["pr_p"]
                rs = jnp.sum(pr_c * dp_c, axis=1, keepdims=True) + jnp.sum(pr_p * dp_p, axis=1, keepdims=True)
                ds_c = (pr_c * (dp_c - rs) * scale)
                ds_p = (pr_p * (dp_p - rs) * scale)
                dsink = jnp.sum(-hs["pr_s"] * rs)
                dsinks = dsinks + jnp.where(lane8 == 2 * ts + e, dsink, 0.0)
                dq_acc = dq_acc + jnp.where(
                    hs["mine"], _dot(ds_c.astype(BF16), hs["kc"], NN) + _dot(ds_p.astype(BF16), hs["kp"], NN), 0.0)
                dv_c = _dot(pr_c.T.astype(BF16), dom, NN)
                dv_p = _dot(pr_p.T.astype(BF16), dom, NN)
                dk_c = _dot(ds_c.T.astype(BF16), hs["qm"], NN)
                dk_p = _dot(ds_p.T.astype(BF16), hs["qm"], NN)
                if flip:
                    dv_c, dv_p, dk_c, dk_p = _roll64(dv_c), _roll64(dv_p), _roll64(dk_c), _roll64(dk_p)
                dvc_ref[:, sl] += dv_c
                dvp_ref[:, sl] += dv_p
                dknc_ref[:, sl] += dk_c
                dknp_ref[:, sl] += dk_p
            dqn_ref[:, LANES * ts:LANES * ts + LANES] = dq_acc

        gqv, gkv = gq_ref[...], gk_ref[...]
        dqg = _rope_t(dqn_ref[...], st["cq"], st["sq"])
        dq_ref[...] = _head_norm_bwd(dqg * gqv, st["qh"], st["q_rstd"]).astype(dq_ref.dtype)
        dkcg = _rope_t(dknc_ref[...], st["ck"], st["sk"])
        dkpg = _rope_t(dknp_ref[...], ckp_ref[...], skp_ref[...])
        dk_cur = _head_norm_bwd(dkcg * gkv, st["kch"], st["kc_rstd"])
        dk_prev = _head_norm_bwd(dkpg * gkv, st["kph"], st["kp_rstd"])
        dgq_part = jnp.broadcast_to(jnp.sum(dqg * st["qh"], axis=0, keepdims=True), dgq_ref.shape)
        dgk_part = jnp.broadcast_to(
            jnp.sum(dkcg * st["kch"] + dkpg * st["kph"], axis=0, keepdims=True), dgk_ref.shape)
        cur = pl.ds(pl.multiple_of(i * BLOCK, BLOCK), BLOCK)
        dk_ref[cur, :] = dk_cur
        dv_ref[cur, :] = dvc_ref[...]

        @pl.when(i == 0)
        def _():
            dgq_ref[...] = dgq_part
            dgk_ref[...] = dgk_part
            dsk_ref[...] = dsinks

        @pl.when(i > 0)
        def _():
            before = pl.ds(pl.multiple_of((i - 1) * BLOCK, BLOCK), BLOCK)
            dk_ref[before, :] += dk_prev
            dv_ref[before, :] += dvp_ref[...]
            dgq_ref[...] += dgq_part
            dgk_ref[...] += dgk_part
            dsk_ref[...] += dsinks

    whole = lambda shape: pl.BlockSpec(shape, lambda i: (0, 0))
    return pl.pallas_call(
        body,
        name=name,
        grid=(nb,),
        in_specs=[sp["sinks"], sp["q"], sp["kc"], sp["kp"], sp["vc"], sp["vp"], sp["tq"], sp["tq"], sp["tkp"],
                  sp["tkp"], sp["gq"], sp["gk"], pl.BlockSpec((BLOCK, wq), lambda i: (i, 0))]
        + ([ANY] if after is not None else []),
        out_specs=[pl.BlockSpec((BLOCK, wq), lambda i: (i, 0)), whole((t, wk)), whole((t, wk)), whole((8, wq)),
                   whole((8, wk)), whole((8, LANES))],
        out_shape=[jax.ShapeDtypeStruct((t, wq), BF16), jax.ShapeDtypeStruct((t, wk), F32),
                   jax.ShapeDtypeStruct((t, wk), F32), jax.ShapeDtypeStruct((8, wq), F32),
                   jax.ShapeDtypeStruct((8, wk), F32), jax.ShapeDtypeStruct((8, LANES), F32)],
        scratch_shapes=[pltpu.VMEM((BLOCK, wq), F32), pltpu.VMEM((BLOCK, wk), F32), pltpu.VMEM((BLOCK, wk), F32),
                        pltpu.VMEM((BLOCK, wk), F32), pltpu.VMEM((BLOCK, wk), F32)],
        compiler_params=_params("arbitrary"),
    )(sinks, proj, proj, proj, proj, proj, cos_t, sin_t, cos_t, sin_t, gq, gk, dout,
      *(() if after is None else (after,)))


_GELU_K = math.sqrt(2.0 / math.pi)
_GELU_A = 0.044715


def _gelu(x):
    return 0.5 * x * (1.0 + jnp.tanh(_GELU_K * (x + _GELU_A * x * x * x)))


def _gelu_grad(x):
    th = jnp.tanh(_GELU_K * (x + _GELU_A * x * x * x))
    return 0.5 * (1.0 + th) + 0.5 * x * (1.0 - th * th) * (_GELU_K * (1.0 + 3.0 * _GELU_A * x * x))


def _group_ln(v):
    mu = jnp.mean(v, axis=1, keepdims=True)
    cen = v - mu
    rstd = lax.rsqrt(jnp.mean(cen * cen, axis=1, keepdims=True) + EPS)
    return cen * rstd, rstd


def _sgu_geometry(off_u, ws):
    cw = math.gcd(off_u, ws)
    return cw, ws // cw, off_u // cw, (off_u + ws) // cw


def _sgu_fwd(proj, ln_g, ln_b, w_s, bt, off_u, ws, name):
    t = proj.shape[0]
    nb = t // BLOCK
    cw, nc, ub, vb = _sgu_geometry(off_u, ws)
    gpc = cw // LANES
    ng = ws // LANES

    def body(u_ref, v_ref, g_ref, b_ref, w_ref, bt_ref, o_ref):
        jc = pl.program_id(0)
        row = lax.broadcasted_iota(jnp.int32, (BLOCK, BLOCK), 0)
        col = lax.broadcasted_iota(jnp.int32, (BLOCK, BLOCK), 1)
        lane_g = _lane((BLOCK, ng))
        for gi in range(gpc):
            sl = slice(LANES * gi, LANES * gi + LANES)
            xh, _ = _group_ln(_gelu(v_ref[:, sl]))
            vn = xh * g_ref[:, sl] + b_ref[:, sl]
            w = jnp.where(row >= col, w_ref[gi], 0.0).astype(BF16)
            bias = jnp.sum(jnp.where(lane_g == jc * gpc + gi, bt_ref[...], 0.0), axis=1, keepdims=True)
            s = _dot(w, vn.astype(BF16), NN) + bias
            o_ref[:, sl] = (_gelu(u_ref[:, sl]) * s).astype(o_ref.dtype)

    return pl.pallas_call(
        body,
        name=name,
        grid=(nc, nb),
        in_specs=[pl.BlockSpec((BLOCK, cw), lambda jc, i: (i, ub + jc)),
                  pl.BlockSpec((BLOCK, cw), lambda jc, i: (i, vb + jc)),
                  pl.BlockSpec((1, cw), lambda jc, i: (0, jc)),
                  pl.BlockSpec((1, cw), lambda jc, i: (0, jc)),
                  pl.BlockSpec((gpc, BLOCK, BLOCK), lambda jc, i: (jc, 0, 0)),
                  pl.BlockSpec((BLOCK, ng), lambda jc, i: (0, 0))],
        out_specs=pl.BlockSpec((BLOCK, cw), lambda jc, i: (i, jc)),
        out_shape=jax.ShapeDtypeStruct((t, ws), BF16),
        compiler_params=_params("parallel", "parallel"),
    )(proj, proj, ln_g, ln_b, w_s, bt)


def _sgu_bwd(proj, dout, ln_g, ln_b, w_s, bt, off_u, ws, name):
    t = proj.shape[0]
    nb = t // BLOCK
    cw, nc, ub, vb = _sgu_geometry(off_u, ws)
    gpc = cw // LANES
    ng = ws // LANES

    def body(u_ref, v_ref, g_ref, b_ref, w_ref, bt_ref, do_ref, du_ref, dv_ref, dg_ref, db_ref, dw_ref, dbs_ref,
             bacc_ref):
        jc = pl.program_id(0)
        i = pl.program_id(1)
        row = lax.broadcasted_iota(jnp.int32, (BLOCK, BLOCK), 0)
        col = lax.broadcasted_iota(jnp.int32, (BLOCK, BLOCK), 1)
        lane_g = _lane((BLOCK, ng))
        tri = row >= col

        @pl.when(i == 0)
        def _():
            dg_ref[...] = jnp.zeros_like(dg_ref)
            db_ref[...] = jnp.zeros_like(db_ref)
            dw_ref[...] = jnp.zeros_like(dw_ref)
            bacc_ref[...] = jnp.zeros_like(bacc_ref)

        for gi in range(gpc):
            sl = slice(LANES * gi, LANES * gi + LANES)
            u_raw, v_raw = u_ref[:, sl], v_ref[:, sl]
            xh, rstd = _group_ln(_gelu(v_raw))
            gam = g_ref[:, sl]
            vn = (xh * gam + b_ref[:, sl]).astype(BF16)
            w = jnp.where(tri, w_ref[gi], 0.0)
            bias = jnp.sum(jnp.where(lane_g == jc * gpc + gi, bt_ref[...], 0.0), axis=1, keepdims=True)
            s = _dot(w.astype(BF16), vn, NN) + bias
            dov = do_ref[:, sl]
            du_ref[:, sl] = (dov * s * _gelu_grad(u_raw)).astype(du_ref.dtype)
            ds = dov * _gelu(u_raw)
            ds16 = ds.astype(BF16)
            dw_ref[gi] += jnp.where(tri, _dot(ds16, vn, NT), 0.0)
            bacc_ref[gi] += ds
            dvn = _dot(w.T.astype(BF16), ds16, NN)
            dg_ref[:, sl] += jnp.broadcast_to(jnp.sum(dvn * xh, axis=0, keepdims=True), (8, LANES))
            db_ref[:, sl] += jnp.broadcast_to(jnp.sum(dvn, axis=0, keepdims=True), (8, LANES))
            dxh = dvn * gam
            dvg = rstd * (dxh - jnp.mean(dxh, axis=1, keepdims=True)
                          - xh * jnp.mean(dxh * xh, axis=1, keepdims=True))
            dv_ref[:, sl] = (dvg * _gelu_grad(v_raw)).astype(dv_ref.dtype)

        @pl.when(i == nb - 1)
        def _():
            for gi in range(gpc):
                dbs_ref[gi] = jnp.broadcast_to(jnp.sum(bacc_ref[gi].T, axis=0, keepdims=True), (8, LANES))

    blk = lambda base: pl.BlockSpec((BLOCK, cw), lambda jc, i: (i, base + jc))
    vec = pl.BlockSpec((1, cw), lambda jc, i: (0, jc))
    acc = pl.BlockSpec((8, cw), lambda jc, i: (0, jc))
    wsp = pl.BlockSpec((gpc, BLOCK, BLOCK), lambda jc, i: (jc, 0, 0))
    return pl.pallas_call(
        body,
        name=name,
        grid=(nc, nb),
        in_specs=[blk(ub), blk(vb), vec, vec, wsp, pl.BlockSpec((BLOCK, ng), lambda jc, i: (0, 0)), blk(0)],
        out_specs=[blk(0), blk(0), acc, acc, wsp, pl.BlockSpec((gpc, 8, LANES), lambda jc, i: (jc, 0, 0))],
        out_shape=[jax.ShapeDtypeStruct((t, ws), BF16), jax.ShapeDtypeStruct((t, ws), BF16),
                   jax.ShapeDtypeStruct((8, ws), F32), jax.ShapeDtypeStruct((8, ws), F32),
                   jax.ShapeDtypeStruct((ng, BLOCK, BLOCK), F32), jax.ShapeDtypeStruct((ng, 8, LANES), F32)],
        scratch_shapes=[pltpu.VMEM((gpc, BLOCK, BLOCK), F32)],
        compiler_params=_params("arbitrary", "arbitrary"),
    )(proj, proj, ln_g, ln_b, w_s, bt, dout)


def _sigmoid(x):
    return 1.0 / (1.0 + jnp.exp(-x))


def _merge_geometry(off_g, d):
    cw = math.gcd(off_g, d)
    return cw, d // cw, off_g // cw, (off_g + d) // cw


def _branches_fwd(attn, sgu, wab_t, wsb_t, proj, off_g, name):
    t = attn.shape[0]
    d = wab_t.shape[0]
    tn, _, ab, bb = _merge_geometry(off_g, d)
    tm = _divisor_tile(t, 1024, 128)

    def body(a1_ref, a2_ref, b1_ref, b2_ref, la_ref, lb_ref, bra_ref, brb_ref, o_ref):
        va = _dot(a1_ref[...], b1_ref[...], NT)
        vb = _dot(a2_ref[...], b2_ref[...], NT)
        bra_ref[...] = va
        brb_ref[...] = vb
        o_ref[...] = (_sigmoid(la_ref[...]) * va + _sigmoid(lb_ref[...]) * vb).astype(o_ref.dtype)

    rows = lambda w: pl.BlockSpec((tm, w), lambda i, j: (i, 0))
    wrow = lambda w: pl.BlockSpec((tn, w), lambda i, j: (j, 0))
    blk = lambda base: pl.BlockSpec((tm, tn), lambda i, j: (i, base + j))
    return pl.pallas_call(
        body,
        name=name,
        grid=(t // tm, d // tn),
        in_specs=[rows(attn.shape[1]), rows(sgu.shape[1]), wrow(wab_t.shape[1]), wrow(wsb_t.shape[1]), blk(ab),
                  blk(bb)],
        out_specs=[blk(0)] * 3,
        out_shape=[jax.ShapeDtypeStruct((t, d), F32), jax.ShapeDtypeStruct((t, d), F32),
                   jax.ShapeDtypeStruct((t, d), BF16)],
        compiler_params=_params("parallel", "parallel"),
    )(attn, sgu, wab_t, wsb_t, proj, proj)


def _branches_bwd(dx16, wo, br_a, br_b, proj, off_g, name, after=None):
    t, d = br_a.shape
    tn, _, ab, bb = _merge_geometry(off_g, d)
    tm = _divisor_tile(t, 1024, 128)
    k = dx16.shape[1]

    def body(a_ref, b_ref, bra_ref, brb_ref, la_ref, lb_ref, *rest):
        dab_ref, dl_ref = rest[-2:]
        dmv = _dot(a_ref[...], b_ref[...], NT)
        ga, gb = _sigmoid(la_ref[...]), _sigmoid(lb_ref[...])
        dab_ref[0] = (dmv * ga).astype(dab_ref.dtype)
        dab_ref[1] = (dmv * gb).astype(dab_ref.dtype)
        dl_ref[0] = (dmv * bra_ref[...] * ga * (1.0 - ga)).astype(dl_ref.dtype)
        dl_ref[1] = (dmv * brb_ref[...] * gb * (1.0 - gb)).astype(dl_ref.dtype)

    blk = lambda base: pl.BlockSpec((tm, tn), lambda i, j: (i, base + j))
    pair = pl.BlockSpec((2, tm, tn), lambda i, j: (0, i, j))
    return pl.pallas_call(
        body,
        name=name,
        grid=(t // tm, d // tn),
        in_specs=[pl.BlockSpec((tm, k), lambda i, j: (i, 0)), pl.BlockSpec((tn, k), lambda i, j: (j, 0)), blk(0),
                  blk(0), blk(ab), blk(bb)] + ([ANY] if after is not None else []),
        out_specs=[pair, pair],
        out_shape=[jax.ShapeDtypeStruct((2, t, d), BF16)] * 2,
        compiler_params=_params("parallel", "parallel"),
    )(dx16, wo, br_a, br_b, proj, proj, *(() if after is None else (after,)))


def _gate_up_fwd(h2, wgu_t, name, after=None):
    t, d = h2.shape
    f = wgu_t.shape[0] // 2
    tm = _divisor_tile(t, 1024, 128)
    tn = _divisor_tile(f, 512, 128)
    nb = f // tn

    def body(a_ref, bg_ref, bu_ref, *rest):
        gu_ref, act_ref = rest[-2:]
        av = a_ref[...]
        gv = _dot(av, bg_ref[...], NT)
        uv = _dot(av, bu_ref[...], NT)
        gu_ref[0] = gv
        gu_ref[1] = uv
        act_ref[...] = (gv * _sigmoid(gv) * uv).astype(act_ref.dtype)

    return pl.pallas_call(
        body,
        name=name,
        grid=(t // tm, nb),
        in_specs=[pl.BlockSpec((tm, d), lambda i, j: (i, 0)), pl.BlockSpec((tn, d), lambda i, j: (j, 0)),
                  pl.BlockSpec((tn, d), lambda i, j: (j + nb, 0))] + ([ANY] if after is not None else []),
        out_specs=[pl.BlockSpec((2, tm, tn), lambda i, j: (0, i, j)), pl.BlockSpec((tm, tn), lambda i, j: (i, j))],
        out_shape=[jax.ShapeDtypeStruct((2, t, f), F32), jax.ShapeDtypeStruct((t, f), BF16)],
        compiler_params=_params("parallel", "parallel"),
    )(h2, wgu_t, wgu_t, *(() if after is None else (after,)))


def _gate_up_bwd(dx16, wd, gu, name, after=None):
    t, d = dx16.shape
    f = wd.shape[0]
    tm = _divisor_tile(t, 1024, 128)
    tn = _divisor_tile(f, 512, 128)

    def body(a_ref, b_ref, gu_ref, *rest):
        o_ref = rest[-1]
        dav = _dot(a_ref[...], b_ref[...], NT)
        gv = gu_ref[0]
        sg = _sigmoid(gv)
        o_ref[0] = (dav * gu_ref[1] * (sg + gv * sg * (1.0 - sg))).astype(o_ref.dtype)
        o_ref[1] = (dav * gv * sg).astype(o_ref.dtype)

    pair = pl.BlockSpec((2, tm, tn), lambda i, j: (0, i, j))
    return pl.pallas_call(
        body,
        name=name,
        grid=(t // tm, f // tn),
        in_specs=[pl.BlockSpec((tm, d), lambda i, j: (i, 0)), pl.BlockSpec((tn, d), lambda i, j: (j, 0)), pair]
        + ([ANY] if after is not None else []),
        out_specs=pair,
        out_shape=jax.ShapeDtypeStruct((2, t, f), BF16),
        compiler_params=_params("parallel", "parallel"),
    )(dx16, wd, gu, *(() if after is None else (after,)))


def _loss_and_grad(y, target, name):
    t, d = y.shape
    tr = _divisor_tile(t, 256, 8)

    def body(y_ref, t_ref, l_ref, dy_ref, dy16_ref):
        i = pl.program_id(0)
        err = y_ref[...] - t_ref[...]
        dy_ref[...] = err * (1.0 / d)
        dy16_ref[...] = (err * (1.0 / d)).astype(dy16_ref.dtype)
        part = jnp.broadcast_to(0.5 * jnp.sum(err * err) * (1.0 / d), l_ref.shape)

        @pl.when(i == 0)
        def _():
            l_ref[...] = part

        @pl.when(i > 0)
        def _():
            l_ref[...] += part

    row = pl.BlockSpec((tr, d), lambda i: (i, 0))
    return pl.pallas_call(
        body,
        name=name,
        grid=(t // tr,),
        in_specs=[row, row],
        out_specs=[pl.BlockSpec((8, LANES), lambda i: (0, 0)), row, row],
        out_shape=[jax.ShapeDtypeStruct((8, LANES), F32), jax.ShapeDtypeStruct((t, d), F32),
                   jax.ShapeDtypeStruct((t, d), BF16)],
        compiler_params=_params("arbitrary"),
    )(y, target)


def _adam_math(w, g, m, v):
    m = ADAM_B1 * m + (1.0 - ADAM_B1) * g
    v = ADAM_B2 * v + (1.0 - ADAM_B2) * (g * g)
    m_hat = m / (1.0 - ADAM_B1 ** ADAM_STEP)
    v_hat = v / (1.0 - ADAM_B2 ** ADAM_STEP)
    delta = -ADAM_LR * (m_hat / (jnp.sqrt(v_hat) + ADAM_EPS) + ADAM_WD * w)
    return delta, m, v


def _row_tile(r, c, elems=512 * 1024):
    return _divisor_tile(r, max(8, elems // c // 8 * 8), 8)


def _adam(w, grads, m, v, chip, name, after=None):
    nl, r, c = w.shape
    tr = _row_tile(r, c, 256 * 1024)
    nb = r // tr
    counts = [len(terms) for terms, _ in grads]

    def body(chip_ref, *refs):
        w_ref, m_ref, v_ref = refs[:3]
        g_ref, d_ref, nm_ref, nv_ref = refs[-4:]
        layer = pl.program_id(0)
        g, at = None, 3
        for li, n in enumerate(counts):
            total = refs[at][...].astype(F32)
            for ref in refs[at + 1:at + n]:
                total = total + ref[...].astype(F32)
            g = total if g is None else jnp.where(layer == li, total, g)
            at += n
        g_ref[...] = g
        d_ref[...], nm_ref[...], nv_ref[...] = _adam_math(w_ref[...], g, m_ref[...], v_ref[...])

    def term_spec(li, p, by_owner):
        def index(l, i, chip_ref):
            rows = jnp.where(l < li, 0, jnp.where(l > li, nb - 1, i))
            return (p, chip_ref[0] if by_owner else 0, rows, 0)
        return pl.BlockSpec((None, None, tr, c), index)

    row = pl.BlockSpec((None, tr, c), lambda l, i, chip_ref: (l, i, 0))
    specs, arrays = [], []
    for li, (terms, p) in enumerate(grads):
        for term in terms:
            specs.append(term_spec(li, p, term.shape[1] == 4))
            arrays.append(term)
    return pl.pallas_call(
        body,
        name=name,
        grid_spec=pltpu.PrefetchScalarGridSpec(
            num_scalar_prefetch=1, grid=(nl, nb),
            in_specs=[row] * 3 + specs + ([ANY] if after is not None else []), out_specs=[row] * 4),
        out_shape=[jax.ShapeDtypeStruct((nl, r, c), F32)] * 4,
        compiler_params=_params("arbitrary", "arbitrary"),
    )(chip, w, m, v, *arrays, *(() if after is None else (after,)))


def _place_shard(parts, layer, dev, out_dtype, name, after=None):
    p = len(parts)
    _, r, c = parts[0].shape
    tr = _row_tile(r, c)

    def body(dev_ref, *refs):
        o_ref = refs[-1]
        x = refs[0][...]
        for pi in range(1, p):
            x = jnp.where(pl.program_id(0) == pi, refs[pi][...], x)
        o_ref[...] = x.astype(o_ref.dtype)

    return pl.pallas_call(
        body,
        name=name,
        grid_spec=pltpu.PrefetchScalarGridSpec(
            num_scalar_prefetch=1,
            grid=(p, r // tr),
            in_specs=[pl.BlockSpec((None, tr, c), lambda pi, i, dev_ref: (layer, i, 0))] * p
            + ([ANY] if after is not None else []),
            out_specs=pl.BlockSpec((None, None, tr, c), lambda pi, i, dev_ref: (pi, dev_ref[0], i, 0)),
        ),
        out_shape=jax.ShapeDtypeStruct((p, N_DEV, r, c), out_dtype),
        compiler_params=_params("parallel", "parallel"),
    )(dev, *parts, *(() if after is None else (after,)))


def _sum_sibling(g, land, core, name):
    p, _, _, r, c = g.shape
    tr = _row_tile(r, c, 1024 * 1024)

    def body(core_ref, g_ref, l_ref, o_ref):
        o_ref[...] = (g_ref[...].astype(F32) + l_ref[...].astype(F32)).astype(o_ref.dtype)

    return pl.pallas_call(
        body,
        name=name,
        grid_spec=pltpu.PrefetchScalarGridSpec(
            num_scalar_prefetch=1,
            grid=(p, 4, r // tr),
            in_specs=[pl.BlockSpec((None, None, None, tr, c), lambda pi, q, i, core_ref: (pi, q, core_ref[0], i, 0)),
                      pl.BlockSpec((None, None, None, tr, c), lambda pi, q, i, core_ref: (pi, q, 0, i, 0))],
            out_specs=pl.BlockSpec((None, None, tr, c), lambda pi, q, i, core_ref: (pi, q, i, 0)),
        ),
        out_shape=jax.ShapeDtypeStruct((p, 4, r, c), BF16),
        compiler_params=_params("parallel", "parallel", "parallel"),
    )(core, g, land)


def _sum_chips(s, lands, chip, name):
    p, _, r, c = s.shape
    tr = _row_tile(r, c)

    def body(chip_ref, s_ref, l0_ref, l1_ref, l2_ref, o_ref):
        total = s_ref[...].astype(F32) + l0_ref[...].astype(F32)
        o_ref[...] = total + l1_ref[...].astype(F32) + l2_ref[...].astype(F32)

    land_spec = pl.BlockSpec((None, None, tr, c), lambda pi, i, chip_ref: (pi, 0, i, 0))
    return pl.pallas_call(
        body,
        name=name,
        grid_spec=pltpu.PrefetchScalarGridSpec(
            num_scalar_prefetch=1,
            grid=(p, r // tr),
            in_specs=[pl.BlockSpec((None, None, tr, c), lambda pi, i, chip_ref: (pi, chip_ref[0], i, 0)),
                      land_spec, land_spec, land_spec],
            out_specs=pl.BlockSpec((None, tr, c), lambda pi, i, chip_ref: (pi, i, 0)),
        ),
        out_shape=jax.ShapeDtypeStruct((p, r, c), F32),
        compiler_params=_params("parallel", "parallel"),
    )(chip, s, *lands)


def _small_reduce_adam(gathered, w, m, v, name):
    _, r, c = gathered.shape
    tr = _row_tile(r, c)

    def body(p_ref, w_ref, m_ref, v_ref, g_ref, d_ref, nm_ref, nv_ref):
        g = p_ref[0]
        for j in range(1, N_DEV):
            g = g + p_ref[j]
        g_ref[...] = g
        d_ref[...], nm_ref[...], nv_ref[...] = _adam_math(w_ref[...], g, m_ref[...], v_ref[...])

    row = pl.BlockSpec((tr, c), lambda i: (i, 0))
    return pl.pallas_call(
        body,
        name=name,
        grid=(r // tr,),
        in_specs=[pl.BlockSpec((N_DEV, tr, c), lambda i: (0, i, 0)), row, row, row],
        out_specs=[row] * 4,
        out_shape=[jax.ShapeDtypeStruct((r, c), F32)] * 4,
        compiler_params=_params("parallel"),
    )(gathered, w, m, v)


def _place():
    return lax.axis_index("x"), lax.axis_index("y"), lax.axis_index("c")


def _all_gather(bufs, name):
    n = len(bufs)

    def body(*refs):
        outs = refs[n:2 * n]
        send_sems, recv_sems = refs[2 * n:]
        x, y, c = _place()
        me, sibling = (x, y, c), (x, y, 1 - c)
        chips = [(1 - x, y), (x, 1 - y), (1 - x, 1 - y)]

        def block(a, px, py, pc):
            return outs[a].at[:, pl.ds(4 * px + 2 * py + pc, 1)]

        def copy(a, k, blk, to):
            return pltpu.make_async_remote_copy(
                src_ref=block(a, *blk), dst_ref=block(a, *blk), send_sem=send_sems.at[a, k],
                recv_sem=recv_sems.at[a, k], device_id=to, device_id_type=MESH)

        first = []
        for a in range(n):
            first.append(copy(a, 0, me, sibling))
            first += [copy(a, 1 + j, me, (*chip, c)) for j, chip in enumerate(chips)]
        for cp in first:
            cp.start()
        passed = []
        for j, chip in enumerate(chips):
            for a in range(n):
                copy(a, 1 + j, (*chip, c), me).wait_recv()
                fwd = copy(a, 4 + j, (*chip, c), sibling)
                fwd.start()
                passed.append(fwd)
        for a in range(n):
            copy(a, 0, sibling, me).wait_recv()
            for j, chip in enumerate(chips):
                copy(a, 4 + j, (*chip, 1 - c), me).wait_recv()
        for cp in first + passed:
            cp.wait_send()

    return pl.pallas_call(
        body,
        name=name,
        in_specs=[ANY] * n,
        out_specs=[ANY] * n,
        out_shape=[jax.ShapeDtypeStruct(b.shape, b.dtype) for b in bufs],
        input_output_aliases={a: a for a in range(n)},
        scratch_shapes=[pltpu.SemaphoreType.DMA((n, 7)), pltpu.SemaphoreType.DMA((n, 7))],
    )(*bufs)


HBM = pl.BlockSpec(memory_space=pltpu.HBM)
SEM = pl.BlockSpec(memory_space=pltpu.SEMAPHORE)
TOKEN = pl.BlockSpec(memory_space=pltpu.VMEM)
EFFECT = pltpu.SideEffectType.DATAFLOW_SIDE_EFFECTING


def _in_hbm(a):
    return pltpu.with_memory_space_constraint(a, pltpu.HBM)


def _gather_start(chunks, name):
    flat = [b for chunk in chunks for b in chunk]
    n, nch = len(flat), len(chunks)

    def body(*refs):
        sems, outs, token = refs[n:n + 2 * nch], refs[n + 2 * nch:2 * n + 2 * nch], refs[2 * n + 2 * nch]
        x, y, c = _place()
        targets = [(x, y, 1 - c), (1 - x, y, c), (x, 1 - y, c), (1 - x, 1 - y, c)]
        a = 0
        for ci, chunk in enumerate(chunks):
            for k in range(len(chunk)):
                mine = outs[a].at[:, pl.ds(4 * x + 2 * y + c, 1)]
                for ti, to in enumerate(targets):
                    pltpu.make_async_remote_copy(
                        src_ref=mine, dst_ref=mine, send_sem=sems[2 * ci].at[4 * k + ti],
                        recv_sem=sems[2 * ci + 1].at[4 * k + ti], device_id=to, device_id_type=MESH).start()
                a += 1
        token[...] = jnp.zeros_like(token)

    sem_shapes = []
    for chunk in chunks:
        sem_shapes += [pltpu.SemaphoreType.DMA((4 * len(chunk),))] * 2
    outs = pl.pallas_call(
        body,
        name=name,
        in_specs=[HBM] * n,
        out_specs=[SEM] * (2 * nch) + [HBM] * n + [TOKEN],
        out_shape=sem_shapes + [pltpu.HBM(b.shape, b.dtype) for b in flat] + [jax.ShapeDtypeStruct((8, LANES), F32)],
        input_output_aliases={i: 2 * nch + i for i in range(n)},
        compiler_params=pltpu.CompilerParams(has_side_effects=EFFECT),
    )(*[_in_hbm(b) for b in flat])
    result, a = [], 2 * nch
    for ci, chunk in enumerate(chunks):
        result.append((outs[2 * ci], outs[2 * ci + 1], list(outs[a:a + len(chunk)])))
        a += len(chunk)
    return result, outs[-1]


def _gather_wait(send_sems, recv_sems, bufs, after, name):
    n = len(bufs)

    def body(*refs):
        ins, ssem, rsem = refs[:n], refs[n], refs[n + 1]
        x, y, c = _place()
        sources = [(x, y, 1 - c), (1 - x, y, c), (x, 1 - y, c), (1 - x, 1 - y, c)]
        for k in range(n):
            for ti, (px, py, pc) in enumerate(sources):
                theirs = ins[k].at[:, pl.ds(4 * px + 2 * py + pc, 1)]
                cp = pltpu.make_async_remote_copy(
                    src_ref=theirs, dst_ref=theirs, send_sem=ssem.at[4 * k + ti], recv_sem=rsem.at[4 * k + ti],
                    device_id=(px, py, pc), device_id_type=MESH)
                cp.wait_send()
                cp.wait_recv()

    return pl.pallas_call(
        body,
        name=name,
        in_specs=[HBM] * n + [SEM, SEM, ANY],
        out_specs=[HBM] * n,
        out_shape=[pltpu.HBM(b.shape, b.dtype) for b in bufs],
        input_output_aliases={i: i for i in range(n)},
        compiler_params=pltpu.CompilerParams(has_side_effects=EFFECT),
    )(*bufs, send_sems, recv_sems, after)


def _forward_start(bufs, name):
    n = len(bufs)

    def body(*refs):
        ssem, rsem = refs[n], refs[n + 1]
        outs, token = refs[n + 2:2 * n + 2], refs[2 * n + 2]
        x, y, c = _place()
        chips = [(1 - x, y), (x, 1 - y), (1 - x, 1 - y)]
        for a in range(n):
            for j, (px, py) in enumerate(chips):
                got = outs[a].at[:, pl.ds(4 * px + 2 * py + c, 1)]
                pltpu.make_async_remote_copy(
                    src_ref=got, dst_ref=got, send_sem=ssem.at[3 * a + j], recv_sem=rsem.at[3 * a + j],
                    device_id=(x, y, 1 - c), device_id_type=MESH).start()
        token[...] = jnp.zeros_like(token)

    outs = pl.pallas_call(
        body,
        name=name,
        in_specs=[HBM] * n,
        out_specs=[SEM, SEM] + [HBM] * n + [TOKEN],
        out_shape=[pltpu.SemaphoreType.DMA((3 * n,))] * 2 + [pltpu.HBM(b.shape, b.dtype) for b in bufs]
        + [jax.ShapeDtypeStruct((8, LANES), F32)],
        input_output_aliases={i: 2 + i for i in range(n)},
        compiler_params=pltpu.CompilerParams(has_side_effects=EFFECT),
    )(*[_in_hbm(b) for b in bufs])
    return outs[0], outs[1], list(outs[2:2 + n]), outs[-1]


def _forward_wait(send_sems, recv_sems, bufs, after, name):
    n = len(bufs)

    def body(*refs):
        ins, ssem, rsem = refs[:n], refs[n], refs[n + 1]
        x, y, c = _place()
        chips = [(1 - x, y), (x, 1 - y), (1 - x, 1 - y)]
        for a in range(n):
            for j, (px, py) in enumerate(chips):
                coming = ins[a].at[:, pl.ds(4 * px + 2 * py + 1 - c, 1)]
                cp = pltpu.make_async_remote_copy(
                    src_ref=coming, dst_ref=coming, send_sem=ssem.at[3 * a + j], recv_sem=rsem.at[3 * a + j],
                    device_id=(x, y, 1 - c), device_id_type=MESH)
                cp.wait_send()
                cp.wait_recv()

    return pl.pallas_call(
        body,
        name=name,
        in_specs=[HBM] * n + [SEM, SEM, ANY],
        out_specs=[HBM] * n,
        out_shape=[pltpu.HBM(b.shape, b.dtype) for b in bufs],
        input_output_aliases={i: i for i in range(n)},
        compiler_params=pltpu.CompilerParams(has_side_effects=EFFECT),
    )(*bufs, send_sems, recv_sems, after)


def _chips_start(sums, name):
    n = len(sums)

    def body(*refs):
        ssem, rsem = refs[4 * n], refs[4 * n + 1]
        src, land = refs[4 * n + 2:5 * n + 2], refs[5 * n + 2:8 * n + 2]
        token = refs[8 * n + 2]
        x, y, c = _place()
        chips = [(1 - x, y), (x, 1 - y), (1 - x, 1 - y)]
        for a in range(n):
            for k, (px, py) in enumerate(chips):
                pltpu.make_async_remote_copy(
                    src_ref=src[a].at[:, pl.ds(2 * px + py, 1)], dst_ref=land[3 * a + k], send_sem=ssem.at[3 * a + k],
                    recv_sem=rsem.at[3 * a + k], device_id=(px, py, c), device_id_type=MESH).start()
        token[...] = jnp.zeros_like(token)

    lands = []
    for s in sums:
        lands += [lax.empty((s.shape[0], 1) + s.shape[2:], s.dtype) for _ in range(3)]
    outs = pl.pallas_call(
        body,
        name=name,
        in_specs=[HBM] * (4 * n),
        out_specs=[SEM, SEM] + [HBM] * (4 * n) + [TOKEN],
        out_shape=[pltpu.SemaphoreType.DMA((3 * n,))] * 2 + [pltpu.HBM(b.shape, b.dtype) for b in list(sums) + lands]
        + [jax.ShapeDtypeStruct((8, LANES), F32)],
        input_output_aliases={i: 2 + i for i in range(4 * n)},
        compiler_params=pltpu.CompilerParams(has_side_effects=EFFECT),
    )(*[_in_hbm(b) for b in list(sums) + lands])
    return outs[0], outs[1], list(outs[2:2 + n]), list(outs[2 + n:2 + 4 * n]), outs[-1]


def _chips_wait(send_sems, recv_sems, sums, lands, after, name):
    n = len(sums)

    def body(*refs):
        src, land = refs[:n], refs[n:4 * n]
        ssem, rsem = refs[4 * n], refs[4 * n + 1]
        x, y, c = _place()
        chips = [(1 - x, y), (x, 1 - y), (1 - x, 1 - y)]
        for a in range(n):
            for k, (px, py) in enumerate(chips):
                cp = pltpu.make_async_remote_copy(
                    src_ref=src[a].at[:, pl.ds(2 * px + py, 1)], dst_ref=land[3 * a + k], send_sem=ssem.at[3 * a + k],
                    recv_sem=rsem.at[3 * a + k], device_id=(px, py, c), device_id_type=MESH)
                cp.wait_send()
                cp.wait_recv()

    both = list(sums) + list(lands)
    outs = pl.pallas_call(
        body,
        name=name,
        in_specs=[HBM] * (4 * n) + [SEM, SEM, ANY],
        out_specs=[HBM] * (4 * n),
        out_shape=[pltpu.HBM(b.shape, b.dtype) for b in both],
        input_output_aliases={i: i for i in range(4 * n)},
        compiler_params=pltpu.CompilerParams(has_side_effects=EFFECT),
    )(*both, send_sems, recv_sems, after)
    return list(outs[:n]), [list(outs[n + 3 * a:n + 3 * a + 3]) for a in range(n)]


def _sibling_start(grads, name):
    n = len(grads)

    def body(*refs):
        ssem, rsem = refs[2 * n], refs[2 * n + 1]
        src, land = refs[2 * n + 2:3 * n + 2], refs[3 * n + 2:4 * n + 2]
        token = refs[4 * n + 2]
        x, y, c = _place()
        for a in range(n):
            pltpu.make_async_remote_copy(
                src_ref=src[a].at[:, :, pl.ds(1 - c, 1)], dst_ref=land[a], send_sem=ssem.at[a], recv_sem=rsem.at[a],
                device_id=(x, y, 1 - c), device_id_type=MESH).start()
        token[...] = jnp.zeros_like(token)

    lands = [lax.empty(g.shape[:2] + (1,) + g.shape[3:], g.dtype) for g in grads]
    both = list(grads) + lands
    outs = pl.pallas_call(
        body,
        name=name,
        in_specs=[HBM] * (2 * n),
        out_specs=[SEM, SEM] + [HBM] * (2 * n) + [TOKEN],
        out_shape=[pltpu.SemaphoreType.DMA((n,))] * 2 + [pltpu.HBM(b.shape, b.dtype) for b in both]
        + [jax.ShapeDtypeStruct((8, LANES), F32)],
        input_output_aliases={i: 2 + i for i in range(2 * n)},
        compiler_params=pltpu.CompilerParams(has_side_effects=EFFECT),
    )(*[_in_hbm(b) for b in both])
    return outs[0], outs[1], list(outs[2:2 + n]), list(outs[2 + n:2 + 2 * n]), outs[-1]


def _sibling_wait(send_sems, recv_sems, grads, lands, after, name):
    n = len(grads)

    def body(*refs):
        src, land = refs[:n], refs[n:2 * n]
        ssem, rsem = refs[2 * n], refs[2 * n + 1]
        x, y, c = _place()
        for a in range(n):
            cp = pltpu.make_async_remote_copy(
                src_ref=src[a].at[:, :, pl.ds(1 - c, 1)], dst_ref=land[a], send_sem=ssem.at[a], recv_sem=rsem.at[a],
                device_id=(x, y, 1 - c), device_id_type=MESH)
            cp.wait_send()
            cp.wait_recv()

    both = list(grads) + list(lands)
    outs = pl.pallas_call(
        body,
        name=name,
        in_specs=[HBM] * (2 * n) + [SEM, SEM, ANY],
        out_specs=[HBM] * (2 * n),
        out_shape=[pltpu.HBM(b.shape, b.dtype) for b in both],
        input_output_aliases={i: i for i in range(2 * n)},
        compiler_params=pltpu.CompilerParams(has_side_effects=EFFECT),
    )(*both, send_sems, recv_sems, after)
    return list(outs[:n]), list(outs[n:])


_SMALL = ("mix_norm", "q_norm", "k_norm", "sinks", "sgu_ln_g", "sgu_ln_b", "w_spatial", "b_spatial", "ffn_norm")


def _pack_rows(a):
    flat = a.reshape(-1)
    pad = (-flat.shape[0]) % LANES
    if pad:
        flat = jnp.pad(flat, (0, pad))
    return flat.reshape(-1, LANES)


def _pack(values):
    rows = jnp.concatenate([_pack_rows(values[k]) for k in _SMALL], axis=0)
    pad = (-rows.shape[0]) % 8
    if pad:
        rows = jnp.pad(rows, ((0, pad), (0, 0)))
    return rows


def _unpack(rows, like):
    out, at = {}, 0
    for k in _SMALL:
        size = like[k].size
        nrows = -(-size // LANES)
        out[k] = rows[at:at + nrows].reshape(-1)[:size].reshape(like[k].shape)
        at += nrows
    return out


def _rope_tables(t, wq):
    pos = jnp.arange(t, dtype=F32)
    inv_freq = jnp.power(ROPE_THETA, -jnp.arange(0, HEAD_DIM, 2, dtype=F32) / HEAD_DIM)
    ang = pos[:, None] * inv_freq[None, :]
    cos, sin = jnp.cos(ang), jnp.sin(ang)
    reps = wq // HEAD_DIM
    return (jnp.tile(jnp.concatenate([cos, cos], axis=1), (1, reps)),
            jnp.tile(jnp.concatenate([-sin, sin], axis=1), (1, reps)))


def kernel(x, mix_norm, w_in, q_norm, k_norm, sinks, sgu_ln_g, sgu_ln_b, w_spatial, b_spatial, w_attn_branch, w_sgu_branch, w_out, ffn_norm, w_gate, w_up, w_down, loss_target, m_mix_norm, m_w_in, m_q_norm, m_k_norm, m_sinks, m_sgu_ln_g, m_sgu_ln_b, m_w_spatial, m_b_spatial, m_w_attn_branch, m_w_sgu_branch, m_w_out, m_ffn_norm, m_w_gate, m_w_up, m_w_down, v_mix_norm, v_w_in, v_q_norm, v_k_norm, v_sinks, v_sgu_ln_g, v_sgu_ln_b, v_w_spatial, v_b_spatial, v_w_attn_branch, v_w_sgu_branch, v_w_out, v_ffn_norm, v_w_gate, v_w_up, v_w_down):
    names = ("mix_norm", "w_in", "q_norm", "k_norm", "sinks", "sgu_ln_g", "sgu_ln_b", "w_spatial", "b_spatial",
             "w_attn_branch", "w_sgu_branch", "w_out", "ffn_norm", "w_gate", "w_up", "w_down")
    weights = dict(zip(names, (mix_norm, w_in, q_norm, k_norm, sinks, sgu_ln_g, sgu_ln_b, w_spatial, b_spatial,
                               w_attn_branch, w_sgu_branch, w_out, ffn_norm, w_gate, w_up, w_down)))
    mom1 = dict(zip(names, (m_mix_norm, m_w_in, m_q_norm, m_k_norm, m_sinks, m_sgu_ln_g, m_sgu_ln_b, m_w_spatial,
                            m_b_spatial, m_w_attn_branch, m_w_sgu_branch, m_w_out, m_ffn_norm, m_w_gate, m_w_up,
                            m_w_down)))
    mom2 = dict(zip(names, (v_mix_norm, v_w_in, v_q_norm, v_k_norm, v_sinks, v_sgu_ln_g, v_sgu_ln_b, v_w_spatial,
                            v_b_spatial, v_w_attn_branch, v_w_sgu_branch, v_w_out, v_ffn_norm, v_w_gate, v_w_up,
                            v_w_down)))
    depth = w_in.shape[0]
    _, t, d = x.shape
    n_q_heads = sinks.shape[1]
    wq = n_q_heads * HEAD_DIM
    wk = wq // Q_PER_KV
    ws = sgu_ln_g.shape[1]
    ng = ws // LANES
    off_u = wq + 2 * wk
    off_g = off_u + 2 * ws
    tables = _rope_tables(t, wq)
    px, py, pc = _place()
    core = pc.astype(jnp.int32)[None]
    chip = (2 * px + py).astype(jnp.int32)[None]
    dev = (4 * px + 2 * py + pc).astype(jnp.int32)[None]

    layers = range(depth)
    chunks = ((0,), (1, 2, 3), (4,), (5,))
    pending, token = [], None
    sources = [[jnp.swapaxes(w_in, 1, 2)], [jnp.swapaxes(w_attn_branch, 1, 2)], [jnp.swapaxes(w_sgu_branch, 1, 2)],
               [w_out], [jnp.swapaxes(w_gate, 1, 2), jnp.swapaxes(w_up, 1, 2)], [w_down]]
    for l in layers:
        pending.append([])
        for ci, chunk in enumerate(chunks):
            bufs = [_place_shard(sources[a], l, dev, BF16, f"place_shard_{l}_{a}", after=token if a == chunk[0] else None)
                    for a in chunk]
            started, token = _gather_start([bufs], f"gather_start_{l}_{ci}")
            pending[l].append(started[0])

    passing = {}

    def arrive(l, ci, after):
        send_sems, recv_sems, bufs = pending[l][ci]
        bufs = _gather_wait(send_sems, recv_sems, bufs, after, f"gather_wait_{l}_{ci}")
        passing[l, ci] = _forward_start(bufs, f"forward_start_{l}_{ci}")
        return passing[l, ci][3]

    def ready(l, ci, after):
        send_sems, recv_sems, bufs, _ = passing.pop((l, ci))
        bufs = _forward_wait(send_sems, recv_sems, bufs, after, f"forward_wait_{l}_{ci}")
        return [f.reshape(f.shape[0] * f.shape[1] * f.shape[2], f.shape[3]) for f in bufs]

    saved = []
    xl = x[0]
    going = arrive(0, 0, token)
    for l in layers:
        gq = jnp.tile(q_norm[l], n_q_heads)[None]
        gk = jnp.tile(k_norm[l], n_q_heads // Q_PER_KV)[None]
        bt = b_spatial[l].T
        h = _rmsnorm_fwd(xl, mix_norm[l][None], f"mix_norm_fwd_{l}", after=going)
        (win_t,) = ready(l, 0, h)
        proj = _mm(h, win_t, "nt", F32, f"in_proj_{l}")
        going = arrive(l, 1, proj)
        attn = _attn_fwd(proj, tables, gq, gk, sinks[l], wq, wk, f"attn_fwd_{l}", after=going)
        sgu = _sgu_fwd(proj, sgu_ln_g[l][None], sgu_ln_b[l][None], w_spatial[l], bt, off_u, ws, f"sgu_fwd_{l}")
        wab_t, wsb_t, wo = ready(l, 1, sgu)
        br_a, br_b, merged = _branches_fwd(attn, sgu, wab_t, wsb_t, proj, off_g, f"branches_{l}")
        x1 = _mm(merged, wo, "nn", F32, f"out_proj_{l}", residual=xl)
        going = arrive(l, 2, x1)
        h2 = _rmsnorm_fwd(x1, ffn_norm[l][None], f"ffn_norm_fwd_{l}", after=going)
        (wgu_t,) = ready(l, 2, h2)
        going = arrive(l, 3, h2) if l > 0 else None
        gu, act = _gate_up_fwd(h2, wgu_t, f"gate_up_{l}", after=going)
        if l == 0:
            arrive(l, 3, act)
        (wd,) = ready(l, 3, act)
        x2 = _mm(act, wd, "nn", F32, f"down_proj_{l}", residual=x1)
        if l + 1 < depth:
            going = arrive(l + 1, 0, x2)
        saved.append(dict(x0=xl, h=h, proj=proj, attn=attn, sgu=sgu, br_a=br_a, br_b=br_b, merged=merged, x1=x1,
                          h2=h2, gu=gu, act=act, gq=gq, gk=gk, bt=bt, win_t=win_t, wab_t=wab_t, wsb_t=wsb_t, wo=wo,
                          wgu_t=wgu_t, wd=wd))
        xl = x2

    loss_part, dx, dx16 = _loss_and_grad(xl, loss_target[0], "loss")
    loss = lax.psum(loss_part[0, 0], ("x", "y", "c"))

    def sibling_start(grads, tag):
        shaped = []
        for g, p in grads:
            rows, c = g.shape
            shaped.append(g.reshape(p, 4, 2, rows // (8 * p), c))
        send_sems, recv_sems, shaped, lands, tok = _sibling_start(shaped, f"rs_sibling_start_{tag}")
        return (send_sems, recv_sems, shaped, lands, tag), tok

    def chips_start(state, after):
        send_sems, recv_sems, shaped, lands, tag = state
        shaped, lands = _sibling_wait(send_sems, recv_sems, shaped, lands, after, f"rs_sibling_wait_{tag}")
        sums = [_sum_sibling(g, o, core, f"rs_add_sibling_{tag}_{a}") for a, (g, o) in enumerate(zip(shaped, lands))]
        send_sems, recv_sems, sums, lands, tok = _chips_start(sums, f"rs_chips_start_{tag}")
        return (send_sems, recv_sems, sums, lands, tag), tok

    def scatter_finish(state, after):
        send_sems, recv_sems, sums, lands, tag = state
        sums, lands = _chips_wait(send_sems, recv_sems, sums, lands, after, f"rs_chips_wait_{tag}")
        return [[s] + o for s, o in zip(sums, lands)]

    in_flight = [dict() for _ in layers]
    small_grads = [None] * depth
    tok, swap_in = None, None
    for l in reversed(layers):
        s = saved[l]
        dgu = _gate_up_bwd(dx16, s["wd"], s["gu"], f"d_gate_up_{l}", after=tok)
        if swap_in is not None:
            in_flight[l + 1]["in"], tok = chips_start(swap_in, dgu)
        g_wd = _mm(s["act"], dx16, "tn", BF16, f"g_w_down_{l}", after=tok)
        swap, tok_s = sibling_start([(g_wd, 1)], f"{l}_down")
        dh2 = _mm(dgu, s["wgu_t"], "nn", F32, f"d_h2_{l}", after=tok_s)
        in_flight[l]["down"], tok = chips_start(swap, dh2)
        g_wgu_t = _mm(dgu, s["h2"], "tn", BF16, f"g_w_gate_up_{l}", after=tok)
        swap, tok_s = sibling_start([(g_wgu_t, 2)], f"{l}_gate_up")
        dx1, dx1_16, g_ffn = _rmsnorm_bwd(s["x1"], ffn_norm[l][None], dh2, dx, f"ffn_norm_bwd_{l}", after=tok_s)
        d_ab, d_logits = _branches_bwd(dx1_16, s["wo"], s["br_a"], s["br_b"], s["proj"], off_g, f"d_branches_{l}")
        in_flight[l]["gate_up"], tok = chips_start(swap, d_ab)
        g_wo = _mm(s["merged"], dx1_16, "tn", BF16, f"g_w_out_{l}", after=tok)
        d_a, d_b = d_ab[0], d_ab[1]
        dattn = _mm(d_a, s["wab_t"], "nn", F32, f"d_attn_{l}", after=g_wo)
        g_wab_t = _mm(d_a, s["attn"], "tn", BF16, f"g_w_attn_branch_{l}")
        dsgu = _mm(d_b, s["wsb_t"], "nn", F32, f"d_sgu_{l}")
        g_wsb_t = _mm(d_b, s["sgu"], "tn", BF16, f"g_w_sgu_branch_{l}")
        swap, tok_s = sibling_start([(g_wab_t, 1), (g_wsb_t, 1), (g_wo, 1)], f"{l}_mix")
        dq, dk, dv, g_gq, g_gk, g_sinks = _attn_bwd(s["proj"], dattn, tables, s["gq"], s["gk"], sinks[l], wq, wk,
                                                    f"attn_bwd_{l}", after=tok_s)
        du, dvv, g_lng, g_lnb, g_ws, g_bs = _sgu_bwd(s["proj"], dsgu, sgu_ln_g[l][None], sgu_ln_b[l][None],
                                                     w_spatial[l], s["bt"], off_u, ws, f"sgu_bwd_{l}")
        dproj = jnp.concatenate([dq, dk.astype(BF16), dv.astype(BF16), du, dvv, d_logits[0], d_logits[1]], axis=1)
        dh = _mm(dproj, s["win_t"], "nn", F32, f"d_h_{l}")
        in_flight[l]["mix"], tok = chips_start(swap, dh)
        g_win_t = _mm(dproj, s["h"], "tn", BF16, f"g_w_in_{l}", after=tok)
        swap_in, tok = sibling_start([(g_win_t, 1)], f"{l}_in")
        dx, dx16, g_mix = _rmsnorm_bwd(s["x0"], mix_norm[l][None], dh, dx1, f"mix_norm_bwd_{l}", after=tok)
        small_grads[l] = dict(
            mix_norm=g_mix[0], q_norm=g_gq[0].reshape(n_q_heads, HEAD_DIM).sum(0),
            k_norm=g_gk[0].reshape(n_q_heads // Q_PER_KV, HEAD_DIM).sum(0), sinks=g_sinks[0, :n_q_heads],
            sgu_ln_g=g_lng[0], sgu_ln_b=g_lnb[0], w_spatial=g_ws, b_spatial=g_bs[:, 0, :], ffn_norm=g_ffn[0])
    grad_x = dx[None]

    result = {key: {} for key in ("grad", "delta", "m", "v")}
    layer_like = {k: weights[k][0] for k in _SMALL}
    packed_g = jnp.concatenate([_pack(small_grads[l]) for l in layers], axis=0)
    rows_per_layer = packed_g.shape[0] // depth
    small_buf = _place_shard([packed_g[None]], 0, dev, F32, "place_small_grads", after=tok)
    (small_started,), tok = _gather_start([[small_buf]], "gather_start_small")
    in_flight[0]["in"], tok = chips_start(swap_in, tok)

    def update(k, grads, transposed, after):
        view = (lambda a: jnp.swapaxes(a, 1, 2)) if transposed else (lambda a: a)
        outs = _adam(view(weights[k]), grads, view(mom1[k]), view(mom2[k]), chip, f"adam_{k}", after=after)
        for key, val in zip(("grad", "delta", "m", "v"), outs):
            result[key][k] = view(val)
        return outs[3]

    def plain(terms, tag):
        s, lands = terms[0], terms[1:]
        g = _sum_chips(s, lands, chip, f"rs_add_chips_{tag}")
        return [jnp.swapaxes(g, 1, 2)[:, None]]

    down = [scatter_finish(in_flight[l]["down"], tok) for l in reversed(layers)][::-1]
    tok = update("w_down", [(down[l][0], 0) for l in layers], False, None)
    gate_up = [scatter_finish(in_flight[l]["gate_up"], tok) for l in reversed(layers)][::-1]
    tok = update("w_gate", [(gate_up[l][0], 0) for l in layers], True, None)
    tok = update("w_up", [(gate_up[l][0], 1) for l in layers], True, tok)
    mix = [scatter_finish(in_flight[l]["mix"], tok) for l in reversed(layers)][::-1]
    tok = update("w_out", [(mix[l][2], 0) for l in layers], False, None)
    tok = update("w_attn_branch", [(plain(mix[l][0], f"{l}_attn_branch"), 0) for l in layers], False, tok)
    tok = update("w_sgu_branch", [(plain(mix[l][1], f"{l}_sgu_branch"), 0) for l in layers], False, tok)

    send_sems, recv_sems, small_bufs = small_started
    small_bufs = _gather_wait(send_sems, recv_sems, small_bufs, tok, "gather_wait_small")
    send_sems, recv_sems, small_bufs, tok = _forward_start(small_bufs, "forward_start_small")
    packed = [jnp.concatenate([_pack({k: src[k][l] for k in _SMALL}) for l in layers], axis=0)
              for src in (weights, mom1, mom2)]
    (gathered_small,) = _forward_wait(send_sems, recv_sems, small_bufs, packed[0], "forward_wait_small")
    small = _small_reduce_adam(gathered_small[0], *packed, "small_reduce_adam")
    for key, rows in zip(("grad", "delta", "m", "v"), small):
        per_layer = [_unpack(rows[l * rows_per_layer:(l + 1) * rows_per_layer], layer_like) for l in layers]
        for k in _SMALL:
            result[key][k] = jnp.stack([per_layer[l][k] for l in layers])

    last = [scatter_finish(in_flight[l]["in"], small[0]) for l in reversed(layers)][::-1]
    update("w_in", [(last[l][0], 0) for l in layers], True, None)

    return (loss, grad_x, *[result["grad"][k] for k in names], *[result["delta"][k] for k in names],
            *[result["m"][k] for k in names], *[result["v"][k] for k in names])
```

```python
import functools
import math

import jax
import jax.numpy as jnp
from jax import lax
from jax.experimental import pallas as pl
from jax.experimental.pallas import tpu as pltpu

F32 = jnp.float32
BF16 = jnp.bfloat16
MESH = pl.DeviceIdType.MESH
ANY = pl.BlockSpec(memory_space=pl.ANY)

N_DEV = 8
HEAD_DIM = 64
Q_PER_KV = 4
BLOCK = 128
LANES = 128
ROPE_THETA = 10000.0
EPS = 1e-6
ADAM_LR = 0.001
ADAM_B1 = 0.9
ADAM_B2 = 0.999
ADAM_EPS = 1e-08
ADAM_WD = 0.01
ADAM_STEP = 10
NEG = -1e30
VMEM_LIMIT_BYTES = 56 * 1024 * 1024

NN = ((1,), (0,))
NT = ((1,), (1,))
TN = ((0,), (0,))


def _dot(a, b, dims):
    return lax.dot_general(a, b, (dims, ((), ())), preferred_element_type=F32)


def _params(*sem):
    return pltpu.CompilerParams(dimension_semantics=sem, vmem_limit_bytes=VMEM_LIMIT_BYTES)


def _divisor_tile(n, limit, unit):
    if n <= limit:
        return n
    best = unit
    for t in range(unit, limit + 1, unit):
        if n % t == 0:
            best = t
    assert n % best == 0, (n, limit, unit)
    return best


def _mm(a, b, mode, out_dtype, name, residual=None, after=None):
    parts = a.shape[0] if a.ndim == 3 else 1
    a2 = a.shape[-2:]
    if mode == "nn":
        (m, kp), (k2, n) = a2, b.shape
        k, mp = kp * parts, m
    elif mode == "nt":
        (m, kp), (n, k2) = a2, b.shape
        k, mp = kp * parts, m
    else:
        (k, mp), (k2, n) = a2, b.shape
        m, kp = mp * parts, k
    assert k == k2, (name, a.shape, b.shape)
    tk = _divisor_tile(kp, 2816, 128)
    nk = k // tk
    tm = _divisor_tile(mp, 512 if mode == "tn" else 1024, 128)
    tn = _divisor_tile(n, 2048 if mode == "tn" else (1024 if nk > 1 else 512), 128)
    kpb, mpb = kp // tk, mp // tm
    dims = {"nn": NN, "nt": NT, "tn": TN}[mode]
    lead = (None,) if a.ndim == 3 else ()
    if mode == "tn":
        a_index = lambda i, j, kk: (i // mpb, kk, i % mpb) if lead else (kk, i)
        a_spec = pl.BlockSpec(lead + (tk, tm), a_index)
    else:
        a_index = lambda i, j, kk: (kk // kpb, i, kk % kpb) if lead else (i, kk)
        a_spec = pl.BlockSpec(lead + (tm, tk), a_index)
    if mode == "nt":
        b_spec = pl.BlockSpec((tn, tk), lambda i, j, kk: (j, kk))
    else:
        b_spec = pl.BlockSpec((tk, tn), lambda i, j, kk: (kk, j))
    o_spec = pl.BlockSpec((tm, tn), lambda i, j, kk: (i, j))
    has_res = residual is not None

    def body(*refs):
        a_ref, b_ref = refs[:2]
        r_ref = refs[2] if has_res else None
        o_ref, acc_ref = refs[-2:]
        kk = pl.program_id(2)
        p = _dot(a_ref[...], b_ref[...], dims)

        def finish(total):
            if has_res:
                total = total + r_ref[...]
            o_ref[...] = total.astype(o_ref.dtype)

        if nk == 1:
            finish(p)
        else:
            @pl.when(kk == 0)
            def _():
                acc_ref[...] = p

            @pl.when(jnp.logical_and(kk > 0, kk < nk - 1))
            def _():
                acc_ref[...] += p

            @pl.when(kk == nk - 1)
            def _():
                finish(acc_ref[...] + p)

    in_specs = [a_spec, b_spec] + ([o_spec] if has_res else []) + ([ANY] if after is not None else [])
    args = (a, b) + ((residual,) if has_res else ()) + ((after,) if after is not None else ())
    acc_shape = (tm, tn) if nk > 1 else (8, LANES)
    return pl.pallas_call(
        body,
        name=name,
        grid=(m // tm, n // tn, nk),
        in_specs=in_specs,
        out_specs=o_spec,
        out_shape=jax.ShapeDtypeStruct((m, n), out_dtype),
        scratch_shapes=[pltpu.VMEM(acc_shape, F32)],
        compiler_params=_params("parallel", "parallel", "arbitrary"),
    )(*args)


def _rmsnorm_fwd(x, g, name, after=None):
    t, d = x.shape
    tr = _divisor_tile(t, 256, 8)

    def body(x_ref, g_ref, *rest):
        h_ref = rest[-1]
        xv = x_ref[...]
        rstd = lax.rsqrt(jnp.mean(xv * xv, axis=-1, keepdims=True) + EPS)
        h_ref[...] = (xv * rstd * g_ref[...]).astype(h_ref.dtype)

    return pl.pallas_call(
        body,
        name=name,
        grid=(t // tr,),
        in_specs=[pl.BlockSpec((tr, d), lambda i: (i, 0)), pl.BlockSpec((1, d), lambda i: (0, 0))]
        + ([ANY] if after is not None else []),
        out_specs=pl.BlockSpec((tr, d), lambda i: (i, 0)),
        out_shape=jax.ShapeDtypeStruct((t, d), BF16),
        compiler_params=_params("parallel"),
    )(x, g, *(() if after is None else (after,)))


def _rmsnorm_bwd(x, g, dh, dres, name, after=None):
    t, d = x.shape
    tr = _divisor_tile(t, 256, 8)

    def body(x_ref, g_ref, dh_ref, dres_ref, *rest):
        dx_ref, dx16_ref, dg_ref = rest[-3:]
        i = pl.program_id(0)
        xv = x_ref[...]
        rstd = lax.rsqrt(jnp.mean(xv * xv, axis=-1, keepdims=True) + EPS)
        xh = xv * rstd
        dhv = dh_ref[...]
        dxh = dhv * g_ref[...]
        dx = dres_ref[...] + rstd * (dxh - xh * jnp.mean(dxh * xh, axis=-1, keepdims=True))
        dx_ref[...] = dx
        dx16_ref[...] = dx.astype(dx16_ref.dtype)
        part = jnp.broadcast_to(jnp.sum(dhv * xh, axis=0, keepdims=True), dg_ref.shape)

        @pl.when(i == 0)
        def _():
            dg_ref[...] = part

        @pl.when(i > 0)
        def _():
            dg_ref[...] += part

    row = pl.BlockSpec((tr, d), lambda i: (i, 0))
    return pl.pallas_call(
        body,
        name=name,
        grid=(t // tr,),
        in_specs=[row, pl.BlockSpec((1, d), lambda i: (0, 0)), row, row] + ([ANY] if after is not None else []),
        out_specs=[row, row, pl.BlockSpec((8, d), lambda i: (0, 0))],
        out_shape=[jax.ShapeDtypeStruct((t, d), F32), jax.ShapeDtypeStruct((t, d), BF16),
                   jax.ShapeDtypeStruct((8, d), F32)],
        compiler_params=_params("arbitrary"),
    )(x, g, dh, dres, *(() if after is None else (after,)))


def _lane(shape):
    return lax.broadcasted_iota(jnp.int32, shape, 1)


def _group_sum64(s):
    row = lax.broadcasted_iota(jnp.int32, (LANES, LANES), 0)
    col = lax.broadcasted_iota(jnp.int32, (LANES, LANES), 1)
    ones = jnp.where((row >= HEAD_DIM) == (col >= HEAD_DIM), 1.0, 0.0).astype(BF16)
    out = []
    for t in range(s.shape[1] // LANES):
        piece = s[:, LANES * t:LANES * t + LANES]
        hi = piece.astype(BF16)
        lo = (piece - hi.astype(F32)).astype(BF16)
        out.append(_dot(hi, ones, NN) + _dot(lo, ones, NN))
    return out[0] if len(out) == 1 else jnp.concatenate(out, axis=1)


def _swap32(x):
    w = x.shape[1]
    return jnp.where((_lane(x.shape) & 32) == 0, pltpu.roll(x, w - 32, axis=1), pltpu.roll(x, 32, axis=1))


def _rope(x, c, s):
    return x * c + _swap32(x) * s


def _rope_t(dy, c, s):
    return dy * c + _swap32(dy * s)


def _head_norm(x):
    rstd = lax.rsqrt(_group_sum64(x * x) * (1.0 / HEAD_DIM) + EPS)
    return x * rstd, rstd


def _head_norm_bwd(dxh, xh, rstd):
    return rstd * (dxh - xh * (_group_sum64(dxh * xh) * (1.0 / HEAD_DIM)))


def _roll64(x):
    return pltpu.roll(x, 64, axis=1)


def _attn_specs(wq, wk):
    kb = wq // wk
    prev = lambda i: jnp.maximum(i - 1, 0)
    return dict(
        q=pl.BlockSpec((BLOCK, wq), lambda i: (i, 0)),
        kc=pl.BlockSpec((BLOCK, wk), lambda i: (i, kb)),
        kp=pl.BlockSpec((BLOCK, wk), lambda i: (prev(i), kb)),
        vc=pl.BlockSpec((BLOCK, wk), lambda i: (i, kb + 1)),
        vp=pl.BlockSpec((BLOCK, wk), lambda i: (prev(i), kb + 1)),
        tq=pl.BlockSpec((BLOCK, wq), lambda i: (i, 0)),
        tkp=pl.BlockSpec((BLOCK, wk), lambda i: (prev(i), 0)),
        gq=pl.BlockSpec((1, wq), lambda i: (0, 0)),
        gk=pl.BlockSpec((1, wk), lambda i: (0, 0)),
        sinks=pl.BlockSpec(memory_space=pltpu.SMEM),
    )


def _attn_prologue(i, q_ref, kc_ref, kp_ref, cq_ref, sq_ref, ckp_ref, skp_ref, gq_ref, gk_ref):
    wk = kc_ref.shape[1]
    cq, sq = cq_ref[...], sq_ref[...]
    ck, sk = cq[:, :wk], sq[:, :wk]
    qh, q_rstd = _head_norm(q_ref[...])
    kch, kc_rstd = _head_norm(kc_ref[...])
    kph, kp_rstd = _head_norm(kp_ref[...])
    qn = _rope(qh * gq_ref[...], cq, sq)
    knc = _rope(kch * gk_ref[...], ck, sk)
    knp = _rope(kph * gk_ref[...], ckp_ref[...], skp_ref[...])
    row = lax.broadcasted_iota(jnp.int32, (BLOCK, BLOCK), 0)
    col = lax.broadcasted_iota(jnp.int32, (BLOCK, BLOCK), 1)
    mask_c = col <= row
    mask_p = jnp.logical_and(col > row, i > 0)
    half = (col >= 64).astype(jnp.int32)
    return dict(cq=cq, sq=sq, ck=ck, sk=sk, qh=qh, q_rstd=q_rstd, kch=kch, kc_rstd=kc_rstd, kph=kph,
                kp_rstd=kp_rstd, qn=qn, knc=knc, knp=knp, mask_c=mask_c, mask_p=mask_p, half=half)


def _head_scores(st, t, e, sink, scale):
    g = (2 * t) // Q_PER_KV
    ks, kpar = g // 2, g % 2
    sl = slice(LANES * ks, LANES * ks + LANES)
    mine = st["half"] == e
    qm = jnp.where(mine, st["qn"][:, LANES * t:LANES * t + LANES], 0.0).astype(BF16)
    kc, kp = st["knc"][:, sl], st["knp"][:, sl]
    flip = e != kpar
    if flip:
        kc, kp = _roll64(kc), _roll64(kp)
    kc, kp = kc.astype(BF16), kp.astype(BF16)
    s_c = jnp.where(st["mask_c"], _dot(qm, kc, NT) * scale, NEG)
    s_p = jnp.where(st["mask_p"], _dot(qm, kp, NT) * scale, NEG)
    m = jnp.maximum(jnp.maximum(jnp.max(s_c, axis=1, keepdims=True), jnp.max(s_p, axis=1, keepdims=True)), sink)
    p_c, p_p = jnp.exp(s_c - m), jnp.exp(s_p - m)
    p_s = jnp.exp(sink - m)
    inv = 1.0 / (jnp.sum(p_c, axis=1, keepdims=True) + jnp.sum(p_p, axis=1, keepdims=True) + p_s)
    return dict(sl=sl, mine=mine, flip=flip, qm=qm, kc=kc, kp=kp, pr_c=p_c * inv, pr_p=p_p * inv, pr_s=p_s * inv)


def _attn_fwd(proj, tables, gq, gk, sinks, wq, wk, name, after=None):
    t = proj.shape[0]
    nb = t // BLOCK
    sp = _attn_specs(wq, wk)
    scale = HEAD_DIM ** -0.5
    cos_t, sin_t = tables

    def body(sinks_ref, q_ref, kc_ref, kp_ref, vc_ref, vp_ref, cq_ref, sq_ref, ckp_ref, skp_ref, gq_ref, gk_ref,
             *rest):
        o_ref = rest[-1]
        i = pl.program_id(0)
        st = _attn_prologue(i, q_ref, kc_ref, kp_ref, cq_ref, sq_ref, ckp_ref, skp_ref, gq_ref, gk_ref)
        vc_all, vp_all = vc_ref[...], vp_ref[...]
        for ts in range(wq // LANES):
            acc = jnp.zeros((BLOCK, LANES), F32)
            for e in (0, 1):
                hs = _head_scores(st, ts, e, sinks_ref[2 * ts + e], scale)
                vc, vp = vc_all[:, hs["sl"]], vp_all[:, hs["sl"]]
                if hs["flip"]:
                    vc, vp = _roll64(vc), _roll64(vp)
                vc = jnp.where(hs["mine"], vc, 0.0).astype(BF16)
                vp = jnp.where(hs["mine"], vp, 0.0).astype(BF16)
                acc = acc + _dot(hs["pr_c"].astype(BF16), vc, NN) + _dot(hs["pr_p"].astype(BF16), vp, NN)
            o_ref[:, LANES * ts:LANES * ts + LANES] = acc.astype(o_ref.dtype)

    return pl.pallas_call(
        body,
        name=name,
        grid=(nb,),
        in_specs=[sp["sinks"], sp["q"], sp["kc"], sp["kp"], sp["vc"], sp["vp"], sp["tq"], sp["tq"], sp["tkp"],
                  sp["tkp"], sp["gq"], sp["gk"]] + ([ANY] if after is not None else []),
        out_specs=pl.BlockSpec((BLOCK, wq), lambda i: (i, 0)),
        out_shape=jax.ShapeDtypeStruct((t, wq), BF16),
        compiler_params=_params("parallel"),
    )(sinks, proj, proj, proj, proj, proj, cos_t, sin_t, cos_t, sin_t, gq, gk, *(() if after is None else (after,)))


def _attn_bwd(proj, dout, tables, gq, gk, sinks, wq, wk, name, after=None):
    t = proj.shape[0]
    nb = t // BLOCK
    sp = _attn_specs(wq, wk)
    scale = HEAD_DIM ** -0.5
    cos_t, sin_t = tables

    def body(sinks_ref, q_ref, kc_ref, kp_ref, vc_ref, vp_ref, cq_ref, sq_ref, ckp_ref, skp_ref, gq_ref, gk_ref,
             do_ref, *rest):
        dq_ref, dk_ref, dv_ref, dgq_ref, dgk_ref, dsk_ref, dqn_ref, dknc_ref, dknp_ref, dvc_ref, dvp_ref = rest[-11:]
        i = pl.program_id(0)
        st = _attn_prologue(i, q_ref, kc_ref, kp_ref, cq_ref, sq_ref, ckp_ref, skp_ref, gq_ref, gk_ref)
        vc_all, vp_all = vc_ref[...], vp_ref[...]
        dknc_ref[...] = jnp.zeros_like(dknc_ref)
        dknp_ref[...] = jnp.zeros_like(dknp_ref)
        dvc_ref[...] = jnp.zeros_like(dvc_ref)
        dvp_ref[...] = jnp.zeros_like(dvp_ref)
        lane8 = _lane((8, LANES))
        dsinks = jnp.zeros((8, LANES), F32)
        for ts in range(wq // LANES):
            dq_acc = jnp.zeros((BLOCK, LANES), F32)
            for e in (0, 1):
                hs = _head_scores(st, ts, e, sinks_ref[2 * ts + e], scale)
                sl, flip = hs["sl"], hs["flip"]
                vc, vp = vc_all[:, sl], vp_all[:, sl]
                if flip:
                    vc, vp = _roll64(vc), _roll64(vp)
                dom = jnp.where(hs["mine"], do_ref[:, LANES * ts:LANES * ts + LANES], 0.0).astype(BF16)
                dp_c = _dot(dom, vc.astype(BF16), NT)
                dp_p = _dot(dom, vp.astype(BF16), NT)
                pr_c, pr_p = hs["pr_c"], hs["pr_p"]
                rs = jnp.sum(pr_c * dp_c, axis=1, keepdims=True) + jnp.sum(pr_p * dp_p, axis=1, keepdims=True)
                ds_c = (pr_c * (dp_c - rs) * scale)
                ds_p = (pr_p * (dp_p - rs) * scale)
                dsink = jnp.sum(-hs["pr_s"] * rs)
                dsinks = dsinks + jnp.where(lane8 == 2 * ts + e, dsink, 0.0)
                dq_acc = dq_acc + jnp.where(
                    hs["mine"], _dot(ds_c.astype(BF16), hs["kc"], NN) + _dot(ds_p.astype(BF16), hs["kp"], NN), 0.0)
                dv_c = _dot(pr_c.T.astype(BF16), dom, NN)
                dv_p = _dot(pr_p.T.astype(BF16), dom, NN)
                dk_c = _dot(ds_c.T.astype(BF16), hs["qm"], NN)
                dk_p = _dot(ds_p.T.astype(BF16), hs["qm"], NN)
                if flip:
                    dv_c, dv_p, dk_c, dk_p = _roll64(dv_c), _roll64(dv_p), _roll64(dk_c), _roll64(dk_p)
                dvc_ref[:, sl] += dv_c
                dvp_ref[:, sl] += dv_p
                dknc_ref[:, sl] += dk_c
                dknp_ref[:, sl] += dk_p
            dqn_ref[:, LANES * ts:LANES * ts + LANES] = dq_acc

        gqv, gkv = gq_ref[...], gk_ref[...]
        dqg = _rope_t(dqn_ref[...], st["cq"], st["sq"])
        dq_ref[...] = _head_norm_bwd(dqg * gqv, st["qh"], st["q_rstd"]).astype(dq_ref.dtype)
        dkcg = _rope_t(dknc_ref[...], st["ck"], st["sk"])
        dkpg = _rope_t(dknp_ref[...], ckp_ref[...], skp_ref[...])
        dk_cur = _head_norm_bwd(dkcg * gkv, st["kch"], st["kc_rstd"])
        dk_prev = _head_norm_bwd(dkpg * gkv, st["kph"], st["kp_rstd"])
        dgq_part = jnp.broadcast_to(jnp.sum(dqg * st["qh"], axis=0, keepdims=True), dgq_ref.shape)
        dgk_part = jnp.broadcast_to(
            jnp.sum(dkcg * st["kch"] + dkpg * st["kph"], axis=0, keepdims=True), dgk_ref.shape)
        cur = pl.ds(pl.multiple_of(i * BLOCK, BLOCK), BLOCK)
        dk_ref[cur, :] = dk_cur
        dv_ref[cur, :] = dvc_ref[...]

        @pl.when(i == 0)
        def _():
            dgq_ref[...] = dgq_part
            dgk_ref[...] = dgk_part
            dsk_ref[...] = dsinks

        @pl.when(i > 0)
        def _():
            before = pl.ds(pl.multiple_of((i - 1) * BLOCK, BLOCK), BLOCK)
            dk_ref[before, :] += dk_prev
            dv_ref[before, :] += dvp_ref[...]
            dgq_ref[...] += dgq_part
            dgk_ref[...] += dgk_part
            dsk_ref[...] += dsinks

    whole = lambda shape: pl.BlockSpec(shape, lambda i: (0, 0))
    return pl.pallas_call(
        body,
        name=name,
        grid=(nb,),
        in_specs=[sp["sinks"], sp["q"], sp["kc"], sp["kp"], sp["vc"], sp["vp"], sp["tq"], sp["tq"], sp["tkp"],
                  sp["tkp"], sp["gq"], sp["gk"], pl.BlockSpec((BLOCK, wq), lambda i: (i, 0))]
        + ([ANY] if after is not None else []),
        out_specs=[pl.BlockSpec((BLOCK, wq), lambda i: (i, 0)), whole((t, wk)), whole((t, wk)), whole((8, wq)),
                   whole((8, wk)), whole((8, LANES))],
        out_shape=[jax.ShapeDtypeStruct((t, wq), BF16), jax.ShapeDtypeStruct((t, wk), F32),
                   jax.ShapeDtypeStruct((t, wk), F32), jax.ShapeDtypeStruct((8, wq), F32),
                   jax.ShapeDtypeStruct((8, wk), F32), jax.ShapeDtypeStruct((8, LANES), F32)],
        scratch_shapes=[pltpu.VMEM((BLOCK, wq), F32), pltpu.VMEM((BLOCK, wk), F32), pltpu.VMEM((BLOCK, wk), F32),
                        pltpu.VMEM((BLOCK, wk), F32), pltpu.VMEM((BLOCK, wk), F32)],
        compiler_params=_params("arbitrary"),
    )(sinks, proj, proj, proj, proj, proj, cos_t, sin_t, cos_t, sin_t, gq, gk, dout,
      *(() if after is None else (after,)))


_GELU_K = math.sqrt(2.0 / math.pi)
_GELU_A = 0.044715


def _gelu(x):
    return 0.5 * x * (1.0 + jnp.tanh(_GELU_K * (x + _GELU_A * x * x * x)))


def _gelu_grad(x):
    th = jnp.tanh(_GELU_K * (x + _GELU_A * x * x * x))
    return 0.5 * (1.0 + th) + 0.5 * x * (1.0 - th * th) * (_GELU_K * (1.0 + 3.0 * _GELU_A * x * x))


def _group_ln(v):
    mu = jnp.mean(v, axis=1, keepdims=True)
    cen = v - mu
    rstd = lax.rsqrt(jnp.mean(cen * cen, axis=1, keepdims=True) + EPS)
    return cen * rstd, rstd


def _sgu_geometry(off_u, ws):
    cw = math.gcd(off_u, ws)
    return cw, ws // cw, off_u // cw, (off_u + ws) // cw


def _sgu_fwd(proj, ln_g, ln_b, w_s, bt, off_u, ws, name, after=None):
    t = proj.shape[0]
    nb = t // BLOCK
    cw, nc, ub, vb = _sgu_geometry(off_u, ws)
    gpc = cw // LANES
    ng = ws // LANES

    def body(u_ref, v_ref, g_ref, b_ref, w_ref, bt_ref, *rest):
        o_ref = rest[-1]
        jc = pl.program_id(0)
        row = lax.broadcasted_iota(jnp.int32, (BLOCK, BLOCK), 0)
        col = lax.broadcasted_iota(jnp.int32, (BLOCK, BLOCK), 1)
        lane_g = _lane((BLOCK, ng))
        for gi in range(gpc):
            sl = slice(LANES * gi, LANES * gi + LANES)
            xh, _ = _group_ln(_gelu(v_ref[:, sl]))
            vn = xh * g_ref[:, sl] + b_ref[:, sl]
            w = jnp.where(row >= col, w_ref[gi], 0.0).astype(BF16)
            bias = jnp.sum(jnp.where(lane_g == jc * gpc + gi, bt_ref[...], 0.0), axis=1, keepdims=True)
            s = _dot(w, vn.astype(BF16), NN) + bias
            o_ref[:, sl] = (_gelu(u_ref[:, sl]) * s).astype(o_ref.dtype)

    return pl.pallas_call(
        body,
        name=name,
        grid=(nc, nb),
        in_specs=[pl.BlockSpec((BLOCK, cw), lambda jc, i: (i, ub + jc)),
                  pl.BlockSpec((BLOCK, cw), lambda jc, i: (i, vb + jc)),
                  pl.BlockSpec((1, cw), lambda jc, i: (0, jc)),
                  pl.BlockSpec((1, cw), lambda jc, i: (0, jc)),
                  pl.BlockSpec((gpc, BLOCK, BLOCK), lambda jc, i: (jc, 0, 0)),
                  pl.BlockSpec((BLOCK, ng), lambda jc, i: (0, 0))] + ([ANY] if after is not None else []),
        out_specs=pl.BlockSpec((BLOCK, cw), lambda jc, i: (i, jc)),
        out_shape=jax.ShapeDtypeStruct((t, ws), BF16),
        compiler_params=_params("parallel", "parallel"),
    )(proj, proj, ln_g, ln_b, w_s, bt, *(() if after is None else (after,)))


def _sgu_bwd(proj, dout, ln_g, ln_b, w_s, bt, off_u, ws, name):
    t = proj.shape[0]
    nb = t // BLOCK
    cw, nc, ub, vb = _sgu_geometry(off_u, ws)
    gpc = cw // LANES
    ng = ws // LANES

    def body(u_ref, v_ref, g_ref, b_ref, w_ref, bt_ref, do_ref, du_ref, dv_ref, dg_ref, db_ref, dw_ref, dbs_ref,
             bacc_ref):
        jc = pl.program_id(0)
        i = pl.program_id(1)
        row = lax.broadcasted_iota(jnp.int32, (BLOCK, BLOCK), 0)
        col = lax.broadcasted_iota(jnp.int32, (BLOCK, BLOCK), 1)
        lane_g = _lane((BLOCK, ng))
        tri = row >= col

        @pl.when(i == 0)
        def _():
            dg_ref[...] = jnp.zeros_like(dg_ref)
            db_ref[...] = jnp.zeros_like(db_ref)
            dw_ref[...] = jnp.zeros_like(dw_ref)
            bacc_ref[...] = jnp.zeros_like(bacc_ref)

        for gi in range(gpc):
            sl = slice(LANES * gi, LANES * gi + LANES)
            u_raw, v_raw = u_ref[:, sl], v_ref[:, sl]
            xh, rstd = _group_ln(_gelu(v_raw))
            gam = g_ref[:, sl]
            vn = (xh * gam + b_ref[:, sl]).astype(BF16)
            w = jnp.where(tri, w_ref[gi], 0.0)
            bias = jnp.sum(jnp.where(lane_g == jc * gpc + gi, bt_ref[...], 0.0), axis=1, keepdims=True)
            s = _dot(w.astype(BF16), vn, NN) + bias
            dov = do_ref[:, sl]
            du_ref[:, sl] = (dov * s * _gelu_grad(u_raw)).astype(du_ref.dtype)
            ds = dov * _gelu(u_raw)
            ds16 = ds.astype(BF16)
            dw_ref[gi] += jnp.where(tri, _dot(ds16, vn, NT), 0.0)
            bacc_ref[gi] += ds
            dvn = _dot(w.T.astype(BF16), ds16, NN)
            dg_ref[:, sl] += jnp.broadcast_to(jnp.sum(dvn * xh, axis=0, keepdims=True), (8, LANES))
            db_ref[:, sl] += jnp.broadcast_to(jnp.sum(dvn, axis=0, keepdims=True), (8, LANES))
            dxh = dvn * gam
            dvg = rstd * (dxh - jnp.mean(dxh, axis=1, keepdims=True)
                          - xh * jnp.mean(dxh * xh, axis=1, keepdims=True))
            dv_ref[:, sl] = (dvg * _gelu_grad(v_raw)).astype(dv_ref.dtype)

        @pl.when(i == nb - 1)
        def _():
            for gi in range(gpc):
                dbs_ref[gi] = jnp.broadcast_to(jnp.sum(bacc_ref[gi].T, axis=0, keepdims=True), (8, LANES))

    blk = lambda base: pl.BlockSpec((BLOCK, cw), lambda jc, i: (i, base + jc))
    vec = pl.BlockSpec((1, cw), lambda jc, i: (0, jc))
    acc = pl.BlockSpec((8, cw), lambda jc, i: (0, jc))
    wsp = pl.BlockSpec((gpc, BLOCK, BLOCK), lambda jc, i: (jc, 0, 0))
    return pl.pallas_call(
        body,
        name=name,
        grid=(nc, nb),
        in_specs=[blk(ub), blk(vb), vec, vec, wsp, pl.BlockSpec((BLOCK, ng), lambda jc, i: (0, 0)), blk(0)],
        out_specs=[blk(0), blk(0), acc, acc, wsp, pl.BlockSpec((gpc, 8, LANES), lambda jc, i: (jc, 0, 0))],
        out_shape=[jax.ShapeDtypeStruct((t, ws), BF16), jax.ShapeDtypeStruct((t, ws), BF16),
                   jax.ShapeDtypeStruct((8, ws), F32), jax.ShapeDtypeStruct((8, ws), F32),
                   jax.ShapeDtypeStruct((ng, BLOCK, BLOCK), F32), jax.ShapeDtypeStruct((ng, 8, LANES), F32)],
        scratch_shapes=[pltpu.VMEM((gpc, BLOCK, BLOCK), F32)],
        compiler_params=_params("arbitrary", "arbitrary"),
    )(proj, proj, ln_g, ln_b, w_s, bt, dout)


def _sigmoid(x):
    return 1.0 / (1.0 + jnp.exp(-x))


def _merge_geometry(off_g, d):
    cw = math.gcd(off_g, d)
    return cw, d // cw, off_g // cw, (off_g + d) // cw


def _branches_fwd(attn, sgu, wab_t, wsb_t, proj, off_g, name):
    t = attn.shape[0]
    d = wab_t.shape[0]
    tn, _, ab, bb = _merge_geometry(off_g, d)
    tm = _divisor_tile(t, 1024, 128)

    def body(a1_ref, a2_ref, b1_ref, b2_ref, la_ref, lb_ref, bra_ref, brb_ref, o_ref):
        va = _dot(a1_ref[...], b1_ref[...], NT)
        vb = _dot(a2_ref[...], b2_ref[...], NT)
        bra_ref[...] = va
        brb_ref[...] = vb
        o_ref[...] = (_sigmoid(la_ref[...]) * va + _sigmoid(lb_ref[...]) * vb).astype(o_ref.dtype)

    rows = lambda w: pl.BlockSpec((tm, w), lambda i, j: (i, 0))
    wrow = lambda w: pl.BlockSpec((tn, w), lambda i, j: (j, 0))
    blk = lambda base: pl.BlockSpec((tm, tn), lambda i, j: (i, base + j))
    return pl.pallas_call(
        body,
        name=name,
        grid=(t // tm, d // tn),
        in_specs=[rows(attn.shape[1]), rows(sgu.shape[1]), wrow(wab_t.shape[1]), wrow(wsb_t.shape[1]), blk(ab),
                  blk(bb)],
        out_specs=[blk(0)] * 3,
        out_shape=[jax.ShapeDtypeStruct((t, d), F32), jax.ShapeDtypeStruct((t, d), F32),
                   jax.ShapeDtypeStruct((t, d), BF16)],
        compiler_params=_params("parallel", "parallel"),
    )(attn, sgu, wab_t, wsb_t, proj, proj)


def _branches_bwd(dx16, wo, br_a, br_b, proj, off_g, name, after=None):
    t, d = br_a.shape
    tn, _, ab, bb = _merge_geometry(off_g, d)
    tm = _divisor_tile(t, 1024, 128)
    k = dx16.shape[1]

    def body(a_ref, b_ref, bra_ref, brb_ref, la_ref, lb_ref, *rest):
        dab_ref, dl_ref = rest[-2:]
        dmv = _dot(a_ref[...], b_ref[...], NT)
        ga, gb = _sigmoid(la_ref[...]), _sigmoid(lb_ref[...])
        dab_ref[0] = (dmv * ga).astype(dab_ref.dtype)
        dab_ref[1] = (dmv * gb).astype(dab_ref.dtype)
        dl_ref[0] = (dmv * bra_ref[...] * ga * (1.0 - ga)).astype(dl_ref.dtype)
        dl_ref[1] = (dmv * brb_ref[...] * gb * (1.0 - gb)).astype(dl_ref.dtype)

    blk = lambda base: pl.BlockSpec((tm, tn), lambda i, j: (i, base + j))
    pair = pl.BlockSpec((2, tm, tn), lambda i, j: (0, i, j))
    return pl.pallas_call(
        body,
        name=name,
        grid=(t // tm, d // tn),
        in_specs=[pl.BlockSpec((tm, k), lambda i, j: (i, 0)), pl.BlockSpec((tn, k), lambda i, j: (j, 0)), blk(0),
                  blk(0), blk(ab), blk(bb)] + ([ANY] if after is not None else []),
        out_specs=[pair, pair],
        out_shape=[jax.ShapeDtypeStruct((2, t, d), BF16)] * 2,
        compiler_params=_params("parallel", "parallel"),
    )(dx16, wo, br_a, br_b, proj, proj, *(() if after is None else (after,)))


def _gate_up_fwd(h2, wgu_t, name, after=None):
    t, d = h2.shape
    f = wgu_t.shape[0] // 2
    tm = _divisor_tile(t, 1024, 128)
    tn = _divisor_tile(f, 512, 128)
    nb = f // tn

    def body(a_ref, bg_ref, bu_ref, *rest):
        gu_ref, act_ref = rest[-2:]
        av = a_ref[...]
        gv = _dot(av, bg_ref[...], NT)
        uv = _dot(av, bu_ref[...], NT)
        gu_ref[0] = gv
        gu_ref[1] = uv
        act_ref[...] = (gv * _sigmoid(gv) * uv).astype(act_ref.dtype)

    return pl.pallas_call(
        body,
        name=name,
        grid=(t // tm, nb),
        in_specs=[pl.BlockSpec((tm, d), lambda i, j: (i, 0)), pl.BlockSpec((tn, d), lambda i, j: (j, 0)),
                  pl.BlockSpec((tn, d), lambda i, j: (j + nb, 0))] + ([ANY] if after is not None else []),
        out_specs=[pl.BlockSpec((2, tm, tn), lambda i, j: (0, i, j)), pl.BlockSpec((tm, tn), lambda i, j: (i, j))],
        out_shape=[jax.ShapeDtypeStruct((2, t, f), F32), jax.ShapeDtypeStruct((t, f), BF16)],
        compiler_params=_params("parallel", "parallel"),
    )(h2, wgu_t, wgu_t, *(() if after is None else (after,)))


def _gate_up_bwd(dx16, wd, gu, name, after=None):
    t, d = dx16.shape
    f = wd.shape[0]
    tm = _divisor_tile(t, 1024, 128)
    tn = _divisor_tile(f, 512, 128)

    def body(a_ref, b_ref, gu_ref, *rest):
        o_ref = rest[-1]
        dav = _dot(a_ref[...], b_ref[...], NT)
        gv = gu_ref[0]
        sg = _sigmoid(gv)
        o_ref[0] = (dav * gu_ref[1] * (sg + gv * sg * (1.0 - sg))).astype(o_ref.dtype)
        o_ref[1] = (dav * gv * sg).astype(o_ref.dtype)

    pair = pl.BlockSpec((2, tm, tn), lambda i, j: (0, i, j))
    return pl.pallas_call(
        body,
        name=name,
        grid=(t // tm, f // tn),
        in_specs=[pl.BlockSpec((tm, d), lambda i, j: (i, 0)), pl.BlockSpec((tn, d), lambda i, j: (j, 0)), pair]
        + ([ANY] if after is not None else []),
        out_specs=pair,
        out_shape=jax.ShapeDtypeStruct((2, t, f), BF16),
        compiler_params=_params("parallel", "parallel"),
    )(dx16, wd, gu, *(() if after is None else (after,)))


def _loss_and_grad(y, target, name):
    t, d = y.shape
    tr = _divisor_tile(t, 256, 8)

    def body(y_ref, t_ref, l_ref, dy_ref, dy16_ref):
        i = pl.program_id(0)
        err = y_ref[...] - t_ref[...]
        dy_ref[...] = err * (1.0 / d)
        dy16_ref[...] = (err * (1.0 / d)).astype(dy16_ref.dtype)
        part = jnp.broadcast_to(0.5 * jnp.sum(err * err) * (1.0 / d), l_ref.shape)

        @pl.when(i == 0)
        def _():
            l_ref[...] = part

        @pl.when(i > 0)
        def _():
            l_ref[...] += part

    row = pl.BlockSpec((tr, d), lambda i: (i, 0))
    return pl.pallas_call(
        body,
        name=name,
        grid=(t // tr,),
        in_specs=[row, row],
        out_specs=[pl.BlockSpec((8, LANES), lambda i: (0, 0)), row, row],
        out_shape=[jax.ShapeDtypeStruct((8, LANES), F32), jax.ShapeDtypeStruct((t, d), F32),
                   jax.ShapeDtypeStruct((t, d), BF16)],
        compiler_params=_params("arbitrary"),
    )(y, target)


def _adam_math(w, g, m, v):
    m = ADAM_B1 * m + (1.0 - ADAM_B1) * g
    v = ADAM_B2 * v + (1.0 - ADAM_B2) * (g * g)
    m_hat = m / (1.0 - ADAM_B1 ** ADAM_STEP)
    v_hat = v / (1.0 - ADAM_B2 ** ADAM_STEP)
    delta = -ADAM_LR * (m_hat / (jnp.sqrt(v_hat) + ADAM_EPS) + ADAM_WD * w)
    return delta, m, v


def _row_tile(r, c, elems=512 * 1024):
    return _divisor_tile(r, max(8, elems // c // 8 * 8), 8)


def _adam(w, grads, m, v, chip, name, after=None):
    nl, r, c = w.shape
    tr = _row_tile(r, c, 256 * 1024)
    nb = r // tr
    counts = [len(terms) for terms, _ in grads]

    def body(chip_ref, *refs):
        w_ref, m_ref, v_ref = refs[:3]
        g_ref, d_ref, nm_ref, nv_ref = refs[-4:]
        layer = pl.program_id(0)
        g, at = None, 3
        for li, n in enumerate(counts):
            total = refs[at][...].astype(F32)
            for ref in refs[at + 1:at + n]:
                total = total + ref[...].astype(F32)
            g = total if g is None else jnp.where(layer == li, total, g)
            at += n
        g_ref[...] = g
        d_ref[...], nm_ref[...], nv_ref[...] = _adam_math(w_ref[...], g, m_ref[...], v_ref[...])

    def term_spec(li, p, by_owner):
        def index(l, i, chip_ref):
            rows = jnp.where(l < li, 0, jnp.where(l > li, nb - 1, i))
            return (p, chip_ref[0] if by_owner else 0, rows, 0)
        return pl.BlockSpec((None, None, tr, c), index)

    row = pl.BlockSpec((None, tr, c), lambda l, i, chip_ref: (l, i, 0))
    specs, arrays = [], []
    for li, (terms, p) in enumerate(grads):
        for term in terms:
            specs.append(term_spec(li, p, term.shape[1] == 4))
            arrays.append(term)
    return pl.pallas_call(
        body,
        name=name,
        grid_spec=pltpu.PrefetchScalarGridSpec(
            num_scalar_prefetch=1, grid=(nl, nb),
            in_specs=[row] * 3 + specs + ([ANY] if after is not None else []), out_specs=[row] * 4),
        out_shape=[jax.ShapeDtypeStruct((nl, r, c), F32)] * 4,
        compiler_params=_params("arbitrary", "arbitrary"),
    )(chip, w, m, v, *arrays, *(() if after is None else (after,)))


def _place_shard(parts, layer, dev, out_dtype, name, after=None):
    p = len(parts)
    _, r, c = parts[0].shape
    tr = _row_tile(r, c)

    def body(dev_ref, *refs):
        o_ref = refs[-1]
        x = refs[0][...]
        for pi in range(1, p):
            x = jnp.where(pl.program_id(0) == pi, refs[pi][...], x)
        o_ref[...] = x.astype(o_ref.dtype)

    return pl.pallas_call(
        body,
        name=name,
        grid_spec=pltpu.PrefetchScalarGridSpec(
            num_scalar_prefetch=1,
            grid=(p, r // tr),
            in_specs=[pl.BlockSpec((None, tr, c), lambda pi, i, dev_ref: (layer, i, 0))] * p
            + ([ANY] if after is not None else []),
            out_specs=pl.BlockSpec((None, None, tr, c), lambda pi, i, dev_ref: (pi, dev_ref[0], i, 0)),
        ),
        out_shape=jax.ShapeDtypeStruct((p, N_DEV, r, c), out_dtype),
        compiler_params=_params("parallel", "parallel"),
    )(dev, *parts, *(() if after is None else (after,)))


def _sum_sibling(g, land, core, name):
    p, _, _, r, c = g.shape
    tr = _row_tile(r, c, 1024 * 1024)

    def body(core_ref, g_ref, l_ref, o_ref):
        o_ref[...] = (g_ref[...].astype(F32) + l_ref[...].astype(F32)).astype(o_ref.dtype)

    return pl.pallas_call(
        body,
        name=name,
        grid_spec=pltpu.PrefetchScalarGridSpec(
            num_scalar_prefetch=1,
            grid=(p, 4, r // tr),
            in_specs=[pl.BlockSpec((None, None, None, tr, c), lambda pi, q, i, core_ref: (pi, q, core_ref[0], i, 0)),
                      pl.BlockSpec((None, None, None, tr, c), lambda pi, q, i, core_ref: (pi, q, 0, i, 0))],
            out_specs=pl.BlockSpec((None, None, tr, c), lambda pi, q, i, core_ref: (pi, q, i, 0)),
        ),
        out_shape=jax.ShapeDtypeStruct((p, 4, r, c), BF16),
        compiler_params=_params("parallel", "parallel", "parallel"),
    )(core, g, land)


def _sum_chips(s, lands, chip, name):
    p, _, r, c = s.shape
    tr = _row_tile(r, c)

    def body(chip_ref, s_ref, l0_ref, l1_ref, l2_ref, o_ref):
        total = s_ref[...].astype(F32) + l0_ref[...].astype(F32)
        o_ref[...] = total + l1_ref[...].astype(F32) + l2_ref[...].astype(F32)

    land_spec = pl.BlockSpec((None, None, tr, c), lambda pi, i, chip_ref: (pi, 0, i, 0))
    return pl.pallas_call(
        body,
        name=name,
        grid_spec=pltpu.PrefetchScalarGridSpec(
            num_scalar_prefetch=1,
            grid=(p, r // tr),
            in_specs=[pl.BlockSpec((None, None, tr, c), lambda pi, i, chip_ref: (pi, chip_ref[0], i, 0)),
                      land_spec, land_spec, land_spec],
            out_specs=pl.BlockSpec((None, tr, c), lambda pi, i, chip_ref: (pi, i, 0)),
        ),
        out_shape=jax.ShapeDtypeStruct((p, r, c), F32),
        compiler_params=_params("parallel", "parallel"),
    )(chip, s, *lands)


def _small_reduce_adam(gathered, w, m, v, name):
    _, r, c = gathered.shape
    tr = _row_tile(r, c)

    def body(p_ref, w_ref, m_ref, v_ref, g_ref, d_ref, nm_ref, nv_ref):
        g = p_ref[0]
        for j in range(1, N_DEV):
            g = g + p_ref[j]
        g_ref[...] = g
        d_ref[...], nm_ref[...], nv_ref[...] = _adam_math(w_ref[...], g, m_ref[...], v_ref[...])

    row = pl.BlockSpec((tr, c), lambda i: (i, 0))
    return pl.pallas_call(
        body,
        name=name,
        grid=(r // tr,),
        in_specs=[pl.BlockSpec((N_DEV, tr, c), lambda i: (0, i, 0)), row, row, row],
        out_specs=[row] * 4,
        out_shape=[jax.ShapeDtypeStruct((r, c), F32)] * 4,
        compiler_params=_params("parallel"),
    )(gathered, w, m, v)


def _place():
    return lax.axis_index("x"), lax.axis_index("y"), lax.axis_index("c")


HBM =pl.BlockSpec(memory_space=pltpu.HBM)
SEM = pl.BlockSpec(memory_space=pltpu.SEMAPHORE)
TOKEN = pl.BlockSpec(memory_space=pltpu.VMEM)
EFFECT = pltpu.SideEffectType.DATAFLOW_SIDE_EFFECTING


def _in_hbm(a):
    return pltpu.with_memory_space_constraint(a, pltpu.HBM)


_FLIPS = {"me": (0, 0, 0), "s": (0, 0, 1), "x": (1, 0, 0), "y": (0, 1, 0), "d": (1, 1, 0)}
GATHER_STAGES = (
    (("s", "me", "all"), ("x", "me", "all"), ("y", "me", "all")),
    (("s", "x", "all"), ("s", "y", "all"), ("y", "x", "first"), ("x", "y", "second")),
    (("s", "d", "all"),),
)


def _flipped(place, *names):
    out = list(place)
    for name in names:
        out = [1 - p if f else p for p, f in zip(out, _FLIPS[name])]
    return tuple(out)


def _block_part(ref, place, part):
    px, py, pc = place
    rows = ref.shape[2]
    span = {"all": pl.ds(0, rows), "first": pl.ds(0, rows // 2), "second": pl.ds(rows // 2, rows // 2)}[part]
    return ref.at[:, pl.ds(4 * px + 2 * py + pc, 1), span]


def _split_start(bufs, moves, name, after=None):
    n, nm = len(bufs), len(moves)
    extra = 0 if after is None else 1

    def body(*refs):
        ssem, rsem = refs[n + extra], refs[n + extra + 1]
        outs, token = refs[n + extra + 2:2 * n + extra + 2], refs[2 * n + extra + 2]
        me = _place()
        for a in range(n):
            for k, (to, owner, part) in enumerate(moves):
                piece = _block_part(outs[a], _flipped(me, owner), part)
                pltpu.make_async_remote_copy(
                    src_ref=piece, dst_ref=piece, send_sem=ssem.at[nm * a + k], recv_sem=rsem.at[nm * a + k],
                    device_id=_flipped(me, to), device_id_type=MESH).start()
        token[...] = jnp.zeros_like(token)

    outs = pl.pallas_call(
        body,
        name=name,
        in_specs=[HBM] * n + [ANY] * extra,
        out_specs=[SEM, SEM] + [HBM] * n + [TOKEN],
        out_shape=[pltpu.SemaphoreType.DMA((nm * n,))] * 2 + [pltpu.HBM(b.shape, b.dtype) for b in bufs]
        + [jax.ShapeDtypeStruct((8, LANES), F32)],
        input_output_aliases={i: 2 + i for i in range(n)},
        compiler_params=pltpu.CompilerParams(has_side_effects=EFFECT),
    )(*[_in_hbm(b) for b in bufs], *(() if after is None else (after,)))
    return outs[0], outs[1], list(outs[2:2 + n]), outs[-1]


def _split_wait(send_sems, recv_sems, bufs, moves, after, name):
    n, nm = len(bufs), len(moves)

    def body(*refs):
        ins, ssem, rsem = refs[:n], refs[n], refs[n + 1]
        me = _place()
        for a in range(n):
            for k, (to, owner, part) in enumerate(moves):
                landed = _block_part(ins[a], _flipped(me, owner, to), part)
                cp = pltpu.make_async_remote_copy(
                    src_ref=landed, dst_ref=landed, send_sem=ssem.at[nm * a + k], recv_sem=rsem.at[nm * a + k],
                    device_id=_flipped(me, to), device_id_type=MESH)
                cp.wait_send()
                cp.wait_recv()

    return pl.pallas_call(
        body,
        name=name,
        in_specs=[HBM] * n + [SEM, SEM, ANY],
        out_specs=[HBM] * n,
        out_shape=[pltpu.HBM(b.shape, b.dtype) for b in bufs],
        input_output_aliases={i: i for i in range(n)},
        compiler_params=pltpu.CompilerParams(has_side_effects=EFFECT),
    )(*bufs, send_sems, recv_sems, after)


def _chips_start(sums, name):
    n = len(sums)

    def body(*refs):
        ssem, rsem = refs[4 * n], refs[4 * n + 1]
        src, land = refs[4 * n + 2:5 * n + 2], refs[5 * n + 2:8 * n + 2]
        token = refs[8 * n + 2]
        x, y, c = _place()
        chips = [(1 - x, y), (x, 1 - y), (1 - x, 1 - y)]
        for a in range(n):
            for k, (px, py) in enumerate(chips):
                pltpu.make_async_remote_copy(
                    src_ref=src[a].at[:, pl.ds(2 * px + py, 1)], dst_ref=land[3 * a + k], send_sem=ssem.at[3 * a + k],
                    recv_sem=rsem.at[3 * a + k], device_id=(px, py, c), device_id_type=MESH).start()
        token[...] = jnp.zeros_like(token)

    lands = []
    for s in sums:
        lands += [lax.empty((s.shape[0], 1) + s.shape[2:], s.dtype) for _ in range(3)]
    outs = pl.pallas_call(
        body,
        name=name,
        in_specs=[HBM] * (4 * n),
        out_specs=[SEM, SEM] + [HBM] * (4 * n) + [TOKEN],
        out_shape=[pltpu.SemaphoreType.DMA((3 * n,))] * 2 + [pltpu.HBM(b.shape, b.dtype) for b in list(sums) + lands]
        + [jax.ShapeDtypeStruct((8, LANES), F32)],
        input_output_aliases={i: 2 + i for i in range(4 * n)},
        compiler_params=pltpu.CompilerParams(has_side_effects=EFFECT),
    )(*[_in_hbm(b) for b in list(sums) + lands])
    return outs[0], outs[1], list(outs[2:2 + n]), list(outs[2 + n:2 + 4 * n]), outs[-1]


def _chips_wait(send_sems, recv_sems, sums, lands, after, name):
    n = len(sums)

    def body(*refs):
        src, land = refs[:n], refs[n:4 * n]
        ssem, rsem = refs[4 * n], refs[4 * n + 1]
        x, y, c = _place()
        chips = [(1 - x, y), (x, 1 - y), (1 - x, 1 - y)]
        for a in range(n):
            for k, (px, py) in enumerate(chips):
                cp = pltpu.make_async_remote_copy(
                    src_ref=src[a].at[:, pl.ds(2 * px + py, 1)], dst_ref=land[3 * a + k], send_sem=ssem.at[3 * a + k],
                    recv_sem=rsem.at[3 * a + k], device_id=(px, py, c), device_id_type=MESH)
                cp.wait_send()
                cp.wait_recv()

    both = list(sums) + list(lands)
    outs = pl.pallas_call(
        body,
        name=name,
        in_specs=[HBM] * (4 * n) + [SEM, SEM, ANY],
        out_specs=[HBM] * (4 * n),
        out_shape=[pltpu.HBM(b.shape, b.dtype) for b in both],
        input_output_aliases={i: i for i in range(4 * n)},
        compiler_params=pltpu.CompilerParams(has_side_effects=EFFECT),
    )(*both, send_sems, recv_sems, after)
    return list(outs[:n]), [list(outs[n + 3 * a:n + 3 * a + 3]) for a in range(n)]


def _sibling_start(grads, name):
    n = len(grads)

    def body(*refs):
        ssem, rsem = refs[2 * n], refs[2 * n + 1]
        src, land = refs[2 * n + 2:3 * n + 2], refs[3 * n + 2:4 * n + 2]
        token = refs[4 * n + 2]
        x, y, c = _place()
        for a in range(n):
            pltpu.make_async_remote_copy(
                src_ref=src[a].at[:, :, pl.ds(1 - c, 1)], dst_ref=land[a], send_sem=ssem.at[a], recv_sem=rsem.at[a],
                device_id=(x, y, 1 - c), device_id_type=MESH).start()
        token[...] = jnp.zeros_like(token)

    lands = [lax.empty(g.shape[:2] + (1,) + g.shape[3:], g.dtype) for g in grads]
    both = list(grads) + lands
    outs = pl.pallas_call(
        body,
        name=name,
        in_specs=[HBM] * (2 * n),
        out_specs=[SEM, SEM] + [HBM] * (2 * n) + [TOKEN],
        out_shape=[pltpu.SemaphoreType.DMA((n,))] * 2 + [pltpu.HBM(b.shape, b.dtype) for b in both]
        + [jax.ShapeDtypeStruct((8, LANES), F32)],
        input_output_aliases={i: 2 + i for i in range(2 * n)},
        compiler_params=pltpu.CompilerParams(has_side_effects=EFFECT),
    )(*[_in_hbm(b) for b in both])
    return outs[0], outs[1], list(outs[2:2 + n]), list(outs[2 + n:2 + 2 * n]), outs[-1]


def _sibling_wait(send_sems, recv_sems, grads, lands, after, name):
    n = len(grads)

    def body(*refs):
        src, land = refs[:n], refs[n:2 * n]
        ssem, rsem = refs[2 * n], refs[2 * n + 1]
        x, y, c = _place()
        for a in range(n):
            cp = pltpu.make_async_remote_copy(
                src_ref=src[a].at[:, :, pl.ds(1 - c, 1)], dst_ref=land[a], send_sem=ssem.at[a], recv_sem=rsem.at[a],
                device_id=(x, y, 1 - c), device_id_type=MESH)
            cp.wait_send()
            cp.wait_recv()

    both = list(grads) + list(lands)
    outs = pl.pallas_call(
        body,
        name=name,
        in_specs=[HBM] * (2 * n) + [SEM, SEM, ANY],
        out_specs=[HBM] * (2 * n),
        out_shape=[pltpu.HBM(b.shape, b.dtype) for b in both],
        input_output_aliases={i: i for i in range(2 * n)},
        compiler_params=pltpu.CompilerParams(has_side_effects=EFFECT),
    )(*both, send_sems, recv_sems, after)
    return list(outs[:n]), list(outs[n:])


_SMALL = ("mix_norm", "q_norm", "k_norm", "sinks", "sgu_ln_g", "sgu_ln_b", "w_spatial", "b_spatial", "ffn_norm")


def _pack_rows(a):
    flat = a.reshape(-1)
    pad = (-flat.shape[0]) % LANES
    if pad:
        flat = jnp.pad(flat, (0, pad))
    return flat.reshape(-1, LANES)


def _pack(values):
    rows = jnp.concatenate([_pack_rows(values[k]) for k in _SMALL], axis=0)
    pad = (-rows.shape[0]) % 8
    if pad:
        rows = jnp.pad(rows, ((0, pad), (0, 0)))
    return rows


def _unpack(rows, like):
    out, at = {}, 0
    for k in _SMALL:
        size = like[k].size
        nrows = -(-size // LANES)
        out[k] = rows[at:at + nrows].reshape(-1)[:size].reshape(like[k].shape)
        at += nrows
    return out


def _rope_tables(t, wq):
    pos = jnp.arange(t, dtype=F32)
    inv_freq = jnp.power(ROPE_THETA, -jnp.arange(0, HEAD_DIM, 2, dtype=F32) / HEAD_DIM)
    ang = pos[:, None] * inv_freq[None, :]
    cos, sin = jnp.cos(ang), jnp.sin(ang)
    reps = wq // HEAD_DIM
    return (jnp.tile(jnp.concatenate([cos, cos], axis=1), (1, reps)),
            jnp.tile(jnp.concatenate([-sin, sin], axis=1), (1, reps)))


def kernel(x, mix_norm, w_in, q_norm, k_norm, sinks, sgu_ln_g, sgu_ln_b, w_spatial, b_spatial, w_attn_branch, w_sgu_branch, w_out, ffn_norm, w_gate, w_up, w_down, loss_target, m_mix_norm, m_w_in, m_q_norm, m_k_norm, m_sinks, m_sgu_ln_g, m_sgu_ln_b, m_w_spatial, m_b_spatial, m_w_attn_branch, m_w_sgu_branch, m_w_out, m_ffn_norm, m_w_gate, m_w_up, m_w_down, v_mix_norm, v_w_in, v_q_norm, v_k_norm, v_sinks, v_sgu_ln_g, v_sgu_ln_b, v_w_spatial, v_b_spatial, v_w_attn_branch, v_w_sgu_branch, v_w_out, v_ffn_norm, v_w_gate, v_w_up, v_w_down):
    names = ("mix_norm", "w_in", "q_norm", "k_norm", "sinks", "sgu_ln_g", "sgu_ln_b", "w_spatial", "b_spatial",
             "w_attn_branch", "w_sgu_branch", "w_out", "ffn_norm", "w_gate", "w_up", "w_down")
    weights = dict(zip(names, (mix_norm, w_in, q_norm, k_norm, sinks, sgu_ln_g, sgu_ln_b, w_spatial, b_spatial,
                               w_attn_branch, w_sgu_branch, w_out, ffn_norm, w_gate, w_up, w_down)))
    mom1 = dict(zip(names, (m_mix_norm, m_w_in, m_q_norm, m_k_norm, m_sinks, m_sgu_ln_g, m_sgu_ln_b, m_w_spatial,
                            m_b_spatial, m_w_attn_branch, m_w_sgu_branch, m_w_out, m_ffn_norm, m_w_gate, m_w_up,
                            m_w_down)))
    mom2 = dict(zip(names, (v_mix_norm, v_w_in, v_q_norm, v_k_norm, v_sinks, v_sgu_ln_g, v_sgu_ln_b, v_w_spatial,
                            v_b_spatial, v_w_attn_branch, v_w_sgu_branch, v_w_out, v_ffn_norm, v_w_gate, v_w_up,
                            v_w_down)))
    depth = w_in.shape[0]
    _, t, d = x.shape
    n_q_heads = sinks.shape[1]
    wq = n_q_heads * HEAD_DIM
    wk = wq // Q_PER_KV
    ws = sgu_ln_g.shape[1]
    ng = ws // LANES
    off_u = wq + 2 * wk
    off_g = off_u + 2 * ws
    tables = _rope_tables(t, wq)
    px, py, pc = _place()
    core = pc.astype(jnp.int32)[None]
    chip = (2 * px + py).astype(jnp.int32)[None]
    dev = (4 * px + 2 * py + pc).astype(jnp.int32)[None]

    layers = range(depth)
    chunks = ((0,), (1, 2, 3), (4,), (5,))
    sources = [[jnp.swapaxes(w_in, 1, 2)], [jnp.swapaxes(w_attn_branch, 1, 2)], [jnp.swapaxes(w_sgu_branch, 1, 2)],
               [w_out], [jnp.swapaxes(w_gate, 1, 2), jnp.swapaxes(w_up, 1, 2)], [w_down]]
    stream = [(l, ci) for l in layers for ci in range(len(chunks))]
    placed, state, token = {}, {}, None

    def send(key, after):
        state[key] = _split_start(placed[key], GATHER_STAGES[0], "gather_send_%d_%d" % key, after)
        return state[key][3]

    def advance(key, after, stage):
        send_sems, recv_sems, bufs, _ = state[key]
        bufs = _split_wait(send_sems, recv_sems, bufs, GATHER_STAGES[stage - 1], after, "gather_wait%d_%d_%d" % (stage, *key))
        state[key] = _split_start(bufs, GATHER_STAGES[stage], "gather_pass%d_%d_%d" % (stage, *key))
        return state[key][3]

    def relay(key, after):
        tok = advance(key, after, 1)
        at = stream.index(key)
        for later in stream[at + 2:at + 3] if at else stream[1:3]:
            tok = send(later, tok)
        return tok

    def ready(key, after):
        send_sems, recv_sems, bufs, _ = state.pop(key)
        bufs = _split_wait(send_sems, recv_sems, bufs, GATHER_STAGES[2], after, "gather_wait3_%d_%d" % key)
        return [f.reshape(f.shape[0] * f.shape[1] * f.shape[2], f.shape[3]) for f in bufs]

    for key in stream:
        l, ci = key
        placed[key] = [_place_shard(sources[a], l, dev, BF16, f"place_shard_{l}_{a}",
                                    after=token if a == chunks[ci][0] else None) for a in chunks[ci]]
        token = send(key, None) if key == stream[0] else placed[key][-1]

    saved = []
    xl = x[0]
    going = relay((0, 0), token)
    going = advance((0, 0), going, 2)
    for l in layers:
        gq = jnp.tile(q_norm[l], n_q_heads)[None]
        gk = jnp.tile(k_norm[l], n_q_heads // Q_PER_KV)[None]
        bt = b_spatial[l].T
        h = _rmsnorm_fwd(xl, mix_norm[l][None], f"mix_norm_fwd_{l}", after=going)
        (win_t,) = ready((l, 0), h)
        proj = _mm(h, win_t, "nt", F32, f"in_proj_{l}")
        going = relay((l, 1), proj)
        attn = _attn_fwd(proj, tables, gq, gk, sinks[l], wq, wk, f"attn_fwd_{l}", after=going)
        going = advance((l, 1), attn, 2)
        sgu = _sgu_fwd(proj, sgu_ln_g[l][None], sgu_ln_b[l][None], w_spatial[l], bt, off_u, ws, f"sgu_fwd_{l}",
                       after=going)
        wab_t, wsb_t, wo = ready((l, 1), sgu)
        br_a, br_b, merged = _branches_fwd(attn, sgu, wab_t, wsb_t, proj, off_g, f"branches_{l}")
        going = relay((l, 2), merged)
        x1 = _mm(merged, wo, "nn", F32, f"out_proj_{l}", residual=xl, after=going)
        going = advance((l, 2), x1, 2)
        h2 = _rmsnorm_fwd(x1, ffn_norm[l][None], f"ffn_norm_fwd_{l}", after=going)
        (wgu_t,) = ready((l, 2), h2)
        going = relay((l, 3), h2)
        gu, act = _gate_up_fwd(h2, wgu_t, f"gate_up_{l}", after=going)
        going = advance((l, 3), act, 2)
        if l + 1 < depth:
            going = relay((l + 1, 0), going)
        (wd,) = ready((l, 3), going)
        x2 = _mm(act, wd, "nn", F32, f"down_proj_{l}", residual=x1)
        if l + 1 < depth:
            going = advance((l + 1, 0), x2, 2)
        saved.append(dict(x0=xl, h=h, proj=proj, attn=attn, sgu=sgu, br_a=br_a, br_b=br_b, merged=merged, x1=x1,
                          h2=h2, gu=gu, act=act, gq=gq, gk=gk, bt=bt, win_t=win_t, wab_t=wab_t, wsb_t=wsb_t, wo=wo,
                          wgu_t=wgu_t, wd=wd))
        xl = x2

    loss_part, dx, dx16 = _loss_and_grad(xl, loss_target[0], "loss")
    loss = lax.psum(loss_part[0, 0], ("x", "y", "c"))

    def sibling_start(grads, tag):
        shaped = []
        for g, p in grads:
            rows, c = g.shape
            shaped.append(g.reshape(p, 4, 2, rows // (8 * p), c))
        send_sems, recv_sems, shaped, lands, tok = _sibling_start(shaped, f"rs_sibling_start_{tag}")
        return (send_sems, recv_sems, shaped, lands, tag), tok

    def chips_start(state, after):
        send_sems, recv_sems, shaped, lands, tag = state
        shaped, lands = _sibling_wait(send_sems, recv_sems, shaped, lands, after, f"rs_sibling_wait_{tag}")
        sums = [_sum_sibling(g, o, core, f"rs_add_sibling_{tag}_{a}") for a, (g, o) in enumerate(zip(shaped, lands))]
        send_sems, recv_sems, sums, lands, tok = _chips_start(sums, f"rs_chips_start_{tag}")
        return (send_sems, recv_sems, sums, lands, tag), tok

    def scatter_finish(state, after):
        send_sems, recv_sems, sums, lands, tag = state
        sums, lands = _chips_wait(send_sems, recv_sems, sums, lands, after, f"rs_chips_wait_{tag}")
        return [[s] + o for s, o in zip(sums, lands)]

    in_flight = [dict() for _ in layers]
    small_grads = [None] * depth
    tok, swap_in = None, None
    for l in reversed(layers):
        s = saved[l]
        dgu = _gate_up_bwd(dx16, s["wd"], s["gu"], f"d_gate_up_{l}", after=tok)
        if swap_in is not None:
            in_flight[l + 1]["in"], tok = chips_start(swap_in, dgu)
        g_wd = _mm(s["act"], dx16, "tn", BF16, f"g_w_down_{l}", after=tok)
        swap, tok_s = sibling_start([(g_wd, 1)], f"{l}_down")
        dh2 = _mm(dgu, s["wgu_t"], "nn", F32, f"d_h2_{l}", after=tok_s)
        in_flight[l]["down"], tok = chips_start(swap, dh2)
        g_wgu_t = _mm(dgu, s["h2"], "tn", BF16, f"g_w_gate_up_{l}", after=tok)
        swap, tok_s = sibling_start([(g_wgu_t, 2)], f"{l}_gate_up")
        dx1, dx1_16, g_ffn = _rmsnorm_bwd(s["x1"], ffn_norm[l][None], dh2, dx, f"ffn_norm_bwd_{l}", after=tok_s)
        d_ab, d_logits = _branches_bwd(dx1_16, s["wo"], s["br_a"], s["br_b"], s["proj"], off_g, f"d_branches_{l}")
        in_flight[l]["gate_up"], tok = chips_start(swap, d_ab)
        g_wo = _mm(s["merged"], dx1_16, "tn", BF16, f"g_w_out_{l}", after=tok)
        d_a, d_b = d_ab[0], d_ab[1]
        dattn = _mm(d_a, s["wab_t"], "nn", F32, f"d_attn_{l}", after=g_wo)
        g_wab_t = _mm(d_a, s["attn"], "tn", BF16, f"g_w_attn_branch_{l}")
        dsgu = _mm(d_b, s["wsb_t"], "nn", F32, f"d_sgu_{l}")
        g_wsb_t = _mm(d_b, s["sgu"], "tn", BF16, f"g_w_sgu_branch_{l}")
        swap, tok_s = sibling_start([(g_wab_t, 1), (g_wsb_t, 1), (g_wo, 1)], f"{l}_mix")
        dq, dk, dv, g_gq, g_gk, g_sinks = _attn_bwd(s["proj"], dattn, tables, s["gq"], s["gk"], sinks[l], wq, wk,
                                                    f"attn_bwd_{l}", after=tok_s)
        du, dvv, g_lng, g_lnb, g_ws, g_bs = _sgu_bwd(s["proj"], dsgu, sgu_ln_g[l][None], sgu_ln_b[l][None],
                                                     w_spatial[l], s["bt"], off_u, ws, f"sgu_bwd_{l}")
        dproj = jnp.concatenate([dq, dk.astype(BF16), dv.astype(BF16), du, dvv, d_logits[0], d_logits[1]], axis=1)
        dh = _mm(dproj, s["win_t"], "nn", F32, f"d_h_{l}")
        in_flight[l]["mix"], tok = chips_start(swap, dh)
        g_win_t = _mm(dproj, s["h"], "tn", BF16, f"g_w_in_{l}", after=tok)
        swap_in, tok = sibling_start([(g_win_t, 1)], f"{l}_in")
        dx, dx16, g_mix = _rmsnorm_bwd(s["x0"], mix_norm[l][None], dh, dx1, f"mix_norm_bwd_{l}", after=tok)
        small_grads[l] = dict(
            mix_norm=g_mix[0], q_norm=g_gq[0].reshape(n_q_heads, HEAD_DIM).sum(0),
            k_norm=g_gk[0].reshape(n_q_heads // Q_PER_KV, HEAD_DIM).sum(0), sinks=g_sinks[0, :n_q_heads],
            sgu_ln_g=g_lng[0], sgu_ln_b=g_lnb[0], w_spatial=g_ws, b_spatial=g_bs[:, 0, :], ffn_norm=g_ffn[0])
    grad_x = dx[None]

    result = {key: {} for key in ("grad", "delta", "m", "v")}
    layer_like = {k: weights[k][0] for k in _SMALL}
    packed_g = jnp.concatenate([_pack(small_grads[l]) for l in layers], axis=0)
    rows_per_layer = packed_g.shape[0] // depth
    small_buf = _place_shard([packed_g[None]], 0, dev, F32, "place_small_grads", after=tok)
    send_sems, recv_sems, small_bufs, tok = _split_start([small_buf], GATHER_STAGES[0], "gather_send_small")
    in_flight[0]["in"], tok = chips_start(swap_in, tok)

    def update(k, grads, transposed, after):
        view = (lambda a: jnp.swapaxes(a, 1, 2)) if transposed else (lambda a: a)
        outs = _adam(view(weights[k]), grads, view(mom1[k]), view(mom2[k]), chip, f"adam_{k}", after=after)
        for key, val in zip(("grad", "delta", "m", "v"), outs):
            result[key][k] = view(val)
        return outs[3]

    def plain(terms, tag):
        s, lands = terms[0], terms[1:]
        g = _sum_chips(s, lands, chip, f"rs_add_chips_{tag}")
        return [jnp.swapaxes(g, 1, 2)[:, None]]

    down = [scatter_finish(in_flight[l]["down"], tok) for l in reversed(layers)][::-1]
    tok = update("w_down", [(down[l][0], 0) for l in layers], False, None)
    gate_up = [scatter_finish(in_flight[l]["gate_up"], tok) for l in reversed(layers)][::-1]
    tok = update("w_gate", [(gate_up[l][0], 0) for l in layers], True, None)
    tok = update("w_up", [(gate_up[l][0], 1) for l in layers], True, tok)
    mix = [scatter_finish(in_flight[l]["mix"], tok) for l in reversed(layers)][::-1]
    tok = update("w_out", [(mix[l][2], 0) for l in layers], False, None)
    tok = update("w_attn_branch", [(plain(mix[l][0], f"{l}_attn_branch"), 0) for l in layers], False, tok)
    tok = update("w_sgu_branch", [(plain(mix[l][1], f"{l}_sgu_branch"), 0) for l in layers], False, tok)

    for stage in (1, 2):
        small_bufs = _split_wait(send_sems, recv_sems, small_bufs, GATHER_STAGES[stage - 1], tok,
                                 f"gather_wait{stage}_small")
        send_sems, recv_sems, small_bufs, tok = _split_start(small_bufs, GATHER_STAGES[stage],
                                                             f"gather_pass{stage}_small")
    packed = [jnp.concatenate([_pack({k: src[k][l] for k in _SMALL}) for l in layers], axis=0)
              for src in (weights, mom1, mom2)]
    (gathered_small,) = _split_wait(send_sems, recv_sems, small_bufs, GATHER_STAGES[2], packed[0],
                                    "gather_wait3_small")
    small = _small_reduce_adam(gathered_small[0], *packed, "small_reduce_adam")
    for key, rows in zip(("grad", "delta", "m", "v"), small):
        per_layer = [_unpack(rows[l * rows_per_layer:(l + 1) * rows_per_layer], layer_like) for l in layers]
        for k in _SMALL:
            result[key][k] = jnp.stack([per_layer[l][k] for l in layers])

    last = [scatter_finish(in_flight[l]["in"], small[0]) for l in reversed(layers)][::-1]
    update("w_in", [(last[l][0], 0) for l in layers], True, None)

    return (loss, grad_x, *[result["grad"][k] for k in names], *[result["delta"][k] for k in names],
            *[result["m"][k] for k in names], *[result["v"][k] for k in names])
```

```python
import functools
import math

import jax
import jax.numpy as jnp
from jax import lax
from jax.experimental import pallas as pl
from jax.experimental.pallas import tpu as pltpu

F32 = jnp.float32
BF16 = jnp.bfloat16
MESH = pl.DeviceIdType.MESH
ANY = pl.BlockSpec(memory_space=pl.ANY)

N_DEV = 8
HEAD_DIM = 64
Q_PER_KV = 4
BLOCK = 128
LANES = 128
ROPE_THETA = 10000.0
EPS = 1e-6
ADAM_LR = 0.001
ADAM_B1 = 0.9
ADAM_B2 = 0.999
ADAM_EPS = 1e-08
ADAM_WD = 0.01
ADAM_STEP = 10
NEG = -1e30
VMEM_LIMIT_BYTES = 56 * 1024 * 1024

NN = ((1,), (0,))
NT = ((1,), (1,))
TN = ((0,), (0,))


def _dot(a, b, dims):
    return lax.dot_general(a, b, (dims, ((), ())), preferred_element_type=F32)


def _params(*sem):
    return pltpu.CompilerParams(dimension_semantics=sem, vmem_limit_bytes=VMEM_LIMIT_BYTES)


def _divisor_tile(n, limit, unit):
    if n <= limit:
        return n
    best = unit
    for t in range(unit, limit + 1, unit):
        if n % t == 0:
            best = t
    assert n % best == 0, (n, limit, unit)
    return best


def _mm(a, b, mode, out_dtype, name, residual=None, after=None):
    parts = a.shape[0] if a.ndim == 3 else 1
    a2 = a.shape[-2:]
    if mode == "nn":
        (m, kp), (k2, n) = a2, b.shape
        k, mp = kp * parts, m
    elif mode == "nt":
        (m, kp), (n, k2) = a2, b.shape
        k, mp = kp * parts, m
    else:
        (k, mp), (k2, n) = a2, b.shape
        m, kp = mp * parts, k
    assert k == k2, (name, a.shape, b.shape)
    tk = _divisor_tile(kp, 2816, 128)
    nk = k // tk
    tm = _divisor_tile(mp, 512 if mode == "tn" else 1024, 128)
    tn = _divisor_tile(n, 2048 if mode == "tn" else (1024 if nk > 1 else 512), 128)
    kpb, mpb = kp // tk, mp // tm
    dims = {"nn": NN, "nt": NT, "tn": TN}[mode]
    lead = (None,) if a.ndim == 3 else ()
    if mode == "tn":
        a_index = lambda i, j, kk: (i // mpb, kk, i % mpb) if lead else (kk, i)
        a_spec = pl.BlockSpec(lead + (tk, tm), a_index)
    else:
        a_index = lambda i, j, kk: (kk // kpb, i, kk % kpb) if lead else (i, kk)
        a_spec = pl.BlockSpec(lead + (tm, tk), a_index)
    if mode == "nt":
        b_spec = pl.BlockSpec((tn, tk), lambda i, j, kk: (j, kk))
    else:
        b_spec = pl.BlockSpec((tk, tn), lambda i, j, kk: (kk, j))
    o_spec = pl.BlockSpec((tm, tn), lambda i, j, kk: (i, j))
    has_res = residual is not None

    def body(*refs):
        a_ref, b_ref = refs[:2]
        r_ref = refs[2] if has_res else None
        o_ref, acc_ref = refs[-2:]
        kk = pl.program_id(2)
        p = _dot(a_ref[...], b_ref[...], dims)

        def finish(total):
            if has_res:
                total = total + r_ref[...]
            o_ref[...] = total.astype(o_ref.dtype)

        if nk == 1:
            finish(p)
        else:
            @pl.when(kk == 0)
            def _():
                acc_ref[...] = p

            @pl.when(jnp.logical_and(kk > 0, kk < nk - 1))
            def _():
                acc_ref[...] += p

            @pl.when(kk == nk - 1)
            def _():
                finish(acc_ref[...] + p)

    in_specs = [a_spec, b_spec] + ([o_spec] if has_res else []) + ([ANY] if after is not None else [])
    args = (a, b) + ((residual,) if has_res else ()) + ((after,) if after is not None else ())
    acc_shape = (tm, tn) if nk > 1 else (8, LANES)
    return pl.pallas_call(
        body,
        name=name,
        grid=(m // tm, n // tn, nk),
        in_specs=in_specs,
        out_specs=o_spec,
        out_shape=jax.ShapeDtypeStruct((m, n), out_dtype),
        scratch_shapes=[pltpu.VMEM(acc_shape, F32)],
        compiler_params=_params("parallel", "parallel", "arbitrary"),
    )(*args)


def _rmsnorm_fwd(x, g, name, after=None):
    t, d = x.shape
    tr = _divisor_tile(t, 256, 8)

    def body(x_ref, g_ref, *rest):
        h_ref = rest[-1]
        xv = x_ref[...]
        rstd = lax.rsqrt(jnp.mean(xv * xv, axis=-1, keepdims=True) + EPS)
        h_ref[...] = (xv * rstd * g_ref[...]).astype(h_ref.dtype)

    return pl.pallas_call(
        body,
        name=name,
        grid=(t // tr,),
        in_specs=[pl.BlockSpec((tr, d), lambda i: (i, 0)), pl.BlockSpec((1, d), lambda i: (0, 0))]
        + ([ANY] if after is not None else []),
        out_specs=pl.BlockSpec((tr, d), lambda i: (i, 0)),
        out_shape=jax.ShapeDtypeStruct((t, d), BF16),
        compiler_params=_params("parallel"),
    )(x, g, *(() if after is None else (after,)))


def _rmsnorm_bwd(x, g, dh, dres, name, after=None):
    t, d = x.shape
    tr = _divisor_tile(t, 256, 8)

    def body(x_ref, g_ref, dh_ref, dres_ref, *rest):
        dx_ref, dx16_ref, dg_ref = rest[-3:]
        i = pl.program_id(0)
        xv = x_ref[...]
        rstd = lax.rsqrt(jnp.mean(xv * xv, axis=-1, keepdims=True) + EPS)
        xh = xv * rstd
        dhv = dh_ref[...]
        dxh = dhv * g_ref[...]
        dx = dres_ref[...] + rstd * (dxh - xh * jnp.mean(dxh * xh, axis=-1, keepdims=True))
        dx_ref[...] = dx
        dx16_ref[...] = dx.astype(dx16_ref.dtype)
        part = jnp.broadcast_to(jnp.sum(dhv * xh, axis=0, keepdims=True), dg_ref.shape)

        @pl.when(i == 0)
        def _():
            dg_ref[...] = part

        @pl.when(i > 0)
        def _():
            dg_ref[...] += part

    row = pl.BlockSpec((tr, d), lambda i: (i, 0))
    return pl.pallas_call(
        body,
        name=name,
        grid=(t // tr,),
        in_specs=[row, pl.BlockSpec((1, d), lambda i: (0, 0)), row, row] + ([ANY] if after is not None else []),
        out_specs=[row, row, pl.BlockSpec((8, d), lambda i: (0, 0))],
        out_shape=[jax.ShapeDtypeStruct((t, d), F32), jax.ShapeDtypeStruct((t, d), BF16),
                   jax.ShapeDtypeStruct((8, d), F32)],
        compiler_params=_params("arbitrary"),
    )(x, g, dh, dres, *(() if after is None else (after,)))


def _lane(shape):
    return lax.broadcasted_iota(jnp.int32, shape, 1)


def _group_sum64(s):
    row = lax.broadcasted_iota(jnp.int32, (LANES, LANES), 0)
    col = lax.broadcasted_iota(jnp.int32, (LANES, LANES), 1)
    ones = jnp.where((row >= HEAD_DIM) == (col >= HEAD_DIM), 1.0, 0.0).astype(BF16)
    out = []
    for t in range(s.shape[1] // LANES):
        piece = s[:, LANES * t:LANES * t + LANES]
        hi = piece.astype(BF16)
        lo = (piece - hi.astype(F32)).astype(BF16)
        out.append(_dot(hi, ones, NN) + _dot(lo, ones, NN))
    return out[0] if len(out) == 1 else jnp.concatenate(out, axis=1)


def _swap32(x):
    w = x.shape[1]
    return jnp.where((_lane(x.shape) & 32) == 0, pltpu.roll(x, w - 32, axis=1), pltpu.roll(x, 32, axis=1))


def _rope(x, c, s):
    return x * c + _swap32(x) * s


def _rope_t(dy, c, s):
    return dy * c + _swap32(dy * s)


def _head_norm(x):
    rstd = lax.rsqrt(_group_sum64(x * x) * (1.0 / HEAD_DIM) + EPS)
    return x * rstd, rstd


def _head_norm_bwd(dxh, xh, rstd):
    return rstd * (dxh - xh * (_group_sum64(dxh * xh) * (1.0 / HEAD_DIM)))


def _roll64(x):
    return pltpu.roll(x, 64, axis=1)


def _attn_specs(wq, wk):
    kb = wq // wk
    prev = lambda i: jnp.maximum(i - 1, 0)
    return dict(
        q=pl.BlockSpec((BLOCK, wq), lambda i: (i, 0)),
        kc=pl.BlockSpec((BLOCK, wk), lambda i: (i, kb)),
        kp=pl.BlockSpec((BLOCK, wk), lambda i: (prev(i), kb)),
        vc=pl.BlockSpec((BLOCK, wk), lambda i: (i, kb + 1)),
        vp=pl.BlockSpec((BLOCK, wk), lambda i: (prev(i), kb + 1)),
        tq=pl.BlockSpec((BLOCK, wq), lambda i: (i, 0)),
        tkp=pl.BlockSpec((BLOCK, wk), lambda i: (prev(i), 0)),
        gq=pl.BlockSpec((1, wq), lambda i: (0, 0)),
        gk=pl.BlockSpec((1, wk), lambda i: (0, 0)),
        sinks=pl.BlockSpec(memory_space=pltpu.SMEM),
    )


def _attn_prologue(i, q_ref, kc_ref, kp_ref, cq_ref, sq_ref, ckp_ref, skp_ref, gq_ref, gk_ref):
    wk = kc_ref.shape[1]
    cq, sq = cq_ref[...], sq_ref[...]
    ck, sk = cq[:, :wk], sq[:, :wk]
    qh, q_rstd = _head_norm(q_ref[...])
    kch, kc_rstd = _head_norm(kc_ref[...])
    kph, kp_rstd = _head_norm(kp_ref[...])
    qn = _rope(qh * gq_ref[...], cq, sq)
    knc = _rope(kch * gk_ref[...], ck, sk)
    knp = _rope(kph * gk_ref[...], ckp_ref[...], skp_ref[...])
    stacked = (Q_PER_KV * BLOCK, BLOCK)
    row = lax.broadcasted_iota(jnp.int32, stacked, 0) & (BLOCK - 1)
    col = lax.broadcasted_iota(jnp.int32, stacked, 1)
    mask_c = col <= row
    mask_p = jnp.logical_and(col > row, i > 0)
    half = (lax.broadcasted_iota(jnp.int32, (BLOCK, BLOCK), 1) >= HEAD_DIM).astype(jnp.int32)
    return dict(cq=cq, sq=sq, ck=ck, sk=sk, qh=qh, q_rstd=q_rstd, kch=kch, kc_rstd=kc_rstd, kph=kph,
                kp_rstd=kp_rstd, qn=qn, knc=knc, knp=knp, mask_c=mask_c, mask_p=mask_p, half=half)


def _stack_heads(x, g, half):
    kpar = g % 2
    pieces = []
    for j in range(Q_PER_KV):
        t, e = divmod(Q_PER_KV * g + j, 2)
        piece = jnp.where(half == e, x[:, LANES * t:LANES * t + LANES], 0.0)
        pieces.append(piece if e == kpar else _roll64(piece))
    return jnp.concatenate(pieces, axis=0)


def _unstack_heads(y, g, half):
    kpar = g % 2
    slabs = {}
    for j in range(Q_PER_KV):
        t, e = divmod(Q_PER_KV * g + j, 2)
        piece = jnp.where(half == kpar, y[BLOCK * j:BLOCK * j + BLOCK], 0.0)
        piece = piece if e == kpar else _roll64(piece)
        slabs[t] = piece if t not in slabs else slabs[t] + piece
    return slabs


def _group_scores(st, g, sinks_ref, scale):
    ks = g // 2
    sl = slice(LANES * ks, LANES * ks + LANES)
    q4 = _stack_heads(st["qn"], g, st["half"]).astype(BF16)
    kc, kp = st["knc"][:, sl].astype(BF16), st["knp"][:, sl].astype(BF16)
    rows = Q_PER_KV * BLOCK
    at = lax.broadcasted_iota(jnp.int32, (rows, 1), 0)
    head = jnp.zeros((rows, 1), jnp.int32)
    sink = jnp.zeros((rows, 1), F32) + sinks_ref[Q_PER_KV * g]
    for j in range(1, Q_PER_KV):
        head = jnp.where(at >= BLOCK * j, j, head)
        sink = jnp.where(at >= BLOCK * j, sinks_ref[Q_PER_KV * g + j], sink)
    s_c = jnp.where(st["mask_c"], _dot(q4, kc, NT) * scale, NEG)
    s_p = jnp.where(st["mask_p"], _dot(q4, kp, NT) * scale, NEG)
    m = jnp.maximum(jnp.maximum(jnp.max(s_c, axis=1, keepdims=True), jnp.max(s_p, axis=1, keepdims=True)), sink)
    p_c, p_p = jnp.exp(s_c - m), jnp.exp(s_p - m)
    p_s = jnp.exp(sink - m)
    inv = 1.0 / (jnp.sum(p_c, axis=1, keepdims=True) + jnp.sum(p_p, axis=1, keepdims=True) + p_s)
    return dict(sl=sl, head=head, q4=q4, kc=kc, kp=kp, pr_c=p_c * inv, pr_p=p_p * inv, pr_s=p_s * inv)


def _attn_fwd(proj, tables, gq, gk, sinks, wq, wk, name, after=None):
    t = proj.shape[0]
    nb = t // BLOCK
    sp = _attn_specs(wq, wk)
    scale = HEAD_DIM ** -0.5
    cos_t, sin_t = tables

    def body(sinks_ref, q_ref, kc_ref, kp_ref, vc_ref, vp_ref, cq_ref, sq_ref, ckp_ref, skp_ref, gq_ref, gk_ref,
             *rest):
        o_ref = rest[-1]
        i = pl.program_id(0)
        st = _attn_prologue(i, q_ref, kc_ref, kp_ref, cq_ref, sq_ref, ckp_ref, skp_ref, gq_ref, gk_ref)
        for g in range(wq // (Q_PER_KV * HEAD_DIM)):
            gs = _group_scores(st, g, sinks_ref, scale)
            own = st["half"] == g % 2
            vc = jnp.where(own, vc_ref[:, gs["sl"]], 0.0).astype(BF16)
            vp = jnp.where(own, vp_ref[:, gs["sl"]], 0.0).astype(BF16)
            out = _dot(gs["pr_c"].astype(BF16), vc, NN) + _dot(gs["pr_p"].astype(BF16), vp, NN)
            for ts, slab in _unstack_heads(out, g, st["half"]).items():
                o_ref[:, LANES * ts:LANES * ts + LANES] = slab.astype(o_ref.dtype)

    return pl.pallas_call(
        body,
        name=name,
        grid=(nb,),
        in_specs=[sp["sinks"], sp["q"], sp["kc"], sp["kp"], sp["vc"], sp["vp"], sp["tq"], sp["tq"], sp["tkp"],
                  sp["tkp"], sp["gq"], sp["gk"]] + ([ANY] if after is not None else []),
        out_specs=pl.BlockSpec((BLOCK, wq), lambda i: (i, 0)),
        out_shape=jax.ShapeDtypeStruct((t, wq), BF16),
        compiler_params=_params("parallel"),
    )(sinks, proj, proj, proj, proj, proj, cos_t, sin_t, cos_t, sin_t, gq, gk, *(() if after is None else (after,)))


def _attn_bwd(proj, dout, tables, gq, gk, sinks, wq, wk, name, after=None):
    t = proj.shape[0]
    nb = t // BLOCK
    sp = _attn_specs(wq, wk)
    scale = HEAD_DIM ** -0.5
    cos_t, sin_t = tables

    def body(sinks_ref, q_ref, kc_ref, kp_ref, vc_ref, vp_ref, cq_ref, sq_ref, ckp_ref, skp_ref, gq_ref, gk_ref,
             do_ref, *rest):
        dq_ref, dk_ref, dv_ref, dgq_ref, dgk_ref, dsk_ref, dqn_ref, dknc_ref, dknp_ref, dvc_ref, dvp_ref = rest[-11:]
        i = pl.program_id(0)
        st = _attn_prologue(i, q_ref, kc_ref, kp_ref, cq_ref, sq_ref, ckp_ref, skp_ref, gq_ref, gk_ref)
        dknc_ref[...] = jnp.zeros_like(dknc_ref)
        dknp_ref[...] = jnp.zeros_like(dknp_ref)
        dvc_ref[...] = jnp.zeros_like(dvc_ref)
        dvp_ref[...] = jnp.zeros_like(dvp_ref)
        lane8 = _lane((8, LANES))
        dsinks = jnp.zeros((8, LANES), F32)
        for g in range(wq // (Q_PER_KV * HEAD_DIM)):
            gs = _group_scores(st, g, sinks_ref, scale)
            sl = gs["sl"]
            do4 = _stack_heads(do_ref[...], g, st["half"]).astype(BF16)
            dp_c = _dot(do4, vc_ref[:, sl].astype(BF16), NT)
            dp_p = _dot(do4, vp_ref[:, sl].astype(BF16), NT)
            pr_c, pr_p = gs["pr_c"], gs["pr_p"]
            rs = jnp.sum(pr_c * dp_c, axis=1, keepdims=True) + jnp.sum(pr_p * dp_p, axis=1, keepdims=True)
            ds_c = (pr_c * (dp_c - rs) * scale).astype(BF16)
            ds_p = (pr_p * (dp_p - rs) * scale).astype(BF16)
            dsink_rows = -gs["pr_s"] * rs
            for j in range(Q_PER_KV):
                dsink = jnp.sum(jnp.where(gs["head"] == j, dsink_rows, 0.0))
                dsinks = dsinks + jnp.where(lane8 == Q_PER_KV * g + j, dsink, 0.0)
            dq4 = _dot(ds_c, gs["kc"], NN) + _dot(ds_p, gs["kp"], NN)
            for ts, slab in _unstack_heads(dq4, g, st["half"]).items():
                dqn_ref[:, LANES * ts:LANES * ts + LANES] = slab
            dvc_ref[:, sl] += _dot(pr_c.astype(BF16), do4, TN)
            dvp_ref[:, sl] += _dot(pr_p.astype(BF16), do4, TN)
            dknc_ref[:, sl] += _dot(ds_c, gs["q4"], TN)
            dknp_ref[:, sl] += _dot(ds_p, gs["q4"], TN)

        gqv, gkv = gq_ref[...], gk_ref[...]
        dqg = _rope_t(dqn_ref[...], st["cq"], st["sq"])
        dq_ref[...] = _head_norm_bwd(dqg * gqv, st["qh"], st["q_rstd"]).astype(dq_ref.dtype)
        dkcg = _rope_t(dknc_ref[...], st["ck"], st["sk"])
        dkpg = _rope_t(dknp_ref[...], ckp_ref[...], skp_ref[...])
        dk_cur = _head_norm_bwd(dkcg * gkv, st["kch"], st["kc_rstd"])
        dk_prev = _head_norm_bwd(dkpg * gkv, st["kph"], st["kp_rstd"])
        dgq_part = jnp.broadcast_to(jnp.sum(dqg * st["qh"], axis=0, keepdims=True), dgq_ref.shape)
        dgk_part = jnp.broadcast_to(
            jnp.sum(dkcg * st["kch"] + dkpg * st["kph"], axis=0, keepdims=True), dgk_ref.shape)
        cur = pl.ds(pl.multiple_of(i * BLOCK, BLOCK), BLOCK)
        dk_ref[cur, :] = dk_cur
        dv_ref[cur, :] = dvc_ref[...]

        @pl.when(i == 0)
        def _():
            dgq_ref[...] = dgq_part
            dgk_ref[...] = dgk_part
            dsk_ref[...] = dsinks

        @pl.when(i > 0)
        def _():
            before = pl.ds(pl.multiple_of((i - 1) * BLOCK, BLOCK), BLOCK)
            dk_ref[before, :] += dk_prev
            dv_ref[before, :] += dvp_ref[...]
            dgq_ref[...] += dgq_part
            dgk_ref[...] += dgk_part
            dsk_ref[...] += dsinks

    whole = lambda shape: pl.BlockSpec(shape, lambda i: (0, 0))
    return pl.pallas_call(
        body,
        name=name,
        grid=(nb,),
        in_specs=[sp["sinks"], sp["q"], sp["kc"], sp["kp"], sp["vc"], sp["vp"], sp["tq"], sp["tq"], sp["tkp"],
                  sp["tkp"], sp["gq"], sp["gk"], pl.BlockSpec((BLOCK, wq), lambda i: (i, 0))]
        + ([ANY] if after is not None else []),
        out_specs=[pl.BlockSpec((BLOCK, wq), lambda i: (i, 0)), whole((t, wk)), whole((t, wk)), whole((8, wq)),
                   whole((8, wk)), whole((8, LANES))],
        out_shape=[jax.ShapeDtypeStruct((t, wq), BF16), jax.ShapeDtypeStruct((t, wk), F32),
                   jax.ShapeDtypeStruct((t, wk), F32), jax.ShapeDtypeStruct((8, wq), F32),
                   jax.ShapeDtypeStruct((8, wk), F32), jax.ShapeDtypeStruct((8, LANES), F32)],
        scratch_shapes=[pltpu.VMEM((BLOCK, wq), F32), pltpu.VMEM((BLOCK, wk), F32), pltpu.VMEM((BLOCK, wk), F32),
                        pltpu.VMEM((BLOCK, wk), F32), pltpu.VMEM((BLOCK, wk), F32)],
        compiler_params=_params("arbitrary"),
    )(sinks, proj, proj, proj, proj, proj, cos_t, sin_t, cos_t, sin_t, gq, gk, dout,
      *(() if after is None else (after,)))


_GELU_K = math.sqrt(2.0 / math.pi)
_GELU_A = 0.044715


def _gelu(x):
    return 0.5 * x * (1.0 + jnp.tanh(_GELU_K * (x + _GELU_A * x * x * x)))


def _gelu_grad(x):
    th = jnp.tanh(_GELU_K * (x + _GELU_A * x * x * x))
    return 0.5 * (1.0 + th) + 0.5 * x * (1.0 - th * th) * (_GELU_K * (1.0 + 3.0 * _GELU_A * x * x))


def _group_ln(v):
    mu = jnp.mean(v, axis=1, keepdims=True)
    cen = v - mu
    rstd = lax.rsqrt(jnp.mean(cen * cen, axis=1, keepdims=True) + EPS)
    return cen * rstd, rstd


def _sgu_geometry(off_u, ws):
    cw = math.gcd(off_u, ws)
    return cw, ws // cw, off_u // cw, (off_u + ws) // cw


def _sgu_fwd(proj, ln_g, ln_b, w_s, bt, off_u, ws, name, after=None):
    t = proj.shape[0]
    nb = t // BLOCK
    cw, nc, ub, vb = _sgu_geometry(off_u, ws)
    gpc = cw // LANES
    ng = ws // LANES

    def body(u_ref, v_ref, g_ref, b_ref, w_ref, bt_ref, *rest):
        o_ref = rest[-1]
        jc = pl.program_id(0)
        row = lax.broadcasted_iota(jnp.int32, (BLOCK, BLOCK), 0)
        col = lax.broadcasted_iota(jnp.int32, (BLOCK, BLOCK), 1)
        lane_g = _lane((BLOCK, ng))
        for gi in range(gpc):
            sl = slice(LANES * gi, LANES * gi + LANES)
            xh, _ = _group_ln(_gelu(v_ref[:, sl]))
            vn = xh * g_ref[:, sl] + b_ref[:, sl]
            w = jnp.where(row >= col, w_ref[gi], 0.0).astype(BF16)
            bias = jnp.sum(jnp.where(lane_g == jc * gpc + gi, bt_ref[...], 0.0), axis=1, keepdims=True)
            s = _dot(w, vn.astype(BF16), NN) + bias
            o_ref[:, sl] = (_gelu(u_ref[:, sl]) * s).astype(o_ref.dtype)

    return pl.pallas_call(
        body,
        name=name,
        grid=(nc, nb),
        in_specs=[pl.BlockSpec((BLOCK, cw), lambda jc, i: (i, ub + jc)),
                  pl.BlockSpec((BLOCK, cw), lambda jc, i: (i, vb + jc)),
                  pl.BlockSpec((1, cw), lambda jc, i: (0, jc)),
                  pl.BlockSpec((1, cw), lambda jc, i: (0, jc)),
                  pl.BlockSpec((gpc, BLOCK, BLOCK), lambda jc, i: (jc, 0, 0)),
                  pl.BlockSpec((BLOCK, ng), lambda jc, i: (0, 0))] + ([ANY] if after is not None else []),
        out_specs=pl.BlockSpec((BLOCK, cw), lambda jc, i: (i, jc)),
        out_shape=jax.ShapeDtypeStruct((t, ws), BF16),
        compiler_params=_params("parallel", "parallel"),
    )(proj, proj, ln_g, ln_b, w_s, bt, *(() if after is None else (after,)))


def _sgu_bwd(proj, dout, ln_g, ln_b, w_s, bt, off_u, ws, name):
    t = proj.shape[0]
    nb = t // BLOCK
    cw, nc, ub, vb = _sgu_geometry(off_u, ws)
    gpc = cw // LANES
    ng = ws // LANES

    def body(u_ref, v_ref, g_ref, b_ref, w_ref, bt_ref, do_ref, du_ref, dv_ref, dg_ref, db_ref, dw_ref, dbs_ref,
             bacc_ref):
        jc = pl.program_id(0)
        i = pl.program_id(1)
        row = lax.broadcasted_iota(jnp.int32, (BLOCK, BLOCK), 0)
        col = lax.broadcasted_iota(jnp.int32, (BLOCK, BLOCK), 1)
        lane_g = _lane((BLOCK, ng))
        tri = row >= col

        @pl.when(i == 0)
        def _():
            dg_ref[...] = jnp.zeros_like(dg_ref)
            db_ref[...] = jnp.zeros_like(db_ref)
            dw_ref[...] = jnp.zeros_like(dw_ref)
            bacc_ref[...] = jnp.zeros_like(bacc_ref)

        for gi in range(gpc):
            sl = slice(LANES * gi, LANES * gi + LANES)
            u_raw, v_raw = u_ref[:, sl], v_ref[:, sl]
            xh, rstd = _group_ln(_gelu(v_raw))
            gam = g_ref[:, sl]
            vn = (xh * gam + b_ref[:, sl]).astype(BF16)
            w = jnp.where(tri, w_ref[gi], 0.0)
            bias = jnp.sum(jnp.where(lane_g == jc * gpc + gi, bt_ref[...], 0.0), axis=1, keepdims=True)
            s = _dot(w.astype(BF16), vn, NN) + bias
            dov = do_ref[:, sl]
            du_ref[:, sl] = (dov * s * _gelu_grad(u_raw)).astype(du_ref.dtype)
            ds = dov * _gelu(u_raw)
            ds16 = ds.astype(BF16)
            dw_ref[gi] += jnp.where(tri, _dot(ds16, vn, NT), 0.0)
            bacc_ref[gi] += ds
            dvn = _dot(w.T.astype(BF16), ds16, NN)
            dg_ref[:, sl] += jnp.broadcast_to(jnp.sum(dvn * xh, axis=0, keepdims=True), (8, LANES))
            db_ref[:, sl] += jnp.broadcast_to(jnp.sum(dvn, axis=0, keepdims=True), (8, LANES))
            dxh = dvn * gam
            dvg = rstd * (dxh - jnp.mean(dxh, axis=1, keepdims=True)
                          - xh * jnp.mean(dxh * xh, axis=1, keepdims=True))
            dv_ref[:, sl] = (dvg * _gelu_grad(v_raw)).astype(dv_ref.dtype)

        @pl.when(i == nb - 1)
        def _():
            for gi in range(gpc):
                dbs_ref[gi] = jnp.broadcast_to(jnp.sum(bacc_ref[gi].T, axis=0, keepdims=True), (8, LANES))

    blk = lambda base: pl.BlockSpec((BLOCK, cw), lambda jc, i: (i, base + jc))
    vec = pl.BlockSpec((1, cw), lambda jc, i: (0, jc))
    acc = pl.BlockSpec((8, cw), lambda jc, i: (0, jc))
    wsp = pl.BlockSpec((gpc, BLOCK, BLOCK), lambda jc, i: (jc, 0, 0))
    return pl.pallas_call(
        body,
        name=name,
        grid=(nc, nb),
        in_specs=[blk(ub), blk(vb), vec, vec, wsp, pl.BlockSpec((BLOCK, ng), lambda jc, i: (0, 0)), blk(0)],
        out_specs=[blk(0), blk(0), acc, acc, wsp, pl.BlockSpec((gpc, 8, LANES), lambda jc, i: (jc, 0, 0))],
        out_shape=[jax.ShapeDtypeStruct((t, ws), BF16), jax.ShapeDtypeStruct((t, ws), BF16),
                   jax.ShapeDtypeStruct((8, ws), F32), jax.ShapeDtypeStruct((8, ws), F32),
                   jax.ShapeDtypeStruct((ng, BLOCK, BLOCK), F32), jax.ShapeDtypeStruct((ng, 8, LANES), F32)],
        scratch_shapes=[pltpu.VMEM((gpc, BLOCK, BLOCK), F32)],
        compiler_params=_params("arbitrary", "arbitrary"),
    )(proj, proj, ln_g, ln_b, w_s, bt, dout)


def _sigmoid(x):
    return 1.0 / (1.0 + jnp.exp(-x))


def _merge_geometry(off_g, d):
    cw = math.gcd(off_g, d)
    return cw, d // cw, off_g // cw, (off_g + d) // cw


def _branches_fwd(attn, sgu, wab_t, wsb_t, proj, off_g, name):
    t = attn.shape[0]
    d = wab_t.shape[0]
    tn, _, ab, bb = _merge_geometry(off_g, d)
    tm = _divisor_tile(t, 1024, 128)

    def body(a1_ref, a2_ref, b1_ref, b2_ref, la_ref, lb_ref, bra_ref, brb_ref, o_ref):
        va = _dot(a1_ref[...], b1_ref[...], NT)
        vb = _dot(a2_ref[...], b2_ref[...], NT)
        bra_ref[...] = va
        brb_ref[...] = vb
        o_ref[...] = (_sigmoid(la_ref[...]) * va + _sigmoid(lb_ref[...]) * vb).astype(o_ref.dtype)

    rows = lambda w: pl.BlockSpec((tm, w), lambda i, j: (i, 0))
    wrow = lambda w: pl.BlockSpec((tn, w), lambda i, j: (j, 0))
    blk = lambda base: pl.BlockSpec((tm, tn), lambda i, j: (i, base + j))
    return pl.pallas_call(
        body,
        name=name,
        grid=(t // tm, d // tn),
        in_specs=[rows(attn.shape[1]), rows(sgu.shape[1]), wrow(wab_t.shape[1]), wrow(wsb_t.shape[1]), blk(ab),
                  blk(bb)],
        out_specs=[blk(0)] * 3,
        out_shape=[jax.ShapeDtypeStruct((t, d), F32), jax.ShapeDtypeStruct((t, d), F32),
                   jax.ShapeDtypeStruct((t, d), BF16)],
        compiler_params=_params("parallel", "parallel"),
    )(attn, sgu, wab_t, wsb_t, proj, proj)


def _branches_bwd(dx16, wo, br_a, br_b, proj, off_g, name, after=None):
    t, d = br_a.shape
    tn, _, ab, bb = _merge_geometry(off_g, d)
    tm = _divisor_tile(t, 1024, 128)
    k = dx16.shape[1]

    def body(a_ref, b_ref, bra_ref, brb_ref, la_ref, lb_ref, *rest):
        dab_ref, dl_ref = rest[-2:]
        dmv = _dot(a_ref[...], b_ref[...], NT)
        ga, gb = _sigmoid(la_ref[...]), _sigmoid(lb_ref[...])
        dab_ref[0] = (dmv * ga).astype(dab_ref.dtype)
        dab_ref[1] = (dmv * gb).astype(dab_ref.dtype)
        dl_ref[0] = (dmv * bra_ref[...] * ga * (1.0 - ga)).astype(dl_ref.dtype)
        dl_ref[1] = (dmv * brb_ref[...] * gb * (1.0 - gb)).astype(dl_ref.dtype)

    blk = lambda base: pl.BlockSpec((tm, tn), lambda i, j: (i, base + j))
    pair = pl.BlockSpec((2, tm, tn), lambda i, j: (0, i, j))
    return pl.pallas_call(
        body,
        name=name,
        grid=(t // tm, d // tn),
        in_specs=[pl.BlockSpec((tm, k), lambda i, j: (i, 0)), pl.BlockSpec((tn, k), lambda i, j: (j, 0)), blk(0),
                  blk(0), blk(ab), blk(bb)] + ([ANY] if after is not None else []),
        out_specs=[pair, pair],
        out_shape=[jax.ShapeDtypeStruct((2, t, d), BF16)] * 2,
        compiler_params=_params("parallel", "parallel"),
    )(dx16, wo, br_a, br_b, proj, proj, *(() if after is None else (after,)))


def _gate_up_fwd(h2, wgu_t, name, after=None):
    t, d = h2.shape
    f = wgu_t.shape[0] // 2
    tm = _divisor_tile(t, 1024, 128)
    tn = _divisor_tile(f, 512, 128)
    nb = f // tn

    def body(a_ref, bg_ref, bu_ref, *rest):
        gu_ref, act_ref = rest[-2:]
        av = a_ref[...]
        gv = _dot(av, bg_ref[...], NT)
        uv = _dot(av, bu_ref[...], NT)
        gu_ref[0] = gv
        gu_ref[1] = uv
        act_ref[...] = (gv * _sigmoid(gv) * uv).astype(act_ref.dtype)

    return pl.pallas_call(
        body,
        name=name,
        grid=(t // tm, nb),
        in_specs=[pl.BlockSpec((tm, d), lambda i, j: (i, 0)), pl.BlockSpec((tn, d), lambda i, j: (j, 0)),
                  pl.BlockSpec((tn, d), lambda i, j: (j + nb, 0))] + ([ANY] if after is not None else []),
        out_specs=[pl.BlockSpec((2, tm, tn), lambda i, j: (0, i, j)), pl.BlockSpec((tm, tn), lambda i, j: (i, j))],
        out_shape=[jax.ShapeDtypeStruct((2, t, f), F32), jax.ShapeDtypeStruct((t, f), BF16)],
        compiler_params=_params("parallel", "parallel"),
    )(h2, wgu_t, wgu_t, *(() if after is None else (after,)))


def _gate_up_bwd(dx16, wd, gu, name, after=None):
    t, d = dx16.shape
    f = wd.shape[0]
    tm = _divisor_tile(t, 1024, 128)
    tn = _divisor_tile(f, 512, 128)

    def body(a_ref, b_ref, gu_ref, *rest):
        o_ref = rest[-1]
        dav = _dot(a_ref[...], b_ref[...], NT)
        gv = gu_ref[0]
        sg = _sigmoid(gv)
        o_ref[0] = (dav * gu_ref[1] * (sg + gv * sg * (1.0 - sg))).astype(o_ref.dtype)
        o_ref[1] = (dav * gv * sg).astype(o_ref.dtype)

    pair = pl.BlockSpec((2, tm, tn), lambda i, j: (0, i, j))
    return pl.pallas_call(
        body,
        name=name,
        grid=(t // tm, f // tn),
        in_specs=[pl.BlockSpec((tm, d), lambda i, j: (i, 0)), pl.BlockSpec((tn, d), lambda i, j: (j, 0)), pair]
        + ([ANY] if after is not None else []),
        out_specs=pair,
        out_shape=jax.ShapeDtypeStruct((2, t, f), BF16),
        compiler_params=_params("parallel", "parallel"),
    )(dx16, wd, gu, *(() if after is None else (after,)))


def _loss_and_grad(y, target, name):
    t, d = y.shape
    tr = _divisor_tile(t, 256, 8)

    def body(y_ref, t_ref, l_ref, dy_ref, dy16_ref):
        i = pl.program_id(0)
        err = y_ref[...] - t_ref[...]
        dy_ref[...] = err * (1.0 / d)
        dy16_ref[...] = (err * (1.0 / d)).astype(dy16_ref.dtype)
        part = jnp.broadcast_to(0.5 * jnp.sum(err * err) * (1.0 / d), l_ref.shape)

        @pl.when(i == 0)
        def _():
            l_ref[...] = part

        @pl.when(i > 0)
        def _():
            l_ref[...] += part

    row = pl.BlockSpec((tr, d), lambda i: (i, 0))
    return pl.pallas_call(
        body,
        name=name,
        grid=(t // tr,),
        in_specs=[row, row],
        out_specs=[pl.BlockSpec((8, LANES), lambda i: (0, 0)), row, row],
        out_shape=[jax.ShapeDtypeStruct((8, LANES), F32), jax.ShapeDtypeStruct((t, d), F32),
                   jax.ShapeDtypeStruct((t, d), BF16)],
        compiler_params=_params("arbitrary"),
    )(y, target)


def _adam_math(w, g, m, v):
    m = ADAM_B1 * m + (1.0 - ADAM_B1) * g
    v = ADAM_B2 * v + (1.0 - ADAM_B2) * (g * g)
    m_hat = m / (1.0 - ADAM_B1 ** ADAM_STEP)
    v_hat = v / (1.0 - ADAM_B2 ** ADAM_STEP)
    delta = -ADAM_LR * (m_hat / (jnp.sqrt(v_hat) + ADAM_EPS) + ADAM_WD * w)
    return delta, m, v


def _row_tile(r, c, elems=512 * 1024):
    return _divisor_tile(r, max(8, elems // c // 8 * 8), 8)


def _adam(w, grads, m, v, chip, name, after=None):
    nl, r, c = w.shape
    tr = _row_tile(r, c, 256 * 1024)
    nb = r // tr
    counts = [len(terms) for terms, _ in grads]

    def body(chip_ref, *refs):
        w_ref, m_ref, v_ref = refs[:3]
        g_ref, d_ref, nm_ref, nv_ref = refs[-4:]
        layer = pl.program_id(0)
        g, at = None, 3
        for li, n in enumerate(counts):
            total = refs[at][...].astype(F32)
            for ref in refs[at + 1:at + n]:
                total = total + ref[...].astype(F32)
            g = total if g is None else jnp.where(layer == li, total, g)
            at += n
        g_ref[...] = g
        d_ref[...], nm_ref[...], nv_ref[...] = _adam_math(w_ref[...], g, m_ref[...], v_ref[...])

    def term_spec(li, p, by_owner):
        def index(l, i, chip_ref):
            rows = jnp.where(l < li, 0, jnp.where(l > li, nb - 1, i))
            return (p, chip_ref[0] if by_owner else 0, rows, 0)
        return pl.BlockSpec((None, None, tr, c), index)

    row = pl.BlockSpec((None, tr, c), lambda l, i, chip_ref: (l, i, 0))
    specs, arrays = [], []
    for li, (terms, p) in enumerate(grads):
        for term in terms:
            specs.append(term_spec(li, p, term.shape[1] == 4))
            arrays.append(term)
    return pl.pallas_call(
        body,
        name=name,
        grid_spec=pltpu.PrefetchScalarGridSpec(
            num_scalar_prefetch=1, grid=(nl, nb),
            in_specs=[row] * 3 + specs + ([ANY] if after is not None else []), out_specs=[row] * 4),
        out_shape=[jax.ShapeDtypeStruct((nl, r, c), F32)] * 4,
        compiler_params=_params("arbitrary", "arbitrary"),
    )(chip, w, m, v, *arrays, *(() if after is None else (after,)))


def _place_shard(parts, layer, dev, out_dtype, name, after=None):
    p = len(parts)
    _, r, c = parts[0].shape
    tr = _row_tile(r, c)

    def body(dev_ref, *refs):
        o_ref = refs[-1]
        x = refs[0][...]
        for pi in range(1, p):
            x = jnp.where(pl.program_id(0) == pi, refs[pi][...], x)
        o_ref[...] = x.astype(o_ref.dtype)

    return pl.pallas_call(
        body,
        name=name,
        grid_spec=pltpu.PrefetchScalarGridSpec(
            num_scalar_prefetch=1,
            grid=(p, r // tr),
            in_specs=[pl.BlockSpec((None, tr, c), lambda pi, i, dev_ref: (layer, i, 0))] * p
            + ([ANY] if after is not None else []),
            out_specs=pl.BlockSpec((None, None, tr, c), lambda pi, i, dev_ref: (pi, dev_ref[0], i, 0)),
        ),
        out_shape=jax.ShapeDtypeStruct((p, N_DEV, r, c), out_dtype),
        compiler_params=_params("parallel", "parallel"),
    )(dev, *parts, *(() if after is None else (after,)))


def _sum_sibling(g, land, core, name):
    p, _, _, r, c = g.shape
    tr = _row_tile(r, c, 1024 * 1024)

    def body(core_ref, g_ref, l_ref, o_ref):
        o_ref[...] = (g_ref[...].astype(F32) + l_ref[...].astype(F32)).astype(o_ref.dtype)

    return pl.pallas_call(
        body,
        name=name,
        grid_spec=pltpu.PrefetchScalarGridSpec(
            num_scalar_prefetch=1,
            grid=(p, 4, r // tr),
            in_specs=[pl.BlockSpec((None, None, None, tr, c), lambda pi, q, i, core_ref: (pi, q, core_ref[0], i, 0)),
                      pl.BlockSpec((None, None, None, tr, c), lambda pi, q, i, core_ref: (pi, q, 0, i, 0))],
            out_specs=pl.BlockSpec((None, None, tr, c), lambda pi, q, i, core_ref: (pi, q, i, 0)),
        ),
        out_shape=jax.ShapeDtypeStruct((p, 4, r, c), BF16),
        compiler_params=_params("parallel", "parallel", "parallel"),
    )(core, g, land)


def _sum_chips(s, lands, chip, name):
    p, _, r, c = s.shape
    tr = _row_tile(r, c)

    def body(chip_ref, s_ref, l0_ref, l1_ref, l2_ref, o_ref):
        total = s_ref[...].astype(F32) + l0_ref[...].astype(F32)
        o_ref[...] = total + l1_ref[...].astype(F32) + l2_ref[...].astype(F32)

    land_spec = pl.BlockSpec((None, None, tr, c), lambda pi, i, chip_ref: (pi, 0, i, 0))
    return pl.pallas_call(
        body,
        name=name,
        grid_spec=pltpu.PrefetchScalarGridSpec(
            num_scalar_prefetch=1,
            grid=(p, r // tr),
            in_specs=[pl.BlockSpec((None, None, tr, c), lambda pi, i, chip_ref: (pi, chip_ref[0], i, 0)),
                      land_spec, land_spec, land_spec],
            out_specs=pl.BlockSpec((None, tr, c), lambda pi, i, chip_ref: (pi, i, 0)),
        ),
        out_shape=jax.ShapeDtypeStruct((p, r, c), F32),
        compiler_params=_params("parallel", "parallel"),
    )(chip, s, *lands)


def _small_reduce_adam(gathered, w, m, v, name):
    _, r, c = gathered.shape
    tr = _row_tile(r, c)

    def body(p_ref, w_ref, m_ref, v_ref, g_ref, d_ref, nm_ref, nv_ref):
        g = p_ref[0]
        for j in range(1, N_DEV):
            g = g + p_ref[j]
        g_ref[...] = g
        d_ref[...], nm_ref[...], nv_ref[...] = _adam_math(w_ref[...], g, m_ref[...], v_ref[...])

    row = pl.BlockSpec((tr, c), lambda i: (i, 0))
    return pl.pallas_call(
        body,
        name=name,
        grid=(r // tr,),
        in_specs=[pl.BlockSpec((N_DEV, tr, c), lambda i: (0, i, 0)), row, row, row],
        out_specs=[row] * 4,
        out_shape=[jax.ShapeDtypeStruct((r, c), F32)] * 4,
        compiler_params=_params("parallel"),
    )(gathered, w, m, v)


def _place():
    return lax.axis_index("x"), lax.axis_index("y"), lax.axis_index("c")


HBM =pl.BlockSpec(memory_space=pltpu.HBM)
SEM = pl.BlockSpec(memory_space=pltpu.SEMAPHORE)
TOKEN = pl.BlockSpec(memory_space=pltpu.VMEM)
EFFECT = pltpu.SideEffectType.DATAFLOW_SIDE_EFFECTING


def _in_hbm(a):
    return pltpu.with_memory_space_constraint(a, pltpu.HBM)


_FLIPS = {"me": (0, 0, 0), "s": (0, 0, 1), "x": (1, 0, 0), "y": (0, 1, 0), "d": (1, 1, 0)}
GATHER_STAGES = (
    (("s", "me", "all"), ("x", "me", "all"), ("y", "me", "all")),
    (("s", "x", "all"), ("s", "y", "all"), ("y", "x", "first"), ("x", "y", "second")),
    (("s", "d", "all"),),
)


def _flipped(place, *names):
    out = list(place)
    for name in names:
        out = [1 - p if f else p for p, f in zip(out, _FLIPS[name])]
    return tuple(out)


def _block_part(ref, place, part):
    px, py, pc = place
    rows = ref.shape[2]
    span = {"all": pl.ds(0, rows), "first": pl.ds(0, rows // 2), "second": pl.ds(rows // 2, rows // 2)}[part]
    return ref.at[:, pl.ds(4 * px + 2 * py + pc, 1), span]


def _split_start(bufs, moves, name, after=None):
    n, nm = len(bufs), len(moves)
    extra = 0 if after is None else 1

    def body(*refs):
        ssem, rsem = refs[n + extra], refs[n + extra + 1]
        outs, token = refs[n + extra + 2:2 * n + extra + 2], refs[2 * n + extra + 2]
        me = _place()
        for a in range(n):
            for k, (to, owner, part) in enumerate(moves):
                piece = _block_part(outs[a], _flipped(me, owner), part)
                pltpu.make_async_remote_copy(
                    src_ref=piece, dst_ref=piece, send_sem=ssem.at[nm * a + k], recv_sem=rsem.at[nm * a + k],
                    device_id=_flipped(me, to), device_id_type=MESH).start()
        token[...] = jnp.zeros_like(token)

    outs = pl.pallas_call(
        body,
        name=name,
        in_specs=[HBM] * n + [ANY] * extra,
        out_specs=[SEM, SEM] + [HBM] * n + [TOKEN],
        out_shape=[pltpu.SemaphoreType.DMA((nm * n,))] * 2 + [pltpu.HBM(b.shape, b.dtype) for b in bufs]
        + [jax.ShapeDtypeStruct((8, LANES), F32)],
        input_output_aliases={i: 2 + i for i in range(n)},
        compiler_params=pltpu.CompilerParams(has_side_effects=EFFECT),
    )(*[_in_hbm(b) for b in bufs], *(() if after is None else (after,)))
    return outs[0], outs[1], list(outs[2:2 + n]), outs[-1]


def _split_wait(send_sems, recv_sems, bufs, moves, after, name):
    n, nm = len(bufs), len(moves)

    def body(*refs):
        ins, ssem, rsem = refs[:n], refs[n], refs[n + 1]
        me = _place()
        for a in range(n):
            for k, (to, owner, part) in enumerate(moves):
                landed = _block_part(ins[a], _flipped(me, owner, to), part)
                cp = pltpu.make_async_remote_copy(
                    src_ref=landed, dst_ref=landed, send_sem=ssem.at[nm * a + k], recv_sem=rsem.at[nm * a + k],
                    device_id=_flipped(me, to), device_id_type=MESH)
                cp.wait_send()
                cp.wait_recv()

    return pl.pallas_call(
        body,
        name=name,
        in_specs=[HBM] * n + [SEM, SEM, ANY],
        out_specs=[HBM] * n,
        out_shape=[pltpu.HBM(b.shape, b.dtype) for b in bufs],
        input_output_aliases={i: i for i in range(n)},
        compiler_params=pltpu.CompilerParams(has_side_effects=EFFECT),
    )(*bufs, send_sems, recv_sems, after)


def _chips_start(sums, name):
    n = len(sums)

    def body(*refs):
        ssem, rsem = refs[4 * n], refs[4 * n + 1]
        src, land = refs[4 * n + 2:5 * n + 2], refs[5 * n + 2:8 * n + 2]
        token = refs[8 * n + 2]
        x, y, c = _place()
        chips = [(1 - x, y), (x, 1 - y), (1 - x, 1 - y)]
        for a in range(n):
            for k, (px, py) in enumerate(chips):
                pltpu.make_async_remote_copy(
                    src_ref=src[a].at[:, pl.ds(2 * px + py, 1)], dst_ref=land[3 * a + k], send_sem=ssem.at[3 * a + k],
                    recv_sem=rsem.at[3 * a + k], device_id=(px, py, c), device_id_type=MESH).start()
        token[...] = jnp.zeros_like(token)

    lands = []
    for s in sums:
        lands += [lax.empty((s.shape[0], 1) + s.shape[2:], s.dtype) for _ in range(3)]
    outs = pl.pallas_call(
        body,
        name=name,
        in_specs=[HBM] * (4 * n),
        out_specs=[SEM, SEM] + [HBM] * (4 * n) + [TOKEN],
        out_shape=[pltpu.SemaphoreType.DMA((3 * n,))] * 2 + [pltpu.HBM(b.shape, b.dtype) for b in list(sums) + lands]
        + [jax.ShapeDtypeStruct((8, LANES), F32)],
        input_output_aliases={i: 2 + i for i in range(4 * n)},
        compiler_params=pltpu.CompilerParams(has_side_effects=EFFECT),
    )(*[_in_hbm(b) for b in list(sums) + lands])
    return outs[0], outs[1], list(outs[2:2 + n]), list(outs[2 + n:2 + 4 * n]), outs[-1]


def _chips_wait(send_sems, recv_sems, sums, lands, after, name):
    n = len(sums)

    def body(*refs):
        src, land = refs[:n], refs[n:4 * n]
        ssem, rsem = refs[4 * n], refs[4 * n + 1]
        x, y, c = _place()
        chips = [(1 - x, y), (x, 1 - y), (1 - x, 1 - y)]
        for a in range(n):
            for k, (px, py) in enumerate(chips):
                cp = pltpu.make_async_remote_copy(
                    src_ref=src[a].at[:, pl.ds(2 * px + py, 1)], dst_ref=land[3 * a + k], send_sem=ssem.at[3 * a + k],
                    recv_sem=rsem.at[3 * a + k], device_id=(px, py, c), device_id_type=MESH)
                cp.wait_send()
                cp.wait_recv()

    both = list(sums) + list(lands)
    outs = pl.pallas_call(
        body,
        name=name,
        in_specs=[HBM] * (4 * n) + [SEM, SEM, ANY],
        out_specs=[HBM] * (4 * n),
        out_shape=[pltpu.HBM(b.shape, b.dtype) for b in both],
        input_output_aliases={i: i for i in range(4 * n)},
        compiler_params=pltpu.CompilerParams(has_side_effects=EFFECT),
    )(*both, send_sems, recv_sems, after)
    return list(outs[:n]), [list(outs[n + 3 * a:n + 3 * a + 3]) for a in range(n)]


def _sibling_start(grads, name):
    n = len(grads)

    def body(*refs):
        ssem, rsem = refs[2 * n], refs[2 * n + 1]
        src, land = refs[2 * n + 2:3 * n + 2], refs[3 * n + 2:4 * n + 2]
        token = refs[4 * n + 2]
        x, y, c = _place()
        for a in range(n):
            pltpu.make_async_remote_copy(
                src_ref=src[a].at[:, :, pl.ds(1 - c, 1)], dst_ref=land[a], send_sem=ssem.at[a], recv_sem=rsem.at[a],
                device_id=(x, y, 1 - c), device_id_type=MESH).start()
        token[...] = jnp.zeros_like(token)

    lands = [lax.empty(g.shape[:2] + (1,) + g.shape[3:], g.dtype) for g in grads]
    both = list(grads) + lands
    outs = pl.pallas_call(
        body,
        name=name,
        in_specs=[HBM] * (2 * n),
        out_specs=[SEM, SEM] + [HBM] * (2 * n) + [TOKEN],
        out_shape=[pltpu.SemaphoreType.DMA((n,))] * 2 + [pltpu.HBM(b.shape, b.dtype) for b in both]
        + [jax.ShapeDtypeStruct((8, LANES), F32)],
        input_output_aliases={i: 2 + i for i in range(2 * n)},
        compiler_params=pltpu.CompilerParams(has_side_effects=EFFECT),
    )(*[_in_hbm(b) for b in both])
    return outs[0], outs[1], list(outs[2:2 + n]), list(outs[2 + n:2 + 2 * n]), outs[-1]


def _sibling_wait(send_sems, recv_sems, grads, lands, after, name):
    n = len(grads)

    def body(*refs):
        src, land = refs[:n], refs[n:2 * n]
        ssem, rsem = refs[2 * n], refs[2 * n + 1]
        x, y, c = _place()
        for a in range(n):
            cp = pltpu.make_async_remote_copy(
                src_ref=src[a].at[:, :, pl.ds(1 - c, 1)], dst_ref=land[a], send_sem=ssem.at[a], recv_sem=rsem.at[a],
                device_id=(x, y, 1 - c), device_id_type=MESH)
            cp.wait_send()
            cp.wait_recv()

    both = list(grads) + list(lands)
    outs = pl.pallas_call(
        body,
        name=name,
        in_specs=[HBM] * (2 * n) + [SEM, SEM, ANY],
        out_specs=[HBM] * (2 * n),
        out_shape=[pltpu.HBM(b.shape, b.dtype) for b in both],
        input_output_aliases={i: i for i in range(2 * n)},
        compiler_params=pltpu.CompilerParams(has_side_effects=EFFECT),
    )(*both, send_sems, recv_sems, after)
    return list(outs[:n]), list(outs[n:])


_SMALL = ("mix_norm", "q_norm", "k_norm", "sinks", "sgu_ln_g", "sgu_ln_b", "w_spatial", "b_spatial", "ffn_norm")


def _pack_rows(a):
    flat = a.reshape(-1)
    pad = (-flat.shape[0]) % LANES
    if pad:
        flat = jnp.pad(flat, (0, pad))
    return flat.reshape(-1, LANES)


def _pack(values):
    rows = jnp.concatenate([_pack_rows(values[k]) for k in _SMALL], axis=0)
    pad = (-rows.shape[0]) % 8
    if pad:
        rows = jnp.pad(rows, ((0, pad), (0, 0)))
    return rows


def _unpack(rows, like):
    out, at = {}, 0
    for k in _SMALL:
        size = like[k].size
        nrows = -(-size // LANES)
        out[k] = rows[at:at + nrows].reshape(-1)[:size].reshape(like[k].shape)
        at += nrows
    return out


def _rope_tables(t, wq):
    pos = jnp.arange(t, dtype=F32)
    inv_freq = jnp.power(ROPE_THETA, -jnp.arange(0, HEAD_DIM, 2, dtype=F32) / HEAD_DIM)
    ang = pos[:, None] * inv_freq[None, :]
    cos, sin = jnp.cos(ang), jnp.sin(ang)
    reps = wq // HEAD_DIM
    return (jnp.tile(jnp.concatenate([cos, cos], axis=1), (1, reps)),
            jnp.tile(jnp.concatenate([-sin, sin], axis=1), (1, reps)))


def kernel(x, mix_norm, w_in, q_norm, k_norm, sinks, sgu_ln_g, sgu_ln_b, w_spatial, b_spatial, w_attn_branch, w_sgu_branch, w_out, ffn_norm, w_gate, w_up, w_down, loss_target, m_mix_norm, m_w_in, m_q_norm, m_k_norm, m_sinks, m_sgu_ln_g, m_sgu_ln_b, m_w_spatial, m_b_spatial, m_w_attn_branch, m_w_sgu_branch, m_w_out, m_ffn_norm, m_w_gate, m_w_up, m_w_down, v_mix_norm, v_w_in, v_q_norm, v_k_norm, v_sinks, v_sgu_ln_g, v_sgu_ln_b, v_w_spatial, v_b_spatial, v_w_attn_branch, v_w_sgu_branch, v_w_out, v_ffn_norm, v_w_gate, v_w_up, v_w_down):
    names = ("mix_norm", "w_in", "q_norm", "k_norm", "sinks", "sgu_ln_g", "sgu_ln_b", "w_spatial", "b_spatial",
             "w_attn_branch", "w_sgu_branch", "w_out", "ffn_norm", "w_gate", "w_up", "w_down")
    weights = dict(zip(names, (mix_norm, w_in, q_norm, k_norm, sinks, sgu_ln_g, sgu_ln_b, w_spatial, b_spatial,
                               w_attn_branch, w_sgu_branch, w_out, ffn_norm, w_gate, w_up, w_down)))
    mom1 = dict(zip(names, (m_mix_norm, m_w_in, m_q_norm, m_k_norm, m_sinks, m_sgu_ln_g, m_sgu_ln_b, m_w_spatial,
                            m_b_spatial, m_w_attn_branch, m_w_sgu_branch, m_w_out, m_ffn_norm, m_w_gate, m_w_up,
                            m_w_down)))
    mom2 = dict(zip(names, (v_mix_norm, v_w_in, v_q_norm, v_k_norm, v_sinks, v_sgu_ln_g, v_sgu_ln_b, v_w_spatial,
                            v_b_spatial, v_w_attn_branch, v_w_sgu_branch, v_w_out, v_ffn_norm, v_w_gate, v_w_up,
                            v_w_down)))
    depth = w_in.shape[0]
    _, t, d = x.shape
    n_q_heads = sinks.shape[1]
    wq = n_q_heads * HEAD_DIM
    wk = wq // Q_PER_KV
    ws = sgu_ln_g.shape[1]
    ng = ws // LANES
    off_u = wq + 2 * wk
    off_g = off_u + 2 * ws
    tables = _rope_tables(t, wq)
    px, py, pc = _place()
    core = pc.astype(jnp.int32)[None]
    chip = (2 * px + py).astype(jnp.int32)[None]
    dev = (4 * px + 2 * py + pc).astype(jnp.int32)[None]

    layers = range(depth)
    chunks = ((0,), (1, 2, 3), (4,), (5,))
    sources = [[jnp.swapaxes(w_in, 1, 2)], [jnp.swapaxes(w_attn_branch, 1, 2)], [jnp.swapaxes(w_sgu_branch, 1, 2)],
               [w_out], [jnp.swapaxes(w_gate, 1, 2), jnp.swapaxes(w_up, 1, 2)], [w_down]]
    stream = [(l, ci) for l in layers for ci in range(len(chunks))]
    placed, state, token = {}, {}, None

    def send(key, after):
        state[key] = _split_start(placed[key], GATHER_STAGES[0], "gather_send_%d_%d" % key, after)
        return state[key][3]

    def advance(key, after, stage):
        send_sems, recv_sems, bufs, _ = state[key]
        bufs = _split_wait(send_sems, recv_sems, bufs, GATHER_STAGES[stage - 1], after, "gather_wait%d_%d_%d" % (stage, *key))
        state[key] = _split_start(bufs, GATHER_STAGES[stage], "gather_pass%d_%d_%d" % (stage, *key))
        return state[key][3]

    def relay(key, after):
        tok = advance(key, after, 1)
        at = stream.index(key)
        for later in stream[at + 2:at + 3] if at else stream[1:3]:
            tok = send(later, tok)
        return tok

    def ready(key, after):
        send_sems, recv_sems, bufs, _ = state.pop(key)
        bufs = _split_wait(send_sems, recv_sems, bufs, GATHER_STAGES[2], after, "gather_wait3_%d_%d" % key)
        return [f.reshape(f.shape[0] * f.shape[1] * f.shape[2], f.shape[3]) for f in bufs]

    for key in stream:
        l, ci = key
        placed[key] = [_place_shard(sources[a], l, dev, BF16, f"place_shard_{l}_{a}",
                                    after=token if a == chunks[ci][0] else None) for a in chunks[ci]]
        token = send(key, None) if key == stream[0] else placed[key][-1]

    saved = []
    xl = x[0]
    going = relay((0, 0), token)
    going = advance((0, 0), going, 2)
    for l in layers:
        gq = jnp.tile(q_norm[l], n_q_heads)[None]
        gk = jnp.tile(k_norm[l], n_q_heads // Q_PER_KV)[None]
        bt = b_spatial[l].T
        h = _rmsnorm_fwd(xl, mix_norm[l][None], f"mix_norm_fwd_{l}", after=going)
        (win_t,) = ready((l, 0), h)
        proj = _mm(h, win_t, "nt", F32, f"in_proj_{l}")
        going = relay((l, 1), proj)
        attn = _attn_fwd(proj, tables, gq, gk, sinks[l], wq, wk, f"attn_fwd_{l}", after=going)
        going = advance((l, 1), attn, 2)
        sgu = _sgu_fwd(proj, sgu_ln_g[l][None], sgu_ln_b[l][None], w_spatial[l], bt, off_u, ws, f"sgu_fwd_{l}",
                       after=going)
        wab_t, wsb_t, wo = ready((l, 1), sgu)
        br_a, br_b, merged = _branches_fwd(attn, sgu, wab_t, wsb_t, proj, off_g, f"branches_{l}")
        going = relay((l, 2), merged)
        x1 = _mm(merged, wo, "nn", F32, f"out_proj_{l}", residual=xl, after=going)
        going = advance((l, 2), x1, 2)
        h2 = _rmsnorm_fwd(x1, ffn_norm[l][None], f"ffn_norm_fwd_{l}", after=going)
        (wgu_t,) = ready((l, 2), h2)
        going = relay((l, 3), h2)
        gu, act = _gate_up_fwd(h2, wgu_t, f"gate_up_{l}", after=going)
        going = advance((l, 3), act, 2)
        if l + 1 < depth:
            going = relay((l + 1, 0), going)
        (wd,) = ready((l, 3), going)
        x2 = _mm(act, wd, "nn", F32, f"down_proj_{l}", residual=x1)
        if l + 1 < depth:
            going = advance((l + 1, 0), x2, 2)
        saved.append(dict(x0=xl, h=h, proj=proj, attn=attn, sgu=sgu, br_a=br_a, br_b=br_b, merged=merged, x1=x1,
                          h2=h2, gu=gu, act=act, gq=gq, gk=gk, bt=bt, win_t=win_t, wab_t=wab_t, wsb_t=wsb_t, wo=wo,
                          wgu_t=wgu_t, wd=wd))
        xl = x2

    loss_part, dx, dx16 = _loss_and_grad(xl, loss_target[0], "loss")
    loss = lax.psum(loss_part[0, 0], ("x", "y", "c"))

    def sibling_start(grads, tag):
        shaped = []
        for g, p in grads:
            rows, c = g.shape
            shaped.append(g.reshape(p, 4, 2, rows // (8 * p), c))
        send_sems, recv_sems, shaped, lands, tok = _sibling_start(shaped, f"rs_sibling_start_{tag}")
        return (send_sems, recv_sems, shaped, lands, tag), tok

    def chips_start(state, after):
        send_sems, recv_sems, shaped, lands, tag = state
        shaped, lands = _sibling_wait(send_sems, recv_sems, shaped, lands, after, f"rs_sibling_wait_{tag}")
        sums = [_sum_sibling(g, o, core, f"rs_add_sibling_{tag}_{a}") for a, (g, o) in enumerate(zip(shaped, lands))]
        send_sems, recv_sems, sums, lands, tok = _chips_start(sums, f"rs_chips_start_{tag}")
        return (send_sems, recv_sems, sums, lands, tag), tok

    def scatter_finish(state, after):
        send_sems, recv_sems, sums, lands, tag = state
        sums, lands = _chips_wait(send_sems, recv_sems, sums, lands, after, f"rs_chips_wait_{tag}")
        return [[s] + o for s, o in zip(sums, lands)]

    in_flight = [dict() for _ in layers]
    small_grads = [None] * depth
    tok, swap_in = None, None
    for l in reversed(layers):
        s = saved[l]
        dgu = _gate_up_bwd(dx16, s["wd"], s["gu"], f"d_gate_up_{l}", after=tok)
        if swap_in is not None:
            in_flight[l + 1]["in"], tok = chips_start(swap_in, dgu)
        g_wd = _mm(s["act"], dx16, "tn", BF16, f"g_w_down_{l}", after=tok)
        swap, tok_s = sibling_start([(g_wd, 1)], f"{l}_down")
        dh2 = _mm(dgu, s["wgu_t"], "nn", F32, f"d_h2_{l}", after=tok_s)
        in_flight[l]["down"], tok = chips_start(swap, dh2)
        g_wgu_t = _mm(dgu, s["h2"], "tn", BF16, f"g_w_gate_up_{l}", after=tok)
        swap, tok_s = sibling_start([(g_wgu_t, 2)], f"{l}_gate_up")
        dx1, dx1_16, g_ffn = _rmsnorm_bwd(s["x1"], ffn_norm[l][None], dh2, dx, f"ffn_norm_bwd_{l}", after=tok_s)
        d_ab, d_logits = _branches_bwd(dx1_16, s["wo"], s["br_a"], s["br_b"], s["proj"], off_g, f"d_branches_{l}")
        in_flight[l]["gate_up"], tok = chips_start(swap, d_ab)
        g_wo = _mm(s["merged"], dx1_16, "tn", BF16, f"g_w_out_{l}", after=tok)
        d_a, d_b = d_ab[0], d_ab[1]
        dattn = _mm(d_a, s["wab_t"], "nn", F32, f"d_attn_{l}", after=g_wo)
        g_wab_t = _mm(d_a, s["attn"], "tn", BF16, f"g_w_attn_branch_{l}")
        dsgu = _mm(d_b, s["wsb_t"], "nn", F32, f"d_sgu_{l}")
        g_wsb_t = _mm(d_b, s["sgu"], "tn", BF16, f"g_w_sgu_branch_{l}")
        swap, tok_s = sibling_start([(g_wab_t, 1), (g_wsb_t, 1), (g_wo, 1)], f"{l}_mix")
        dq, dk, dv, g_gq, g_gk, g_sinks = _attn_bwd(s["proj"], dattn, tables, s["gq"], s["gk"], sinks[l], wq, wk,
                                                    f"attn_bwd_{l}", after=tok_s)
        du, dvv, g_lng, g_lnb, g_ws, g_bs = _sgu_bwd(s["proj"], dsgu, sgu_ln_g[l][None], sgu_ln_b[l][None],
                                                     w_spatial[l], s["bt"], off_u, ws, f"sgu_bwd_{l}")
        dproj = jnp.concatenate([dq, dk.astype(BF16), dv.astype(BF16), du, dvv, d_logits[0], d_logits[1]], axis=1)
        dh = _mm(dproj, s["win_t"], "nn", F32, f"d_h_{l}")
        in_flight[l]["mix"], tok = chips_start(swap, dh)
        g_win_t = _mm(dproj, s["h"], "tn", BF16, f"g_w_in_{l}", after=tok)
        swap_in, tok = sibling_start([(g_win_t, 1)], f"{l}_in")
        dx, dx16, g_mix = _rmsnorm_bwd(s["x0"], mix_norm[l][None], dh, dx1, f"mix_norm_bwd_{l}", after=tok)
        small_grads[l] = dict(
            mix_norm=g_mix[0], q_norm=g_gq[0].reshape(n_q_heads, HEAD_DIM).sum(0),
            k_norm=g_gk[0].reshape(n_q_heads // Q_PER_KV, HEAD_DIM).sum(0), sinks=g_sinks[0, :n_q_heads],
            sgu_ln_g=g_lng[0], sgu_ln_b=g_lnb[0], w_spatial=g_ws, b_spatial=g_bs[:, 0, :], ffn_norm=g_ffn[0])
    grad_x = dx[None]

    result = {key: {} for key in ("grad", "delta", "m", "v")}
    layer_like = {k: weights[k][0] for k in _SMALL}
    packed_g = jnp.concatenate([_pack(small_grads[l]) for l in layers], axis=0)
    rows_per_layer = packed_g.shape[0] // depth
    small_buf = _place_shard([packed_g[None]], 0, dev, F32, "place_small_grads", after=tok)
    send_sems, recv_sems, small_bufs, tok = _split_start([small_buf], GATHER_STAGES[0], "gather_send_small")
    in_flight[0]["in"], tok = chips_start(swap_in, tok)

    def update(k, grads, transposed, after):
        view = (lambda a: jnp.swapaxes(a, 1, 2)) if transposed else (lambda a: a)
        outs = _adam(view(weights[k]), grads, view(mom1[k]), view(mom2[k]), chip, f"adam_{k}", after=after)
        for key, val in zip(("grad", "delta", "m", "v"), outs):
            result[key][k] = view(val)
        return outs[3]

    def plain(terms, tag):
        s, lands = terms[0], terms[1:]
        g = _sum_chips(s, lands, chip, f"rs_add_chips_{tag}")
        return [jnp.swapaxes(g, 1, 2)[:, None]]

    down = [scatter_finish(in_flight[l]["down"], tok) for l in reversed(layers)][::-1]
    tok = update("w_down", [(down[l][0], 0) for l in layers], False, None)
    gate_up = [scatter_finish(in_flight[l]["gate_up"], tok) for l in reversed(layers)][::-1]
    tok = update("w_gate", [(gate_up[l][0], 0) for l in layers], True, None)
    tok = update("w_up", [(gate_up[l][0], 1) for l in layers], True, tok)
    mix = [scatter_finish(in_flight[l]["mix"], tok) for l in reversed(layers)][::-1]
    tok = update("w_out", [(mix[l][2], 0) for l in layers], False, None)
    tok = update("w_attn_branch", [(plain(mix[l][0], f"{l}_attn_branch"), 0) for l in layers], False, tok)
    tok = update("w_sgu_branch", [(plain(mix[l][1], f"{l}_sgu_branch"), 0) for l in layers], False, tok)

    for stage in (1, 2):
        small_bufs = _split_wait(send_sems, recv_sems, small_bufs, GATHER_STAGES[stage - 1], tok,
                                 f"gather_wait{stage}_small")
        send_sems, recv_sems, small_bufs, tok = _split_start(small_bufs, GATHER_STAGES[stage],
                                                             f"gather_pass{stage}_small")
    packed = [jnp.concatenate([_pack({k: src[k][l] for k in _SMALL}) for l in layers], axis=0)
              for src in (weights, mom1, mom2)]
    (gathered_small,) = _split_wait(send_sems, recv_sems, small_bufs, GATHER_STAGES[2], packed[0],
                                    "gather_wait3_small")
    small = _small_reduce_adam(gathered_small[0], *packed, "small_reduce_adam")
    for key, rows in zip(("grad", "delta", "m", "v"), small):
        per_layer = [_unpack(rows[l * rows_per_layer:(l + 1) * rows_per_layer], layer_like) for l in layers]
        for k in _SMALL:
            result[key][k] = jnp.stack([per_layer[l][k] for l in layers])

    last = [scatter_finish(in_flight[l]["in"], small[0]) for l in reversed(layers)][::-1]
    update("w_in", [(last[l][0], 0) for l in layers], True, None)

    return (loss, grad_x, *[result["grad"][k] for k in names], *[result["delta"][k] for k in names],
            *[result["m"][k] for k in names], *[result["v"][k] for k in names])
```

```python
import functools
import math

import jax
import jax.numpy as jnp
from jax import lax
from jax.experimental import pallas as pl
from jax.experimental.pallas import tpu as pltpu

F32 = jnp.float32
BF16 = jnp.bfloat16
MESH = pl.DeviceIdType.MESH
ANY = pl.BlockSpec(memory_space=pl.ANY)

N_DEV = 8
HEAD_DIM = 64
Q_PER_KV = 4
BLOCK = 128
LANES = 128
ROPE_THETA = 10000.0
EPS = 1e-6
ADAM_LR = 0.001
ADAM_B1 = 0.9
ADAM_B2 = 0.999
ADAM_EPS = 1e-08
ADAM_WD = 0.01
ADAM_STEP = 10
NEG = -1e30
VMEM_LIMIT_BYTES = 56 * 1024 * 1024

NN = ((1,), (0,))
NT = ((1,), (1,))
TN = ((0,), (0,))


def _dot(a, b, dims):
    return lax.dot_general(a, b, (dims, ((), ())), preferred_element_type=F32)


def _params(*sem):
    return pltpu.CompilerParams(dimension_semantics=sem, vmem_limit_bytes=VMEM_LIMIT_BYTES)


def _divisor_tile(n, limit, unit):
    if n <= limit:
        return n
    best = unit
    for t in range(unit, limit + 1, unit):
        if n % t == 0:
            best = t
    assert n % best == 0, (n, limit, unit)
    return best


def _mm(a, b, mode, out_dtype, name, residual=None, after=None):
    parts = a.shape[0] if a.ndim == 3 else 1
    a2 = a.shape[-2:]
    if mode == "nn":
        (m, kp), (k2, n) = a2, b.shape
        k, mp = kp * parts, m
    elif mode == "nt":
        (m, kp), (n, k2) = a2, b.shape
        k, mp = kp * parts, m
    else:
        (k, mp), (k2, n) = a2, b.shape
        m, kp = mp * parts, k
    assert k == k2, (name, a.shape, b.shape)
    tk = _divisor_tile(kp, 2816, 128)
    nk = k // tk
    tm = _divisor_tile(mp, 512 if mode == "tn" else 1024, 128)
    tn = _divisor_tile(n, 2048 if mode == "tn" else 1024, 128)
    kpb, mpb = kp // tk, mp // tm
    dims = {"nn": NN, "nt": NT, "tn": TN}[mode]
    lead = (None,) if a.ndim == 3 else ()
    if mode == "tn":
        a_index = lambda i, j, kk: (i // mpb, kk, i % mpb) if lead else (kk, i)
        a_spec = pl.BlockSpec(lead + (tk, tm), a_index)
    else:
        a_index = lambda i, j, kk: (kk // kpb, i, kk % kpb) if lead else (i, kk)
        a_spec = pl.BlockSpec(lead + (tm, tk), a_index)
    if mode == "nt":
        b_spec = pl.BlockSpec((tn, tk), lambda i, j, kk: (j, kk))
    else:
        b_spec = pl.BlockSpec((tk, tn), lambda i, j, kk: (kk, j))
    o_spec = pl.BlockSpec((tm, tn), lambda i, j, kk: (i, j))
    has_res = residual is not None

    def body(*refs):
        a_ref, b_ref = refs[:2]
        r_ref = refs[2] if has_res else None
        o_ref, acc_ref = refs[-2:]
        kk = pl.program_id(2)
        p = _dot(a_ref[...], b_ref[...], dims)

        def finish(total):
            if has_res:
                total = total + r_ref[...]
            o_ref[...] = total.astype(o_ref.dtype)

        if nk == 1:
            finish(p)
        else:
            @pl.when(kk == 0)
            def _():
                acc_ref[...] = p

            @pl.when(jnp.logical_and(kk > 0, kk < nk - 1))
            def _():
                acc_ref[...] += p

            @pl.when(kk == nk - 1)
            def _():
                finish(acc_ref[...] + p)

    in_specs = [a_spec, b_spec] + ([o_spec] if has_res else []) + ([ANY] if after is not None else [])
    args = (a, b) + ((residual,) if has_res else ()) + ((after,) if after is not None else ())
    acc_shape = (tm, tn) if nk > 1 else (8, LANES)
    return pl.pallas_call(
        body,
        name=name,
        grid=(m // tm, n // tn, nk),
        in_specs=in_specs,
        out_specs=o_spec,
        out_shape=jax.ShapeDtypeStruct((m, n), out_dtype),
        scratch_shapes=[pltpu.VMEM(acc_shape, F32)],
        compiler_params=_params("parallel", "parallel", "arbitrary"),
    )(*args)


def _rmsnorm_fwd(x, g, name, after=None):
    t, d = x.shape
    tr = _divisor_tile(t, 256, 8)

    def body(x_ref, g_ref, *rest):
        h_ref = rest[-1]
        xv = x_ref[...]
        rstd = lax.rsqrt(jnp.mean(xv * xv, axis=-1, keepdims=True) + EPS)
        h_ref[...] = (xv * rstd * g_ref[...]).astype(h_ref.dtype)

    return pl.pallas_call(
        body,
        name=name,
        grid=(t // tr,),
        in_specs=[pl.BlockSpec((tr, d), lambda i: (i, 0)), pl.BlockSpec((1, d), lambda i: (0, 0))]
        + ([ANY] if after is not None else []),
        out_specs=pl.BlockSpec((tr, d), lambda i: (i, 0)),
        out_shape=jax.ShapeDtypeStruct((t, d), BF16),
        compiler_params=_params("parallel"),
    )(x, g, *(() if after is None else (after,)))


def _rmsnorm_bwd(x, g, dh, dres, name, after=None):
    t, d = x.shape
    tr = _divisor_tile(t, 256, 8)

    def body(x_ref, g_ref, dh_ref, dres_ref, *rest):
        dx_ref, dx16_ref, dg_ref = rest[-3:]
        i = pl.program_id(0)
        xv = x_ref[...]
        rstd = lax.rsqrt(jnp.mean(xv * xv, axis=-1, keepdims=True) + EPS)
        xh = xv * rstd
        dhv = dh_ref[...]
        dxh = dhv * g_ref[...]
        dx = dres_ref[...] + rstd * (dxh - xh * jnp.mean(dxh * xh, axis=-1, keepdims=True))
        dx_ref[...] = dx
        dx16_ref[...] = dx.astype(dx16_ref.dtype)
        part = jnp.broadcast_to(jnp.sum(dhv * xh, axis=0, keepdims=True), dg_ref.shape)

        @pl.when(i == 0)
        def _():
            dg_ref[...] = part

        @pl.when(i > 0)
        def _():
            dg_ref[...] += part

    row = pl.BlockSpec((tr, d), lambda i: (i, 0))
    return pl.pallas_call(
        body,
        name=name,
        grid=(t // tr,),
        in_specs=[row, pl.BlockSpec((1, d), lambda i: (0, 0)), row, row] + ([ANY] if after is not None else []),
        out_specs=[row, row, pl.BlockSpec((8, d), lambda i: (0, 0))],
        out_shape=[jax.ShapeDtypeStruct((t, d), F32), jax.ShapeDtypeStruct((t, d), BF16),
                   jax.ShapeDtypeStruct((8, d), F32)],
        compiler_params=_params("arbitrary"),
    )(x, g, dh, dres, *(() if after is None else (after,)))


def _lane(shape):
    return lax.broadcasted_iota(jnp.int32, shape, 1)


def _group_sum64(s):
    row = lax.broadcasted_iota(jnp.int32, (LANES, LANES), 0)
    col = lax.broadcasted_iota(jnp.int32, (LANES, LANES), 1)
    ones = jnp.where((row >= HEAD_DIM) == (col >= HEAD_DIM), 1.0, 0.0).astype(BF16)
    out = []
    for t in range(s.shape[1] // LANES):
        piece = s[:, LANES * t:LANES * t + LANES]
        hi = piece.astype(BF16)
        lo = (piece - hi.astype(F32)).astype(BF16)
        out.append(_dot(hi, ones, NN) + _dot(lo, ones, NN))
    return out[0] if len(out) == 1 else jnp.concatenate(out, axis=1)


def _swap32(x):
    w = x.shape[1]
    return jnp.where((_lane(x.shape) & 32) == 0, pltpu.roll(x, w - 32, axis=1), pltpu.roll(x, 32, axis=1))


def _rope(x, c, s):
    return x * c + _swap32(x) * s


def _rope_t(dy, c, s):
    return dy * c + _swap32(dy * s)


def _head_norm(x):
    rstd = lax.rsqrt(_group_sum64(x * x) * (1.0 / HEAD_DIM) + EPS)
    return x * rstd, rstd


def _head_norm_bwd(dxh, xh, rstd):
    return rstd * (dxh - xh * (_group_sum64(dxh * xh) * (1.0 / HEAD_DIM)))


def _roll64(x):
    return pltpu.roll(x, 64, axis=1)


def _attn_specs(wq, wk):
    kb = wq // wk
    prev = lambda i: jnp.maximum(i - 1, 0)
    return dict(
        q=pl.BlockSpec((BLOCK, wq), lambda i: (i, 0)),
        kc=pl.BlockSpec((BLOCK, wk), lambda i: (i, kb)),
        kp=pl.BlockSpec((BLOCK, wk), lambda i: (prev(i), kb)),
        vc=pl.BlockSpec((BLOCK, wk), lambda i: (i, kb + 1)),
        vp=pl.BlockSpec((BLOCK, wk), lambda i: (prev(i), kb + 1)),
        tq=pl.BlockSpec((BLOCK, wq), lambda i: (i, 0)),
        tkp=pl.BlockSpec((BLOCK, wk), lambda i: (prev(i), 0)),
        gq=pl.BlockSpec((1, wq), lambda i: (0, 0)),
        gk=pl.BlockSpec((1, wk), lambda i: (0, 0)),
        sinks=pl.BlockSpec(memory_space=pltpu.SMEM),
    )


def _attn_prologue(i, q_ref, kc_ref, kp_ref, cq_ref, sq_ref, ckp_ref, skp_ref, gq_ref, gk_ref):
    wk = kc_ref.shape[1]
    cq, sq = cq_ref[...], sq_ref[...]
    ck, sk = cq[:, :wk], sq[:, :wk]
    qh, q_rstd = _head_norm(q_ref[...])
    kch, kc_rstd = _head_norm(kc_ref[...])
    kph, kp_rstd = _head_norm(kp_ref[...])
    qn = _rope(qh * gq_ref[...], cq, sq)
    knc = _rope(kch * gk_ref[...], ck, sk)
    knp = _rope(kph * gk_ref[...], ckp_ref[...], skp_ref[...])
    stacked = (Q_PER_KV * BLOCK, BLOCK)
    row = lax.broadcasted_iota(jnp.int32, stacked, 0) & (BLOCK - 1)
    col = lax.broadcasted_iota(jnp.int32, stacked, 1)
    mask_c = col <= row
    mask_p = jnp.logical_and(col > row, i > 0)
    half = (lax.broadcasted_iota(jnp.int32, (BLOCK, BLOCK), 1) >= HEAD_DIM).astype(jnp.int32)
    return dict(cq=cq, sq=sq, ck=ck, sk=sk, qh=qh, q_rstd=q_rstd, kch=kch, kc_rstd=kc_rstd, kph=kph,
                kp_rstd=kp_rstd, qn=qn, knc=knc, knp=knp, mask_c=mask_c, mask_p=mask_p, half=half)


def _stack_heads(x, g, half):
    kpar = g % 2
    pieces = []
    for j in range(Q_PER_KV):
        t, e = divmod(Q_PER_KV * g + j, 2)
        piece = jnp.where(half == e, x[:, LANES * t:LANES * t + LANES], 0.0)
        pieces.append(piece if e == kpar else _roll64(piece))
    return jnp.concatenate(pieces, axis=0)


def _unstack_heads(y, g, half):
    kpar = g % 2
    slabs = {}
    for j in range(Q_PER_KV):
        t, e = divmod(Q_PER_KV * g + j, 2)
        piece = jnp.where(half == kpar, y[BLOCK * j:BLOCK * j + BLOCK], 0.0)
        piece = piece if e == kpar else _roll64(piece)
        slabs[t] = piece if t not in slabs else slabs[t] + piece
    return slabs


def _group_scores(st, g, sinks_ref, scale):
    ks = g // 2
    sl = slice(LANES * ks, LANES * ks + LANES)
    q4 = _stack_heads(st["qn"], g, st["half"]).astype(BF16)
    kc, kp = st["knc"][:, sl].astype(BF16), st["knp"][:, sl].astype(BF16)
    rows = Q_PER_KV * BLOCK
    at = lax.broadcasted_iota(jnp.int32, (rows, 1), 0)
    head = jnp.zeros((rows, 1), jnp.int32)
    sink = jnp.zeros((rows, 1), F32) + sinks_ref[Q_PER_KV * g]
    for j in range(1, Q_PER_KV):
        head = jnp.where(at >= BLOCK * j, j, head)
        sink = jnp.where(at >= BLOCK * j, sinks_ref[Q_PER_KV * g + j], sink)
    s_c = jnp.where(st["mask_c"], _dot(q4, kc, NT) * scale, NEG)
    s_p = jnp.where(st["mask_p"], _dot(q4, kp, NT) * scale, NEG)
    m = jnp.maximum(jnp.maximum(jnp.max(s_c, axis=1, keepdims=True), jnp.max(s_p, axis=1, keepdims=True)), sink)
    p_c, p_p = jnp.exp(s_c - m), jnp.exp(s_p - m)
    p_s = jnp.exp(sink - m)
    inv = 1.0 / (jnp.sum(p_c, axis=1, keepdims=True) + jnp.sum(p_p, axis=1, keepdims=True) + p_s)
    return dict(sl=sl, head=head, q4=q4, kc=kc, kp=kp, pr_c=p_c * inv, pr_p=p_p * inv, pr_s=p_s * inv)


def _attn_fwd(proj, tables, gq, gk, sinks, wq, wk, name, after=None):
    t = proj.shape[0]
    nb = t // BLOCK
    sp = _attn_specs(wq, wk)
    scale = HEAD_DIM ** -0.5
    cos_t, sin_t, cos_k, sin_k = tables

    def body(sinks_ref, q_ref, kc_ref, kp_ref, vc_ref, vp_ref, cq_ref, sq_ref, ckp_ref, skp_ref, gq_ref, gk_ref,
             *rest):
        o_ref = rest[-1]
        i = pl.program_id(0)
        st = _attn_prologue(i, q_ref, kc_ref, kp_ref, cq_ref, sq_ref, ckp_ref, skp_ref, gq_ref, gk_ref)
        for g in range(wq // (Q_PER_KV * HEAD_DIM)):
            gs = _group_scores(st, g, sinks_ref, scale)
            own = st["half"] == g % 2
            vc = jnp.where(own, vc_ref[:, gs["sl"]], 0.0).astype(BF16)
            vp = jnp.where(own, vp_ref[:, gs["sl"]], 0.0).astype(BF16)
            out = _dot(gs["pr_c"].astype(BF16), vc, NN) + _dot(gs["pr_p"].astype(BF16), vp, NN)
            for ts, slab in _unstack_heads(out, g, st["half"]).items():
                o_ref[:, LANES * ts:LANES * ts + LANES] = slab.astype(o_ref.dtype)

    return pl.pallas_call(
        body,
        name=name,
        grid=(nb,),
        in_specs=[sp["sinks"], sp["q"], sp["kc"], sp["kp"], sp["vc"], sp["vp"], sp["tq"], sp["tq"], sp["tkp"],
                  sp["tkp"], sp["gq"], sp["gk"]] + ([ANY] if after is not None else []),
        out_specs=pl.BlockSpec((BLOCK, wq), lambda i: (i, 0)),
        out_shape=jax.ShapeDtypeStruct((t, wq), BF16),
        compiler_params=_params("parallel"),
    )(sinks, proj, proj, proj, proj, proj, cos_t, sin_t, cos_k, sin_k, gq, gk, *(() if after is None else (after,)))


def _attn_bwd(proj, dout, tables, gq, gk, sinks, wq, wk, name, after=None):
    t = proj.shape[0]
    nb = t // BLOCK
    sp = _attn_specs(wq, wk)
    scale = HEAD_DIM ** -0.5
    cos_t, sin_t, cos_k, sin_k = tables

    def body(sinks_ref, q_ref, kc_ref, kp_ref, vc_ref, vp_ref, cq_ref, sq_ref, ckp_ref, skp_ref, gq_ref, gk_ref,
             do_ref, *rest):
        dq_ref, dk_ref, dv_ref, dgq_ref, dgk_ref, dsk_ref, dqn_ref, dknc_ref, dknp_ref, dvc_ref, dvp_ref = rest[-11:]
        i = pl.program_id(0)
        st = _attn_prologue(i, q_ref, kc_ref, kp_ref, cq_ref, sq_ref, ckp_ref, skp_ref, gq_ref, gk_ref)
        dknc_ref[...] = jnp.zeros_like(dknc_ref)
        dknp_ref[...] = jnp.zeros_like(dknp_ref)
        dvc_ref[...] = jnp.zeros_like(dvc_ref)
        dvp_ref[...] = jnp.zeros_like(dvp_ref)
        lane8 = _lane((8, LANES))
        dsinks = jnp.zeros((8, LANES), F32)
        for g in range(wq // (Q_PER_KV * HEAD_DIM)):
            gs = _group_scores(st, g, sinks_ref, scale)
            sl = gs["sl"]
            do4 = _stack_heads(do_ref[...], g, st["half"]).astype(BF16)
            dp_c = _dot(do4, vc_ref[:, sl].astype(BF16), NT)
            dp_p = _dot(do4, vp_ref[:, sl].astype(BF16), NT)
            pr_c, pr_p = gs["pr_c"], gs["pr_p"]
            rs = jnp.sum(pr_c * dp_c, axis=1, keepdims=True) + jnp.sum(pr_p * dp_p, axis=1, keepdims=True)
            ds_c = (pr_c * (dp_c - rs) * scale).astype(BF16)
            ds_p = (pr_p * (dp_p - rs) * scale).astype(BF16)
            dsink_rows = -gs["pr_s"] * rs
            for j in range(Q_PER_KV):
                dsink = jnp.sum(jnp.where(gs["head"] == j, dsink_rows, 0.0))
                dsinks = dsinks + jnp.where(lane8 == Q_PER_KV * g + j, dsink, 0.0)
            dq4 = _dot(ds_c, gs["kc"], NN) + _dot(ds_p, gs["kp"], NN)
            for ts, slab in _unstack_heads(dq4, g, st["half"]).items():
                dqn_ref[:, LANES * ts:LANES * ts + LANES] = slab
            dvc_ref[:, sl] += _dot(pr_c.astype(BF16), do4, TN)
            dvp_ref[:, sl] += _dot(pr_p.astype(BF16), do4, TN)
            dknc_ref[:, sl] += _dot(ds_c, gs["q4"], TN)
            dknp_ref[:, sl] += _dot(ds_p, gs["q4"], TN)

        gqv, gkv = gq_ref[...], gk_ref[...]
        dqg = _rope_t(dqn_ref[...], st["cq"], st["sq"])
        dq_ref[...] = _head_norm_bwd(dqg * gqv, st["qh"], st["q_rstd"]).astype(dq_ref.dtype)
        dkcg = _rope_t(dknc_ref[...], st["ck"], st["sk"])
        dkpg = _rope_t(dknp_ref[...], ckp_ref[...], skp_ref[...])
        dk_cur = _head_norm_bwd(dkcg * gkv, st["kch"], st["kc_rstd"])
        dk_prev = _head_norm_bwd(dkpg * gkv, st["kph"], st["kp_rstd"])
        dgq_part = jnp.broadcast_to(jnp.sum(dqg * st["qh"], axis=0, keepdims=True), dgq_ref.shape)
        dgk_part = jnp.broadcast_to(
            jnp.sum(dkcg * st["kch"] + dkpg * st["kph"], axis=0, keepdims=True), dgk_ref.shape)
        cur = pl.ds(pl.multiple_of(i * BLOCK, BLOCK), BLOCK)
        dk_ref[cur, :] = dk_cur
        dv_ref[cur, :] = dvc_ref[...]

        @pl.when(i == 0)
        def _():
            dgq_ref[...] = dgq_part
            dgk_ref[...] = dgk_part
            dsk_ref[...] = dsinks

        @pl.when(i > 0)
        def _():
            before = pl.ds(pl.multiple_of((i - 1) * BLOCK, BLOCK), BLOCK)
            dk_ref[before, :] += dk_prev
            dv_ref[before, :] += dvp_ref[...]
            dgq_ref[...] += dgq_part
            dgk_ref[...] += dgk_part
            dsk_ref[...] += dsinks

    whole = lambda shape: pl.BlockSpec(shape, lambda i: (0, 0))
    return pl.pallas_call(
        body,
        name=name,
        grid=(nb,),
        in_specs=[sp["sinks"], sp["q"], sp["kc"], sp["kp"], sp["vc"], sp["vp"], sp["tq"], sp["tq"], sp["tkp"],
                  sp["tkp"], sp["gq"], sp["gk"], pl.BlockSpec((BLOCK, wq), lambda i: (i, 0))]
        + ([ANY] if after is not None else []),
        out_specs=[pl.BlockSpec((BLOCK, wq), lambda i: (i, 0)), whole((t, wk)), whole((t, wk)), whole((8, wq)),
                   whole((8, wk)), whole((8, LANES))],
        out_shape=[jax.ShapeDtypeStruct((t, wq), BF16), jax.ShapeDtypeStruct((t, wk), F32),
                   jax.ShapeDtypeStruct((t, wk), F32), jax.ShapeDtypeStruct((8, wq), F32),
                   jax.ShapeDtypeStruct((8, wk), F32), jax.ShapeDtypeStruct((8, LANES), F32)],
        scratch_shapes=[pltpu.VMEM((BLOCK, wq), F32), pltpu.VMEM((BLOCK, wk), F32), pltpu.VMEM((BLOCK, wk), F32),
                        pltpu.VMEM((BLOCK, wk), F32), pltpu.VMEM((BLOCK, wk), F32)],
        compiler_params=_params("arbitrary"),
    )(sinks, proj, proj, proj, proj, proj, cos_t, sin_t, cos_k, sin_k, gq, gk, dout,
      *(() if after is None else (after,)))


_GELU_K = math.sqrt(2.0 / math.pi)
_GELU_A = 0.044715


def _gelu(x):
    return 0.5 * x * (1.0 + jnp.tanh(_GELU_K * (x + _GELU_A * x * x * x)))


def _gelu_grad(x):
    th = jnp.tanh(_GELU_K * (x + _GELU_A * x * x * x))
    return 0.5 * (1.0 + th) + 0.5 * x * (1.0 - th * th) * (_GELU_K * (1.0 + 3.0 * _GELU_A * x * x))


def _group_ln(v):
    mu = jnp.mean(v, axis=1, keepdims=True)
    cen = v - mu
    rstd = lax.rsqrt(jnp.mean(cen * cen, axis=1, keepdims=True) + EPS)
    return cen * rstd, rstd


def _sgu_geometry(off_u, ws):
    cw = math.gcd(off_u, ws)
    return cw, ws // cw, off_u // cw, (off_u + ws) // cw


def _sgu_fwd(proj, ln_g, ln_b, w_s, bt, off_u, ws, name, after=None):
    t = proj.shape[0]
    nb = t // BLOCK
    cw, nc, ub, vb = _sgu_geometry(off_u, ws)
    gpc = cw // LANES
    ng = ws // LANES

    def body(u_ref, v_ref, g_ref, b_ref, w_ref, bt_ref, *rest):
        o_ref = rest[-1]
        jc = pl.program_id(0)
        row = lax.broadcasted_iota(jnp.int32, (BLOCK, BLOCK), 0)
        col = lax.broadcasted_iota(jnp.int32, (BLOCK, BLOCK), 1)
        lane_g = _lane((BLOCK, ng))
        for gi in range(gpc):
            sl = slice(LANES * gi, LANES * gi + LANES)
            xh, _ = _group_ln(_gelu(v_ref[:, sl]))
            vn = xh * g_ref[:, sl] + b_ref[:, sl]
            w = jnp.where(row >= col, w_ref[gi], 0.0).astype(BF16)
            bias = jnp.sum(jnp.where(lane_g == jc * gpc + gi, bt_ref[...], 0.0), axis=1, keepdims=True)
            s = _dot(w, vn.astype(BF16), NN) + bias
            o_ref[:, sl] = (_gelu(u_ref[:, sl]) * s).astype(o_ref.dtype)

    return pl.pallas_call(
        body,
        name=name,
        grid=(nc, nb),
        in_specs=[pl.BlockSpec((BLOCK, cw), lambda jc, i: (i, ub + jc)),
                  pl.BlockSpec((BLOCK, cw), lambda jc, i: (i, vb + jc)),
                  pl.BlockSpec((1, cw), lambda jc, i: (0, jc)),
                  pl.BlockSpec((1, cw), lambda jc, i: (0, jc)),
                  pl.BlockSpec((gpc, BLOCK, BLOCK), lambda jc, i: (jc, 0, 0)),
                  pl.BlockSpec((BLOCK, ng), lambda jc, i: (0, 0))] + ([ANY] if after is not None else []),
        out_specs=pl.BlockSpec((BLOCK, cw), lambda jc, i: (i, jc)),
        out_shape=jax.ShapeDtypeStruct((t, ws), BF16),
        compiler_params=_params("parallel", "parallel"),
    )(proj, proj, ln_g, ln_b, w_s, bt, *(() if after is None else (after,)))


def _sgu_bwd(proj, dout, ln_g, ln_b, w_s, bt, off_u, ws, name):
    t = proj.shape[0]
    nb = t // BLOCK
    cw, nc, ub, vb = _sgu_geometry(off_u, ws)
    gpc = cw // LANES
    ng = ws // LANES

    def body(u_ref, v_ref, g_ref, b_ref, w_ref, bt_ref, do_ref, du_ref, dv_ref, dg_ref, db_ref, dw_ref, dbs_ref,
             bacc_ref):
        jc = pl.program_id(0)
        i = pl.program_id(1)
        row = lax.broadcasted_iota(jnp.int32, (BLOCK, BLOCK), 0)
        col = lax.broadcasted_iota(jnp.int32, (BLOCK, BLOCK), 1)
        lane_g = _lane((BLOCK, ng))
        tri = row >= col

        @pl.when(i == 0)
        def _():
            dg_ref[...] = jnp.zeros_like(dg_ref)
            db_ref[...] = jnp.zeros_like(db_ref)
            dw_ref[...] = jnp.zeros_like(dw_ref)
            bacc_ref[...] = jnp.zeros_like(bacc_ref)

        for gi in range(gpc):
            sl = slice(LANES * gi, LANES * gi + LANES)
            u_raw, v_raw = u_ref[:, sl], v_ref[:, sl]
            xh, rstd = _group_ln(_gelu(v_raw))
            gam = g_ref[:, sl]
            vn = (xh * gam + b_ref[:, sl]).astype(BF16)
            w = jnp.where(tri, w_ref[gi], 0.0)
            bias = jnp.sum(jnp.where(lane_g == jc * gpc + gi, bt_ref[...], 0.0), axis=1, keepdims=True)
            s = _dot(w.astype(BF16), vn, NN) + bias
            dov = do_ref[:, sl]
            du_ref[:, sl] = (dov * s * _gelu_grad(u_raw)).astype(du_ref.dtype)
            ds = dov * _gelu(u_raw)
            ds16 = ds.astype(BF16)
            dw_ref[gi] += jnp.where(tri, _dot(ds16, vn, NT), 0.0)
            bacc_ref[gi] += ds
            dvn = _dot(w.T.astype(BF16), ds16, NN)
            dg_ref[:, sl] += jnp.broadcast_to(jnp.sum(dvn * xh, axis=0, keepdims=True), (8, LANES))
            db_ref[:, sl] += jnp.broadcast_to(jnp.sum(dvn, axis=0, keepdims=True), (8, LANES))
            dxh = dvn * gam
            dvg = rstd * (dxh - jnp.mean(dxh, axis=1, keepdims=True)
                          - xh * jnp.mean(dxh * xh, axis=1, keepdims=True))
            dv_ref[:, sl] = (dvg * _gelu_grad(v_raw)).astype(dv_ref.dtype)

        @pl.when(i == nb - 1)
        def _():
            for gi in range(gpc):
                dbs_ref[gi] = jnp.broadcast_to(jnp.sum(bacc_ref[gi].T, axis=0, keepdims=True), (8, LANES))

    blk = lambda base: pl.BlockSpec((BLOCK, cw), lambda jc, i: (i, base + jc))
    vec = pl.BlockSpec((1, cw), lambda jc, i: (0, jc))
    acc = pl.BlockSpec((8, cw), lambda jc, i: (0, jc))
    wsp = pl.BlockSpec((gpc, BLOCK, BLOCK), lambda jc, i: (jc, 0, 0))
    return pl.pallas_call(
        body,
        name=name,
        grid=(nc, nb),
        in_specs=[blk(ub), blk(vb), vec, vec, wsp, pl.BlockSpec((BLOCK, ng), lambda jc, i: (0, 0)), blk(0)],
        out_specs=[blk(0), blk(0), acc, acc, wsp, pl.BlockSpec((gpc, 8, LANES), lambda jc, i: (jc, 0, 0))],
        out_shape=[jax.ShapeDtypeStruct((t, ws), BF16), jax.ShapeDtypeStruct((t, ws), BF16),
                   jax.ShapeDtypeStruct((8, ws), F32), jax.ShapeDtypeStruct((8, ws), F32),
                   jax.ShapeDtypeStruct((ng, BLOCK, BLOCK), F32), jax.ShapeDtypeStruct((ng, 8, LANES), F32)],
        scratch_shapes=[pltpu.VMEM((gpc, BLOCK, BLOCK), F32)],
        compiler_params=_params("arbitrary", "arbitrary"),
    )(proj, proj, ln_g, ln_b, w_s, bt, dout)


def _sigmoid(x):
    return 1.0 / (1.0 + jnp.exp(-x))


def _merge_geometry(off_g, d):
    cw = math.gcd(off_g, d)
    return cw, d // cw, off_g // cw, (off_g + d) // cw


def _branches_fwd(attn, sgu, wab_t, wsb_t, proj, off_g, name):
    t = attn.shape[0]
    d = wab_t.shape[0]
    tn, _, ab, bb = _merge_geometry(off_g, d)
    tm = _divisor_tile(t, 1024, 128)

    def body(a1_ref, a2_ref, b1_ref, b2_ref, la_ref, lb_ref, bra_ref, brb_ref, o_ref):
        va = _dot(a1_ref[...], b1_ref[...], NT)
        vb = _dot(a2_ref[...], b2_ref[...], NT)
        bra_ref[...] = va
        brb_ref[...] = vb
        o_ref[...] = (_sigmoid(la_ref[...]) * va + _sigmoid(lb_ref[...]) * vb).astype(o_ref.dtype)

    rows = lambda w: pl.BlockSpec((tm, w), lambda i, j: (i, 0))
    wrow = lambda w: pl.BlockSpec((tn, w), lambda i, j: (j, 0))
    blk = lambda base: pl.BlockSpec((tm, tn), lambda i, j: (i, base + j))
    return pl.pallas_call(
        body,
        name=name,
        grid=(t // tm, d // tn),
        in_specs=[rows(attn.shape[1]), rows(sgu.shape[1]), wrow(wab_t.shape[1]), wrow(wsb_t.shape[1]), blk(ab),
                  blk(bb)],
        out_specs=[blk(0)] * 3,
        out_shape=[jax.ShapeDtypeStruct((t, d), F32), jax.ShapeDtypeStruct((t, d), F32),
                   jax.ShapeDtypeStruct((t, d), BF16)],
        compiler_params=_params("parallel", "parallel"),
    )(attn, sgu, wab_t, wsb_t, proj, proj)


def _branches_bwd(dx16, wo, br_a, br_b, proj, off_g, name, after=None):
    t, d = br_a.shape
    tn, _, ab, bb = _merge_geometry(off_g, d)
    tm = _divisor_tile(t, 1024, 128)
    k = dx16.shape[1]

    def body(a_ref, b_ref, bra_ref, brb_ref, la_ref, lb_ref, *rest):
        da_ref, db_ref, dla_ref, dlb_ref = rest[-4:]
        dmv = _dot(a_ref[...], b_ref[...], NT)
        ga, gb = _sigmoid(la_ref[...]), _sigmoid(lb_ref[...])
        da_ref[...] = (dmv * ga).astype(da_ref.dtype)
        db_ref[...] = (dmv * gb).astype(db_ref.dtype)
        dla_ref[...] = (dmv * bra_ref[...] * ga * (1.0 - ga)).astype(dla_ref.dtype)
        dlb_ref[...] = (dmv * brb_ref[...] * gb * (1.0 - gb)).astype(dlb_ref.dtype)

    blk = lambda base: pl.BlockSpec((tm, tn), lambda i, j: (i, base + j))
    return pl.pallas_call(
        body,
        name=name,
        grid=(t // tm, d // tn),
        in_specs=[pl.BlockSpec((tm, k), lambda i, j: (i, 0)), pl.BlockSpec((tn, k), lambda i, j: (j, 0)), blk(0),
                  blk(0), blk(ab), blk(bb)] + ([ANY] if after is not None else []),
        out_specs=[blk(0)] * 4,
        out_shape=[jax.ShapeDtypeStruct((t, d), BF16)] * 4,
        compiler_params=_params("parallel", "parallel"),
    )(dx16, wo, br_a, br_b, proj, proj, *(() if after is None else (after,)))


def _gate_up_fwd(h2, wgu_t, name, after=None):
    t, d = h2.shape
    f = wgu_t.shape[0] // 2
    tm = _divisor_tile(t, 1024, 128)
    tn = _divisor_tile(f, 512, 128)
    nb = f // tn

    def body(a_ref, bg_ref, bu_ref, *rest):
        gu_ref, act_ref = rest[-2:]
        av = a_ref[...]
        gv = _dot(av, bg_ref[...], NT)
        uv = _dot(av, bu_ref[...], NT)
        gu_ref[0] = gv
        gu_ref[1] = uv
        act_ref[...] = (gv * _sigmoid(gv) * uv).astype(act_ref.dtype)

    return pl.pallas_call(
        body,
        name=name,
        grid=(t // tm, nb),
        in_specs=[pl.BlockSpec((tm, d), lambda i, j: (i, 0)), pl.BlockSpec((tn, d), lambda i, j: (j, 0)),
                  pl.BlockSpec((tn, d), lambda i, j: (j + nb, 0))] + ([ANY] if after is not None else []),
        out_specs=[pl.BlockSpec((2, tm, tn), lambda i, j: (0, i, j)), pl.BlockSpec((tm, tn), lambda i, j: (i, j))],
        out_shape=[jax.ShapeDtypeStruct((2, t, f), F32), jax.ShapeDtypeStruct((t, f), BF16)],
        compiler_params=_params("parallel", "parallel"),
    )(h2, wgu_t, wgu_t, *(() if after is None else (after,)))


def _gate_up_bwd(dx16, wd, gu, name, after=None):
    t, d = dx16.shape
    f = wd.shape[0]
    tm = _divisor_tile(t, 1024, 128)
    tn = _divisor_tile(f, 512, 128)

    def body(a_ref, b_ref, gu_ref, *rest):
        o_ref = rest[-1]
        dav = _dot(a_ref[...], b_ref[...], NT)
        gv = gu_ref[0]
        sg = _sigmoid(gv)
        o_ref[0] = (dav * gu_ref[1] * (sg + gv * sg * (1.0 - sg))).astype(o_ref.dtype)
        o_ref[1] = (dav * gv * sg).astype(o_ref.dtype)

    pair = pl.BlockSpec((2, tm, tn), lambda i, j: (0, i, j))
    return pl.pallas_call(
        body,
        name=name,
        grid=(t // tm, f // tn),
        in_specs=[pl.BlockSpec((tm, d), lambda i, j: (i, 0)), pl.BlockSpec((tn, d), lambda i, j: (j, 0)), pair]
        + ([ANY] if after is not None else []),
        out_specs=pair,
        out_shape=jax.ShapeDtypeStruct((2, t, f), BF16),
        compiler_params=_params("parallel", "parallel"),
    )(dx16, wd, gu, *(() if after is None else (after,)))


def _loss_and_grad(y, target, name):
    t, d = y.shape
    tr = _divisor_tile(t, 256, 8)

    def body(y_ref, t_ref, l_ref, dy_ref, dy16_ref):
        i = pl.program_id(0)
        err = y_ref[...] - t_ref[...]
        dy_ref[...] = err * (1.0 / d)
        dy16_ref[...] = (err * (1.0 / d)).astype(dy16_ref.dtype)
        part = jnp.broadcast_to(0.5 * jnp.sum(err * err) * (1.0 / d), l_ref.shape)

        @pl.when(i == 0)
        def _():
            l_ref[...] = part

        @pl.when(i > 0)
        def _():
            l_ref[...] += part

    row = pl.BlockSpec((tr, d), lambda i: (i, 0))
    return pl.pallas_call(
        body,
        name=name,
        grid=(t // tr,),
        in_specs=[row, row],
        out_specs=[pl.BlockSpec((8, LANES), lambda i: (0, 0)), row, row],
        out_shape=[jax.ShapeDtypeStruct((8, LANES), F32), jax.ShapeDtypeStruct((t, d), F32),
                   jax.ShapeDtypeStruct((t, d), BF16)],
        compiler_params=_params("arbitrary"),
    )(y, target)


def _adam_math(w, g, m, v):
    m = ADAM_B1 * m + (1.0 - ADAM_B1) * g
    v = ADAM_B2 * v + (1.0 - ADAM_B2) * (g * g)
    m_hat = m / (1.0 - ADAM_B1 ** ADAM_STEP)
    v_hat = v / (1.0 - ADAM_B2 ** ADAM_STEP)
    delta = -ADAM_LR * (m_hat / (jnp.sqrt(v_hat) + ADAM_EPS) + ADAM_WD * w)
    return delta, m, v


def _row_tile(r, c, elems=512 * 1024):
    return _divisor_tile(r, max(8, elems // c // 8 * 8), 8)


def _adam(w, grads, m, v, chip, name, after=None):
    nl, r, c = w.shape
    tr = _row_tile(r, c, 256 * 1024)
    nb = r // tr
    counts = [len(terms) for terms, _ in grads]

    def body(chip_ref, *refs):
        w_ref, m_ref, v_ref = refs[:3]
        g_ref, d_ref, nm_ref, nv_ref = refs[-4:]
        layer = pl.program_id(0)
        g, at = None, 3
        for li, n in enumerate(counts):
            total = refs[at][...].astype(F32)
            for ref in refs[at + 1:at + n]:
                total = total + ref[...].astype(F32)
            g = total if g is None else jnp.where(layer == li, total, g)
            at += n
        g_ref[...] = g
        d_ref[...], nm_ref[...], nv_ref[...] = _adam_math(w_ref[...], g, m_ref[...], v_ref[...])

    def term_spec(li, p, by_owner):
        def index(l, i, chip_ref):
            rows = jnp.where(l < li, 0, jnp.where(l > li, nb - 1, i))
            return (p, chip_ref[0] if by_owner else 0, rows, 0)
        return pl.BlockSpec((None, None, tr, c), index)

    row = pl.BlockSpec((None, tr, c), lambda l, i, chip_ref: (l, i, 0))
    specs, arrays = [], []
    for li, (terms, p) in enumerate(grads):
        for term in terms:
            specs.append(term_spec(li, p, term.shape[1] == 4))
            arrays.append(term)
    return pl.pallas_call(
        body,
        name=name,
        grid_spec=pltpu.PrefetchScalarGridSpec(
            num_scalar_prefetch=1, grid=(nl, nb),
            in_specs=[row] * 3 + specs + ([ANY] if after is not None else []), out_specs=[row] * 4),
        out_shape=[jax.ShapeDtypeStruct((nl, r, c), F32)] * 4,
        compiler_params=_params("arbitrary", "arbitrary"),
    )(chip, w, m, v, *arrays, *(() if after is None else (after,)))


def _place_shard(parts, layer, dev, out_dtype, name, after=None):
    p = len(parts)
    _, r, c = parts[0].shape
    tr = _row_tile(r, c)

    def body(dev_ref, *refs):
        o_ref = refs[-1]
        x = refs[0][...]
        for pi in range(1, p):
            x = jnp.where(pl.program_id(0) == pi, refs[pi][...], x)
        o_ref[...] = x.astype(o_ref.dtype)

    return pl.pallas_call(
        body,
        name=name,
        grid_spec=pltpu.PrefetchScalarGridSpec(
            num_scalar_prefetch=1,
            grid=(p, r // tr),
            in_specs=[pl.BlockSpec((None, tr, c), lambda pi, i, dev_ref: (layer, i, 0))] * p
            + ([ANY] if after is not None else []),
            out_specs=pl.BlockSpec((None, None, tr, c), lambda pi, i, dev_ref: (pi, dev_ref[0], i, 0)),
        ),
        out_shape=jax.ShapeDtypeStruct((p, N_DEV, r, c), out_dtype),
        compiler_params=_params("parallel", "parallel"),
    )(dev, *parts, *(() if after is None else (after,)))


def _sum_sibling(g, land, core, name):
    p, _, _, r, c = g.shape
    tr = _row_tile(r, c, 1024 * 1024)

    def body(core_ref, g_ref, l_ref, o_ref):
        o_ref[...] = (g_ref[...].astype(F32) + l_ref[...].astype(F32)).astype(o_ref.dtype)

    return pl.pallas_call(
        body,
        name=name,
        grid_spec=pltpu.PrefetchScalarGridSpec(
            num_scalar_prefetch=1,
            grid=(p, 4, r // tr),
            in_specs=[pl.BlockSpec((None, None, None, tr, c), lambda pi, q, i, core_ref: (pi, q, core_ref[0], i, 0)),
                      pl.BlockSpec((None, None, None, tr, c), lambda pi, q, i, core_ref: (pi, q, 0, i, 0))],
            out_specs=pl.BlockSpec((None, None, tr, c), lambda pi, q, i, core_ref: (pi, q, i, 0)),
        ),
        out_shape=jax.ShapeDtypeStruct((p, 4, r, c), BF16),
        compiler_params=_params("parallel", "parallel", "parallel"),
    )(core, g, land)


def _sum_chips(s, lands, chip, name):
    p, _, r, c = s.shape
    tr = _row_tile(r, c)

    def body(chip_ref, s_ref, l0_ref, l1_ref, l2_ref, o_ref):
        total = s_ref[...].astype(F32) + l0_ref[...].astype(F32)
        o_ref[...] = total + l1_ref[...].astype(F32) + l2_ref[...].astype(F32)

    land_spec = pl.BlockSpec((None, None, tr, c), lambda pi, i, chip_ref: (pi, 0, i, 0))
    return pl.pallas_call(
        body,
        name=name,
        grid_spec=pltpu.PrefetchScalarGridSpec(
            num_scalar_prefetch=1,
            grid=(p, r // tr),
            in_specs=[pl.BlockSpec((None, None, tr, c), lambda pi, i, chip_ref: (pi, chip_ref[0], i, 0)),
                      land_spec, land_spec, land_spec],
            out_specs=pl.BlockSpec((None, tr, c), lambda pi, i, chip_ref: (pi, i, 0)),
        ),
        out_shape=jax.ShapeDtypeStruct((p, r, c), F32),
        compiler_params=_params("parallel", "parallel"),
    )(chip, s, *lands)


def _small_reduce_adam(gathered, w, m, v, name):
    _, r, c = gathered.shape
    tr = _row_tile(r, c)

    def body(p_ref, w_ref, m_ref, v_ref, g_ref, d_ref, nm_ref, nv_ref):
        g = p_ref[0]
        for j in range(1, N_DEV):
            g = g + p_ref[j]
        g_ref[...] = g
        d_ref[...], nm_ref[...], nv_ref[...] = _adam_math(w_ref[...], g, m_ref[...], v_ref[...])

    row = pl.BlockSpec((tr, c), lambda i: (i, 0))
    return pl.pallas_call(
        body,
        name=name,
        grid=(r // tr,),
        in_specs=[pl.BlockSpec((N_DEV, tr, c), lambda i: (0, i, 0)), row, row, row],
        out_specs=[row] * 4,
        out_shape=[jax.ShapeDtypeStruct((r, c), F32)] * 4,
        compiler_params=_params("parallel"),
    )(gathered, w, m, v)


def _place():
    return lax.axis_index("x"), lax.axis_index("y"), lax.axis_index("c")


HBM =pl.BlockSpec(memory_space=pltpu.HBM)
SEM = pl.BlockSpec(memory_space=pltpu.SEMAPHORE)
TOKEN = pl.BlockSpec(memory_space=pltpu.VMEM)
EFFECT = pltpu.SideEffectType.DATAFLOW_SIDE_EFFECTING


def _in_hbm(a):
    return pltpu.with_memory_space_constraint(a, pltpu.HBM)


_FLIPS = {"me": (0, 0, 0), "s": (0, 0, 1), "x": (1, 0, 0), "y": (0, 1, 0), "d": (1, 1, 0)}
GATHER_STAGES = (
    (("s", "me", "all"), ("x", "me", "all"), ("y", "me", "all")),
    (("s", "x", "all"), ("s", "y", "all"), ("y", "x", "first"), ("x", "y", "second")),
    (("s", "d", "all"),),
)


def _flipped(place, *names):
    out = list(place)
    for name in names:
        out = [1 - p if f else p for p, f in zip(out, _FLIPS[name])]
    return tuple(out)


def _block_part(ref, place, part):
    px, py, pc = place
    rows = ref.shape[2]
    span = {"all": pl.ds(0, rows), "first": pl.ds(0, rows // 2), "second": pl.ds(rows // 2, rows // 2)}[part]
    return ref.at[:, pl.ds(4 * px + 2 * py + pc, 1), span]


def _split_start(bufs, moves, name, after=None):
    n, nm = len(bufs), len(moves)
    extra = 0 if after is None else 1

    def body(*refs):
        ssem, rsem = refs[n + extra], refs[n + extra + 1]
        outs, token = refs[n + extra + 2:2 * n + extra + 2], refs[2 * n + extra + 2]
        me = _place()
        for a in range(n):
            for k, (to, owner, part) in enumerate(moves):
                piece = _block_part(outs[a], _flipped(me, owner), part)
                pltpu.make_async_remote_copy(
                    src_ref=piece, dst_ref=piece, send_sem=ssem.at[nm * a + k], recv_sem=rsem.at[nm * a + k],
                    device_id=_flipped(me, to), device_id_type=MESH).start()
        token[...] = jnp.zeros_like(token)

    outs = pl.pallas_call(
        body,
        name=name,
        in_specs=[HBM] * n + [ANY] * extra,
        out_specs=[SEM, SEM] + [HBM] * n + [TOKEN],
        out_shape=[pltpu.SemaphoreType.DMA((nm * n,))] * 2 + [pltpu.HBM(b.shape, b.dtype) for b in bufs]
        + [jax.ShapeDtypeStruct((8, LANES), F32)],
        input_output_aliases={i: 2 + i for i in range(n)},
        compiler_params=pltpu.CompilerParams(has_side_effects=EFFECT),
    )(*[_in_hbm(b) for b in bufs], *(() if after is None else (after,)))
    return outs[0], outs[1], list(outs[2:2 + n]), outs[-1]


def _split_wait(send_sems, recv_sems, bufs, moves, after, name):
    n, nm = len(bufs), len(moves)

    def body(*refs):
        ins, ssem, rsem = refs[:n], refs[n], refs[n + 1]
        me = _place()
        for a in range(n):
            for k, (to, owner, part) in enumerate(moves):
                landed = _block_part(ins[a], _flipped(me, owner, to), part)
                cp = pltpu.make_async_remote_copy(
                    src_ref=landed, dst_ref=landed, send_sem=ssem.at[nm * a + k], recv_sem=rsem.at[nm * a + k],
                    device_id=_flipped(me, to), device_id_type=MESH)
                cp.wait_send()
                cp.wait_recv()

    return pl.pallas_call(
        body,
        name=name,
        in_specs=[HBM] * n + [SEM, SEM, ANY],
        out_specs=[HBM] * n,
        out_shape=[pltpu.HBM(b.shape, b.dtype) for b in bufs],
        input_output_aliases={i: i for i in range(n)},
        compiler_params=pltpu.CompilerParams(has_side_effects=EFFECT),
    )(*bufs, send_sems, recv_sems, after)


def _chips_start(sums, name):
    n = len(sums)

    def body(*refs):
        ssem, rsem = refs[4 * n], refs[4 * n + 1]
        src, land = refs[4 * n + 2:5 * n + 2], refs[5 * n + 2:8 * n + 2]
        token = refs[8 * n + 2]
        x, y, c = _place()
        chips = [(1 - x, y), (x, 1 - y), (1 - x, 1 - y)]
        for a in range(n):
            for k, (px, py) in enumerate(chips):
                pltpu.make_async_remote_copy(
                    src_ref=src[a].at[:, pl.ds(2 * px + py, 1)], dst_ref=land[3 * a + k], send_sem=ssem.at[3 * a + k],
                    recv_sem=rsem.at[3 * a + k], device_id=(px, py, c), device_id_type=MESH).start()
        token[...] = jnp.zeros_like(token)

    lands = []
    for s in sums:
        lands += [lax.empty((s.shape[0], 1) + s.shape[2:], s.dtype) for _ in range(3)]
    outs = pl.pallas_call(
        body,
        name=name,
        in_specs=[HBM] * (4 * n),
        out_specs=[SEM, SEM] + [HBM] * (4 * n) + [TOKEN],
        out_shape=[pltpu.SemaphoreType.DMA((3 * n,))] * 2 + [pltpu.HBM(b.shape, b.dtype) for b in list(sums) + lands]
        + [jax.ShapeDtypeStruct((8, LANES), F32)],
        input_output_aliases={i: 2 + i for i in range(4 * n)},
        compiler_params=pltpu.CompilerParams(has_side_effects=EFFECT),
    )(*[_in_hbm(b) for b in list(sums) + lands])
    return outs[0], outs[1], list(outs[2:2 + n]), list(outs[2 + n:2 + 4 * n]), outs[-1]


def _chips_wait(send_sems, recv_sems, sums, lands, after, name):
    n = len(sums)

    def body(*refs):
        src, land = refs[:n], refs[n:4 * n]
        ssem, rsem = refs[4 * n], refs[4 * n + 1]
        x, y, c = _place()
        chips = [(1 - x, y), (x, 1 - y), (1 - x, 1 - y)]
        for a in range(n):
            for k, (px, py) in enumerate(chips):
                cp = pltpu.make_async_remote_copy(
                    src_ref=src[a].at[:, pl.ds(2 * px + py, 1)], dst_ref=land[3 * a + k], send_sem=ssem.at[3 * a + k],
                    recv_sem=rsem.at[3 * a + k], device_id=(px, py, c), device_id_type=MESH)
                cp.wait_send()
                cp.wait_recv()

    both = list(sums) + list(lands)
    outs = pl.pallas_call(
        body,
        name=name,
        in_specs=[HBM] * (4 * n) + [SEM, SEM, ANY],
        out_specs=[HBM] * (4 * n),
        out_shape=[pltpu.HBM(b.shape, b.dtype) for b in both],
        input_output_aliases={i: i for i in range(4 * n)},
        compiler_params=pltpu.CompilerParams(has_side_effects=EFFECT),
    )(*both, send_sems, recv_sems, after)
    return list(outs[:n]), [list(outs[n + 3 * a:n + 3 * a + 3]) for a in range(n)]


def _sibling_start(grads, name):
    n = len(grads)

    def body(*refs):
        ssem, rsem = refs[2 * n], refs[2 * n + 1]
        src, land = refs[2 * n + 2:3 * n + 2], refs[3 * n + 2:4 * n + 2]
        token = refs[4 * n + 2]
        x, y, c = _place()
        for a in range(n):
            pltpu.make_async_remote_copy(
                src_ref=src[a].at[:, :, pl.ds(1 - c, 1)], dst_ref=land[a], send_sem=ssem.at[a], recv_sem=rsem.at[a],
                device_id=(x, y, 1 - c), device_id_type=MESH).start()
        token[...] = jnp.zeros_like(token)

    lands = [lax.empty(g.shape[:2] + (1,) + g.shape[3:], g.dtype) for g in grads]
    both = list(grads) + lands
    outs = pl.pallas_call(
        body,
        name=name,
        in_specs=[HBM] * (2 * n),
        out_specs=[SEM, SEM] + [HBM] * (2 * n) + [TOKEN],
        out_shape=[pltpu.SemaphoreType.DMA((n,))] * 2 + [pltpu.HBM(b.shape, b.dtype) for b in both]
        + [jax.ShapeDtypeStruct((8, LANES), F32)],
        input_output_aliases={i: 2 + i for i in range(2 * n)},
        compiler_params=pltpu.CompilerParams(has_side_effects=EFFECT),
    )(*[_in_hbm(b) for b in both])
    return outs[0], outs[1], list(outs[2:2 + n]), list(outs[2 + n:2 + 2 * n]), outs[-1]


def _sibling_wait(send_sems, recv_sems, grads, lands, after, name):
    n = len(grads)

    def body(*refs):
        src, land = refs[:n], refs[n:2 * n]
        ssem, rsem = refs[2 * n], refs[2 * n + 1]
        x, y, c = _place()
        for a in range(n):
            cp = pltpu.make_async_remote_copy(
                src_ref=src[a].at[:, :, pl.ds(1 - c, 1)], dst_ref=land[a], send_sem=ssem.at[a], recv_sem=rsem.at[a],
                device_id=(x, y, 1 - c), device_id_type=MESH)
            cp.wait_send()
            cp.wait_recv()

    both = list(grads) + list(lands)
    outs = pl.pallas_call(
        body,
        name=name,
        in_specs=[HBM] * (2 * n) + [SEM, SEM, ANY],
        out_specs=[HBM] * (2 * n),
        out_shape=[pltpu.HBM(b.shape, b.dtype) for b in both],
        input_output_aliases={i: i for i in range(2 * n)},
        compiler_params=pltpu.CompilerParams(has_side_effects=EFFECT),
    )(*both, send_sems, recv_sems, after)
    return list(outs[:n]), list(outs[n:])


_SMALL = ("mix_norm", "q_norm", "k_norm", "sinks", "sgu_ln_g", "sgu_ln_b", "w_spatial", "b_spatial", "ffn_norm")


def _pack_rows(a):
    flat = a.reshape(-1)
    pad = (-flat.shape[0]) % LANES
    if pad:
        flat = jnp.pad(flat, (0, pad))
    return flat.reshape(-1, LANES)


def _pack(values):
    rows = jnp.concatenate([_pack_rows(values[k]) for k in _SMALL], axis=0)
    pad = (-rows.shape[0]) % 8
    if pad:
        rows = jnp.pad(rows, ((0, pad), (0, 0)))
    return rows


def _unpack(rows, like):
    out, at = {}, 0
    for k in _SMALL:
        size = like[k].size
        nrows = -(-size // LANES)
        out[k] = rows[at:at + nrows].reshape(-1)[:size].reshape(like[k].shape)
        at += nrows
    return out


def _rope_tables(t, wq, wk):
    pos = jnp.arange(t, dtype=F32)
    inv_freq = jnp.power(ROPE_THETA, -jnp.arange(0, HEAD_DIM, 2, dtype=F32) / HEAD_DIM)
    ang = pos[:, None] * inv_freq[None, :]
    cos, sin = jnp.cos(ang), jnp.sin(ang)
    cos2, sin2 = jnp.concatenate([cos, cos], axis=1), jnp.concatenate([-sin, sin], axis=1)
    return (jnp.tile(cos2, (1, wq // HEAD_DIM)), jnp.tile(sin2, (1, wq // HEAD_DIM)),
            jnp.tile(cos2, (1, wk // HEAD_DIM)), jnp.tile(sin2, (1, wk // HEAD_DIM)))


def kernel(x, mix_norm, w_in, q_norm, k_norm, sinks, sgu_ln_g, sgu_ln_b, w_spatial, b_spatial, w_attn_branch, w_sgu_branch, w_out, ffn_norm, w_gate, w_up, w_down, loss_target, m_mix_norm, m_w_in, m_q_norm, m_k_norm, m_sinks, m_sgu_ln_g, m_sgu_ln_b, m_w_spatial, m_b_spatial, m_w_attn_branch, m_w_sgu_branch, m_w_out, m_ffn_norm, m_w_gate, m_w_up, m_w_down, v_mix_norm, v_w_in, v_q_norm, v_k_norm, v_sinks, v_sgu_ln_g, v_sgu_ln_b, v_w_spatial, v_b_spatial, v_w_attn_branch, v_w_sgu_branch, v_w_out, v_ffn_norm, v_w_gate, v_w_up, v_w_down):
    names = ("mix_norm", "w_in", "q_norm", "k_norm", "sinks", "sgu_ln_g", "sgu_ln_b", "w_spatial", "b_spatial",
             "w_attn_branch", "w_sgu_branch", "w_out", "ffn_norm", "w_gate", "w_up", "w_down")
    weights = dict(zip(names, (mix_norm, w_in, q_norm, k_norm, sinks, sgu_ln_g, sgu_ln_b, w_spatial, b_spatial,
                               w_attn_branch, w_sgu_branch, w_out, ffn_norm, w_gate, w_up, w_down)))
    mom1 = dict(zip(names, (m_mix_norm, m_w_in, m_q_norm, m_k_norm, m_sinks, m_sgu_ln_g, m_sgu_ln_b, m_w_spatial,
                            m_b_spatial, m_w_attn_branch, m_w_sgu_branch, m_w_out, m_ffn_norm, m_w_gate, m_w_up,
                            m_w_down)))
    mom2 = dict(zip(names, (v_mix_norm, v_w_in, v_q_norm, v_k_norm, v_sinks, v_sgu_ln_g, v_sgu_ln_b, v_w_spatial,
                            v_b_spatial, v_w_attn_branch, v_w_sgu_branch, v_w_out, v_ffn_norm, v_w_gate, v_w_up,
                            v_w_down)))
    depth = w_in.shape[0]
    _, t, d = x.shape
    n_q_heads = sinks.shape[1]
    wq = n_q_heads * HEAD_DIM
    wk = wq // Q_PER_KV
    ws = sgu_ln_g.shape[1]
    ng = ws // LANES
    off_u = wq + 2 * wk
    off_g = off_u + 2 * ws
    tables = _rope_tables(t, wq, wk)
    px, py, pc = _place()
    core = pc.astype(jnp.int32)[None]
    chip = (2 * px + py).astype(jnp.int32)[None]
    dev = (4 * px + 2 * py + pc).astype(jnp.int32)[None]

    layers = range(depth)
    chunks = ((0,), (1, 2, 3), (4,), (5,))
    sources = [[jnp.swapaxes(w_in, 1, 2)], [jnp.swapaxes(w_attn_branch, 1, 2)], [jnp.swapaxes(w_sgu_branch, 1, 2)],
               [w_out], [jnp.swapaxes(w_gate, 1, 2), jnp.swapaxes(w_up, 1, 2)], [w_down]]
    stream = [(l, ci) for l in layers for ci in range(len(chunks))]
    placed, state, token = {}, {}, None

    def send(key, after):
        state[key] = _split_start(placed[key], GATHER_STAGES[0], "gather_send_%d_%d" % key, after)
        return state[key][3]

    def advance(key, after, stage):
        send_sems, recv_sems, bufs, _ = state[key]
        bufs = _split_wait(send_sems, recv_sems, bufs, GATHER_STAGES[stage - 1], after, "gather_wait%d_%d_%d" % (stage, *key))
        state[key] = _split_start(bufs, GATHER_STAGES[stage], "gather_pass%d_%d_%d" % (stage, *key))
        return state[key][3]

    def relay(key, after):
        tok = advance(key, after, 1)
        at = stream.index(key)
        for later in stream[at + 2:at + 3] if at else stream[1:3]:
            tok = send(later, tok)
        return tok

    def ready(key, after):
        send_sems, recv_sems, bufs, _ = state.pop(key)
        bufs = _split_wait(send_sems, recv_sems, bufs, GATHER_STAGES[2], after, "gather_wait3_%d_%d" % key)
        return [f.reshape(f.shape[0] * f.shape[1] * f.shape[2], f.shape[3]) for f in bufs]

    for key in stream:
        l, ci = key
        placed[key] = [_place_shard(sources[a], l, dev, BF16, f"place_shard_{l}_{a}",
                                    after=token if a == chunks[ci][0] else None) for a in chunks[ci]]
        token = send(key, None) if key == stream[0] else placed[key][-1]

    saved = []
    xl = x[0]
    going = relay((0, 0), token)
    going = advance((0, 0), going, 2)
    for l in layers:
        gq = jnp.tile(q_norm[l], n_q_heads)[None]
        gk = jnp.tile(k_norm[l], n_q_heads // Q_PER_KV)[None]
        bt = b_spatial[l].T
        h = _rmsnorm_fwd(xl, mix_norm[l][None], f"mix_norm_fwd_{l}", after=going)
        (win_t,) = ready((l, 0), h)
        proj = _mm(h, win_t, "nt", F32, f"in_proj_{l}")
        going = relay((l, 1), proj)
        attn = _attn_fwd(proj, tables, gq, gk, sinks[l], wq, wk, f"attn_fwd_{l}", after=going)
        going = advance((l, 1), attn, 2)
        sgu = _sgu_fwd(proj, sgu_ln_g[l][None], sgu_ln_b[l][None], w_spatial[l], bt, off_u, ws, f"sgu_fwd_{l}",
                       after=going)
        wab_t, wsb_t, wo = ready((l, 1), sgu)
        br_a, br_b, merged = _branches_fwd(attn, sgu, wab_t, wsb_t, proj, off_g, f"branches_{l}")
        going = relay((l, 2), merged)
        x1 = _mm(merged, wo, "nn", F32, f"out_proj_{l}", residual=xl, after=going)
        going = advance((l, 2), x1, 2)
        h2 = _rmsnorm_fwd(x1, ffn_norm[l][None], f"ffn_norm_fwd_{l}", after=going)
        (wgu_t,) = ready((l, 2), h2)
        going = relay((l, 3), h2)
        gu, act = _gate_up_fwd(h2, wgu_t, f"gate_up_{l}", after=going)
        going = advance((l, 3), act, 2)
        if l + 1 < depth:
            going = relay((l + 1, 0), going)
        (wd,) = ready((l, 3), going)
        x2 = _mm(act, wd, "nn", F32, f"down_proj_{l}", residual=x1)
        if l + 1 < depth:
            going = advance((l + 1, 0), x2, 2)
        saved.append(dict(x0=xl, h=h, proj=proj, attn=attn, sgu=sgu, br_a=br_a, br_b=br_b, merged=merged, x1=x1,
                          h2=h2, gu=gu, act=act, gq=gq, gk=gk, bt=bt, win_t=win_t, wab_t=wab_t, wsb_t=wsb_t, wo=wo,
                          wgu_t=wgu_t, wd=wd))
        xl = x2

    loss_part, dx, dx16 = _loss_and_grad(xl, loss_target[0], "loss")
    loss = lax.psum(loss_part[0, 0], ("x", "y", "c"))

    def sibling_start(grads, tag):
        shaped = []
        for g, p in grads:
            rows, c = g.shape
            shaped.append(g.reshape(p, 4, 2, rows // (8 * p), c))
        send_sems, recv_sems, shaped, lands, tok = _sibling_start(shaped, f"rs_sibling_start_{tag}")
        return (send_sems, recv_sems, shaped, lands, tag), tok

    def chips_start(state, after):
        send_sems, recv_sems, shaped, lands, tag = state
        shaped, lands = _sibling_wait(send_sems, recv_sems, shaped, lands, after, f"rs_sibling_wait_{tag}")
        sums = [_sum_sibling(g, o, core, f"rs_add_sibling_{tag}_{a}") for a, (g, o) in enumerate(zip(shaped, lands))]
        send_sems, recv_sems, sums, lands, tok = _chips_start(sums, f"rs_chips_start_{tag}")
        return (send_sems, recv_sems, sums, lands, tag), tok

    def scatter_finish(state, after):
        send_sems, recv_sems, sums, lands, tag = state
        sums, lands = _chips_wait(send_sems, recv_sems, sums, lands, after, f"rs_chips_wait_{tag}")
        return [[s] + o for s, o in zip(sums, lands)]

    in_flight = [dict() for _ in layers]
    small_grads = [None] * depth
    tok, swap_in = None, None
    for l in reversed(layers):
        s = saved[l]
        dgu = _gate_up_bwd(dx16, s["wd"], s["gu"], f"d_gate_up_{l}", after=tok)
        if swap_in is not None:
            in_flight[l + 1]["in"], tok = chips_start(swap_in, dgu)
        g_wd = _mm(s["act"], dx16, "tn", BF16, f"g_w_down_{l}", after=tok)
        swap, tok_s = sibling_start([(g_wd, 1)], f"{l}_down")
        dh2 = _mm(dgu, s["wgu_t"], "nn", F32, f"d_h2_{l}", after=tok_s)
        in_flight[l]["down"], tok = chips_start(swap, dh2)
        g_wgu_t = _mm(dgu, s["h2"], "tn", BF16, f"g_w_gate_up_{l}", after=tok)
        swap, tok_s = sibling_start([(g_wgu_t, 2)], f"{l}_gate_up")
        dx1, dx1_16, g_ffn = _rmsnorm_bwd(s["x1"], ffn_norm[l][None], dh2, dx, f"ffn_norm_bwd_{l}", after=tok_s)
        d_a, d_b, dla, dlb = _branches_bwd(dx1_16, s["wo"], s["br_a"], s["br_b"], s["proj"], off_g,
                                           f"d_branches_{l}")
        in_flight[l]["gate_up"], tok = chips_start(swap, d_a)
        g_wo = _mm(s["merged"], dx1_16, "tn", BF16, f"g_w_out_{l}", after=tok)
        dattn = _mm(d_a, s["wab_t"], "nn", F32, f"d_attn_{l}", after=g_wo)
        g_wab_t = _mm(d_a, s["attn"], "tn", BF16, f"g_w_attn_branch_{l}")
        dsgu = _mm(d_b, s["wsb_t"], "nn", F32, f"d_sgu_{l}")
        g_wsb_t = _mm(d_b, s["sgu"], "tn", BF16, f"g_w_sgu_branch_{l}")
        swap, tok_s = sibling_start([(g_wab_t, 1), (g_wsb_t, 1), (g_wo, 1)], f"{l}_mix")
        dq, dk, dv, g_gq, g_gk, g_sinks = _attn_bwd(s["proj"], dattn, tables, s["gq"], s["gk"], sinks[l], wq, wk,
                                                    f"attn_bwd_{l}", after=tok_s)
        du, dvv, g_lng, g_lnb, g_ws, g_bs = _sgu_bwd(s["proj"], dsgu, sgu_ln_g[l][None], sgu_ln_b[l][None],
                                                     w_spatial[l], s["bt"], off_u, ws, f"sgu_bwd_{l}")
        dproj = jnp.concatenate([dq, dk.astype(BF16), dv.astype(BF16), du, dvv, dla, dlb], axis=1)
        dh = _mm(dproj, s["win_t"], "nn", F32, f"d_h_{l}")
        in_flight[l]["mix"], tok = chips_start(swap, dh)
        g_win_t = _mm(dproj, s["h"], "tn", BF16, f"g_w_in_{l}", after=tok)
        swap_in, tok = sibling_start([(g_win_t, 1)], f"{l}_in")
        dx, dx16, g_mix = _rmsnorm_bwd(s["x0"], mix_norm[l][None], dh, dx1, f"mix_norm_bwd_{l}", after=tok)
        small_grads[l] = dict(
            mix_norm=g_mix[0], q_norm=g_gq[0].reshape(n_q_heads, HEAD_DIM).sum(0),
            k_norm=g_gk[0].reshape(n_q_heads // Q_PER_KV, HEAD_DIM).sum(0), sinks=g_sinks[0, :n_q_heads],
            sgu_ln_g=g_lng[0], sgu_ln_b=g_lnb[0], w_spatial=g_ws, b_spatial=g_bs[:, 0, :], ffn_norm=g_ffn[0])
    grad_x = dx[None]

    result = {key: {} for key in ("grad", "delta", "m", "v")}
    layer_like = {k: weights[k][0] for k in _SMALL}
    packed_g = jnp.concatenate([_pack(small_grads[l]) for l in layers], axis=0)
    rows_per_layer = packed_g.shape[0] // depth
    small_buf = _place_shard([packed_g[None]], 0, dev, F32, "place_small_grads", after=tok)
    send_sems, recv_sems, small_bufs, tok = _split_start([small_buf], GATHER_STAGES[0], "gather_send_small")
    in_flight[0]["in"], tok = chips_start(swap_in, tok)

    def update(k, grads, transposed, after):
        view = (lambda a: jnp.swapaxes(a, 1, 2)) if transposed else (lambda a: a)
        outs = _adam(view(weights[k]), grads, view(mom1[k]), view(mom2[k]), chip, f"adam_{k}", after=after)
        for key, val in zip(("grad", "delta", "m", "v"), outs):
            result[key][k] = view(val)
        return outs[3]

    def plain(terms, tag):
        s, lands = terms[0], terms[1:]
        g = _sum_chips(s, lands, chip, f"rs_add_chips_{tag}")
        return [jnp.swapaxes(g, 1, 2)[:, None]]

    down = [scatter_finish(in_flight[l]["down"], tok) for l in reversed(layers)][::-1]
    tok = update("w_down", [(down[l][0], 0) for l in layers], False, None)
    gate_up = [scatter_finish(in_flight[l]["gate_up"], tok) for l in reversed(layers)][::-1]
    tok = update("w_gate", [(gate_up[l][0], 0) for l in layers], True, None)
    tok = update("w_up", [(gate_up[l][0], 1) for l in layers], True, tok)
    mix = [scatter_finish(in_flight[l]["mix"], tok) for l in reversed(layers)][::-1]
    tok = update("w_out", [(mix[l][2], 0) for l in layers], False, None)
    tok = update("w_attn_branch", [(plain(mix[l][0], f"{l}_attn_branch"), 0) for l in layers], False, tok)
    tok = update("w_sgu_branch", [(plain(mix[l][1], f"{l}_sgu_branch"), 0) for l in layers], False, tok)

    for stage in (1, 2):
        small_bufs = _split_wait(send_sems, recv_sems, small_bufs, GATHER_STAGES[stage - 1], tok,
                                 f"gather_wait{stage}_small")
        send_sems, recv_sems, small_bufs, tok = _split_start(small_bufs, GATHER_STAGES[stage],
                                                             f"gather_pass{stage}_small")
    packed = [jnp.concatenate([_pack({k: src[k][l] for k in _SMALL}) for l in layers], axis=0)
              for src in (weights, mom1, mom2)]
    (gathered_small,) = _split_wait(send_sems, recv_sems, small_bufs, GATHER_STAGES[2], packed[0],
                                    "gather_wait3_small")
    small = _small_reduce_adam(gathered_small[0], *packed, "small_reduce_adam")
    for key, rows in zip(("grad", "delta", "m", "v"), small):
        per_layer = [_unpack(rows[l * rows_per_layer:(l + 1) * rows_per_layer], layer_like) for l in layers]
        for k in _SMALL:
            result[key][k] = jnp.stack([per_layer[l][k] for l in layers])

    last = [scatter_finish(in_flight[l]["in"], small[0]) for l in reversed(layers)][::-1]
    update("w_in", [(last[l][0], 0) for l in layers], True, None)

    return (loss, grad_x, *[result["grad"][k] for k in names], *[result["delta"][k] for k in names],
            *[result["m"][k] for k in names], *[result["v"][k] for k in names])
```

```python
import functools
import math

import jax
import jax.numpy as jnp
from jax import lax
from jax.experimental import pallas as pl
from jax.experimental.pallas import tpu as pltpu

F32 = jnp.float32
BF16 = jnp.bfloat16
MESH = pl.DeviceIdType.MESH
ANY = pl.BlockSpec(memory_space=pl.ANY)

N_DEV = 8
HEAD_DIM = 64
Q_PER_KV = 4
BLOCK = 128
LANES = 128
ROPE_THETA = 10000.0
EPS = 1e-6
ADAM_LR = 0.001
ADAM_B1 = 0.9
ADAM_B2 = 0.999
ADAM_EPS = 1e-08
ADAM_WD = 0.01
ADAM_STEP = 10
NEG = -1e30
VMEM_LIMIT_BYTES = 56 * 1024 * 1024

NN = ((1,), (0,))
NT = ((1,), (1,))
TN = ((0,), (0,))


def _dot(a, b, dims):
    return lax.dot_general(a, b, (dims, ((), ())), preferred_element_type=F32)


def _params(*sem):
    return pltpu.CompilerParams(dimension_semantics=sem, vmem_limit_bytes=VMEM_LIMIT_BYTES)


def _divisor_tile(n, limit, unit):
    if n <= limit:
        return n
    best = unit
    for t in range(unit, limit + 1, unit):
        if n % t == 0:
            best = t
    assert n % best == 0, (n, limit, unit)
    return best


def _row_chunks(rows, size=256):
    size = min(size, rows)
    assert rows % size == 0, (rows, size)
    return [pl.ds(start, size) for start in range(0, rows, size)]


def _mm(a, b, mode, out_dtype, name, residual=None, after=None):
    parts = a.shape[0] if a.ndim == 3 else 1
    a2 = a.shape[-2:]
    if mode == "nn":
        (m, kp), (k2, n) = a2, b.shape
        k, mp = kp * parts, m
    elif mode == "nt":
        (m, kp), (n, k2) = a2, b.shape
        k, mp = kp * parts, m
    else:
        (k, mp), (k2, n) = a2, b.shape
        m, kp = mp * parts, k
    assert k == k2, (name, a.shape, b.shape)
    tk = _divisor_tile(kp, 2816, 128)
    nk = k // tk
    tm = _divisor_tile(mp, 512 if mode == "tn" else 1024, 128)
    tn = _divisor_tile(n, 2048 if mode == "tn" else 1024, 128)
    kpb, mpb = kp // tk, mp // tm
    dims = {"nn": NN, "nt": NT, "tn": TN}[mode]
    lead = (None,) if a.ndim == 3 else ()
    if mode == "tn":
        a_index = lambda i, j, kk: (i // mpb, kk, i % mpb) if lead else (kk, i)
        a_spec = pl.BlockSpec(lead + (tk, tm), a_index)
    else:
        a_index = lambda i, j, kk: (kk // kpb, i, kk % kpb) if lead else (i, kk)
        a_spec = pl.BlockSpec(lead + (tm, tk), a_index)
    if mode == "nt":
        b_spec = pl.BlockSpec((tn, tk), lambda i, j, kk: (j, kk))
    else:
        b_spec = pl.BlockSpec((tk, tn), lambda i, j, kk: (kk, j))
    o_spec = pl.BlockSpec((tm, tn), lambda i, j, kk: (i, j))
    has_res = residual is not None

    def body(*refs):
        a_ref, b_ref = refs[:2]
        r_ref = refs[2] if has_res else None
        o_ref, acc_ref = refs[-2:]
        kk = pl.program_id(2)
        p = _dot(a_ref[...], b_ref[...], dims)

        def finish(total):
            if has_res:
                total = total + r_ref[...]
            o_ref[...] = total.astype(o_ref.dtype)

        if nk == 1:
            finish(p)
        else:
            @pl.when(kk == 0)
            def _():
                acc_ref[...] = p

            @pl.when(jnp.logical_and(kk > 0, kk < nk - 1))
            def _():
                acc_ref[...] += p

            @pl.when(kk == nk - 1)
            def _():
                finish(acc_ref[...] + p)

    in_specs = [a_spec, b_spec] + ([o_spec] if has_res else []) + ([ANY] if after is not None else [])
    args = (a, b) + ((residual,) if has_res else ()) + ((after,) if after is not None else ())
    acc_shape = (tm, tn) if nk > 1 else (8, LANES)
    return pl.pallas_call(
        body,
        name=name,
        grid=(m // tm, n // tn, nk),
        in_specs=in_specs,
        out_specs=o_spec,
        out_shape=jax.ShapeDtypeStruct((m, n), out_dtype),
        scratch_shapes=[pltpu.VMEM(acc_shape, F32)],
        compiler_params=_params("parallel", "parallel", "arbitrary"),
    )(*args)


def _rmsnorm_fwd(x, g, name, after=None):
    t, d = x.shape
    tr = _divisor_tile(t, 512, 8)

    def body(x_ref, g_ref, *rest):
        h_ref = rest[-1]
        xv = x_ref[...]
        rstd = lax.rsqrt(jnp.mean(xv * xv, axis=-1, keepdims=True) + EPS)
        h_ref[...] = (xv * rstd * g_ref[...]).astype(h_ref.dtype)

    return pl.pallas_call(
        body,
        name=name,
        grid=(t // tr,),
        in_specs=[pl.BlockSpec((tr, d), lambda i: (i, 0)), pl.BlockSpec((1, d), lambda i: (0, 0))]
        + ([ANY] if after is not None else []),
        out_specs=pl.BlockSpec((tr, d), lambda i: (i, 0)),
        out_shape=jax.ShapeDtypeStruct((t, d), BF16),
        compiler_params=_params("parallel"),
    )(x, g, *(() if after is None else (after,)))


def _rmsnorm_bwd(x, g, dh, dres, name, after=None):
    t, d = x.shape
    tr = _divisor_tile(t, 512, 8)

    def body(x_ref, g_ref, dh_ref, dres_ref, *rest):
        dx_ref, dx16_ref, dg_ref = rest[-3:]
        i = pl.program_id(0)
        xv = x_ref[...]
        rstd = lax.rsqrt(jnp.mean(xv * xv, axis=-1, keepdims=True) + EPS)
        xh = xv * rstd
        dhv = dh_ref[...]
        dxh = dhv * g_ref[...]
        dx = dres_ref[...] + rstd * (dxh - xh * jnp.mean(dxh * xh, axis=-1, keepdims=True))
        dx_ref[...] = dx
        dx16_ref[...] = dx.astype(dx16_ref.dtype)
        part = jnp.broadcast_to(jnp.sum(dhv * xh, axis=0, keepdims=True), dg_ref.shape)

        @pl.when(i == 0)
        def _():
            dg_ref[...] = part

        @pl.when(i > 0)
        def _():
            dg_ref[...] += part

    row = pl.BlockSpec((tr, d), lambda i: (i, 0))
    return pl.pallas_call(
        body,
        name=name,
        grid=(t // tr,),
        in_specs=[row, pl.BlockSpec((1, d), lambda i: (0, 0)), row, row] + ([ANY] if after is not None else []),
        out_specs=[row, row, pl.BlockSpec((8, d), lambda i: (0, 0))],
        out_shape=[jax.ShapeDtypeStruct((t, d), F32), jax.ShapeDtypeStruct((t, d), BF16),
                   jax.ShapeDtypeStruct((8, d), F32)],
        compiler_params=_params("arbitrary"),
    )(x, g, dh, dres, *(() if after is None else (after,)))


def _lane(shape):
    return lax.broadcasted_iota(jnp.int32, shape, 1)


def _group_sum64(s):
    row = lax.broadcasted_iota(jnp.int32, (LANES, LANES), 0)
    col = lax.broadcasted_iota(jnp.int32, (LANES, LANES), 1)
    ones = jnp.where((row >= HEAD_DIM) == (col >= HEAD_DIM), 1.0, 0.0).astype(BF16)
    out = []
    for t in range(s.shape[1] // LANES):
        piece = s[:, LANES * t:LANES * t + LANES]
        hi = piece.astype(BF16)
        lo = (piece - hi.astype(F32)).astype(BF16)
        out.append(_dot(hi, ones, NN) + _dot(lo, ones, NN))
    return out[0] if len(out) == 1 else jnp.concatenate(out, axis=1)


def _swap32(x):
    w = x.shape[1]
    return jnp.where((_lane(x.shape) & 32) == 0, pltpu.roll(x, w - 32, axis=1), pltpu.roll(x, 32, axis=1))


def _rope(x, c, s):
    return x * c + _swap32(x) * s


def _rope_t(dy, c, s):
    return dy * c + _swap32(dy * s)


def _head_norm(x):
    rstd = lax.rsqrt(_group_sum64(x * x) * (1.0 / HEAD_DIM) + EPS)
    return x * rstd, rstd


def _head_norm_bwd(dxh, xh, rstd):
    return rstd * (dxh - xh * (_group_sum64(dxh * xh) * (1.0 / HEAD_DIM)))


def _roll64(x):
    return pltpu.roll(x, 64, axis=1)


def _attn_specs(wq, wk):
    kb = wq // wk
    prev = lambda i: jnp.maximum(i - 1, 0)
    return dict(
        q=pl.BlockSpec((BLOCK, wq), lambda i: (i, 0)),
        kc=pl.BlockSpec((BLOCK, wk), lambda i: (i, kb)),
        kp=pl.BlockSpec((BLOCK, wk), lambda i: (prev(i), kb)),
        vc=pl.BlockSpec((BLOCK, wk), lambda i: (i, kb + 1)),
        vp=pl.BlockSpec((BLOCK, wk), lambda i: (prev(i), kb + 1)),
        tq=pl.BlockSpec((BLOCK, wq), lambda i: (i, 0)),
        tkp=pl.BlockSpec((BLOCK, wk), lambda i: (prev(i), 0)),
        gq=pl.BlockSpec((1, wq), lambda i: (0, 0)),
        gk=pl.BlockSpec((1, wk), lambda i: (0, 0)),
        sinks=pl.BlockSpec(memory_space=pltpu.SMEM),
    )


def _attn_prologue(i, q_ref, kc_ref, kp_ref, cq_ref, sq_ref, ckp_ref, skp_ref, gq_ref, gk_ref):
    wk = kc_ref.shape[1]
    cq, sq = cq_ref[...], sq_ref[...]
    ck, sk = cq[:, :wk], sq[:, :wk]
    qh, q_rstd = _head_norm(q_ref[...])
    kch, kc_rstd = _head_norm(kc_ref[...])
    kph, kp_rstd = _head_norm(kp_ref[...])
    qn = _rope(qh * gq_ref[...], cq, sq)
    knc = _rope(kch * gk_ref[...], ck, sk)
    knp = _rope(kph * gk_ref[...], ckp_ref[...], skp_ref[...])
    stacked = (Q_PER_KV * BLOCK, BLOCK)
    row = lax.broadcasted_iota(jnp.int32, stacked, 0) & (BLOCK - 1)
    col = lax.broadcasted_iota(jnp.int32, stacked, 1)
    mask_c = col <= row
    mask_p = jnp.logical_and(col > row, i > 0)
    half = (lax.broadcasted_iota(jnp.int32, (BLOCK, BLOCK), 1) >= HEAD_DIM).astype(jnp.int32)
    return dict(cq=cq, sq=sq, ck=ck, sk=sk, qh=qh, q_rstd=q_rstd, kch=kch, kc_rstd=kc_rstd, kph=kph,
                kp_rstd=kp_rstd, qn=qn, knc=knc, knp=knp, mask_c=mask_c, mask_p=mask_p, half=half)


def _stack_heads(x, g, half):
    kpar = g % 2
    pieces = []
    for j in range(Q_PER_KV):
        t, e = divmod(Q_PER_KV * g + j, 2)
        piece = jnp.where(half == e, x[:, LANES * t:LANES * t + LANES], 0.0)
        pieces.append(piece if e == kpar else _roll64(piece))
    return jnp.concatenate(pieces, axis=0)


def _unstack_heads(y, g, half):
    kpar = g % 2
    slabs = {}
    for j in range(Q_PER_KV):
        t, e = divmod(Q_PER_KV * g + j, 2)
        piece = jnp.where(half == kpar, y[BLOCK * j:BLOCK * j + BLOCK], 0.0)
        piece = piece if e == kpar else _roll64(piece)
        slabs[t] = piece if t not in slabs else slabs[t] + piece
    return slabs


def _group_scores(st, g, sinks_ref, scale):
    ks = g // 2
    sl = slice(LANES * ks, LANES * ks + LANES)
    q4 = _stack_heads(st["qn"], g, st["half"]).astype(BF16)
    kc, kp = st["knc"][:, sl].astype(BF16), st["knp"][:, sl].astype(BF16)
    rows = Q_PER_KV * BLOCK
    at = lax.broadcasted_iota(jnp.int32, (rows, 1), 0)
    head = jnp.zeros((rows, 1), jnp.int32)
    sink = jnp.zeros((rows, 1), F32) + sinks_ref[Q_PER_KV * g]
    for j in range(1, Q_PER_KV):
        head = jnp.where(at >= BLOCK * j, j, head)
        sink = jnp.where(at >= BLOCK * j, sinks_ref[Q_PER_KV * g + j], sink)
    s_c = jnp.where(st["mask_c"], _dot(q4, kc, NT) * scale, NEG)
    s_p = jnp.where(st["mask_p"], _dot(q4, kp, NT) * scale, NEG)
    m = jnp.maximum(jnp.maximum(jnp.max(s_c, axis=1, keepdims=True), jnp.max(s_p, axis=1, keepdims=True)), sink)
    p_c, p_p = jnp.exp(s_c - m), jnp.exp(s_p - m)
    p_s = jnp.exp(sink - m)
    inv = 1.0 / (jnp.sum(p_c, axis=1, keepdims=True) + jnp.sum(p_p, axis=1, keepdims=True) + p_s)
    return dict(sl=sl, head=head, q4=q4, kc=kc, kp=kp, pr_c=p_c * inv, pr_p=p_p * inv, pr_s=p_s * inv)


def _attn_fwd(proj, tables, gq, gk, sinks, wq, wk, name, after=None):
    t = proj.shape[0]
    nb = t // BLOCK
    sp = _attn_specs(wq, wk)
    scale = HEAD_DIM ** -0.5
    cos_t, sin_t, cos_k, sin_k = tables

    def body(sinks_ref, q_ref, kc_ref, kp_ref, vc_ref, vp_ref, cq_ref, sq_ref, ckp_ref, skp_ref, gq_ref, gk_ref,
             *rest):
        o_ref = rest[-1]
        i = pl.program_id(0)
        st = _attn_prologue(i, q_ref, kc_ref, kp_ref, cq_ref, sq_ref, ckp_ref, skp_ref, gq_ref, gk_ref)
        for g in range(wq // (Q_PER_KV * HEAD_DIM)):
            gs = _group_scores(st, g, sinks_ref, scale)
            own = st["half"] == g % 2
            vc = jnp.where(own, vc_ref[:, gs["sl"]], 0.0).astype(BF16)
            vp = jnp.where(own, vp_ref[:, gs["sl"]], 0.0).astype(BF16)
            out = _dot(gs["pr_c"].astype(BF16), vc, NN) + _dot(gs["pr_p"].astype(BF16), vp, NN)
            for ts, slab in _unstack_heads(out, g, st["half"]).items():
                o_ref[:, LANES * ts:LANES * ts + LANES] = slab.astype(o_ref.dtype)

    return pl.pallas_call(
        body,
        name=name,
        grid=(nb,),
        in_specs=[sp["sinks"], sp["q"], sp["kc"], sp["kp"], sp["vc"], sp["vp"], sp["tq"], sp["tq"], sp["tkp"],
                  sp["tkp"], sp["gq"], sp["gk"]] + ([ANY] if after is not None else []),
        out_specs=pl.BlockSpec((BLOCK, wq), lambda i: (i, 0)),
        out_shape=jax.ShapeDtypeStruct((t, wq), BF16),
        compiler_params=_params("parallel"),
    )(sinks, proj, proj, proj, proj, proj, cos_t, sin_t, cos_k, sin_k, gq, gk, *(() if after is None else (after,)))


def _attn_bwd(proj, dout, tables, gq, gk, sinks, wq, wk, name, after=None):
    t = proj.shape[0]
    nb = t // BLOCK
    sp = _attn_specs(wq, wk)
    scale = HEAD_DIM ** -0.5
    cos_t, sin_t, cos_k, sin_k = tables

    def body(sinks_ref, q_ref, kc_ref, kp_ref, vc_ref, vp_ref, cq_ref, sq_ref, ckp_ref, skp_ref, gq_ref, gk_ref,
             do_ref, *rest):
        dq_ref, dk_ref, dv_ref, dgq_ref, dgk_ref, dsk_ref, dqn_ref, dknc_ref, dknp_ref, dvc_ref, dvp_ref = rest[-11:]
        i = pl.program_id(0)
        st = _attn_prologue(i, q_ref, kc_ref, kp_ref, cq_ref, sq_ref, ckp_ref, skp_ref, gq_ref, gk_ref)
        dknc_ref[...] = jnp.zeros_like(dknc_ref)
        dknp_ref[...] = jnp.zeros_like(dknp_ref)
        dvc_ref[...] = jnp.zeros_like(dvc_ref)
        dvp_ref[...] = jnp.zeros_like(dvp_ref)
        lane8 = _lane((8, LANES))
        dsinks = jnp.zeros((8, LANES), F32)
        for g in range(wq // (Q_PER_KV * HEAD_DIM)):
            gs = _group_scores(st, g, sinks_ref, scale)
            sl = gs["sl"]
            do4 = _stack_heads(do_ref[...], g, st["half"]).astype(BF16)
            dp_c = _dot(do4, vc_ref[:, sl].astype(BF16), NT)
            dp_p = _dot(do4, vp_ref[:, sl].astype(BF16), NT)
            pr_c, pr_p = gs["pr_c"], gs["pr_p"]
            rs = jnp.sum(pr_c * dp_c, axis=1, keepdims=True) + jnp.sum(pr_p * dp_p, axis=1, keepdims=True)
            ds_c = (pr_c * (dp_c - rs) * scale).astype(BF16)
            ds_p = (pr_p * (dp_p - rs) * scale).astype(BF16)
            dsink_rows = -gs["pr_s"] * rs
            for j in range(Q_PER_KV):
                dsink = jnp.sum(jnp.where(gs["head"] == j, dsink_rows, 0.0))
                dsinks = dsinks + jnp.where(lane8 == Q_PER_KV * g + j, dsink, 0.0)
            dq4 = _dot(ds_c, gs["kc"], NN) + _dot(ds_p, gs["kp"], NN)
            for ts, slab in _unstack_heads(dq4, g, st["half"]).items():
                dqn_ref[:, LANES * ts:LANES * ts + LANES] = slab
            dvc_ref[:, sl] += _dot(pr_c.astype(BF16), do4, TN)
            dvp_ref[:, sl] += _dot(pr_p.astype(BF16), do4, TN)
            dknc_ref[:, sl] += _dot(ds_c, gs["q4"], TN)
            dknp_ref[:, sl] += _dot(ds_p, gs["q4"], TN)

        gqv, gkv = gq_ref[...], gk_ref[...]
        dqg = _rope_t(dqn_ref[...], st["cq"], st["sq"])
        dq_ref[...] = _head_norm_bwd(dqg * gqv, st["qh"], st["q_rstd"]).astype(dq_ref.dtype)
        dkcg = _rope_t(dknc_ref[...], st["ck"], st["sk"])
        dkpg = _rope_t(dknp_ref[...], ckp_ref[...], skp_ref[...])
        dk_cur = _head_norm_bwd(dkcg * gkv, st["kch"], st["kc_rstd"])
        dk_prev = _head_norm_bwd(dkpg * gkv, st["kph"], st["kp_rstd"])
        dgq_part = jnp.broadcast_to(jnp.sum(dqg * st["qh"], axis=0, keepdims=True), dgq_ref.shape)
        dgk_part = jnp.broadcast_to(
            jnp.sum(dkcg * st["kch"] + dkpg * st["kph"], axis=0, keepdims=True), dgk_ref.shape)
        cur = pl.ds(pl.multiple_of(i * BLOCK, BLOCK), BLOCK)
        dk_ref[cur, :] = dk_cur
        dv_ref[cur, :] = dvc_ref[...]

        @pl.when(i == 0)
        def _():
            dgq_ref[...] = dgq_part
            dgk_ref[...] = dgk_part
            dsk_ref[...] = dsinks

        @pl.when(i > 0)
        def _():
            before = pl.ds(pl.multiple_of((i - 1) * BLOCK, BLOCK), BLOCK)
            dk_ref[before, :] += dk_prev
            dv_ref[before, :] += dvp_ref[...]
            dgq_ref[...] += dgq_part
            dgk_ref[...] += dgk_part
            dsk_ref[...] += dsinks

    whole = lambda shape: pl.BlockSpec(shape, lambda i: (0, 0))
    return pl.pallas_call(
        body,
        name=name,
        grid=(nb,),
        in_specs=[sp["sinks"], sp["q"], sp["kc"], sp["kp"], sp["vc"], sp["vp"], sp["tq"], sp["tq"], sp["tkp"],
                  sp["tkp"], sp["gq"], sp["gk"], pl.BlockSpec((BLOCK, wq), lambda i: (i, 0))]
        + ([ANY] if after is not None else []),
        out_specs=[pl.BlockSpec((BLOCK, wq), lambda i: (i, 0)), whole((t, wk)), whole((t, wk)), whole((8, wq)),
                   whole((8, wk)), whole((8, LANES))],
        out_shape=[jax.ShapeDtypeStruct((t, wq), BF16), jax.ShapeDtypeStruct((t, wk), F32),
                   jax.ShapeDtypeStruct((t, wk), F32), jax.ShapeDtypeStruct((8, wq), F32),
                   jax.ShapeDtypeStruct((8, wk), F32), jax.ShapeDtypeStruct((8, LANES), F32)],
        scratch_shapes=[pltpu.VMEM((BLOCK, wq), F32), pltpu.VMEM((BLOCK, wk), F32), pltpu.VMEM((BLOCK, wk), F32),
                        pltpu.VMEM((BLOCK, wk), F32), pltpu.VMEM((BLOCK, wk), F32)],
        compiler_params=_params("arbitrary"),
    )(sinks, proj, proj, proj, proj, proj, cos_t, sin_t, cos_k, sin_k, gq, gk, dout,
      *(() if after is None else (after,)))


_GELU_K = math.sqrt(2.0 / math.pi)
_GELU_A = 0.044715


def _gelu(x):
    return 0.5 * x * (1.0 + jnp.tanh(_GELU_K * (x + _GELU_A * x * x * x)))


def _gelu_grad(x):
    th = jnp.tanh(_GELU_K * (x + _GELU_A * x * x * x))
    return 0.5 * (1.0 + th) + 0.5 * x * (1.0 - th * th) * (_GELU_K * (1.0 + 3.0 * _GELU_A * x * x))


def _group_ln(v):
    mu = jnp.mean(v, axis=1, keepdims=True)
    cen = v - mu
    rstd = lax.rsqrt(jnp.mean(cen * cen, axis=1, keepdims=True) + EPS)
    return cen * rstd, rstd


def _sgu_geometry(off_u, ws):
    cw = math.gcd(off_u, ws)
    return cw, ws // cw, off_u // cw, (off_u + ws) // cw


def _sgu_fwd(proj, ln_g, ln_b, w_s, bt, off_u, ws, name, after=None):
    t = proj.shape[0]
    nb = t // BLOCK
    cw, nc, ub, vb = _sgu_geometry(off_u, ws)
    gpc = cw // LANES
    ng = ws // LANES

    def body(u_ref, v_ref, g_ref, b_ref, w_ref, bt_ref, *rest):
        o_ref = rest[-1]
        jc = pl.program_id(0)
        row = lax.broadcasted_iota(jnp.int32, (BLOCK, BLOCK), 0)
        col = lax.broadcasted_iota(jnp.int32, (BLOCK, BLOCK), 1)
        lane_g = _lane((BLOCK, ng))
        for gi in range(gpc):
            sl = slice(LANES * gi, LANES * gi + LANES)
            xh, _ = _group_ln(_gelu(v_ref[:, sl]))
            vn = xh * g_ref[:, sl] + b_ref[:, sl]
            w = jnp.where(row >= col, w_ref[gi], 0.0).astype(BF16)
            bias = jnp.sum(jnp.where(lane_g == jc * gpc + gi, bt_ref[...], 0.0), axis=1, keepdims=True)
            s = _dot(w, vn.astype(BF16), NN) + bias
            o_ref[:, sl] = (_gelu(u_ref[:, sl]) * s).astype(o_ref.dtype)

    return pl.pallas_call(
        body,
        name=name,
        grid=(nc, nb),
        in_specs=[pl.BlockSpec((BLOCK, cw), lambda jc, i: (i, ub + jc)),
                  pl.BlockSpec((BLOCK, cw), lambda jc, i: (i, vb + jc)),
                  pl.BlockSpec((1, cw), lambda jc, i: (0, jc)),
                  pl.BlockSpec((1, cw), lambda jc, i: (0, jc)),
                  pl.BlockSpec((gpc, BLOCK, BLOCK), lambda jc, i: (jc, 0, 0)),
                  pl.BlockSpec((BLOCK, ng), lambda jc, i: (0, 0))] + ([ANY] if after is not None else []),
        out_specs=pl.BlockSpec((BLOCK, cw), lambda jc, i: (i, jc)),
        out_shape=jax.ShapeDtypeStruct((t, ws), BF16),
        compiler_params=_params("parallel", "parallel"),
    )(proj, proj, ln_g, ln_b, w_s, bt, *(() if after is None else (after,)))


def _sgu_bwd(proj, dout, ln_g, ln_b, w_s, bt, off_u, ws, name):
    t = proj.shape[0]
    nb = t // BLOCK
    cw, nc, ub, vb = _sgu_geometry(off_u, ws)
    gpc = cw // LANES
    ng = ws // LANES

    def body(u_ref, v_ref, g_ref, b_ref, w_ref, bt_ref, do_ref, du_ref, dv_ref, dg_ref, db_ref, dw_ref, dbs_ref,
             bacc_ref):
        jc = pl.program_id(0)
        i = pl.program_id(1)
        row = lax.broadcasted_iota(jnp.int32, (BLOCK, BLOCK), 0)
        col = lax.broadcasted_iota(jnp.int32, (BLOCK, BLOCK), 1)
        lane_g = _lane((BLOCK, ng))
        tri = row >= col

        @pl.when(i == 0)
        def _():
            dg_ref[...] = jnp.zeros_like(dg_ref)
            db_ref[...] = jnp.zeros_like(db_ref)
            dw_ref[...] = jnp.zeros_like(dw_ref)
            bacc_ref[...] = jnp.zeros_like(bacc_ref)

        for gi in range(gpc):
            sl = slice(LANES * gi, LANES * gi + LANES)
            u_raw, v_raw = u_ref[:, sl], v_ref[:, sl]
            xh, rstd = _group_ln(_gelu(v_raw))
            gam = g_ref[:, sl]
            vn = (xh * gam + b_ref[:, sl]).astype(BF16)
            w = jnp.where(tri, w_ref[gi], 0.0)
            bias = jnp.sum(jnp.where(lane_g == jc * gpc + gi, bt_ref[...], 0.0), axis=1, keepdims=True)
            s = _dot(w.astype(BF16), vn, NN) + bias
            dov = do_ref[:, sl]
            du_ref[:, sl] = (dov * s * _gelu_grad(u_raw)).astype(du_ref.dtype)
            ds = dov * _gelu(u_raw)
            ds16 = ds.astype(BF16)
            dw_ref[gi] += jnp.where(tri, _dot(ds16, vn, NT), 0.0)
            bacc_ref[gi] += ds
            dvn = _dot(w.T.astype(BF16), ds16, NN)
            dg_ref[:, sl] += jnp.broadcast_to(jnp.sum(dvn * xh, axis=0, keepdims=True), (8, LANES))
            db_ref[:, sl] += jnp.broadcast_to(jnp.sum(dvn, axis=0, keepdims=True), (8, LANES))
            dxh = dvn * gam
            dvg = rstd * (dxh - jnp.mean(dxh, axis=1, keepdims=True)
                          - xh * jnp.mean(dxh * xh, axis=1, keepdims=True))
            dv_ref[:, sl] = (dvg * _gelu_grad(v_raw)).astype(dv_ref.dtype)

        @pl.when(i == nb - 1)
        def _():
            for gi in range(gpc):
                dbs_ref[gi] = jnp.broadcast_to(jnp.sum(bacc_ref[gi].T, axis=0, keepdims=True), (8, LANES))

    blk = lambda base: pl.BlockSpec((BLOCK, cw), lambda jc, i: (i, base + jc))
    vec = pl.BlockSpec((1, cw), lambda jc, i: (0, jc))
    acc = pl.BlockSpec((8, cw), lambda jc, i: (0, jc))
    wsp = pl.BlockSpec((gpc, BLOCK, BLOCK), lambda jc, i: (jc, 0, 0))
    return pl.pallas_call(
        body,
        name=name,
        grid=(nc, nb),
        in_specs=[blk(ub), blk(vb), vec, vec, wsp, pl.BlockSpec((BLOCK, ng), lambda jc, i: (0, 0)), blk(0)],
        out_specs=[blk(0), blk(0), acc, acc, wsp, pl.BlockSpec((gpc, 8, LANES), lambda jc, i: (jc, 0, 0))],
        out_shape=[jax.ShapeDtypeStruct((t, ws), BF16), jax.ShapeDtypeStruct((t, ws), BF16),
                   jax.ShapeDtypeStruct((8, ws), F32), jax.ShapeDtypeStruct((8, ws), F32),
                   jax.ShapeDtypeStruct((ng, BLOCK, BLOCK), F32), jax.ShapeDtypeStruct((ng, 8, LANES), F32)],
        scratch_shapes=[pltpu.VMEM((gpc, BLOCK, BLOCK), F32)],
        compiler_params=_params("arbitrary", "arbitrary"),
    )(proj, proj, ln_g, ln_b, w_s, bt, dout)


def _sigmoid(x):
    return 1.0 / (1.0 + jnp.exp(-x))


def _merge_geometry(off_g, d):
    cw = math.gcd(off_g, d)
    return cw, d // cw, off_g // cw, (off_g + d) // cw


def _branches_fwd(attn, sgu, wab_t, wsb_t, proj, off_g, name):
    t = attn.shape[0]
    d = wab_t.shape[0]
    tn, _, ab, bb = _merge_geometry(off_g, d)
    tm = _divisor_tile(t, 1024, 128)

    def body(a1_ref, a2_ref, b1_ref, b2_ref, la_ref, lb_ref, bra_ref, brb_ref, o_ref):
        for rows in _row_chunks(tm):
            va = _dot(a1_ref[rows, :], b1_ref[...], NT)
            vb = _dot(a2_ref[rows, :], b2_ref[...], NT)
            bra_ref[rows, :] = va
            brb_ref[rows, :] = vb
            o_ref[rows, :] = (_sigmoid(la_ref[rows, :]) * va + _sigmoid(lb_ref[rows, :]) * vb).astype(o_ref.dtype)

    rows = lambda w: pl.BlockSpec((tm, w), lambda i, j: (i, 0))
    wrow = lambda w: pl.BlockSpec((tn, w), lambda i, j: (j, 0))
    blk = lambda base: pl.BlockSpec((tm, tn), lambda i, j: (i, base + j))
    return pl.pallas_call(
        body,
        name=name,
        grid=(t // tm, d // tn),
        in_specs=[rows(attn.shape[1]), rows(sgu.shape[1]), wrow(wab_t.shape[1]), wrow(wsb_t.shape[1]), blk(ab),
                  blk(bb)],
        out_specs=[blk(0)] * 3,
        out_shape=[jax.ShapeDtypeStruct((t, d), F32), jax.ShapeDtypeStruct((t, d), F32),
                   jax.ShapeDtypeStruct((t, d), BF16)],
        compiler_params=_params("parallel", "parallel"),
    )(attn, sgu, wab_t, wsb_t, proj, proj)


def _branches_bwd(dx16, wo, br_a, br_b, proj, off_g, name, after=None):
    t, d = br_a.shape
    tn, _, ab, bb = _merge_geometry(off_g, d)
    tm = _divisor_tile(t, 1024, 128)
    k = dx16.shape[1]

    def body(a_ref, b_ref, bra_ref, brb_ref, la_ref, lb_ref, *rest):
        da_ref, db_ref, dla_ref, dlb_ref = rest[-4:]
        for rows in _row_chunks(tm):
            dmv = _dot(a_ref[rows, :], b_ref[...], NT)
            ga, gb = _sigmoid(la_ref[rows, :]), _sigmoid(lb_ref[rows, :])
            da_ref[rows, :] = (dmv * ga).astype(da_ref.dtype)
            db_ref[rows, :] = (dmv * gb).astype(db_ref.dtype)
            dla_ref[rows, :] = (dmv * bra_ref[rows, :] * ga * (1.0 - ga)).astype(dla_ref.dtype)
            dlb_ref[rows, :] = (dmv * brb_ref[rows, :] * gb * (1.0 - gb)).astype(dlb_ref.dtype)

    blk = lambda base: pl.BlockSpec((tm, tn), lambda i, j: (i, base + j))
    return pl.pallas_call(
        body,
        name=name,
        grid=(t // tm, d // tn),
        in_specs=[pl.BlockSpec((tm, k), lambda i, j: (i, 0)), pl.BlockSpec((tn, k), lambda i, j: (j, 0)), blk(0),
                  blk(0), blk(ab), blk(bb)] + ([ANY] if after is not None else []),
        out_specs=[blk(0)] * 4,
        out_shape=[jax.ShapeDtypeStruct((t, d), BF16)] * 4,
        compiler_params=_params("parallel", "parallel"),
    )(dx16, wo, br_a, br_b, proj, proj, *(() if after is None else (after,)))


def _gate_up_fwd(h2, wgu_t, name, after=None):
    t, d = h2.shape
    f = wgu_t.shape[0] // 2
    tm = _divisor_tile(t, 1024, 128)
    tn = _divisor_tile(f, 512, 128)
    nb = f // tn

    def body(a_ref, bg_ref, bu_ref, *rest):
        gu_ref, act_ref = rest[-2:]
        for rows in _row_chunks(tm):
            av = a_ref[rows, :]
            gv = _dot(av, bg_ref[...], NT)
            uv = _dot(av, bu_ref[...], NT)
            gu_ref[0, rows, :] = gv
            gu_ref[1, rows, :] = uv
            act_ref[rows, :] = (gv * _sigmoid(gv) * uv).astype(act_ref.dtype)

    return pl.pallas_call(
        body,
        name=name,
        grid=(t // tm, nb),
        in_specs=[pl.BlockSpec((tm, d), lambda i, j: (i, 0)), pl.BlockSpec((tn, d), lambda i, j: (j, 0)),
                  pl.BlockSpec((tn, d), lambda i, j: (j + nb, 0))] + ([ANY] if after is not None else []),
        out_specs=[pl.BlockSpec((2, tm, tn), lambda i, j: (0, i, j)), pl.BlockSpec((tm, tn), lambda i, j: (i, j))],
        out_shape=[jax.ShapeDtypeStruct((2, t, f), F32), jax.ShapeDtypeStruct((t, f), BF16)],
        compiler_params=_params("parallel", "parallel"),
    )(h2, wgu_t, wgu_t, *(() if after is None else (after,)))


def _gate_up_bwd(dx16, wd, gu, name, after=None):
    t, d = dx16.shape
    f = wd.shape[0]
    tm = _divisor_tile(t, 1024, 128)
    tn = _divisor_tile(f, 512, 128)

    def body(a_ref, b_ref, gu_ref, *rest):
        o_ref = rest[-1]
        for rows in _row_chunks(tm):
            dav = _dot(a_ref[rows, :], b_ref[...], NT)
            gv = gu_ref[0, rows, :]
            sg = _sigmoid(gv)
            o_ref[0, rows, :] = (dav * gu_ref[1, rows, :] * (sg + gv * sg * (1.0 - sg))).astype(o_ref.dtype)
            o_ref[1, rows, :] = (dav * gv * sg).astype(o_ref.dtype)

    pair = pl.BlockSpec((2, tm, tn), lambda i, j: (0, i, j))
    return pl.pallas_call(
        body,
        name=name,
        grid=(t // tm, f // tn),
        in_specs=[pl.BlockSpec((tm, d), lambda i, j: (i, 0)), pl.BlockSpec((tn, d), lambda i, j: (j, 0)), pair]
        + ([ANY] if after is not None else []),
        out_specs=pair,
        out_shape=jax.ShapeDtypeStruct((2, t, f), BF16),
        compiler_params=_params("parallel", "parallel"),
    )(dx16, wd, gu, *(() if after is None else (after,)))


def _loss_and_grad(y, target, name):
    t, d = y.shape
    tr = _divisor_tile(t, 512, 8)

    def body(y_ref, t_ref, l_ref, dy_ref, dy16_ref):
        i = pl.program_id(0)
        err = y_ref[...] - t_ref[...]
        dy_ref[...] = err * (1.0 / d)
        dy16_ref[...] = (err * (1.0 / d)).astype(dy16_ref.dtype)
        part = jnp.broadcast_to(0.5 * jnp.sum(err * err) * (1.0 / d), l_ref.shape)

        @pl.when(i == 0)
        def _():
            l_ref[...] = part

        @pl.when(i > 0)
        def _():
            l_ref[...] += part

    row = pl.BlockSpec((tr, d), lambda i: (i, 0))
    return pl.pallas_call(
        body,
        name=name,
        grid=(t // tr,),
        in_specs=[row, row],
        out_specs=[pl.BlockSpec((8, LANES), lambda i: (0, 0)), row, row],
        out_shape=[jax.ShapeDtypeStruct((8, LANES), F32), jax.ShapeDtypeStruct((t, d), F32),
                   jax.ShapeDtypeStruct((t, d), BF16)],
        compiler_params=_params("arbitrary"),
    )(y, target)


def _adam_math(w, g, m, v):
    m = ADAM_B1 * m + (1.0 - ADAM_B1) * g
    v = ADAM_B2 * v + (1.0 - ADAM_B2) * (g * g)
    m_hat = m / (1.0 - ADAM_B1 ** ADAM_STEP)
    v_hat = v / (1.0 - ADAM_B2 ** ADAM_STEP)
    delta = -ADAM_LR * (m_hat / (jnp.sqrt(v_hat) + ADAM_EPS) + ADAM_WD * w)
    return delta, m, v


def _row_tile(r, c, elems=512 * 1024):
    return _divisor_tile(r, max(8, elems // c // 8 * 8), 8)


def _adam(w, grads, m, v, chip, name, after=None):
    nl, r, c = w.shape
    tr = _row_tile(r, c, 256 * 1024)
    nb = r // tr
    counts = [len(terms) for terms, _ in grads]

    def body(chip_ref, *refs):
        w_ref, m_ref, v_ref = refs[:3]
        g_ref, d_ref, nm_ref, nv_ref = refs[-4:]
        layer = pl.program_id(0)
        g, at = None, 3
        for li, n in enumerate(counts):
            total = refs[at][...].astype(F32)
            for ref in refs[at + 1:at + n]:
                total = total + ref[...].astype(F32)
            g = total if g is None else jnp.where(layer == li, total, g)
            at += n
        g_ref[...] = g
        d_ref[...], nm_ref[...], nv_ref[...] = _adam_math(w_ref[...], g, m_ref[...], v_ref[...])

    def term_spec(li, p, by_owner):
        def index(l, i, chip_ref):
            rows = jnp.where(l < li, 0, jnp.where(l > li, nb - 1, i))
            return (p, chip_ref[0] if by_owner else 0, rows, 0)
        return pl.BlockSpec((None, None, tr, c), index)

    row = pl.BlockSpec((None, tr, c), lambda l, i, chip_ref: (l, i, 0))
    specs, arrays = [], []
    for li, (terms, p) in enumerate(grads):
        for term in terms:
            specs.append(term_spec(li, p, term.shape[1] == 4))
            arrays.append(term)
    return pl.pallas_call(
        body,
        name=name,
        grid_spec=pltpu.PrefetchScalarGridSpec(
            num_scalar_prefetch=1, grid=(nl, nb),
            in_specs=[row] * 3 + specs + ([ANY] if after is not None else []), out_specs=[row] * 4),
        out_shape=[jax.ShapeDtypeStruct((nl, r, c), F32)] * 4,
        compiler_params=_params("arbitrary", "arbitrary"),
    )(chip, w, m, v, *arrays, *(() if after is None else (after,)))


def _place_shard(parts, layer, dev, out_dtype, name, after=None):
    p = len(parts)
    _, r, c = parts[0].shape
    tr = _row_tile(r, c)

    def body(dev_ref, *refs):
        o_ref = refs[-1]
        x = refs[0][...]
        for pi in range(1, p):
            x = jnp.where(pl.program_id(0) == pi, refs[pi][...], x)
        o_ref[...] = x.astype(o_ref.dtype)

    return pl.pallas_call(
        body,
        name=name,
        grid_spec=pltpu.PrefetchScalarGridSpec(
            num_scalar_prefetch=1,
            grid=(p, r // tr),
            in_specs=[pl.BlockSpec((None, tr, c), lambda pi, i, dev_ref: (layer, i, 0))] * p
            + ([ANY] if after is not None else []),
            out_specs=pl.BlockSpec((None, None, tr, c), lambda pi, i, dev_ref: (pi, dev_ref[0], i, 0)),
        ),
        out_shape=jax.ShapeDtypeStruct((p, N_DEV, r, c), out_dtype),
        compiler_params=_params("parallel", "parallel"),
    )(dev, *parts, *(() if after is None else (after,)))


def _sum_sibling(g, land, core, name):
    p, _, _, r, c = g.shape
    tr = _row_tile(r, c, 1024 * 1024)

    def body(core_ref, g_ref, l_ref, o_ref):
        o_ref[...] = (g_ref[...].astype(F32) + l_ref[...].astype(F32)).astype(o_ref.dtype)

    return pl.pallas_call(
        body,
        name=name,
        grid_spec=pltpu.PrefetchScalarGridSpec(
            num_scalar_prefetch=1,
            grid=(p, 4, r // tr),
            in_specs=[pl.BlockSpec((None, None, None, tr, c), lambda pi, q, i, core_ref: (pi, q, core_ref[0], i, 0)),
                      pl.BlockSpec((None, None, None, tr, c), lambda pi, q, i, core_ref: (pi, q, 0, i, 0))],
            out_specs=pl.BlockSpec((None, None, tr, c), lambda pi, q, i, core_ref: (pi, q, i, 0)),
        ),
        out_shape=jax.ShapeDtypeStruct((p, 4, r, c), BF16),
        compiler_params=_params("parallel", "parallel", "parallel"),
    )(core, g, land)


def _sum_chips(s, lands, chip, name):
    p, _, r, c = s.shape
    tr = _row_tile(r, c)

    def body(chip_ref, s_ref, l0_ref, l1_ref, l2_ref, o_ref):
        total = s_ref[...].astype(F32) + l0_ref[...].astype(F32)
        o_ref[...] = total + l1_ref[...].astype(F32) + l2_ref[...].astype(F32)

    land_spec = pl.BlockSpec((None, None, tr, c), lambda pi, i, chip_ref: (pi, 0, i, 0))
    return pl.pallas_call(
        body,
        name=name,
        grid_spec=pltpu.PrefetchScalarGridSpec(
            num_scalar_prefetch=1,
            grid=(p, r // tr),
            in_specs=[pl.BlockSpec((None, None, tr, c), lambda pi, i, chip_ref: (pi, chip_ref[0], i, 0)),
                      land_spec, land_spec, land_spec],
            out_specs=pl.BlockSpec((None, tr, c), lambda pi, i, chip_ref: (pi, i, 0)),
        ),
        out_shape=jax.ShapeDtypeStruct((p, r, c), F32),
        compiler_params=_params("parallel", "parallel"),
    )(chip, s, *lands)


def _small_reduce_adam(gathered, w, m, v, name):
    _, r, c = gathered.shape
    tr = _row_tile(r, c)

    def body(p_ref, w_ref, m_ref, v_ref, g_ref, d_ref, nm_ref, nv_ref):
        g = p_ref[0]
        for j in range(1, N_DEV):
            g = g + p_ref[j]
        g_ref[...] = g
        d_ref[...], nm_ref[...], nv_ref[...] = _adam_math(w_ref[...], g, m_ref[...], v_ref[...])

    row = pl.BlockSpec((tr, c), lambda i: (i, 0))
    return pl.pallas_call(
        body,
        name=name,
        grid=(r // tr,),
        in_specs=[pl.BlockSpec((N_DEV, tr, c), lambda i: (0, i, 0)), row, row, row],
        out_specs=[row] * 4,
        out_shape=[jax.ShapeDtypeStruct((r, c), F32)] * 4,
        compiler_params=_params("parallel"),
    )(gathered, w, m, v)


def _place():
    return lax.axis_index("x"), lax.axis_index("y"), lax.axis_index("c")


HBM =pl.BlockSpec(memory_space=pltpu.HBM)
SEM = pl.BlockSpec(memory_space=pltpu.SEMAPHORE)
TOKEN = pl.BlockSpec(memory_space=pltpu.VMEM)
EFFECT = pltpu.SideEffectType.DATAFLOW_SIDE_EFFECTING


def _in_hbm(a):
    return pltpu.with_memory_space_constraint(a, pltpu.HBM)


_FLIPS = {"me": (0, 0, 0), "s": (0, 0, 1), "x": (1, 0, 0), "y": (0, 1, 0), "d": (1, 1, 0)}
GATHER_STAGES = (
    (("s", "me", "all"), ("x", "me", "all"), ("y", "me", "all")),
    (("s", "x", "all"), ("s", "y", "all"), ("y", "x", "first"), ("x", "y", "second")),
    (("s", "d", "all"),),
)


def _flipped(place, *names):
    out = list(place)
    for name in names:
        out = [1 - p if f else p for p, f in zip(out, _FLIPS[name])]
    return tuple(out)


def _block_part(ref, place, part):
    px, py, pc = place
    rows = ref.shape[2]
    span = {"all": pl.ds(0, rows), "first": pl.ds(0, rows // 2), "second": pl.ds(rows // 2, rows // 2)}[part]
    return ref.at[:, pl.ds(4 * px + 2 * py + pc, 1), span]


def _split_start(bufs, moves, name, after=None):
    n, nm = len(bufs), len(moves)
    extra = 0 if after is None else 1

    def body(*refs):
        ssem, rsem = refs[n + extra], refs[n + extra + 1]
        outs, token = refs[n + extra + 2:2 * n + extra + 2], refs[2 * n + extra + 2]
        me = _place()
        for a in range(n):
            for k, (to, owner, part) in enumerate(moves):
                piece = _block_part(outs[a], _flipped(me, owner), part)
                pltpu.make_async_remote_copy(
                    src_ref=piece, dst_ref=piece, send_sem=ssem.at[nm * a + k], recv_sem=rsem.at[nm * a + k],
                    device_id=_flipped(me, to), device_id_type=MESH).start()
        token[...] = jnp.zeros_like(token)

    outs = pl.pallas_call(
        body,
        name=name,
        in_specs=[HBM] * n + [ANY] * extra,
        out_specs=[SEM, SEM] + [HBM] * n + [TOKEN],
        out_shape=[pltpu.SemaphoreType.DMA((nm * n,))] * 2 + [pltpu.HBM(b.shape, b.dtype) for b in bufs]
        + [jax.ShapeDtypeStruct((8, LANES), F32)],
        input_output_aliases={i: 2 + i for i in range(n)},
        compiler_params=pltpu.CompilerParams(has_side_effects=EFFECT),
    )(*[_in_hbm(b) for b in bufs], *(() if after is None else (after,)))
    return outs[0], outs[1], list(outs[2:2 + n]), outs[-1]


def _split_wait(send_sems, recv_sems, bufs, moves, after, name):
    n, nm = len(bufs), len(moves)

    def body(*refs):
        ins, ssem, rsem = refs[:n], refs[n], refs[n + 1]
        me = _place()
        for a in range(n):
            for k, (to, owner, part) in enumerate(moves):
                landed = _block_part(ins[a], _flipped(me, owner, to), part)
                cp = pltpu.make_async_remote_copy(
                    src_ref=landed, dst_ref=landed, send_sem=ssem.at[nm * a + k], recv_sem=rsem.at[nm * a + k],
                    device_id=_flipped(me, to), device_id_type=MESH)
                cp.wait_send()
                cp.wait_recv()

    return pl.pallas_call(
        body,
        name=name,
        in_specs=[HBM] * n + [SEM, SEM, ANY],
        out_specs=[HBM] * n,
        out_shape=[pltpu.HBM(b.shape, b.dtype) for b in bufs],
        input_output_aliases={i: i for i in range(n)},
        compiler_params=pltpu.CompilerParams(has_side_effects=EFFECT),
    )(*bufs, send_sems, recv_sems, after)


def _chips_start(sums, name):
    n = len(sums)

    def body(*refs):
        ssem, rsem = refs[4 * n], refs[4 * n + 1]
        src, land = refs[4 * n + 2:5 * n + 2], refs[5 * n + 2:8 * n + 2]
        token = refs[8 * n + 2]
        x, y, c = _place()
        chips = [(1 - x, y), (x, 1 - y), (1 - x, 1 - y)]
        for a in range(n):
            for k, (px, py) in enumerate(chips):
                pltpu.make_async_remote_copy(
                    src_ref=src[a].at[:, pl.ds(2 * px + py, 1)], dst_ref=land[3 * a + k], send_sem=ssem.at[3 * a + k],
                    recv_sem=rsem.at[3 * a + k], device_id=(px, py, c), device_id_type=MESH).start()
        token[...] = jnp.zeros_like(token)

    lands = []
    for s in sums:
        lands += [lax.empty((s.shape[0], 1) + s.shape[2:], s.dtype) for _ in range(3)]
    outs = pl.pallas_call(
        body,
        name=name,
        in_specs=[HBM] * (4 * n),
        out_specs=[SEM, SEM] + [HBM] * (4 * n) + [TOKEN],
        out_shape=[pltpu.SemaphoreType.DMA((3 * n,))] * 2 + [pltpu.HBM(b.shape, b.dtype) for b in list(sums) + lands]
        + [jax.ShapeDtypeStruct((8, LANES), F32)],
        input_output_aliases={i: 2 + i for i in range(4 * n)},
        compiler_params=pltpu.CompilerParams(has_side_effects=EFFECT),
    )(*[_in_hbm(b) for b in list(sums) + lands])
    return outs[0], outs[1], list(outs[2:2 + n]), list(outs[2 + n:2 + 4 * n]), outs[-1]


def _chips_wait(send_sems, recv_sems, sums, lands, after, name):
    n = len(sums)

    def body(*refs):
        src, land = refs[:n], refs[n:4 * n]
        ssem, rsem = refs[4 * n], refs[4 * n + 1]
        x, y, c = _place()
        chips = [(1 - x, y), (x, 1 - y), (1 - x, 1 - y)]
        for a in range(n):
            for k, (px, py) in enumerate(chips):
                cp = pltpu.make_async_remote_copy(
                    src_ref=src[a].at[:, pl.ds(2 * px + py, 1)], dst_ref=land[3 * a + k], send_sem=ssem.at[3 * a + k],
                    recv_sem=rsem.at[3 * a + k], device_id=(px, py, c), device_id_type=MESH)
                cp.wait_send()
                cp.wait_recv()

    both = list(sums) + list(lands)
    outs = pl.pallas_call(
        body,
        name=name,
        in_specs=[HBM] * (4 * n) + [SEM, SEM, ANY],
        out_specs=[HBM] * (4 * n),
        out_shape=[pltpu.HBM(b.shape, b.dtype) for b in both],
        input_output_aliases={i: i for i in range(4 * n)},
        compiler_params=pltpu.CompilerParams(has_side_effects=EFFECT),
    )(*both, send_sems, recv_sems, after)
    return list(outs[:n]), [list(outs[n + 3 * a:n + 3 * a + 3]) for a in range(n)]


def _sibling_start(grads, name):
    n = len(grads)

    def body(*refs):
        ssem, rsem = refs[2 * n], refs[2 * n + 1]
        src, land = refs[2 * n + 2:3 * n + 2], refs[3 * n + 2:4 * n + 2]
        token = refs[4 * n + 2]
        x, y, c = _place()
        for a in range(n):
            pltpu.make_async_remote_copy(
                src_ref=src[a].at[:, :, pl.ds(1 - c, 1)], dst_ref=land[a], send_sem=ssem.at[a], recv_sem=rsem.at[a],
                device_id=(x, y, 1 - c), device_id_type=MESH).start()
        token[...] = jnp.zeros_like(token)

    lands = [lax.empty(g.shape[:2] + (1,) + g.shape[3:], g.dtype) for g in grads]
    both = list(grads) + lands
    outs = pl.pallas_call(
        body,
        name=name,
        in_specs=[HBM] * (2 * n),
        out_specs=[SEM, SEM] + [HBM] * (2 * n) + [TOKEN],
        out_shape=[pltpu.SemaphoreType.DMA((n,))] * 2 + [pltpu.HBM(b.shape, b.dtype) for b in both]
        + [jax.ShapeDtypeStruct((8, LANES), F32)],
        input_output_aliases={i: 2 + i for i in range(2 * n)},
        compiler_params=pltpu.CompilerParams(has_side_effects=EFFECT),
    )(*[_in_hbm(b) for b in both])
    return outs[0], outs[1], list(outs[2:2 + n]), list(outs[2 + n:2 + 2 * n]), outs[-1]


def _sibling_wait(send_sems, recv_sems, grads, lands, after, name):
    n = len(grads)

    def body(*refs):
        src, land = refs[:n], refs[n:2 * n]
        ssem, rsem = refs[2 * n], refs[2 * n + 1]
        x, y, c = _place()
        for a in range(n):
            cp = pltpu.make_async_remote_copy(
                src_ref=src[a].at[:, :, pl.ds(1 - c, 1)], dst_ref=land[a], send_sem=ssem.at[a], recv_sem=rsem.at[a],
                device_id=(x, y, 1 - c), device_id_type=MESH)
            cp.wait_send()
            cp.wait_recv()

    both = list(grads) + list(lands)
    outs = pl.pallas_call(
        body,
        name=name,
        in_specs=[HBM] * (2 * n) + [SEM, SEM, ANY],
        out_specs=[HBM] * (2 * n),
        out_shape=[pltpu.HBM(b.shape, b.dtype) for b in both],
        input_output_aliases={i: i for i in range(2 * n)},
        compiler_params=pltpu.CompilerParams(has_side_effects=EFFECT),
    )(*both, send_sems, recv_sems, after)
    return list(outs[:n]), list(outs[n:])


_SMALL = ("mix_norm", "q_norm", "k_norm", "sinks", "sgu_ln_g", "sgu_ln_b", "w_spatial", "b_spatial", "ffn_norm")


def _pack_rows(a):
    flat = a.reshape(-1)
    pad = (-flat.shape[0]) % LANES
    if pad:
        flat = jnp.pad(flat, (0, pad))
    return flat.reshape(-1, LANES)


def _pack(values):
    rows = jnp.concatenate([_pack_rows(values[k]) for k in _SMALL], axis=0)
    pad = (-rows.shape[0]) % 8
    if pad:
        rows = jnp.pad(rows, ((0, pad), (0, 0)))
    return rows


def _unpack(rows, like):
    out, at = {}, 0
    for k in _SMALL:
        size = like[k].size
        nrows = -(-size // LANES)
        out[k] = rows[at:at + nrows].reshape(-1)[:size].reshape(like[k].shape)
        at += nrows
    return out


def _rope_tables(t, wq, wk):
    pos = jnp.arange(t, dtype=F32)
    inv_freq = jnp.power(ROPE_THETA, -jnp.arange(0, HEAD_DIM, 2, dtype=F32) / HEAD_DIM)
    ang = pos[:, None] * inv_freq[None, :]
    cos, sin = jnp.cos(ang), jnp.sin(ang)
    cos2, sin2 = jnp.concatenate([cos, cos], axis=1), jnp.concatenate([-sin, sin], axis=1)
    return (jnp.tile(cos2, (1, wq // HEAD_DIM)), jnp.tile(sin2, (1, wq // HEAD_DIM)),
            jnp.tile(cos2, (1, wk // HEAD_DIM)), jnp.tile(sin2, (1, wk // HEAD_DIM)))


def kernel(x, mix_norm, w_in, q_norm, k_norm, sinks, sgu_ln_g, sgu_ln_b, w_spatial, b_spatial, w_attn_branch, w_sgu_branch, w_out, ffn_norm, w_gate, w_up, w_down, loss_target, m_mix_norm, m_w_in, m_q_norm, m_k_norm, m_sinks, m_sgu_ln_g, m_sgu_ln_b, m_w_spatial, m_b_spatial, m_w_attn_branch, m_w_sgu_branch, m_w_out, m_ffn_norm, m_w_gate, m_w_up, m_w_down, v_mix_norm, v_w_in, v_q_norm, v_k_norm, v_sinks, v_sgu_ln_g, v_sgu_ln_b, v_w_spatial, v_b_spatial, v_w_attn_branch, v_w_sgu_branch, v_w_out, v_ffn_norm, v_w_gate, v_w_up, v_w_down):
    names = ("mix_norm", "w_in", "q_norm", "k_norm", "sinks", "sgu_ln_g", "sgu_ln_b", "w_spatial", "b_spatial",
             "w_attn_branch", "w_sgu_branch", "w_out", "ffn_norm", "w_gate", "w_up", "w_down")
    weights = dict(zip(names, (mix_norm, w_in, q_norm, k_norm, sinks, sgu_ln_g, sgu_ln_b, w_spatial, b_spatial,
                               w_attn_branch, w_sgu_branch, w_out, ffn_norm, w_gate, w_up, w_down)))
    mom1 = dict(zip(names, (m_mix_norm, m_w_in, m_q_norm, m_k_norm, m_sinks, m_sgu_ln_g, m_sgu_ln_b, m_w_spatial,
                            m_b_spatial, m_w_attn_branch, m_w_sgu_branch, m_w_out, m_ffn_norm, m_w_gate, m_w_up,
                            m_w_down)))
    mom2 = dict(zip(names, (v_mix_norm, v_w_in, v_q_norm, v_k_norm, v_sinks, v_sgu_ln_g, v_sgu_ln_b, v_w_spatial,
                            v_b_spatial, v_w_attn_branch, v_w_sgu_branch, v_w_out, v_ffn_norm, v_w_gate, v_w_up,
                            v_w_down)))
    depth = w_in.shape[0]
    _, t, d = x.shape
    n_q_heads = sinks.shape[1]
    wq = n_q_heads * HEAD_DIM
    wk = wq // Q_PER_KV
    ws = sgu_ln_g.shape[1]
    ng = ws // LANES
    off_u = wq + 2 * wk
    off_g = off_u + 2 * ws
    tables = _rope_tables(t, wq, wk)
    px, py, pc = _place()
    core = pc.astype(jnp.int32)[None]
    chip = (2 * px + py).astype(jnp.int32)[None]
    dev = (4 * px + 2 * py + pc).astype(jnp.int32)[None]

    layers = range(depth)
    chunks = ((0,), (1, 2, 3), (4,), (5,))
    sources = [[jnp.swapaxes(w_in, 1, 2)], [jnp.swapaxes(w_attn_branch, 1, 2)], [jnp.swapaxes(w_sgu_branch, 1, 2)],
               [w_out], [jnp.swapaxes(w_gate, 1, 2), jnp.swapaxes(w_up, 1, 2)], [w_down]]
    stream = [(l, ci) for l in layers for ci in range(len(chunks))]
    placed, state, token = {}, {}, None

    def send(key, after):
        state[key] = _split_start(placed[key], GATHER_STAGES[0], "gather_send_%d_%d" % key, after)
        return state[key][3]

    def advance(key, after, stage):
        send_sems, recv_sems, bufs, _ = state[key]
        bufs = _split_wait(send_sems, recv_sems, bufs, GATHER_STAGES[stage - 1], after, "gather_wait%d_%d_%d" % (stage, *key))
        state[key] = _split_start(bufs, GATHER_STAGES[stage], "gather_pass%d_%d_%d" % (stage, *key))
        return state[key][3]

    def relay(key, after):
        tok = advance(key, after, 1)
        at = stream.index(key)
        for later in stream[at + 2:at + 3] if at else stream[1:3]:
            tok = send(later, tok)
        return tok

    def ready(key, after):
        send_sems, recv_sems, bufs, _ = state.pop(key)
        bufs = _split_wait(send_sems, recv_sems, bufs, GATHER_STAGES[2], after, "gather_wait3_%d_%d" % key)
        return [f.reshape(f.shape[0] * f.shape[1] * f.shape[2], f.shape[3]) for f in bufs]

    for key in stream:
        l, ci = key
        placed[key] = [_place_shard(sources[a], l, dev, BF16, f"place_shard_{l}_{a}",
                                    after=token if a == chunks[ci][0] else None) for a in chunks[ci]]
        token = send(key, None) if key == stream[0] else placed[key][-1]

    saved = []
    xl = x[0]
    going = relay((0, 0), token)
    going = advance((0, 0), going, 2)
    for l in layers:
        gq = jnp.tile(q_norm[l], n_q_heads)[None]
        gk = jnp.tile(k_norm[l], n_q_heads // Q_PER_KV)[None]
        bt = b_spatial[l].T
        h = _rmsnorm_fwd(xl, mix_norm[l][None], f"mix_norm_fwd_{l}", after=going)
        (win_t,) = ready((l, 0), h)
        proj = _mm(h, win_t, "nt", F32, f"in_proj_{l}")
        going = relay((l, 1), proj)
        attn = _attn_fwd(proj, tables, gq, gk, sinks[l], wq, wk, f"attn_fwd_{l}", after=going)
        going = advance((l, 1), attn, 2)
        sgu = _sgu_fwd(proj, sgu_ln_g[l][None], sgu_ln_b[l][None], w_spatial[l], bt, off_u, ws, f"sgu_fwd_{l}",
                       after=going)
        wab_t, wsb_t, wo = ready((l, 1), sgu)
        br_a, br_b, merged = _branches_fwd(attn, sgu, wab_t, wsb_t, proj, off_g, f"branches_{l}")
        going = relay((l, 2), merged)
        x1 = _mm(merged, wo, "nn", F32, f"out_proj_{l}", residual=xl, after=going)
        going = advance((l, 2), x1, 2)
        h2 = _rmsnorm_fwd(x1, ffn_norm[l][None], f"ffn_norm_fwd_{l}", after=going)
        (wgu_t,) = ready((l, 2), h2)
        going = relay((l, 3), h2)
        gu, act = _gate_up_fwd(h2, wgu_t, f"gate_up_{l}", after=going)
        going = advance((l, 3), act, 2)
        if l + 1 < depth:
            going = relay((l + 1, 0), going)
        (wd,) = ready((l, 3), going)
        x2 = _mm(act, wd, "nn", F32, f"down_proj_{l}", residual=x1)
        if l + 1 < depth:
            going = advance((l + 1, 0), x2, 2)
        saved.append(dict(x0=xl, h=h, proj=proj, attn=attn, sgu=sgu, br_a=br_a, br_b=br_b, merged=merged, x1=x1,
                          h2=h2, gu=gu, act=act, gq=gq, gk=gk, bt=bt, win_t=win_t, wab_t=wab_t, wsb_t=wsb_t, wo=wo,
                          wgu_t=wgu_t, wd=wd))
        xl = x2

    loss_part, dx, dx16 = _loss_and_grad(xl, loss_target[0], "loss")
    loss = lax.psum(loss_part[0, 0], ("x", "y", "c"))

    def sibling_start(grads, tag):
        shaped = []
        for g, p in grads:
            rows, c = g.shape
            shaped.append(g.reshape(p, 4, 2, rows // (8 * p), c))
        send_sems, recv_sems, shaped, lands, tok = _sibling_start(shaped, f"rs_sibling_start_{tag}")
        return (send_sems, recv_sems, shaped, lands, tag), tok

    def chips_start(state, after):
        send_sems, recv_sems, shaped, lands, tag = state
        shaped, lands = _sibling_wait(send_sems, recv_sems, shaped, lands, after, f"rs_sibling_wait_{tag}")
        sums = [_sum_sibling(g, o, core, f"rs_add_sibling_{tag}_{a}") for a, (g, o) in enumerate(zip(shaped, lands))]
        send_sems, recv_sems, sums, lands, tok = _chips_start(sums, f"rs_chips_start_{tag}")
        return (send_sems, recv_sems, sums, lands, tag), tok

    def scatter_finish(state, after):
        send_sems, recv_sems, sums, lands, tag = state
        sums, lands = _chips_wait(send_sems, recv_sems, sums, lands, after, f"rs_chips_wait_{tag}")
        return [[s] + o for s, o in zip(sums, lands)]

    in_flight = [dict() for _ in layers]
    small_grads = [None] * depth
    tok, swap_in = None, None
    for l in reversed(layers):
        s = saved[l]
        dgu = _gate_up_bwd(dx16, s["wd"], s["gu"], f"d_gate_up_{l}", after=tok)
        if swap_in is not None:
            in_flight[l + 1]["in"], tok = chips_start(swap_in, dgu)
        g_wd = _mm(s["act"], dx16, "tn", BF16, f"g_w_down_{l}", after=tok)
        swap, tok_s = sibling_start([(g_wd, 1)], f"{l}_down")
        dh2 = _mm(dgu, s["wgu_t"], "nn", F32, f"d_h2_{l}", after=tok_s)
        in_flight[l]["down"], tok = chips_start(swap, dh2)
        g_wgu_t = _mm(dgu, s["h2"], "tn", BF16, f"g_w_gate_up_{l}", after=tok)
        swap, tok_s = sibling_start([(g_wgu_t, 2)], f"{l}_gate_up")
        dx1, dx1_16, g_ffn = _rmsnorm_bwd(s["x1"], ffn_norm[l][None], dh2, dx, f"ffn_norm_bwd_{l}", after=tok_s)
        d_a, d_b, dla, dlb = _branches_bwd(dx1_16, s["wo"], s["br_a"], s["br_b"], s["proj"], off_g,
                                           f"d_branches_{l}")
        in_flight[l]["gate_up"], tok = chips_start(swap, d_a)
        g_wo = _mm(s["merged"], dx1_16, "tn", BF16, f"g_w_out_{l}", after=tok)
        dattn = _mm(d_a, s["wab_t"], "nn", F32, f"d_attn_{l}", after=g_wo)
        g_wab_t = _mm(d_a, s["attn"], "tn", BF16, f"g_w_attn_branch_{l}")
        dsgu = _mm(d_b, s["wsb_t"], "nn", F32, f"d_sgu_{l}")
        g_wsb_t = _mm(d_b, s["sgu"], "tn", BF16, f"g_w_sgu_branch_{l}")
        swap, tok_s = sibling_start([(g_wab_t, 1), (g_wsb_t, 1), (g_wo, 1)], f"{l}_mix")
        dq, dk, dv, g_gq, g_gk, g_sinks = _attn_bwd(s["proj"], dattn, tables, s["gq"], s["gk"], sinks[l], wq, wk,
                                                    f"attn_bwd_{l}", after=tok_s)
        du, dvv, g_lng, g_lnb, g_ws, g_bs = _sgu_bwd(s["proj"], dsgu, sgu_ln_g[l][None], sgu_ln_b[l][None],
                                                     w_spatial[l], s["bt"], off_u, ws, f"sgu_bwd_{l}")
        dproj = jnp.concatenate([dq, dk.astype(BF16), dv.astype(BF16), du, dvv, dla, dlb], axis=1)
        dh = _mm(dproj, s["win_t"], "nn", F32, f"d_h_{l}")
        in_flight[l]["mix"], tok = chips_start(swap, dh)
        g_win_t = _mm(dproj, s["h"], "tn", BF16, f"g_w_in_{l}", after=tok)
        swap_in, tok = sibling_start([(g_win_t, 1)], f"{l}_in")
        dx, dx16, g_mix = _rmsnorm_bwd(s["x0"], mix_norm[l][None], dh, dx1, f"mix_norm_bwd_{l}", after=tok)
        small_grads[l] = dict(
            mix_norm=g_mix[0], q_norm=g_gq[0].reshape(n_q_heads, HEAD_DIM).sum(0),
            k_norm=g_gk[0].reshape(n_q_heads // Q_PER_KV, HEAD_DIM).sum(0), sinks=g_sinks[0, :n_q_heads],
            sgu_ln_g=g_lng[0], sgu_ln_b=g_lnb[0], w_spatial=g_ws, b_spatial=g_bs[:, 0, :], ffn_norm=g_ffn[0])
    grad_x = dx[None]

    result = {key: {} for key in ("grad", "delta", "m", "v")}
    layer_like = {k: weights[k][0] for k in _SMALL}
    packed_g = jnp.concatenate([_pack(small_grads[l]) for l in layers], axis=0)
    rows_per_layer = packed_g.shape[0] // depth
    small_buf = _place_shard([packed_g[None]], 0, dev, F32, "place_small_grads", after=tok)
    send_sems, recv_sems, small_bufs, tok = _split_start([small_buf], GATHER_STAGES[0], "gather_send_small")
    in_flight[0]["in"], tok = chips_start(swap_in, tok)

    def update(k, grads, transposed, after):
        view = (lambda a: jnp.swapaxes(a, 1, 2)) if transposed else (lambda a: a)
        outs = _adam(view(weights[k]), grads, view(mom1[k]), view(mom2[k]), chip, f"adam_{k}", after=after)
        for key, val in zip(("grad", "delta", "m", "v"), outs):
            result[key][k] = view(val)
        return outs[3]

    def plain(terms, tag):
        s, lands = terms[0], terms[1:]
        g = _sum_chips(s, lands, chip, f"rs_add_chips_{tag}")
        return [jnp.swapaxes(g, 1, 2)[:, None]]

    small_state = (send_sems, recv_sems, small_bufs)

    def small_stage(stage, after):
        bufs = _split_wait(*small_state[:2], small_state[2], GATHER_STAGES[stage - 1], after,
                           f"gather_wait{stage}_small")
        ssem, rsem, bufs, tok = _split_start(bufs, GATHER_STAGES[stage], f"gather_pass{stage}_small")
        return (ssem, rsem, bufs), tok

    down = [scatter_finish(in_flight[l]["down"], tok) for l in reversed(layers)][::-1]
    tok = update("w_down", [(down[l][0], 0) for l in layers], False, None)
    small_state, tok = small_stage(1, tok)
    gate_up = [scatter_finish(in_flight[l]["gate_up"], tok) for l in reversed(layers)][::-1]
    tok = update("w_gate", [(gate_up[l][0], 0) for l in layers], True, None)
    tok = update("w_up", [(gate_up[l][0], 1) for l in layers], True, tok)
    small_state, tok = small_stage(2, tok)
    mix = [scatter_finish(in_flight[l]["mix"], tok) for l in reversed(layers)][::-1]
    tok = update("w_out", [(mix[l][2], 0) for l in layers], False, None)
    tok = update("w_attn_branch", [(plain(mix[l][0], f"{l}_attn_branch"), 0) for l in layers], False, tok)
    tok = update("w_sgu_branch", [(plain(mix[l][1], f"{l}_sgu_branch"), 0) for l in layers], False, tok)

    packed = [jnp.concatenate([_pack({k: src[k][l] for k in _SMALL}) for l in layers], axis=0)
              for src in (weights, mom1, mom2)]
    (gathered_small,) = _split_wait(*small_state[:2], small_state[2], GATHER_STAGES[2], tok, "gather_wait3_small")
    small = _small_reduce_adam(gathered_small[0], *packed, "small_reduce_adam")
    for key, rows in zip(("grad", "delta", "m", "v"), small):
        per_layer = [_unpack(rows[l * rows_per_layer:(l + 1) * rows_per_layer], layer_like) for l in layers]
        for k in _SMALL:
            result[key][k] = jnp.stack([per_layer[l][k] for l in layers])

    last = [scatter_finish(in_flight[l]["in"], small[0]) for l in reversed(layers)][::-1]
    update("w_in", [(last[l][0], 0) for l in layers], True, result["v"]["ffn_norm"])

    return (loss, grad_x, *[result["grad"][k] for k in names], *[result["delta"][k] for k in names],
            *[result["m"][k] for k in names], *[result["v"][k] for k in names])
```

```python
import functools
import math

import jax
import jax.numpy as jnp
from jax import lax
from jax.experimental import pallas as pl
from jax.experimental.pallas import tpu as pltpu

F32 = jnp.float32
BF16 = jnp.bfloat16
MESH = pl.DeviceIdType.MESH
ANY = pl.BlockSpec(memory_space=pl.ANY)

N_DEV = 8
HEAD_DIM = 64
Q_PER_KV = 4
BLOCK = 128
LANES = 128
ROPE_THETA = 10000.0
EPS = 1e-6
ADAM_LR = 0.001
ADAM_B1 = 0.9
ADAM_B2 = 0.999
ADAM_EPS = 1e-08
ADAM_WD = 0.01
ADAM_STEP = 10
NEG = -1e30
VMEM_LIMIT_BYTES = 56 * 1024 * 1024

NN = ((1,), (0,))
NT = ((1,), (1,))
TN = ((0,), (0,))


def _dot(a, b, dims):
    return lax.dot_general(a, b, (dims, ((), ())), preferred_element_type=F32)


def _params(*sem):
    return pltpu.CompilerParams(dimension_semantics=sem, vmem_limit_bytes=VMEM_LIMIT_BYTES)


def _divisor_tile(n, limit, unit):
    if n <= limit:
        return n
    best = unit
    for t in range(unit, limit + 1, unit):
        if n % t == 0:
            best = t
    assert n % best == 0, (n, limit, unit)
    return best


def _row_chunks(rows, size=256):
    size = min(size, rows)
    assert rows % size == 0, (rows, size)
    return [pl.ds(start, size) for start in range(0, rows, size)]


def _mm(a, b, mode, out_dtype, name, residual=None, after=None):
    parts = a.shape[0] if a.ndim == 3 else 1
    a2 = a.shape[-2:]
    if mode == "nn":
        (m, kp), (k2, n) = a2, b.shape
        k, mp = kp * parts, m
    elif mode == "nt":
        (m, kp), (n, k2) = a2, b.shape
        k, mp = kp * parts, m
    else:
        (k, mp), (k2, n) = a2, b.shape
        m, kp = mp * parts, k
    assert k == k2, (name, a.shape, b.shape)
    tk = _divisor_tile(kp, 2816, 128)
    nk = k // tk
    tm = _divisor_tile(mp, 512 if mode == "tn" else 1024, 128)
    tn = _divisor_tile(n, 2048 if mode == "tn" else 1024, 128)
    kpb, mpb = kp // tk, mp // tm
    dims = {"nn": NN, "nt": NT, "tn": TN}[mode]
    lead = (None,) if a.ndim == 3 else ()
    if mode == "tn":
        a_index = lambda i, j, kk: (i // mpb, kk, i % mpb) if lead else (kk, i)
        a_spec = pl.BlockSpec(lead + (tk, tm), a_index)
    else:
        a_index = lambda i, j, kk: (kk // kpb, i, kk % kpb) if lead else (i, kk)
        a_spec = pl.BlockSpec(lead + (tm, tk), a_index)
    if mode == "nt":
        b_spec = pl.BlockSpec((tn, tk), lambda i, j, kk: (j, kk))
    else:
        b_spec = pl.BlockSpec((tk, tn), lambda i, j, kk: (kk, j))
    o_spec = pl.BlockSpec((tm, tn), lambda i, j, kk: (i, j))
    has_res = residual is not None

    def body(*refs):
        a_ref, b_ref = refs[:2]
        r_ref = refs[2] if has_res else None
        o_ref, acc_ref = refs[-2:]
        kk = pl.program_id(2)
        p = _dot(a_ref[...], b_ref[...], dims)

        def finish(total):
            if has_res:
                total = total + r_ref[...]
            o_ref[...] = total.astype(o_ref.dtype)

        if nk == 1:
            finish(p)
        else:
            @pl.when(kk == 0)
            def _():
                acc_ref[...] = p

            @pl.when(jnp.logical_and(kk > 0, kk < nk - 1))
            def _():
                acc_ref[...] += p

            @pl.when(kk == nk - 1)
            def _():
                finish(acc_ref[...] + p)

    in_specs = [a_spec, b_spec] + ([o_spec] if has_res else []) + ([ANY] if after is not None else [])
    args = (a, b) + ((residual,) if has_res else ()) + ((after,) if after is not None else ())
    acc_shape = (tm, tn) if nk > 1 else (8, LANES)
    return pl.pallas_call(
        body,
        name=name,
        grid=(m // tm, n // tn, nk),
        in_specs=in_specs,
        out_specs=o_spec,
        out_shape=jax.ShapeDtypeStruct((m, n), out_dtype),
        scratch_shapes=[pltpu.VMEM(acc_shape, F32)],
        compiler_params=_params("parallel", "parallel", "arbitrary"),
    )(*args)


def _rmsnorm_fwd(x, g, name, after=None):
    t, d = x.shape
    tr = _divisor_tile(t, 512, 8)

    def body(x_ref, g_ref, *rest):
        h_ref = rest[-1]
        xv = x_ref[...]
        rstd = lax.rsqrt(jnp.mean(xv * xv, axis=-1, keepdims=True) + EPS)
        h_ref[...] = (xv * rstd * g_ref[...]).astype(h_ref.dtype)

    return pl.pallas_call(
        body,
        name=name,
        grid=(t // tr,),
        in_specs=[pl.BlockSpec((tr, d), lambda i: (i, 0)), pl.BlockSpec((1, d), lambda i: (0, 0))]
        + ([ANY] if after is not None else []),
        out_specs=pl.BlockSpec((tr, d), lambda i: (i, 0)),
        out_shape=jax.ShapeDtypeStruct((t, d), BF16),
        compiler_params=_params("parallel"),
    )(x, g, *(() if after is None else (after,)))


def _rmsnorm_bwd(x, g, dh, dres, name, after=None):
    t, d = x.shape
    tr = _divisor_tile(t, 256, 8)

    def body(x_ref, g_ref, dh_ref, dres_ref, *rest):
        dx_ref, dx16_ref, dg_ref = rest[-3:]
        i = pl.program_id(0)
        xv = x_ref[...]
        rstd = lax.rsqrt(jnp.mean(xv * xv, axis=-1, keepdims=True) + EPS)
        xh = xv * rstd
        dhv = dh_ref[...]
        dxh = dhv * g_ref[...]
        dx = dres_ref[...] + rstd * (dxh - xh * jnp.mean(dxh * xh, axis=-1, keepdims=True))
        dx_ref[...] = dx
        dx16_ref[...] = dx.astype(dx16_ref.dtype)
        part = jnp.broadcast_to(jnp.sum(dhv * xh, axis=0, keepdims=True), dg_ref.shape)

        @pl.when(i == 0)
        def _():
            dg_ref[...] = part

        @pl.when(i > 0)
        def _():
            dg_ref[...] += part

    row = pl.BlockSpec((tr, d), lambda i: (i, 0))
    return pl.pallas_call(
        body,
        name=name,
        grid=(t // tr,),
        in_specs=[row, pl.BlockSpec((1, d), lambda i: (0, 0)), row, row] + ([ANY] if after is not None else []),
        out_specs=[row, row, pl.BlockSpec((8, d), lambda i: (0, 0))],
        out_shape=[jax.ShapeDtypeStruct((t, d), F32), jax.ShapeDtypeStruct((t, d), BF16),
                   jax.ShapeDtypeStruct((8, d), F32)],
        compiler_params=_params("arbitrary"),
    )(x, g, dh, dres, *(() if after is None else (after,)))


def _lane(shape):
    return lax.broadcasted_iota(jnp.int32, shape, 1)


def _group_sum64(s):
    row = lax.broadcasted_iota(jnp.int32, (LANES, LANES), 0)
    col = lax.broadcasted_iota(jnp.int32, (LANES, LANES), 1)
    ones = jnp.where((row >= HEAD_DIM) == (col >= HEAD_DIM), 1.0, 0.0).astype(BF16)
    out = []
    for t in range(s.shape[1] // LANES):
        piece = s[:, LANES * t:LANES * t + LANES]
        hi = piece.astype(BF16)
        lo = (piece - hi.astype(F32)).astype(BF16)
        out.append(_dot(hi, ones, NN) + _dot(lo, ones, NN))
    return out[0] if len(out) == 1 else jnp.concatenate(out, axis=1)


def _swap32(x):
    w = x.shape[1]
    return jnp.where((_lane(x.shape) & 32) == 0, pltpu.roll(x, w - 32, axis=1), pltpu.roll(x, 32, axis=1))


def _rope(x, c, s):
    return x * c + _swap32(x) * s


def _rope_t(dy, c, s):
    return dy * c + _swap32(dy * s)


def _head_norm(x):
    rstd = lax.rsqrt(_group_sum64(x * x) * (1.0 / HEAD_DIM) + EPS)
    return x * rstd, rstd


def _head_norm_bwd(dxh, xh, rstd):
    return rstd * (dxh - xh * (_group_sum64(dxh * xh) * (1.0 / HEAD_DIM)))


def _roll64(x):
    return pltpu.roll(x, 64, axis=1)


def _attn_specs(wq, wk):
    kb = wq // wk
    prev = lambda i: jnp.maximum(i - 1, 0)
    return dict(
        q=pl.BlockSpec((BLOCK, wq), lambda i: (i, 0)),
        kc=pl.BlockSpec((BLOCK, wk), lambda i: (i, kb)),
        kp=pl.BlockSpec((BLOCK, wk), lambda i: (prev(i), kb)),
        vc=pl.BlockSpec((BLOCK, wk), lambda i: (i, kb + 1)),
        vp=pl.BlockSpec((BLOCK, wk), lambda i: (prev(i), kb + 1)),
        tq=pl.BlockSpec((BLOCK, wq), lambda i: (i, 0)),
        tkp=pl.BlockSpec((BLOCK, wk), lambda i: (prev(i), 0)),
        gq=pl.BlockSpec((1, wq), lambda i: (0, 0)),
        gk=pl.BlockSpec((1, wk), lambda i: (0, 0)),
        sinks=pl.BlockSpec(memory_space=pltpu.SMEM),
    )


def _attn_prologue(i, q_ref, kc_ref, kp_ref, cq_ref, sq_ref, ckp_ref, skp_ref, gq_ref, gk_ref):
    wk = kc_ref.shape[1]
    cq, sq = cq_ref[...], sq_ref[...]
    ck, sk = cq[:, :wk], sq[:, :wk]
    qh, q_rstd = _head_norm(q_ref[...])
    kch, kc_rstd = _head_norm(kc_ref[...])
    kph, kp_rstd = _head_norm(kp_ref[...])
    qn = _rope(qh * gq_ref[...], cq, sq)
    knc = _rope(kch * gk_ref[...], ck, sk)
    knp = _rope(kph * gk_ref[...], ckp_ref[...], skp_ref[...])
    stacked = (Q_PER_KV * BLOCK, BLOCK)
    row = lax.broadcasted_iota(jnp.int32, stacked, 0) & (BLOCK - 1)
    col = lax.broadcasted_iota(jnp.int32, stacked, 1)
    mask_c = col <= row
    mask_p = jnp.logical_and(col > row, i > 0)
    half = (lax.broadcasted_iota(jnp.int32, (BLOCK, BLOCK), 1) >= HEAD_DIM).astype(jnp.int32)
    return dict(cq=cq, sq=sq, ck=ck, sk=sk, qh=qh, q_rstd=q_rstd, kch=kch, kc_rstd=kc_rstd, kph=kph,
                kp_rstd=kp_rstd, qn=qn, knc=knc, knp=knp, mask_c=mask_c, mask_p=mask_p, half=half)


def _stack_heads(x, g, half):
    kpar = g % 2
    pieces = []
    for j in range(Q_PER_KV):
        t, e = divmod(Q_PER_KV * g + j, 2)
        piece = jnp.where(half == e, x[:, LANES * t:LANES * t + LANES], 0.0)
        pieces.append(piece if e == kpar else _roll64(piece))
    return jnp.concatenate(pieces, axis=0)


def _unstack_heads(y, g, half):
    kpar = g % 2
    slabs = {}
    for j in range(Q_PER_KV):
        t, e = divmod(Q_PER_KV * g + j, 2)
        piece = jnp.where(half == kpar, y[BLOCK * j:BLOCK * j + BLOCK], 0.0)
        piece = piece if e == kpar else _roll64(piece)
        slabs[t] = piece if t not in slabs else slabs[t] + piece
    return slabs


def _group_scores(st, g, sinks_ref, scale):
    ks = g // 2
    sl = slice(LANES * ks, LANES * ks + LANES)
    q4 = _stack_heads(st["qn"], g, st["half"]).astype(BF16)
    kc, kp = st["knc"][:, sl].astype(BF16), st["knp"][:, sl].astype(BF16)
    rows = Q_PER_KV * BLOCK
    at = lax.broadcasted_iota(jnp.int32, (rows, 1), 0)
    head = jnp.zeros((rows, 1), jnp.int32)
    sink = jnp.zeros((rows, 1), F32) + sinks_ref[Q_PER_KV * g]
    for j in range(1, Q_PER_KV):
        head = jnp.where(at >= BLOCK * j, j, head)
        sink = jnp.where(at >= BLOCK * j, sinks_ref[Q_PER_KV * g + j], sink)
    s_c = jnp.where(st["mask_c"], _dot(q4, kc, NT) * scale, NEG)
    s_p = jnp.where(st["mask_p"], _dot(q4, kp, NT) * scale, NEG)
    m = jnp.maximum(jnp.maximum(jnp.max(s_c, axis=1, keepdims=True), jnp.max(s_p, axis=1, keepdims=True)), sink)
    p_c, p_p = jnp.exp(s_c - m), jnp.exp(s_p - m)
    p_s = jnp.exp(sink - m)
    inv = 1.0 / (jnp.sum(p_c, axis=1, keepdims=True) + jnp.sum(p_p, axis=1, keepdims=True) + p_s)
    return dict(sl=sl, head=head, q4=q4, kc=kc, kp=kp, pr_c=p_c * inv, pr_p=p_p * inv, pr_s=p_s * inv)


def _attn_fwd(proj, tables, gq, gk, sinks, wq, wk, name, after=None):
    t = proj.shape[0]
    nb = t // BLOCK
    sp = _attn_specs(wq, wk)
    scale = HEAD_DIM ** -0.5
    cos_t, sin_t, cos_k, sin_k = tables

    def body(sinks_ref, q_ref, kc_ref, kp_ref, vc_ref, vp_ref, cq_ref, sq_ref, ckp_ref, skp_ref, gq_ref, gk_ref,
             *rest):
        o_ref = rest[-1]
        i = pl.program_id(0)
        st = _attn_prologue(i, q_ref, kc_ref, kp_ref, cq_ref, sq_ref, ckp_ref, skp_ref, gq_ref, gk_ref)
        for g in range(wq // (Q_PER_KV * HEAD_DIM)):
            gs = _group_scores(st, g, sinks_ref, scale)
            own = st["half"] == g % 2
            vc = jnp.where(own, vc_ref[:, gs["sl"]], 0.0).astype(BF16)
            vp = jnp.where(own, vp_ref[:, gs["sl"]], 0.0).astype(BF16)
            out = _dot(gs["pr_c"].astype(BF16), vc, NN) + _dot(gs["pr_p"].astype(BF16), vp, NN)
            for ts, slab in _unstack_heads(out, g, st["half"]).items():
                o_ref[:, LANES * ts:LANES * ts + LANES] = slab.astype(o_ref.dtype)

    return pl.pallas_call(
        body,
        name=name,
        grid=(nb,),
        in_specs=[sp["sinks"], sp["q"], sp["kc"], sp["kp"], sp["vc"], sp["vp"], sp["tq"], sp["tq"], sp["tkp"],
                  sp["tkp"], sp["gq"], sp["gk"]] + ([ANY] if after is not None else []),
        out_specs=pl.BlockSpec((BLOCK, wq), lambda i: (i, 0)),
        out_shape=jax.ShapeDtypeStruct((t, wq), BF16),
        compiler_params=_params("parallel"),
    )(sinks, proj, proj, proj, proj, proj, cos_t, sin_t, cos_k, sin_k, gq, gk, *(() if after is None else (after,)))


def _attn_bwd(proj, dout, tables, gq, gk, sinks, wq, wk, name, after=None):
    t = proj.shape[0]
    nb = t // BLOCK
    sp = _attn_specs(wq, wk)
    scale = HEAD_DIM ** -0.5
    cos_t, sin_t, cos_k, sin_k = tables

    def body(sinks_ref, q_ref, kc_ref, kp_ref, vc_ref, vp_ref, cq_ref, sq_ref, ckp_ref, skp_ref, gq_ref, gk_ref,
             do_ref, *rest):
        dq_ref, dk_ref, dv_ref, dgq_ref, dgk_ref, dsk_ref, dqn_ref, dknc_ref, dknp_ref, dvc_ref, dvp_ref = rest[-11:]
        i = pl.program_id(0)
        st = _attn_prologue(i, q_ref, kc_ref, kp_ref, cq_ref, sq_ref, ckp_ref, skp_ref, gq_ref, gk_ref)
        dknc_ref[...] = jnp.zeros_like(dknc_ref)
        dknp_ref[...] = jnp.zeros_like(dknp_ref)
        dvc_ref[...] = jnp.zeros_like(dvc_ref)
        dvp_ref[...] = jnp.zeros_like(dvp_ref)
        lane8 = _lane((8, LANES))
        dsinks = jnp.zeros((8, LANES), F32)
        for g in range(wq // (Q_PER_KV * HEAD_DIM)):
            gs = _group_scores(st, g, sinks_ref, scale)
            sl = gs["sl"]
            do4 = _stack_heads(do_ref[...], g, st["half"]).astype(BF16)
            dp_c = _dot(do4, vc_ref[:, sl].astype(BF16), NT)
            dp_p = _dot(do4, vp_ref[:, sl].astype(BF16), NT)
            pr_c, pr_p = gs["pr_c"], gs["pr_p"]
            rs = jnp.sum(pr_c * dp_c, axis=1, keepdims=True) + jnp.sum(pr_p * dp_p, axis=1, keepdims=True)
            ds_c = (pr_c * (dp_c - rs) * scale).astype(BF16)
            ds_p = (pr_p * (dp_p - rs) * scale).astype(BF16)
            dsink_rows = -gs["pr_s"] * rs
            for j in range(Q_PER_KV):
                dsink = jnp.sum(jnp.where(gs["head"] == j, dsink_rows, 0.0))
                dsinks = dsinks + jnp.where(lane8 == Q_PER_KV * g + j, dsink, 0.0)
            dq4 = _dot(ds_c, gs["kc"], NN) + _dot(ds_p, gs["kp"], NN)
            for ts, slab in _unstack_heads(dq4, g, st["half"]).items():
                dqn_ref[:, LANES * ts:LANES * ts + LANES] = slab
            dvc_ref[:, sl] += _dot(pr_c.astype(BF16), do4, TN)
            dvp_ref[:, sl] += _dot(pr_p.astype(BF16), do4, TN)
            dknc_ref[:, sl] += _dot(ds_c, gs["q4"], TN)
            dknp_ref[:, sl] += _dot(ds_p, gs["q4"], TN)

        gqv, gkv = gq_ref[...], gk_ref[...]
        dqg = _rope_t(dqn_ref[...], st["cq"], st["sq"])
        dq_ref[...] = _head_norm_bwd(dqg * gqv, st["qh"], st["q_rstd"]).astype(dq_ref.dtype)
        dkcg = _rope_t(dknc_ref[...], st["ck"], st["sk"])
        dkpg = _rope_t(dknp_ref[...], ckp_ref[...], skp_ref[...])
        dk_cur = _head_norm_bwd(dkcg * gkv, st["kch"], st["kc_rstd"])
        dk_prev = _head_norm_bwd(dkpg * gkv, st["kph"], st["kp_rstd"])
        dgq_part = jnp.broadcast_to(jnp.sum(dqg * st["qh"], axis=0, keepdims=True), dgq_ref.shape)
        dgk_part = jnp.broadcast_to(
            jnp.sum(dkcg * st["kch"] + dkpg * st["kph"], axis=0, keepdims=True), dgk_ref.shape)
        cur = pl.ds(pl.multiple_of(i * BLOCK, BLOCK), BLOCK)
        dk_ref[cur, :] = dk_cur
        dv_ref[cur, :] = dvc_ref[...]

        @pl.when(i == 0)
        def _():
            dgq_ref[...] = dgq_part
            dgk_ref[...] = dgk_part
            dsk_ref[...] = dsinks

        @pl.when(i > 0)
        def _():
            before = pl.ds(pl.multiple_of((i - 1) * BLOCK, BLOCK), BLOCK)
            dk_ref[before, :] += dk_prev
            dv_ref[before, :] += dvp_ref[...]
            dgq_ref[...] += dgq_part
            dgk_ref[...] += dgk_part
            dsk_ref[...] += dsinks

    whole = lambda shape: pl.BlockSpec(shape, lambda i: (0, 0))
    return pl.pallas_call(
        body,
        name=name,
        grid=(nb,),
        in_specs=[sp["sinks"], sp["q"], sp["kc"], sp["kp"], sp["vc"], sp["vp"], sp["tq"], sp["tq"], sp["tkp"],
                  sp["tkp"], sp["gq"], sp["gk"], pl.BlockSpec((BLOCK, wq), lambda i: (i, 0))]
        + ([ANY] if after is not None else []),
        out_specs=[pl.BlockSpec((BLOCK, wq), lambda i: (i, 0)), whole((t, wk)), whole((t, wk)), whole((8, wq)),
                   whole((8, wk)), whole((8, LANES))],
        out_shape=[jax.ShapeDtypeStruct((t, wq), BF16), jax.ShapeDtypeStruct((t, wk), F32),
                   jax.ShapeDtypeStruct((t, wk), F32), jax.ShapeDtypeStruct((8, wq), F32),
                   jax.ShapeDtypeStruct((8, wk), F32), jax.ShapeDtypeStruct((8, LANES), F32)],
        scratch_shapes=[pltpu.VMEM((BLOCK, wq), F32), pltpu.VMEM((BLOCK, wk), F32), pltpu.VMEM((BLOCK, wk), F32),
                        pltpu.VMEM((BLOCK, wk), F32), pltpu.VMEM((BLOCK, wk), F32)],
        compiler_params=_params("arbitrary"),
    )(sinks, proj, proj, proj, proj, proj, cos_t, sin_t, cos_k, sin_k, gq, gk, dout,
      *(() if after is None else (after,)))


_GELU_K = math.sqrt(2.0 / math.pi)
_GELU_A = 0.044715


def _gelu(x):
    return 0.5 * x * (1.0 + jnp.tanh(_GELU_K * (x + _GELU_A * x * x * x)))


def _gelu_grad(x):
    th = jnp.tanh(_GELU_K * (x + _GELU_A * x * x * x))
    return 0.5 * (1.0 + th) + 0.5 * x * (1.0 - th * th) * (_GELU_K * (1.0 + 3.0 * _GELU_A * x * x))


def _group_ln(v):
    mu = jnp.mean(v, axis=1, keepdims=True)
    cen = v - mu
    rstd = lax.rsqrt(jnp.mean(cen * cen, axis=1, keepdims=True) + EPS)
    return cen * rstd, rstd


def _sgu_geometry(off_u, ws):
    cw = math.gcd(off_u, ws)
    return cw, ws // cw, off_u // cw, (off_u + ws) // cw


def _sgu_fwd(proj, ln_g, ln_b, w_s, bt, off_u, ws, name, after=None):
    t = proj.shape[0]
    nb = t // BLOCK
    cw, nc, ub, vb = _sgu_geometry(off_u, ws)
    gpc = cw // LANES
    ng = ws // LANES

    def body(u_ref, v_ref, g_ref, b_ref, w_ref, bt_ref, *rest):
        o_ref = rest[-1]
        jc = pl.program_id(0)
        row = lax.broadcasted_iota(jnp.int32, (BLOCK, BLOCK), 0)
        col = lax.broadcasted_iota(jnp.int32, (BLOCK, BLOCK), 1)
        lane_g = _lane((BLOCK, ng))
        for gi in range(gpc):
            sl = slice(LANES * gi, LANES * gi + LANES)
            xh, _ = _group_ln(_gelu(v_ref[:, sl]))
            vn = xh * g_ref[:, sl] + b_ref[:, sl]
            w = jnp.where(row >= col, w_ref[gi], 0.0).astype(BF16)
            bias = jnp.sum(jnp.where(lane_g == jc * gpc + gi, bt_ref[...], 0.0), axis=1, keepdims=True)
            s = _dot(w, vn.astype(BF16), NN) + bias
            o_ref[:, sl] = (_gelu(u_ref[:, sl]) * s).astype(o_ref.dtype)

    return pl.pallas_call(
        body,
        name=name,
        grid=(nc, nb),
        in_specs=[pl.BlockSpec((BLOCK, cw), lambda jc, i: (i, ub + jc)),
                  pl.BlockSpec((BLOCK, cw), lambda jc, i: (i, vb + jc)),
                  pl.BlockSpec((1, cw), lambda jc, i: (0, jc)),
                  pl.BlockSpec((1, cw), lambda jc, i: (0, jc)),
                  pl.BlockSpec((gpc, BLOCK, BLOCK), lambda jc, i: (jc, 0, 0)),
                  pl.BlockSpec((BLOCK, ng), lambda jc, i: (0, 0))] + ([ANY] if after is not None else []),
        out_specs=pl.BlockSpec((BLOCK, cw), lambda jc, i: (i, jc)),
        out_shape=jax.ShapeDtypeStruct((t, ws), BF16),
        compiler_params=_params("parallel", "parallel"),
    )(proj, proj, ln_g, ln_b, w_s, bt, *(() if after is None else (after,)))


def _sgu_bwd(proj, dout, ln_g, ln_b, w_s, bt, off_u, ws, name):
    t = proj.shape[0]
    nb = t // BLOCK
    cw, nc, ub, vb = _sgu_geometry(off_u, ws)
    gpc = cw // LANES
    ng = ws // LANES

    def body(u_ref, v_ref, g_ref, b_ref, w_ref, bt_ref, do_ref, du_ref, dv_ref, dg_ref, db_ref, dw_ref, dbs_ref,
             bacc_ref):
        jc = pl.program_id(0)
        i = pl.program_id(1)
        row = lax.broadcasted_iota(jnp.int32, (BLOCK, BLOCK), 0)
        col = lax.broadcasted_iota(jnp.int32, (BLOCK, BLOCK), 1)
        lane_g = _lane((BLOCK, ng))
        tri = row >= col

        @pl.when(i == 0)
        def _():
            dg_ref[...] = jnp.zeros_like(dg_ref)
            db_ref[...] = jnp.zeros_like(db_ref)
            dw_ref[...] = jnp.zeros_like(dw_ref)
            bacc_ref[...] = jnp.zeros_like(bacc_ref)

        for gi in range(gpc):
            sl = slice(LANES * gi, LANES * gi + LANES)
            u_raw, v_raw = u_ref[:, sl], v_ref[:, sl]
            xh, rstd = _group_ln(_gelu(v_raw))
            gam = g_ref[:, sl]
            vn = (xh * gam + b_ref[:, sl]).astype(BF16)
            w = jnp.where(tri, w_ref[gi], 0.0)
            bias = jnp.sum(jnp.where(lane_g == jc * gpc + gi, bt_ref[...], 0.0), axis=1, keepdims=True)
            s = _dot(w.astype(BF16), vn, NN) + bias
            dov = do_ref[:, sl]
            du_ref[:, sl] = (dov * s * _gelu_grad(u_raw)).astype(du_ref.dtype)
            ds = dov * _gelu(u_raw)
            ds16 = ds.astype(BF16)
            dw_ref[gi] += jnp.where(tri, _dot(ds16, vn, NT), 0.0)
            bacc_ref[gi] += ds
            dvn = _dot(w.T.astype(BF16), ds16, NN)
            dg_ref[:, sl] += jnp.broadcast_to(jnp.sum(dvn * xh, axis=0, keepdims=True), (8, LANES))
            db_ref[:, sl] += jnp.broadcast_to(jnp.sum(dvn, axis=0, keepdims=True), (8, LANES))
            dxh = dvn * gam
            dvg = rstd * (dxh - jnp.mean(dxh, axis=1, keepdims=True)
                          - xh * jnp.mean(dxh * xh, axis=1, keepdims=True))
            dv_ref[:, sl] = (dvg * _gelu_grad(v_raw)).astype(dv_ref.dtype)

        @pl.when(i == nb - 1)
        def _():
            for gi in range(gpc):
                dbs_ref[gi] = jnp.broadcast_to(jnp.sum(bacc_ref[gi].T, axis=0, keepdims=True), (8, LANES))

    blk = lambda base: pl.BlockSpec((BLOCK, cw), lambda jc, i: (i, base + jc))
    vec = pl.BlockSpec((1, cw), lambda jc, i: (0, jc))
    acc = pl.BlockSpec((8, cw), lambda jc, i: (0, jc))
    wsp = pl.BlockSpec((gpc, BLOCK, BLOCK), lambda jc, i: (jc, 0, 0))
    return pl.pallas_call(
        body,
        name=name,
        grid=(nc, nb),
        in_specs=[blk(ub), blk(vb), vec, vec, wsp, pl.BlockSpec((BLOCK, ng), lambda jc, i: (0, 0)), blk(0)],
        out_specs=[blk(0), blk(0), acc, acc, wsp, pl.BlockSpec((gpc, 8, LANES), lambda jc, i: (jc, 0, 0))],
        out_shape=[jax.ShapeDtypeStruct((t, ws), BF16), jax.ShapeDtypeStruct((t, ws), BF16),
                   jax.ShapeDtypeStruct((8, ws), F32), jax.ShapeDtypeStruct((8, ws), F32),
                   jax.ShapeDtypeStruct((ng, BLOCK, BLOCK), F32), jax.ShapeDtypeStruct((ng, 8, LANES), F32)],
        scratch_shapes=[pltpu.VMEM((gpc, BLOCK, BLOCK), F32)],
        compiler_params=_params("arbitrary", "arbitrary"),
    )(proj, proj, ln_g, ln_b, w_s, bt, dout)


def _sigmoid(x):
    return 1.0 / (1.0 + jnp.exp(-x))


def _merge_geometry(off_g, d):
    cw = math.gcd(off_g, d)
    return cw, d // cw, off_g // cw, (off_g + d) // cw


def _branches_fwd(attn, sgu, wab_t, wsb_t, proj, off_g, name):
    t = attn.shape[0]
    d = wab_t.shape[0]
    tn, _, ab, bb = _merge_geometry(off_g, d)
    tm = _divisor_tile(t, 1024, 128)

    def body(a1_ref, a2_ref, b1_ref, b2_ref, la_ref, lb_ref, bra_ref, brb_ref, o_ref):
        for rows in _row_chunks(tm):
            va = _dot(a1_ref[rows, :], b1_ref[...], NT)
            vb = _dot(a2_ref[rows, :], b2_ref[...], NT)
            bra_ref[rows, :] = va
            brb_ref[rows, :] = vb
            o_ref[rows, :] = (_sigmoid(la_ref[rows, :]) * va + _sigmoid(lb_ref[rows, :]) * vb).astype(o_ref.dtype)

    rows = lambda w: pl.BlockSpec((tm, w), lambda i, j: (i, 0))
    wrow = lambda w: pl.BlockSpec((tn, w), lambda i, j: (j, 0))
    blk = lambda base: pl.BlockSpec((tm, tn), lambda i, j: (i, base + j))
    return pl.pallas_call(
        body,
        name=name,
        grid=(t // tm, d // tn),
        in_specs=[rows(attn.shape[1]), rows(sgu.shape[1]), wrow(wab_t.shape[1]), wrow(wsb_t.shape[1]), blk(ab),
                  blk(bb)],
        out_specs=[blk(0)] * 3,
        out_shape=[jax.ShapeDtypeStruct((t, d), F32), jax.ShapeDtypeStruct((t, d), F32),
                   jax.ShapeDtypeStruct((t, d), BF16)],
        compiler_params=_params("parallel", "parallel"),
    )(attn, sgu, wab_t, wsb_t, proj, proj)


def _branches_bwd(dx16, wo, br_a, br_b, proj, off_g, name, after=None):
    t, d = br_a.shape
    tn, _, ab, bb = _merge_geometry(off_g, d)
    tm = _divisor_tile(t, 1024, 128)
    k = dx16.shape[1]

    def body(a_ref, b_ref, bra_ref, brb_ref, la_ref, lb_ref, *rest):
        da_ref, db_ref, dla_ref, dlb_ref = rest[-4:]
        for rows in _row_chunks(tm):
            dmv = _dot(a_ref[rows, :], b_ref[...], NT)
            ga, gb = _sigmoid(la_ref[rows, :]), _sigmoid(lb_ref[rows, :])
            da_ref[rows, :] = (dmv * ga).astype(da_ref.dtype)
            db_ref[rows, :] = (dmv * gb).astype(db_ref.dtype)
            dla_ref[rows, :] = (dmv * bra_ref[rows, :] * ga * (1.0 - ga)).astype(dla_ref.dtype)
            dlb_ref[rows, :] = (dmv * brb_ref[rows, :] * gb * (1.0 - gb)).astype(dlb_ref.dtype)

    blk = lambda base: pl.BlockSpec((tm, tn), lambda i, j: (i, base + j))
    return pl.pallas_call(
        body,
        name=name,
        grid=(t // tm, d // tn),
        in_specs=[pl.BlockSpec((tm, k), lambda i, j: (i, 0)), pl.BlockSpec((tn, k), lambda i, j: (j, 0)), blk(0),
                  blk(0), blk(ab), blk(bb)] + ([ANY] if after is not None else []),
        out_specs=[blk(0)] * 4,
        out_shape=[jax.ShapeDtypeStruct((t, d), BF16)] * 4,
        compiler_params=_params("parallel", "parallel"),
    )(dx16, wo, br_a, br_b, proj, proj, *(() if after is None else (after,)))


def _gate_up_fwd(h2, wgu_t, name, after=None):
    t, d = h2.shape
    f = wgu_t.shape[0] // 2
    tm = _divisor_tile(t, 1024, 128)
    tn = _divisor_tile(f, 512, 128)
    nb = f // tn

    def body(a_ref, bg_ref, bu_ref, *rest):
        gu_ref, act_ref = rest[-2:]
        for rows in _row_chunks(tm):
            av = a_ref[rows, :]
            gv = _dot(av, bg_ref[...], NT)
            uv = _dot(av, bu_ref[...], NT)
            gu_ref[0, rows, :] = gv
            gu_ref[1, rows, :] = uv
            act_ref[rows, :] = (gv * _sigmoid(gv) * uv).astype(act_ref.dtype)

    return pl.pallas_call(
        body,
        name=name,
        grid=(t // tm, nb),
        in_specs=[pl.BlockSpec((tm, d), lambda i, j: (i, 0)), pl.BlockSpec((tn, d), lambda i, j: (j, 0)),
                  pl.BlockSpec((tn, d), lambda i, j: (j + nb, 0))] + ([ANY] if after is not None else []),
        out_specs=[pl.BlockSpec((2, tm, tn), lambda i, j: (0, i, j)), pl.BlockSpec((tm, tn), lambda i, j: (i, j))],
        out_shape=[jax.ShapeDtypeStruct((2, t, f), F32), jax.ShapeDtypeStruct((t, f), BF16)],
        compiler_params=_params("parallel", "parallel"),
    )(h2, wgu_t, wgu_t, *(() if after is None else (after,)))


def _gate_up_bwd(dx16, wd, gu, name, after=None):
    t, d = dx16.shape
    f = wd.shape[0]
    tm = _divisor_tile(t, 1024, 128)
    tn = _divisor_tile(f, 512, 128)

    def body(a_ref, b_ref, gu_ref, *rest):
        o_ref = rest[-1]
        for rows in _row_chunks(tm):
            dav = _dot(a_ref[rows, :], b_ref[...], NT)
            gv = gu_ref[0, rows, :]
            sg = _sigmoid(gv)
            o_ref[0, rows, :] = (dav * gu_ref[1, rows, :] * (sg + gv * sg * (1.0 - sg))).astype(o_ref.dtype)
            o_ref[1, rows, :] = (dav * gv * sg).astype(o_ref.dtype)

    pair = pl.BlockSpec((2, tm, tn), lambda i, j: (0, i, j))
    return pl.pallas_call(
        body,
        name=name,
        grid=(t // tm, f // tn),
        in_specs=[pl.BlockSpec((tm, d), lambda i, j: (i, 0)), pl.BlockSpec((tn, d), lambda i, j: (j, 0)), pair]
        + ([ANY] if after is not None else []),
        out_specs=pair,
        out_shape=jax.ShapeDtypeStruct((2, t, f), BF16),
        compiler_params=_params("parallel", "parallel"),
    )(dx16, wd, gu, *(() if after is None else (after,)))


def _loss_and_grad(y, target, name):
    t, d = y.shape
    tr = _divisor_tile(t, 512, 8)

    def body(y_ref, t_ref, l_ref, dy_ref, dy16_ref):
        i = pl.program_id(0)
        err = y_ref[...] - t_ref[...]
        dy_ref[...] = err * (1.0 / d)
        dy16_ref[...] = (err * (1.0 / d)).astype(dy16_ref.dtype)
        part = jnp.broadcast_to(0.5 * jnp.sum(err * err) * (1.0 / d), l_ref.shape)

        @pl.when(i == 0)
        def _():
            l_ref[...] = part

        @pl.when(i > 0)
        def _():
            l_ref[...] += part

    row = pl.BlockSpec((tr, d), lambda i: (i, 0))
    return pl.pallas_call(
        body,
        name=name,
        grid=(t // tr,),
        in_specs=[row, row],
        out_specs=[pl.BlockSpec((8, LANES), lambda i: (0, 0)), row, row],
        out_shape=[jax.ShapeDtypeStruct((8, LANES), F32), jax.ShapeDtypeStruct((t, d), F32),
                   jax.ShapeDtypeStruct((t, d), BF16)],
        compiler_params=_params("arbitrary"),
    )(y, target)


def _adam_math(w, g, m, v):
    m = ADAM_B1 * m + (1.0 - ADAM_B1) * g
    v = ADAM_B2 * v + (1.0 - ADAM_B2) * (g * g)
    m_hat = m / (1.0 - ADAM_B1 ** ADAM_STEP)
    v_hat = v / (1.0 - ADAM_B2 ** ADAM_STEP)
    delta = -ADAM_LR * (m_hat / (jnp.sqrt(v_hat) + ADAM_EPS) + ADAM_WD * w)
    return delta, m, v


def _row_tile(r, c, elems=512 * 1024):
    return _divisor_tile(r, max(8, elems // c // 8 * 8), 8)


def _adam(w, grads, m, v, chip, name, after=None):
    nl, r, c = w.shape
    tr = _row_tile(r, c, 256 * 1024)
    nb = r // tr
    counts = [len(terms) for terms, _ in grads]

    def body(chip_ref, *refs):
        w_ref, m_ref, v_ref = refs[:3]
        g_ref, d_ref, nm_ref, nv_ref = refs[-4:]
        layer = pl.program_id(0)
        g, at = None, 3
        for li, n in enumerate(counts):
            total = refs[at][...].astype(F32)
            for ref in refs[at + 1:at + n]:
                total = total + ref[...].astype(F32)
            g = total if g is None else jnp.where(layer == li, total, g)
            at += n
        g_ref[...] = g
        d_ref[...], nm_ref[...], nv_ref[...] = _adam_math(w_ref[...], g, m_ref[...], v_ref[...])

    def term_spec(li, p, by_owner):
        def index(l, i, chip_ref):
            rows = jnp.where(l < li, 0, jnp.where(l > li, nb - 1, i))
            return (p, chip_ref[0] if by_owner else 0, rows, 0)
        return pl.BlockSpec((None, None, tr, c), index)

    row = pl.BlockSpec((None, tr, c), lambda l, i, chip_ref: (l, i, 0))
    specs, arrays = [], []
    for li, (terms, p) in enumerate(grads):
        for term in terms:
            specs.append(term_spec(li, p, term.shape[1] == 4))
            arrays.append(term)
    return pl.pallas_call(
        body,
        name=name,
        grid_spec=pltpu.PrefetchScalarGridSpec(
            num_scalar_prefetch=1, grid=(nl, nb),
            in_specs=[row] * 3 + specs + ([ANY] if after is not None else []), out_specs=[row] * 4),
        out_shape=[jax.ShapeDtypeStruct((nl, r, c), F32)] * 4,
        compiler_params=_params("arbitrary", "arbitrary"),
    )(chip, w, m, v, *arrays, *(() if after is None else (after,)))


def _place_shard(parts, layer, dev, out_dtype, name, after=None):
    p = len(parts)
    _, r, c = parts[0].shape
    tr = _row_tile(r, c)

    def body(dev_ref, *refs):
        o_ref = refs[-1]
        x = refs[0][...]
        for pi in range(1, p):
            x = jnp.where(pl.program_id(0) == pi, refs[pi][...], x)
        o_ref[...] = x.astype(o_ref.dtype)

    return pl.pallas_call(
        body,
        name=name,
        grid_spec=pltpu.PrefetchScalarGridSpec(
            num_scalar_prefetch=1,
            grid=(p, r // tr),
            in_specs=[pl.BlockSpec((None, tr, c), lambda pi, i, dev_ref: (layer, i, 0))] * p
            + ([ANY] if after is not None else []),
            out_specs=pl.BlockSpec((None, None, tr, c), lambda pi, i, dev_ref: (pi, dev_ref[0], i, 0)),
        ),
        out_shape=jax.ShapeDtypeStruct((p, N_DEV, r, c), out_dtype),
        compiler_params=_params("parallel", "parallel"),
    )(dev, *parts, *(() if after is None else (after,)))


def _sum_sibling(g, land, core, name):
    p, _, _, r, c = g.shape
    tr = _row_tile(r, c, 1024 * 1024)

    def body(core_ref, g_ref, l_ref, o_ref):
        o_ref[...] = (g_ref[...].astype(F32) + l_ref[...].astype(F32)).astype(o_ref.dtype)

    return pl.pallas_call(
        body,
        name=name,
        grid_spec=pltpu.PrefetchScalarGridSpec(
            num_scalar_prefetch=1,
            grid=(p, 4, r // tr),
            in_specs=[pl.BlockSpec((None, None, None, tr, c), lambda pi, q, i, core_ref: (pi, q, core_ref[0], i, 0)),
                      pl.BlockSpec((None, None, None, tr, c), lambda pi, q, i, core_ref: (pi, q, 0, i, 0))],
            out_specs=pl.BlockSpec((None, None, tr, c), lambda pi, q, i, core_ref: (pi, q, i, 0)),
        ),
        out_shape=jax.ShapeDtypeStruct((p, 4, r, c), BF16),
        compiler_params=_params("parallel", "parallel", "parallel"),
    )(core, g, land)


def _sum_chips(s, lands, chip, name):
    p, _, r, c = s.shape
    tr = _row_tile(r, c)

    def body(chip_ref, s_ref, l0_ref, l1_ref, l2_ref, o_ref):
        total = s_ref[...].astype(F32) + l0_ref[...].astype(F32)
        o_ref[...] = total + l1_ref[...].astype(F32) + l2_ref[...].astype(F32)

    land_spec = pl.BlockSpec((None, None, tr, c), lambda pi, i, chip_ref: (pi, 0, i, 0))
    return pl.pallas_call(
        body,
        name=name,
        grid_spec=pltpu.PrefetchScalarGridSpec(
            num_scalar_prefetch=1,
            grid=(p, r // tr),
            in_specs=[pl.BlockSpec((None, None, tr, c), lambda pi, i, chip_ref: (pi, chip_ref[0], i, 0)),
                      land_spec, land_spec, land_spec],
            out_specs=pl.BlockSpec((None, tr, c), lambda pi, i, chip_ref: (pi, i, 0)),
        ),
        out_shape=jax.ShapeDtypeStruct((p, r, c), F32),
        compiler_params=_params("parallel", "parallel"),
    )(chip, s, *lands)


def _small_reduce_adam(gathered, w, m, v, name):
    _, r, c = gathered.shape
    tr = _row_tile(r, c)

    def body(p_ref, w_ref, m_ref, v_ref, g_ref, d_ref, nm_ref, nv_ref):
        g = p_ref[0]
        for j in range(1, N_DEV):
            g = g + p_ref[j]
        g_ref[...] = g
        d_ref[...], nm_ref[...], nv_ref[...] = _adam_math(w_ref[...], g, m_ref[...], v_ref[...])

    row = pl.BlockSpec((tr, c), lambda i: (i, 0))
    return pl.pallas_call(
        body,
        name=name,
        grid=(r // tr,),
        in_specs=[pl.BlockSpec((N_DEV, tr, c), lambda i: (0, i, 0)), row, row, row],
        out_specs=[row] * 4,
        out_shape=[jax.ShapeDtypeStruct((r, c), F32)] * 4,
        compiler_params=_params("parallel"),
    )(gathered, w, m, v)


def _place():
    return lax.axis_index("x"), lax.axis_index("y"), lax.axis_index("c")


HBM =pl.BlockSpec(memory_space=pltpu.HBM)
SEM = pl.BlockSpec(memory_space=pltpu.SEMAPHORE)
TOKEN = pl.BlockSpec(memory_space=pltpu.VMEM)
EFFECT = pltpu.SideEffectType.DATAFLOW_SIDE_EFFECTING


def _in_hbm(a):
    return pltpu.with_memory_space_constraint(a, pltpu.HBM)


_FLIPS = {"me": (0, 0, 0), "s": (0, 0, 1), "x": (1, 0, 0), "y": (0, 1, 0), "d": (1, 1, 0)}
GATHER_STAGES = (
    (("s", "me", "all"), ("x", "me", "all"), ("y", "me", "all")),
    (("s", "x", "all"), ("s", "y", "all"), ("y", "x", "first"), ("x", "y", "second")),
    (("s", "d", "all"),),
)


def _flipped(place, *names):
    out = list(place)
    for name in names:
        out = [1 - p if f else p for p, f in zip(out, _FLIPS[name])]
    return tuple(out)


def _block_part(ref, place, part):
    px, py, pc = place
    rows = ref.shape[2]
    span = {"all": pl.ds(0, rows), "first": pl.ds(0, rows // 2), "second": pl.ds(rows // 2, rows // 2)}[part]
    return ref.at[:, pl.ds(4 * px + 2 * py + pc, 1), span]


def _split_start(bufs, moves, name, after=None):
    n, nm = len(bufs), len(moves)
    extra = 0 if after is None else 1

    def body(*refs):
        ssem, rsem = refs[n + extra], refs[n + extra + 1]
        outs, token = refs[n + extra + 2:2 * n + extra + 2], refs[2 * n + extra + 2]
        me = _place()
        for a in range(n):
            for k, (to, owner, part) in enumerate(moves):
                piece = _block_part(outs[a], _flipped(me, owner), part)
                pltpu.make_async_remote_copy(
                    src_ref=piece, dst_ref=piece, send_sem=ssem.at[nm * a + k], recv_sem=rsem.at[nm * a + k],
                    device_id=_flipped(me, to), device_id_type=MESH).start()
        token[...] = jnp.zeros_like(token)

    outs = pl.pallas_call(
        body,
        name=name,
        in_specs=[HBM] * n + [ANY] * extra,
        out_specs=[SEM, SEM] + [HBM] * n + [TOKEN],
        out_shape=[pltpu.SemaphoreType.DMA((nm * n,))] * 2 + [pltpu.HBM(b.shape, b.dtype) for b in bufs]
        + [jax.ShapeDtypeStruct((8, LANES), F32)],
        input_output_aliases={i: 2 + i for i in range(n)},
        compiler_params=pltpu.CompilerParams(has_side_effects=EFFECT),
    )(*[_in_hbm(b) for b in bufs], *(() if after is None else (after,)))
    return outs[0], outs[1], list(outs[2:2 + n]), outs[-1]


def _split_wait(send_sems, recv_sems, bufs, moves, after, name):
    n, nm = len(bufs), len(moves)

    def body(*refs):
        ins, ssem, rsem = refs[:n], refs[n], refs[n + 1]
        me = _place()
        for a in range(n):
            for k, (to, owner, part) in enumerate(moves):
                landed = _block_part(ins[a], _flipped(me, owner, to), part)
                cp = pltpu.make_async_remote_copy(
                    src_ref=landed, dst_ref=landed, send_sem=ssem.at[nm * a + k], recv_sem=rsem.at[nm * a + k],
                    device_id=_flipped(me, to), device_id_type=MESH)
                cp.wait_send()
                cp.wait_recv()

    return pl.pallas_call(
        body,
        name=name,
        in_specs=[HBM] * n + [SEM, SEM, ANY],
        out_specs=[HBM] * n,
        out_shape=[pltpu.HBM(b.shape, b.dtype) for b in bufs],
        input_output_aliases={i: i for i in range(n)},
        compiler_params=pltpu.CompilerParams(has_side_effects=EFFECT),
    )(*bufs, send_sems, recv_sems, after)


def _chips_start(sums, name, after=None):
    n = len(sums)
    extra = 0 if after is None else 1

    def body(*refs):
        refs = refs[:4 * n] + refs[4 * n + extra:]
        ssem, rsem = refs[4 * n], refs[4 * n + 1]
        src, land = refs[4 * n + 2:5 * n + 2], refs[5 * n + 2:8 * n + 2]
        token = refs[8 * n + 2]
        x, y, c = _place()
        chips = [(1 - x, y), (x, 1 - y), (1 - x, 1 - y)]
        for a in range(n):
            for k, (px, py) in enumerate(chips):
                pltpu.make_async_remote_copy(
                    src_ref=src[a].at[:, pl.ds(2 * px + py, 1)], dst_ref=land[3 * a + k], send_sem=ssem.at[3 * a + k],
                    recv_sem=rsem.at[3 * a + k], device_id=(px, py, c), device_id_type=MESH).start()
        token[...] = jnp.zeros_like(token)

    lands = []
    for s in sums:
        lands += [lax.empty((s.shape[0], 1) + s.shape[2:], s.dtype) for _ in range(3)]
    outs = pl.pallas_call(
        body,
        name=name,
        in_specs=[HBM] * (4 * n) + [ANY] * extra,
        out_specs=[SEM, SEM] + [HBM] * (4 * n) + [TOKEN],
        out_shape=[pltpu.SemaphoreType.DMA((3 * n,))] * 2 + [pltpu.HBM(b.shape, b.dtype) for b in list(sums) + lands]
        + [jax.ShapeDtypeStruct((8, LANES), F32)],
        input_output_aliases={i: 2 + i for i in range(4 * n)},
        compiler_params=pltpu.CompilerParams(has_side_effects=EFFECT),
    )(*[_in_hbm(b) for b in list(sums) + lands], *(() if after is None else (after,)))
    return outs[0], outs[1], list(outs[2:2 + n]), list(outs[2 + n:2 + 4 * n]), outs[-1]


def _chips_wait(send_sems, recv_sems, sums, lands, after, name):
    n = len(sums)

    def body(*refs):
        src, land = refs[:n], refs[n:4 * n]
        ssem, rsem = refs[4 * n], refs[4 * n + 1]
        x, y, c = _place()
        chips = [(1 - x, y), (x, 1 - y), (1 - x, 1 - y)]
        for a in range(n):
            for k, (px, py) in enumerate(chips):
                cp = pltpu.make_async_remote_copy(
                    src_ref=src[a].at[:, pl.ds(2 * px + py, 1)], dst_ref=land[3 * a + k], send_sem=ssem.at[3 * a + k],
                    recv_sem=rsem.at[3 * a + k], device_id=(px, py, c), device_id_type=MESH)
                cp.wait_send()
                cp.wait_recv()

    both = list(sums) + list(lands)
    outs = pl.pallas_call(
        body,
        name=name,
        in_specs=[HBM] * (4 * n) + [SEM, SEM, ANY],
        out_specs=[HBM] * (4 * n),
        out_shape=[pltpu.HBM(b.shape, b.dtype) for b in both],
        input_output_aliases={i: i for i in range(4 * n)},
        compiler_params=pltpu.CompilerParams(has_side_effects=EFFECT),
    )(*both, send_sems, recv_sems, after)
    return list(outs[:n]), [list(outs[n + 3 * a:n + 3 * a + 3]) for a in range(n)]


def _sibling_start(grads, name):
    n = len(grads)

    def body(*refs):
        ssem, rsem = refs[2 * n], refs[2 * n + 1]
        src, land = refs[2 * n + 2:3 * n + 2], refs[3 * n + 2:4 * n + 2]
        token = refs[4 * n + 2]
        x, y, c = _place()
        for a in range(n):
            pltpu.make_async_remote_copy(
                src_ref=src[a].at[:, :, pl.ds(1 - c, 1)], dst_ref=land[a], send_sem=ssem.at[a], recv_sem=rsem.at[a],
                device_id=(x, y, 1 - c), device_id_type=MESH).start()
        token[...] = jnp.zeros_like(token)

    lands = [lax.empty(g.shape[:2] + (1,) + g.shape[3:], g.dtype) for g in grads]
    both = list(grads) + lands
    outs = pl.pallas_call(
        body,
        name=name,
        in_specs=[HBM] * (2 * n),
        out_specs=[SEM, SEM] + [HBM] * (2 * n) + [TOKEN],
        out_shape=[pltpu.SemaphoreType.DMA((n,))] * 2 + [pltpu.HBM(b.shape, b.dtype) for b in both]
        + [jax.ShapeDtypeStruct((8, LANES), F32)],
        input_output_aliases={i: 2 + i for i in range(2 * n)},
        compiler_params=pltpu.CompilerParams(has_side_effects=EFFECT),
    )(*[_in_hbm(b) for b in both])
    return outs[0], outs[1], list(outs[2:2 + n]), list(outs[2 + n:2 + 2 * n]), outs[-1]


def _sibling_wait(send_sems, recv_sems, grads, lands, after, name):
    n = len(grads)

    def body(*refs):
        src, land = refs[:n], refs[n:2 * n]
        ssem, rsem = refs[2 * n], refs[2 * n + 1]
        x, y, c = _place()
        for a in range(n):
            cp = pltpu.make_async_remote_copy(
                src_ref=src[a].at[:, :, pl.ds(1 - c, 1)], dst_ref=land[a], send_sem=ssem.at[a], recv_sem=rsem.at[a],
                device_id=(x, y, 1 - c), device_id_type=MESH)
            cp.wait_send()
            cp.wait_recv()

    both = list(grads) + list(lands)
    outs = pl.pallas_call(
        body,
        name=name,
        in_specs=[HBM] * (2 * n) + [SEM, SEM, ANY],
        out_specs=[HBM] * (2 * n),
        out_shape=[pltpu.HBM(b.shape, b.dtype) for b in both],
        input_output_aliases={i: i for i in range(2 * n)},
        compiler_params=pltpu.CompilerParams(has_side_effects=EFFECT),
    )(*both, send_sems, recv_sems, after)
    return list(outs[:n]), list(outs[n:])


_SMALL = ("mix_norm", "q_norm", "k_norm", "sinks", "sgu_ln_g", "sgu_ln_b", "w_spatial", "b_spatial", "ffn_norm")


def _pack_rows(a):
    flat = a.reshape(-1)
    pad = (-flat.shape[0]) % LANES
    if pad:
        flat = jnp.pad(flat, (0, pad))
    return flat.reshape(-1, LANES)


def _pack(values):
    rows = jnp.concatenate([_pack_rows(values[k]) for k in _SMALL], axis=0)
    pad = (-rows.shape[0]) % 8
    if pad:
        rows = jnp.pad(rows, ((0, pad), (0, 0)))
    return rows


def _unpack(rows, like):
    out, at = {}, 0
    for k in _SMALL:
        size = like[k].size
        nrows = -(-size // LANES)
        out[k] = rows[at:at + nrows].reshape(-1)[:size].reshape(like[k].shape)
        at += nrows
    return out


def _rope_tables(t, wq, wk):
    pos = jnp.arange(t, dtype=F32)
    inv_freq = jnp.power(ROPE_THETA, -jnp.arange(0, HEAD_DIM, 2, dtype=F32) / HEAD_DIM)
    ang = pos[:, None] * inv_freq[None, :]
    cos, sin = jnp.cos(ang), jnp.sin(ang)
    cos2, sin2 = jnp.concatenate([cos, cos], axis=1), jnp.concatenate([-sin, sin], axis=1)
    return (jnp.tile(cos2, (1, wq // HEAD_DIM)), jnp.tile(sin2, (1, wq // HEAD_DIM)),
            jnp.tile(cos2, (1, wk // HEAD_DIM)), jnp.tile(sin2, (1, wk // HEAD_DIM)))


def kernel(x, mix_norm, w_in, q_norm, k_norm, sinks, sgu_ln_g, sgu_ln_b, w_spatial, b_spatial, w_attn_branch, w_sgu_branch, w_out, ffn_norm, w_gate, w_up, w_down, loss_target, m_mix_norm, m_w_in, m_q_norm, m_k_norm, m_sinks, m_sgu_ln_g, m_sgu_ln_b, m_w_spatial, m_b_spatial, m_w_attn_branch, m_w_sgu_branch, m_w_out, m_ffn_norm, m_w_gate, m_w_up, m_w_down, v_mix_norm, v_w_in, v_q_norm, v_k_norm, v_sinks, v_sgu_ln_g, v_sgu_ln_b, v_w_spatial, v_b_spatial, v_w_attn_branch, v_w_sgu_branch, v_w_out, v_ffn_norm, v_w_gate, v_w_up, v_w_down):
    names = ("mix_norm", "w_in", "q_norm", "k_norm", "sinks", "sgu_ln_g", "sgu_ln_b", "w_spatial", "b_spatial",
             "w_attn_branch", "w_sgu_branch", "w_out", "ffn_norm", "w_gate", "w_up", "w_down")
    weights = dict(zip(names, (mix_norm, w_in, q_norm, k_norm, sinks, sgu_ln_g, sgu_ln_b, w_spatial, b_spatial,
                               w_attn_branch, w_sgu_branch, w_out, ffn_norm, w_gate, w_up, w_down)))
    mom1 = dict(zip(names, (m_mix_norm, m_w_in, m_q_norm, m_k_norm, m_sinks, m_sgu_ln_g, m_sgu_ln_b, m_w_spatial,
                            m_b_spatial, m_w_attn_branch, m_w_sgu_branch, m_w_out, m_ffn_norm, m_w_gate, m_w_up,
                            m_w_down)))
    mom2 = dict(zip(names, (v_mix_norm, v_w_in, v_q_norm, v_k_norm, v_sinks, v_sgu_ln_g, v_sgu_ln_b, v_w_spatial,
                            v_b_spatial, v_w_attn_branch, v_w_sgu_branch, v_w_out, v_ffn_norm, v_w_gate, v_w_up,
                            v_w_down)))
    depth = w_in.shape[0]
    _, t, d = x.shape
    n_q_heads = sinks.shape[1]
    wq = n_q_heads * HEAD_DIM
    wk = wq // Q_PER_KV
    ws = sgu_ln_g.shape[1]
    ng = ws // LANES
    off_u = wq + 2 * wk
    off_g = off_u + 2 * ws
    tables = _rope_tables(t, wq, wk)
    px, py, pc = _place()
    core = pc.astype(jnp.int32)[None]
    chip = (2 * px + py).astype(jnp.int32)[None]
    dev = (4 * px + 2 * py + pc).astype(jnp.int32)[None]

    layers = range(depth)
    chunks = ((0,), (1, 2, 3), (4,), (5,))
    sources = [[jnp.swapaxes(w_in, 1, 2)], [jnp.swapaxes(w_attn_branch, 1, 2)], [jnp.swapaxes(w_sgu_branch, 1, 2)],
               [w_out], [jnp.swapaxes(w_gate, 1, 2), jnp.swapaxes(w_up, 1, 2)], [w_down]]
    stream = [(l, ci) for l in layers for ci in range(len(chunks))]
    placed, state, token = {}, {}, None

    def send(key, after):
        state[key] = _split_start(placed[key], GATHER_STAGES[0], "gather_send_%d_%d" % key, after)
        return state[key][3]

    def advance(key, after, stage):
        send_sems, recv_sems, bufs, _ = state[key]
        bufs = _split_wait(send_sems, recv_sems, bufs, GATHER_STAGES[stage - 1], after, "gather_wait%d_%d_%d" % (stage, *key))
        state[key] = _split_start(bufs, GATHER_STAGES[stage], "gather_pass%d_%d_%d" % (stage, *key))
        return state[key][3]

    def relay(key, after):
        tok = advance(key, after, 1)
        at = stream.index(key)
        for later in stream[at + 2:at + 3] if at else stream[1:3]:
            tok = send(later, tok)
        return tok

    def ready(key, after):
        send_sems, recv_sems, bufs, _ = state.pop(key)
        bufs = _split_wait(send_sems, recv_sems, bufs, GATHER_STAGES[2], after, "gather_wait3_%d_%d" % key)
        return [f.reshape(f.shape[0] * f.shape[1] * f.shape[2], f.shape[3]) for f in bufs]

    for key in stream:
        l, ci = key
        placed[key] = [_place_shard(sources[a], l, dev, BF16, f"place_shard_{l}_{a}",
                                    after=token if a == chunks[ci][0] else None) for a in chunks[ci]]
        token = send(key, None) if key == stream[0] else placed[key][-1]

    saved = []
    xl = x[0]
    going = relay((0, 0), token)
    going = advance((0, 0), going, 2)
    for l in layers:
        gq = jnp.tile(q_norm[l], n_q_heads)[None]
        gk = jnp.tile(k_norm[l], n_q_heads // Q_PER_KV)[None]
        bt = b_spatial[l].T
        h = _rmsnorm_fwd(xl, mix_norm[l][None], f"mix_norm_fwd_{l}", after=going)
        (win_t,) = ready((l, 0), h)
        proj = _mm(h, win_t, "nt", F32, f"in_proj_{l}")
        going = relay((l, 1), proj)
        attn = _attn_fwd(proj, tables, gq, gk, sinks[l], wq, wk, f"attn_fwd_{l}", after=going)
        going = advance((l, 1), attn, 2)
        sgu = _sgu_fwd(proj, sgu_ln_g[l][None], sgu_ln_b[l][None], w_spatial[l], bt, off_u, ws, f"sgu_fwd_{l}",
                       after=going)
        wab_t, wsb_t, wo = ready((l, 1), sgu)
        br_a, br_b, merged = _branches_fwd(attn, sgu, wab_t, wsb_t, proj, off_g, f"branches_{l}")
        going = relay((l, 2), merged)
        x1 = _mm(merged, wo, "nn", F32, f"out_proj_{l}", residual=xl, after=going)
        going = advance((l, 2), x1, 2)
        h2 = _rmsnorm_fwd(x1, ffn_norm[l][None], f"ffn_norm_fwd_{l}", after=going)
        (wgu_t,) = ready((l, 2), h2)
        going = relay((l, 3), h2)
        gu, act = _gate_up_fwd(h2, wgu_t, f"gate_up_{l}", after=going)
        going = advance((l, 3), act, 2)
        if l + 1 < depth:
            going = relay((l + 1, 0), going)
        (wd,) = ready((l, 3), going)
        x2 = _mm(act, wd, "nn", F32, f"down_proj_{l}", residual=x1)
        if l + 1 < depth:
            going = advance((l + 1, 0), x2, 2)
        saved.append(dict(x0=xl, h=h, proj=proj, attn=attn, sgu=sgu, br_a=br_a, br_b=br_b, merged=merged, x1=x1,
                          h2=h2, gu=gu, act=act, gq=gq, gk=gk, bt=bt, win_t=win_t, wab_t=wab_t, wsb_t=wsb_t, wo=wo,
                          wgu_t=wgu_t, wd=wd))
        xl = x2

    loss_part, dx, dx16 = _loss_and_grad(xl, loss_target[0], "loss")
    loss = lax.psum(loss_part[0, 0], ("x", "y", "c"))

    def sibling_start(grads, tag):
        shaped = []
        for g, p in grads:
            rows, c = g.shape
            shaped.append(g.reshape(p, 4, 2, rows // (8 * p), c))
        send_sems, recv_sems, shaped, lands, tok = _sibling_start(shaped, f"rs_sibling_start_{tag}")
        return (send_sems, recv_sems, shaped, lands, tag), tok

    def chips_start(state, after, first=None):
        send_sems, recv_sems, shaped, lands, tag = state
        shaped, lands = _sibling_wait(send_sems, recv_sems, shaped, lands, after, f"rs_sibling_wait_{tag}")
        sums = [_sum_sibling(g, o, core, f"rs_add_sibling_{tag}_{a}") for a, (g, o) in enumerate(zip(shaped, lands))]
        gate = None if first is None else first(sums[0])
        send_sems, recv_sems, sums, lands, tok = _chips_start(sums, f"rs_chips_start_{tag}", after=gate)
        return (send_sems, recv_sems, sums, lands, tag), tok

    def scatter_finish(state, after):
        send_sems, recv_sems, sums, lands, tag = state
        sums, lands = _chips_wait(send_sems, recv_sems, sums, lands, after, f"rs_chips_wait_{tag}")
        return [[s] + o for s, o in zip(sums, lands)]

    in_flight = [dict() for _ in layers]
    small_grads = [None] * depth
    tok, swap_in = None, None
    for l in reversed(layers):
        s = saved[l]
        dgu = _gate_up_bwd(dx16, s["wd"], s["gu"], f"d_gate_up_{l}", after=tok)
        if swap_in is not None:
            in_flight[l + 1]["in"], tok = chips_start(swap_in, dgu)
        g_wd = _mm(s["act"], dx16, "tn", BF16, f"g_w_down_{l}", after=tok)
        swap, tok_s = sibling_start([(g_wd, 1)], f"{l}_down")
        dh2 = _mm(dgu, s["wgu_t"], "nn", F32, f"d_h2_{l}", after=tok_s)
        in_flight[l]["down"], tok = chips_start(swap, dh2)
        g_wgu_t = _mm(dgu, s["h2"], "tn", BF16, f"g_w_gate_up_{l}", after=tok)
        swap, tok_s = sibling_start([(g_wgu_t, 2)], f"{l}_gate_up")
        dx1, dx1_16, g_ffn = _rmsnorm_bwd(s["x1"], ffn_norm[l][None], dh2, dx, f"ffn_norm_bwd_{l}", after=tok_s)
        d_a, d_b, dla, dlb = _branches_bwd(dx1_16, s["wo"], s["br_a"], s["br_b"], s["proj"], off_g,
                                           f"d_branches_{l}")
        in_flight[l]["gate_up"], tok = chips_start(swap, d_a)
        g_wo = _mm(s["merged"], dx1_16, "tn", BF16, f"g_w_out_{l}", after=tok)
        dattn = _mm(d_a, s["wab_t"], "nn", F32, f"d_attn_{l}", after=g_wo)
        g_wab_t = _mm(d_a, s["attn"], "tn", BF16, f"g_w_attn_branch_{l}")
        dsgu = _mm(d_b, s["wsb_t"], "nn", F32, f"d_sgu_{l}")
        g_wsb_t = _mm(d_b, s["sgu"], "tn", BF16, f"g_w_sgu_branch_{l}")
        swap, tok_s = sibling_start([(g_wab_t, 1), (g_wsb_t, 1), (g_wo, 1)], f"{l}_mix")
        dq, dk, dv, g_gq, g_gk, g_sinks = _attn_bwd(s["proj"], dattn, tables, s["gq"], s["gk"], sinks[l], wq, wk,
                                                    f"attn_bwd_{l}", after=tok_s)
        du, dvv, g_lng, g_lnb, g_ws, g_bs = _sgu_bwd(s["proj"], dsgu, sgu_ln_g[l][None], sgu_ln_b[l][None],
                                                     w_spatial[l], s["bt"], off_u, ws, f"sgu_bwd_{l}")
        dproj = jnp.concatenate([dq, dk.astype(BF16), dv.astype(BF16), du, dvv, dla, dlb], axis=1)
        dh = _mm(dproj, s["win_t"], "nn", F32, f"d_h_{l}")
        in_flight[l]["mix"], tok = chips_start(swap, dh)
        g_win_t = _mm(dproj, s["h"], "tn", BF16, f"g_w_in_{l}", after=tok)
        swap_in, tok = sibling_start([(g_win_t, 1)], f"{l}_in")
        dx, dx16, g_mix = _rmsnorm_bwd(s["x0"], mix_norm[l][None], dh, dx1, f"mix_norm_bwd_{l}", after=tok)
        small_grads[l] = dict(
            mix_norm=g_mix[0], q_norm=g_gq[0].reshape(n_q_heads, HEAD_DIM).sum(0),
            k_norm=g_gk[0].reshape(n_q_heads // Q_PER_KV, HEAD_DIM).sum(0), sinks=g_sinks[0, :n_q_heads],
            sgu_ln_g=g_lng[0], sgu_ln_b=g_lnb[0], w_spatial=g_ws, b_spatial=g_bs[:, 0, :], ffn_norm=g_ffn[0])
    grad_x = dx[None]

    result = {key: {} for key in ("grad", "delta", "m", "v")}
    layer_like = {k: weights[k][0] for k in _SMALL}
    packed_g = jnp.concatenate([_pack(small_grads[l]) for l in layers], axis=0)
    rows_per_layer = packed_g.shape[0] // depth
    small_buf = _place_shard([packed_g[None]], 0, dev, F32, "place_small_grads", after=tok)
    send_sems, recv_sems, small_bufs, tok = _split_start([small_buf], GATHER_STAGES[0], "gather_send_small")
    small_state = [(send_sems, recv_sems, small_bufs)]

    def small_stage(stage, after):
        ssem, rsem, bufs = small_state[0]
        bufs = _split_wait(ssem, rsem, bufs, GATHER_STAGES[stage - 1], after, f"gather_wait{stage}_small")
        ssem, rsem, bufs, token = _split_start(bufs, GATHER_STAGES[stage], f"gather_pass{stage}_small")
        small_state[0] = (ssem, rsem, bufs)
        return token

    in_flight[0]["in"], tok = chips_start(swap_in, tok, first=functools.partial(small_stage, 1))

    def update(k, grads, transposed, after):
        view = (lambda a: jnp.swapaxes(a, 1, 2)) if transposed else (lambda a: a)
        outs = _adam(view(weights[k]), grads, view(mom1[k]), view(mom2[k]), chip, f"adam_{k}", after=after)
        for key, val in zip(("grad", "delta", "m", "v"), outs):
            result[key][k] = view(val)
        return outs[3]

    def plain(terms, tag):
        s, lands = terms[0], terms[1:]
        g = _sum_chips(s, lands, chip, f"rs_add_chips_{tag}")
        return [jnp.swapaxes(g, 1, 2)[:, None]]

    down = [scatter_finish(in_flight[l]["down"], tok) for l in reversed(layers)][::-1]
    tok = update("w_down", [(down[l][0], 0) for l in layers], False, None)
    tok = small_stage(2, tok)
    gate_up = [scatter_finish(in_flight[l]["gate_up"], tok) for l in reversed(layers)][::-1]
    tok = update("w_gate", [(gate_up[l][0], 0) for l in layers], True, None)
    tok = update("w_up", [(gate_up[l][0], 1) for l in layers], True, tok)
    mix =[scatter_finish(in_flight[l]["mix"], tok) for l in reversed(layers)][::-1]
    tok = update("w_out", [(mix[l][2], 0) for l in layers], False, None)
    tok = update("w_attn_branch", [(plain(mix[l][0], f"{l}_attn_branch"), 0) for l in layers], False, tok)
    tok = update("w_sgu_branch", [(plain(mix[l][1], f"{l}_sgu_branch"), 0) for l in layers], False, tok)

    packed = [jnp.concatenate([_pack({k: src[k][l] for k in _SMALL}) for l in layers], axis=0)
              for src in (weights, mom1, mom2)]
    (gathered_small,) = _split_wait(*small_state[0], GATHER_STAGES[2], tok, "gather_wait3_small")
    small = _small_reduce_adam(gathered_small[0], *packed, "small_reduce_adam")
    for key, rows in zip(("grad", "delta", "m", "v"), small):
        per_layer = [_unpack(rows[l * rows_per_layer:(l + 1) * rows_per_layer], layer_like) for l in layers]
        for k in _SMALL:
            result[key][k] = jnp.stack([per_layer[l][k] for l in layers])

    last = [scatter_finish(in_flight[l]["in"], small[0]) for l in reversed(layers)][::-1]
    update("w_in", [(last[l][0], 0) for l in layers], True, result["v"]["ffn_norm"])

    return (loss, grad_x, *[result["grad"][k] for k in names], *[result["delta"][k] for k in names],
            *[result["m"][k] for k in names], *[result["v"][k] for k in names])
```

```python
import functools
import math

import jax
import jax.numpy as jnp
from jax import lax
from jax.experimental import pallas as pl
from jax.experimental.pallas import tpu as pltpu

F32 = jnp.float32
BF16 = jnp.bfloat16
MESH = pl.DeviceIdType.MESH
ANY = pl.BlockSpec(memory_space=pl.ANY)

N_DEV = 8
HEAD_DIM = 64
Q_PER_KV = 4
BLOCK = 128
LANES = 128
ROPE_THETA = 10000.0
EPS = 1e-6
ADAM_LR = 0.001
ADAM_B1 = 0.9
ADAM_B2 = 0.999
ADAM_EPS = 1e-08
ADAM_WD = 0.01
ADAM_STEP = 10
NEG = -1e30
VMEM_LIMIT_BYTES = 56 * 1024 * 1024

NN = ((1,), (0,))
NT = ((1,), (1,))
TN = ((0,), (0,))


def _dot(a, b, dims):
    return lax.dot_general(a, b, (dims, ((), ())), preferred_element_type=F32)


def _params(*sem):
    return pltpu.CompilerParams(dimension_semantics=sem, vmem_limit_bytes=VMEM_LIMIT_BYTES)


def _divisor_tile(n, limit, unit):
    if n <= limit:
        return n
    best = unit
    for t in range(unit, limit + 1, unit):
        if n % t == 0:
            best = t
    assert n % best == 0, (n, limit, unit)
    return best


def _row_chunks(rows, size=256):
    size = min(size, rows)
    assert rows % size == 0, (rows, size)
    return [pl.ds(start, size) for start in range(0, rows, size)]


def _mm(a, b, mode, out_dtype, name, residual=None, after=None):
    parts = a.shape[0] if a.ndim == 3 else 1
    a2 = a.shape[-2:]
    if mode == "nn":
        (m, kp), (k2, n) = a2, b.shape
        k, mp = kp * parts, m
    elif mode == "nt":
        (m, kp), (n, k2) = a2, b.shape
        k, mp = kp * parts, m
    else:
        (k, mp), (k2, n) = a2, b.shape
        m, kp = mp * parts, k
    assert k == k2, (name, a.shape, b.shape)
    tk = _divisor_tile(kp, 2816, 128)
    nk = k // tk
    tm = _divisor_tile(mp, 512 if mode == "tn" else 1024, 128)
    tn = _divisor_tile(n, 2048 if mode == "tn" else 1024, 128)
    kpb, mpb = kp // tk, mp // tm
    dims = {"nn": NN, "nt": NT, "tn": TN}[mode]
    lead = (None,) if a.ndim == 3 else ()
    if mode == "tn":
        a_index = lambda i, j, kk: (i // mpb, kk, i % mpb) if lead else (kk, i)
        a_spec = pl.BlockSpec(lead + (tk, tm), a_index)
    else:
        a_index = lambda i, j, kk: (kk // kpb, i, kk % kpb) if lead else (i, kk)
        a_spec = pl.BlockSpec(lead + (tm, tk), a_index)
    if mode == "nt":
        b_spec = pl.BlockSpec((tn, tk), lambda i, j, kk: (j, kk))
    else:
        b_spec = pl.BlockSpec((tk, tn), lambda i, j, kk: (kk, j))
    o_spec = pl.BlockSpec((tm, tn), lambda i, j, kk: (i, j))
    has_res = residual is not None

    def body(*refs):
        a_ref, b_ref = refs[:2]
        r_ref = refs[2] if has_res else None
        o_ref, acc_ref = refs[-2:]
        kk = pl.program_id(2)
        p = _dot(a_ref[...], b_ref[...], dims)

        def finish(total):
            if has_res:
                total = total + r_ref[...]
            o_ref[...] = total.astype(o_ref.dtype)

        if nk == 1:
            finish(p)
        else:
            @pl.when(kk == 0)
            def _():
                acc_ref[...] = p

            @pl.when(jnp.logical_and(kk > 0, kk < nk - 1))
            def _():
                acc_ref[...] += p

            @pl.when(kk == nk - 1)
            def _():
                finish(acc_ref[...] + p)

    in_specs = [a_spec, b_spec] + ([o_spec] if has_res else []) + ([ANY] if after is not None else [])
    args = (a, b) + ((residual,) if has_res else ()) + ((after,) if after is not None else ())
    acc_shape = (tm, tn) if nk > 1 else (8, LANES)
    return pl.pallas_call(
        body,
        name=name,
        grid=(m // tm, n // tn, nk),
        in_specs=in_specs,
        out_specs=o_spec,
        out_shape=jax.ShapeDtypeStruct((m, n), out_dtype),
        scratch_shapes=[pltpu.VMEM(acc_shape, F32)],
        compiler_params=_params("parallel", "parallel", "arbitrary"),
    )(*args)


def _rmsnorm_fwd(x, g, name, after=None):
    t, d = x.shape
    tr = _divisor_tile(t, 512, 8)

    def body(x_ref, g_ref, *rest):
        h_ref = rest[-1]
        xv = x_ref[...]
        rstd = lax.rsqrt(jnp.mean(xv * xv, axis=-1, keepdims=True) + EPS)
        h_ref[...] = (xv * rstd * g_ref[...]).astype(h_ref.dtype)

    return pl.pallas_call(
        body,
        name=name,
        grid=(t // tr,),
        in_specs=[pl.BlockSpec((tr, d), lambda i: (i, 0)), pl.BlockSpec((1, d), lambda i: (0, 0))]
        + ([ANY] if after is not None else []),
        out_specs=pl.BlockSpec((tr, d), lambda i: (i, 0)),
        out_shape=jax.ShapeDtypeStruct((t, d), BF16),
        compiler_params=_params("parallel"),
    )(x, g, *(() if after is None else (after,)))


def _rmsnorm_bwd(x, g, dh, dres, name, after=None):
    t, d = x.shape
    tr = _divisor_tile(t, 256, 8)

    def body(x_ref, g_ref, dh_ref, dres_ref, *rest):
        dx_ref, dx16_ref, dg_ref = rest[-3:]
        i = pl.program_id(0)
        xv = x_ref[...]
        rstd = lax.rsqrt(jnp.mean(xv * xv, axis=-1, keepdims=True) + EPS)
        xh = xv * rstd
        dhv = dh_ref[...]
        dxh = dhv * g_ref[...]
        dx = dres_ref[...] + rstd * (dxh - xh * jnp.mean(dxh * xh, axis=-1, keepdims=True))
        dx_ref[...] = dx
        dx16_ref[...] = dx.astype(dx16_ref.dtype)
        part = jnp.broadcast_to(jnp.sum(dhv * xh, axis=0, keepdims=True), dg_ref.shape)

        @pl.when(i == 0)
        def _():
            dg_ref[...] = part

        @pl.when(i > 0)
        def _():
            dg_ref[...] += part

    row = pl.BlockSpec((tr, d), lambda i: (i, 0))
    return pl.pallas_call(
        body,
        name=name,
        grid=(t // tr,),
        in_specs=[row, pl.BlockSpec((1, d), lambda i: (0, 0)), row, row] + ([ANY] if after is not None else []),
        out_specs=[row, row, pl.BlockSpec((8, d), lambda i: (0, 0))],
        out_shape=[jax.ShapeDtypeStruct((t, d), F32), jax.ShapeDtypeStruct((t, d), BF16),
                   jax.ShapeDtypeStruct((8, d), F32)],
        compiler_params=_params("arbitrary"),
    )(x, g, dh, dres, *(() if after is None else (after,)))


def _lane(shape):
    return lax.broadcasted_iota(jnp.int32, shape, 1)


def _group_sum64(s):
    row = lax.broadcasted_iota(jnp.int32, (LANES, LANES), 0)
    col = lax.broadcasted_iota(jnp.int32, (LANES, LANES), 1)
    ones = jnp.where((row >= HEAD_DIM) == (col >= HEAD_DIM), 1.0, 0.0).astype(BF16)
    out = []
    for t in range(s.shape[1] // LANES):
        piece = s[:, LANES * t:LANES * t + LANES]
        hi = piece.astype(BF16)
        lo = (piece - hi.astype(F32)).astype(BF16)
        out.append(_dot(hi, ones, NN) + _dot(lo, ones, NN))
    return out[0] if len(out) == 1 else jnp.concatenate(out, axis=1)


def _swap32(x):
    w = x.shape[1]
    return jnp.where((_lane(x.shape) & 32) == 0, pltpu.roll(x, w - 32, axis=1), pltpu.roll(x, 32, axis=1))


def _rope(x, c, s):
    return x * c + _swap32(x) * s


def _rope_t(dy, c, s):
    return dy * c + _swap32(dy * s)


def _head_norm(x):
    rstd = lax.rsqrt(_group_sum64(x * x) * (1.0 / HEAD_DIM) + EPS)
    return x * rstd, rstd


def _head_norm_bwd(dxh, xh, rstd):
    return rstd * (dxh - xh * (_group_sum64(dxh * xh) * (1.0 / HEAD_DIM)))


def _roll64(x):
    return pltpu.roll(x, 64, axis=1)


def _attn_specs(wq, wk):
    kb = wq // wk
    prev = lambda i: jnp.maximum(i - 1, 0)
    return dict(
        q=pl.BlockSpec((BLOCK, wq), lambda i: (i, 0)),
        kc=pl.BlockSpec((BLOCK, wk), lambda i: (i, kb)),
        kp=pl.BlockSpec((BLOCK, wk), lambda i: (prev(i), kb)),
        vc=pl.BlockSpec((BLOCK, wk), lambda i: (i, kb + 1)),
        vp=pl.BlockSpec((BLOCK, wk), lambda i: (prev(i), kb + 1)),
        tq=pl.BlockSpec((BLOCK, wq), lambda i: (i, 0)),
        tkp=pl.BlockSpec((BLOCK, wk), lambda i: (prev(i), 0)),
        gq=pl.BlockSpec((1, wq), lambda i: (0, 0)),
        gk=pl.BlockSpec((1, wk), lambda i: (0, 0)),
        sinks=pl.BlockSpec(memory_space=pltpu.SMEM),
    )


def _attn_prologue(i, q_ref, kc_ref, kp_ref, cq_ref, sq_ref, ckp_ref, skp_ref, gq_ref, gk_ref):
    wk = kc_ref.shape[1]
    cq, sq = cq_ref[...], sq_ref[...]
    ck, sk = cq[:, :wk], sq[:, :wk]
    qh, q_rstd = _head_norm(q_ref[...])
    kch, kc_rstd = _head_norm(kc_ref[...])
    kph, kp_rstd = _head_norm(kp_ref[...])
    qn = _rope(qh * gq_ref[...], cq, sq)
    knc = _rope(kch * gk_ref[...], ck, sk)
    knp = _rope(kph * gk_ref[...], ckp_ref[...], skp_ref[...])
    stacked = (Q_PER_KV * BLOCK, BLOCK)
    row = lax.broadcasted_iota(jnp.int32, stacked, 0) & (BLOCK - 1)
    col = lax.broadcasted_iota(jnp.int32, stacked, 1)
    mask_c = col <= row
    mask_p = jnp.logical_and(col > row, i > 0)
    half = (lax.broadcasted_iota(jnp.int32, (BLOCK, BLOCK), 1) >= HEAD_DIM).astype(jnp.int32)
    return dict(cq=cq, sq=sq, ck=ck, sk=sk, qh=qh, q_rstd=q_rstd, kch=kch, kc_rstd=kc_rstd, kph=kph,
                kp_rstd=kp_rstd, qn=qn, knc=knc, knp=knp, mask_c=mask_c, mask_p=mask_p, half=half)


def _stack_heads(x, g, half):
    kpar = g % 2
    pieces = []
    for j in range(Q_PER_KV):
        t, e = divmod(Q_PER_KV * g + j, 2)
        piece = jnp.where(half == e, x[:, LANES * t:LANES * t + LANES], 0.0)
        pieces.append(piece if e == kpar else _roll64(piece))
    return jnp.concatenate(pieces, axis=0)


def _unstack_heads(y, g, half):
    kpar = g % 2
    slabs = {}
    for j in range(Q_PER_KV):
        t, e = divmod(Q_PER_KV * g + j, 2)
        piece = jnp.where(half == kpar, y[BLOCK * j:BLOCK * j + BLOCK], 0.0)
        piece = piece if e == kpar else _roll64(piece)
        slabs[t] = piece if t not in slabs else slabs[t] + piece
    return slabs


def _group_scores(st, g, sinks_ref, scale):
    ks = g // 2
    sl = slice(LANES * ks, LANES * ks + LANES)
    q4 = _stack_heads(st["qn"], g, st["half"]).astype(BF16)
    kc, kp = st["knc"][:, sl].astype(BF16), st["knp"][:, sl].astype(BF16)
    rows = Q_PER_KV * BLOCK
    at = lax.broadcasted_iota(jnp.int32, (rows, 1), 0)
    head = jnp.zeros((rows, 1), jnp.int32)
    sink = jnp.zeros((rows, 1), F32) + sinks_ref[Q_PER_KV * g]
    for j in range(1, Q_PER_KV):
        head = jnp.where(at >= BLOCK * j, j, head)
        sink = jnp.where(at >= BLOCK * j, sinks_ref[Q_PER_KV * g + j], sink)
    s_c = jnp.where(st["mask_c"], _dot(q4, kc, NT) * scale, NEG)
    s_p = jnp.where(st["mask_p"], _dot(q4, kp, NT) * scale, NEG)
    m = jnp.maximum(jnp.maximum(jnp.max(s_c, axis=1, keepdims=True), jnp.max(s_p, axis=1, keepdims=True)), sink)
    p_c, p_p = jnp.exp(s_c - m), jnp.exp(s_p - m)
    p_s = jnp.exp(sink - m)
    inv = 1.0 / (jnp.sum(p_c, axis=1, keepdims=True) + jnp.sum(p_p, axis=1, keepdims=True) + p_s)
    return dict(sl=sl, head=head, q4=q4, kc=kc, kp=kp, pr_c=p_c * inv, pr_p=p_p * inv, pr_s=p_s * inv)


def _attn_fwd(proj, tables, gq, gk, sinks, wq, wk, name, after=None):
    t = proj.shape[0]
    nb = t // BLOCK
    sp = _attn_specs(wq, wk)
    scale = HEAD_DIM ** -0.5
    cos_t, sin_t, cos_k, sin_k = tables

    def body(sinks_ref, q_ref, kc_ref, kp_ref, vc_ref, vp_ref, cq_ref, sq_ref, ckp_ref, skp_ref, gq_ref, gk_ref,
             *rest):
        o_ref = rest[-1]
        i = pl.program_id(0)
        st = _attn_prologue(i, q_ref, kc_ref, kp_ref, cq_ref, sq_ref, ckp_ref, skp_ref, gq_ref, gk_ref)
        for g in range(wq // (Q_PER_KV * HEAD_DIM)):
            gs = _group_scores(st, g, sinks_ref, scale)
            own = st["half"] == g % 2
            vc = jnp.where(own, vc_ref[:, gs["sl"]], 0.0).astype(BF16)
            vp = jnp.where(own, vp_ref[:, gs["sl"]], 0.0).astype(BF16)
            out = _dot(gs["pr_c"].astype(BF16), vc, NN) + _dot(gs["pr_p"].astype(BF16), vp, NN)
            for ts, slab in _unstack_heads(out, g, st["half"]).items():
                o_ref[:, LANES * ts:LANES * ts + LANES] = slab.astype(o_ref.dtype)

    return pl.pallas_call(
        body,
        name=name,
        grid=(nb,),
        in_specs=[sp["sinks"], sp["q"], sp["kc"], sp["kp"], sp["vc"], sp["vp"], sp["tq"], sp["tq"], sp["tkp"],
                  sp["tkp"], sp["gq"], sp["gk"]] + ([ANY] if after is not None else []),
        out_specs=pl.BlockSpec((BLOCK, wq), lambda i: (i, 0)),
        out_shape=jax.ShapeDtypeStruct((t, wq), BF16),
        compiler_params=_params("parallel"),
    )(sinks, proj, proj, proj, proj, proj, cos_t, sin_t, cos_k, sin_k, gq, gk, *(() if after is None else (after,)))


def _attn_bwd(proj, dout, tables, gq, gk, sinks, wq, wk, name, after=None):
    t = proj.shape[0]
    nb = t // BLOCK
    sp = _attn_specs(wq, wk)
    scale = HEAD_DIM ** -0.5
    cos_t, sin_t, cos_k, sin_k = tables

    def body(sinks_ref, q_ref, kc_ref, kp_ref, vc_ref, vp_ref, cq_ref, sq_ref, ckp_ref, skp_ref, gq_ref, gk_ref,
             do_ref, *rest):
        dq_ref, dk_ref, dv_ref, dgq_ref, dgk_ref, dsk_ref, dqn_ref, dknc_ref, dknp_ref, dvc_ref, dvp_ref = rest[-11:]
        i = pl.program_id(0)
        st = _attn_prologue(i, q_ref, kc_ref, kp_ref, cq_ref, sq_ref, ckp_ref, skp_ref, gq_ref, gk_ref)
        dknc_ref[...] = jnp.zeros_like(dknc_ref)
        dknp_ref[...] = jnp.zeros_like(dknp_ref)
        dvc_ref[...] = jnp.zeros_like(dvc_ref)
        dvp_ref[...] = jnp.zeros_like(dvp_ref)
        lane8 = _lane((8, LANES))
        dsinks = jnp.zeros((8, LANES), F32)
        for g in range(wq // (Q_PER_KV * HEAD_DIM)):
            gs = _group_scores(st, g, sinks_ref, scale)
            sl = gs["sl"]
            do4 = _stack_heads(do_ref[...], g, st["half"]).astype(BF16)
            dp_c = _dot(do4, vc_ref[:, sl].astype(BF16), NT)
            dp_p = _dot(do4, vp_ref[:, sl].astype(BF16), NT)
            pr_c, pr_p = gs["pr_c"], gs["pr_p"]
            rs = jnp.sum(pr_c * dp_c, axis=1, keepdims=True) + jnp.sum(pr_p * dp_p, axis=1, keepdims=True)
            ds_c = (pr_c * (dp_c - rs) * scale).astype(BF16)
            ds_p = (pr_p * (dp_p - rs) * scale).astype(BF16)
            dsink_rows = -gs["pr_s"] * rs
            for j in range(Q_PER_KV):
                dsink = jnp.sum(jnp.where(gs["head"] == j, dsink_rows, 0.0))
                dsinks = dsinks + jnp.where(lane8 == Q_PER_KV * g + j, dsink, 0.0)
            dq4 = _dot(ds_c, gs["kc"], NN) + _dot(ds_p, gs["kp"], NN)
            for ts, slab in _unstack_heads(dq4, g, st["half"]).items():
                dqn_ref[:, LANES * ts:LANES * ts + LANES] = slab
            dvc_ref[:, sl] += _dot(pr_c.astype(BF16), do4, TN)
            dvp_ref[:, sl] += _dot(pr_p.astype(BF16), do4, TN)
            dknc_ref[:, sl] += _dot(ds_c, gs["q4"], TN)
            dknp_ref[:, sl] += _dot(ds_p, gs["q4"], TN)

        gqv, gkv = gq_ref[...], gk_ref[...]
        dqg = _rope_t(dqn_ref[...], st["cq"], st["sq"])
        dq_ref[...] = _head_norm_bwd(dqg * gqv, st["qh"], st["q_rstd"]).astype(dq_ref.dtype)
        dkcg = _rope_t(dknc_ref[...], st["ck"], st["sk"])
        dkpg = _rope_t(dknp_ref[...], ckp_ref[...], skp_ref[...])
        dk_cur = _head_norm_bwd(dkcg * gkv, st["kch"], st["kc_rstd"])
        dk_prev = _head_norm_bwd(dkpg * gkv, st["kph"], st["kp_rstd"])
        dgq_part = jnp.broadcast_to(jnp.sum(dqg * st["qh"], axis=0, keepdims=True), dgq_ref.shape)
        dgk_part = jnp.broadcast_to(
            jnp.sum(dkcg * st["kch"] + dkpg * st["kph"], axis=0, keepdims=True), dgk_ref.shape)
        cur = pl.ds(pl.multiple_of(i * BLOCK, BLOCK), BLOCK)
        dk_ref[cur, :] = dk_cur
        dv_ref[cur, :] = dvc_ref[...]

        @pl.when(i == 0)
        def _():
            dgq_ref[...] = dgq_part
            dgk_ref[...] = dgk_part
            dsk_ref[...] = dsinks

        @pl.when(i > 0)
        def _():
            before = pl.ds(pl.multiple_of((i - 1) * BLOCK, BLOCK), BLOCK)
            dk_ref[before, :] += dk_prev
            dv_ref[before, :] += dvp_ref[...]
            dgq_ref[...] += dgq_part
            dgk_ref[...] += dgk_part
            dsk_ref[...] += dsinks

    whole = lambda shape: pl.BlockSpec(shape, lambda i: (0, 0))
    return pl.pallas_call(
        body,
        name=name,
        grid=(nb,),
        in_specs=[sp["sinks"], sp["q"], sp["kc"], sp["kp"], sp["vc"], sp["vp"], sp["tq"], sp["tq"], sp["tkp"],
                  sp["tkp"], sp["gq"], sp["gk"], pl.BlockSpec((BLOCK, wq), lambda i: (i, 0))]
        + ([ANY] if after is not None else []),
        out_specs=[pl.BlockSpec((BLOCK, wq), lambda i: (i, 0)), whole((t, wk)), whole((t, wk)), whole((8, wq)),
                   whole((8, wk)), whole((8, LANES))],
        out_shape=[jax.ShapeDtypeStruct((t, wq), BF16), jax.ShapeDtypeStruct((t, wk), F32),
                   jax.ShapeDtypeStruct((t, wk), F32), jax.ShapeDtypeStruct((8, wq), F32),
                   jax.ShapeDtypeStruct((8, wk), F32), jax.ShapeDtypeStruct((8, LANES), F32)],
        scratch_shapes=[pltpu.VMEM((BLOCK, wq), F32), pltpu.VMEM((BLOCK, wk), F32), pltpu.VMEM((BLOCK, wk), F32),
                        pltpu.VMEM((BLOCK, wk), F32), pltpu.VMEM((BLOCK, wk), F32)],
        compiler_params=_params("arbitrary"),
    )(sinks, proj, proj, proj, proj, proj, cos_t, sin_t, cos_k, sin_k, gq, gk, dout,
      *(() if after is None else (after,)))


_GELU_K = math.sqrt(2.0 / math.pi)
_GELU_A = 0.044715


def _gelu(x):
    return 0.5 * x * (1.0 + jnp.tanh(_GELU_K * (x + _GELU_A * x * x * x)))


def _gelu_and_grad(x):
    th = jnp.tanh(_GELU_K * (x + _GELU_A * x * x * x))
    return (0.5 * x * (1.0 + th),
            0.5 * (1.0 + th) + 0.5 * x * (1.0 - th * th) * (_GELU_K * (1.0 + 3.0 * _GELU_A * x * x)))


def _group_ln(v):
    mu = jnp.mean(v, axis=1, keepdims=True)
    cen = v - mu
    rstd = lax.rsqrt(jnp.mean(cen * cen, axis=1, keepdims=True) + EPS)
    return cen * rstd, rstd


def _sgu_geometry(off_u, ws):
    cw = math.gcd(off_u, ws)
    return cw, ws // cw, off_u // cw, (off_u + ws) // cw


def _sgu_fwd(proj, ln_g, ln_b, w_s, bt, off_u, ws, name, after=None):
    t = proj.shape[0]
    nb = t // BLOCK
    cw, nc, ub, vb = _sgu_geometry(off_u, ws)
    gpc = cw // LANES
    ng = ws // LANES

    def body(u_ref, v_ref, g_ref, b_ref, w_ref, bt_ref, *rest):
        o_ref = rest[-1]
        jc = pl.program_id(0)
        row = lax.broadcasted_iota(jnp.int32, (BLOCK, BLOCK), 0)
        col = lax.broadcasted_iota(jnp.int32, (BLOCK, BLOCK), 1)
        lane_g = _lane((BLOCK, ng))
        for gi in range(gpc):
            sl = slice(LANES * gi, LANES * gi + LANES)
            xh, _ = _group_ln(_gelu(v_ref[:, sl]))
            vn = xh * g_ref[:, sl] + b_ref[:, sl]
            w = jnp.where(row >= col, w_ref[gi], 0.0).astype(BF16)
            bias = jnp.sum(jnp.where(lane_g == jc * gpc + gi, bt_ref[...], 0.0), axis=1, keepdims=True)
            s = _dot(w, vn.astype(BF16), NN) + bias
            o_ref[:, sl] = (_gelu(u_ref[:, sl]) * s).astype(o_ref.dtype)

    return pl.pallas_call(
        body,
        name=name,
        grid=(nc, nb),
        in_specs=[pl.BlockSpec((BLOCK, cw), lambda jc, i: (i, ub + jc)),
                  pl.BlockSpec((BLOCK, cw), lambda jc, i: (i, vb + jc)),
                  pl.BlockSpec((1, cw), lambda jc, i: (0, jc)),
                  pl.BlockSpec((1, cw), lambda jc, i: (0, jc)),
                  pl.BlockSpec((gpc, BLOCK, BLOCK), lambda jc, i: (jc, 0, 0)),
                  pl.BlockSpec((BLOCK, ng), lambda jc, i: (0, 0))] + ([ANY] if after is not None else []),
        out_specs=pl.BlockSpec((BLOCK, cw), lambda jc, i: (i, jc)),
        out_shape=jax.ShapeDtypeStruct((t, ws), BF16),
        compiler_params=_params("parallel", "parallel"),
    )(proj, proj, ln_g, ln_b, w_s, bt, *(() if after is None else (after,)))


def _sgu_bwd(proj, dout, ln_g, ln_b, w_s, bt, off_u, ws, name):
    t = proj.shape[0]
    nb = t // BLOCK
    cw, nc, ub, vb = _sgu_geometry(off_u, ws)
    gpc = cw // LANES
    ng = ws // LANES

    def body(u_ref, v_ref, g_ref, b_ref, w_ref, bt_ref, do_ref, du_ref, dv_ref, dg_ref, db_ref, dw_ref, dbs_ref,
             bacc_ref):
        jc = pl.program_id(0)
        i = pl.program_id(1)
        row = lax.broadcasted_iota(jnp.int32, (BLOCK, BLOCK), 0)
        col = lax.broadcasted_iota(jnp.int32, (BLOCK, BLOCK), 1)
        lane_g = _lane((BLOCK, ng))
        tri = row >= col

        @pl.when(i == 0)
        def _():
            dg_ref[...] = jnp.zeros_like(dg_ref)
            db_ref[...] = jnp.zeros_like(db_ref)
            dw_ref[...] = jnp.zeros_like(dw_ref)
            bacc_ref[...] = jnp.zeros_like(bacc_ref)

        for gi in range(gpc):
            sl = slice(LANES * gi, LANES * gi + LANES)
            u_raw, v_raw = u_ref[:, sl], v_ref[:, sl]
            u_act, u_slope = _gelu_and_grad(u_raw)
            v_act, v_slope = _gelu_and_grad(v_raw)
            xh, rstd = _group_ln(v_act)
            gam = g_ref[:, sl]
            vn = (xh * gam + b_ref[:, sl]).astype(BF16)
            w = jnp.where(tri, w_ref[gi], 0.0)
            bias = jnp.sum(jnp.where(lane_g == jc * gpc + gi, bt_ref[...], 0.0), axis=1, keepdims=True)
            s = _dot(w.astype(BF16), vn, NN) + bias
            dov = do_ref[:, sl]
            du_ref[:, sl] = (dov * s * u_slope).astype(du_ref.dtype)
            ds = dov * u_act
            ds16 = ds.astype(BF16)
            dw_ref[gi] += jnp.where(tri, _dot(ds16, vn, NT), 0.0)
            bacc_ref[gi] += ds
            dvn = _dot(w.T.astype(BF16), ds16, NN)
            dg_ref[:, sl] += jnp.broadcast_to(jnp.sum(dvn * xh, axis=0, keepdims=True), (8, LANES))
            db_ref[:, sl] += jnp.broadcast_to(jnp.sum(dvn, axis=0, keepdims=True), (8, LANES))
            dxh = dvn * gam
            dvg = rstd * (dxh - jnp.mean(dxh, axis=1, keepdims=True)
                          - xh * jnp.mean(dxh * xh, axis=1, keepdims=True))
            dv_ref[:, sl] = (dvg * v_slope).astype(dv_ref.dtype)

        @pl.when(i == nb - 1)
        def _():
            for gi in range(gpc):
                dbs_ref[gi] = jnp.broadcast_to(jnp.sum(bacc_ref[gi].T, axis=0, keepdims=True), (8, LANES))

    blk = lambda base: pl.BlockSpec((BLOCK, cw), lambda jc, i: (i, base + jc))
    vec = pl.BlockSpec((1, cw), lambda jc, i: (0, jc))
    acc = pl.BlockSpec((8, cw), lambda jc, i: (0, jc))
    wsp = pl.BlockSpec((gpc, BLOCK, BLOCK), lambda jc, i: (jc, 0, 0))
    return pl.pallas_call(
        body,
        name=name,
        grid=(nc, nb),
        in_specs=[blk(ub), blk(vb), vec, vec, wsp, pl.BlockSpec((BLOCK, ng), lambda jc, i: (0, 0)), blk(0)],
        out_specs=[blk(0), blk(0), acc, acc, wsp, pl.BlockSpec((gpc, 8, LANES), lambda jc, i: (jc, 0, 0))],
        out_shape=[jax.ShapeDtypeStruct((t, ws), BF16), jax.ShapeDtypeStruct((t, ws), BF16),
                   jax.ShapeDtypeStruct((8, ws), F32), jax.ShapeDtypeStruct((8, ws), F32),
                   jax.ShapeDtypeStruct((ng, BLOCK, BLOCK), F32), jax.ShapeDtypeStruct((ng, 8, LANES), F32)],
        scratch_shapes=[pltpu.VMEM((gpc, BLOCK, BLOCK), F32)],
        compiler_params=_params("arbitrary", "arbitrary"),
    )(proj, proj, ln_g, ln_b, w_s, bt, dout)


def _sigmoid(x):
    return 1.0 / (1.0 + jnp.exp(-x))


def _merge_geometry(off_g, d):
    cw = math.gcd(off_g, d)
    return cw, d // cw, off_g // cw, (off_g + d) // cw


def _branches_fwd(attn, sgu, wab_t, wsb_t, proj, off_g, name):
    t = attn.shape[0]
    d = wab_t.shape[0]
    tn, _, ab, bb = _merge_geometry(off_g, d)
    tm = _divisor_tile(t, 1024, 128)

    def body(a1_ref, a2_ref, b1_ref, b2_ref, la_ref, lb_ref, bra_ref, brb_ref, o_ref):
        for rows in _row_chunks(tm):
            va = _dot(a1_ref[rows, :], b1_ref[...], NT)
            vb = _dot(a2_ref[rows, :], b2_ref[...], NT)
            bra_ref[rows, :] = va
            brb_ref[rows, :] = vb
            o_ref[rows, :] = (_sigmoid(la_ref[rows, :]) * va + _sigmoid(lb_ref[rows, :]) * vb).astype(o_ref.dtype)

    rows = lambda w: pl.BlockSpec((tm, w), lambda i, j: (i, 0))
    wrow = lambda w: pl.BlockSpec((tn, w), lambda i, j: (j, 0))
    blk = lambda base: pl.BlockSpec((tm, tn), lambda i, j: (i, base + j))
    return pl.pallas_call(
        body,
        name=name,
        grid=(t // tm, d // tn),
        in_specs=[rows(attn.shape[1]), rows(sgu.shape[1]), wrow(wab_t.shape[1]), wrow(wsb_t.shape[1]), blk(ab),
                  blk(bb)],
        out_specs=[blk(0)] * 3,
        out_shape=[jax.ShapeDtypeStruct((t, d), F32), jax.ShapeDtypeStruct((t, d), F32),
                   jax.ShapeDtypeStruct((t, d), BF16)],
        compiler_params=_params("parallel", "parallel"),
    )(attn, sgu, wab_t, wsb_t, proj, proj)


def _branches_bwd(dx16, wo, br_a, br_b, proj, off_g, name, after=None):
    t, d = br_a.shape
    tn, _, ab, bb = _merge_geometry(off_g, d)
    tm = _divisor_tile(t, 1024, 128)
    k = dx16.shape[1]

    def body(a_ref, b_ref, bra_ref, brb_ref, la_ref, lb_ref, *rest):
        da_ref, db_ref, dla_ref, dlb_ref = rest[-4:]
        for rows in _row_chunks(tm):
            dmv = _dot(a_ref[rows, :], b_ref[...], NT)
            ga, gb = _sigmoid(la_ref[rows, :]), _sigmoid(lb_ref[rows, :])
            da_ref[rows, :] = (dmv * ga).astype(da_ref.dtype)
            db_ref[rows, :] = (dmv * gb).astype(db_ref.dtype)
            dla_ref[rows, :] = (dmv * bra_ref[rows, :] * ga * (1.0 - ga)).astype(dla_ref.dtype)
            dlb_ref[rows, :] = (dmv * brb_ref[rows, :] * gb * (1.0 - gb)).astype(dlb_ref.dtype)

    blk = lambda base: pl.BlockSpec((tm, tn), lambda i, j: (i, base + j))
    return pl.pallas_call(
        body,
        name=name,
        grid=(t // tm, d // tn),
        in_specs=[pl.BlockSpec((tm, k), lambda i, j: (i, 0)), pl.BlockSpec((tn, k), lambda i, j: (j, 0)), blk(0),
                  blk(0), blk(ab), blk(bb)] + ([ANY] if after is not None else []),
        out_specs=[blk(0)] * 4,
        out_shape=[jax.ShapeDtypeStruct((t, d), BF16)] * 4,
        compiler_params=_params("parallel", "parallel"),
    )(dx16, wo, br_a, br_b, proj, proj, *(() if after is None else (after,)))


def _gate_up_fwd(h2, wgu_t, name, after=None):
    t, d = h2.shape
    f = wgu_t.shape[0] // 2
    tm = _divisor_tile(t, 1024, 128)
    tn = _divisor_tile(f, 512, 128)
    nb = f // tn

    def body(a_ref, bg_ref, bu_ref, *rest):
        gu_ref, act_ref = rest[-2:]
        for rows in _row_chunks(tm):
            av = a_ref[rows, :]
            gv = _dot(av, bg_ref[...], NT)
            uv = _dot(av, bu_ref[...], NT)
            gu_ref[0, rows, :] = gv
            gu_ref[1, rows, :] = uv
            act_ref[rows, :] = (gv * _sigmoid(gv) * uv).astype(act_ref.dtype)

    return pl.pallas_call(
        body,
        name=name,
        grid=(t // tm, nb),
        in_specs=[pl.BlockSpec((tm, d), lambda i, j: (i, 0)), pl.BlockSpec((tn, d), lambda i, j: (j, 0)),
                  pl.BlockSpec((tn, d), lambda i, j: (j + nb, 0))] + ([ANY] if after is not None else []),
        out_specs=[pl.BlockSpec((2, tm, tn), lambda i, j: (0, i, j)), pl.BlockSpec((tm, tn), lambda i, j: (i, j))],
        out_shape=[jax.ShapeDtypeStruct((2, t, f), F32), jax.ShapeDtypeStruct((t, f), BF16)],
        compiler_params=_params("parallel", "parallel"),
    )(h2, wgu_t, wgu_t, *(() if after is None else (after,)))


def _gate_up_bwd(dx16, wd, gu, name, after=None):
    t, d = dx16.shape
    f = wd.shape[0]
    tm = _divisor_tile(t, 1024, 128)
    tn = _divisor_tile(f, 512, 128)

    def body(a_ref, b_ref, gu_ref, *rest):
        o_ref = rest[-1]
        for rows in _row_chunks(tm):
            dav = _dot(a_ref[rows, :], b_ref[...], NT)
            gv = gu_ref[0, rows, :]
            sg = _sigmoid(gv)
            o_ref[0, rows, :] = (dav * gu_ref[1, rows, :] * (sg + gv * sg * (1.0 - sg))).astype(o_ref.dtype)
            o_ref[1, rows, :] = (dav * gv * sg).astype(o_ref.dtype)

    pair = pl.BlockSpec((2, tm, tn), lambda i, j: (0, i, j))
    return pl.pallas_call(
        body,
        name=name,
        grid=(t // tm, f // tn),
        in_specs=[pl.BlockSpec((tm, d), lambda i, j: (i, 0)), pl.BlockSpec((tn, d), lambda i, j: (j, 0)), pair]
        + ([ANY] if after is not None else []),
        out_specs=pair,
        out_shape=jax.ShapeDtypeStruct((2, t, f), BF16),
        compiler_params=_params("parallel", "parallel"),
    )(dx16, wd, gu, *(() if after is None else (after,)))


def _loss_and_grad(y, target, name):
    t, d = y.shape
    tr = _divisor_tile(t, 512, 8)

    def body(y_ref, t_ref, l_ref, dy_ref, dy16_ref):
        i = pl.program_id(0)
        err = y_ref[...] - t_ref[...]
        dy_ref[...] = err * (1.0 / d)
        dy16_ref[...] = (err * (1.0 / d)).astype(dy16_ref.dtype)
        part = jnp.broadcast_to(0.5 * jnp.sum(err * err) * (1.0 / d), l_ref.shape)

        @pl.when(i == 0)
        def _():
            l_ref[...] = part

        @pl.when(i > 0)
        def _():
            l_ref[...] += part

    row = pl.BlockSpec((tr, d), lambda i: (i, 0))
    return pl.pallas_call(
        body,
        name=name,
        grid=(t // tr,),
        in_specs=[row, row],
        out_specs=[pl.BlockSpec((8, LANES), lambda i: (0, 0)), row, row],
        out_shape=[jax.ShapeDtypeStruct((8, LANES), F32), jax.ShapeDtypeStruct((t, d), F32),
                   jax.ShapeDtypeStruct((t, d), BF16)],
        compiler_params=_params("arbitrary"),
    )(y, target)


def _adam_math(w, g, m, v):
    m = ADAM_B1 * m + (1.0 - ADAM_B1) * g
    v = ADAM_B2 * v + (1.0 - ADAM_B2) * (g * g)
    m_hat = m / (1.0 - ADAM_B1 ** ADAM_STEP)
    v_hat = v / (1.0 - ADAM_B2 ** ADAM_STEP)
    delta = -ADAM_LR * (m_hat / (jnp.sqrt(v_hat) + ADAM_EPS) + ADAM_WD * w)
    return delta, m, v


def _row_tile(r, c, elems=512 * 1024):
    return _divisor_tile(r, max(8, elems // c // 8 * 8), 8)


def _adam(w, grads, m, v, chip, name, after=None):
    nl, r, c = w.shape
    tr = _row_tile(r, c, 256 * 1024)
    nb = r // tr
    counts = [len(terms) for terms, _ in grads]

    def body(chip_ref, *refs):
        w_ref, m_ref, v_ref = refs[:3]
        g_ref, d_ref, nm_ref, nv_ref = refs[-4:]
        layer = pl.program_id(0)
        g, at = None, 3
        for li, n in enumerate(counts):
            total = refs[at][...].astype(F32)
            for ref in refs[at + 1:at + n]:
                total = total + ref[...].astype(F32)
            g = total if g is None else jnp.where(layer == li, total, g)
            at += n
        g_ref[...] = g
        d_ref[...], nm_ref[...], nv_ref[...] = _adam_math(w_ref[...], g, m_ref[...], v_ref[...])

    def term_spec(li, p, by_owner):
        def index(l, i, chip_ref):
            rows = jnp.where(l < li, 0, jnp.where(l > li, nb - 1, i))
            return (p, chip_ref[0] if by_owner else 0, rows, 0)
        return pl.BlockSpec((None, None, tr, c), index)

    row = pl.BlockSpec((None, tr, c), lambda l, i, chip_ref: (l, i, 0))
    specs, arrays = [], []
    for li, (terms, p) in enumerate(grads):
        for term in terms:
            specs.append(term_spec(li, p, term.shape[1] == 4))
            arrays.append(term)
    return pl.pallas_call(
        body,
        name=name,
        grid_spec=pltpu.PrefetchScalarGridSpec(
            num_scalar_prefetch=1, grid=(nl, nb),
            in_specs=[row] * 3 + specs + ([ANY] if after is not None else []), out_specs=[row] * 4),
        out_shape=[jax.ShapeDtypeStruct((nl, r, c), F32)] * 4,
        compiler_params=_params("arbitrary", "arbitrary"),
    )(chip, w, m, v, *arrays, *(() if after is None else (after,)))


def _place_shard(parts, layer, dev, out_dtype, name, after=None):
    p = len(parts)
    _, r, c = parts[0].shape
    tr = _row_tile(r, c)

    def body(dev_ref, *refs):
        o_ref = refs[-1]
        x = refs[0][...]
        for pi in range(1, p):
            x = jnp.where(pl.program_id(0) == pi, refs[pi][...], x)
        o_ref[...] = x.astype(o_ref.dtype)

    return pl.pallas_call(
        body,
        name=name,
        grid_spec=pltpu.PrefetchScalarGridSpec(
            num_scalar_prefetch=1,
            grid=(p, r // tr),
            in_specs=[pl.BlockSpec((None, tr, c), lambda pi, i, dev_ref: (layer, i, 0))] * p
            + ([ANY] if after is not None else []),
            out_specs=pl.BlockSpec((None, None, tr, c), lambda pi, i, dev_ref: (pi, dev_ref[0], i, 0)),
        ),
        out_shape=jax.ShapeDtypeStruct((p, N_DEV, r, c), out_dtype),
        compiler_params=_params("parallel", "parallel"),
    )(dev, *parts, *(() if after is None else (after,)))


def _sum_sibling(g, land, core, name):
    p, _, _, r, c = g.shape
    tr = _row_tile(r, c, 1024 * 1024)

    def body(core_ref, g_ref, l_ref, o_ref):
        o_ref[...] = (g_ref[...].astype(F32) + l_ref[...].astype(F32)).astype(o_ref.dtype)

    return pl.pallas_call(
        body,
        name=name,
        grid_spec=pltpu.PrefetchScalarGridSpec(
            num_scalar_prefetch=1,
            grid=(p, 4, r // tr),
            in_specs=[pl.BlockSpec((None, None, None, tr, c), lambda pi, q, i, core_ref: (pi, q, core_ref[0], i, 0)),
                      pl.BlockSpec((None, None, None, tr, c), lambda pi, q, i, core_ref: (pi, q, 0, i, 0))],
            out_specs=pl.BlockSpec((None, None, tr, c), lambda pi, q, i, core_ref: (pi, q, i, 0)),
        ),
        out_shape=jax.ShapeDtypeStruct((p, 4, r, c), BF16),
        compiler_params=_params("parallel", "parallel", "parallel"),
    )(core, g, land)


def _sum_chips(s, lands, chip, name):
    p, _, r, c = s.shape
    tr = _row_tile(r, c)

    def body(chip_ref, s_ref, l0_ref, l1_ref, l2_ref, o_ref):
        total = s_ref[...].astype(F32) + l0_ref[...].astype(F32)
        o_ref[...] = total + l1_ref[...].astype(F32) + l2_ref[...].astype(F32)

    land_spec = pl.BlockSpec((None, None, tr, c), lambda pi, i, chip_ref: (pi, 0, i, 0))
    return pl.pallas_call(
        body,
        name=name,
        grid_spec=pltpu.PrefetchScalarGridSpec(
            num_scalar_prefetch=1,
            grid=(p, r // tr),
            in_specs=[pl.BlockSpec((None, None, tr, c), lambda pi, i, chip_ref: (pi, chip_ref[0], i, 0)),
                      land_spec, land_spec, land_spec],
            out_specs=pl.BlockSpec((None, tr, c), lambda pi, i, chip_ref: (pi, i, 0)),
        ),
        out_shape=jax.ShapeDtypeStruct((p, r, c), F32),
        compiler_params=_params("parallel", "parallel"),
    )(chip, s, *lands)


def _small_reduce_adam(gathered, w, m, v, name):
    _, r, c = gathered.shape
    tr = _row_tile(r, c)

    def body(p_ref, w_ref, m_ref, v_ref, g_ref, d_ref, nm_ref, nv_ref):
        g = p_ref[0]
        for j in range(1, N_DEV):
            g = g + p_ref[j]
        g_ref[...] = g
        d_ref[...], nm_ref[...], nv_ref[...] = _adam_math(w_ref[...], g, m_ref[...], v_ref[...])

    row = pl.BlockSpec((tr, c), lambda i: (i, 0))
    return pl.pallas_call(
        body,
        name=name,
        grid=(r // tr,),
        in_specs=[pl.BlockSpec((N_DEV, tr, c), lambda i: (0, i, 0)), row, row, row],
        out_specs=[row] * 4,
        out_shape=[jax.ShapeDtypeStruct((r, c), F32)] * 4,
        compiler_params=_params("parallel"),
    )(gathered, w, m, v)


def _place():
    return lax.axis_index("x"), lax.axis_index("y"), lax.axis_index("c")


HBM =pl.BlockSpec(memory_space=pltpu.HBM)
SEM = pl.BlockSpec(memory_space=pltpu.SEMAPHORE)
TOKEN = pl.BlockSpec(memory_space=pltpu.VMEM)
EFFECT = pltpu.SideEffectType.DATAFLOW_SIDE_EFFECTING


def _in_hbm(a):
    return pltpu.with_memory_space_constraint(a, pltpu.HBM)


_FLIPS = {"me": (0, 0, 0), "s": (0, 0, 1), "x": (1, 0, 0), "y": (0, 1, 0), "d": (1, 1, 0)}
GATHER_STAGES = (
    (("s", "me", "all"), ("x", "me", "all"), ("y", "me", "all")),
    (("s", "x", "all"), ("s", "y", "all"), ("y", "x", "first"), ("x", "y", "second")),
    (("s", "d", "all"),),
)


def _flipped(place, *names):
    out = list(place)
    for name in names:
        out = [1 - p if f else p for p, f in zip(out, _FLIPS[name])]
    return tuple(out)


def _block_part(ref, place, part):
    px, py, pc = place
    rows = ref.shape[2]
    span = {"all": pl.ds(0, rows), "first": pl.ds(0, rows // 2), "second": pl.ds(rows // 2, rows // 2)}[part]
    return ref.at[:, pl.ds(4 * px + 2 * py + pc, 1), span]


def _split_start_many(groups, name, after=None):
    flat = [b for bufs, _ in groups for b in bufs]
    n, ng = len(flat), len(groups)
    extra = 0 if after is None else 1

    def body(*refs):
        sems = refs[n + extra:n + extra + 2 * ng]
        outs, token = refs[n + extra + 2 * ng:2 * n + extra + 2 * ng], refs[2 * n + extra + 2 * ng]
        me = _place()
        at = 0
        for gi, (bufs, moves) in enumerate(groups):
            nm = len(moves)
            for a in range(len(bufs)):
                for k, (to, owner, part) in enumerate(moves):
                    piece = _block_part(outs[at + a], _flipped(me, owner), part)
                    pltpu.make_async_remote_copy(
                        src_ref=piece, dst_ref=piece, send_sem=sems[2 * gi].at[nm * a + k],
                        recv_sem=sems[2 * gi + 1].at[nm * a + k], device_id=_flipped(me, to),
                        device_id_type=MESH).start()
            at += len(bufs)
        token[...] = jnp.zeros_like(token)

    sem_shapes = []
    for bufs, moves in groups:
        sem_shapes += [pltpu.SemaphoreType.DMA((len(moves) * len(bufs),))] * 2
    outs = pl.pallas_call(
        body,
        name=name,
        in_specs=[HBM] * n + [ANY] * extra,
        out_specs=[SEM] * (2 * ng) + [HBM] * n + [TOKEN],
        out_shape=sem_shapes + [pltpu.HBM(b.shape, b.dtype) for b in flat] + [jax.ShapeDtypeStruct((8, LANES), F32)],
        input_output_aliases={i: 2 * ng + i for i in range(n)},
        compiler_params=pltpu.CompilerParams(has_side_effects=EFFECT),
    )(*[_in_hbm(b) for b in flat], *(() if after is None else (after,)))
    result, at = [], 2 * ng
    for gi, (bufs, _) in enumerate(groups):
        result.append((outs[2 * gi], outs[2 * gi + 1], list(outs[at:at + len(bufs)])))
        at += len(bufs)
    return result, outs[-1]


def _split_start(bufs, moves, name, after=None):
    ((send_sems, recv_sems, bufs),), token = _split_start_many([(bufs, moves)], name, after)
    return send_sems, recv_sems, bufs, token


def _split_wait(send_sems, recv_sems, bufs, moves, after, name):
    n, nm = len(bufs), len(moves)

    def body(*refs):
        ins, ssem, rsem = refs[:n], refs[n], refs[n + 1]
        me = _place()
        for a in range(n):
            for k, (to, owner, part) in enumerate(moves):
                landed = _block_part(ins[a], _flipped(me, owner, to), part)
                cp = pltpu.make_async_remote_copy(
                    src_ref=landed, dst_ref=landed, send_sem=ssem.at[nm * a + k], recv_sem=rsem.at[nm * a + k],
                    device_id=_flipped(me, to), device_id_type=MESH)
                cp.wait_send()
                cp.wait_recv()

    return pl.pallas_call(
        body,
        name=name,
        in_specs=[HBM] * n + [SEM, SEM, ANY],
        out_specs=[HBM] * n,
        out_shape=[pltpu.HBM(b.shape, b.dtype) for b in bufs],
        input_output_aliases={i: i for i in range(n)},
        compiler_params=pltpu.CompilerParams(has_side_effects=EFFECT),
    )(*bufs, send_sems, recv_sems, after)


def _chips_start(sums, name, after=None):
    n = len(sums)
    extra = 0 if after is None else 1

    def body(*refs):
        refs = refs[:4 * n] + refs[4 * n + extra:]
        ssem, rsem = refs[4 * n], refs[4 * n + 1]
        src, land = refs[4 * n + 2:5 * n + 2], refs[5 * n + 2:8 * n + 2]
        token = refs[8 * n + 2]
        x, y, c = _place()
        chips = [(1 - x, y), (x, 1 - y), (1 - x, 1 - y)]
        for a in range(n):
            for k, (px, py) in enumerate(chips):
                pltpu.make_async_remote_copy(
                    src_ref=src[a].at[:, pl.ds(2 * px + py, 1)], dst_ref=land[3 * a + k], send_sem=ssem.at[3 * a + k],
                    recv_sem=rsem.at[3 * a + k], device_id=(px, py, c), device_id_type=MESH).start()
        token[...] = jnp.zeros_like(token)

    lands = []
    for s in sums:
        lands += [lax.empty((s.shape[0], 1) + s.shape[2:], s.dtype) for _ in range(3)]
    outs = pl.pallas_call(
        body,
        name=name,
        in_specs=[HBM] * (4 * n) + [ANY] * extra,
        out_specs=[SEM, SEM] + [HBM] * (4 * n) + [TOKEN],
        out_shape=[pltpu.SemaphoreType.DMA((3 * n,))] * 2 + [pltpu.HBM(b.shape, b.dtype) for b in list(sums) + lands]
        + [jax.ShapeDtypeStruct((8, LANES), F32)],
        input_output_aliases={i: 2 + i for i in range(4 * n)},
        compiler_params=pltpu.CompilerParams(has_side_effects=EFFECT),
    )(*[_in_hbm(b) for b in list(sums) + lands], *(() if after is None else (after,)))
    return outs[0], outs[1], list(outs[2:2 + n]), list(outs[2 + n:2 + 4 * n]), outs[-1]


def _chips_wait(send_sems, recv_sems, sums, lands, after, name):
    n = len(sums)

    def body(*refs):
        src, land = refs[:n], refs[n:4 * n]
        ssem, rsem = refs[4 * n], refs[4 * n + 1]
        x, y, c = _place()
        chips = [(1 - x, y), (x, 1 - y), (1 - x, 1 - y)]
        for a in range(n):
            for k, (px, py) in enumerate(chips):
                cp = pltpu.make_async_remote_copy(
                    src_ref=src[a].at[:, pl.ds(2 * px + py, 1)], dst_ref=land[3 * a + k], send_sem=ssem.at[3 * a + k],
                    recv_sem=rsem.at[3 * a + k], device_id=(px, py, c), device_id_type=MESH)
                cp.wait_send()
                cp.wait_recv()

    both = list(sums) + list(lands)
    outs = pl.pallas_call(
        body,
        name=name,
        in_specs=[HBM] * (4 * n) + [SEM, SEM, ANY],
        out_specs=[HBM] * (4 * n),
        out_shape=[pltpu.HBM(b.shape, b.dtype) for b in both],
        input_output_aliases={i: i for i in range(4 * n)},
        compiler_params=pltpu.CompilerParams(has_side_effects=EFFECT),
    )(*both, send_sems, recv_sems, after)
    return list(outs[:n]), [list(outs[n + 3 * a:n + 3 * a + 3]) for a in range(n)]


def _sibling_start(grads, name):
    n = len(grads)

    def body(*refs):
        ssem, rsem = refs[2 * n], refs[2 * n + 1]
        src, land = refs[2 * n + 2:3 * n + 2], refs[3 * n + 2:4 * n + 2]
        token = refs[4 * n + 2]
        x, y, c = _place()
        for a in range(n):
            pltpu.make_async_remote_copy(
                src_ref=src[a].at[:, :, pl.ds(1 - c, 1)], dst_ref=land[a], send_sem=ssem.at[a], recv_sem=rsem.at[a],
                device_id=(x, y, 1 - c), device_id_type=MESH).start()
        token[...] = jnp.zeros_like(token)

    lands = [lax.empty(g.shape[:2] + (1,) + g.shape[3:], g.dtype) for g in grads]
    both = list(grads) + lands
    outs = pl.pallas_call(
        body,
        name=name,
        in_specs=[HBM] * (2 * n),
        out_specs=[SEM, SEM] + [HBM] * (2 * n) + [TOKEN],
        out_shape=[pltpu.SemaphoreType.DMA((n,))] * 2 + [pltpu.HBM(b.shape, b.dtype) for b in both]
        + [jax.ShapeDtypeStruct((8, LANES), F32)],
        input_output_aliases={i: 2 + i for i in range(2 * n)},
        compiler_params=pltpu.CompilerParams(has_side_effects=EFFECT),
    )(*[_in_hbm(b) for b in both])
    return outs[0], outs[1], list(outs[2:2 + n]), list(outs[2 + n:2 + 2 * n]), outs[-1]


def _sibling_wait(send_sems, recv_sems, grads, lands, after, name):
    n = len(grads)

    def body(*refs):
        src, land = refs[:n], refs[n:2 * n]
        ssem, rsem = refs[2 * n], refs[2 * n + 1]
        x, y, c = _place()
        for a in range(n):
            cp = pltpu.make_async_remote_copy(
                src_ref=src[a].at[:, :, pl.ds(1 - c, 1)], dst_ref=land[a], send_sem=ssem.at[a], recv_sem=rsem.at[a],
                device_id=(x, y, 1 - c), device_id_type=MESH)
            cp.wait_send()
            cp.wait_recv()

    both = list(grads) + list(lands)
    outs = pl.pallas_call(
        body,
        name=name,
        in_specs=[HBM] * (2 * n) + [SEM, SEM, ANY],
        out_specs=[HBM] * (2 * n),
        out_shape=[pltpu.HBM(b.shape, b.dtype) for b in both],
        input_output_aliases={i: i for i in range(2 * n)},
        compiler_params=pltpu.CompilerParams(has_side_effects=EFFECT),
    )(*both, send_sems, recv_sems, after)
    return list(outs[:n]), list(outs[n:])


_SMALL = ("mix_norm", "q_norm", "k_norm", "sinks", "sgu_ln_g", "sgu_ln_b", "w_spatial", "b_spatial", "ffn_norm")


def _pack_rows(a):
    flat = a.reshape(-1)
    pad = (-flat.shape[0]) % LANES
    if pad:
        flat = jnp.pad(flat, (0, pad))
    return flat.reshape(-1, LANES)


def _pack(values):
    rows = jnp.concatenate([_pack_rows(values[k]) for k in _SMALL], axis=0)
    pad = (-rows.shape[0]) % 8
    if pad:
        rows = jnp.pad(rows, ((0, pad), (0, 0)))
    return rows


def _unpack(rows, like):
    out, at = {}, 0
    for k in _SMALL:
        size = like[k].size
        nrows = -(-size // LANES)
        out[k] = rows[at:at + nrows].reshape(-1)[:size].reshape(like[k].shape)
        at += nrows
    return out


def _rope_tables(t, wq, wk):
    pos = jnp.arange(t, dtype=F32)
    inv_freq = jnp.power(ROPE_THETA, -jnp.arange(0, HEAD_DIM, 2, dtype=F32) / HEAD_DIM)
    ang = pos[:, None] * inv_freq[None, :]
    cos, sin = jnp.cos(ang), jnp.sin(ang)
    cos2, sin2 = jnp.concatenate([cos, cos], axis=1), jnp.concatenate([-sin, sin], axis=1)
    return (jnp.tile(cos2, (1, wq // HEAD_DIM)), jnp.tile(sin2, (1, wq // HEAD_DIM)),
            jnp.tile(cos2, (1, wk // HEAD_DIM)), jnp.tile(sin2, (1, wk // HEAD_DIM)))


def kernel(x, mix_norm, w_in, q_norm, k_norm, sinks, sgu_ln_g, sgu_ln_b, w_spatial, b_spatial, w_attn_branch, w_sgu_branch, w_out, ffn_norm, w_gate, w_up, w_down, loss_target, m_mix_norm, m_w_in, m_q_norm, m_k_norm, m_sinks, m_sgu_ln_g, m_sgu_ln_b, m_w_spatial, m_b_spatial, m_w_attn_branch, m_w_sgu_branch, m_w_out, m_ffn_norm, m_w_gate, m_w_up, m_w_down, v_mix_norm, v_w_in, v_q_norm, v_k_norm, v_sinks, v_sgu_ln_g, v_sgu_ln_b, v_w_spatial, v_b_spatial, v_w_attn_branch, v_w_sgu_branch, v_w_out, v_ffn_norm, v_w_gate, v_w_up, v_w_down):
    names = ("mix_norm", "w_in", "q_norm", "k_norm", "sinks", "sgu_ln_g", "sgu_ln_b", "w_spatial", "b_spatial",
             "w_attn_branch", "w_sgu_branch", "w_out", "ffn_norm", "w_gate", "w_up", "w_down")
    weights = dict(zip(names, (mix_norm, w_in, q_norm, k_norm, sinks, sgu_ln_g, sgu_ln_b, w_spatial, b_spatial,
                               w_attn_branch, w_sgu_branch, w_out, ffn_norm, w_gate, w_up, w_down)))
    mom1 = dict(zip(names, (m_mix_norm, m_w_in, m_q_norm, m_k_norm, m_sinks, m_sgu_ln_g, m_sgu_ln_b, m_w_spatial,
                            m_b_spatial, m_w_attn_branch, m_w_sgu_branch, m_w_out, m_ffn_norm, m_w_gate, m_w_up,
                            m_w_down)))
    mom2 = dict(zip(names, (v_mix_norm, v_w_in, v_q_norm, v_k_norm, v_sinks, v_sgu_ln_g, v_sgu_ln_b, v_w_spatial,
                            v_b_spatial, v_w_attn_branch, v_w_sgu_branch, v_w_out, v_ffn_norm, v_w_gate, v_w_up,
                            v_w_down)))
    depth = w_in.shape[0]
    _, t, d = x.shape
    n_q_heads = sinks.shape[1]
    wq = n_q_heads * HEAD_DIM
    wk = wq // Q_PER_KV
    ws = sgu_ln_g.shape[1]
    ng = ws // LANES
    off_u = wq + 2 * wk
    off_g = off_u + 2 * ws
    tables = _rope_tables(t, wq, wk)
    px, py, pc = _place()
    core = pc.astype(jnp.int32)[None]
    chip = (2 * px + py).astype(jnp.int32)[None]
    dev = (4 * px + 2 * py + pc).astype(jnp.int32)[None]

    layers = range(depth)
    chunks = ((0,), (1, 2, 3), (4,), (5,))
    sources = [[jnp.swapaxes(w_in, 1, 2)], [jnp.swapaxes(w_attn_branch, 1, 2)], [jnp.swapaxes(w_sgu_branch, 1, 2)],
               [w_out], [jnp.swapaxes(w_gate, 1, 2), jnp.swapaxes(w_up, 1, 2)], [w_down]]
    stream = [(l, ci) for l in layers for ci in range(len(chunks))]
    placed, state, token = {}, {}, None

    def send(key, after):
        state[key] = _split_start(placed[key], GATHER_STAGES[0], "gather_send_%d_%d" % key, after)
        return state[key][3]

    def advance(key, after, stage):
        send_sems, recv_sems, bufs, _ = state[key]
        bufs = _split_wait(send_sems, recv_sems, bufs, GATHER_STAGES[stage - 1], after, "gather_wait%d_%d_%d" % (stage, *key))
        state[key] = _split_start(bufs, GATHER_STAGES[stage], "gather_pass%d_%d_%d" % (stage, *key))
        return state[key][3]

    def relay(key, after):
        send_sems, recv_sems, bufs, _ = state[key]
        bufs = _split_wait(send_sems, recv_sems, bufs, GATHER_STAGES[0], after, "gather_wait1_%d_%d" % key)
        at = stream.index(key)
        later = stream[at + 2:at + 3] if at else stream[1:3]
        groups = [(bufs, GATHER_STAGES[1])] + [(placed[k], GATHER_STAGES[0]) for k in later]
        started, tok = _split_start_many(groups, "gather_pass1_%d_%d" % key)
        for k, (ssem, rsem, arrays) in zip([key] + later, started):
            state[k] = (ssem, rsem, arrays, tok)
        return tok

    def ready(key, after):
        send_sems, recv_sems, bufs, _ = state.pop(key)
        bufs = _split_wait(send_sems, recv_sems, bufs, GATHER_STAGES[2], after, "gather_wait3_%d_%d" % key)
        return [f.reshape(f.shape[0] * f.shape[1] * f.shape[2], f.shape[3]) for f in bufs]

    for key in stream:
        l, ci = key
        placed[key] = [_place_shard(sources[a], l, dev, BF16, f"place_shard_{l}_{a}",
                                    after=token if a == chunks[ci][0] else None) for a in chunks[ci]]
        token = send(key, None) if key == stream[0] else placed[key][-1]

    saved = []
    xl = x[0]
    going = relay((0, 0), token)
    going = advance((0, 0), going, 2)
    for l in layers:
        gq = jnp.tile(q_norm[l], n_q_heads)[None]
        gk = jnp.tile(k_norm[l], n_q_heads // Q_PER_KV)[None]
        bt = b_spatial[l].T
        h = _rmsnorm_fwd(xl, mix_norm[l][None], f"mix_norm_fwd_{l}", after=going)
        (win_t,) = ready((l, 0), h)
        proj = _mm(h, win_t, "nt", F32, f"in_proj_{l}")
        going = relay((l, 1), proj)
        attn = _attn_fwd(proj, tables, gq, gk, sinks[l], wq, wk, f"attn_fwd_{l}", after=going)
        going = advance((l, 1), attn, 2)
        sgu = _sgu_fwd(proj, sgu_ln_g[l][None], sgu_ln_b[l][None], w_spatial[l], bt, off_u, ws, f"sgu_fwd_{l}",
                       after=going)
        wab_t, wsb_t, wo = ready((l, 1), sgu)
        br_a, br_b, merged = _branches_fwd(attn, sgu, wab_t, wsb_t, proj, off_g, f"branches_{l}")
        going = relay((l, 2), merged)
        x1 = _mm(merged, wo, "nn", F32, f"out_proj_{l}", residual=xl, after=going)
        going = advance((l, 2), x1, 2)
        h2 = _rmsnorm_fwd(x1, ffn_norm[l][None], f"ffn_norm_fwd_{l}", after=going)
        (wgu_t,) = ready((l, 2), h2)
        going = relay((l, 3), h2)
        gu, act = _gate_up_fwd(h2, wgu_t, f"gate_up_{l}", after=going)
        going = advance((l, 3), act, 2)
        if l + 1 < depth:
            going = relay((l + 1, 0), going)
        (wd,) = ready((l, 3), going)
        x2 = _mm(act, wd, "nn", F32, f"down_proj_{l}", residual=x1)
        if l + 1 < depth:
            going = advance((l + 1, 0), x2, 2)
        saved.append(dict(x0=xl, h=h, proj=proj, attn=attn, sgu=sgu, br_a=br_a, br_b=br_b, merged=merged, x1=x1,
                          h2=h2, gu=gu, act=act, gq=gq, gk=gk, bt=bt, win_t=win_t, wab_t=wab_t, wsb_t=wsb_t, wo=wo,
                          wgu_t=wgu_t, wd=wd))
        xl = x2

    loss_part, dx, dx16 = _loss_and_grad(xl, loss_target[0], "loss")
    loss = lax.psum(loss_part[0, 0], ("x", "y", "c"))

    def sibling_start(grads, tag):
        shaped = []
        for g, p in grads:
            rows, c = g.shape
            shaped.append(g.reshape(p, 4, 2, rows // (8 * p), c))
        send_sems, recv_sems, shaped, lands, tok = _sibling_start(shaped, f"rs_sibling_start_{tag}")
        return (send_sems, recv_sems, shaped, lands, tag), tok

    def chips_start(state, after, first=None):
        send_sems, recv_sems, shaped, lands, tag = state
        shaped, lands = _sibling_wait(send_sems, recv_sems, shaped, lands, after, f"rs_sibling_wait_{tag}")
        sums = [_sum_sibling(g, o, core, f"rs_add_sibling_{tag}_{a}") for a, (g, o) in enumerate(zip(shaped, lands))]
        gate = None if first is None else first(sums[0])
        send_sems, recv_sems, sums, lands, tok = _chips_start(sums, f"rs_chips_start_{tag}", after=gate)
        return (send_sems, recv_sems, sums, lands, tag), tok

    def scatter_finish(state, after):
        send_sems, recv_sems, sums, lands, tag = state
        sums, lands = _chips_wait(send_sems, recv_sems, sums, lands, after, f"rs_chips_wait_{tag}")
        return [[s] + o for s, o in zip(sums, lands)]

    in_flight = [dict() for _ in layers]
    small_grads = [None] * depth
    tok, swap_in = None, None
    for l in reversed(layers):
        s = saved[l]
        dgu = _gate_up_bwd(dx16, s["wd"], s["gu"], f"d_gate_up_{l}", after=tok)
        if swap_in is not None:
            in_flight[l + 1]["in"], tok = chips_start(swap_in, dgu)
        g_wd = _mm(s["act"], dx16, "tn", BF16, f"g_w_down_{l}", after=tok)
        swap, tok_s = sibling_start([(g_wd, 1)], f"{l}_down")
        dh2 = _mm(dgu, s["wgu_t"], "nn", F32, f"d_h2_{l}", after=tok_s)
        in_flight[l]["down"], tok = chips_start(swap, dh2)
        g_wgu_t = _mm(dgu, s["h2"], "tn", BF16, f"g_w_gate_up_{l}", after=tok)
        swap, tok_s = sibling_start([(g_wgu_t, 2)], f"{l}_gate_up")
        dx1, dx1_16, g_ffn = _rmsnorm_bwd(s["x1"], ffn_norm[l][None], dh2, dx, f"ffn_norm_bwd_{l}", after=tok_s)
        d_a, d_b, dla, dlb = _branches_bwd(dx1_16, s["wo"], s["br_a"], s["br_b"], s["proj"], off_g,
                                           f"d_branches_{l}")
        in_flight[l]["gate_up"], tok = chips_start(swap, d_a)
        g_wo = _mm(s["merged"], dx1_16, "tn", BF16, f"g_w_out_{l}", after=tok)
        dattn = _mm(d_a, s["wab_t"], "nn", F32, f"d_attn_{l}", after=g_wo)
        g_wab_t = _mm(d_a, s["attn"], "tn", BF16, f"g_w_attn_branch_{l}")
        dsgu = _mm(d_b, s["wsb_t"], "nn", F32, f"d_sgu_{l}")
        g_wsb_t = _mm(d_b, s["sgu"], "tn", BF16, f"g_w_sgu_branch_{l}")
        swap, tok_s = sibling_start([(g_wab_t, 1), (g_wsb_t, 1), (g_wo, 1)], f"{l}_mix")
        dq, dk, dv, g_gq, g_gk, g_sinks = _attn_bwd(s["proj"], dattn, tables, s["gq"], s["gk"], sinks[l], wq, wk,
                                                    f"attn_bwd_{l}", after=tok_s)
        du, dvv, g_lng, g_lnb, g_ws, g_bs = _sgu_bwd(s["proj"], dsgu, sgu_ln_g[l][None], sgu_ln_b[l][None],
                                                     w_spatial[l], s["bt"], off_u, ws, f"sgu_bwd_{l}")
        dproj = jnp.concatenate([dq, dk.astype(BF16), dv.astype(BF16), du, dvv, dla, dlb], axis=1)
        dh = _mm(dproj, s["win_t"], "nn", F32, f"d_h_{l}")
        in_flight[l]["mix"], tok = chips_start(swap, dh)
        g_win_t = _mm(dproj, s["h"], "tn", BF16, f"g_w_in_{l}", after=tok)
        swap_in, tok = sibling_start([(g_win_t, 1)], f"{l}_in")
        dx, dx16, g_mix = _rmsnorm_bwd(s["x0"], mix_norm[l][None], dh, dx1, f"mix_norm_bwd_{l}", after=tok)
        small_grads[l] = dict(
            mix_norm=g_mix[0], q_norm=g_gq[0].reshape(n_q_heads, HEAD_DIM).sum(0),
            k_norm=g_gk[0].reshape(n_q_heads // Q_PER_KV, HEAD_DIM).sum(0), sinks=g_sinks[0, :n_q_heads],
            sgu_ln_g=g_lng[0], sgu_ln_b=g_lnb[0], w_spatial=g_ws, b_spatial=g_bs[:, 0, :], ffn_norm=g_ffn[0])
    grad_x = dx[None]

    result = {key: {} for key in ("grad", "delta", "m", "v")}
    layer_like = {k: weights[k][0] for k in _SMALL}
    packed_g = jnp.concatenate([_pack(small_grads[l]) for l in layers], axis=0)
    rows_per_layer = packed_g.shape[0] // depth
    small_buf = _place_shard([packed_g[None]], 0, dev, F32, "place_small_grads", after=tok)
    send_sems, recv_sems, small_bufs, tok = _split_start([small_buf], GATHER_STAGES[0], "gather_send_small")
    small_state = [(send_sems, recv_sems, small_bufs)]

    def small_stage(stage, after):
        ssem, rsem, bufs = small_state[0]
        bufs = _split_wait(ssem, rsem, bufs, GATHER_STAGES[stage - 1], after, f"gather_wait{stage}_small")
        ssem, rsem, bufs, token = _split_start(bufs, GATHER_STAGES[stage], f"gather_pass{stage}_small")
        small_state[0] = (ssem, rsem, bufs)
        return token

    in_flight[0]["in"], tok = chips_start(swap_in, tok, first=functools.partial(small_stage, 1))

    def update(k, grads, transposed, after):
        view = (lambda a: jnp.swapaxes(a, 1, 2)) if transposed else (lambda a: a)
        outs = _adam(view(weights[k]), grads, view(mom1[k]), view(mom2[k]), chip, f"adam_{k}", after=after)
        for key, val in zip(("grad", "delta", "m", "v"), outs):
            result[key][k] = view(val)
        return outs[3]

    def plain(terms, tag):
        s, lands = terms[0], terms[1:]
        g = _sum_chips(s, lands, chip, f"rs_add_chips_{tag}")
        return [jnp.swapaxes(g, 1, 2)[:, None]]

    down = [scatter_finish(in_flight[l]["down"], tok) for l in reversed(layers)][::-1]
    tok = update("w_down", [(down[l][0], 0) for l in layers], False, None)
    tok = small_stage(2, tok)
    gate_up = [scatter_finish(in_flight[l]["gate_up"], tok) for l in reversed(layers)][::-1]
    tok = update("w_gate", [(gate_up[l][0], 0) for l in layers], True, None)
    tok = update("w_up", [(gate_up[l][0], 1) for l in layers], True, tok)
    mix =[scatter_finish(in_flight[l]["mix"], tok) for l in reversed(layers)][::-1]
    tok = update("w_out", [(mix[l][2], 0) for l in layers], False, None)
    tok = update("w_attn_branch", [(plain(mix[l][0], f"{l}_attn_branch"), 0) for l in layers], False, tok)
    tok = update("w_sgu_branch", [(plain(mix[l][1], f"{l}_sgu_branch"), 0) for l in layers], False, tok)

    packed = [jnp.concatenate([_pack({k: src[k][l] for k in _SMALL}) for l in layers], axis=0)
              for src in (weights, mom1, mom2)]
    (gathered_small,) = _split_wait(*small_state[0], GATHER_STAGES[2], tok, "gather_wait3_small")
    small = _small_reduce_adam(gathered_small[0], *packed, "small_reduce_adam")
    for key, rows in zip(("grad", "delta", "m", "v"), small):
        per_layer = [_unpack(rows[l * rows_per_layer:(l + 1) * rows_per_layer], layer_like) for l in layers]
        for k in _SMALL:
            result[key][k] = jnp.stack([per_layer[l][k] for l in layers])

    last = [scatter_finish(in_flight[l]["in"], small[0]) for l in reversed(layers)][::-1]
    update("w_in", [(last[l][0], 0) for l in layers], True, result["v"]["ffn_norm"])

    return (loss, grad_x, *[result["grad"][k] for k in names], *[result["delta"][k] for k in names],
            *[result["m"][k] for k in names], *[result["v"][k] for k in names])
```

```python
import functools
import math

import jax
import jax.numpy as jnp
from jax import lax
from jax.experimental import pallas as pl
from jax.experimental.pallas import tpu as pltpu

F32 = jnp.float32
BF16 = jnp.bfloat16
MESH = pl.DeviceIdType.MESH
ANY = pl.BlockSpec(memory_space=pl.ANY)

N_DEV = 8
HEAD_DIM = 64
Q_PER_KV = 4
BLOCK = 128
LANES = 128
ROPE_THETA = 10000.0
EPS = 1e-6
ADAM_LR = 0.001
ADAM_B1 = 0.9
ADAM_B2 = 0.999
ADAM_EPS = 1e-08
ADAM_WD = 0.01
ADAM_STEP = 10
NEG = -1e30
VMEM_LIMIT_BYTES = 56 * 1024 * 1024

NN = ((1,), (0,))
NT = ((1,), (1,))
TN = ((0,), (0,))


def _dot(a, b, dims):
    return lax.dot_general(a, b, (dims, ((), ())), preferred_element_type=F32)


def _params(*sem):
    return pltpu.CompilerParams(dimension_semantics=sem, vmem_limit_bytes=VMEM_LIMIT_BYTES)


def _divisor_tile(n, limit, unit):
    if n <= limit:
        return n
    best = unit
    for t in range(unit, limit + 1, unit):
        if n % t == 0:
            best = t
    assert n % best == 0, (n, limit, unit)
    return best


def _row_chunks(rows, size=256):
    size = min(size, rows)
    assert rows % size == 0, (rows, size)
    return [pl.ds(start, size) for start in range(0, rows, size)]


def _mm(a, b, mode, out_dtype, name, residual=None, after=None):
    parts = a.shape[0] if a.ndim == 3 else 1
    a2 = a.shape[-2:]
    if mode == "nn":
        (m, kp), (k2, n) = a2, b.shape
        k, mp = kp * parts, m
    elif mode == "nt":
        (m, kp), (n, k2) = a2, b.shape
        k, mp = kp * parts, m
    else:
        (k, mp), (k2, n) = a2, b.shape
        m, kp = mp * parts, k
    assert k == k2, (name, a.shape, b.shape)
    tk = _divisor_tile(kp, 2816, 128)
    nk = k // tk
    tm = _divisor_tile(mp, 512 if mode == "tn" else 1024, 128)
    tn = _divisor_tile(n, 2048 if mode == "tn" else 1024, 128)
    kpb, mpb = kp // tk, mp // tm
    dims = {"nn": NN, "nt": NT, "tn": TN}[mode]
    lead = (None,) if a.ndim == 3 else ()
    if mode == "tn":
        a_index = lambda i, j, kk: (i // mpb, kk, i % mpb) if lead else (kk, i)
        a_spec = pl.BlockSpec(lead + (tk, tm), a_index)
    else:
        a_index = lambda i, j, kk: (kk // kpb, i, kk % kpb) if lead else (i, kk)
        a_spec = pl.BlockSpec(lead + (tm, tk), a_index)
    if mode == "nt":
        b_spec = pl.BlockSpec((tn, tk), lambda i, j, kk: (j, kk))
    else:
        b_spec = pl.BlockSpec((tk, tn), lambda i, j, kk: (kk, j))
    o_spec = pl.BlockSpec((tm, tn), lambda i, j, kk: (i, j))
    has_res = residual is not None

    def body(*refs):
        a_ref, b_ref = refs[:2]
        r_ref = refs[2] if has_res else None
        o_ref, acc_ref = refs[-2:]
        kk = pl.program_id(2)
        p = _dot(a_ref[...], b_ref[...], dims)

        def finish(total):
            if has_res:
                total = total + r_ref[...]
            o_ref[...] = total.astype(o_ref.dtype)

        if nk == 1:
            finish(p)
        else:
            @pl.when(kk == 0)
            def _():
                acc_ref[...] = p

            @pl.when(jnp.logical_and(kk > 0, kk < nk - 1))
            def _():
                acc_ref[...] += p

            @pl.when(kk == nk - 1)
            def _():
                finish(acc_ref[...] + p)

    in_specs = [a_spec, b_spec] + ([o_spec] if has_res else []) + ([ANY] if after is not None else [])
    args = (a, b) + ((residual,) if has_res else ()) + ((after,) if after is not None else ())
    acc_shape = (tm, tn) if nk > 1 else (8, LANES)
    return pl.pallas_call(
        body,
        name=name,
        grid=(m // tm, n // tn, nk),
        in_specs=in_specs,
        out_specs=o_spec,
        out_shape=jax.ShapeDtypeStruct((m, n), out_dtype),
        scratch_shapes=[pltpu.VMEM(acc_shape, F32)],
        compiler_params=_params("parallel", "parallel", "arbitrary"),
    )(*args)


def _rmsnorm_fwd(x, g, name, after=None):
    t, d = x.shape
    tr = _divisor_tile(t, 512, 8)

    def body(x_ref, g_ref, *rest):
        h_ref = rest[-1]
        xv = x_ref[...]
        rstd = lax.rsqrt(jnp.mean(xv * xv, axis=-1, keepdims=True) + EPS)
        h_ref[...] = (xv * rstd * g_ref[...]).astype(h_ref.dtype)

    return pl.pallas_call(
        body,
        name=name,
        grid=(t // tr,),
        in_specs=[pl.BlockSpec((tr, d), lambda i: (i, 0)), pl.BlockSpec((1, d), lambda i: (0, 0))]
        + ([ANY] if after is not None else []),
        out_specs=pl.BlockSpec((tr, d), lambda i: (i, 0)),
        out_shape=jax.ShapeDtypeStruct((t, d), BF16),
        compiler_params=_params("parallel"),
    )(x, g, *(() if after is None else (after,)))


def _rmsnorm_bwd(x, g, dh, dres, name, after=None):
    t, d = x.shape
    tr = _divisor_tile(t, 256, 8)

    def body(x_ref, g_ref, dh_ref, dres_ref, *rest):
        dx_ref, dx16_ref, dg_ref = rest[-3:]
        i = pl.program_id(0)
        xv = x_ref[...]
        rstd = lax.rsqrt(jnp.mean(xv * xv, axis=-1, keepdims=True) + EPS)
        xh = xv * rstd
        dhv = dh_ref[...]
        dxh = dhv * g_ref[...]
        dx = dres_ref[...] + rstd * (dxh - xh * jnp.mean(dxh * xh, axis=-1, keepdims=True))
        dx_ref[...] = dx
        dx16_ref[...] = dx.astype(dx16_ref.dtype)
        part = jnp.broadcast_to(jnp.sum(dhv * xh, axis=0, keepdims=True), dg_ref.shape)

        @pl.when(i == 0)
        def _():
            dg_ref[...] = part

        @pl.when(i > 0)
        def _():
            dg_ref[...] += part

    row = pl.BlockSpec((tr, d), lambda i: (i, 0))
    return pl.pallas_call(
        body,
        name=name,
        grid=(t // tr,),
        in_specs=[row, pl.BlockSpec((1, d), lambda i: (0, 0)), row, row] + ([ANY] if after is not None else []),
        out_specs=[row, row, pl.BlockSpec((8, d), lambda i: (0, 0))],
        out_shape=[jax.ShapeDtypeStruct((t, d), F32), jax.ShapeDtypeStruct((t, d), BF16),
                   jax.ShapeDtypeStruct((8, d), F32)],
        compiler_params=_params("arbitrary"),
    )(x, g, dh, dres, *(() if after is None else (after,)))


def _lane(shape):
    return lax.broadcasted_iota(jnp.int32, shape, 1)


def _group_sum64(s):
    row = lax.broadcasted_iota(jnp.int32, (LANES, LANES), 0)
    col = lax.broadcasted_iota(jnp.int32, (LANES, LANES), 1)
    ones = jnp.where((row >= HEAD_DIM) == (col >= HEAD_DIM), 1.0, 0.0).astype(BF16)
    out = []
    for t in range(s.shape[1] // LANES):
        piece = s[:, LANES * t:LANES * t + LANES]
        hi = piece.astype(BF16)
        lo = (piece - hi.astype(F32)).astype(BF16)
        out.append(_dot(hi, ones, NN) + _dot(lo, ones, NN))
    return out[0] if len(out) == 1 else jnp.concatenate(out, axis=1)


def _swap32(x):
    w = x.shape[1]
    return jnp.where((_lane(x.shape) & 32) == 0, pltpu.roll(x, w - 32, axis=1), pltpu.roll(x, 32, axis=1))


def _rope(x, c, s):
    return x * c + _swap32(x) * s


def _rope_t(dy, c, s):
    return dy * c + _swap32(dy * s)


def _head_norm(x):
    rstd = lax.rsqrt(_group_sum64(x * x) * (1.0 / HEAD_DIM) + EPS)
    return x * rstd, rstd


def _head_norm_bwd(dxh, xh, rstd):
    return rstd * (dxh - xh * (_group_sum64(dxh * xh) * (1.0 / HEAD_DIM)))


def _roll64(x):
    return pltpu.roll(x, 64, axis=1)


def _attn_specs(wq, wk):
    kb = wq // wk
    prev = lambda i: jnp.maximum(i - 1, 0)
    return dict(
        q=pl.BlockSpec((BLOCK, wq), lambda i: (i, 0)),
        kc=pl.BlockSpec((BLOCK, wk), lambda i: (i, kb)),
        kp=pl.BlockSpec((BLOCK, wk), lambda i: (prev(i), kb)),
        vc=pl.BlockSpec((BLOCK, wk), lambda i: (i, kb + 1)),
        vp=pl.BlockSpec((BLOCK, wk), lambda i: (prev(i), kb + 1)),
        tq=pl.BlockSpec((BLOCK, wq), lambda i: (i, 0)),
        tkp=pl.BlockSpec((BLOCK, wk), lambda i: (prev(i), 0)),
        gq=pl.BlockSpec((1, wq), lambda i: (0, 0)),
        gk=pl.BlockSpec((1, wk), lambda i: (0, 0)),
        sinks=pl.BlockSpec(memory_space=pltpu.SMEM),
    )


def _attn_prologue(i, q_ref, kc_ref, kp_ref, cq_ref, sq_ref, ckp_ref, skp_ref, gq_ref, gk_ref):
    wk = kc_ref.shape[1]
    cq, sq = cq_ref[...], sq_ref[...]
    ck, sk = cq[:, :wk], sq[:, :wk]
    qh, q_rstd = _head_norm(q_ref[...])
    kch, kc_rstd = _head_norm(kc_ref[...])
    kph, kp_rstd = _head_norm(kp_ref[...])
    qn = _rope(qh * gq_ref[...], cq, sq)
    knc = _rope(kch * gk_ref[...], ck, sk)
    knp = _rope(kph * gk_ref[...], ckp_ref[...], skp_ref[...])
    stacked = (Q_PER_KV * BLOCK, BLOCK)
    row = lax.broadcasted_iota(jnp.int32, stacked, 0) & (BLOCK - 1)
    col = lax.broadcasted_iota(jnp.int32, stacked, 1)
    mask_c = col <= row
    valid = jnp.logical_or(mask_c, i > 0)
    half = (lax.broadcasted_iota(jnp.int32, (BLOCK, BLOCK), 1) >= HEAD_DIM).astype(jnp.int32)
    return dict(cq=cq, sq=sq, ck=ck, sk=sk, qh=qh, q_rstd=q_rstd, kch=kch, kc_rstd=kc_rstd, kph=kph,
                kp_rstd=kp_rstd, qn=qn, knc=knc, knp=knp, mask_c=mask_c, valid=valid, half=half)


def _stack_heads(x, g, half):
    kpar = g % 2
    pieces = []
    for j in range(Q_PER_KV):
        t, e = divmod(Q_PER_KV * g + j, 2)
        piece = jnp.where(half == e, x[:, LANES * t:LANES * t + LANES], 0.0)
        pieces.append(piece if e == kpar else _roll64(piece))
    return jnp.concatenate(pieces, axis=0)


def _unstack_heads(y, g, half):
    kpar = g % 2
    slabs = {}
    for j in range(Q_PER_KV):
        t, e = divmod(Q_PER_KV * g + j, 2)
        piece = jnp.where(half == kpar, y[BLOCK * j:BLOCK * j + BLOCK], 0.0)
        piece = piece if e == kpar else _roll64(piece)
        slabs[t] = piece if t not in slabs else slabs[t] + piece
    return slabs


def _group_scores(st, g, sinks_ref, scale):
    ks = g // 2
    sl = slice(LANES * ks, LANES * ks + LANES)
    q4 = _stack_heads(st["qn"], g, st["half"]).astype(BF16)
    kc, kp = st["knc"][:, sl].astype(BF16), st["knp"][:, sl].astype(BF16)
    rows = Q_PER_KV * BLOCK
    at = lax.broadcasted_iota(jnp.int32, (rows, 1), 0)
    head = jnp.zeros((rows, 1), jnp.int32)
    sink = jnp.zeros((rows, 1), F32) + sinks_ref[Q_PER_KV * g]
    for j in range(1, Q_PER_KV):
        head = jnp.where(at >= BLOCK * j, j, head)
        sink = jnp.where(at >= BLOCK * j, sinks_ref[Q_PER_KV * g + j], sink)
    cur = st["mask_c"]
    s = jnp.where(cur, _dot(q4, kc, NT), _dot(q4, kp, NT)) * scale
    s = jnp.where(st["valid"], s, NEG)
    m = jnp.maximum(jnp.max(s, axis=1, keepdims=True), sink)
    p = jnp.exp(s - m)
    p_s = jnp.exp(sink - m)
    inv = 1.0 / (jnp.sum(p, axis=1, keepdims=True) + p_s)
    return dict(sl=sl, head=head, q4=q4, kc=kc, kp=kp, cur=cur, pr=p * inv, pr_s=p_s * inv)


def _attn_fwd(proj, tables, gq, gk, sinks, wq, wk, name, after=None):
    t = proj.shape[0]
    nb = t // BLOCK
    sp = _attn_specs(wq, wk)
    scale = HEAD_DIM ** -0.5
    cos_t, sin_t, cos_k, sin_k = tables

    def body(sinks_ref, q_ref, kc_ref, kp_ref, vc_ref, vp_ref, cq_ref, sq_ref, ckp_ref, skp_ref, gq_ref, gk_ref,
             *rest):
        o_ref = rest[-1]
        i = pl.program_id(0)
        st = _attn_prologue(i, q_ref, kc_ref, kp_ref, cq_ref, sq_ref, ckp_ref, skp_ref, gq_ref, gk_ref)
        for g in range(wq // (Q_PER_KV * HEAD_DIM)):
            gs = _group_scores(st, g, sinks_ref, scale)
            own = st["half"] == g % 2
            vc = jnp.where(own, vc_ref[:, gs["sl"]], 0.0).astype(BF16)
            vp = jnp.where(own, vp_ref[:, gs["sl"]], 0.0).astype(BF16)
            pr = gs["pr"].astype(BF16)
            zero = jnp.zeros_like(pr)
            out = _dot(jnp.where(gs["cur"], pr, zero), vc, NN) + _dot(jnp.where(gs["cur"], zero, pr), vp, NN)
            for ts, slab in _unstack_heads(out, g, st["half"]).items():
                o_ref[:, LANES * ts:LANES * ts + LANES] = slab.astype(o_ref.dtype)

    return pl.pallas_call(
        body,
        name=name,
        grid=(nb,),
        in_specs=[sp["sinks"], sp["q"], sp["kc"], sp["kp"], sp["vc"], sp["vp"], sp["tq"], sp["tq"], sp["tkp"],
                  sp["tkp"], sp["gq"], sp["gk"]] + ([ANY] if after is not None else []),
        out_specs=pl.BlockSpec((BLOCK, wq), lambda i: (i, 0)),
        out_shape=jax.ShapeDtypeStruct((t, wq), BF16),
        compiler_params=_params("parallel"),
    )(sinks, proj, proj, proj, proj, proj, cos_t, sin_t, cos_k, sin_k, gq, gk, *(() if after is None else (after,)))


def _attn_bwd(proj, dout, tables, gq, gk, sinks, wq, wk, name, after=None):
    t = proj.shape[0]
    nb = t // BLOCK
    sp = _attn_specs(wq, wk)
    scale = HEAD_DIM ** -0.5
    cos_t, sin_t, cos_k, sin_k = tables

    def body(sinks_ref, q_ref, kc_ref, kp_ref, vc_ref, vp_ref, cq_ref, sq_ref, ckp_ref, skp_ref, gq_ref, gk_ref,
             do_ref, *rest):
        dq_ref, dk_ref, dv_ref, dgq_ref, dgk_ref, dsk_ref, dqn_ref, dknc_ref, dknp_ref, dvc_ref, dvp_ref = rest[-11:]
        i = pl.program_id(0)
        st = _attn_prologue(i, q_ref, kc_ref, kp_ref, cq_ref, sq_ref, ckp_ref, skp_ref, gq_ref, gk_ref)
        dknc_ref[...] = jnp.zeros_like(dknc_ref)
        dknp_ref[...] = jnp.zeros_like(dknp_ref)
        dvc_ref[...] = jnp.zeros_like(dvc_ref)
        dvp_ref[...] = jnp.zeros_like(dvp_ref)
        lane8 = _lane((8, LANES))
        dsinks = jnp.zeros((8, LANES), F32)
        for g in range(wq // (Q_PER_KV * HEAD_DIM)):
            gs = _group_scores(st, g, sinks_ref, scale)
            sl = gs["sl"]
            do4 = _stack_heads(do_ref[...], g, st["half"]).astype(BF16)
            cur, pr = gs["cur"], gs["pr"]
            dp = jnp.where(cur, _dot(do4, vc_ref[:, sl].astype(BF16), NT), _dot(do4, vp_ref[:, sl].astype(BF16), NT))
            rs = jnp.sum(pr * dp, axis=1, keepdims=True)
            ds = (pr * (dp - rs) * scale).astype(BF16)
            pr16 = pr.astype(BF16)
            zero = jnp.zeros_like(ds)
            ds_c, ds_p = jnp.where(cur, ds, zero), jnp.where(cur, zero, ds)
            pr_c, pr_p = jnp.where(cur, pr16, zero), jnp.where(cur, zero, pr16)
            dsink_rows = -gs["pr_s"] * rs
            for j in range(Q_PER_KV):
                dsink = jnp.sum(jnp.where(gs["head"] == j, dsink_rows, 0.0))
                dsinks = dsinks + jnp.where(lane8 == Q_PER_KV * g + j, dsink, 0.0)
            dq4 = _dot(ds_c, gs["kc"], NN) + _dot(ds_p, gs["kp"], NN)
            for ts, slab in _unstack_heads(dq4, g, st["half"]).items():
                dqn_ref[:, LANES * ts:LANES * ts + LANES] = slab
            dvc_ref[:, sl] += _dot(pr_c.astype(BF16), do4, TN)
            dvp_ref[:, sl] += _dot(pr_p.astype(BF16), do4, TN)
            dknc_ref[:, sl] += _dot(ds_c, gs["q4"], TN)
            dknp_ref[:, sl] += _dot(ds_p, gs["q4"], TN)

        gqv, gkv = gq_ref[...], gk_ref[...]
        dqg = _rope_t(dqn_ref[...], st["cq"], st["sq"])
        dq_ref[...] = _head_norm_bwd(dqg * gqv, st["qh"], st["q_rstd"]).astype(dq_ref.dtype)
        dkcg = _rope_t(dknc_ref[...], st["ck"], st["sk"])
        dkpg = _rope_t(dknp_ref[...], ckp_ref[...], skp_ref[...])
        dk_cur = _head_norm_bwd(dkcg * gkv, st["kch"], st["kc_rstd"])
        dk_prev = _head_norm_bwd(dkpg * gkv, st["kph"], st["kp_rstd"])
        dgq_part = jnp.broadcast_to(jnp.sum(dqg * st["qh"], axis=0, keepdims=True), dgq_ref.shape)
        dgk_part = jnp.broadcast_to(
            jnp.sum(dkcg * st["kch"] + dkpg * st["kph"], axis=0, keepdims=True), dgk_ref.shape)
        cur = pl.ds(pl.multiple_of(i * BLOCK, BLOCK), BLOCK)
        dk_ref[cur, :] = dk_cur
        dv_ref[cur, :] = dvc_ref[...]

        @pl.when(i == 0)
        def _():
            dgq_ref[...] = dgq_part
            dgk_ref[...] = dgk_part
            dsk_ref[...] = dsinks

        @pl.when(i > 0)
        def _():
            before = pl.ds(pl.multiple_of((i - 1) * BLOCK, BLOCK), BLOCK)
            dk_ref[before, :] += dk_prev
            dv_ref[before, :] += dvp_ref[...]
            dgq_ref[...] += dgq_part
            dgk_ref[...] += dgk_part
            dsk_ref[...] += dsinks

    whole = lambda shape: pl.BlockSpec(shape, lambda i: (0, 0))
    return pl.pallas_call(
        body,
        name=name,
        grid=(nb,),
        in_specs=[sp["sinks"], sp["q"], sp["kc"], sp["kp"], sp["vc"], sp["vp"], sp["tq"], sp["tq"], sp["tkp"],
                  sp["tkp"], sp["gq"], sp["gk"], pl.BlockSpec((BLOCK, wq), lambda i: (i, 0))]
        + ([ANY] if after is not None else []),
        out_specs=[pl.BlockSpec((BLOCK, wq), lambda i: (i, 0)), whole((t, wk)), whole((t, wk)), whole((8, wq)),
                   whole((8, wk)), whole((8, LANES))],
        out_shape=[jax.ShapeDtypeStruct((t, wq), BF16), jax.ShapeDtypeStruct((t, wk), F32),
                   jax.ShapeDtypeStruct((t, wk), F32), jax.ShapeDtypeStruct((8, wq), F32),
                   jax.ShapeDtypeStruct((8, wk), F32), jax.ShapeDtypeStruct((8, LANES), F32)],
        scratch_shapes=[pltpu.VMEM((BLOCK, wq), F32), pltpu.VMEM((BLOCK, wk), F32), pltpu.VMEM((BLOCK, wk), F32),
                        pltpu.VMEM((BLOCK, wk), F32), pltpu.VMEM((BLOCK, wk), F32)],
        compiler_params=_params("arbitrary"),
    )(sinks, proj, proj, proj, proj, proj, cos_t, sin_t, cos_k, sin_k, gq, gk, dout,
      *(() if after is None else (after,)))


_GELU_K = math.sqrt(2.0 / math.pi)
_GELU_A = 0.044715


def _gelu(x):
    return 0.5 * x * (1.0 + jnp.tanh(_GELU_K * (x + _GELU_A * x * x * x)))


def _gelu_and_grad(x):
    th = jnp.tanh(_GELU_K * (x + _GELU_A * x * x * x))
    return (0.5 * x * (1.0 + th),
            0.5 * (1.0 + th) + 0.5 * x * (1.0 - th * th) * (_GELU_K * (1.0 + 3.0 * _GELU_A * x * x)))


def _group_ln(v):
    mu = jnp.mean(v, axis=1, keepdims=True)
    cen = v - mu
    rstd = lax.rsqrt(jnp.mean(cen * cen, axis=1, keepdims=True) + EPS)
    return cen * rstd, rstd


def _sgu_geometry(off_u, ws):
    cw = math.gcd(off_u, ws)
    return cw, ws // cw, off_u // cw, (off_u + ws) // cw


def _sgu_fwd(proj, ln_g, ln_b, w_s, bt, off_u, ws, name, after=None):
    t = proj.shape[0]
    nb = t // BLOCK
    cw, nc, ub, vb = _sgu_geometry(off_u, ws)
    gpc = cw // LANES
    ng = ws // LANES

    def body(u_ref, v_ref, g_ref, b_ref, w_ref, bt_ref, *rest):
        o_ref = rest[-1]
        jc = pl.program_id(0)
        row = lax.broadcasted_iota(jnp.int32, (BLOCK, BLOCK), 0)
        col = lax.broadcasted_iota(jnp.int32, (BLOCK, BLOCK), 1)
        lane_g = _lane((BLOCK, ng))
        for gi in range(gpc):
            sl = slice(LANES * gi, LANES * gi + LANES)
            xh, _ = _group_ln(_gelu(v_ref[:, sl]))
            vn = xh * g_ref[:, sl] + b_ref[:, sl]
            w = jnp.where(row >= col, w_ref[gi], 0.0).astype(BF16)
            bias = jnp.sum(jnp.where(lane_g == jc * gpc + gi, bt_ref[...], 0.0), axis=1, keepdims=True)
            s = _dot(w, vn.astype(BF16), NN) + bias
            o_ref[:, sl] = (_gelu(u_ref[:, sl]) * s).astype(o_ref.dtype)

    return pl.pallas_call(
        body,
        name=name,
        grid=(nc, nb),
        in_specs=[pl.BlockSpec((BLOCK, cw), lambda jc, i: (i, ub + jc)),
                  pl.BlockSpec((BLOCK, cw), lambda jc, i: (i, vb + jc)),
                  pl.BlockSpec((1, cw), lambda jc, i: (0, jc)),
                  pl.BlockSpec((1, cw), lambda jc, i: (0, jc)),
                  pl.BlockSpec((gpc, BLOCK, BLOCK), lambda jc, i: (jc, 0, 0)),
                  pl.BlockSpec((BLOCK, ng), lambda jc, i: (0, 0))] + ([ANY] if after is not None else []),
        out_specs=pl.BlockSpec((BLOCK, cw), lambda jc, i: (i, jc)),
        out_shape=jax.ShapeDtypeStruct((t, ws), BF16),
        compiler_params=_params("parallel", "parallel"),
    )(proj, proj, ln_g, ln_b, w_s, bt, *(() if after is None else (after,)))


def _sgu_bwd(proj, dout, ln_g, ln_b, w_s, bt, off_u, ws, name):
    t = proj.shape[0]
    nb = t // BLOCK
    cw, nc, ub, vb = _sgu_geometry(off_u, ws)
    gpc = cw // LANES
    ng = ws // LANES

    def body(u_ref, v_ref, g_ref, b_ref, w_ref, bt_ref, do_ref, du_ref, dv_ref, dg_ref, db_ref, dw_ref, dbs_ref,
             bacc_ref):
        jc = pl.program_id(0)
        i = pl.program_id(1)
        row = lax.broadcasted_iota(jnp.int32, (BLOCK, BLOCK), 0)
        col = lax.broadcasted_iota(jnp.int32, (BLOCK, BLOCK), 1)
        lane_g = _lane((BLOCK, ng))
        tri = row >= col

        @pl.when(i == 0)
        def _():
            dg_ref[...] = jnp.zeros_like(dg_ref)
            db_ref[...] = jnp.zeros_like(db_ref)
            dw_ref[...] = jnp.zeros_like(dw_ref)
            bacc_ref[...] = jnp.zeros_like(bacc_ref)

        for gi in range(gpc):
            sl = slice(LANES * gi, LANES * gi + LANES)
            u_raw, v_raw = u_ref[:, sl], v_ref[:, sl]
            u_act, u_slope = _gelu_and_grad(u_raw)
            v_act, v_slope = _gelu_and_grad(v_raw)
            xh, rstd = _group_ln(v_act)
            gam = g_ref[:, sl]
            vn = (xh * gam + b_ref[:, sl]).astype(BF16)
            w = jnp.where(tri, w_ref[gi], 0.0)
            bias = jnp.sum(jnp.where(lane_g == jc * gpc + gi, bt_ref[...], 0.0), axis=1, keepdims=True)
            s = _dot(w.astype(BF16), vn, NN) + bias
            dov = do_ref[:, sl]
            du_ref[:, sl] = (dov * s * u_slope).astype(du_ref.dtype)
            ds = dov * u_act
            ds16 = ds.astype(BF16)
            dw_ref[gi] += jnp.where(tri, _dot(ds16, vn, NT), 0.0)
            bacc_ref[gi] += ds
            dvn = _dot(w.T.astype(BF16), ds16, NN)
            dg_ref[:, sl] += jnp.broadcast_to(jnp.sum(dvn * xh, axis=0, keepdims=True), (8, LANES))
            db_ref[:, sl] += jnp.broadcast_to(jnp.sum(dvn, axis=0, keepdims=True), (8, LANES))
            dxh = dvn * gam
            dvg = rstd * (dxh - jnp.mean(dxh, axis=1, keepdims=True)
                          - xh * jnp.mean(dxh * xh, axis=1, keepdims=True))
            dv_ref[:, sl] = (dvg * v_slope).astype(dv_ref.dtype)

        @pl.when(i == nb - 1)
        def _():
            for gi in range(gpc):
                dbs_ref[gi] = jnp.broadcast_to(jnp.sum(bacc_ref[gi].T, axis=0, keepdims=True), (8, LANES))

    blk = lambda base: pl.BlockSpec((BLOCK, cw), lambda jc, i: (i, base + jc))
    vec = pl.BlockSpec((1, cw), lambda jc, i: (0, jc))
    acc = pl.BlockSpec((8, cw), lambda jc, i: (0, jc))
    wsp = pl.BlockSpec((gpc, BLOCK, BLOCK), lambda jc, i: (jc, 0, 0))
    return pl.pallas_call(
        body,
        name=name,
        grid=(nc, nb),
        in_specs=[blk(ub), blk(vb), vec, vec, wsp, pl.BlockSpec((BLOCK, ng), lambda jc, i: (0, 0)), blk(0)],
        out_specs=[blk(0), blk(0), acc, acc, wsp, pl.BlockSpec((gpc, 8, LANES), lambda jc, i: (jc, 0, 0))],
        out_shape=[jax.ShapeDtypeStruct((t, ws), BF16), jax.ShapeDtypeStruct((t, ws), BF16),
                   jax.ShapeDtypeStruct((8, ws), F32), jax.ShapeDtypeStruct((8, ws), F32),
                   jax.ShapeDtypeStruct((ng, BLOCK, BLOCK), F32), jax.ShapeDtypeStruct((ng, 8, LANES), F32)],
        scratch_shapes=[pltpu.VMEM((gpc, BLOCK, BLOCK), F32)],
        compiler_params=_params("arbitrary", "arbitrary"),
    )(proj, proj, ln_g, ln_b, w_s, bt, dout)


def _sigmoid(x):
    return 1.0 / (1.0 + jnp.exp(-x))


def _merge_geometry(off_g, d):
    cw = math.gcd(off_g, d)
    return cw, d // cw, off_g // cw, (off_g + d) // cw


def _branches_fwd(attn, sgu, wab_t, wsb_t, proj, off_g, name):
    t = attn.shape[0]
    d = wab_t.shape[0]
    tn, _, ab, bb = _merge_geometry(off_g, d)
    tm = _divisor_tile(t, 1024, 128)

    def body(a1_ref, a2_ref, b1_ref, b2_ref, la_ref, lb_ref, bra_ref, brb_ref, o_ref):
        for rows in _row_chunks(tm):
            va = _dot(a1_ref[rows, :], b1_ref[...], NT)
            vb = _dot(a2_ref[rows, :], b2_ref[...], NT)
            bra_ref[rows, :] = va
            brb_ref[rows, :] = vb
            o_ref[rows, :] = (_sigmoid(la_ref[rows, :]) * va + _sigmoid(lb_ref[rows, :]) * vb).astype(o_ref.dtype)

    rows = lambda w: pl.BlockSpec((tm, w), lambda i, j: (i, 0))
    wrow = lambda w: pl.BlockSpec((tn, w), lambda i, j: (j, 0))
    blk = lambda base: pl.BlockSpec((tm, tn), lambda i, j: (i, base + j))
    return pl.pallas_call(
        body,
        name=name,
        grid=(t // tm, d // tn),
        in_specs=[rows(attn.shape[1]), rows(sgu.shape[1]), wrow(wab_t.shape[1]), wrow(wsb_t.shape[1]), blk(ab),
                  blk(bb)],
        out_specs=[blk(0)] * 3,
        out_shape=[jax.ShapeDtypeStruct((t, d), F32), jax.ShapeDtypeStruct((t, d), F32),
                   jax.ShapeDtypeStruct((t, d), BF16)],
        compiler_params=_params("parallel", "parallel"),
    )(attn, sgu, wab_t, wsb_t, proj, proj)


def _branches_bwd(dx16, wo, br_a, br_b, proj, off_g, name, after=None):
    t, d = br_a.shape
    tn, _, ab, bb = _merge_geometry(off_g, d)
    tm = _divisor_tile(t, 1024, 128)
    k = dx16.shape[1]

    def body(a_ref, b_ref, bra_ref, brb_ref, la_ref, lb_ref, *rest):
        da_ref, db_ref, dla_ref, dlb_ref = rest[-4:]
        for rows in _row_chunks(tm):
            dmv = _dot(a_ref[rows, :], b_ref[...], NT)
            ga, gb = _sigmoid(la_ref[rows, :]), _sigmoid(lb_ref[rows, :])
            da_ref[rows, :] = (dmv * ga).astype(da_ref.dtype)
            db_ref[rows, :] = (dmv * gb).astype(db_ref.dtype)
            dla_ref[rows, :] = (dmv * bra_ref[rows, :] * ga * (1.0 - ga)).astype(dla_ref.dtype)
            dlb_ref[rows, :] = (dmv * brb_ref[rows, :] * gb * (1.0 - gb)).astype(dlb_ref.dtype)

    blk = lambda base: pl.BlockSpec((tm, tn), lambda i, j: (i, base + j))
    return pl.pallas_call(
        body,
        name=name,
        grid=(t // tm, d // tn),
        in_specs=[pl.BlockSpec((tm, k), lambda i, j: (i, 0)), pl.BlockSpec((tn, k), lambda i, j: (j, 0)), blk(0),
                  blk(0), blk(ab), blk(bb)] + ([ANY] if after is not None else []),
        out_specs=[blk(0)] * 4,
        out_shape=[jax.ShapeDtypeStruct((t, d), BF16)] * 4,
        compiler_params=_params("parallel", "parallel"),
    )(dx16, wo, br_a, br_b, proj, proj, *(() if after is None else (after,)))


def _gate_up_fwd(h2, wgu_t, name, after=None):
    t, d = h2.shape
    f = wgu_t.shape[0] // 2
    tm = _divisor_tile(t, 1024, 128)
    tn = _divisor_tile(f, 512, 128)
    nb = f // tn

    def body(a_ref, bg_ref, bu_ref, *rest):
        gu_ref, act_ref = rest[-2:]
        for rows in _row_chunks(tm):
            av = a_ref[rows, :]
            gv = _dot(av, bg_ref[...], NT)
            uv = _dot(av, bu_ref[...], NT)
            gu_ref[0, rows, :] = gv
            gu_ref[1, rows, :] = uv
            act_ref[rows, :] = (gv * _sigmoid(gv) * uv).astype(act_ref.dtype)

    return pl.pallas_call(
        body,
        name=name,
        grid=(t // tm, nb),
        in_specs=[pl.BlockSpec((tm, d), lambda i, j: (i, 0)), pl.BlockSpec((tn, d), lambda i, j: (j, 0)),
                  pl.BlockSpec((tn, d), lambda i, j: (j + nb, 0))] + ([ANY] if after is not None else []),
        out_specs=[pl.BlockSpec((2, tm, tn), lambda i, j: (0, i, j)), pl.BlockSpec((tm, tn), lambda i, j: (i, j))],
        out_shape=[jax.ShapeDtypeStruct((2, t, f), F32), jax.ShapeDtypeStruct((t, f), BF16)],
        compiler_params=_params("parallel", "parallel"),
    )(h2, wgu_t, wgu_t, *(() if after is None else (after,)))


def _gate_up_bwd(dx16, wd, gu, name, after=None):
    t, d = dx16.shape
    f = wd.shape[0]
    tm = _divisor_tile(t, 1024, 128)
    tn = _divisor_tile(f, 512, 128)

    def body(a_ref, b_ref, gu_ref, *rest):
        o_ref = rest[-1]
        for rows in _row_chunks(tm):
            dav = _dot(a_ref[rows, :], b_ref[...], NT)
            gv = gu_ref[0, rows, :]
            sg = _sigmoid(gv)
            o_ref[0, rows, :] = (dav * gu_ref[1, rows, :] * (sg + gv * sg * (1.0 - sg))).astype(o_ref.dtype)
            o_ref[1, rows, :] = (dav * gv * sg).astype(o_ref.dtype)

    pair = pl.BlockSpec((2, tm, tn), lambda i, j: (0, i, j))
    return pl.pallas_call(
        body,
        name=name,
        grid=(t // tm, f // tn),
        in_specs=[pl.BlockSpec((tm, d), lambda i, j: (i, 0)), pl.BlockSpec((tn, d), lambda i, j: (j, 0)), pair]
        + ([ANY] if after is not None else []),
        out_specs=pair,
        out_shape=jax.ShapeDtypeStruct((2, t, f), BF16),
        compiler_params=_params("parallel", "parallel"),
    )(dx16, wd, gu, *(() if after is None else (after,)))


def _loss_and_grad(y, target, name):
    t, d = y.shape
    tr = _divisor_tile(t, 512, 8)

    def body(y_ref, t_ref, l_ref, dy_ref, dy16_ref):
        i = pl.program_id(0)
        err = y_ref[...] - t_ref[...]
        dy_ref[...] = err * (1.0 / d)
        dy16_ref[...] = (err * (1.0 / d)).astype(dy16_ref.dtype)
        part = jnp.broadcast_to(0.5 * jnp.sum(err * err) * (1.0 / d), l_ref.shape)

        @pl.when(i == 0)
        def _():
            l_ref[...] = part

        @pl.when(i > 0)
        def _():
            l_ref[...] += part

    row = pl.BlockSpec((tr, d), lambda i: (i, 0))
    return pl.pallas_call(
        body,
        name=name,
        grid=(t // tr,),
        in_specs=[row, row],
        out_specs=[pl.BlockSpec((8, LANES), lambda i: (0, 0)), row, row],
        out_shape=[jax.ShapeDtypeStruct((8, LANES), F32), jax.ShapeDtypeStruct((t, d), F32),
                   jax.ShapeDtypeStruct((t, d), BF16)],
        compiler_params=_params("arbitrary"),
    )(y, target)


def _adam_math(w, g, m, v):
    m = ADAM_B1 * m + (1.0 - ADAM_B1) * g
    v = ADAM_B2 * v + (1.0 - ADAM_B2) * (g * g)
    m_hat = m / (1.0 - ADAM_B1 ** ADAM_STEP)
    v_hat = v / (1.0 - ADAM_B2 ** ADAM_STEP)
    delta = -ADAM_LR * (m_hat / (jnp.sqrt(v_hat) + ADAM_EPS) + ADAM_WD * w)
    return delta, m, v


def _row_tile(r, c, elems=512 * 1024):
    return _divisor_tile(r, max(8, elems // c // 8 * 8), 8)


def _adam(w, grads, m, v, chip, name, after=None):
    nl, r, c = w.shape
    tr = _row_tile(r, c, 384 * 1024)
    nb = r // tr
    counts = [len(terms) for terms, _ in grads]

    def body(chip_ref, *refs):
        w_ref, m_ref, v_ref = refs[:3]
        g_ref, d_ref, nm_ref, nv_ref = refs[-4:]
        layer = pl.program_id(0)
        g, at = None, 3
        for li, n in enumerate(counts):
            total = refs[at][...].astype(F32)
            for ref in refs[at + 1:at + n]:
                total = total + ref[...].astype(F32)
            g = total if g is None else jnp.where(layer == li, total, g)
            at += n
        g_ref[...] = g
        d_ref[...], nm_ref[...], nv_ref[...] = _adam_math(w_ref[...], g, m_ref[...], v_ref[...])

    def term_spec(li, p, by_owner):
        def index(l, i, chip_ref):
            rows = jnp.where(l < li, 0, jnp.where(l > li, nb - 1, i))
            return (p, chip_ref[0] if by_owner else 0, rows, 0)
        return pl.BlockSpec((None, None, tr, c), index)

    row = pl.BlockSpec((None, tr, c), lambda l, i, chip_ref: (l, i, 0))
    specs, arrays = [], []
    for li, (terms, p) in enumerate(grads):
        for term in terms:
            specs.append(term_spec(li, p, term.shape[1] == 4))
            arrays.append(term)
    return pl.pallas_call(
        body,
        name=name,
        grid_spec=pltpu.PrefetchScalarGridSpec(
            num_scalar_prefetch=1, grid=(nl, nb),
            in_specs=[row] * 3 + specs + ([ANY] if after is not None else []), out_specs=[row] * 4),
        out_shape=[jax.ShapeDtypeStruct((nl, r, c), F32)] * 4,
        compiler_params=_params("arbitrary", "arbitrary"),
    )(chip, w, m, v, *arrays, *(() if after is None else (after,)))


def _place_shard(parts, layer, dev, out_dtype, name, after=None):
    p = len(parts)
    _, r, c = parts[0].shape
    tr = _row_tile(r, c)

    def body(dev_ref, *refs):
        o_ref = refs[-1]
        x = refs[0][...]
        for pi in range(1, p):
            x = jnp.where(pl.program_id(0) == pi, refs[pi][...], x)
        o_ref[...] = x.astype(o_ref.dtype)

    return pl.pallas_call(
        body,
        name=name,
        grid_spec=pltpu.PrefetchScalarGridSpec(
            num_scalar_prefetch=1,
            grid=(p, r // tr),
            in_specs=[pl.BlockSpec((None, tr, c), lambda pi, i, dev_ref: (layer, i, 0))] * p
            + ([ANY] if after is not None else []),
            out_specs=pl.BlockSpec((None, None, tr, c), lambda pi, i, dev_ref: (pi, dev_ref[0], i, 0)),
        ),
        out_shape=jax.ShapeDtypeStruct((p, N_DEV, r, c), out_dtype),
        compiler_params=_params("parallel", "parallel"),
    )(dev, *parts, *(() if after is None else (after,)))


def _sum_sibling(g, land, core, name):
    p, _, _, r, c = g.shape
    tr = _row_tile(r, c, 1024 * 1024)

    def body(core_ref, g_ref, l_ref, o_ref):
        o_ref[...] = (g_ref[...].astype(F32) + l_ref[...].astype(F32)).astype(o_ref.dtype)

    return pl.pallas_call(
        body,
        name=name,
        grid_spec=pltpu.PrefetchScalarGridSpec(
            num_scalar_prefetch=1,
            grid=(p, 4, r // tr),
            in_specs=[pl.BlockSpec((None, None, None, tr, c), lambda pi, q, i, core_ref: (pi, q, core_ref[0], i, 0)),
                      pl.BlockSpec((None, None, None, tr, c), lambda pi, q, i, core_ref: (pi, q, 0, i, 0))],
            out_specs=pl.BlockSpec((None, None, tr, c), lambda pi, q, i, core_ref: (pi, q, i, 0)),
        ),
        out_shape=jax.ShapeDtypeStruct((p, 4, r, c), BF16),
        compiler_params=_params("parallel", "parallel", "parallel"),
    )(core, g, land)


def _sum_chips(s, lands, chip, name):
    p, _, r, c = s.shape
    tr = _row_tile(r, c)

    def body(chip_ref, s_ref, l0_ref, l1_ref, l2_ref, o_ref):
        total = s_ref[...].astype(F32) + l0_ref[...].astype(F32)
        o_ref[...] = total + l1_ref[...].astype(F32) + l2_ref[...].astype(F32)

    land_spec = pl.BlockSpec((None, None, tr, c), lambda pi, i, chip_ref: (pi, 0, i, 0))
    return pl.pallas_call(
        body,
        name=name,
        grid_spec=pltpu.PrefetchScalarGridSpec(
            num_scalar_prefetch=1,
            grid=(p, r // tr),
            in_specs=[pl.BlockSpec((None, None, tr, c), lambda pi, i, chip_ref: (pi, chip_ref[0], i, 0)),
                      land_spec, land_spec, land_spec],
            out_specs=pl.BlockSpec((None, tr, c), lambda pi, i, chip_ref: (pi, i, 0)),
        ),
        out_shape=jax.ShapeDtypeStruct((p, r, c), F32),
        compiler_params=_params("parallel", "parallel"),
    )(chip, s, *lands)


def _small_reduce_adam(gathered, w, m, v, name):
    _, r, c = gathered.shape
    tr = _row_tile(r, c)

    def body(p_ref, w_ref, m_ref, v_ref, g_ref, d_ref, nm_ref, nv_ref):
        g = p_ref[0]
        for j in range(1, N_DEV):
            g = g + p_ref[j]
        g_ref[...] = g
        d_ref[...], nm_ref[...], nv_ref[...] = _adam_math(w_ref[...], g, m_ref[...], v_ref[...])

    row = pl.BlockSpec((tr, c), lambda i: (i, 0))
    return pl.pallas_call(
        body,
        name=name,
        grid=(r // tr,),
        in_specs=[pl.BlockSpec((N_DEV, tr, c), lambda i: (0, i, 0)), row, row, row],
        out_specs=[row] * 4,
        out_shape=[jax.ShapeDtypeStruct((r, c), F32)] * 4,
        compiler_params=_params("parallel"),
    )(gathered, w, m, v)


def _place():
    return lax.axis_index("x"), lax.axis_index("y"), lax.axis_index("c")


HBM =pl.BlockSpec(memory_space=pltpu.HBM)
SEM = pl.BlockSpec(memory_space=pltpu.SEMAPHORE)
TOKEN = pl.BlockSpec(memory_space=pltpu.VMEM)
EFFECT = pltpu.SideEffectType.DATAFLOW_SIDE_EFFECTING


def _in_hbm(a):
    return pltpu.with_memory_space_constraint(a, pltpu.HBM)


_FLIPS = {"me": (0, 0, 0), "s": (0, 0, 1), "x": (1, 0, 0), "y": (0, 1, 0), "d": (1, 1, 0)}
GATHER_STAGES = (
    (("s", "me", "all"), ("x", "me", "all"), ("y", "me", "all")),
    (("s", "x", "all"), ("s", "y", "all"), ("y", "x", "first"), ("x", "y", "second")),
    (("s", "d", "all"),),
)


def _flipped(place, *names):
    out = list(place)
    for name in names:
        out = [1 - p if f else p for p, f in zip(out, _FLIPS[name])]
    return tuple(out)


def _block_part(ref, place, part):
    px, py, pc = place
    rows = ref.shape[2]
    span = {"all": pl.ds(0, rows), "first": pl.ds(0, rows // 2), "second": pl.ds(rows // 2, rows // 2)}[part]
    return ref.at[:, pl.ds(4 * px + 2 * py + pc, 1), span]


def _split_start(bufs, moves, name, after=None):
    n, nm = len(bufs), len(moves)
    extra = 0 if after is None else 1

    def body(*refs):
        ssem, rsem = refs[n + extra], refs[n + extra + 1]
        outs, token = refs[n + extra + 2:2 * n + extra + 2], refs[2 * n + extra + 2]
        me = _place()
        for a in range(n):
            for k, (to, owner, part) in enumerate(moves):
                piece = _block_part(outs[a], _flipped(me, owner), part)
                pltpu.make_async_remote_copy(
                    src_ref=piece, dst_ref=piece, send_sem=ssem.at[nm * a + k], recv_sem=rsem.at[nm * a + k],
                    device_id=_flipped(me, to), device_id_type=MESH).start()
        token[...] = jnp.zeros_like(token)

    outs = pl.pallas_call(
        body,
        name=name,
        in_specs=[HBM] * n + [ANY] * extra,
        out_specs=[SEM, SEM] + [HBM] * n + [TOKEN],
        out_shape=[pltpu.SemaphoreType.DMA((nm * n,))] * 2 + [pltpu.HBM(b.shape, b.dtype) for b in bufs]
        + [jax.ShapeDtypeStruct((8, LANES), F32)],
        input_output_aliases={i: 2 + i for i in range(n)},
        compiler_params=pltpu.CompilerParams(has_side_effects=EFFECT),
    )(*[_in_hbm(b) for b in bufs], *(() if after is None else (after,)))
    return outs[0], outs[1], list(outs[2:2 + n]), outs[-1]


def _split_wait(send_sems, recv_sems, bufs, moves, after, name):
    n, nm = len(bufs), len(moves)

    def body(*refs):
        ins, ssem, rsem = refs[:n], refs[n], refs[n + 1]
        me = _place()
        for a in range(n):
            for k, (to, owner, part) in enumerate(moves):
                landed = _block_part(ins[a], _flipped(me, owner, to), part)
                cp = pltpu.make_async_remote_copy(
                    src_ref=landed, dst_ref=landed, send_sem=ssem.at[nm * a + k], recv_sem=rsem.at[nm * a + k],
                    device_id=_flipped(me, to), device_id_type=MESH)
                cp.wait_send()
                cp.wait_recv()

    return pl.pallas_call(
        body,
        name=name,
        in_specs=[HBM] * n + [SEM, SEM, ANY],
        out_specs=[HBM] * n,
        out_shape=[pltpu.HBM(b.shape, b.dtype) for b in bufs],
        input_output_aliases={i: i for i in range(n)},
        compiler_params=pltpu.CompilerParams(has_side_effects=EFFECT),
    )(*bufs, send_sems, recv_sems, after)


def _chips_start(sums, name, after=None):
    n = len(sums)
    extra = 0 if after is None else 1

    def body(*refs):
        refs = refs[:4 * n] + refs[4 * n + extra:]
        ssem, rsem = refs[4 * n], refs[4 * n + 1]
        src, land = refs[4 * n + 2:5 * n + 2], refs[5 * n + 2:8 * n + 2]
        token = refs[8 * n + 2]
        x, y, c = _place()
        chips = [(1 - x, y), (x, 1 - y), (1 - x, 1 - y)]
        for a in range(n):
            for k, (px, py) in enumerate(chips):
                pltpu.make_async_remote_copy(
                    src_ref=src[a].at[:, pl.ds(2 * px + py, 1)], dst_ref=land[3 * a + k], send_sem=ssem.at[3 * a + k],
                    recv_sem=rsem.at[3 * a + k], device_id=(px, py, c), device_id_type=MESH).start()
        token[...] = jnp.zeros_like(token)

    lands = []
    for s in sums:
        lands += [lax.empty((s.shape[0], 1) + s.shape[2:], s.dtype) for _ in range(3)]
    outs = pl.pallas_call(
        body,
        name=name,
        in_specs=[HBM] * (4 * n) + [ANY] * extra,
        out_specs=[SEM, SEM] + [HBM] * (4 * n) + [TOKEN],
        out_shape=[pltpu.SemaphoreType.DMA((3 * n,))] * 2 + [pltpu.HBM(b.shape, b.dtype) for b in list(sums) + lands]
        + [jax.ShapeDtypeStruct((8, LANES), F32)],
        input_output_aliases={i: 2 + i for i in range(4 * n)},
        compiler_params=pltpu.CompilerParams(has_side_effects=EFFECT),
    )(*[_in_hbm(b) for b in list(sums) + lands], *(() if after is None else (after,)))
    return outs[0], outs[1], list(outs[2:2 + n]), list(outs[2 + n:2 + 4 * n]), outs[-1]


def _chips_wait(send_sems, recv_sems, sums, lands, after, name):
    n = len(sums)

    def body(*refs):
        src, land = refs[:n], refs[n:4 * n]
        ssem, rsem = refs[4 * n], refs[4 * n + 1]
        x, y, c = _place()
        chips = [(1 - x, y), (x, 1 - y), (1 - x, 1 - y)]
        for a in range(n):
            for k, (px, py) in enumerate(chips):
                cp = pltpu.make_async_remote_copy(
                    src_ref=src[a].at[:, pl.ds(2 * px + py, 1)], dst_ref=land[3 * a + k], send_sem=ssem.at[3 * a + k],
                    recv_sem=rsem.at[3 * a + k], device_id=(px, py, c), device_id_type=MESH)
                cp.wait_send()
                cp.wait_recv()

    both = list(sums) + list(lands)
    outs = pl.pallas_call(
        body,
        name=name,
        in_specs=[HBM] * (4 * n) + [SEM, SEM, ANY],
        out_specs=[HBM] * (4 * n),
        out_shape=[pltpu.HBM(b.shape, b.dtype) for b in both],
        input_output_aliases={i: i for i in range(4 * n)},
        compiler_params=pltpu.CompilerParams(has_side_effects=EFFECT),
    )(*both, send_sems, recv_sems, after)
    return list(outs[:n]), [list(outs[n + 3 * a:n + 3 * a + 3]) for a in range(n)]


def _sibling_start(grads, name):
    n = len(grads)

    def body(*refs):
        ssem, rsem = refs[2 * n], refs[2 * n + 1]
        src, land = refs[2 * n + 2:3 * n + 2], refs[3 * n + 2:4 * n + 2]
        token = refs[4 * n + 2]
        x, y, c = _place()
        for a in range(n):
            pltpu.make_async_remote_copy(
                src_ref=src[a].at[:, :, pl.ds(1 - c, 1)], dst_ref=land[a], send_sem=ssem.at[a], recv_sem=rsem.at[a],
                device_id=(x, y, 1 - c), device_id_type=MESH).start()
        token[...] = jnp.zeros_like(token)

    lands = [lax.empty(g.shape[:2] + (1,) + g.shape[3:], g.dtype) for g in grads]
    both = list(grads) + lands
    outs = pl.pallas_call(
        body,
        name=name,
        in_specs=[HBM] * (2 * n),
        out_specs=[SEM, SEM] + [HBM] * (2 * n) + [TOKEN],
        out_shape=[pltpu.SemaphoreType.DMA((n,))] * 2 + [pltpu.HBM(b.shape, b.dtype) for b in both]
        + [jax.ShapeDtypeStruct((8, LANES), F32)],
        input_output_aliases={i: 2 + i for i in range(2 * n)},
        compiler_params=pltpu.CompilerParams(has_side_effects=EFFECT),
    )(*[_in_hbm(b) for b in both])
    return outs[0], outs[1], list(outs[2:2 + n]), list(outs[2 + n:2 + 2 * n]), outs[-1]


def _sibling_wait(send_sems, recv_sems, grads, lands, after, name):
    n = len(grads)

    def body(*refs):
        src, land = refs[:n], refs[n:2 * n]
        ssem, rsem = refs[2 * n], refs[2 * n + 1]
        x, y, c = _place()
        for a in range(n):
            cp = pltpu.make_async_remote_copy(
                src_ref=src[a].at[:, :, pl.ds(1 - c, 1)], dst_ref=land[a], send_sem=ssem.at[a], recv_sem=rsem.at[a],
                device_id=(x, y, 1 - c), device_id_type=MESH)
            cp.wait_send()
            cp.wait_recv()

    both = list(grads) + list(lands)
    outs = pl.pallas_call(
        body,
        name=name,
        in_specs=[HBM] * (2 * n) + [SEM, SEM, ANY],
        out_specs=[HBM] * (2 * n),
        out_shape=[pltpu.HBM(b.shape, b.dtype) for b in both],
        input_output_aliases={i: i for i in range(2 * n)},
        compiler_params=pltpu.CompilerParams(has_side_effects=EFFECT),
    )(*both, send_sems, recv_sems, after)
    return list(outs[:n]), list(outs[n:])


_SMALL = ("mix_norm", "q_norm", "k_norm", "sinks", "sgu_ln_g", "sgu_ln_b", "w_spatial", "b_spatial", "ffn_norm")


def _pack_rows(a):
    flat = a.reshape(-1)
    pad = (-flat.shape[0]) % LANES
    if pad:
        flat = jnp.pad(flat, (0, pad))
    return flat.reshape(-1, LANES)


def _pack(values):
    rows = jnp.concatenate([_pack_rows(values[k]) for k in _SMALL], axis=0)
    pad = (-rows.shape[0]) % 8
    if pad:
        rows = jnp.pad(rows, ((0, pad), (0, 0)))
    return rows


def _unpack(rows, like):
    out, at = {}, 0
    for k in _SMALL:
        size = like[k].size
        nrows = -(-size // LANES)
        out[k] = rows[at:at + nrows].reshape(-1)[:size].reshape(like[k].shape)
        at += nrows
    return out


def _rope_tables(t, wq, wk):
    pos = jnp.arange(t, dtype=F32)
    inv_freq = jnp.power(ROPE_THETA, -jnp.arange(0, HEAD_DIM, 2, dtype=F32) / HEAD_DIM)
    ang = pos[:, None] * inv_freq[None, :]
    cos, sin = jnp.cos(ang), jnp.sin(ang)
    cos2, sin2 = jnp.concatenate([cos, cos], axis=1), jnp.concatenate([-sin, sin], axis=1)
    return (jnp.tile(cos2, (1, wq // HEAD_DIM)), jnp.tile(sin2, (1, wq // HEAD_DIM)),
            jnp.tile(cos2, (1, wk // HEAD_DIM)), jnp.tile(sin2, (1, wk // HEAD_DIM)))


def kernel(x, mix_norm, w_in, q_norm, k_norm, sinks, sgu_ln_g, sgu_ln_b, w_spatial, b_spatial, w_attn_branch, w_sgu_branch, w_out, ffn_norm, w_gate, w_up, w_down, loss_target, m_mix_norm, m_w_in, m_q_norm, m_k_norm, m_sinks, m_sgu_ln_g, m_sgu_ln_b, m_w_spatial, m_b_spatial, m_w_attn_branch, m_w_sgu_branch, m_w_out, m_ffn_norm, m_w_gate, m_w_up, m_w_down, v_mix_norm, v_w_in, v_q_norm, v_k_norm, v_sinks, v_sgu_ln_g, v_sgu_ln_b, v_w_spatial, v_b_spatial, v_w_attn_branch, v_w_sgu_branch, v_w_out, v_ffn_norm, v_w_gate, v_w_up, v_w_down):
    names = ("mix_norm", "w_in", "q_norm", "k_norm", "sinks", "sgu_ln_g", "sgu_ln_b", "w_spatial", "b_spatial",
             "w_attn_branch", "w_sgu_branch", "w_out", "ffn_norm", "w_gate", "w_up", "w_down")
    weights = dict(zip(names, (mix_norm, w_in, q_norm, k_norm, sinks, sgu_ln_g, sgu_ln_b, w_spatial, b_spatial,
                               w_attn_branch, w_sgu_branch, w_out, ffn_norm, w_gate, w_up, w_down)))
    mom1 = dict(zip(names, (m_mix_norm, m_w_in, m_q_norm, m_k_norm, m_sinks, m_sgu_ln_g, m_sgu_ln_b, m_w_spatial,
                            m_b_spatial, m_w_attn_branch, m_w_sgu_branch, m_w_out, m_ffn_norm, m_w_gate, m_w_up,
                            m_w_down)))
    mom2 = dict(zip(names, (v_mix_norm, v_w_in, v_q_norm, v_k_norm, v_sinks, v_sgu_ln_g, v_sgu_ln_b, v_w_spatial,
                            v_b_spatial, v_w_attn_branch, v_w_sgu_branch, v_w_out, v_ffn_norm, v_w_gate, v_w_up,
                            v_w_down)))
    depth = w_in.shape[0]
    _, t, d = x.shape
    n_q_heads = sinks.shape[1]
    wq = n_q_heads * HEAD_DIM
    wk = wq // Q_PER_KV
    ws = sgu_ln_g.shape[1]
    ng = ws // LANES
    off_u = wq + 2 * wk
    off_g = off_u + 2 * ws
    tables = _rope_tables(t, wq, wk)
    px, py, pc = _place()
    core = pc.astype(jnp.int32)[None]
    chip = (2 * px + py).astype(jnp.int32)[None]
    dev = (4 * px + 2 * py + pc).astype(jnp.int32)[None]

    layers = range(depth)
    chunks = ((0,), (1, 2, 3), (4,), (5,))
    sources = [[jnp.swapaxes(w_in, 1, 2)], [jnp.swapaxes(w_attn_branch, 1, 2)], [jnp.swapaxes(w_sgu_branch, 1, 2)],
               [w_out], [jnp.swapaxes(w_gate, 1, 2), jnp.swapaxes(w_up, 1, 2)], [w_down]]
    stream = [(l, ci) for l in layers for ci in range(len(chunks))]
    placed, state, token = {}, {}, None

    def send(key, after):
        state[key] = _split_start(placed[key], GATHER_STAGES[0], "gather_send_%d_%d" % key, after)
        return state[key][3]

    def advance(key, after, stage):
        send_sems, recv_sems, bufs, _ = state[key]
        bufs = _split_wait(send_sems, recv_sems, bufs, GATHER_STAGES[stage - 1], after, "gather_wait%d_%d_%d" % (stage, *key))
        state[key] = _split_start(bufs, GATHER_STAGES[stage], "gather_pass%d_%d_%d" % (stage, *key))
        return state[key][3]

    def relay(key, after):
        tok = advance(key, after, 1)
        at = stream.index(key)
        for later in stream[at + 2:at + 3] if at else stream[1:3]:
            tok = send(later, tok)
        return tok

    def ready(key, after):
        send_sems, recv_sems, bufs, _ = state.pop(key)
        bufs = _split_wait(send_sems, recv_sems, bufs, GATHER_STAGES[2], after, "gather_wait3_%d_%d" % key)
        return [f.reshape(f.shape[0] * f.shape[1] * f.shape[2], f.shape[3]) for f in bufs]

    for key in stream:
        l, ci = key
        placed[key] = [_place_shard(sources[a], l, dev, BF16, f"place_shard_{l}_{a}",
                                    after=token if a == chunks[ci][0] else None) for a in chunks[ci]]
        token = send(key, None) if key == stream[0] else placed[key][-1]

    saved = []
    xl = x[0]
    going = relay((0, 0), token)
    going = advance((0, 0), going, 2)
    for l in layers:
        gq = jnp.tile(q_norm[l], n_q_heads)[None]
        gk = jnp.tile(k_norm[l], n_q_heads // Q_PER_KV)[None]
        bt = b_spatial[l].T
        h = _rmsnorm_fwd(xl, mix_norm[l][None], f"mix_norm_fwd_{l}", after=going)
        (win_t,) = ready((l, 0), h)
        proj = _mm(h, win_t, "nt", F32, f"in_proj_{l}")
        going = relay((l, 1), proj)
        attn = _attn_fwd(proj, tables, gq, gk, sinks[l], wq, wk, f"attn_fwd_{l}", after=going)
        going = advance((l, 1), attn, 2)
        sgu = _sgu_fwd(proj, sgu_ln_g[l][None], sgu_ln_b[l][None], w_spatial[l], bt, off_u, ws, f"sgu_fwd_{l}",
                       after=going)
        wab_t, wsb_t, wo = ready((l, 1), sgu)
        br_a, br_b, merged = _branches_fwd(attn, sgu, wab_t, wsb_t, proj, off_g, f"branches_{l}")
        going = relay((l, 2), merged)
        x1 = _mm(merged, wo, "nn", F32, f"out_proj_{l}", residual=xl, after=going)
        going = advance((l, 2), x1, 2)
        h2 = _rmsnorm_fwd(x1, ffn_norm[l][None], f"ffn_norm_fwd_{l}", after=going)
        (wgu_t,) = ready((l, 2), h2)
        going = relay((l, 3), h2)
        gu, act = _gate_up_fwd(h2, wgu_t, f"gate_up_{l}", after=going)
        going = advance((l, 3), act, 2)
        if l + 1 < depth:
            going = relay((l + 1, 0), going)
        (wd,) = ready((l, 3), going)
        x2 = _mm(act, wd, "nn", F32, f"down_proj_{l}", residual=x1)
        if l + 1 < depth:
            going = advance((l + 1, 0), x2, 2)
        saved.append(dict(x0=xl, h=h, proj=proj, attn=attn, sgu=sgu, br_a=br_a, br_b=br_b, merged=merged, x1=x1,
                          h2=h2, gu=gu, act=act, gq=gq, gk=gk, bt=bt, win_t=win_t, wab_t=wab_t, wsb_t=wsb_t, wo=wo,
                          wgu_t=wgu_t, wd=wd))
        xl = x2

    loss_part, dx, dx16 = _loss_and_grad(xl, loss_target[0], "loss")
    loss = lax.psum(loss_part[0, 0], ("x", "y", "c"))

    def sibling_start(grads, tag):
        shaped = []
        for g, p in grads:
            rows, c = g.shape
            shaped.append(g.reshape(p, 4, 2, rows // (8 * p), c))
        send_sems, recv_sems, shaped, lands, tok = _sibling_start(shaped, f"rs_sibling_start_{tag}")
        return (send_sems, recv_sems, shaped, lands, tag), tok

    def chips_start(state, after, first=None):
        send_sems, recv_sems, shaped, lands, tag = state
        shaped, lands = _sibling_wait(send_sems, recv_sems, shaped, lands, after, f"rs_sibling_wait_{tag}")
        sums = [_sum_sibling(g, o, core, f"rs_add_sibling_{tag}_{a}") for a, (g, o) in enumerate(zip(shaped, lands))]
        gate = None if first is None else first(sums[0])
        send_sems, recv_sems, sums, lands, tok = _chips_start(sums, f"rs_chips_start_{tag}", after=gate)
        return (send_sems, recv_sems, sums, lands, tag), tok

    def scatter_finish(state, after):
        send_sems, recv_sems, sums, lands, tag = state
        sums, lands = _chips_wait(send_sems, recv_sems, sums, lands, after, f"rs_chips_wait_{tag}")
        return [[s] + o for s, o in zip(sums, lands)]

    in_flight = [dict() for _ in layers]
    small_grads = [None] * depth
    tok, swap_in = None, None
    for l in reversed(layers):
        s = saved[l]
        dgu = _gate_up_bwd(dx16, s["wd"], s["gu"], f"d_gate_up_{l}", after=tok)
        if swap_in is not None:
            in_flight[l + 1]["in"], tok = chips_start(swap_in, dgu)
        g_wd = _mm(s["act"], dx16, "tn", BF16, f"g_w_down_{l}", after=tok)
        swap, tok_s = sibling_start([(g_wd, 1)], f"{l}_down")
        dh2 = _mm(dgu, s["wgu_t"], "nn", F32, f"d_h2_{l}", after=tok_s)
        in_flight[l]["down"], tok = chips_start(swap, dh2)
        g_wgu_t = _mm(dgu, s["h2"], "tn", BF16, f"g_w_gate_up_{l}", after=tok)
        swap, tok_s = sibling_start([(g_wgu_t, 2)], f"{l}_gate_up")
        dx1, dx1_16, g_ffn = _rmsnorm_bwd(s["x1"], ffn_norm[l][None], dh2, dx, f"ffn_norm_bwd_{l}", after=tok_s)
        d_a, d_b, dla, dlb = _branches_bwd(dx1_16, s["wo"], s["br_a"], s["br_b"], s["proj"], off_g,
                                           f"d_branches_{l}")
        in_flight[l]["gate_up"], tok = chips_start(swap, d_a)
        g_wo = _mm(s["merged"], dx1_16, "tn", BF16, f"g_w_out_{l}", after=tok)
        dattn = _mm(d_a, s["wab_t"], "nn", F32, f"d_attn_{l}", after=g_wo)
        g_wab_t = _mm(d_a, s["attn"], "tn", BF16, f"g_w_attn_branch_{l}")
        dsgu = _mm(d_b, s["wsb_t"], "nn", F32, f"d_sgu_{l}")
        g_wsb_t = _mm(d_b, s["sgu"], "tn", BF16, f"g_w_sgu_branch_{l}")
        swap, tok_s = sibling_start([(g_wab_t, 1), (g_wsb_t, 1), (g_wo, 1)], f"{l}_mix")
        dq, dk, dv, g_gq, g_gk, g_sinks = _attn_bwd(s["proj"], dattn, tables, s["gq"], s["gk"], sinks[l], wq, wk,
                                                    f"attn_bwd_{l}", after=tok_s)
        du, dvv, g_lng, g_lnb, g_ws, g_bs = _sgu_bwd(s["proj"], dsgu, sgu_ln_g[l][None], sgu_ln_b[l][None],
                                                     w_spatial[l], s["bt"], off_u, ws, f"sgu_bwd_{l}")
        dproj = jnp.concatenate([dq, dk.astype(BF16), dv.astype(BF16), du, dvv, dla, dlb], axis=1)
        dh = _mm(dproj, s["win_t"], "nn", F32, f"d_h_{l}")
        in_flight[l]["mix"], tok = chips_start(swap, dh)
        g_win_t = _mm(dproj, s["h"], "tn", BF16, f"g_w_in_{l}", after=tok)
        swap_in, tok = sibling_start([(g_win_t, 1)], f"{l}_in")
        dx, dx16, g_mix = _rmsnorm_bwd(s["x0"], mix_norm[l][None], dh, dx1, f"mix_norm_bwd_{l}", after=tok)
        small_grads[l] = dict(
            mix_norm=g_mix[0], q_norm=g_gq[0].reshape(n_q_heads, HEAD_DIM).sum(0),
            k_norm=g_gk[0].reshape(n_q_heads // Q_PER_KV, HEAD_DIM).sum(0), sinks=g_sinks[0, :n_q_heads],
            sgu_ln_g=g_lng[0], sgu_ln_b=g_lnb[0], w_spatial=g_ws, b_spatial=g_bs[:, 0, :], ffn_norm=g_ffn[0])
    grad_x = dx[None]

    result = {key: {} for key in ("grad", "delta", "m", "v")}
    layer_like = {k: weights[k][0] for k in _SMALL}
    packed_g = jnp.concatenate([_pack(small_grads[l]) for l in layers], axis=0)
    rows_per_layer = packed_g.shape[0] // depth
    small_buf = _place_shard([packed_g[None]], 0, dev, F32, "place_small_grads", after=tok)
    send_sems, recv_sems, small_bufs, tok = _split_start([small_buf], GATHER_STAGES[0], "gather_send_small")
    small_state = [(send_sems, recv_sems, small_bufs)]

    def small_stage(stage, after):
        ssem, rsem, bufs = small_state[0]
        bufs = _split_wait(ssem, rsem, bufs, GATHER_STAGES[stage - 1], after, f"gather_wait{stage}_small")
        ssem, rsem, bufs, token = _split_start(bufs, GATHER_STAGES[stage], f"gather_pass{stage}_small")
        small_state[0] = (ssem, rsem, bufs)
        return token

    in_flight[0]["in"], tok = chips_start(swap_in, tok, first=functools.partial(small_stage, 1))

    def update(k, grads, transposed, after):
        view = (lambda a: jnp.swapaxes(a, 1, 2)) if transposed else (lambda a: a)
        outs = _adam(view(weights[k]), grads, view(mom1[k]), view(mom2[k]), chip, f"adam_{k}", after=after)
        for key, val in zip(("grad", "delta", "m", "v"), outs):
            result[key][k] = view(val)
        return outs[3]

    def plain(terms, tag):
        s, lands = terms[0], terms[1:]
        g = _sum_chips(s, lands, chip, f"rs_add_chips_{tag}")
        return [jnp.swapaxes(g, 1, 2)[:, None]]

    down = [scatter_finish(in_flight[l]["down"], tok) for l in reversed(layers)][::-1]
    tok = update("w_down", [(down[l][0], 0) for l in layers], False, None)
    tok = small_stage(2, tok)
    gate_up = [scatter_finish(in_flight[l]["gate_up"], tok) for l in reversed(layers)][::-1]
    tok = update("w_gate", [(gate_up[l][0], 0) for l in layers], True, None)
    tok = update("w_up", [(gate_up[l][0], 1) for l in layers], True, tok)
    mix =[scatter_finish(in_flight[l]["mix"], tok) for l in reversed(layers)][::-1]
    tok = update("w_out", [(mix[l][2], 0) for l in layers], False, None)
    tok = update("w_attn_branch", [(plain(mix[l][0], f"{l}_attn_branch"), 0) for l in layers], False, tok)
    tok = update("w_sgu_branch", [(plain(mix[l][1], f"{l}_sgu_branch"), 0) for l in layers], False, tok)

    packed = [jnp.concatenate([_pack({k: src[k][l] for k in _SMALL}) for l in layers], axis=0)
              for src in (weights, mom1, mom2)]
    (gathered_small,) = _split_wait(*small_state[0], GATHER_STAGES[2], tok, "gather_wait3_small")
    small = _small_reduce_adam(gathered_small[0], *packed, "small_reduce_adam")
    for key, rows in zip(("grad", "delta", "m", "v"), small):
        per_layer = [_unpack(rows[l * rows_per_layer:(l + 1) * rows_per_layer], layer_like) for l in layers]
        for k in _SMALL:
            result[key][k] = jnp.stack([per_layer[l][k] for l in layers])

    last = [scatter_finish(in_flight[l]["in"], small[0]) for l in reversed(layers)][::-1]
    update("w_in", [(last[l][0], 0) for l in layers], True, result["v"]["ffn_norm"])

    return (loss, grad_x, *[result["grad"][k] for k in names], *[result["delta"][k] for k in names],
            *[result["m"][k] for k in names], *[result["v"][k] for k in names])
```

```python
import functools
import math

import jax
import jax.numpy as jnp
from jax import lax
from jax.experimental import pallas as pl
from jax.experimental.pallas import tpu as pltpu

F32 = jnp.float32
BF16 = jnp.bfloat16
MESH = pl.DeviceIdType.MESH
ANY = pl.BlockSpec(memory_space=pl.ANY)

N_DEV = 8
HEAD_DIM = 64
Q_PER_KV = 4
BLOCK = 128
LANES = 128
ROPE_THETA = 10000.0
EPS = 1e-6
ADAM_LR = 0.001
ADAM_B1 = 0.9
ADAM_B2 = 0.999
ADAM_EPS = 1e-08
ADAM_WD = 0.01
ADAM_STEP = 10
NEG = -1e30
VMEM_LIMIT_BYTES = 56 * 1024 * 1024

NN = ((1,), (0,))
NT = ((1,), (1,))
TN = ((0,), (0,))


def _dot(a, b, dims):
    return lax.dot_general(a, b, (dims, ((), ())), preferred_element_type=F32)


def _params(*sem):
    return pltpu.CompilerParams(dimension_semantics=sem, vmem_limit_bytes=VMEM_LIMIT_BYTES)


def _divisor_tile(n, limit, unit):
    if n <= limit:
        return n
    best = unit
    for t in range(unit, limit + 1, unit):
        if n % t == 0:
            best = t
    assert n % best == 0, (n, limit, unit)
    return best


def _row_chunks(rows, size=256):
    size = min(size, rows)
    assert rows % size == 0, (rows, size)
    return [pl.ds(start, size) for start in range(0, rows, size)]


def _mm(a, b, mode, out_dtype, name, residual=None, after=None):
    parts = a.shape[0] if a.ndim == 3 else 1
    a2 = a.shape[-2:]
    if mode == "nn":
        (m, kp), (k2, n) = a2, b.shape
        k, mp = kp * parts, m
    elif mode == "nt":
        (m, kp), (n, k2) = a2, b.shape
        k, mp = kp * parts, m
    else:
        (k, mp), (k2, n) = a2, b.shape
        m, kp = mp * parts, k
    assert k == k2, (name, a.shape, b.shape)
    tk = _divisor_tile(kp, 2816, 128)
    nk = k // tk
    tm = _divisor_tile(mp, 512 if mode == "tn" else 1024, 128)
    tn = _divisor_tile(n, 2048 if mode == "tn" else 1024, 128)
    kpb, mpb = kp // tk, mp // tm
    dims = {"nn": NN, "nt": NT, "tn": TN}[mode]
    lead = (None,) if a.ndim == 3 else ()
    if mode == "tn":
        a_index = lambda i, j, kk: (i // mpb, kk, i % mpb) if lead else (kk, i)
        a_spec = pl.BlockSpec(lead + (tk, tm), a_index)
    else:
        a_index = lambda i, j, kk: (kk // kpb, i, kk % kpb) if lead else (i, kk)
        a_spec = pl.BlockSpec(lead + (tm, tk), a_index)
    if mode == "nt":
        b_spec = pl.BlockSpec((tn, tk), lambda i, j, kk: (j, kk))
    else:
        b_spec = pl.BlockSpec((tk, tn), lambda i, j, kk: (kk, j))
    o_spec = pl.BlockSpec((tm, tn), lambda i, j, kk: (i, j))
    has_res = residual is not None

    def body(*refs):
        a_ref, b_ref = refs[:2]
        r_ref = refs[2] if has_res else None
        o_ref, acc_ref = refs[-2:]
        kk = pl.program_id(2)
        p = _dot(a_ref[...], b_ref[...], dims)

        def finish(total):
            if has_res:
                total = total + r_ref[...]
            o_ref[...] = total.astype(o_ref.dtype)

        if nk == 1:
            finish(p)
        else:
            @pl.when(kk == 0)
            def _():
                acc_ref[...] = p

            @pl.when(jnp.logical_and(kk > 0, kk < nk - 1))
            def _():
                acc_ref[...] += p

            @pl.when(kk == nk - 1)
            def _():
                finish(acc_ref[...] + p)

    in_specs = [a_spec, b_spec] + ([o_spec] if has_res else []) + ([ANY] if after is not None else [])
    args = (a, b) + ((residual,) if has_res else ()) + ((after,) if after is not None else ())
    acc_shape = (tm, tn) if nk > 1 else (8, LANES)
    return pl.pallas_call(
        body,
        name=name,
        grid=(m // tm, n // tn, nk),
        in_specs=in_specs,
        out_specs=o_spec,
        out_shape=jax.ShapeDtypeStruct((m, n), out_dtype),
        scratch_shapes=[pltpu.VMEM(acc_shape, F32)],
        compiler_params=_params("parallel", "parallel", "arbitrary"),
    )(*args)


def _rmsnorm_fwd(x, g, name, after=None):
    t, d = x.shape
    tr = _divisor_tile(t, 512, 8)

    def body(x_ref, g_ref, *rest):
        h_ref = rest[-1]
        xv = x_ref[...]
        rstd = lax.rsqrt(jnp.mean(xv * xv, axis=-1, keepdims=True) + EPS)
        h_ref[...] = (xv * rstd * g_ref[...]).astype(h_ref.dtype)

    return pl.pallas_call(
        body,
        name=name,
        grid=(t // tr,),
        in_specs=[pl.BlockSpec((tr, d), lambda i: (i, 0)), pl.BlockSpec((1, d), lambda i: (0, 0))]
        + ([ANY] if after is not None else []),
        out_specs=pl.BlockSpec((tr, d), lambda i: (i, 0)),
        out_shape=jax.ShapeDtypeStruct((t, d), BF16),
        compiler_params=_params("parallel"),
    )(x, g, *(() if after is None else (after,)))


def _rmsnorm_bwd(x, g, dh, dres, name, after=None):
    t, d = x.shape
    tr = _divisor_tile(t, 256, 8)

    def body(x_ref, g_ref, dh_ref, dres_ref, *rest):
        dx_ref, dx16_ref, dg_ref = rest[-3:]
        i = pl.program_id(0)
        xv = x_ref[...]
        rstd = lax.rsqrt(jnp.mean(xv * xv, axis=-1, keepdims=True) + EPS)
        xh = xv * rstd
        dhv = dh_ref[...]
        dxh = dhv * g_ref[...]
        dx = dres_ref[...] + rstd * (dxh - xh * jnp.mean(dxh * xh, axis=-1, keepdims=True))
        dx_ref[...] = dx
        dx16_ref[...] = dx.astype(dx16_ref.dtype)
        part = jnp.broadcast_to(jnp.sum(dhv * xh, axis=0, keepdims=True), dg_ref.shape)

        @pl.when(i == 0)
        def _():
            dg_ref[...] = part

        @pl.when(i > 0)
        def _():
            dg_ref[...] += part

    row = pl.BlockSpec((tr, d), lambda i: (i, 0))
    return pl.pallas_call(
        body,
        name=name,
        grid=(t // tr,),
        in_specs=[row, pl.BlockSpec((1, d), lambda i: (0, 0)), row, row] + ([ANY] if after is not None else []),
        out_specs=[row, row, pl.BlockSpec((8, d), lambda i: (0, 0))],
        out_shape=[jax.ShapeDtypeStruct((t, d), F32), jax.ShapeDtypeStruct((t, d), BF16),
                   jax.ShapeDtypeStruct((8, d), F32)],
        compiler_params=_params("arbitrary"),
    )(x, g, dh, dres, *(() if after is None else (after,)))


def _lane(shape):
    return lax.broadcasted_iota(jnp.int32, shape, 1)


def _group_sum64(s):
    row = lax.broadcasted_iota(jnp.int32, (LANES, LANES), 0)
    col = lax.broadcasted_iota(jnp.int32, (LANES, LANES), 1)
    ones = jnp.where((row >= HEAD_DIM) == (col >= HEAD_DIM), 1.0, 0.0).astype(BF16)
    out = []
    for t in range(s.shape[1] // LANES):
        piece = s[:, LANES * t:LANES * t + LANES]
        hi = piece.astype(BF16)
        lo = (piece - hi.astype(F32)).astype(BF16)
        out.append(_dot(hi, ones, NN) + _dot(lo, ones, NN))
    return out[0] if len(out) == 1 else jnp.concatenate(out, axis=1)


def _swap32(x):
    w = x.shape[1]
    return jnp.where((_lane(x.shape) & 32) == 0, pltpu.roll(x, w - 32, axis=1), pltpu.roll(x, 32, axis=1))


def _rope(x, c, s):
    return x * c + _swap32(x) * s


def _rope_t(dy, c, s):
    return dy * c + _swap32(dy * s)


def _head_norm(x):
    rstd = lax.rsqrt(_group_sum64(x * x) * (1.0 / HEAD_DIM) + EPS)
    return x * rstd, rstd


def _head_norm_bwd(dxh, xh, rstd):
    return rstd * (dxh - xh * (_group_sum64(dxh * xh) * (1.0 / HEAD_DIM)))


def _roll64(x):
    return pltpu.roll(x, 64, axis=1)


def _attn_specs(wq, wk):
    kb = wq // wk
    prev = lambda i: jnp.maximum(i - 1, 0)
    return dict(
        q=pl.BlockSpec((BLOCK, wq), lambda i: (i, 0)),
        kc=pl.BlockSpec((BLOCK, wk), lambda i: (i, kb)),
        kp=pl.BlockSpec((BLOCK, wk), lambda i: (prev(i), kb)),
        vc=pl.BlockSpec((BLOCK, wk), lambda i: (i, kb + 1)),
        vp=pl.BlockSpec((BLOCK, wk), lambda i: (prev(i), kb + 1)),
        tq=pl.BlockSpec((BLOCK, wq), lambda i: (i, 0)),
        tkp=pl.BlockSpec((BLOCK, wk), lambda i: (prev(i), 0)),
        gq=pl.BlockSpec((1, wq), lambda i: (0, 0)),
        gk=pl.BlockSpec((1, wk), lambda i: (0, 0)),
        sinks=pl.BlockSpec(memory_space=pltpu.SMEM),
    )


def _attn_prologue(i, q_ref, kc_ref, kp_ref, cq_ref, sq_ref, ckp_ref, skp_ref, gq_ref, gk_ref):
    wk = kc_ref.shape[1]
    cq, sq = cq_ref[...], sq_ref[...]
    ck, sk = cq[:, :wk], sq[:, :wk]
    qh, q_rstd = _head_norm(q_ref[...])
    kch, kc_rstd = _head_norm(kc_ref[...])
    kph, kp_rstd = _head_norm(kp_ref[...])
    qn = _rope(qh * gq_ref[...], cq, sq)
    knc = _rope(kch * gk_ref[...], ck, sk)
    knp = _rope(kph * gk_ref[...], ckp_ref[...], skp_ref[...])
    stacked = (Q_PER_KV * BLOCK, BLOCK)
    row = lax.broadcasted_iota(jnp.int32, stacked, 0) & (BLOCK - 1)
    col = lax.broadcasted_iota(jnp.int32, stacked, 1)
    mask_c = col <= row
    valid = jnp.logical_or(mask_c, i > 0)
    half = (lax.broadcasted_iota(jnp.int32, (BLOCK, BLOCK), 1) >= HEAD_DIM).astype(jnp.int32)
    return dict(cq=cq, sq=sq, ck=ck, sk=sk, qh=qh, q_rstd=q_rstd, kch=kch, kc_rstd=kc_rstd, kph=kph,
                kp_rstd=kp_rstd, qn=qn, knc=knc, knp=knp, mask_c=mask_c, valid=valid, half=half)


def _stack_heads(x, g, half):
    kpar = g % 2
    pieces = []
    for j in range(Q_PER_KV):
        t, e = divmod(Q_PER_KV * g + j, 2)
        piece = jnp.where(half == e, x[:, LANES * t:LANES * t + LANES], 0.0)
        pieces.append(piece if e == kpar else _roll64(piece))
    return jnp.concatenate(pieces, axis=0)


def _unstack_heads(y, g, half):
    kpar = g % 2
    slabs = {}
    for j in range(Q_PER_KV):
        t, e = divmod(Q_PER_KV * g + j, 2)
        piece = jnp.where(half == kpar, y[BLOCK * j:BLOCK * j + BLOCK], 0.0)
        piece = piece if e == kpar else _roll64(piece)
        slabs[t] = piece if t not in slabs else slabs[t] + piece
    return slabs


def _group_scores(st, g, sinks_ref, scale):
    ks = g // 2
    sl = slice(LANES * ks, LANES * ks + LANES)
    q4 = _stack_heads(st["qn"], g, st["half"]).astype(BF16)
    kc, kp = st["knc"][:, sl].astype(BF16), st["knp"][:, sl].astype(BF16)
    rows = Q_PER_KV * BLOCK
    at = lax.broadcasted_iota(jnp.int32, (rows, 1), 0)
    head = jnp.zeros((rows, 1), jnp.int32)
    sink = jnp.zeros((rows, 1), F32) + sinks_ref[Q_PER_KV * g]
    for j in range(1, Q_PER_KV):
        head = jnp.where(at >= BLOCK * j, j, head)
        sink = jnp.where(at >= BLOCK * j, sinks_ref[Q_PER_KV * g + j], sink)
    cur = st["mask_c"]
    s = jnp.where(cur, _dot(q4, kc, NT), _dot(q4, kp, NT)) * scale
    s = jnp.where(st["valid"], s, NEG)
    m = jnp.maximum(jnp.max(s, axis=1, keepdims=True), sink)
    p = jnp.exp(s - m)
    p_s = jnp.exp(sink - m)
    inv = 1.0 / (jnp.sum(p, axis=1, keepdims=True) + p_s)
    return dict(sl=sl, head=head, q4=q4, kc=kc, kp=kp, cur=cur, pr=p * inv, pr_s=p_s * inv)


def _attn_fwd(proj, tables, gq, gk, sinks, wq, wk, name, after=None):
    t = proj.shape[0]
    nb = t // BLOCK
    sp = _attn_specs(wq, wk)
    scale = HEAD_DIM ** -0.5
    cos_t, sin_t, cos_k, sin_k = tables

    def body(sinks_ref, q_ref, kc_ref, kp_ref, vc_ref, vp_ref, cq_ref, sq_ref, ckp_ref, skp_ref, gq_ref, gk_ref,
             *rest):
        o_ref = rest[-1]
        i = pl.program_id(0)
        st = _attn_prologue(i, q_ref, kc_ref, kp_ref, cq_ref, sq_ref, ckp_ref, skp_ref, gq_ref, gk_ref)
        for g in range(wq // (Q_PER_KV * HEAD_DIM)):
            gs = _group_scores(st, g, sinks_ref, scale)
            own = st["half"] == g % 2
            vc = jnp.where(own, vc_ref[:, gs["sl"]], 0.0).astype(BF16)
            vp = jnp.where(own, vp_ref[:, gs["sl"]], 0.0).astype(BF16)
            pr = gs["pr"].astype(BF16)
            zero = jnp.zeros_like(pr)
            out = _dot(jnp.where(gs["cur"], pr, zero), vc, NN) + _dot(jnp.where(gs["cur"], zero, pr), vp, NN)
            for ts, slab in _unstack_heads(out, g, st["half"]).items():
                o_ref[:, LANES * ts:LANES * ts + LANES] = slab.astype(o_ref.dtype)

    return pl.pallas_call(
        body,
        name=name,
        grid=(nb,),
        in_specs=[sp["sinks"], sp["q"], sp["kc"], sp["kp"], sp["vc"], sp["vp"], sp["tq"], sp["tq"], sp["tkp"],
                  sp["tkp"], sp["gq"], sp["gk"]] + ([ANY] if after is not None else []),
        out_specs=pl.BlockSpec((BLOCK, wq), lambda i: (i, 0)),
        out_shape=jax.ShapeDtypeStruct((t, wq), BF16),
        compiler_params=_params("parallel"),
    )(sinks, proj, proj, proj, proj, proj, cos_t, sin_t, cos_k, sin_k, gq, gk, *(() if after is None else (after,)))


def _attn_bwd(proj, dout, tables, gq, gk, sinks, wq, wk, name, after=None):
    t = proj.shape[0]
    nb = t // BLOCK
    sp = _attn_specs(wq, wk)
    scale = HEAD_DIM ** -0.5
    cos_t, sin_t, cos_k, sin_k = tables

    def body(sinks_ref, q_ref, kc_ref, kp_ref, vc_ref, vp_ref, cq_ref, sq_ref, ckp_ref, skp_ref, gq_ref, gk_ref,
             do_ref, *rest):
        dq_ref, dk_ref, dv_ref, dgq_ref, dgk_ref, dsk_ref, dqn_ref, dknc_ref, dknp_ref, dvc_ref, dvp_ref = rest[-11:]
        i = pl.program_id(0)
        st = _attn_prologue(i, q_ref, kc_ref, kp_ref, cq_ref, sq_ref, ckp_ref, skp_ref, gq_ref, gk_ref)
        dknc_ref[...] = jnp.zeros_like(dknc_ref)
        dknp_ref[...] = jnp.zeros_like(dknp_ref)
        dvc_ref[...] = jnp.zeros_like(dvc_ref)
        dvp_ref[...] = jnp.zeros_like(dvp_ref)
        lane8 = _lane((8, LANES))
        dsinks = jnp.zeros((8, LANES), F32)
        for g in range(wq // (Q_PER_KV * HEAD_DIM)):
            gs = _group_scores(st, g, sinks_ref, scale)
            sl = gs["sl"]
            do4 = _stack_heads(do_ref[...], g, st["half"]).astype(BF16)
            cur, pr = gs["cur"], gs["pr"]
            dp = jnp.where(cur, _dot(do4, vc_ref[:, sl].astype(BF16), NT), _dot(do4, vp_ref[:, sl].astype(BF16), NT))
            rs = jnp.sum(pr * dp, axis=1, keepdims=True)
            ds = (pr * (dp - rs) * scale).astype(BF16)
            pr16 = pr.astype(BF16)
            zero = jnp.zeros_like(ds)
            ds_c, ds_p = jnp.where(cur, ds, zero), jnp.where(cur, zero, ds)
            pr_c, pr_p = jnp.where(cur, pr16, zero), jnp.where(cur, zero, pr16)
            dsink_rows = -gs["pr_s"] * rs
            for j in range(Q_PER_KV):
                dsink = jnp.sum(jnp.where(gs["head"] == j, dsink_rows, 0.0))
                dsinks = dsinks + jnp.where(lane8 == Q_PER_KV * g + j, dsink, 0.0)
            dq4 = _dot(ds_c, gs["kc"], NN) + _dot(ds_p, gs["kp"], NN)
            for ts, slab in _unstack_heads(dq4, g, st["half"]).items():
                dqn_ref[:, LANES * ts:LANES * ts + LANES] = slab
            dvc_ref[:, sl] += _dot(pr_c.astype(BF16), do4, TN)
            dvp_ref[:, sl] += _dot(pr_p.astype(BF16), do4, TN)
            dknc_ref[:, sl] += _dot(ds_c, gs["q4"], TN)
            dknp_ref[:, sl] += _dot(ds_p, gs["q4"], TN)

        gqv, gkv = gq_ref[...], gk_ref[...]
        dqg = _rope_t(dqn_ref[...], st["cq"], st["sq"])
        dq_ref[...] = _head_norm_bwd(dqg * gqv, st["qh"], st["q_rstd"]).astype(dq_ref.dtype)
        dkcg = _rope_t(dknc_ref[...], st["ck"], st["sk"])
        dkpg = _rope_t(dknp_ref[...], ckp_ref[...], skp_ref[...])
        dk_cur = _head_norm_bwd(dkcg * gkv, st["kch"], st["kc_rstd"])
        dk_prev = _head_norm_bwd(dkpg * gkv, st["kph"], st["kp_rstd"])
        dgq_part = jnp.broadcast_to(jnp.sum(dqg * st["qh"], axis=0, keepdims=True), dgq_ref.shape)
        dgk_part = jnp.broadcast_to(
            jnp.sum(dkcg * st["kch"] + dkpg * st["kph"], axis=0, keepdims=True), dgk_ref.shape)
        cur = pl.ds(pl.multiple_of(i * BLOCK, BLOCK), BLOCK)
        dk_ref[cur, :] = dk_cur
        dv_ref[cur, :] = dvc_ref[...]

        @pl.when(i == 0)
        def _():
            dgq_ref[...] = dgq_part
            dgk_ref[...] = dgk_part
            dsk_ref[...] = dsinks

        @pl.when(i > 0)
        def _():
            before = pl.ds(pl.multiple_of((i - 1) * BLOCK, BLOCK), BLOCK)
            dk_ref[before, :] += dk_prev
            dv_ref[before, :] += dvp_ref[...]
            dgq_ref[...] += dgq_part
            dgk_ref[...] += dgk_part
            dsk_ref[...] += dsinks

    whole = lambda shape: pl.BlockSpec(shape, lambda i: (0, 0))
    return pl.pallas_call(
        body,
        name=name,
        grid=(nb,),
        in_specs=[sp["sinks"], sp["q"], sp["kc"], sp["kp"], sp["vc"], sp["vp"], sp["tq"], sp["tq"], sp["tkp"],
                  sp["tkp"], sp["gq"], sp["gk"], pl.BlockSpec((BLOCK, wq), lambda i: (i, 0))]
        + ([ANY] if after is not None else []),
        out_specs=[pl.BlockSpec((BLOCK, wq), lambda i: (i, 0)), whole((t, wk)), whole((t, wk)), whole((8, wq)),
                   whole((8, wk)), whole((8, LANES))],
        out_shape=[jax.ShapeDtypeStruct((t, wq), BF16), jax.ShapeDtypeStruct((t, wk), F32),
                   jax.ShapeDtypeStruct((t, wk), F32), jax.ShapeDtypeStruct((8, wq), F32),
                   jax.ShapeDtypeStruct((8, wk), F32), jax.ShapeDtypeStruct((8, LANES), F32)],
        scratch_shapes=[pltpu.VMEM((BLOCK, wq), F32), pltpu.VMEM((BLOCK, wk), F32), pltpu.VMEM((BLOCK, wk), F32),
                        pltpu.VMEM((BLOCK, wk), F32), pltpu.VMEM((BLOCK, wk), F32)],
        compiler_params=_params("arbitrary"),
    )(sinks, proj, proj, proj, proj, proj, cos_t, sin_t, cos_k, sin_k, gq, gk, dout,
      *(() if after is None else (after,)))


_GELU_K = math.sqrt(2.0 / math.pi)
_GELU_A = 0.044715


def _gelu(x):
    return 0.5 * x * (1.0 + jnp.tanh(_GELU_K * (x + _GELU_A * x * x * x)))


def _gelu_and_grad(x):
    th = jnp.tanh(_GELU_K * (x + _GELU_A * x * x * x))
    return (0.5 * x * (1.0 + th),
            0.5 * (1.0 + th) + 0.5 * x * (1.0 - th * th) * (_GELU_K * (1.0 + 3.0 * _GELU_A * x * x)))


def _group_ln(v):
    mu = jnp.mean(v, axis=1, keepdims=True)
    cen = v - mu
    rstd = lax.rsqrt(jnp.mean(cen * cen, axis=1, keepdims=True) + EPS)
    return cen * rstd, rstd


def _sgu_geometry(off_u, ws):
    cw = math.gcd(off_u, ws)
    return cw, ws // cw, off_u // cw, (off_u + ws) // cw


def _sgu_fwd(proj, ln_g, ln_b, w_s, bt, off_u, ws, name, after=None):
    t = proj.shape[0]
    nb = t // BLOCK
    cw, nc, ub, vb = _sgu_geometry(off_u, ws)
    gpc = cw // LANES
    ng = ws // LANES

    def body(u_ref, v_ref, g_ref, b_ref, w_ref, bt_ref, *rest):
        o_ref = rest[-1]
        jc = pl.program_id(0)
        row = lax.broadcasted_iota(jnp.int32, (BLOCK, BLOCK), 0)
        col = lax.broadcasted_iota(jnp.int32, (BLOCK, BLOCK), 1)
        lane_g = _lane((BLOCK, ng))
        for gi in range(gpc):
            sl = slice(LANES * gi, LANES * gi + LANES)
            xh, _ = _group_ln(_gelu(v_ref[:, sl]))
            vn = xh * g_ref[:, sl] + b_ref[:, sl]
            w = jnp.where(row >= col, w_ref[gi], 0.0).astype(BF16)
            bias = jnp.sum(jnp.where(lane_g == jc * gpc + gi, bt_ref[...], 0.0), axis=1, keepdims=True)
            s = _dot(w, vn.astype(BF16), NN) + bias
            o_ref[:, sl] = (_gelu(u_ref[:, sl]) * s).astype(o_ref.dtype)

    return pl.pallas_call(
        body,
        name=name,
        grid=(nc, nb),
        in_specs=[pl.BlockSpec((BLOCK, cw), lambda jc, i: (i, ub + jc)),
                  pl.BlockSpec((BLOCK, cw), lambda jc, i: (i, vb + jc)),
                  pl.BlockSpec((1, cw), lambda jc, i: (0, jc)),
                  pl.BlockSpec((1, cw), lambda jc, i: (0, jc)),
                  pl.BlockSpec((gpc, BLOCK, BLOCK), lambda jc, i: (jc, 0, 0)),
                  pl.BlockSpec((BLOCK, ng), lambda jc, i: (0, 0))] + ([ANY] if after is not None else []),
        out_specs=pl.BlockSpec((BLOCK, cw), lambda jc, i: (i, jc)),
        out_shape=jax.ShapeDtypeStruct((t, ws), BF16),
        compiler_params=_params("parallel", "parallel"),
    )(proj, proj, ln_g, ln_b, w_s, bt, *(() if after is None else (after,)))


def _sgu_bwd(proj, dout, ln_g, ln_b, w_s, bt, off_u, ws, name):
    t = proj.shape[0]
    nb = t // BLOCK
    cw, nc, ub, vb = _sgu_geometry(off_u, ws)
    gpc = cw // LANES
    ng = ws // LANES

    def body(u_ref, v_ref, g_ref, b_ref, w_ref, bt_ref, do_ref, du_ref, dv_ref, dg_ref, db_ref, dw_ref, dbs_ref,
             bacc_ref):
        jc = pl.program_id(0)
        i = pl.program_id(1)
        row = lax.broadcasted_iota(jnp.int32, (BLOCK, BLOCK), 0)
        col = lax.broadcasted_iota(jnp.int32, (BLOCK, BLOCK), 1)
        lane_g = _lane((BLOCK, ng))
        tri = row >= col

        @pl.when(i == 0)
        def _():
            dg_ref[...] = jnp.zeros_like(dg_ref)
            db_ref[...] = jnp.zeros_like(db_ref)
            dw_ref[...] = jnp.zeros_like(dw_ref)
            bacc_ref[...] = jnp.zeros_like(bacc_ref)

        for gi in range(gpc):
            sl = slice(LANES * gi, LANES * gi + LANES)
            u_raw, v_raw = u_ref[:, sl], v_ref[:, sl]
            u_act, u_slope = _gelu_and_grad(u_raw)
            v_act, v_slope = _gelu_and_grad(v_raw)
            xh, rstd = _group_ln(v_act)
            gam = g_ref[:, sl]
            vn = (xh * gam + b_ref[:, sl]).astype(BF16)
            w = jnp.where(tri, w_ref[gi], 0.0)
            bias = jnp.sum(jnp.where(lane_g == jc * gpc + gi, bt_ref[...], 0.0), axis=1, keepdims=True)
            s = _dot(w.astype(BF16), vn, NN) + bias
            dov = do_ref[:, sl]
            du_ref[:, sl] = (dov * s * u_slope).astype(du_ref.dtype)
            ds = dov * u_act
            ds16 = ds.astype(BF16)
            dw_ref[gi] += jnp.where(tri, _dot(ds16, vn, NT), 0.0)
            bacc_ref[gi] += ds
            dvn = _dot(w.T.astype(BF16), ds16, NN)
            dg_ref[:, sl] += jnp.broadcast_to(jnp.sum(dvn * xh, axis=0, keepdims=True), (8, LANES))
            db_ref[:, sl] += jnp.broadcast_to(jnp.sum(dvn, axis=0, keepdims=True), (8, LANES))
            dxh = dvn * gam
            dvg = rstd * (dxh - jnp.mean(dxh, axis=1, keepdims=True)
                          - xh * jnp.mean(dxh * xh, axis=1, keepdims=True))
            dv_ref[:, sl] = (dvg * v_slope).astype(dv_ref.dtype)

        @pl.when(i == nb - 1)
        def _():
            for gi in range(gpc):
                dbs_ref[gi] = jnp.broadcast_to(jnp.sum(bacc_ref[gi].T, axis=0, keepdims=True), (8, LANES))

    blk = lambda base: pl.BlockSpec((BLOCK, cw), lambda jc, i: (i, base + jc))
    vec = pl.BlockSpec((1, cw), lambda jc, i: (0, jc))
    acc = pl.BlockSpec((8, cw), lambda jc, i: (0, jc))
    wsp = pl.BlockSpec((gpc, BLOCK, BLOCK), lambda jc, i: (jc, 0, 0))
    return pl.pallas_call(
        body,
        name=name,
        grid=(nc, nb),
        in_specs=[blk(ub), blk(vb), vec, vec, wsp, pl.BlockSpec((BLOCK, ng), lambda jc, i: (0, 0)), blk(0)],
        out_specs=[blk(0), blk(0), acc, acc, wsp, pl.BlockSpec((gpc, 8, LANES), lambda jc, i: (jc, 0, 0))],
        out_shape=[jax.ShapeDtypeStruct((t, ws), BF16), jax.ShapeDtypeStruct((t, ws), BF16),
                   jax.ShapeDtypeStruct((8, ws), F32), jax.ShapeDtypeStruct((8, ws), F32),
                   jax.ShapeDtypeStruct((ng, BLOCK, BLOCK), F32), jax.ShapeDtypeStruct((ng, 8, LANES), F32)],
        scratch_shapes=[pltpu.VMEM((gpc, BLOCK, BLOCK), F32)],
        compiler_params=_params("arbitrary", "arbitrary"),
    )(proj, proj, ln_g, ln_b, w_s, bt, dout)


def _sigmoid(x):
    return 1.0 / (1.0 + jnp.exp(-x))


def _merge_geometry(off_g, d):
    cw = math.gcd(off_g, d)
    return cw, d // cw, off_g // cw, (off_g + d) // cw


def _branches_fwd(attn, sgu, wab_t, wsb_t, proj, off_g, name):
    t = attn.shape[0]
    d = wab_t.shape[0]
    tn, _, ab, bb = _merge_geometry(off_g, d)
    tm = _divisor_tile(t, 1024, 128)

    def body(a1_ref, a2_ref, b1_ref, b2_ref, la_ref, lb_ref, bra_ref, brb_ref, o_ref):
        for rows in _row_chunks(tm):
            va = _dot(a1_ref[rows, :], b1_ref[...], NT)
            vb = _dot(a2_ref[rows, :], b2_ref[...], NT)
            bra_ref[rows, :] = va
            brb_ref[rows, :] = vb
            o_ref[rows, :] = (_sigmoid(la_ref[rows, :]) * va + _sigmoid(lb_ref[rows, :]) * vb).astype(o_ref.dtype)

    rows = lambda w: pl.BlockSpec((tm, w), lambda i, j: (i, 0))
    wrow = lambda w: pl.BlockSpec((tn, w), lambda i, j: (j, 0))
    blk = lambda base: pl.BlockSpec((tm, tn), lambda i, j: (i, base + j))
    return pl.pallas_call(
        body,
        name=name,
        grid=(t // tm, d // tn),
        in_specs=[rows(attn.shape[1]), rows(sgu.shape[1]), wrow(wab_t.shape[1]), wrow(wsb_t.shape[1]), blk(ab),
                  blk(bb)],
        out_specs=[blk(0)] * 3,
        out_shape=[jax.ShapeDtypeStruct((t, d), F32), jax.ShapeDtypeStruct((t, d), F32),
                   jax.ShapeDtypeStruct((t, d), BF16)],
        compiler_params=_params("parallel", "parallel"),
    )(attn, sgu, wab_t, wsb_t, proj, proj)


def _branches_bwd(dx16, wo, br_a, br_b, proj, off_g, name, after=None):
    t, d = br_a.shape
    tn, _, ab, bb = _merge_geometry(off_g, d)
    tm = _divisor_tile(t, 1024, 128)
    k = dx16.shape[1]

    def body(a_ref, b_ref, bra_ref, brb_ref, la_ref, lb_ref, *rest):
        da_ref, db_ref, dla_ref, dlb_ref = rest[-4:]
        for rows in _row_chunks(tm):
            dmv = _dot(a_ref[rows, :], b_ref[...], NT)
            ga, gb = _sigmoid(la_ref[rows, :]), _sigmoid(lb_ref[rows, :])
            da_ref[rows, :] = (dmv * ga).astype(da_ref.dtype)
            db_ref[rows, :] = (dmv * gb).astype(db_ref.dtype)
            dla_ref[rows, :] = (dmv * bra_ref[rows, :] * ga * (1.0 - ga)).astype(dla_ref.dtype)
            dlb_ref[rows, :] = (dmv * brb_ref[rows, :] * gb * (1.0 - gb)).astype(dlb_ref.dtype)

    blk = lambda base: pl.BlockSpec((tm, tn), lambda i, j: (i, base + j))
    return pl.pallas_call(
        body,
        name=name,
        grid=(t // tm, d // tn),
        in_specs=[pl.BlockSpec((tm, k), lambda i, j: (i, 0)), pl.BlockSpec((tn, k), lambda i, j: (j, 0)), blk(0),
                  blk(0), blk(ab), blk(bb)] + ([ANY] if after is not None else []),
        out_specs=[blk(0)] * 4,
        out_shape=[jax.ShapeDtypeStruct((t, d), BF16)] * 4,
        compiler_params=_params("parallel", "parallel"),
    )(dx16, wo, br_a, br_b, proj, proj, *(() if after is None else (after,)))


def _gate_up_fwd(h2, wgu_t, name, after=None):
    t, d = h2.shape
    f = wgu_t.shape[0] // 2
    tm = _divisor_tile(t, 1024, 128)
    tn = _divisor_tile(f, 512, 128)
    nb = f // tn

    def body(a_ref, bg_ref, bu_ref, *rest):
        gu_ref, act_ref = rest[-2:]
        for rows in _row_chunks(tm):
            av = a_ref[rows, :]
            gv = _dot(av, bg_ref[...], NT)
            uv = _dot(av, bu_ref[...], NT)
            gu_ref[0, rows, :] = gv
            gu_ref[1, rows, :] = uv
            act_ref[rows, :] = (gv * _sigmoid(gv) * uv).astype(act_ref.dtype)

    return pl.pallas_call(
        body,
        name=name,
        grid=(t // tm, nb),
        in_specs=[pl.BlockSpec((tm, d), lambda i, j: (i, 0)), pl.BlockSpec((tn, d), lambda i, j: (j, 0)),
                  pl.BlockSpec((tn, d), lambda i, j: (j + nb, 0))] + ([ANY] if after is not None else []),
        out_specs=[pl.BlockSpec((2, tm, tn), lambda i, j: (0, i, j)), pl.BlockSpec((tm, tn), lambda i, j: (i, j))],
        out_shape=[jax.ShapeDtypeStruct((2, t, f), F32), jax.ShapeDtypeStruct((t, f), BF16)],
        compiler_params=_params("parallel", "parallel"),
    )(h2, wgu_t, wgu_t, *(() if after is None else (after,)))


def _gate_up_bwd(dx16, wd, gu, name, after=None):
    t, d = dx16.shape
    f = wd.shape[0]
    tm = _divisor_tile(t, 1024, 128)
    tn = _divisor_tile(f, 512, 128)

    def body(a_ref, b_ref, gu_ref, *rest):
        o_ref = rest[-1]
        for rows in _row_chunks(tm):
            dav = _dot(a_ref[rows, :], b_ref[...], NT)
            gv = gu_ref[0, rows, :]
            sg = _sigmoid(gv)
            o_ref[0, rows, :] = (dav * gu_ref[1, rows, :] * (sg + gv * sg * (1.0 - sg))).astype(o_ref.dtype)
            o_ref[1, rows, :] = (dav * gv * sg).astype(o_ref.dtype)

    pair = pl.BlockSpec((2, tm, tn), lambda i, j: (0, i, j))
    return pl.pallas_call(
        body,
        name=name,
        grid=(t // tm, f // tn),
        in_specs=[pl.BlockSpec((tm, d), lambda i, j: (i, 0)), pl.BlockSpec((tn, d), lambda i, j: (j, 0)), pair]
        + ([ANY] if after is not None else []),
        out_specs=pair,
        out_shape=jax.ShapeDtypeStruct((2, t, f), BF16),
        compiler_params=_params("parallel", "parallel"),
    )(dx16, wd, gu, *(() if after is None else (after,)))


def _loss_and_grad(y, target, name):
    t, d = y.shape
    tr = _divisor_tile(t, 512, 8)

    def body(y_ref, t_ref, l_ref, dy_ref, dy16_ref):
        i = pl.program_id(0)
        err = y_ref[...] - t_ref[...]
        dy_ref[...] = err * (1.0 / d)
        dy16_ref[...] = (err * (1.0 / d)).astype(dy16_ref.dtype)
        part = jnp.broadcast_to(0.5 * jnp.sum(err * err) * (1.0 / d), l_ref.shape)

        @pl.when(i == 0)
        def _():
            l_ref[...] = part

        @pl.when(i > 0)
        def _():
            l_ref[...] += part

    row = pl.BlockSpec((tr, d), lambda i: (i, 0))
    return pl.pallas_call(
        body,
        name=name,
        grid=(t // tr,),
        in_specs=[row, row],
        out_specs=[pl.BlockSpec((8, LANES), lambda i: (0, 0)), row, row],
        out_shape=[jax.ShapeDtypeStruct((8, LANES), F32), jax.ShapeDtypeStruct((t, d), F32),
                   jax.ShapeDtypeStruct((t, d), BF16)],
        compiler_params=_params("arbitrary"),
    )(y, target)


def _adam_math(w, g, m, v):
    m = ADAM_B1 * m + (1.0 - ADAM_B1) * g
    v = ADAM_B2 * v + (1.0 - ADAM_B2) * (g * g)
    m_hat = m / (1.0 - ADAM_B1 ** ADAM_STEP)
    v_hat = v / (1.0 - ADAM_B2 ** ADAM_STEP)
    delta = -ADAM_LR * (m_hat / (jnp.sqrt(v_hat) + ADAM_EPS) + ADAM_WD * w)
    return delta, m, v


def _row_tile(r, c, elems=512 * 1024):
    return _divisor_tile(r, max(8, elems // c // 8 * 8), 8)


def _adam(w, grads, m, v, chip, name, after=None):
    nl, r, c = w.shape
    tr = _row_tile(r, c, 384 * 1024)
    nb = r // tr
    counts = [len(terms) for terms, _ in grads]

    def body(chip_ref, *refs):
        w_ref, m_ref, v_ref = refs[:3]
        g_ref, d_ref, nm_ref, nv_ref = refs[-4:]
        layer = pl.program_id(0)
        g, at = None, 3
        for li, n in enumerate(counts):
            total = refs[at][...].astype(F32)
            for ref in refs[at + 1:at + n]:
                total = total + ref[...].astype(F32)
            g = total if g is None else jnp.where(layer == li, total, g)
            at += n
        g_ref[...] = g
        d_ref[...], nm_ref[...], nv_ref[...] = _adam_math(w_ref[...], g, m_ref[...], v_ref[...])

    def term_spec(li, p, by_owner):
        def index(l, i, chip_ref):
            rows = jnp.where(l < li, 0, jnp.where(l > li, nb - 1, i))
            return (p, chip_ref[0] if by_owner else 0, rows, 0)
        return pl.BlockSpec((None, None, tr, c), index)

    row = pl.BlockSpec((None, tr, c), lambda l, i, chip_ref: (l, i, 0))
    specs, arrays = [], []
    for li, (terms, p) in enumerate(grads):
        for term in terms:
            specs.append(term_spec(li, p, term.shape[1] == 4))
            arrays.append(term)
    return pl.pallas_call(
        body,
        name=name,
        grid_spec=pltpu.PrefetchScalarGridSpec(
            num_scalar_prefetch=1, grid=(nl, nb),
            in_specs=[row] * 3 + specs + ([ANY] if after is not None else []), out_specs=[row] * 4),
        out_shape=[jax.ShapeDtypeStruct((nl, r, c), F32)] * 4,
        compiler_params=_params("arbitrary", "arbitrary"),
    )(chip, w, m, v, *arrays, *(() if after is None else (after,)))


def _place_shard(parts, layer, dev, out_dtype, name, after=None):
    p = len(parts)
    _, r, c = parts[0].shape
    tr = _row_tile(r, c)

    def body(dev_ref, *refs):
        o_ref = refs[-1]
        x = refs[0][...]
        for pi in range(1, p):
            x = jnp.where(pl.program_id(0) == pi, refs[pi][...], x)
        o_ref[...] = x.astype(o_ref.dtype)

    return pl.pallas_call(
        body,
        name=name,
        grid_spec=pltpu.PrefetchScalarGridSpec(
            num_scalar_prefetch=1,
            grid=(p, r // tr),
            in_specs=[pl.BlockSpec((None, tr, c), lambda pi, i, dev_ref: (layer, i, 0))] * p
            + ([ANY] if after is not None else []),
            out_specs=pl.BlockSpec((None, None, tr, c), lambda pi, i, dev_ref: (pi, dev_ref[0], i, 0)),
        ),
        out_shape=jax.ShapeDtypeStruct((p, N_DEV, r, c), out_dtype),
        compiler_params=_params("parallel", "parallel"),
    )(dev, *parts, *(() if after is None else (after,)))


def _sum_sibling(g, land, core, name):
    p, _, _, r, c = g.shape
    tr = _row_tile(r, c, 1024 * 1024)

    def body(core_ref, g_ref, l_ref, o_ref):
        o_ref[...] = (g_ref[...].astype(F32) + l_ref[...].astype(F32)).astype(o_ref.dtype)

    return pl.pallas_call(
        body,
        name=name,
        grid_spec=pltpu.PrefetchScalarGridSpec(
            num_scalar_prefetch=1,
            grid=(p, 4, r // tr),
            in_specs=[pl.BlockSpec((None, None, None, tr, c), lambda pi, q, i, core_ref: (pi, q, core_ref[0], i, 0)),
                      pl.BlockSpec((None, None, None, tr, c), lambda pi, q, i, core_ref: (pi, q, 0, i, 0))],
            out_specs=pl.BlockSpec((None, None, tr, c), lambda pi, q, i, core_ref: (pi, q, i, 0)),
        ),
        out_shape=jax.ShapeDtypeStruct((p, 4, r, c), BF16),
        compiler_params=_params("parallel", "parallel", "parallel"),
    )(core, g, land)


def _sum_chips(s, lands, chip, name):
    p, _, r, c = s.shape
    tr = _row_tile(r, c)

    def body(chip_ref, s_ref, l0_ref, l1_ref, l2_ref, o_ref):
        total = s_ref[...].astype(F32) + l0_ref[...].astype(F32)
        o_ref[...] = total + l1_ref[...].astype(F32) + l2_ref[...].astype(F32)

    land_spec = pl.BlockSpec((None, None, tr, c), lambda pi, i, chip_ref: (pi, 0, i, 0))
    return pl.pallas_call(
        body,
        name=name,
        grid_spec=pltpu.PrefetchScalarGridSpec(
            num_scalar_prefetch=1,
            grid=(p, r // tr),
            in_specs=[pl.BlockSpec((None, None, tr, c), lambda pi, i, chip_ref: (pi, chip_ref[0], i, 0)),
                      land_spec, land_spec, land_spec],
            out_specs=pl.BlockSpec((None, tr, c), lambda pi, i, chip_ref: (pi, i, 0)),
        ),
        out_shape=jax.ShapeDtypeStruct((p, r, c), F32),
        compiler_params=_params("parallel", "parallel"),
    )(chip, s, *lands)


def _small_reduce_adam(gathered, w, m, v, name):
    _, r, c = gathered.shape
    tr = _row_tile(r, c)

    def body(p_ref, w_ref, m_ref, v_ref, g_ref, d_ref, nm_ref, nv_ref):
        g = p_ref[0]
        for j in range(1, N_DEV):
            g = g + p_ref[j]
        g_ref[...] = g
        d_ref[...], nm_ref[...], nv_ref[...] = _adam_math(w_ref[...], g, m_ref[...], v_ref[...])

    row = pl.BlockSpec((tr, c), lambda i: (i, 0))
    return pl.pallas_call(
        body,
        name=name,
        grid=(r // tr,),
        in_specs=[pl.BlockSpec((N_DEV, tr, c), lambda i: (0, i, 0)), row, row, row],
        out_specs=[row] * 4,
        out_shape=[jax.ShapeDtypeStruct((r, c), F32)] * 4,
        compiler_params=_params("parallel"),
    )(gathered, w, m, v)


def _place():
    return lax.axis_index("x"), lax.axis_index("y"), lax.axis_index("c")


HBM =pl.BlockSpec(memory_space=pltpu.HBM)
SEM = pl.BlockSpec(memory_space=pltpu.SEMAPHORE)
TOKEN = pl.BlockSpec(memory_space=pltpu.VMEM)
EFFECT = pltpu.SideEffectType.DATAFLOW_SIDE_EFFECTING


def _in_hbm(a):
    return pltpu.with_memory_space_constraint(a, pltpu.HBM)


_FLIPS = {"me": (0, 0, 0), "s": (0, 0, 1), "x": (1, 0, 0), "y": (0, 1, 0), "d": (1, 1, 0)}
GATHER_STAGES = (
    (("s", "me", "all"), ("x", "me", "all"), ("y", "me", "all")),
    (("s", "x", "all"), ("s", "y", "all"), ("y", "x", "first"), ("x", "y", "second")),
    (("s", "d", "all"),),
)


def _flipped(place, *names):
    out = list(place)
    for name in names:
        out = [1 - p if f else p for p, f in zip(out, _FLIPS[name])]
    return tuple(out)


def _block_part(ref, place, part):
    px, py, pc = place
    rows = ref.shape[2]
    span = {"all": pl.ds(0, rows), "first": pl.ds(0, rows // 2), "second": pl.ds(rows // 2, rows // 2)}[part]
    return ref.at[:, pl.ds(4 * px + 2 * py + pc, 1), span]


def _split_start(bufs, moves, name, after=None):
    n, nm = len(bufs), len(moves)
    extra = 0 if after is None else 1

    def body(*refs):
        ssem, rsem = refs[n + extra], refs[n + extra + 1]
        outs, token = refs[n + extra + 2:2 * n + extra + 2], refs[2 * n + extra + 2]
        me = _place()
        for a in range(n):
            for k, (to, owner, part) in enumerate(moves):
                piece = _block_part(outs[a], _flipped(me, owner), part)
                pltpu.make_async_remote_copy(
                    src_ref=piece, dst_ref=piece, send_sem=ssem.at[nm * a + k], recv_sem=rsem.at[nm * a + k],
                    device_id=_flipped(me, to), device_id_type=MESH).start()
        token[...] = jnp.zeros_like(token)

    outs = pl.pallas_call(
        body,
        name=name,
        in_specs=[HBM] * n + [ANY] * extra,
        out_specs=[SEM, SEM] + [HBM] * n + [TOKEN],
        out_shape=[pltpu.SemaphoreType.DMA((nm * n,))] * 2 + [pltpu.HBM(b.shape, b.dtype) for b in bufs]
        + [jax.ShapeDtypeStruct((8, LANES), F32)],
        input_output_aliases={i: 2 + i for i in range(n)},
        compiler_params=pltpu.CompilerParams(has_side_effects=EFFECT),
    )(*[_in_hbm(b) for b in bufs], *(() if after is None else (after,)))
    return outs[0], outs[1], list(outs[2:2 + n]), outs[-1]


def _split_wait(send_sems, recv_sems, bufs, moves, after, name):
    n, nm = len(bufs), len(moves)

    def body(*refs):
        ins, ssem, rsem = refs[:n], refs[n], refs[n + 1]
        me = _place()
        for a in range(n):
            for k, (to, owner, part) in enumerate(moves):
                landed = _block_part(ins[a], _flipped(me, owner, to), part)
                cp = pltpu.make_async_remote_copy(
                    src_ref=landed, dst_ref=landed, send_sem=ssem.at[nm * a + k], recv_sem=rsem.at[nm * a + k],
                    device_id=_flipped(me, to), device_id_type=MESH)
                cp.wait_send()
                cp.wait_recv()

    return pl.pallas_call(
        body,
        name=name,
        in_specs=[HBM] * n + [SEM, SEM, ANY],
        out_specs=[HBM] * n,
        out_shape=[pltpu.HBM(b.shape, b.dtype) for b in bufs],
        input_output_aliases={i: i for i in range(n)},
        compiler_params=pltpu.CompilerParams(has_side_effects=EFFECT),
    )(*bufs, send_sems, recv_sems, after)


def _chips_start(sums, name, after=None):
    n = len(sums)
    extra = 0 if after is None else 1

    def body(*refs):
        refs = refs[:4 * n] + refs[4 * n + extra:]
        ssem, rsem = refs[4 * n], refs[4 * n + 1]
        src, land = refs[4 * n + 2:5 * n + 2], refs[5 * n + 2:8 * n + 2]
        token = refs[8 * n + 2]
        x, y, c = _place()
        chips = [(1 - x, y), (x, 1 - y), (1 - x, 1 - y)]
        for a in range(n):
            for k, (px, py) in enumerate(chips):
                pltpu.make_async_remote_copy(
                    src_ref=src[a].at[:, pl.ds(2 * px + py, 1)], dst_ref=land[3 * a + k], send_sem=ssem.at[3 * a + k],
                    recv_sem=rsem.at[3 * a + k], device_id=(px, py, c), device_id_type=MESH).start()
        token[...] = jnp.zeros_like(token)

    lands = []
    for s in sums:
        lands += [lax.empty((s.shape[0], 1) + s.shape[2:], s.dtype) for _ in range(3)]
    outs = pl.pallas_call(
        body,
        name=name,
        in_specs=[HBM] * (4 * n) + [ANY] * extra,
        out_specs=[SEM, SEM] + [HBM] * (4 * n) + [TOKEN],
        out_shape=[pltpu.SemaphoreType.DMA((3 * n,))] * 2 + [pltpu.HBM(b.shape, b.dtype) for b in list(sums) + lands]
        + [jax.ShapeDtypeStruct((8, LANES), F32)],
        input_output_aliases={i: 2 + i for i in range(4 * n)},
        compiler_params=pltpu.CompilerParams(has_side_effects=EFFECT),
    )(*[_in_hbm(b) for b in list(sums) + lands], *(() if after is None else (after,)))
    return outs[0], outs[1], list(outs[2:2 + n]), list(outs[2 + n:2 + 4 * n]), outs[-1]


def _chips_wait(send_sems, recv_sems, sums, lands, after, name):
    n = len(sums)

    def body(*refs):
        src, land = refs[:n], refs[n:4 * n]
        ssem, rsem = refs[4 * n], refs[4 * n + 1]
        x, y, c = _place()
        chips = [(1 - x, y), (x, 1 - y), (1 - x, 1 - y)]
        for a in range(n):
            for k, (px, py) in enumerate(chips):
                cp = pltpu.make_async_remote_copy(
                    src_ref=src[a].at[:, pl.ds(2 * px + py, 1)], dst_ref=land[3 * a + k], send_sem=ssem.at[3 * a + k],
                    recv_sem=rsem.at[3 * a + k], device_id=(px, py, c), device_id_type=MESH)
                cp.wait_send()
                cp.wait_recv()

    both = list(sums) + list(lands)
    outs = pl.pallas_call(
        body,
        name=name,
        in_specs=[HBM] * (4 * n) + [SEM, SEM, ANY],
        out_specs=[HBM] * (4 * n),
        out_shape=[pltpu.HBM(b.shape, b.dtype) for b in both],
        input_output_aliases={i: i for i in range(4 * n)},
        compiler_params=pltpu.CompilerParams(has_side_effects=EFFECT),
    )(*both, send_sems, recv_sems, after)
    return list(outs[:n]), [list(outs[n + 3 * a:n + 3 * a + 3]) for a in range(n)]


def _sibling_start(grads, name):
    n = len(grads)

    def body(*refs):
        ssem, rsem = refs[2 * n], refs[2 * n + 1]
        src, land = refs[2 * n + 2:3 * n + 2], refs[3 * n + 2:4 * n + 2]
        token = refs[4 * n + 2]
        x, y, c = _place()
        for a in range(n):
            pltpu.make_async_remote_copy(
                src_ref=src[a].at[:, :, pl.ds(1 - c, 1)], dst_ref=land[a], send_sem=ssem.at[a], recv_sem=rsem.at[a],
                device_id=(x, y, 1 - c), device_id_type=MESH).start()
        token[...] = jnp.zeros_like(token)

    lands = [lax.empty(g.shape[:2] + (1,) + g.shape[3:], g.dtype) for g in grads]
    both = list(grads) + lands
    outs = pl.pallas_call(
        body,
        name=name,
        in_specs=[HBM] * (2 * n),
        out_specs=[SEM, SEM] + [HBM] * (2 * n) + [TOKEN],
        out_shape=[pltpu.SemaphoreType.DMA((n,))] * 2 + [pltpu.HBM(b.shape, b.dtype) for b in both]
        + [jax.ShapeDtypeStruct((8, LANES), F32)],
        input_output_aliases={i: 2 + i for i in range(2 * n)},
        compiler_params=pltpu.CompilerParams(has_side_effects=EFFECT),
    )(*[_in_hbm(b) for b in both])
    return outs[0], outs[1], list(outs[2:2 + n]), list(outs[2 + n:2 + 2 * n]), outs[-1]


def _sibling_wait(send_sems, recv_sems, grads, lands, after, name):
    n = len(grads)

    def body(*refs):
        src, land = refs[:n], refs[n:2 * n]
        ssem, rsem = refs[2 * n], refs[2 * n + 1]
        x, y, c = _place()
        for a in range(n):
            cp = pltpu.make_async_remote_copy(
                src_ref=src[a].at[:, :, pl.ds(1 - c, 1)], dst_ref=land[a], send_sem=ssem.at[a], recv_sem=rsem.at[a],
                device_id=(x, y, 1 - c), device_id_type=MESH)
            cp.wait_send()
            cp.wait_recv()

    both = list(grads) + list(lands)
    outs = pl.pallas_call(
        body,
        name=name,
        in_specs=[HBM] * (2 * n) + [SEM, SEM, ANY],
        out_specs=[HBM] * (2 * n),
        out_shape=[pltpu.HBM(b.shape, b.dtype) for b in both],
        input_output_aliases={i: i for i in range(2 * n)},
        compiler_params=pltpu.CompilerParams(has_side_effects=EFFECT),
    )(*both, send_sems, recv_sems, after)
    return list(outs[:n]), list(outs[n:])


_SMALL = ("mix_norm", "q_norm", "k_norm", "sinks", "sgu_ln_g", "sgu_ln_b", "w_spatial", "b_spatial", "ffn_norm")


def _pack_rows(a):
    flat = a.reshape(-1)
    pad = (-flat.shape[0]) % LANES
    if pad:
        flat = jnp.pad(flat, (0, pad))
    return flat.reshape(-1, LANES)


def _pack(values):
    rows = jnp.concatenate([_pack_rows(values[k]) for k in _SMALL], axis=0)
    pad = (-rows.shape[0]) % 8
    if pad:
        rows = jnp.pad(rows, ((0, pad), (0, 0)))
    return rows


def _unpack(rows, like):
    out, at = {}, 0
    for k in _SMALL:
        size = like[k].size
        nrows = -(-size // LANES)
        out[k] = rows[at:at + nrows].reshape(-1)[:size].reshape(like[k].shape)
        at += nrows
    return out


def _rope_tables(t, wq, wk):
    pos = jnp.arange(t, dtype=F32)
    inv_freq = jnp.power(ROPE_THETA, -jnp.arange(0, HEAD_DIM, 2, dtype=F32) / HEAD_DIM)
    ang = pos[:, None] * inv_freq[None, :]
    cos, sin = jnp.cos(ang), jnp.sin(ang)
    cos2, sin2 = jnp.concatenate([cos, cos], axis=1), jnp.concatenate([-sin, sin], axis=1)
    return (jnp.tile(cos2, (1, wq // HEAD_DIM)), jnp.tile(sin2, (1, wq // HEAD_DIM)),
            jnp.tile(cos2, (1, wk // HEAD_DIM)), jnp.tile(sin2, (1, wk // HEAD_DIM)))


def kernel(x, mix_norm, w_in, q_norm, k_norm, sinks, sgu_ln_g, sgu_ln_b, w_spatial, b_spatial, w_attn_branch, w_sgu_branch, w_out, ffn_norm, w_gate, w_up, w_down, loss_target, m_mix_norm, m_w_in, m_q_norm, m_k_norm, m_sinks, m_sgu_ln_g, m_sgu_ln_b, m_w_spatial, m_b_spatial, m_w_attn_branch, m_w_sgu_branch, m_w_out, m_ffn_norm, m_w_gate, m_w_up, m_w_down, v_mix_norm, v_w_in, v_q_norm, v_k_norm, v_sinks, v_sgu_ln_g, v_sgu_ln_b, v_w_spatial, v_b_spatial, v_w_attn_branch, v_w_sgu_branch, v_w_out, v_ffn_norm, v_w_gate, v_w_up, v_w_down):
    names = ("mix_norm", "w_in", "q_norm", "k_norm", "sinks", "sgu_ln_g", "sgu_ln_b", "w_spatial", "b_spatial",
             "w_attn_branch", "w_sgu_branch", "w_out", "ffn_norm", "w_gate", "w_up", "w_down")
    weights = dict(zip(names, (mix_norm, w_in, q_norm, k_norm, sinks, sgu_ln_g, sgu_ln_b, w_spatial, b_spatial,
                               w_attn_branch, w_sgu_branch, w_out, ffn_norm, w_gate, w_up, w_down)))
    mom1 = dict(zip(names, (m_mix_norm, m_w_in, m_q_norm, m_k_norm, m_sinks, m_sgu_ln_g, m_sgu_ln_b, m_w_spatial,
                            m_b_spatial, m_w_attn_branch, m_w_sgu_branch, m_w_out, m_ffn_norm, m_w_gate, m_w_up,
                            m_w_down)))
    mom2 = dict(zip(names, (v_mix_norm, v_w_in, v_q_norm, v_k_norm, v_sinks, v_sgu_ln_g, v_sgu_ln_b, v_w_spatial,
                            v_b_spatial, v_w_attn_branch, v_w_sgu_branch, v_w_out, v_ffn_norm, v_w_gate, v_w_up,
                            v_w_down)))
    depth = w_in.shape[0]
    _, t, d = x.shape
    n_q_heads = sinks.shape[1]
    wq = n_q_heads * HEAD_DIM
    wk = wq // Q_PER_KV
    ws = sgu_ln_g.shape[1]
    ng = ws // LANES
    off_u = wq + 2 * wk
    off_g = off_u + 2 * ws
    tables = _rope_tables(t, wq, wk)
    px, py, pc = _place()
    core = pc.astype(jnp.int32)[None]
    chip = (2 * px + py).astype(jnp.int32)[None]
    dev = (4 * px + 2 * py + pc).astype(jnp.int32)[None]

    layers = range(depth)
    chunks = ((0,), (1, 2, 3), (4,), (5,))
    sources = [[jnp.swapaxes(w_in, 1, 2)], [jnp.swapaxes(w_attn_branch, 1, 2)], [jnp.swapaxes(w_sgu_branch, 1, 2)],
               [w_out], [jnp.swapaxes(w_gate, 1, 2), jnp.swapaxes(w_up, 1, 2)], [w_down]]
    stream = [(l, ci) for l in layers for ci in range(len(chunks))]
    placed, state, token = {}, {}, None

    def send(key, after):
        state[key] = _split_start(placed[key], GATHER_STAGES[0], "gather_send_%d_%d" % key, after)
        return state[key][3]

    def advance(key, after, stage):
        send_sems, recv_sems, bufs, _ = state[key]
        bufs = _split_wait(send_sems, recv_sems, bufs, GATHER_STAGES[stage - 1], after, "gather_wait%d_%d_%d" % (stage, *key))
        state[key] = _split_start(bufs, GATHER_STAGES[stage], "gather_pass%d_%d_%d" % (stage, *key))
        return state[key][3]

    def relay(key, after):
        tok = advance(key, after, 1)
        at = stream.index(key)
        for later in stream[at + 2:at + 3] if at else stream[1:3]:
            tok = send(later, tok)
        return tok

    def ready(key, after):
        send_sems, recv_sems, bufs, _ = state.pop(key)
        bufs = _split_wait(send_sems, recv_sems, bufs, GATHER_STAGES[2], after, "gather_wait3_%d_%d" % key)
        return [f.reshape(f.shape[0] * f.shape[1] * f.shape[2], f.shape[3]) for f in bufs]

    for key in stream:
        l, ci = key
        placed[key] = [_place_shard(sources[a], l, dev, BF16, f"place_shard_{l}_{a}",
                                    after=token if a == chunks[ci][0] else None) for a in chunks[ci]]
        token = send(key, None) if key == stream[0] else placed[key][-1]

    saved = []
    xl = x[0]
    going = relay((0, 0), token)
    going = advance((0, 0), going, 2)
    for l in layers:
        gq = jnp.tile(q_norm[l], n_q_heads)[None]
        gk = jnp.tile(k_norm[l], n_q_heads // Q_PER_KV)[None]
        bt = b_spatial[l].T
        h = _rmsnorm_fwd(xl, mix_norm[l][None], f"mix_norm_fwd_{l}", after=going)
        (win_t,) = ready((l, 0), h)
        proj = _mm(h, win_t, "nt", F32, f"in_proj_{l}")
        going = relay((l, 1), proj)
        attn = _attn_fwd(proj, tables, gq, gk, sinks[l], wq, wk, f"attn_fwd_{l}", after=going)
        going = advance((l, 1), attn, 2)
        sgu = _sgu_fwd(proj, sgu_ln_g[l][None], sgu_ln_b[l][None], w_spatial[l], bt, off_u, ws, f"sgu_fwd_{l}",
                       after=going)
        wab_t, wsb_t, wo = ready((l, 1), sgu)
        br_a, br_b, merged = _branches_fwd(attn, sgu, wab_t, wsb_t, proj, off_g, f"branches_{l}")
        going = relay((l, 2), merged)
        x1 = _mm(merged, wo, "nn", F32, f"out_proj_{l}", residual=xl, after=going)
        going = advance((l, 2), x1, 2)
        h2 = _rmsnorm_fwd(x1, ffn_norm[l][None], f"ffn_norm_fwd_{l}", after=going)
        (wgu_t,) = ready((l, 2), h2)
        going = relay((l, 3), h2)
        gu, act = _gate_up_fwd(h2, wgu_t, f"gate_up_{l}", after=going)
        going = advance((l, 3), act, 2)
        if l + 1 < depth:
            going = relay((l + 1, 0), going)
        (wd,) = ready((l, 3), going)
        x2 = _mm(act, wd, "nn", F32, f"down_proj_{l}", residual=x1)
        if l + 1 < depth:
            going = advance((l + 1, 0), x2, 2)
        saved.append(dict(x0=xl, h=h, proj=proj, attn=attn, sgu=sgu, br_a=br_a, br_b=br_b, merged=merged, x1=x1,
                          h2=h2, gu=gu, act=act, gq=gq, gk=gk, bt=bt, win_t=win_t, wab_t=wab_t, wsb_t=wsb_t, wo=wo,
                          wgu_t=wgu_t, wd=wd))
        xl = x2

    loss_part, dx, dx16 = _loss_and_grad(xl, loss_target[0], "loss")
    loss = lax.psum(loss_part[0, 0], ("x", "y", "c"))

    def sibling_start(grads, tag):
        shaped = []
        for g, p in grads:
            rows, c = g.shape
            shaped.append(g.reshape(p, 4, 2, rows // (8 * p), c))
        send_sems, recv_sems, shaped, lands, tok = _sibling_start(shaped, f"rs_sibling_start_{tag}")
        return (send_sems, recv_sems, shaped, lands, tag), tok

    def chips_start(state, after, first=None):
        send_sems, recv_sems, shaped, lands, tag = state
        shaped, lands = _sibling_wait(send_sems, recv_sems, shaped, lands, after, f"rs_sibling_wait_{tag}")
        sums = [_sum_sibling(g, o, core, f"rs_add_sibling_{tag}_{a}") for a, (g, o) in enumerate(zip(shaped, lands))]
        gate = None if first is None else first(sums[0])
        send_sems, recv_sems, sums, lands, tok = _chips_start(sums, f"rs_chips_start_{tag}", after=gate)
        return (send_sems, recv_sems, sums, lands, tag), tok

    def scatter_finish(state, after):
        send_sems, recv_sems, sums, lands, tag = state
        sums, lands = _chips_wait(send_sems, recv_sems, sums, lands, after, f"rs_chips_wait_{tag}")
        return [[s] + o for s, o in zip(sums, lands)]

    in_flight = [dict() for _ in layers]
    small_grads = [None] * depth
    tok, swap_in = None, None
    for l in reversed(layers):
        s = saved[l]
        dgu = _gate_up_bwd(dx16, s["wd"], s["gu"], f"d_gate_up_{l}", after=tok)
        if swap_in is not None:
            in_flight[l + 1]["in"], tok = chips_start(swap_in, dgu)
        g_wd = _mm(s["act"], dx16, "tn", BF16, f"g_w_down_{l}", after=tok)
        dh2 = _mm(dgu, s["wgu_t"], "nn", F32, f"d_h2_{l}", after=g_wd)
        g_wgu_t = _mm(dgu, s["h2"], "tn", BF16, f"g_w_gate_up_{l}", after=dh2)
        swap, tok_s = sibling_start([(g_wd, 1), (g_wgu_t, 2)], f"{l}_gate_up")
        dx1, dx1_16, g_ffn = _rmsnorm_bwd(s["x1"], ffn_norm[l][None], dh2, dx, f"ffn_norm_bwd_{l}", after=tok_s)
        d_a, d_b, dla, dlb = _branches_bwd(dx1_16, s["wo"], s["br_a"], s["br_b"], s["proj"], off_g,
                                           f"d_branches_{l}")
        in_flight[l]["gate_up"], tok = chips_start(swap, d_a)
        g_wo = _mm(s["merged"], dx1_16, "tn", BF16, f"g_w_out_{l}", after=tok)
        dattn = _mm(d_a, s["wab_t"], "nn", F32, f"d_attn_{l}", after=g_wo)
        g_wab_t = _mm(d_a, s["attn"], "tn", BF16, f"g_w_attn_branch_{l}")
        dsgu = _mm(d_b, s["wsb_t"], "nn", F32, f"d_sgu_{l}")
        g_wsb_t = _mm(d_b, s["sgu"], "tn", BF16, f"g_w_sgu_branch_{l}")
        swap, tok_s = sibling_start([(g_wab_t, 1), (g_wsb_t, 1), (g_wo, 1)], f"{l}_mix")
        dq, dk, dv, g_gq, g_gk, g_sinks = _attn_bwd(s["proj"], dattn, tables, s["gq"], s["gk"], sinks[l], wq, wk,
                                                    f"attn_bwd_{l}", after=tok_s)
        du, dvv, g_lng, g_lnb, g_ws, g_bs = _sgu_bwd(s["proj"], dsgu, sgu_ln_g[l][None], sgu_ln_b[l][None],
                                                     w_spatial[l], s["bt"], off_u, ws, f"sgu_bwd_{l}")
        dproj = jnp.concatenate([dq, dk.astype(BF16), dv.astype(BF16), du, dvv, dla, dlb], axis=1)
        dh = _mm(dproj, s["win_t"], "nn", F32, f"d_h_{l}")
        in_flight[l]["mix"], tok = chips_start(swap, dh)
        g_win_t = _mm(dproj, s["h"], "tn", BF16, f"g_w_in_{l}", after=tok)
        swap_in, tok = sibling_start([(g_win_t, 1)], f"{l}_in")
        dx, dx16, g_mix = _rmsnorm_bwd(s["x0"], mix_norm[l][None], dh, dx1, f"mix_norm_bwd_{l}", after=tok)
        small_grads[l] = dict(
            mix_norm=g_mix[0], q_norm=g_gq[0].reshape(n_q_heads, HEAD_DIM).sum(0),
            k_norm=g_gk[0].reshape(n_q_heads // Q_PER_KV, HEAD_DIM).sum(0), sinks=g_sinks[0, :n_q_heads],
            sgu_ln_g=g_lng[0], sgu_ln_b=g_lnb[0], w_spatial=g_ws, b_spatial=g_bs[:, 0, :], ffn_norm=g_ffn[0])
    grad_x = dx[None]

    result = {key: {} for key in ("grad", "delta", "m", "v")}
    layer_like = {k: weights[k][0] for k in _SMALL}
    packed_g = jnp.concatenate([_pack(small_grads[l]) for l in layers], axis=0)
    rows_per_layer = packed_g.shape[0] // depth
    small_buf = _place_shard([packed_g[None]], 0, dev, F32, "place_small_grads", after=tok)
    send_sems, recv_sems, small_bufs, tok = _split_start([small_buf], GATHER_STAGES[0], "gather_send_small")
    small_state = [(send_sems, recv_sems, small_bufs)]

    def small_stage(stage, after):
        ssem, rsem, bufs = small_state[0]
        bufs = _split_wait(ssem, rsem, bufs, GATHER_STAGES[stage - 1], after, f"gather_wait{stage}_small")
        ssem, rsem, bufs, token = _split_start(bufs, GATHER_STAGES[stage], f"gather_pass{stage}_small")
        small_state[0] = (ssem, rsem, bufs)
        return token

    in_flight[0]["in"], tok = chips_start(swap_in, tok, first=functools.partial(small_stage, 1))

    def update(k, grads, transposed, after):
        view = (lambda a: jnp.swapaxes(a, 1, 2)) if transposed else (lambda a: a)
        outs = _adam(view(weights[k]), grads, view(mom1[k]), view(mom2[k]), chip, f"adam_{k}", after=after)
        for key, val in zip(("grad", "delta", "m", "v"), outs):
            result[key][k] = view(val)
        return outs[3]

    def plain(terms, tag):
        s, lands = terms[0], terms[1:]
        g = _sum_chips(s, lands, chip, f"rs_add_chips_{tag}")
        return [jnp.swapaxes(g, 1, 2)[:, None]]

    gate_up = [scatter_finish(in_flight[l]["gate_up"], tok) for l in reversed(layers)][::-1]
    tok = update("w_down", [(gate_up[l][0], 0) for l in layers], False, None)
    tok = small_stage(2, tok)
    tok = update("w_gate", [(gate_up[l][1], 0) for l in layers], True, tok)
    tok = update("w_up", [(gate_up[l][1], 1) for l in layers], True, tok)
    mix =[scatter_finish(in_flight[l]["mix"], tok) for l in reversed(layers)][::-1]
    tok = update("w_out", [(mix[l][2], 0) for l in layers], False, None)
    tok = update("w_attn_branch", [(plain(mix[l][0], f"{l}_attn_branch"), 0) for l in layers], False, tok)
    tok = update("w_sgu_branch", [(plain(mix[l][1], f"{l}_sgu_branch"), 0) for l in layers], False, tok)

    packed = [jnp.concatenate([_pack({k: src[k][l] for k in _SMALL}) for l in layers], axis=0)
              for src in (weights, mom1, mom2)]
    (gathered_small,) = _split_wait(*small_state[0], GATHER_STAGES[2], tok, "gather_wait3_small")
    small = _small_reduce_adam(gathered_small[0], *packed, "small_reduce_adam")
    for key, rows in zip(("grad", "delta", "m", "v"), small):
        per_layer = [_unpack(rows[l * rows_per_layer:(l + 1) * rows_per_layer], layer_like) for l in layers]
        for k in _SMALL:
            result[key][k] = jnp.stack([per_layer[l][k] for l in layers])

    last = [scatter_finish(in_flight[l]["in"], small[0]) for l in reversed(layers)][::-1]
    update("w_in", [(last[l][0], 0) for l in layers], True, result["v"]["ffn_norm"])

    return (loss, grad_x, *[result["grad"][k] for k in names], *[result["delta"][k] for k in names],
            *[result["m"][k] for k in names], *[result["v"][k] for k in names])
```

```python
import functools
import math

import jax
import jax.numpy as jnp
from jax import lax
from jax.experimental import pallas as pl
from jax.experimental.pallas import tpu as pltpu

F32 = jnp.float32
BF16 = jnp.bfloat16
MESH = pl.DeviceIdType.MESH
ANY = pl.BlockSpec(memory_space=pl.ANY)

N_DEV = 8
HEAD_DIM = 64
Q_PER_KV = 4
BLOCK = 128
LANES = 128
ROPE_THETA = 10000.0
EPS = 1e-6
ADAM_LR = 0.001
ADAM_B1 = 0.9
ADAM_B2 = 0.999
ADAM_EPS = 1e-08
ADAM_WD = 0.01
ADAM_STEP = 10
NEG = -1e30
VMEM_LIMIT_BYTES = 56 * 1024 * 1024

NN = ((1,), (0,))
NT = ((1,), (1,))
TN = ((0,), (0,))


def _dot(a, b, dims):
    return lax.dot_general(a, b, (dims, ((), ())), preferred_element_type=F32)


def _params(*sem):
    return pltpu.CompilerParams(dimension_semantics=sem, vmem_limit_bytes=VMEM_LIMIT_BYTES)


def _divisor_tile(n, limit, unit):
    if n <= limit:
        return n
    best = unit
    for t in range(unit, limit + 1, unit):
        if n % t == 0:
            best = t
    assert n % best == 0, (n, limit, unit)
    return best


def _row_chunks(rows, size=256):
    size = min(size, rows)
    assert rows % size == 0, (rows, size)
    return [pl.ds(start, size) for start in range(0, rows, size)]


def _mm(a, b, mode, out_dtype, name, residual=None, after=None):
    parts = a.shape[0] if a.ndim == 3 else 1
    a2 = a.shape[-2:]
    if mode == "nn":
        (m, kp), (k2, n) = a2, b.shape
        k, mp = kp * parts, m
    elif mode == "nt":
        (m, kp), (n, k2) = a2, b.shape
        k, mp = kp * parts, m
    else:
        (k, mp), (k2, n) = a2, b.shape
        m, kp = mp * parts, k
    assert k == k2, (name, a.shape, b.shape)
    tk = _divisor_tile(kp, 2816, 128)
    nk = k // tk
    tm = _divisor_tile(mp, 512 if mode == "tn" else 1024, 128)
    tn = _divisor_tile(n, 2048 if mode == "tn" else 1024, 128)
    kpb, mpb = kp // tk, mp // tm
    dims = {"nn": NN, "nt": NT, "tn": TN}[mode]
    lead = (None,) if a.ndim == 3 else ()
    if mode == "tn":
        a_index = lambda i, j, kk: (i // mpb, kk, i % mpb) if lead else (kk, i)
        a_spec = pl.BlockSpec(lead + (tk, tm), a_index)
    else:
        a_index = lambda i, j, kk: (kk // kpb, i, kk % kpb) if lead else (i, kk)
        a_spec = pl.BlockSpec(lead + (tm, tk), a_index)
    if mode == "nt":
        b_spec = pl.BlockSpec((tn, tk), lambda i, j, kk: (j, kk))
    else:
        b_spec = pl.BlockSpec((tk, tn), lambda i, j, kk: (kk, j))
    o_spec = pl.BlockSpec((tm, tn), lambda i, j, kk: (i, j))
    has_res = residual is not None

    def body(*refs):
        a_ref, b_ref = refs[:2]
        r_ref = refs[2] if has_res else None
        o_ref, acc_ref = refs[-2:]
        kk = pl.program_id(2)
        p = _dot(a_ref[...], b_ref[...], dims)

        def finish(total):
            if has_res:
                total = total + r_ref[...]
            o_ref[...] = total.astype(o_ref.dtype)

        if nk == 1:
            finish(p)
        else:
            @pl.when(kk == 0)
            def _():
                acc_ref[...] = p

            @pl.when(jnp.logical_and(kk > 0, kk < nk - 1))
            def _():
                acc_ref[...] += p

            @pl.when(kk == nk - 1)
            def _():
                finish(acc_ref[...] + p)

    in_specs = [a_spec, b_spec] + ([o_spec] if has_res else []) + ([ANY] if after is not None else [])
    args = (a, b) + ((residual,) if has_res else ()) + ((after,) if after is not None else ())
    acc_shape = (tm, tn) if nk > 1 else (8, LANES)
    return pl.pallas_call(
        body,
        name=name,
        grid=(m // tm, n // tn, nk),
        in_specs=in_specs,
        out_specs=o_spec,
        out_shape=jax.ShapeDtypeStruct((m, n), out_dtype),
        scratch_shapes=[pltpu.VMEM(acc_shape, F32)],
        compiler_params=_params("parallel", "parallel", "arbitrary"),
    )(*args)


def _rmsnorm_fwd(x, g, name, after=None):
    t, d = x.shape
    tr = _divisor_tile(t, 512, 8)

    def body(x_ref, g_ref, *rest):
        h_ref = rest[-1]
        xv = x_ref[...]
        rstd = lax.rsqrt(jnp.mean(xv * xv, axis=-1, keepdims=True) + EPS)
        h_ref[...] = (xv * rstd * g_ref[...]).astype(h_ref.dtype)

    return pl.pallas_call(
        body,
        name=name,
        grid=(t // tr,),
        in_specs=[pl.BlockSpec((tr, d), lambda i: (i, 0)), pl.BlockSpec((1, d), lambda i: (0, 0))]
        + ([ANY] if after is not None else []),
        out_specs=pl.BlockSpec((tr, d), lambda i: (i, 0)),
        out_shape=jax.ShapeDtypeStruct((t, d), BF16),
        compiler_params=_params("parallel"),
    )(x, g, *(() if after is None else (after,)))


def _rmsnorm_bwd(x, g, dh, dres, name, after=None):
    t, d = x.shape
    tr = _divisor_tile(t, 256, 8)

    def body(x_ref, g_ref, dh_ref, dres_ref, *rest):
        dx_ref, dx16_ref, dg_ref = rest[-3:]
        i = pl.program_id(0)
        xv = x_ref[...]
        rstd = lax.rsqrt(jnp.mean(xv * xv, axis=-1, keepdims=True) + EPS)
        xh = xv * rstd
        dhv = dh_ref[...]
        dxh = dhv * g_ref[...]
        dx = dres_ref[...] + rstd * (dxh - xh * jnp.mean(dxh * xh, axis=-1, keepdims=True))
        dx_ref[...] = dx
        dx16_ref[...] = dx.astype(dx16_ref.dtype)
        part = jnp.broadcast_to(jnp.sum(dhv * xh, axis=0, keepdims=True), dg_ref.shape)

        @pl.when(i == 0)
        def _():
            dg_ref[...] = part

        @pl.when(i > 0)
        def _():
            dg_ref[...] += part

    row = pl.BlockSpec((tr, d), lambda i: (i, 0))
    return pl.pallas_call(
        body,
        name=name,
        grid=(t // tr,),
        in_specs=[row, pl.BlockSpec((1, d), lambda i: (0, 0)), row, row] + ([ANY] if after is not None else []),
        out_specs=[row, row, pl.BlockSpec((8, d), lambda i: (0, 0))],
        out_shape=[jax.ShapeDtypeStruct((t, d), F32), jax.ShapeDtypeStruct((t, d), BF16),
                   jax.ShapeDtypeStruct((8, d), F32)],
        compiler_params=_params("arbitrary"),
    )(x, g, dh, dres, *(() if after is None else (after,)))


def _lane(shape):
    return lax.broadcasted_iota(jnp.int32, shape, 1)


def _group_sum64(s):
    row = lax.broadcasted_iota(jnp.int32, (LANES, LANES), 0)
    col = lax.broadcasted_iota(jnp.int32, (LANES, LANES), 1)
    ones = jnp.where((row >= HEAD_DIM) == (col >= HEAD_DIM), 1.0, 0.0).astype(BF16)
    out = []
    for t in range(s.shape[1] // LANES):
        piece = s[:, LANES * t:LANES * t + LANES]
        hi = piece.astype(BF16)
        lo = (piece - hi.astype(F32)).astype(BF16)
        out.append(_dot(hi, ones, NN) + _dot(lo, ones, NN))
    return out[0] if len(out) == 1 else jnp.concatenate(out, axis=1)


def _swap32(x):
    w = x.shape[1]
    return jnp.where((_lane(x.shape) & 32) == 0, pltpu.roll(x, w - 32, axis=1), pltpu.roll(x, 32, axis=1))


def _rope(x, c, s):
    return x * c + _swap32(x) * s


def _rope_t(dy, c, s):
    return dy * c + _swap32(dy * s)


def _head_norm(x):
    rstd = lax.rsqrt(_group_sum64(x * x) * (1.0 / HEAD_DIM) + EPS)
    return x * rstd, rstd


def _head_norm_bwd(dxh, xh, rstd):
    return rstd * (dxh - xh * (_group_sum64(dxh * xh) * (1.0 / HEAD_DIM)))


def _roll64(x):
    return pltpu.roll(x, 64, axis=1)


def _attn_specs(wq, wk):
    kb = wq // wk
    prev = lambda i: jnp.maximum(i - 1, 0)
    return dict(
        q=pl.BlockSpec((BLOCK, wq), lambda i: (i, 0)),
        kc=pl.BlockSpec((BLOCK, wk), lambda i: (i, kb)),
        kp=pl.BlockSpec((BLOCK, wk), lambda i: (prev(i), kb)),
        vc=pl.BlockSpec((BLOCK, wk), lambda i: (i, kb + 1)),
        vp=pl.BlockSpec((BLOCK, wk), lambda i: (prev(i), kb + 1)),
        tq=pl.BlockSpec((BLOCK, wq), lambda i: (i, 0)),
        tkp=pl.BlockSpec((BLOCK, wk), lambda i: (prev(i), 0)),
        gq=pl.BlockSpec((1, wq), lambda i: (0, 0)),
        gk=pl.BlockSpec((1, wk), lambda i: (0, 0)),
        sinks=pl.BlockSpec(memory_space=pltpu.SMEM),
    )


def _attn_prologue(i, q_ref, kc_ref, kp_ref, cq_ref, sq_ref, ckp_ref, skp_ref, gq_ref, gk_ref):
    wk = kc_ref.shape[1]
    cq, sq = cq_ref[...], sq_ref[...]
    ck, sk = cq[:, :wk], sq[:, :wk]
    qh, q_rstd = _head_norm(q_ref[...])
    kch, kc_rstd = _head_norm(kc_ref[...])
    kph, kp_rstd = _head_norm(kp_ref[...])
    qn = _rope(qh * gq_ref[...], cq, sq)
    knc = _rope(kch * gk_ref[...], ck, sk)
    knp = _rope(kph * gk_ref[...], ckp_ref[...], skp_ref[...])
    stacked = (Q_PER_KV * BLOCK, BLOCK)
    row = lax.broadcasted_iota(jnp.int32, stacked, 0) & (BLOCK - 1)
    col = lax.broadcasted_iota(jnp.int32, stacked, 1)
    mask_c = col <= row
    valid = jnp.logical_or(mask_c, i > 0)
    half = (lax.broadcasted_iota(jnp.int32, (BLOCK, BLOCK), 1) >= HEAD_DIM).astype(jnp.int32)
    return dict(cq=cq, sq=sq, ck=ck, sk=sk, qh=qh, q_rstd=q_rstd, kch=kch, kc_rstd=kc_rstd, kph=kph,
                kp_rstd=kp_rstd, qn=qn, knc=knc, knp=knp, mask_c=mask_c, valid=valid, half=half)


def _stack_heads(x, g, half):
    kpar = g % 2
    pieces = []
    for j in range(Q_PER_KV):
        t, e = divmod(Q_PER_KV * g + j, 2)
        piece = jnp.where(half == e, x[:, LANES * t:LANES * t + LANES], 0.0)
        pieces.append(piece if e == kpar else _roll64(piece))
    return jnp.concatenate(pieces, axis=0)


def _unstack_heads(y, g, half):
    kpar = g % 2
    slabs = {}
    for j in range(Q_PER_KV):
        t, e = divmod(Q_PER_KV * g + j, 2)
        piece = jnp.where(half == kpar, y[BLOCK * j:BLOCK * j + BLOCK], 0.0)
        piece = piece if e == kpar else _roll64(piece)
        slabs[t] = piece if t not in slabs else slabs[t] + piece
    return slabs


def _group_scores(st, g, sinks_ref, scale):
    ks = g // 2
    sl = slice(LANES * ks, LANES * ks + LANES)
    q4 = _stack_heads(st["qn"], g, st["half"]).astype(BF16)
    kc, kp = st["knc"][:, sl].astype(BF16), st["knp"][:, sl].astype(BF16)
    rows = Q_PER_KV * BLOCK
    at = lax.broadcasted_iota(jnp.int32, (rows, 1), 0)
    head = jnp.zeros((rows, 1), jnp.int32)
    sink = jnp.zeros((rows, 1), F32) + sinks_ref[Q_PER_KV * g]
    for j in range(1, Q_PER_KV):
        head = jnp.where(at >= BLOCK * j, j, head)
        sink = jnp.where(at >= BLOCK * j, sinks_ref[Q_PER_KV * g + j], sink)
    cur = st["mask_c"]
    s = jnp.where(cur, _dot(q4, kc, NT), _dot(q4, kp, NT)) * scale
    s = jnp.where(st["valid"], s, NEG)
    m = jnp.maximum(jnp.max(s, axis=1, keepdims=True), sink)
    p = jnp.exp(s - m)
    p_s = jnp.exp(sink - m)
    inv = 1.0 / (jnp.sum(p, axis=1, keepdims=True) + p_s)
    return dict(sl=sl, head=head, q4=q4, kc=kc, kp=kp, cur=cur, pr=p * inv, pr_s=p_s * inv)


def _attn_fwd(proj, tables, gq, gk, sinks, wq, wk, name, after=None):
    t = proj.shape[0]
    nb = t // BLOCK
    sp = _attn_specs(wq, wk)
    scale = HEAD_DIM ** -0.5
    cos_t, sin_t, cos_k, sin_k = tables

    def body(sinks_ref, q_ref, kc_ref, kp_ref, vc_ref, vp_ref, cq_ref, sq_ref, ckp_ref, skp_ref, gq_ref, gk_ref,
             *rest):
        o_ref = rest[-1]
        i = pl.program_id(0)
        st = _attn_prologue(i, q_ref, kc_ref, kp_ref, cq_ref, sq_ref, ckp_ref, skp_ref, gq_ref, gk_ref)
        for g in range(wq // (Q_PER_KV * HEAD_DIM)):
            gs = _group_scores(st, g, sinks_ref, scale)
            own = st["half"] == g % 2
            vc = jnp.where(own, vc_ref[:, gs["sl"]], 0.0).astype(BF16)
            vp = jnp.where(own, vp_ref[:, gs["sl"]], 0.0).astype(BF16)
            pr = gs["pr"].astype(BF16)
            zero = jnp.zeros_like(pr)
            out = _dot(jnp.where(gs["cur"], pr, zero), vc, NN) + _dot(jnp.where(gs["cur"], zero, pr), vp, NN)
            for ts, slab in _unstack_heads(out, g, st["half"]).items():
                o_ref[:, LANES * ts:LANES * ts + LANES] = slab.astype(o_ref.dtype)

    return pl.pallas_call(
        body,
        name=name,
        grid=(nb,),
        in_specs=[sp["sinks"], sp["q"], sp["kc"], sp["kp"], sp["vc"], sp["vp"], sp["tq"], sp["tq"], sp["tkp"],
                  sp["tkp"], sp["gq"], sp["gk"]] + ([ANY] if after is not None else []),
        out_specs=pl.BlockSpec((BLOCK, wq), lambda i: (i, 0)),
        out_shape=jax.ShapeDtypeStruct((t, wq), BF16),
        compiler_params=_params("parallel"),
    )(sinks, proj, proj, proj, proj, proj, cos_t, sin_t, cos_k, sin_k, gq, gk, *(() if after is None else (after,)))


def _attn_bwd(proj, dout, tables, gq, gk, sinks, wq, wk, name, after=None):
    t = proj.shape[0]
    nb = t // BLOCK
    sp = _attn_specs(wq, wk)
    scale = HEAD_DIM ** -0.5
    cos_t, sin_t, cos_k, sin_k = tables

    def body(sinks_ref, q_ref, kc_ref, kp_ref, vc_ref, vp_ref, cq_ref, sq_ref, ckp_ref, skp_ref, gq_ref, gk_ref,
             do_ref, *rest):
        dq_ref, dk_ref, dv_ref, dgq_ref, dgk_ref, dsk_ref, dqn_ref, dknc_ref, dknp_ref, dvc_ref, dvp_ref = rest[-11:]
        i = pl.program_id(0)
        st = _attn_prologue(i, q_ref, kc_ref, kp_ref, cq_ref, sq_ref, ckp_ref, skp_ref, gq_ref, gk_ref)
        dknc_ref[...] = jnp.zeros_like(dknc_ref)
        dknp_ref[...] = jnp.zeros_like(dknp_ref)
        dvc_ref[...] = jnp.zeros_like(dvc_ref)
        dvp_ref[...] = jnp.zeros_like(dvp_ref)
        lane8 = _lane((8, LANES))
        dsinks = jnp.zeros((8, LANES), F32)
        for g in range(wq // (Q_PER_KV * HEAD_DIM)):
            gs = _group_scores(st, g, sinks_ref, scale)
            sl = gs["sl"]
            do4 = _stack_heads(do_ref[...], g, st["half"]).astype(BF16)
            cur, pr = gs["cur"], gs["pr"]
            dp = jnp.where(cur, _dot(do4, vc_ref[:, sl].astype(BF16), NT), _dot(do4, vp_ref[:, sl].astype(BF16), NT))
            rs = jnp.sum(pr * dp, axis=1, keepdims=True)
            ds = (pr * (dp - rs) * scale).astype(BF16)
            pr16 = pr.astype(BF16)
            zero = jnp.zeros_like(ds)
            ds_c, ds_p = jnp.where(cur, ds, zero), jnp.where(cur, zero, ds)
            pr_c, pr_p = jnp.where(cur, pr16, zero), jnp.where(cur, zero, pr16)
            dsink_rows = -gs["pr_s"] * rs
            for j in range(Q_PER_KV):
                dsink = jnp.sum(jnp.where(gs["head"] == j, dsink_rows, 0.0))
                dsinks = dsinks + jnp.where(lane8 == Q_PER_KV * g + j, dsink, 0.0)
            dq4 = _dot(ds_c, gs["kc"], NN) + _dot(ds_p, gs["kp"], NN)
            for ts, slab in _unstack_heads(dq4, g, st["half"]).items():
                dqn_ref[:, LANES * ts:LANES * ts + LANES] = slab
            dvc_ref[:, sl] += _dot(pr_c.astype(BF16), do4, TN)
            dvp_ref[:, sl] += _dot(pr_p.astype(BF16), do4, TN)
            dknc_ref[:, sl] += _dot(ds_c, gs["q4"], TN)
            dknp_ref[:, sl] += _dot(ds_p, gs["q4"], TN)

        gqv, gkv = gq_ref[...], gk_ref[...]
        dqg = _rope_t(dqn_ref[...], st["cq"], st["sq"])
        dq_ref[...] = _head_norm_bwd(dqg * gqv, st["qh"], st["q_rstd"]).astype(dq_ref.dtype)
        dkcg = _rope_t(dknc_ref[...], st["ck"], st["sk"])
        dkpg = _rope_t(dknp_ref[...], ckp_ref[...], skp_ref[...])
        dk_cur = _head_norm_bwd(dkcg * gkv, st["kch"], st["kc_rstd"])
        dk_prev = _head_norm_bwd(dkpg * gkv, st["kph"], st["kp_rstd"])
        dgq_part = jnp.broadcast_to(jnp.sum(dqg * st["qh"], axis=0, keepdims=True), dgq_ref.shape)
        dgk_part = jnp.broadcast_to(
            jnp.sum(dkcg * st["kch"] + dkpg * st["kph"], axis=0, keepdims=True), dgk_ref.shape)
        cur = pl.ds(pl.multiple_of(i * BLOCK, BLOCK), BLOCK)
        dk_ref[cur, :] = dk_cur
        dv_ref[cur, :] = dvc_ref[...]

        @pl.when(i == 0)
        def _():
            dgq_ref[...] = dgq_part
            dgk_ref[...] = dgk_part
            dsk_ref[...] = dsinks

        @pl.when(i > 0)
        def _():
            before = pl.ds(pl.multiple_of((i - 1) * BLOCK, BLOCK), BLOCK)
            dk_ref[before, :] += dk_prev
            dv_ref[before, :] += dvp_ref[...]
            dgq_ref[...] += dgq_part
            dgk_ref[...] += dgk_part
            dsk_ref[...] += dsinks

    whole = lambda shape: pl.BlockSpec(shape, lambda i: (0, 0))
    return pl.pallas_call(
        body,
        name=name,
        grid=(nb,),
        in_specs=[sp["sinks"], sp["q"], sp["kc"], sp["kp"], sp["vc"], sp["vp"], sp["tq"], sp["tq"], sp["tkp"],
                  sp["tkp"], sp["gq"], sp["gk"], pl.BlockSpec((BLOCK, wq), lambda i: (i, 0))]
        + ([ANY] if after is not None else []),
        out_specs=[pl.BlockSpec((BLOCK, wq), lambda i: (i, 0)), whole((t, wk)), whole((t, wk)), whole((8, wq)),
                   whole((8, wk)), whole((8, LANES))],
        out_shape=[jax.ShapeDtypeStruct((t, wq), BF16), jax.ShapeDtypeStruct((t, wk), F32),
                   jax.ShapeDtypeStruct((t, wk), F32), jax.ShapeDtypeStruct((8, wq), F32),
                   jax.ShapeDtypeStruct((8, wk), F32), jax.ShapeDtypeStruct((8, LANES), F32)],
        scratch_shapes=[pltpu.VMEM((BLOCK, wq), F32), pltpu.VMEM((BLOCK, wk), F32), pltpu.VMEM((BLOCK, wk), F32),
                        pltpu.VMEM((BLOCK, wk), F32), pltpu.VMEM((BLOCK, wk), F32)],
        compiler_params=_params("arbitrary"),
    )(sinks, proj, proj, proj, proj, proj, cos_t, sin_t, cos_k, sin_k, gq, gk, dout,
      *(() if after is None else (after,)))


_GELU_K = math.sqrt(2.0 / math.pi)
_GELU_A = 0.044715


def _gelu(x):
    return 0.5 * x * (1.0 + jnp.tanh(_GELU_K * (x + _GELU_A * x * x * x)))


def _gelu_and_grad(x):
    th = jnp.tanh(_GELU_K * (x + _GELU_A * x * x * x))
    return (0.5 * x * (1.0 + th),
            0.5 * (1.0 + th) + 0.5 * x * (1.0 - th * th) * (_GELU_K * (1.0 + 3.0 * _GELU_A * x * x)))


def _group_ln(v):
    mu = jnp.mean(v, axis=1, keepdims=True)
    cen = v - mu
    rstd = lax.rsqrt(jnp.mean(cen * cen, axis=1, keepdims=True) + EPS)
    return cen * rstd, rstd


def _sgu_geometry(off_u, ws):
    cw = math.gcd(off_u, ws)
    return cw, ws // cw, off_u // cw, (off_u + ws) // cw


def _sgu_fwd(proj, ln_g, ln_b, w_s, bt, off_u, ws, name, after=None):
    t = proj.shape[0]
    nb = t // BLOCK
    cw, nc, ub, vb = _sgu_geometry(off_u, ws)
    gpc = cw // LANES
    ng = ws // LANES

    def body(u_ref, v_ref, g_ref, b_ref, w_ref, bt_ref, *rest):
        o_ref = rest[-1]
        jc = pl.program_id(0)
        row = lax.broadcasted_iota(jnp.int32, (BLOCK, BLOCK), 0)
        col = lax.broadcasted_iota(jnp.int32, (BLOCK, BLOCK), 1)
        lane_g = _lane((BLOCK, ng))
        for gi in range(gpc):
            sl = slice(LANES * gi, LANES * gi + LANES)
            xh, _ = _group_ln(_gelu(v_ref[:, sl]))
            vn = xh * g_ref[:, sl] + b_ref[:, sl]
            w = jnp.where(row >= col, w_ref[gi], 0.0).astype(BF16)
            bias = jnp.sum(jnp.where(lane_g == jc * gpc + gi, bt_ref[...], 0.0), axis=1, keepdims=True)
            s = _dot(w, vn.astype(BF16), NN) + bias
            o_ref[:, sl] = (_gelu(u_ref[:, sl]) * s).astype(o_ref.dtype)

    return pl.pallas_call(
        body,
        name=name,
        grid=(nc, nb),
        in_specs=[pl.BlockSpec((BLOCK, cw), lambda jc, i: (i, ub + jc)),
                  pl.BlockSpec((BLOCK, cw), lambda jc, i: (i, vb + jc)),
                  pl.BlockSpec((1, cw), lambda jc, i: (0, jc)),
                  pl.BlockSpec((1, cw), lambda jc, i: (0, jc)),
                  pl.BlockSpec((gpc, BLOCK, BLOCK), lambda jc, i: (jc, 0, 0)),
                  pl.BlockSpec((BLOCK, ng), lambda jc, i: (0, 0))] + ([ANY] if after is not None else []),
        out_specs=pl.BlockSpec((BLOCK, cw), lambda jc, i: (i, jc)),
        out_shape=jax.ShapeDtypeStruct((t, ws), BF16),
        compiler_params=_params("parallel", "parallel"),
    )(proj, proj, ln_g, ln_b, w_s, bt, *(() if after is None else (after,)))


def _sgu_bwd(proj, dout, ln_g, ln_b, w_s, bt, off_u, ws, name):
    t = proj.shape[0]
    nb = t // BLOCK
    cw, nc, ub, vb = _sgu_geometry(off_u, ws)
    gpc = cw // LANES
    ng = ws // LANES

    def body(u_ref, v_ref, g_ref, b_ref, w_ref, bt_ref, do_ref, du_ref, dv_ref, dg_ref, db_ref, dw_ref, dbs_ref,
             bacc_ref):
        jc = pl.program_id(0)
        i = pl.program_id(1)
        row = lax.broadcasted_iota(jnp.int32, (BLOCK, BLOCK), 0)
        col = lax.broadcasted_iota(jnp.int32, (BLOCK, BLOCK), 1)
        lane_g = _lane((BLOCK, ng))
        tri = row >= col

        @pl.when(i == 0)
        def _():
            dg_ref[...] = jnp.zeros_like(dg_ref)
            db_ref[...] = jnp.zeros_like(db_ref)
            dw_ref[...] = jnp.zeros_like(dw_ref)
            bacc_ref[...] = jnp.zeros_like(bacc_ref)

        for gi in range(gpc):
            sl = slice(LANES * gi, LANES * gi + LANES)
            u_raw, v_raw = u_ref[:, sl], v_ref[:, sl]
            u_act, u_slope = _gelu_and_grad(u_raw)
            v_act, v_slope = _gelu_and_grad(v_raw)
            xh, rstd = _group_ln(v_act)
            gam = g_ref[:, sl]
            vn = (xh * gam + b_ref[:, sl]).astype(BF16)
            w = jnp.where(tri, w_ref[gi], 0.0)
            bias = jnp.sum(jnp.where(lane_g == jc * gpc + gi, bt_ref[...], 0.0), axis=1, keepdims=True)
            s = _dot(w.astype(BF16), vn, NN) + bias
            dov = do_ref[:, sl]
            du_ref[:, sl] = (dov * s * u_slope).astype(du_ref.dtype)
            ds = dov * u_act
            ds16 = ds.astype(BF16)
            dw_ref[gi] += jnp.where(tri, _dot(ds16, vn, NT), 0.0)
            bacc_ref[gi] += ds
            dvn = _dot(w.T.astype(BF16), ds16, NN)
            dg_ref[:, sl] += jnp.broadcast_to(jnp.sum(dvn * xh, axis=0, keepdims=True), (8, LANES))
            db_ref[:, sl] += jnp.broadcast_to(jnp.sum(dvn, axis=0, keepdims=True), (8, LANES))
            dxh = dvn * gam
            dvg = rstd * (dxh - jnp.mean(dxh, axis=1, keepdims=True)
                          - xh * jnp.mean(dxh * xh, axis=1, keepdims=True))
            dv_ref[:, sl] = (dvg * v_slope).astype(dv_ref.dtype)

        @pl.when(i == nb - 1)
        def _():
            for gi in range(gpc):
                dbs_ref[gi] = jnp.broadcast_to(jnp.sum(bacc_ref[gi].T, axis=0, keepdims=True), (8, LANES))

    blk = lambda base: pl.BlockSpec((BLOCK, cw), lambda jc, i: (i, base + jc))
    vec = pl.BlockSpec((1, cw), lambda jc, i: (0, jc))
    acc = pl.BlockSpec((8, cw), lambda jc, i: (0, jc))
    wsp = pl.BlockSpec((gpc, BLOCK, BLOCK), lambda jc, i: (jc, 0, 0))
    return pl.pallas_call(
        body,
        name=name,
        grid=(nc, nb),
        in_specs=[blk(ub), blk(vb), vec, vec, wsp, pl.BlockSpec((BLOCK, ng), lambda jc, i: (0, 0)), blk(0)],
        out_specs=[blk(0), blk(0), acc, acc, wsp, pl.BlockSpec((gpc, 8, LANES), lambda jc, i: (jc, 0, 0))],
        out_shape=[jax.ShapeDtypeStruct((t, ws), BF16), jax.ShapeDtypeStruct((t, ws), BF16),
                   jax.ShapeDtypeStruct((8, ws), F32), jax.ShapeDtypeStruct((8, ws), F32),
                   jax.ShapeDtypeStruct((ng, BLOCK, BLOCK), F32), jax.ShapeDtypeStruct((ng, 8, LANES), F32)],
        scratch_shapes=[pltpu.VMEM((gpc, BLOCK, BLOCK), F32)],
        compiler_params=_params("arbitrary", "arbitrary"),
    )(proj, proj, ln_g, ln_b, w_s, bt, dout)


def _sigmoid(x):
    return 1.0 / (1.0 + jnp.exp(-x))


def _merge_geometry(off_g, d):
    cw = math.gcd(off_g, d)
    return cw, d // cw, off_g // cw, (off_g + d) // cw


def _branches_fwd(attn, sgu, wab_t, wsb_t, proj, off_g, name):
    t = attn.shape[0]
    d = wab_t.shape[0]
    tn, _, ab, bb = _merge_geometry(off_g, d)
    tm = _divisor_tile(t, 1024, 128)

    def body(a1_ref, a2_ref, b1_ref, b2_ref, la_ref, lb_ref, bra_ref, brb_ref, o_ref):
        for rows in _row_chunks(tm):
            va = _dot(a1_ref[rows, :], b1_ref[...], NT)
            vb = _dot(a2_ref[rows, :], b2_ref[...], NT)
            bra_ref[rows, :] = va
            brb_ref[rows, :] = vb
            o_ref[rows, :] = (_sigmoid(la_ref[rows, :]) * va + _sigmoid(lb_ref[rows, :]) * vb).astype(o_ref.dtype)

    rows = lambda w: pl.BlockSpec((tm, w), lambda i, j: (i, 0))
    wrow = lambda w: pl.BlockSpec((tn, w), lambda i, j: (j, 0))
    blk = lambda base: pl.BlockSpec((tm, tn), lambda i, j: (i, base + j))
    return pl.pallas_call(
        body,
        name=name,
        grid=(t // tm, d // tn),
        in_specs=[rows(attn.shape[1]), rows(sgu.shape[1]), wrow(wab_t.shape[1]), wrow(wsb_t.shape[1]), blk(ab),
                  blk(bb)],
        out_specs=[blk(0)] * 3,
        out_shape=[jax.ShapeDtypeStruct((t, d), F32), jax.ShapeDtypeStruct((t, d), F32),
                   jax.ShapeDtypeStruct((t, d), BF16)],
        compiler_params=_params("parallel", "parallel"),
    )(attn, sgu, wab_t, wsb_t, proj, proj)


def _branches_bwd(dx16, wo, br_a, br_b, proj, off_g, name, after=None):
    t, d = br_a.shape
    tn, _, ab, bb = _merge_geometry(off_g, d)
    tm = _divisor_tile(t, 1024, 128)
    k = dx16.shape[1]

    def body(a_ref, b_ref, bra_ref, brb_ref, la_ref, lb_ref, *rest):
        da_ref, db_ref, dla_ref, dlb_ref = rest[-4:]
        for rows in _row_chunks(tm):
            dmv = _dot(a_ref[rows, :], b_ref[...], NT)
            ga, gb = _sigmoid(la_ref[rows, :]), _sigmoid(lb_ref[rows, :])
            da_ref[rows, :] = (dmv * ga).astype(da_ref.dtype)
            db_ref[rows, :] = (dmv * gb).astype(db_ref.dtype)
            dla_ref[rows, :] = (dmv * bra_ref[rows, :] * ga * (1.0 - ga)).astype(dla_ref.dtype)
            dlb_ref[rows, :] = (dmv * brb_ref[rows, :] * gb * (1.0 - gb)).astype(dlb_ref.dtype)

    blk = lambda base: pl.BlockSpec((tm, tn), lambda i, j: (i, base + j))
    return pl.pallas_call(
        body,
        name=name,
        grid=(t // tm, d // tn),
        in_specs=[pl.BlockSpec((tm, k), lambda i, j: (i, 0)), pl.BlockSpec((tn, k), lambda i, j: (j, 0)), blk(0),
                  blk(0), blk(ab), blk(bb)] + ([ANY] if after is not None else []),
        out_specs=[blk(0)] * 4,
        out_shape=[jax.ShapeDtypeStruct((t, d), BF16)] * 4,
        compiler_params=_params("parallel", "parallel"),
    )(dx16, wo, br_a, br_b, proj, proj, *(() if after is None else (after,)))


def _gate_up_fwd(h2, wgu_t, name, after=None):
    t, d = h2.shape
    f = wgu_t.shape[0] // 2
    tm = _divisor_tile(t, 1024, 128)
    tn = _divisor_tile(f, 512, 128)
    nb = f // tn

    def body(a_ref, bg_ref, bu_ref, *rest):
        gu_ref, act_ref = rest[-2:]
        for rows in _row_chunks(tm):
            av = a_ref[rows, :]
            gv = _dot(av, bg_ref[...], NT)
            uv = _dot(av, bu_ref[...], NT)
            gu_ref[0, rows, :] = gv
            gu_ref[1, rows, :] = uv
            act_ref[rows, :] = (gv * _sigmoid(gv) * uv).astype(act_ref.dtype)

    return pl.pallas_call(
        body,
        name=name,
        grid=(t // tm, nb),
        in_specs=[pl.BlockSpec((tm, d), lambda i, j: (i, 0)), pl.BlockSpec((tn, d), lambda i, j: (j, 0)),
                  pl.BlockSpec((tn, d), lambda i, j: (j + nb, 0))] + ([ANY] if after is not None else []),
        out_specs=[pl.BlockSpec((2, tm, tn), lambda i, j: (0, i, j)), pl.BlockSpec((tm, tn), lambda i, j: (i, j))],
        out_shape=[jax.ShapeDtypeStruct((2, t, f), F32), jax.ShapeDtypeStruct((t, f), BF16)],
        compiler_params=_params("parallel", "parallel"),
    )(h2, wgu_t, wgu_t, *(() if after is None else (after,)))


def _gate_up_bwd(dx16, wd, gu, name, after=None):
    t, d = dx16.shape
    f = wd.shape[0]
    tm = _divisor_tile(t, 1024, 128)
    tn = _divisor_tile(f, 512, 128)

    def body(a_ref, b_ref, gu_ref, *rest):
        o_ref = rest[-1]
        for rows in _row_chunks(tm):
            dav = _dot(a_ref[rows, :], b_ref[...], NT)
            gv = gu_ref[0, rows, :]
            sg = _sigmoid(gv)
            o_ref[0, rows, :] = (dav * gu_ref[1, rows, :] * (sg + gv * sg * (1.0 - sg))).astype(o_ref.dtype)
            o_ref[1, rows, :] = (dav * gv * sg).astype(o_ref.dtype)

    pair = pl.BlockSpec((2, tm, tn), lambda i, j: (0, i, j))
    return pl.pallas_call(
        body,
        name=name,
        grid=(t // tm, f // tn),
        in_specs=[pl.BlockSpec((tm, d), lambda i, j: (i, 0)), pl.BlockSpec((tn, d), lambda i, j: (j, 0)), pair]
        + ([ANY] if after is not None else []),
        out_specs=pair,
        out_shape=jax.ShapeDtypeStruct((2, t, f), BF16),
        compiler_params=_params("parallel", "parallel"),
    )(dx16, wd, gu, *(() if after is None else (after,)))


def _loss_and_grad(y, target, name):
    t, d = y.shape
    tr = _divisor_tile(t, 512, 8)

    def body(y_ref, t_ref, l_ref, dy_ref, dy16_ref):
        i = pl.program_id(0)
        err = y_ref[...] - t_ref[...]
        dy_ref[...] = err * (1.0 / d)
        dy16_ref[...] = (err * (1.0 / d)).astype(dy16_ref.dtype)
        part = jnp.broadcast_to(0.5 * jnp.sum(err * err) * (1.0 / d), l_ref.shape)

        @pl.when(i == 0)
        def _():
            l_ref[...] = part

        @pl.when(i > 0)
        def _():
            l_ref[...] += part

    row = pl.BlockSpec((tr, d), lambda i: (i, 0))
    return pl.pallas_call(
        body,
        name=name,
        grid=(t // tr,),
        in_specs=[row, row],
        out_specs=[pl.BlockSpec((8, LANES), lambda i: (0, 0)), row, row],
        out_shape=[jax.ShapeDtypeStruct((8, LANES), F32), jax.ShapeDtypeStruct((t, d), F32),
                   jax.ShapeDtypeStruct((t, d), BF16)],
        compiler_params=_params("arbitrary"),
    )(y, target)


def _adam_math(w, g, m, v):
    m = ADAM_B1 * m + (1.0 - ADAM_B1) * g
    v = ADAM_B2 * v + (1.0 - ADAM_B2) * (g * g)
    m_hat = m / (1.0 - ADAM_B1 ** ADAM_STEP)
    v_hat = v / (1.0 - ADAM_B2 ** ADAM_STEP)
    delta = -ADAM_LR * (m_hat / (jnp.sqrt(v_hat) + ADAM_EPS) + ADAM_WD * w)
    return delta, m, v


def _row_tile(r, c, elems=512 * 1024):
    return _divisor_tile(r, max(8, elems // c // 8 * 8), 8)


def _adam(w, grads, m, v, chip, name, after=None):
    nl, r, c = w.shape
    tr = _row_tile(r, c, 384 * 1024)
    nb = r // tr
    counts = [len(terms) for terms, _ in grads]

    def body(chip_ref, *refs):
        w_ref, m_ref, v_ref = refs[:3]
        g_ref, d_ref, nm_ref, nv_ref = refs[-4:]
        layer = pl.program_id(0)
        g, at = None, 3
        for li, n in enumerate(counts):
            total = refs[at][...].astype(F32)
            for ref in refs[at + 1:at + n]:
                total = total + ref[...].astype(F32)
            g = total if g is None else jnp.where(layer == li, total, g)
            at += n
        g_ref[...] = g
        d_ref[...], nm_ref[...], nv_ref[...] = _adam_math(w_ref[...], g, m_ref[...], v_ref[...])

    def term_spec(li, p, by_owner):
        def index(l, i, chip_ref):
            rows = jnp.where(l < li, 0, jnp.where(l > li, nb - 1, i))
            return (p, chip_ref[0] if by_owner else 0, rows, 0)
        return pl.BlockSpec((None, None, tr, c), index)

    row = pl.BlockSpec((None, tr, c), lambda l, i, chip_ref: (l, i, 0))
    specs, arrays = [], []
    for li, (terms, p) in enumerate(grads):
        for term in terms:
            specs.append(term_spec(li, p, term.shape[1] == 4))
            arrays.append(term)
    return pl.pallas_call(
        body,
        name=name,
        grid_spec=pltpu.PrefetchScalarGridSpec(
            num_scalar_prefetch=1, grid=(nl, nb),
            in_specs=[row] * 3 + specs + ([ANY] if after is not None else []), out_specs=[row] * 4),
        out_shape=[jax.ShapeDtypeStruct((nl, r, c), F32)] * 4,
        compiler_params=_params("arbitrary", "arbitrary"),
    )(chip, w, m, v, *arrays, *(() if after is None else (after,)))


def _place_shard(parts, layer, dev, out_dtype, name, after=None):
    p = len(parts)
    _, r, c = parts[0].shape
    tr = _row_tile(r, c)

    def body(dev_ref, *refs):
        o_ref = refs[-1]
        x = refs[0][...]
        for pi in range(1, p):
            x = jnp.where(pl.program_id(0) == pi, refs[pi][...], x)
        o_ref[...] = x.astype(o_ref.dtype)

    return pl.pallas_call(
        body,
        name=name,
        grid_spec=pltpu.PrefetchScalarGridSpec(
            num_scalar_prefetch=1,
            grid=(p, r // tr),
            in_specs=[pl.BlockSpec((None, tr, c), lambda pi, i, dev_ref: (layer, i, 0))] * p
            + ([ANY] if after is not None else []),
            out_specs=pl.BlockSpec((None, None, tr, c), lambda pi, i, dev_ref: (pi, dev_ref[0], i, 0)),
        ),
        out_shape=jax.ShapeDtypeStruct((p, N_DEV, r, c), out_dtype),
        compiler_params=_params("parallel", "parallel"),
    )(dev, *parts, *(() if after is None else (after,)))


def _sum_sibling(g, land, core, name):
    p, _, _, r, c = g.shape
    tr = _row_tile(r, c, 1024 * 1024)

    def body(core_ref, g_ref, l_ref, o_ref):
        o_ref[...] = (g_ref[...].astype(F32) + l_ref[...].astype(F32)).astype(o_ref.dtype)

    return pl.pallas_call(
        body,
        name=name,
        grid_spec=pltpu.PrefetchScalarGridSpec(
            num_scalar_prefetch=1,
            grid=(p, 4, r // tr),
            in_specs=[pl.BlockSpec((None, None, None, tr, c), lambda pi, q, i, core_ref: (pi, q, core_ref[0], i, 0)),
                      pl.BlockSpec((None, None, None, tr, c), lambda pi, q, i, core_ref: (pi, q, 0, i, 0))],
            out_specs=pl.BlockSpec((None, None, tr, c), lambda pi, q, i, core_ref: (pi, q, i, 0)),
        ),
        out_shape=jax.ShapeDtypeStruct((p, 4, r, c), BF16),
        compiler_params=_params("parallel", "parallel", "parallel"),
    )(core, g, land)


def _sum_chips(s, lands, chip, name):
    p, _, r, c = s.shape
    tr = _row_tile(r, c)

    def body(chip_ref, s_ref, l0_ref, l1_ref, l2_ref, o_ref):
        total = s_ref[...].astype(F32) + l0_ref[...].astype(F32)
        o_ref[...] = total + l1_ref[...].astype(F32) + l2_ref[...].astype(F32)

    land_spec = pl.BlockSpec((None, None, tr, c), lambda pi, i, chip_ref: (pi, 0, i, 0))
    return pl.pallas_call(
        body,
        name=name,
        grid_spec=pltpu.PrefetchScalarGridSpec(
            num_scalar_prefetch=1,
            grid=(p, r // tr),
            in_specs=[pl.BlockSpec((None, None, tr, c), lambda pi, i, chip_ref: (pi, chip_ref[0], i, 0)),
                      land_spec, land_spec, land_spec],
            out_specs=pl.BlockSpec((None, tr, c), lambda pi, i, chip_ref: (pi, i, 0)),
        ),
        out_shape=jax.ShapeDtypeStruct((p, r, c), F32),
        compiler_params=_params("parallel", "parallel"),
    )(chip, s, *lands)


def _small_reduce_adam(gathered, w, m, v, name):
    _, r, c = gathered.shape
    tr = _row_tile(r, c)

    def body(p_ref, w_ref, m_ref, v_ref, g_ref, d_ref, nm_ref, nv_ref):
        g = p_ref[0]
        for j in range(1, N_DEV):
            g = g + p_ref[j]
        g_ref[...] = g
        d_ref[...], nm_ref[...], nv_ref[...] = _adam_math(w_ref[...], g, m_ref[...], v_ref[...])

    row = pl.BlockSpec((tr, c), lambda i: (i, 0))
    return pl.pallas_call(
        body,
        name=name,
        grid=(r // tr,),
        in_specs=[pl.BlockSpec((N_DEV, tr, c), lambda i: (0, i, 0)), row, row, row],
        out_specs=[row] * 4,
        out_shape=[jax.ShapeDtypeStruct((r, c), F32)] * 4,
        compiler_params=_params("parallel"),
    )(gathered, w, m, v)


def _place():
    return lax.axis_index("x"), lax.axis_index("y"), lax.axis_index("c")


HBM =pl.BlockSpec(memory_space=pltpu.HBM)
SEM = pl.BlockSpec(memory_space=pltpu.SEMAPHORE)
TOKEN = pl.BlockSpec(memory_space=pltpu.VMEM)
EFFECT = pltpu.SideEffectType.DATAFLOW_SIDE_EFFECTING


def _in_hbm(a):
    return pltpu.with_memory_space_constraint(a, pltpu.HBM)


_FLIPS = {"me": (0, 0, 0), "s": (0, 0, 1), "x": (1, 0, 0), "y": (0, 1, 0), "d": (1, 1, 0)}
GATHER_STAGES = (
    (("s", "me", "all"), ("x", "me", "all"), ("y", "me", "all")),
    (("s", "x", "all"), ("s", "y", "all"), ("y", "x", "first"), ("x", "y", "second")),
    (("s", "d", "all"),),
)


def _flipped(place, *names):
    out = list(place)
    for name in names:
        out = [1 - p if f else p for p, f in zip(out, _FLIPS[name])]
    return tuple(out)


def _block_part(ref, place, part):
    px, py, pc = place
    rows = ref.shape[2]
    span = {"all": pl.ds(0, rows), "first": pl.ds(0, rows // 2), "second": pl.ds(rows // 2, rows // 2)}[part]
    return ref.at[:, pl.ds(4 * px + 2 * py + pc, 1), span]


def _split_start(bufs, moves, name, after=None):
    n, nm = len(bufs), len(moves)
    extra = 0 if after is None else 1

    def body(*refs):
        ssem, rsem = refs[n + extra], refs[n + extra + 1]
        outs, token = refs[n + extra + 2:2 * n + extra + 2], refs[2 * n + extra + 2]
        me = _place()
        for a in range(n):
            for k, (to, owner, part) in enumerate(moves):
                piece = _block_part(outs[a], _flipped(me, owner), part)
                pltpu.make_async_remote_copy(
                    src_ref=piece, dst_ref=piece, send_sem=ssem.at[nm * a + k], recv_sem=rsem.at[nm * a + k],
                    device_id=_flipped(me, to), device_id_type=MESH).start()
        token[...] = jnp.zeros_like(token)

    outs = pl.pallas_call(
        body,
        name=name,
        in_specs=[HBM] * n + [ANY] * extra,
        out_specs=[SEM, SEM] + [HBM] * n + [TOKEN],
        out_shape=[pltpu.SemaphoreType.DMA((nm * n,))] * 2 + [pltpu.HBM(b.shape, b.dtype) for b in bufs]
        + [jax.ShapeDtypeStruct((8, LANES), F32)],
        input_output_aliases={i: 2 + i for i in range(n)},
        compiler_params=pltpu.CompilerParams(has_side_effects=EFFECT),
    )(*[_in_hbm(b) for b in bufs], *(() if after is None else (after,)))
    return outs[0], outs[1], list(outs[2:2 + n]), outs[-1]


def _split_wait(send_sems, recv_sems, bufs, moves, after, name):
    n, nm = len(bufs), len(moves)

    def body(*refs):
        ins, ssem, rsem = refs[:n], refs[n], refs[n + 1]
        me = _place()
        for a in range(n):
            for k, (to, owner, part) in enumerate(moves):
                landed = _block_part(ins[a], _flipped(me, owner, to), part)
                cp = pltpu.make_async_remote_copy(
                    src_ref=landed, dst_ref=landed, send_sem=ssem.at[nm * a + k], recv_sem=rsem.at[nm * a + k],
                    device_id=_flipped(me, to), device_id_type=MESH)
                cp.wait_send()
                cp.wait_recv()

    return pl.pallas_call(
        body,
        name=name,
        in_specs=[HBM] * n + [SEM, SEM, ANY],
        out_specs=[HBM] * n,
        out_shape=[pltpu.HBM(b.shape, b.dtype) for b in bufs],
        input_output_aliases={i: i for i in range(n)},
        compiler_params=pltpu.CompilerParams(has_side_effects=EFFECT),
    )(*bufs, send_sems, recv_sems, after)


def _chips_start(sums, name, after=None):
    n = len(sums)
    extra = 0 if after is None else 1

    def body(*refs):
        refs = refs[:4 * n] + refs[4 * n + extra:]
        ssem, rsem = refs[4 * n], refs[4 * n + 1]
        src, land = refs[4 * n + 2:5 * n + 2], refs[5 * n + 2:8 * n + 2]
        token = refs[8 * n + 2]
        x, y, c = _place()
        chips = [(1 - x, y), (x, 1 - y), (1 - x, 1 - y)]
        for a in range(n):
            for k, (px, py) in enumerate(chips):
                pltpu.make_async_remote_copy(
                    src_ref=src[a].at[:, pl.ds(2 * px + py, 1)], dst_ref=land[3 * a + k], send_sem=ssem.at[3 * a + k],
                    recv_sem=rsem.at[3 * a + k], device_id=(px, py, c), device_id_type=MESH).start()
        token[...] = jnp.zeros_like(token)

    lands = []
    for s in sums:
        lands += [lax.empty((s.shape[0], 1) + s.shape[2:], s.dtype) for _ in range(3)]
    outs = pl.pallas_call(
        body,
        name=name,
        in_specs=[HBM] * (4 * n) + [ANY] * extra,
        out_specs=[SEM, SEM] + [HBM] * (4 * n) + [TOKEN],
        out_shape=[pltpu.SemaphoreType.DMA((3 * n,))] * 2 + [pltpu.HBM(b.shape, b.dtype) for b in list(sums) + lands]
        + [jax.ShapeDtypeStruct((8, LANES), F32)],
        input_output_aliases={i: 2 + i for i in range(4 * n)},
        compiler_params=pltpu.CompilerParams(has_side_effects=EFFECT),
    )(*[_in_hbm(b) for b in list(sums) + lands], *(() if after is None else (after,)))
    return outs[0], outs[1], list(outs[2:2 + n]), list(outs[2 + n:2 + 4 * n]), outs[-1]


def _chips_wait(send_sems, recv_sems, sums, lands, after, name):
    n = len(sums)

    def body(*refs):
        src, land = refs[:n], refs[n:4 * n]
        ssem, rsem = refs[4 * n], refs[4 * n + 1]
        x, y, c = _place()
        chips = [(1 - x, y), (x, 1 - y), (1 - x, 1 - y)]
        for a in range(n):
            for k, (px, py) in enumerate(chips):
                cp = pltpu.make_async_remote_copy(
                    src_ref=src[a].at[:, pl.ds(2 * px + py, 1)], dst_ref=land[3 * a + k], send_sem=ssem.at[3 * a + k],
                    recv_sem=rsem.at[3 * a + k], device_id=(px, py, c), device_id_type=MESH)
                cp.wait_send()
                cp.wait_recv()

    both = list(sums) + list(lands)
    outs = pl.pallas_call(
        body,
        name=name,
        in_specs=[HBM] * (4 * n) + [SEM, SEM, ANY],
        out_specs=[HBM] * (4 * n),
        out_shape=[pltpu.HBM(b.shape, b.dtype) for b in both],
        input_output_aliases={i: i for i in range(4 * n)},
        compiler_params=pltpu.CompilerParams(has_side_effects=EFFECT),
    )(*both, send_sems, recv_sems, after)
    return list(outs[:n]), [list(outs[n + 3 * a:n + 3 * a + 3]) for a in range(n)]


def _sibling_start(grads, name):
    n = len(grads)

    def body(*refs):
        ssem, rsem = refs[2 * n], refs[2 * n + 1]
        src, land = refs[2 * n + 2:3 * n + 2], refs[3 * n + 2:4 * n + 2]
        token = refs[4 * n + 2]
        x, y, c = _place()
        for a in range(n):
            pltpu.make_async_remote_copy(
                src_ref=src[a].at[:, :, pl.ds(1 - c, 1)], dst_ref=land[a], send_sem=ssem.at[a], recv_sem=rsem.at[a],
                device_id=(x, y, 1 - c), device_id_type=MESH).start()
        token[...] = jnp.zeros_like(token)

    lands = [lax.empty(g.shape[:2] + (1,) + g.shape[3:], g.dtype) for g in grads]
    both = list(grads) + lands
    outs = pl.pallas_call(
        body,
        name=name,
        in_specs=[HBM] * (2 * n),
        out_specs=[SEM, SEM] + [HBM] * (2 * n) + [TOKEN],
        out_shape=[pltpu.SemaphoreType.DMA((n,))] * 2 + [pltpu.HBM(b.shape, b.dtype) for b in both]
        + [jax.ShapeDtypeStruct((8, LANES), F32)],
        input_output_aliases={i: 2 + i for i in range(2 * n)},
        compiler_params=pltpu.CompilerParams(has_side_effects=EFFECT),
    )(*[_in_hbm(b) for b in both])
    return outs[0], outs[1], list(outs[2:2 + n]), list(outs[2 + n:2 + 2 * n]), outs[-1]


def _sibling_wait(send_sems, recv_sems, grads, lands, after, name):
    n = len(grads)

    def body(*refs):
        src, land = refs[:n], refs[n:2 * n]
        ssem, rsem = refs[2 * n], refs[2 * n + 1]
        x, y, c = _place()
        for a in range(n):
            cp = pltpu.make_async_remote_copy(
                src_ref=src[a].at[:, :, pl.ds(1 - c, 1)], dst_ref=land[a], send_sem=ssem.at[a], recv_sem=rsem.at[a],
                device_id=(x, y, 1 - c), device_id_type=MESH)
            cp.wait_send()
            cp.wait_recv()

    both = list(grads) + list(lands)
    outs = pl.pallas_call(
        body,
        name=name,
        in_specs=[HBM] * (2 * n) + [SEM, SEM, ANY],
        out_specs=[HBM] * (2 * n),
        out_shape=[pltpu.HBM(b.shape, b.dtype) for b in both],
        input_output_aliases={i: i for i in range(2 * n)},
        compiler_params=pltpu.CompilerParams(has_side_effects=EFFECT),
    )(*both, send_sems, recv_sems, after)
    return list(outs[:n]), list(outs[n:])


_SMALL = ("mix_norm", "q_norm", "k_norm", "sinks", "sgu_ln_g", "sgu_ln_b", "w_spatial", "b_spatial", "ffn_norm")


def _pack_rows(a):
    flat = a.reshape(-1)
    pad = (-flat.shape[0]) % LANES
    if pad:
        flat = jnp.pad(flat, (0, pad))
    return flat.reshape(-1, LANES)


def _pack(values):
    rows = jnp.concatenate([_pack_rows(values[k]) for k in _SMALL], axis=0)
    pad = (-rows.shape[0]) % 8
    if pad:
        rows = jnp.pad(rows, ((0, pad), (0, 0)))
    return rows


def _pack_layers(values):
    depth = values[_SMALL[0]].shape[0]
    pieces = []
    for k in _SMALL:
        flat = values[k].reshape(depth, -1)
        pad = (-flat.shape[1]) % LANES
        if pad:
            flat = jnp.pad(flat, ((0, 0), (0, pad)))
        pieces.append(flat.reshape(depth, -1, LANES))
    rows = jnp.concatenate(pieces, axis=1)
    pad = (-rows.shape[1]) % 8
    if pad:
        rows = jnp.pad(rows, ((0, 0), (0, pad), (0, 0)))
    return rows.reshape(-1, LANES)


def _unpack_layers(rows, like, depth):
    per_layer = rows.reshape(depth, -1, LANES)
    out, at = {}, 0
    for k in _SMALL:
        size = like[k].size
        nrows = -(-size // LANES)
        out[k] = per_layer[:, at:at + nrows].reshape(depth, -1)[:, :size].reshape((depth,) + like[k].shape)
        at += nrows
    return out


def _rope_tables(t, wq, wk):
    pos = jnp.arange(t, dtype=F32)
    inv_freq = jnp.power(ROPE_THETA, -jnp.arange(0, HEAD_DIM, 2, dtype=F32) / HEAD_DIM)
    ang = pos[:, None] * inv_freq[None, :]
    cos, sin = jnp.cos(ang), jnp.sin(ang)
    cos2, sin2 = jnp.concatenate([cos, cos], axis=1), jnp.concatenate([-sin, sin], axis=1)
    return (jnp.tile(cos2, (1, wq // HEAD_DIM)), jnp.tile(sin2, (1, wq // HEAD_DIM)),
            jnp.tile(cos2, (1, wk // HEAD_DIM)), jnp.tile(sin2, (1, wk // HEAD_DIM)))


def kernel(x, mix_norm, w_in, q_norm, k_norm, sinks, sgu_ln_g, sgu_ln_b, w_spatial, b_spatial, w_attn_branch, w_sgu_branch, w_out, ffn_norm, w_gate, w_up, w_down, loss_target, m_mix_norm, m_w_in, m_q_norm, m_k_norm, m_sinks, m_sgu_ln_g, m_sgu_ln_b, m_w_spatial, m_b_spatial, m_w_attn_branch, m_w_sgu_branch, m_w_out, m_ffn_norm, m_w_gate, m_w_up, m_w_down, v_mix_norm, v_w_in, v_q_norm, v_k_norm, v_sinks, v_sgu_ln_g, v_sgu_ln_b, v_w_spatial, v_b_spatial, v_w_attn_branch, v_w_sgu_branch, v_w_out, v_ffn_norm, v_w_gate, v_w_up, v_w_down):
    names = ("mix_norm", "w_in", "q_norm", "k_norm", "sinks", "sgu_ln_g", "sgu_ln_b", "w_spatial", "b_spatial",
             "w_attn_branch", "w_sgu_branch", "w_out", "ffn_norm", "w_gate", "w_up", "w_down")
    weights = dict(zip(names, (mix_norm, w_in, q_norm, k_norm, sinks, sgu_ln_g, sgu_ln_b, w_spatial, b_spatial,
                               w_attn_branch, w_sgu_branch, w_out, ffn_norm, w_gate, w_up, w_down)))
    mom1 = dict(zip(names, (m_mix_norm, m_w_in, m_q_norm, m_k_norm, m_sinks, m_sgu_ln_g, m_sgu_ln_b, m_w_spatial,
                            m_b_spatial, m_w_attn_branch, m_w_sgu_branch, m_w_out, m_ffn_norm, m_w_gate, m_w_up,
                            m_w_down)))
    mom2 = dict(zip(names, (v_mix_norm, v_w_in, v_q_norm, v_k_norm, v_sinks, v_sgu_ln_g, v_sgu_ln_b, v_w_spatial,
                            v_b_spatial, v_w_attn_branch, v_w_sgu_branch, v_w_out, v_ffn_norm, v_w_gate, v_w_up,
                            v_w_down)))
    depth = w_in.shape[0]
    _, t, d = x.shape
    n_q_heads = sinks.shape[1]
    wq = n_q_heads * HEAD_DIM
    wk = wq // Q_PER_KV
    ws = sgu_ln_g.shape[1]
    ng = ws // LANES
    off_u = wq + 2 * wk
    off_g = off_u + 2 * ws
    tables = _rope_tables(t, wq, wk)
    px, py, pc = _place()
    core = pc.astype(jnp.int32)[None]
    chip = (2 * px + py).astype(jnp.int32)[None]
    dev = (4 * px + 2 * py + pc).astype(jnp.int32)[None]

    layers = range(depth)
    chunks = ((0,), (1, 2, 3), (4,), (5,))
    sources = [[jnp.swapaxes(w_in, 1, 2)], [jnp.swapaxes(w_attn_branch, 1, 2)], [jnp.swapaxes(w_sgu_branch, 1, 2)],
               [w_out], [jnp.swapaxes(w_gate, 1, 2), jnp.swapaxes(w_up, 1, 2)], [w_down]]
    stream = [(l, ci) for l in layers for ci in range(len(chunks))]
    placed, state, token = {}, {}, None

    def send(key, after):
        state[key] = _split_start(placed[key], GATHER_STAGES[0], "gather_send_%d_%d" % key, after)
        return state[key][3]

    def advance(key, after, stage):
        send_sems, recv_sems, bufs, _ = state[key]
        bufs = _split_wait(send_sems, recv_sems, bufs, GATHER_STAGES[stage - 1], after, "gather_wait%d_%d_%d" % (stage, *key))
        state[key] = _split_start(bufs, GATHER_STAGES[stage], "gather_pass%d_%d_%d" % (stage, *key))
        return state[key][3]

    def relay(key, after):
        tok = advance(key, after, 1)
        at = stream.index(key)
        for later in stream[at + 2:at + 3] if at else stream[1:3]:
            tok = send(later, tok)
        return tok

    def ready(key, after):
        send_sems, recv_sems, bufs, _ = state.pop(key)
        bufs = _split_wait(send_sems, recv_sems, bufs, GATHER_STAGES[2], after, "gather_wait3_%d_%d" % key)
        return [f.reshape(f.shape[0] * f.shape[1] * f.shape[2], f.shape[3]) for f in bufs]

    for key in stream:
        l, ci = key
        placed[key] = [_place_shard(sources[a], l, dev, BF16, f"place_shard_{l}_{a}",
                                    after=token if a == chunks[ci][0] else None) for a in chunks[ci]]
        token = send(key, None) if key == stream[0] else placed[key][-1]

    saved = []
    xl = x[0]
    going = relay((0, 0), token)
    going = advance((0, 0), going, 2)
    for l in layers:
        gq = jnp.tile(q_norm[l], n_q_heads)[None]
        gk = jnp.tile(k_norm[l], n_q_heads // Q_PER_KV)[None]
        bt = b_spatial[l].T
        h = _rmsnorm_fwd(xl, mix_norm[l][None], f"mix_norm_fwd_{l}", after=going)
        (win_t,) = ready((l, 0), h)
        proj = _mm(h, win_t, "nt", F32, f"in_proj_{l}")
        going = relay((l, 1), proj)
        attn = _attn_fwd(proj, tables, gq, gk, sinks[l], wq, wk, f"attn_fwd_{l}", after=going)
        going = advance((l, 1), attn, 2)
        sgu = _sgu_fwd(proj, sgu_ln_g[l][None], sgu_ln_b[l][None], w_spatial[l], bt, off_u, ws, f"sgu_fwd_{l}",
                       after=going)
        wab_t, wsb_t, wo = ready((l, 1), sgu)
        br_a, br_b, merged = _branches_fwd(attn, sgu, wab_t, wsb_t, proj, off_g, f"branches_{l}")
        going = relay((l, 2), merged)
        x1 = _mm(merged, wo, "nn", F32, f"out_proj_{l}", residual=xl, after=going)
        going = advance((l, 2), x1, 2)
        h2 = _rmsnorm_fwd(x1, ffn_norm[l][None], f"ffn_norm_fwd_{l}", after=going)
        (wgu_t,) = ready((l, 2), h2)
        going = relay((l, 3), h2)
        gu, act = _gate_up_fwd(h2, wgu_t, f"gate_up_{l}", after=going)
        going = advance((l, 3), act, 2)
        if l + 1 < depth:
            going = relay((l + 1, 0), going)
        (wd,) = ready((l, 3), going)
        x2 = _mm(act, wd, "nn", F32, f"down_proj_{l}", residual=x1)
        if l + 1 < depth:
            going = advance((l + 1, 0), x2, 2)
        saved.append(dict(x0=xl, h=h, proj=proj, attn=attn, sgu=sgu, br_a=br_a, br_b=br_b, merged=merged, x1=x1,
                          h2=h2, gu=gu, act=act, gq=gq, gk=gk, bt=bt, win_t=win_t, wab_t=wab_t, wsb_t=wsb_t, wo=wo,
                          wgu_t=wgu_t, wd=wd))
        xl = x2

    loss_part, dx, dx16 = _loss_and_grad(xl, loss_target[0], "loss")
    loss = lax.psum(loss_part[0, 0], ("x", "y", "c"))

    def sibling_start(grads, tag):
        shaped = []
        for g, p in grads:
            rows, c = g.shape
            shaped.append(g.reshape(p, 4, 2, rows // (8 * p), c))
        send_sems, recv_sems, shaped, lands, tok = _sibling_start(shaped, f"rs_sibling_start_{tag}")
        return (send_sems, recv_sems, shaped, lands, tag), tok

    def chips_start(state, after, first=None):
        send_sems, recv_sems, shaped, lands, tag = state
        shaped, lands = _sibling_wait(send_sems, recv_sems, shaped, lands, after, f"rs_sibling_wait_{tag}")
        sums = [_sum_sibling(g, o, core, f"rs_add_sibling_{tag}_{a}") for a, (g, o) in enumerate(zip(shaped, lands))]
        gate = None if first is None else first(sums[0])
        send_sems, recv_sems, sums, lands, tok = _chips_start(sums, f"rs_chips_start_{tag}", after=gate)
        return (send_sems, recv_sems, sums, lands, tag), tok

    def scatter_finish(state, after):
        send_sems, recv_sems, sums, lands, tag = state
        sums, lands = _chips_wait(send_sems, recv_sems, sums, lands, after, f"rs_chips_wait_{tag}")
        return [[s] + o for s, o in zip(sums, lands)]

    in_flight = [dict() for _ in layers]
    small_grads = [None] * depth
    tok, swap_in = None, None
    for l in reversed(layers):
        s = saved[l]
        dgu = _gate_up_bwd(dx16, s["wd"], s["gu"], f"d_gate_up_{l}", after=tok)
        if swap_in is not None:
            in_flight[l + 1]["in"], tok = chips_start(swap_in, dgu)
        g_wd = _mm(s["act"], dx16, "tn", BF16, f"g_w_down_{l}", after=tok)
        dh2 = _mm(dgu, s["wgu_t"], "nn", F32, f"d_h2_{l}", after=g_wd)
        g_wgu_t = _mm(dgu, s["h2"], "tn", BF16, f"g_w_gate_up_{l}", after=dh2)
        swap, tok_s = sibling_start([(g_wd, 1), (g_wgu_t, 2)], f"{l}_gate_up")
        dx1, dx1_16, g_ffn = _rmsnorm_bwd(s["x1"], ffn_norm[l][None], dh2, dx, f"ffn_norm_bwd_{l}", after=tok_s)
        d_a, d_b, dla, dlb = _branches_bwd(dx1_16, s["wo"], s["br_a"], s["br_b"], s["proj"], off_g,
                                           f"d_branches_{l}")
        in_flight[l]["gate_up"], tok = chips_start(swap, d_a)
        g_wo = _mm(s["merged"], dx1_16, "tn", BF16, f"g_w_out_{l}", after=tok)
        dattn = _mm(d_a, s["wab_t"], "nn", F32, f"d_attn_{l}", after=g_wo)
        g_wab_t = _mm(d_a, s["attn"], "tn", BF16, f"g_w_attn_branch_{l}")
        dsgu = _mm(d_b, s["wsb_t"], "nn", F32, f"d_sgu_{l}")
        g_wsb_t = _mm(d_b, s["sgu"], "tn", BF16, f"g_w_sgu_branch_{l}")
        swap, tok_s = sibling_start([(g_wab_t, 1), (g_wsb_t, 1), (g_wo, 1)], f"{l}_mix")
        dq, dk, dv, g_gq, g_gk, g_sinks = _attn_bwd(s["proj"], dattn, tables, s["gq"], s["gk"], sinks[l], wq, wk,
                                                    f"attn_bwd_{l}", after=tok_s)
        du, dvv, g_lng, g_lnb, g_ws, g_bs = _sgu_bwd(s["proj"], dsgu, sgu_ln_g[l][None], sgu_ln_b[l][None],
                                                     w_spatial[l], s["bt"], off_u, ws, f"sgu_bwd_{l}")
        dproj = jnp.concatenate([dq, dk.astype(BF16), dv.astype(BF16), du, dvv, dla, dlb], axis=1)
        dh = _mm(dproj, s["win_t"], "nn", F32, f"d_h_{l}")
        in_flight[l]["mix"], tok = chips_start(swap, dh)
        g_win_t = _mm(dproj, s["h"], "tn", BF16, f"g_w_in_{l}", after=tok)
        swap_in, tok = sibling_start([(g_win_t, 1)], f"{l}_in")
        dx, dx16, g_mix = _rmsnorm_bwd(s["x0"], mix_norm[l][None], dh, dx1, f"mix_norm_bwd_{l}", after=tok)
        small_grads[l] = dict(
            mix_norm=g_mix[0], q_norm=g_gq[0].reshape(n_q_heads, HEAD_DIM).sum(0),
            k_norm=g_gk[0].reshape(n_q_heads // Q_PER_KV, HEAD_DIM).sum(0), sinks=g_sinks[0, :n_q_heads],
            sgu_ln_g=g_lng[0], sgu_ln_b=g_lnb[0], w_spatial=g_ws, b_spatial=g_bs[:, 0, :], ffn_norm=g_ffn[0])
    grad_x = dx[None]

    result = {key: {} for key in ("grad", "delta", "m", "v")}
    layer_like = {k: weights[k][0] for k in _SMALL}
    packed_g = jnp.concatenate([_pack(small_grads[l]) for l in layers], axis=0)
    small_buf = _place_shard([packed_g[None]], 0, dev, F32, "place_small_grads", after=tok)
    send_sems, recv_sems, small_bufs, tok = _split_start([small_buf], GATHER_STAGES[0], "gather_send_small")
    small_state = [(send_sems, recv_sems, small_bufs)]

    def small_stage(stage, after):
        ssem, rsem, bufs = small_state[0]
        bufs = _split_wait(ssem, rsem, bufs, GATHER_STAGES[stage - 1], after, f"gather_wait{stage}_small")
        ssem, rsem, bufs, token = _split_start(bufs, GATHER_STAGES[stage], f"gather_pass{stage}_small")
        small_state[0] = (ssem, rsem, bufs)
        return token

    in_flight[0]["in"], tok = chips_start(swap_in, tok, first=functools.partial(small_stage, 1))

    def update(k, grads, transposed, after):
        view = (lambda a: jnp.swapaxes(a, 1, 2)) if transposed else (lambda a: a)
        outs = _adam(view(weights[k]), grads, view(mom1[k]), view(mom2[k]), chip, f"adam_{k}", after=after)
        for key, val in zip(("grad", "delta", "m", "v"), outs):
            result[key][k] = view(val)
        return outs[3]

    def plain(terms, tag):
        s, lands = terms[0], terms[1:]
        g = _sum_chips(s, lands, chip, f"rs_add_chips_{tag}")
        return [jnp.swapaxes(g, 1, 2)[:, None]]

    gate_up = [scatter_finish(in_flight[l]["gate_up"], tok) for l in reversed(layers)][::-1]
    tok = update("w_down", [(gate_up[l][0], 0) for l in layers], False, None)
    tok = small_stage(2, tok)
    tok = update("w_gate", [(gate_up[l][1], 0) for l in layers], True, tok)
    tok = update("w_up", [(gate_up[l][1], 1) for l in layers], True, tok)
    mix =[scatter_finish(in_flight[l]["mix"], tok) for l in reversed(layers)][::-1]
    tok = update("w_out", [(mix[l][2], 0) for l in layers], False, None)
    tok = update("w_attn_branch", [(plain(mix[l][0], f"{l}_attn_branch"), 0) for l in layers], False, tok)
    tok = update("w_sgu_branch", [(plain(mix[l][1], f"{l}_sgu_branch"), 0) for l in layers], False, tok)

    packed = [_pack_layers(src) for src in (weights, mom1, mom2)]
    (gathered_small,) = _split_wait(*small_state[0], GATHER_STAGES[2], tok, "gather_wait3_small")
    small = _small_reduce_adam(gathered_small[0], *packed, "small_reduce_adam")
    for key, rows in zip(("grad", "delta", "m", "v"), small):
        result[key].update(_unpack_layers(rows, layer_like, depth))

    last = [scatter_finish(in_flight[l]["in"], small[0]) for l in reversed(layers)][::-1]
    update("w_in", [(last[l][0], 0) for l in layers], True, result["v"]["ffn_norm"])

    return (loss, grad_x, *[result["grad"][k] for k in names], *[result["delta"][k] for k in names],
            *[result["m"][k] for k in names], *[result["v"][k] for k in names])
```

```python
import functools
import math

import jax
import jax.numpy as jnp
from jax import lax
from jax.experimental import pallas as pl
from jax.experimental.pallas import tpu as pltpu

F32 = jnp.float32
BF16 = jnp.bfloat16
MESH = pl.DeviceIdType.MESH
ANY = pl.BlockSpec(memory_space=pl.ANY)

N_DEV = 8
HEAD_DIM = 64
Q_PER_KV = 4
BLOCK = 128
LANES = 128
ROPE_THETA = 10000.0
EPS = 1e-6
ADAM_LR = 0.001
ADAM_B1 = 0.9
ADAM_B2 = 0.999
ADAM_EPS = 1e-08
ADAM_WD = 0.01
ADAM_STEP = 10
NEG = -1e30
VMEM_LIMIT_BYTES = 56 * 1024 * 1024

NN = ((1,), (0,))
NT = ((1,), (1,))
TN = ((0,), (0,))


def _dot(a, b, dims):
    return lax.dot_general(a, b, (dims, ((), ())), preferred_element_type=F32)


def _params(*sem):
    return pltpu.CompilerParams(dimension_semantics=sem, vmem_limit_bytes=VMEM_LIMIT_BYTES)


def _divisor_tile(n, limit, unit):
    if n <= limit:
        return n
    best = unit
    for t in range(unit, limit + 1, unit):
        if n % t == 0:
            best = t
    assert n % best == 0, (n, limit, unit)
    return best


def _row_chunks(rows, size=256):
    size = min(size, rows)
    assert rows % size == 0, (rows, size)
    return [pl.ds(start, size) for start in range(0, rows, size)]


def _mm(a, b, mode, out_dtype, name, residual=None, after=None):
    parts = a.shape[0] if a.ndim == 3 else 1
    a2 = a.shape[-2:]
    if mode == "nn":
        (m, kp), (k2, n) = a2, b.shape
        k, mp = kp * parts, m
    elif mode == "nt":
        (m, kp), (n, k2) = a2, b.shape
        k, mp = kp * parts, m
    else:
        (k, mp), (k2, n) = a2, b.shape
        m, kp = mp * parts, k
    assert k == k2, (name, a.shape, b.shape)
    tk = _divisor_tile(kp, 2816, 128)
    nk = k // tk
    tm = _divisor_tile(mp, 512 if mode == "tn" else 1024, 128)
    tn = _divisor_tile(n, 2048 if mode == "tn" else 1024, 128)
    kpb, mpb = kp // tk, mp // tm
    dims = {"nn": NN, "nt": NT, "tn": TN}[mode]
    lead = (None,) if a.ndim == 3 else ()
    if mode == "tn":
        a_index = lambda i, j, kk: (i // mpb, kk, i % mpb) if lead else (kk, i)
        a_spec = pl.BlockSpec(lead + (tk, tm), a_index)
    else:
        a_index = lambda i, j, kk: (kk // kpb, i, kk % kpb) if lead else (i, kk)
        a_spec = pl.BlockSpec(lead + (tm, tk), a_index)
    if mode == "nt":
        b_spec = pl.BlockSpec((tn, tk), lambda i, j, kk: (j, kk))
    else:
        b_spec = pl.BlockSpec((tk, tn), lambda i, j, kk: (kk, j))
    o_spec = pl.BlockSpec((tm, tn), lambda i, j, kk: (i, j))
    has_res = residual is not None

    def body(*refs):
        a_ref, b_ref = refs[:2]
        r_ref = refs[2] if has_res else None
        o_ref, acc_ref = refs[-2:]
        kk = pl.program_id(2)
        p = _dot(a_ref[...], b_ref[...], dims)

        def finish(total):
            if has_res:
                total = total + r_ref[...]
            o_ref[...] = total.astype(o_ref.dtype)

        if nk == 1:
            finish(p)
        else:
            @pl.when(kk == 0)
            def _():
                acc_ref[...] = p

            @pl.when(jnp.logical_and(kk > 0, kk < nk - 1))
            def _():
                acc_ref[...] += p

            @pl.when(kk == nk - 1)
            def _():
                finish(acc_ref[...] + p)

    in_specs = [a_spec, b_spec] + ([o_spec] if has_res else []) + ([ANY] if after is not None else [])
    args = (a, b) + ((residual,) if has_res else ()) + ((after,) if after is not None else ())
    acc_shape = (tm, tn) if nk > 1 else (8, LANES)
    return pl.pallas_call(
        body,
        name=name,
        grid=(m // tm, n // tn, nk),
        in_specs=in_specs,
        out_specs=o_spec,
        out_shape=jax.ShapeDtypeStruct((m, n), out_dtype),
        scratch_shapes=[pltpu.VMEM(acc_shape, F32)],
        compiler_params=_params("parallel", "parallel", "arbitrary"),
    )(*args)


def _rmsnorm_fwd(x, g, name, after=None):
    t, d = x.shape
    tr = _divisor_tile(t, 512, 8)

    def body(x_ref, g_ref, *rest):
        h_ref = rest[-1]
        xv = x_ref[...]
        rstd = lax.rsqrt(jnp.mean(xv * xv, axis=-1, keepdims=True) + EPS)
        h_ref[...] = (xv * rstd * g_ref[...]).astype(h_ref.dtype)

    return pl.pallas_call(
        body,
        name=name,
        grid=(t // tr,),
        in_specs=[pl.BlockSpec((tr, d), lambda i: (i, 0)), pl.BlockSpec((1, d), lambda i: (0, 0))]
        + ([ANY] if after is not None else []),
        out_specs=pl.BlockSpec((tr, d), lambda i: (i, 0)),
        out_shape=jax.ShapeDtypeStruct((t, d), BF16),
        compiler_params=_params("parallel"),
    )(x, g, *(() if after is None else (after,)))


def _rmsnorm_bwd(x, g, dh, dres, name, after=None):
    t, d = x.shape
    tr = _divisor_tile(t, 256, 8)

    def body(x_ref, g_ref, dh_ref, dres_ref, *rest):
        dx_ref, dx16_ref, dg_ref = rest[-3:]
        i = pl.program_id(0)
        xv = x_ref[...]
        rstd = lax.rsqrt(jnp.mean(xv * xv, axis=-1, keepdims=True) + EPS)
        xh = xv * rstd
        dhv = dh_ref[...]
        dxh = dhv * g_ref[...]
        dx = dres_ref[...] + rstd * (dxh - xh * jnp.mean(dxh * xh, axis=-1, keepdims=True))
        dx_ref[...] = dx
        dx16_ref[...] = dx.astype(dx16_ref.dtype)
        part = jnp.broadcast_to(jnp.sum(dhv * xh, axis=0, keepdims=True), dg_ref.shape)

        @pl.when(i == 0)
        def _():
            dg_ref[...] = part

        @pl.when(i > 0)
        def _():
            dg_ref[...] += part

    row = pl.BlockSpec((tr, d), lambda i: (i, 0))
    return pl.pallas_call(
        body,
        name=name,
        grid=(t // tr,),
        in_specs=[row, pl.BlockSpec((1, d), lambda i: (0, 0)), row, row] + ([ANY] if after is not None else []),
        out_specs=[row, row, pl.BlockSpec((8, d), lambda i: (0, 0))],
        out_shape=[jax.ShapeDtypeStruct((t, d), F32), jax.ShapeDtypeStruct((t, d), BF16),
                   jax.ShapeDtypeStruct((8, d), F32)],
        compiler_params=_params("arbitrary"),
    )(x, g, dh, dres, *(() if after is None else (after,)))


def _lane(shape):
    return lax.broadcasted_iota(jnp.int32, shape, 1)


def _group_sum64(s):
    row = lax.broadcasted_iota(jnp.int32, (LANES, LANES), 0)
    col = lax.broadcasted_iota(jnp.int32, (LANES, LANES), 1)
    ones = jnp.where((row >= HEAD_DIM) == (col >= HEAD_DIM), 1.0, 0.0).astype(BF16)
    out = []
    for t in range(s.shape[1] // LANES):
        piece = s[:, LANES * t:LANES * t + LANES]
        hi = piece.astype(BF16)
        lo = (piece - hi.astype(F32)).astype(BF16)
        out.append(_dot(hi, ones, NN) + _dot(lo, ones, NN))
    return out[0] if len(out) == 1 else jnp.concatenate(out, axis=1)


def _swap32(x):
    w = x.shape[1]
    return jnp.where((_lane(x.shape) & 32) == 0, pltpu.roll(x, w - 32, axis=1), pltpu.roll(x, 32, axis=1))


def _rope(x, c, s):
    return x * c + _swap32(x) * s


def _rope_t(dy, c, s):
    return dy * c + _swap32(dy * s)


def _head_norm(x):
    rstd = lax.rsqrt(_group_sum64(x * x) * (1.0 / HEAD_DIM) + EPS)
    return x * rstd, rstd


def _head_norm_bwd(dxh, xh, rstd):
    return rstd * (dxh - xh * (_group_sum64(dxh * xh) * (1.0 / HEAD_DIM)))


def _roll64(x):
    return pltpu.roll(x, 64, axis=1)


def _attn_specs(wq, wk):
    kb = wq // wk
    prev = lambda i: jnp.maximum(i - 1, 0)
    return dict(
        q=pl.BlockSpec((BLOCK, wq), lambda i: (i, 0)),
        kc=pl.BlockSpec((BLOCK, wk), lambda i: (i, kb)),
        kp=pl.BlockSpec((BLOCK, wk), lambda i: (prev(i), kb)),
        vc=pl.BlockSpec((BLOCK, wk), lambda i: (i, kb + 1)),
        vp=pl.BlockSpec((BLOCK, wk), lambda i: (prev(i), kb + 1)),
        tq=pl.BlockSpec((BLOCK, wq), lambda i: (i, 0)),
        tkp=pl.BlockSpec((BLOCK, wk), lambda i: (prev(i), 0)),
        gq=pl.BlockSpec((1, wq), lambda i: (0, 0)),
        gk=pl.BlockSpec((1, wk), lambda i: (0, 0)),
        sinks=pl.BlockSpec(memory_space=pltpu.SMEM),
    )


def _attn_prologue(i, q_ref, kc_ref, kp_ref, cq_ref, sq_ref, ckp_ref, skp_ref, gq_ref, gk_ref):
    wk = kc_ref.shape[1]
    cq, sq = cq_ref[...], sq_ref[...]
    ck, sk = cq[:, :wk], sq[:, :wk]
    qh, q_rstd = _head_norm(q_ref[...])
    kch, kc_rstd = _head_norm(kc_ref[...])
    kph, kp_rstd = _head_norm(kp_ref[...])
    qn = _rope(qh * gq_ref[...], cq, sq)
    knc = _rope(kch * gk_ref[...], ck, sk)
    knp = _rope(kph * gk_ref[...], ckp_ref[...], skp_ref[...])
    stacked = (Q_PER_KV * BLOCK, BLOCK)
    row = lax.broadcasted_iota(jnp.int32, stacked, 0) & (BLOCK - 1)
    col = lax.broadcasted_iota(jnp.int32, stacked, 1)
    mask_c = col <= row
    valid = jnp.logical_or(mask_c, i > 0)
    half = (lax.broadcasted_iota(jnp.int32, (BLOCK, BLOCK), 1) >= HEAD_DIM).astype(jnp.int32)
    return dict(cq=cq, sq=sq, ck=ck, sk=sk, qh=qh, q_rstd=q_rstd, kch=kch, kc_rstd=kc_rstd, kph=kph,
                kp_rstd=kp_rstd, qn=qn, knc=knc, knp=knp, mask_c=mask_c, valid=valid, half=half)


def _stack_heads(x, g, half):
    kpar = g % 2
    pieces = []
    for j in range(Q_PER_KV):
        t, e = divmod(Q_PER_KV * g + j, 2)
        piece = jnp.where(half == e, x[:, LANES * t:LANES * t + LANES], 0.0)
        pieces.append(piece if e == kpar else _roll64(piece))
    return jnp.concatenate(pieces, axis=0)


def _unstack_heads(y, g, half):
    kpar = g % 2
    slabs = {}
    for j in range(Q_PER_KV):
        t, e = divmod(Q_PER_KV * g + j, 2)
        piece = jnp.where(half == kpar, y[BLOCK * j:BLOCK * j + BLOCK], 0.0)
        piece = piece if e == kpar else _roll64(piece)
        slabs[t] = piece if t not in slabs else slabs[t] + piece
    return slabs


def _group_scores(st, g, sinks_ref, scale):
    ks = g // 2
    sl = slice(LANES * ks, LANES * ks + LANES)
    q4 = _stack_heads(st["qn"], g, st["half"]).astype(BF16)
    kc, kp = st["knc"][:, sl].astype(BF16), st["knp"][:, sl].astype(BF16)
    rows = Q_PER_KV * BLOCK
    at = lax.broadcasted_iota(jnp.int32, (rows, 1), 0)
    head = jnp.zeros((rows, 1), jnp.int32)
    sink = jnp.zeros((rows, 1), F32) + sinks_ref[Q_PER_KV * g]
    for j in range(1, Q_PER_KV):
        head = jnp.where(at >= BLOCK * j, j, head)
        sink = jnp.where(at >= BLOCK * j, sinks_ref[Q_PER_KV * g + j], sink)
    cur = st["mask_c"]
    s = jnp.where(cur, _dot(q4, kc, NT), _dot(q4, kp, NT)) * scale
    s = jnp.where(st["valid"], s, NEG)
    m = jnp.maximum(jnp.max(s, axis=1, keepdims=True), sink)
    p = jnp.exp(s - m)
    p_s = jnp.exp(sink - m)
    inv = 1.0 / (jnp.sum(p, axis=1, keepdims=True) + p_s)
    return dict(sl=sl, head=head, q4=q4, kc=kc, kp=kp, cur=cur, pr=p * inv, pr_s=p_s * inv)


def _attn_fwd(proj, tables, gq, gk, sinks, wq, wk, name, after=None):
    t = proj.shape[0]
    nb = t // BLOCK
    sp = _attn_specs(wq, wk)
    scale = HEAD_DIM ** -0.5
    cos_t, sin_t, cos_k, sin_k = tables

    def body(sinks_ref, q_ref, kc_ref, kp_ref, vc_ref, vp_ref, cq_ref, sq_ref, ckp_ref, skp_ref, gq_ref, gk_ref,
             *rest):
        o_ref = rest[-1]
        i = pl.program_id(0)
        st = _attn_prologue(i, q_ref, kc_ref, kp_ref, cq_ref, sq_ref, ckp_ref, skp_ref, gq_ref, gk_ref)
        for g in range(wq // (Q_PER_KV * HEAD_DIM)):
            gs = _group_scores(st, g, sinks_ref, scale)
            own = st["half"] == g % 2
            vc = jnp.where(own, vc_ref[:, gs["sl"]], 0.0).astype(BF16)
            vp = jnp.where(own, vp_ref[:, gs["sl"]], 0.0).astype(BF16)
            pr = gs["pr"].astype(BF16)
            zero = jnp.zeros_like(pr)
            out = _dot(jnp.where(gs["cur"], pr, zero), vc, NN) + _dot(jnp.where(gs["cur"], zero, pr), vp, NN)
            for ts, slab in _unstack_heads(out, g, st["half"]).items():
                o_ref[:, LANES * ts:LANES * ts + LANES] = slab.astype(o_ref.dtype)

    return pl.pallas_call(
        body,
        name=name,
        grid=(nb,),
        in_specs=[sp["sinks"], sp["q"], sp["kc"], sp["kp"], sp["vc"], sp["vp"], sp["tq"], sp["tq"], sp["tkp"],
                  sp["tkp"], sp["gq"], sp["gk"]] + ([ANY] if after is not None else []),
        out_specs=pl.BlockSpec((BLOCK, wq), lambda i: (i, 0)),
        out_shape=jax.ShapeDtypeStruct((t, wq), BF16),
        compiler_params=_params("parallel"),
    )(sinks, proj, proj, proj, proj, proj, cos_t, sin_t, cos_k, sin_k, gq, gk, *(() if after is None else (after,)))


def _attn_bwd(proj, dout, tables, gq, gk, sinks, wq, wk, name, after=None):
    t = proj.shape[0]
    nb = t // BLOCK
    sp = _attn_specs(wq, wk)
    scale = HEAD_DIM ** -0.5
    cos_t, sin_t, cos_k, sin_k = tables

    def body(sinks_ref, q_ref, kc_ref, kp_ref, vc_ref, vp_ref, cq_ref, sq_ref, ckp_ref, skp_ref, gq_ref, gk_ref,
             do_ref, *rest):
        dq_ref, dk_ref, dv_ref, dgq_ref, dgk_ref, dsk_ref, dqn_ref, dknc_ref, dknp_ref, dvc_ref, dvp_ref = rest[-11:]
        i = pl.program_id(0)
        st = _attn_prologue(i, q_ref, kc_ref, kp_ref, cq_ref, sq_ref, ckp_ref, skp_ref, gq_ref, gk_ref)
        dknc_ref[...] = jnp.zeros_like(dknc_ref)
        dknp_ref[...] = jnp.zeros_like(dknp_ref)
        dvc_ref[...] = jnp.zeros_like(dvc_ref)
        dvp_ref[...] = jnp.zeros_like(dvp_ref)
        lane8 = _lane((8, LANES))
        dsinks = jnp.zeros((8, LANES), F32)
        for g in range(wq // (Q_PER_KV * HEAD_DIM)):
            gs = _group_scores(st, g, sinks_ref, scale)
            sl = gs["sl"]
            do4 = _stack_heads(do_ref[...], g, st["half"]).astype(BF16)
            cur, pr = gs["cur"], gs["pr"]
            dp = jnp.where(cur, _dot(do4, vc_ref[:, sl].astype(BF16), NT), _dot(do4, vp_ref[:, sl].astype(BF16), NT))
            rs = jnp.sum(pr * dp, axis=1, keepdims=True)
            ds = (pr * (dp - rs) * scale).astype(BF16)
            pr16 = pr.astype(BF16)
            zero = jnp.zeros_like(ds)
            ds_c, ds_p = jnp.where(cur, ds, zero), jnp.where(cur, zero, ds)
            pr_c, pr_p = jnp.where(cur, pr16, zero), jnp.where(cur, zero, pr16)
            dsink_rows = -gs["pr_s"] * rs
            for j in range(Q_PER_KV):
                dsink = jnp.sum(jnp.where(gs["head"] == j, dsink_rows, 0.0))
                dsinks = dsinks + jnp.where(lane8 == Q_PER_KV * g + j, dsink, 0.0)
            dq4 = _dot(ds_c, gs["kc"], NN) + _dot(ds_p, gs["kp"], NN)
            for ts, slab in _unstack_heads(dq4, g, st["half"]).items():
                dqn_ref[:, LANES * ts:LANES * ts + LANES] = slab
            dvc_ref[:, sl] += _dot(pr_c.astype(BF16), do4, TN)
            dvp_ref[:, sl] += _dot(pr_p.astype(BF16), do4, TN)
            dknc_ref[:, sl] += _dot(ds_c, gs["q4"], TN)
            dknp_ref[:, sl] += _dot(ds_p, gs["q4"], TN)

        gqv, gkv = gq_ref[...], gk_ref[...]
        dqg = _rope_t(dqn_ref[...], st["cq"], st["sq"])
        dq_ref[...] = _head_norm_bwd(dqg * gqv, st["qh"], st["q_rstd"]).astype(dq_ref.dtype)
        dkcg = _rope_t(dknc_ref[...], st["ck"], st["sk"])
        dkpg = _rope_t(dknp_ref[...], ckp_ref[...], skp_ref[...])
        dk_cur = _head_norm_bwd(dkcg * gkv, st["kch"], st["kc_rstd"])
        dk_prev = _head_norm_bwd(dkpg * gkv, st["kph"], st["kp_rstd"])
        dgq_part = jnp.broadcast_to(jnp.sum(dqg * st["qh"], axis=0, keepdims=True), dgq_ref.shape)
        dgk_part = jnp.broadcast_to(
            jnp.sum(dkcg * st["kch"] + dkpg * st["kph"], axis=0, keepdims=True), dgk_ref.shape)
        cur = pl.ds(pl.multiple_of(i * BLOCK, BLOCK), BLOCK)
        dk_ref[cur, :] = dk_cur
        dv_ref[cur, :] = dvc_ref[...]

        @pl.when(i == 0)
        def _():
            dgq_ref[...] = dgq_part
            dgk_ref[...] = dgk_part
            dsk_ref[...] = dsinks

        @pl.when(i > 0)
        def _():
            before = pl.ds(pl.multiple_of((i - 1) * BLOCK, BLOCK), BLOCK)
            dk_ref[before, :] += dk_prev
            dv_ref[before, :] += dvp_ref[...]
            dgq_ref[...] += dgq_part
            dgk_ref[...] += dgk_part
            dsk_ref[...] += dsinks

    whole = lambda shape: pl.BlockSpec(shape, lambda i: (0, 0))
    return pl.pallas_call(
        body,
        name=name,
        grid=(nb,),
        in_specs=[sp["sinks"], sp["q"], sp["kc"], sp["kp"], sp["vc"], sp["vp"], sp["tq"], sp["tq"], sp["tkp"],
                  sp["tkp"], sp["gq"], sp["gk"], pl.BlockSpec((BLOCK, wq), lambda i: (i, 0))]
        + ([ANY] if after is not None else []),
        out_specs=[pl.BlockSpec((BLOCK, wq), lambda i: (i, 0)), whole((t, wk)), whole((t, wk)), whole((8, wq)),
                   whole((8, wk)), whole((8, LANES))],
        out_shape=[jax.ShapeDtypeStruct((t, wq), BF16), jax.ShapeDtypeStruct((t, wk), F32),
                   jax.ShapeDtypeStruct((t, wk), F32), jax.ShapeDtypeStruct((8, wq), F32),
                   jax.ShapeDtypeStruct((8, wk), F32), jax.ShapeDtypeStruct((8, LANES), F32)],
        scratch_shapes=[pltpu.VMEM((BLOCK, wq), F32), pltpu.VMEM((BLOCK, wk), F32), pltpu.VMEM((BLOCK, wk), F32),
                        pltpu.VMEM((BLOCK, wk), F32), pltpu.VMEM((BLOCK, wk), F32)],
        compiler_params=_params("arbitrary"),
    )(sinks, proj, proj, proj, proj, proj, cos_t, sin_t, cos_k, sin_k, gq, gk, dout,
      *(() if after is None else (after,)))


_GELU_K = math.sqrt(2.0 / math.pi)
_GELU_A = 0.044715


def _gelu(x):
    return 0.5 * x * (1.0 + jnp.tanh(_GELU_K * (x + _GELU_A * x * x * x)))


def _gelu_and_grad(x):
    th = jnp.tanh(_GELU_K * (x + _GELU_A * x * x * x))
    return (0.5 * x * (1.0 + th),
            0.5 * (1.0 + th) + 0.5 * x * (1.0 - th * th) * (_GELU_K * (1.0 + 3.0 * _GELU_A * x * x)))


def _group_ln(v):
    mu = jnp.mean(v, axis=1, keepdims=True)
    cen = v - mu
    rstd = lax.rsqrt(jnp.mean(cen * cen, axis=1, keepdims=True) + EPS)
    return cen * rstd, rstd


def _sgu_geometry(off_u, ws):
    cw = math.gcd(off_u, ws)
    return cw, ws // cw, off_u // cw, (off_u + ws) // cw


def _sgu_fwd(proj, ln_g, ln_b, w_s, bt, off_u, ws, name, after=None):
    t = proj.shape[0]
    nb = t // BLOCK
    cw, nc, ub, vb = _sgu_geometry(off_u, ws)
    gpc = cw // LANES
    ng = ws // LANES

    def body(u_ref, v_ref, g_ref, b_ref, w_ref, bt_ref, *rest):
        o_ref = rest[-1]
        jc = pl.program_id(0)
        row = lax.broadcasted_iota(jnp.int32, (BLOCK, BLOCK), 0)
        col = lax.broadcasted_iota(jnp.int32, (BLOCK, BLOCK), 1)
        lane_g = _lane((BLOCK, ng))
        for gi in range(gpc):
            sl = slice(LANES * gi, LANES * gi + LANES)
            xh, _ = _group_ln(_gelu(v_ref[:, sl]))
            vn = xh * g_ref[:, sl] + b_ref[:, sl]
            w = jnp.where(row >= col, w_ref[gi], 0.0).astype(BF16)
            bias = jnp.sum(jnp.where(lane_g == jc * gpc + gi, bt_ref[...], 0.0), axis=1, keepdims=True)
            s = _dot(w, vn.astype(BF16), NN) + bias
            o_ref[:, sl] = (_gelu(u_ref[:, sl]) * s).astype(o_ref.dtype)

    return pl.pallas_call(
        body,
        name=name,
        grid=(nc, nb),
        in_specs=[pl.BlockSpec((BLOCK, cw), lambda jc, i: (i, ub + jc)),
                  pl.BlockSpec((BLOCK, cw), lambda jc, i: (i, vb + jc)),
                  pl.BlockSpec((1, cw), lambda jc, i: (0, jc)),
                  pl.BlockSpec((1, cw), lambda jc, i: (0, jc)),
                  pl.BlockSpec((gpc, BLOCK, BLOCK), lambda jc, i: (jc, 0, 0)),
                  pl.BlockSpec((BLOCK, ng), lambda jc, i: (0, 0))] + ([ANY] if after is not None else []),
        out_specs=pl.BlockSpec((BLOCK, cw), lambda jc, i: (i, jc)),
        out_shape=jax.ShapeDtypeStruct((t, ws), BF16),
        compiler_params=_params("parallel", "parallel"),
    )(proj, proj, ln_g, ln_b, w_s, bt, *(() if after is None else (after,)))


def _sgu_bwd(proj, dout, ln_g, ln_b, w_s, bt, off_u, ws, name):
    t = proj.shape[0]
    nb = t // BLOCK
    cw, nc, ub, vb = _sgu_geometry(off_u, ws)
    gpc = cw // LANES
    ng = ws // LANES

    def body(u_ref, v_ref, g_ref, b_ref, w_ref, bt_ref, do_ref, du_ref, dv_ref, dg_ref, db_ref, dw_ref, dbs_ref,
             bacc_ref):
        jc = pl.program_id(0)
        i = pl.program_id(1)
        row = lax.broadcasted_iota(jnp.int32, (BLOCK, BLOCK), 0)
        col = lax.broadcasted_iota(jnp.int32, (BLOCK, BLOCK), 1)
        lane_g = _lane((BLOCK, ng))
        tri = row >= col

        @pl.when(i == 0)
        def _():
            dg_ref[...] = jnp.zeros_like(dg_ref)
            db_ref[...] = jnp.zeros_like(db_ref)
            dw_ref[...] = jnp.zeros_like(dw_ref)
            bacc_ref[...] = jnp.zeros_like(bacc_ref)

        for gi in range(gpc):
            sl = slice(LANES * gi, LANES * gi + LANES)
            u_raw, v_raw = u_ref[:, sl], v_ref[:, sl]
            u_act, u_slope = _gelu_and_grad(u_raw)
            v_act, v_slope = _gelu_and_grad(v_raw)
            xh, rstd = _group_ln(v_act)
            gam = g_ref[:, sl]
            vn = (xh * gam + b_ref[:, sl]).astype(BF16)
            w = jnp.where(tri, w_ref[gi], 0.0)
            bias = jnp.sum(jnp.where(lane_g == jc * gpc + gi, bt_ref[...], 0.0), axis=1, keepdims=True)
            s = _dot(w.astype(BF16), vn, NN) + bias
            dov = do_ref[:, sl]
            du_ref[:, sl] = (dov * s * u_slope).astype(du_ref.dtype)
            ds = dov * u_act
            ds16 = ds.astype(BF16)
            dw_ref[gi] += jnp.where(tri, _dot(ds16, vn, NT), 0.0)
            bacc_ref[gi] += ds
            dvn = _dot(w.T.astype(BF16), ds16, NN)
            dg_ref[:, sl] += jnp.broadcast_to(jnp.sum(dvn * xh, axis=0, keepdims=True), (8, LANES))
            db_ref[:, sl] += jnp.broadcast_to(jnp.sum(dvn, axis=0, keepdims=True), (8, LANES))
            dxh = dvn * gam
            dvg = rstd * (dxh - jnp.mean(dxh, axis=1, keepdims=True)
                          - xh * jnp.mean(dxh * xh, axis=1, keepdims=True))
            dv_ref[:, sl] = (dvg * v_slope).astype(dv_ref.dtype)

        @pl.when(i == nb - 1)
        def _():
            for gi in range(gpc):
                dbs_ref[gi] = jnp.broadcast_to(jnp.sum(bacc_ref[gi].T, axis=0, keepdims=True), (8, LANES))

    blk = lambda base: pl.BlockSpec((BLOCK, cw), lambda jc, i: (i, base + jc))
    vec = pl.BlockSpec((1, cw), lambda jc, i: (0, jc))
    acc = pl.BlockSpec((8, cw), lambda jc, i: (0, jc))
    wsp = pl.BlockSpec((gpc, BLOCK, BLOCK), lambda jc, i: (jc, 0, 0))
    return pl.pallas_call(
        body,
        name=name,
        grid=(nc, nb),
        in_specs=[blk(ub), blk(vb), vec, vec, wsp, pl.BlockSpec((BLOCK, ng), lambda jc, i: (0, 0)), blk(0)],
        out_specs=[blk(0), blk(0), acc, acc, wsp, pl.BlockSpec((gpc, 8, LANES), lambda jc, i: (jc, 0, 0))],
        out_shape=[jax.ShapeDtypeStruct((t, ws), BF16), jax.ShapeDtypeStruct((t, ws), BF16),
                   jax.ShapeDtypeStruct((8, ws), F32), jax.ShapeDtypeStruct((8, ws), F32),
                   jax.ShapeDtypeStruct((ng, BLOCK, BLOCK), F32), jax.ShapeDtypeStruct((ng, 8, LANES), F32)],
        scratch_shapes=[pltpu.VMEM((gpc, BLOCK, BLOCK), F32)],
        compiler_params=_params("arbitrary", "arbitrary"),
    )(proj, proj, ln_g, ln_b, w_s, bt, dout)


def _sigmoid(x):
    return 1.0 / (1.0 + jnp.exp(-x))


def _merge_geometry(off_g, d):
    cw = math.gcd(off_g, d)
    return cw, d // cw, off_g // cw, (off_g + d) // cw


def _branches_fwd(attn, sgu, wab_t, wsb_t, proj, off_g, name):
    t = attn.shape[0]
    d = wab_t.shape[0]
    tn, _, ab, bb = _merge_geometry(off_g, d)
    tm = _divisor_tile(t, 1024, 128)

    def body(a1_ref, a2_ref, b1_ref, b2_ref, la_ref, lb_ref, bra_ref, brb_ref, o_ref):
        for rows in _row_chunks(tm):
            va = _dot(a1_ref[rows, :], b1_ref[...], NT)
            vb = _dot(a2_ref[rows, :], b2_ref[...], NT)
            bra_ref[rows, :] = va
            brb_ref[rows, :] = vb
            o_ref[rows, :] = (_sigmoid(la_ref[rows, :]) * va + _sigmoid(lb_ref[rows, :]) * vb).astype(o_ref.dtype)

    rows = lambda w: pl.BlockSpec((tm, w), lambda i, j: (i, 0))
    wrow = lambda w: pl.BlockSpec((tn, w), lambda i, j: (j, 0))
    blk = lambda base: pl.BlockSpec((tm, tn), lambda i, j: (i, base + j))
    return pl.pallas_call(
        body,
        name=name,
        grid=(t // tm, d // tn),
        in_specs=[rows(attn.shape[1]), rows(sgu.shape[1]), wrow(wab_t.shape[1]), wrow(wsb_t.shape[1]), blk(ab),
                  blk(bb)],
        out_specs=[blk(0)] * 3,
        out_shape=[jax.ShapeDtypeStruct((t, d), F32), jax.ShapeDtypeStruct((t, d), F32),
                   jax.ShapeDtypeStruct((t, d), BF16)],
        compiler_params=_params("parallel", "parallel"),
    )(attn, sgu, wab_t, wsb_t, proj, proj)


def _branches_bwd(dx16, wo, br_a, br_b, proj, off_g, name, after=None):
    t, d = br_a.shape
    tn, _, ab, bb = _merge_geometry(off_g, d)
    tm = _divisor_tile(t, 1024, 128)
    k = dx16.shape[1]

    def body(a_ref, b_ref, bra_ref, brb_ref, la_ref, lb_ref, *rest):
        da_ref, db_ref, dla_ref, dlb_ref = rest[-4:]
        for rows in _row_chunks(tm):
            dmv = _dot(a_ref[rows, :], b_ref[...], NT)
            ga, gb = _sigmoid(la_ref[rows, :]), _sigmoid(lb_ref[rows, :])
            da_ref[rows, :] = (dmv * ga).astype(da_ref.dtype)
            db_ref[rows, :] = (dmv * gb).astype(db_ref.dtype)
            dla_ref[rows, :] = (dmv * bra_ref[rows, :] * ga * (1.0 - ga)).astype(dla_ref.dtype)
            dlb_ref[rows, :] = (dmv * brb_ref[rows, :] * gb * (1.0 - gb)).astype(dlb_ref.dtype)

    blk = lambda base: pl.BlockSpec((tm, tn), lambda i, j: (i, base + j))
    return pl.pallas_call(
        body,
        name=name,
        grid=(t // tm, d // tn),
        in_specs=[pl.BlockSpec((tm, k), lambda i, j: (i, 0)), pl.BlockSpec((tn, k), lambda i, j: (j, 0)), blk(0),
                  blk(0), blk(ab), blk(bb)] + ([ANY] if after is not None else []),
        out_specs=[blk(0)] * 4,
        out_shape=[jax.ShapeDtypeStruct((t, d), BF16)] * 4,
        compiler_params=_params("parallel", "parallel"),
    )(dx16, wo, br_a, br_b, proj, proj, *(() if after is None else (after,)))


def _gate_up_fwd(h2, wgu_t, name, after=None):
    t, d = h2.shape
    f = wgu_t.shape[0] // 2
    tm = _divisor_tile(t, 1024, 128)
    tn = _divisor_tile(f, 512, 128)
    nb = f // tn

    def body(a_ref, bg_ref, bu_ref, *rest):
        gu_ref, act_ref = rest[-2:]
        for rows in _row_chunks(tm):
            av = a_ref[rows, :]
            gv = _dot(av, bg_ref[...], NT)
            uv = _dot(av, bu_ref[...], NT)
            gu_ref[0, rows, :] = gv
            gu_ref[1, rows, :] = uv
            act_ref[rows, :] = (gv * _sigmoid(gv) * uv).astype(act_ref.dtype)

    return pl.pallas_call(
        body,
        name=name,
        grid=(t // tm, nb),
        in_specs=[pl.BlockSpec((tm, d), lambda i, j: (i, 0)), pl.BlockSpec((tn, d), lambda i, j: (j, 0)),
                  pl.BlockSpec((tn, d), lambda i, j: (j + nb, 0))] + ([ANY] if after is not None else []),
        out_specs=[pl.BlockSpec((2, tm, tn), lambda i, j: (0, i, j)), pl.BlockSpec((tm, tn), lambda i, j: (i, j))],
        out_shape=[jax.ShapeDtypeStruct((2, t, f), F32), jax.ShapeDtypeStruct((t, f), BF16)],
        compiler_params=_params("parallel", "parallel"),
    )(h2, wgu_t, wgu_t, *(() if after is None else (after,)))


def _gate_up_bwd(dx16, wd, gu, name, after=None):
    t, d = dx16.shape
    f = wd.shape[0]
    tm = _divisor_tile(t, 1024, 128)
    tn = _divisor_tile(f, 512, 128)

    def body(a_ref, b_ref, gu_ref, *rest):
        o_ref = rest[-1]
        for rows in _row_chunks(tm):
            dav = _dot(a_ref[rows, :], b_ref[...], NT)
            gv = gu_ref[0, rows, :]
            sg = _sigmoid(gv)
            o_ref[0, rows, :] = (dav * gu_ref[1, rows, :] * (sg + gv * sg * (1.0 - sg))).astype(o_ref.dtype)
            o_ref[1, rows, :] = (dav * gv * sg).astype(o_ref.dtype)

    pair = pl.BlockSpec((2, tm, tn), lambda i, j: (0, i, j))
    return pl.pallas_call(
        body,
        name=name,
        grid=(t // tm, f // tn),
        in_specs=[pl.BlockSpec((tm, d), lambda i, j: (i, 0)), pl.BlockSpec((tn, d), lambda i, j: (j, 0)), pair]
        + ([ANY] if after is not None else []),
        out_specs=pair,
        out_shape=jax.ShapeDtypeStruct((2, t, f), BF16),
        compiler_params=_params("parallel", "parallel"),
    )(dx16, wd, gu, *(() if after is None else (after,)))


def _loss_and_grad(y, target, name):
    t, d = y.shape
    tr = _divisor_tile(t, 512, 8)

    def body(y_ref, t_ref, l_ref, dy_ref, dy16_ref):
        i = pl.program_id(0)
        err = y_ref[...] - t_ref[...]
        dy_ref[...] = err * (1.0 / d)
        dy16_ref[...] = (err * (1.0 / d)).astype(dy16_ref.dtype)
        part = jnp.broadcast_to(0.5 * jnp.sum(err * err) * (1.0 / d), l_ref.shape)

        @pl.when(i == 0)
        def _():
            l_ref[...] = part

        @pl.when(i > 0)
        def _():
            l_ref[...] += part

    row = pl.BlockSpec((tr, d), lambda i: (i, 0))
    return pl.pallas_call(
        body,
        name=name,
        grid=(t // tr,),
        in_specs=[row, row],
        out_specs=[pl.BlockSpec((8, LANES), lambda i: (0, 0)), row, row],
        out_shape=[jax.ShapeDtypeStruct((8, LANES), F32), jax.ShapeDtypeStruct((t, d), F32),
                   jax.ShapeDtypeStruct((t, d), BF16)],
        compiler_params=_params("arbitrary"),
    )(y, target)


def _adam_math(w, g, m, v):
    m = ADAM_B1 * m + (1.0 - ADAM_B1) * g
    v = ADAM_B2 * v + (1.0 - ADAM_B2) * (g * g)
    m_hat = m / (1.0 - ADAM_B1 ** ADAM_STEP)
    v_hat = v / (1.0 - ADAM_B2 ** ADAM_STEP)
    delta = -ADAM_LR * (m_hat / (jnp.sqrt(v_hat) + ADAM_EPS) + ADAM_WD * w)
    return delta, m, v


def _row_tile(r, c, elems=512 * 1024):
    return _divisor_tile(r, max(8, elems // c // 8 * 8), 8)


def _adam(w, grads, m, v, chip, name, after=None):
    nl, r, c = w.shape
    tr = _row_tile(r, c, 384 * 1024)
    nb = r // tr
    counts = [len(terms) for terms, _ in grads]

    def body(chip_ref, *refs):
        w_ref, m_ref, v_ref = refs[:3]
        g_ref, d_ref, nm_ref, nv_ref = refs[-4:]
        layer = pl.program_id(0)
        g, at = None, 3
        for li, n in enumerate(counts):
            total = refs[at][...].astype(F32)
            for ref in refs[at + 1:at + n]:
                total = total + ref[...].astype(F32)
            g = total if g is None else jnp.where(layer == li, total, g)
            at += n
        g_ref[...] = g
        d_ref[...], nm_ref[...], nv_ref[...] = _adam_math(w_ref[...], g, m_ref[...], v_ref[...])

    def term_spec(li, p, by_owner):
        def index(l, i, chip_ref):
            rows = jnp.where(l < li, 0, jnp.where(l > li, nb - 1, i))
            return (p, chip_ref[0] if by_owner else 0, rows, 0)
        return pl.BlockSpec((None, None, tr, c), index)

    row = pl.BlockSpec((None, tr, c), lambda l, i, chip_ref: (l, i, 0))
    specs, arrays = [], []
    for li, (terms, p) in enumerate(grads):
        for term in terms:
            specs.append(term_spec(li, p, term.shape[1] == 4))
            arrays.append(term)
    return pl.pallas_call(
        body,
        name=name,
        grid_spec=pltpu.PrefetchScalarGridSpec(
            num_scalar_prefetch=1, grid=(nl, nb),
            in_specs=[row] * 3 + specs + ([ANY] if after is not None else []), out_specs=[row] * 4),
        out_shape=[jax.ShapeDtypeStruct((nl, r, c), F32)] * 4,
        compiler_params=_params("arbitrary", "arbitrary"),
    )(chip, w, m, v, *arrays, *(() if after is None else (after,)))


def _place_shard(parts, layer, dev, out_dtype, name, after=None):
    p = len(parts)
    _, r, c = parts[0].shape
    tr = _row_tile(r, c)

    def body(dev_ref, *refs):
        o_ref = refs[-1]
        x = refs[0][...]
        for pi in range(1, p):
            x = jnp.where(pl.program_id(0) == pi, refs[pi][...], x)
        o_ref[...] = x.astype(o_ref.dtype)

    return pl.pallas_call(
        body,
        name=name,
        grid_spec=pltpu.PrefetchScalarGridSpec(
            num_scalar_prefetch=1,
            grid=(p, r // tr),
            in_specs=[pl.BlockSpec((None, tr, c), lambda pi, i, dev_ref: (layer, i, 0))] * p
            + ([ANY] if after is not None else []),
            out_specs=pl.BlockSpec((None, None, tr, c), lambda pi, i, dev_ref: (pi, dev_ref[0], i, 0)),
        ),
        out_shape=jax.ShapeDtypeStruct((p, N_DEV, r, c), out_dtype),
        compiler_params=_params("parallel", "parallel"),
    )(dev, *parts, *(() if after is None else (after,)))


def _sum_sibling(g, land, core, name):
    p, _, _, r, c = g.shape
    tr = _row_tile(r, c, 1024 * 1024)

    def body(core_ref, g_ref, l_ref, o_ref):
        o_ref[...] = (g_ref[...].astype(F32) + l_ref[...].astype(F32)).astype(o_ref.dtype)

    return pl.pallas_call(
        body,
        name=name,
        grid_spec=pltpu.PrefetchScalarGridSpec(
            num_scalar_prefetch=1,
            grid=(p, 4, r // tr),
            in_specs=[pl.BlockSpec((None, None, None, tr, c), lambda pi, q, i, core_ref: (pi, q, core_ref[0], i, 0)),
                      pl.BlockSpec((None, None, None, tr, c), lambda pi, q, i, core_ref: (pi, q, 0, i, 0))],
            out_specs=pl.BlockSpec((None, None, tr, c), lambda pi, q, i, core_ref: (pi, q, i, 0)),
        ),
        out_shape=jax.ShapeDtypeStruct((p, 4, r, c), BF16),
        compiler_params=_params("parallel", "parallel", "parallel"),
    )(core, g, land)


def _sum_chips(s, lands, chip, name):
    p, _, r, c = s.shape
    tr = _row_tile(r, c)

    def body(chip_ref, s_ref, l0_ref, l1_ref, l2_ref, o_ref):
        total = s_ref[...].astype(F32) + l0_ref[...].astype(F32)
        o_ref[...] = total + l1_ref[...].astype(F32) + l2_ref[...].astype(F32)

    land_spec = pl.BlockSpec((None, None, tr, c), lambda pi, i, chip_ref: (pi, 0, i, 0))
    return pl.pallas_call(
        body,
        name=name,
        grid_spec=pltpu.PrefetchScalarGridSpec(
            num_scalar_prefetch=1,
            grid=(p, r // tr),
            in_specs=[pl.BlockSpec((None, None, tr, c), lambda pi, i, chip_ref: (pi, chip_ref[0], i, 0)),
                      land_spec, land_spec, land_spec],
            out_specs=pl.BlockSpec((None, tr, c), lambda pi, i, chip_ref: (pi, i, 0)),
        ),
        out_shape=jax.ShapeDtypeStruct((p, r, c), F32),
        compiler_params=_params("parallel", "parallel"),
    )(chip, s, *lands)


def _small_reduce_adam(gathered, w, m, v, name):
    _, r, c = gathered.shape
    tr = _row_tile(r, c)

    def body(p_ref, w_ref, m_ref, v_ref, g_ref, d_ref, nm_ref, nv_ref):
        g = p_ref[0]
        for j in range(1, N_DEV):
            g = g + p_ref[j]
        g_ref[...] = g
        d_ref[...], nm_ref[...], nv_ref[...] = _adam_math(w_ref[...], g, m_ref[...], v_ref[...])

    row = pl.BlockSpec((tr, c), lambda i: (i, 0))
    return pl.pallas_call(
        body,
        name=name,
        grid=(r // tr,),
        in_specs=[pl.BlockSpec((N_DEV, tr, c), lambda i: (0, i, 0)), row, row, row],
        out_specs=[row] * 4,
        out_shape=[jax.ShapeDtypeStruct((r, c), F32)] * 4,
        compiler_params=_params("parallel"),
    )(gathered, w, m, v)


def _place():
    return lax.axis_index("x"), lax.axis_index("y"), lax.axis_index("c")


HBM =pl.BlockSpec(memory_space=pltpu.HBM)
SEM = pl.BlockSpec(memory_space=pltpu.SEMAPHORE)
TOKEN = pl.BlockSpec(memory_space=pltpu.VMEM)
EFFECT = pltpu.SideEffectType.DATAFLOW_SIDE_EFFECTING


def _in_hbm(a):
    return pltpu.with_memory_space_constraint(a, pltpu.HBM)


_FLIPS = {"me": (0, 0, 0), "s": (0, 0, 1), "x": (1, 0, 0), "y": (0, 1, 0), "d": (1, 1, 0)}
GATHER_STAGES = (
    (("s", "me", "all"), ("x", "me", "all"), ("y", "me", "all")),
    (("s", "x", "all"), ("s", "y", "all"), ("y", "x", "first"), ("x", "y", "second")),
    (("s", "d", "all"),),
)


def _flipped(place, *names):
    out = list(place)
    for name in names:
        out = [1 - p if f else p for p, f in zip(out, _FLIPS[name])]
    return tuple(out)


def _block_part(ref, place, part):
    px, py, pc = place
    rows = ref.shape[2]
    span = {"all": pl.ds(0, rows), "first": pl.ds(0, rows // 2), "second": pl.ds(rows // 2, rows // 2)}[part]
    return ref.at[:, pl.ds(4 * px + 2 * py + pc, 1), span]


def _split_start(bufs, moves, name, after=None):
    n, nm = len(bufs), len(moves)
    extra = 0 if after is None else 1

    def body(*refs):
        ssem, rsem = refs[n + extra], refs[n + extra + 1]
        outs, token = refs[n + extra + 2:2 * n + extra + 2], refs[2 * n + extra + 2]
        me = _place()
        for a in range(n):
            for k, (to, owner, part) in enumerate(moves):
                piece = _block_part(outs[a], _flipped(me, owner), part)
                pltpu.make_async_remote_copy(
                    src_ref=piece, dst_ref=piece, send_sem=ssem.at[nm * a + k], recv_sem=rsem.at[nm * a + k],
                    device_id=_flipped(me, to), device_id_type=MESH).start()
        token[...] = jnp.zeros_like(token)

    outs = pl.pallas_call(
        body,
        name=name,
        in_specs=[HBM] * n + [ANY] * extra,
        out_specs=[SEM, SEM] + [HBM] * n + [TOKEN],
        out_shape=[pltpu.SemaphoreType.DMA((nm * n,))] * 2 + [pltpu.HBM(b.shape, b.dtype) for b in bufs]
        + [jax.ShapeDtypeStruct((8, LANES), F32)],
        input_output_aliases={i: 2 + i for i in range(n)},
        compiler_params=pltpu.CompilerParams(has_side_effects=EFFECT),
    )(*[_in_hbm(b) for b in bufs], *(() if after is None else (after,)))
    return outs[0], outs[1], list(outs[2:2 + n]), outs[-1]


def _split_wait(send_sems, recv_sems, bufs, moves, after, name):
    n, nm = len(bufs), len(moves)

    def body(*refs):
        ins, ssem, rsem = refs[:n], refs[n], refs[n + 1]
        me = _place()
        for a in range(n):
            for k, (to, owner, part) in enumerate(moves):
                landed = _block_part(ins[a], _flipped(me, owner, to), part)
                cp = pltpu.make_async_remote_copy(
                    src_ref=landed, dst_ref=landed, send_sem=ssem.at[nm * a + k], recv_sem=rsem.at[nm * a + k],
                    device_id=_flipped(me, to), device_id_type=MESH)
                cp.wait_send()
                cp.wait_recv()

    return pl.pallas_call(
        body,
        name=name,
        in_specs=[HBM] * n + [SEM, SEM, ANY],
        out_specs=[HBM] * n,
        out_shape=[pltpu.HBM(b.shape, b.dtype) for b in bufs],
        input_output_aliases={i: i for i in range(n)},
        compiler_params=pltpu.CompilerParams(has_side_effects=EFFECT),
    )(*bufs, send_sems, recv_sems, after)


def _chips_start(sums, name, after=None):
    n = len(sums)
    extra = 0 if after is None else 1

    def body(*refs):
        refs = refs[:4 * n] + refs[4 * n + extra:]
        ssem, rsem = refs[4 * n], refs[4 * n + 1]
        src, land = refs[4 * n + 2:5 * n + 2], refs[5 * n + 2:8 * n + 2]
        token = refs[8 * n + 2]
        x, y, c = _place()
        chips = [(1 - x, y), (x, 1 - y), (1 - x, 1 - y)]
        for a in range(n):
            for k, (px, py) in enumerate(chips):
                pltpu.make_async_remote_copy(
                    src_ref=src[a].at[:, pl.ds(2 * px + py, 1)], dst_ref=land[3 * a + k], send_sem=ssem.at[3 * a + k],
                    recv_sem=rsem.at[3 * a + k], device_id=(px, py, c), device_id_type=MESH).start()
        token[...] = jnp.zeros_like(token)

    lands = []
    for s in sums:
        lands += [lax.empty((s.shape[0], 1) + s.shape[2:], s.dtype) for _ in range(3)]
    outs = pl.pallas_call(
        body,
        name=name,
        in_specs=[HBM] * (4 * n) + [ANY] * extra,
        out_specs=[SEM, SEM] + [HBM] * (4 * n) + [TOKEN],
        out_shape=[pltpu.SemaphoreType.DMA((3 * n,))] * 2 + [pltpu.HBM(b.shape, b.dtype) for b in list(sums) + lands]
        + [jax.ShapeDtypeStruct((8, LANES), F32)],
        input_output_aliases={i: 2 + i for i in range(4 * n)},
        compiler_params=pltpu.CompilerParams(has_side_effects=EFFECT),
    )(*[_in_hbm(b) for b in list(sums) + lands], *(() if after is None else (after,)))
    return outs[0], outs[1], list(outs[2:2 + n]), list(outs[2 + n:2 + 4 * n]), outs[-1]


def _chips_wait(send_sems, recv_sems, sums, lands, after, name):
    n = len(sums)

    def body(*refs):
        src, land = refs[:n], refs[n:4 * n]
        ssem, rsem = refs[4 * n], refs[4 * n + 1]
        x, y, c = _place()
        chips = [(1 - x, y), (x, 1 - y), (1 - x, 1 - y)]
        for a in range(n):
            for k, (px, py) in enumerate(chips):
                cp = pltpu.make_async_remote_copy(
                    src_ref=src[a].at[:, pl.ds(2 * px + py, 1)], dst_ref=land[3 * a + k], send_sem=ssem.at[3 * a + k],
                    recv_sem=rsem.at[3 * a + k], device_id=(px, py, c), device_id_type=MESH)
                cp.wait_send()
                cp.wait_recv()

    both = list(sums) + list(lands)
    outs = pl.pallas_call(
        body,
        name=name,
        in_specs=[HBM] * (4 * n) + [SEM, SEM, ANY],
        out_specs=[HBM] * (4 * n),
        out_shape=[pltpu.HBM(b.shape, b.dtype) for b in both],
        input_output_aliases={i: i for i in range(4 * n)},
        compiler_params=pltpu.CompilerParams(has_side_effects=EFFECT),
    )(*both, send_sems, recv_sems, after)
    return list(outs[:n]), [list(outs[n + 3 * a:n + 3 * a + 3]) for a in range(n)]


def _sibling_start(grads, name):
    n = len(grads)

    def body(*refs):
        ssem, rsem = refs[2 * n], refs[2 * n + 1]
        src, land = refs[2 * n + 2:3 * n + 2], refs[3 * n + 2:4 * n + 2]
        token = refs[4 * n + 2]
        x, y, c = _place()
        for a in range(n):
            pltpu.make_async_remote_copy(
                src_ref=src[a].at[:, :, pl.ds(1 - c, 1)], dst_ref=land[a], send_sem=ssem.at[a], recv_sem=rsem.at[a],
                device_id=(x, y, 1 - c), device_id_type=MESH).start()
        token[...] = jnp.zeros_like(token)

    lands = [lax.empty(g.shape[:2] + (1,) + g.shape[3:], g.dtype) for g in grads]
    both = list(grads) + lands
    outs = pl.pallas_call(
        body,
        name=name,
        in_specs=[HBM] * (2 * n),
        out_specs=[SEM, SEM] + [HBM] * (2 * n) + [TOKEN],
        out_shape=[pltpu.SemaphoreType.DMA((n,))] * 2 + [pltpu.HBM(b.shape, b.dtype) for b in both]
        + [jax.ShapeDtypeStruct((8, LANES), F32)],
        input_output_aliases={i: 2 + i for i in range(2 * n)},
        compiler_params=pltpu.CompilerParams(has_side_effects=EFFECT),
    )(*[_in_hbm(b) for b in both])
    return outs[0], outs[1], list(outs[2:2 + n]), list(outs[2 + n:2 + 2 * n]), outs[-1]


def _sibling_wait(send_sems, recv_sems, grads, lands, after, name):
    n = len(grads)

    def body(*refs):
        src, land = refs[:n], refs[n:2 * n]
        ssem, rsem = refs[2 * n], refs[2 * n + 1]
        x, y, c = _place()
        for a in range(n):
            cp = pltpu.make_async_remote_copy(
                src_ref=src[a].at[:, :, pl.ds(1 - c, 1)], dst_ref=land[a], send_sem=ssem.at[a], recv_sem=rsem.at[a],
                device_id=(x, y, 1 - c), device_id_type=MESH)
            cp.wait_send()
            cp.wait_recv()

    both = list(grads) + list(lands)
    outs = pl.pallas_call(
        body,
        name=name,
        in_specs=[HBM] * (2 * n) + [SEM, SEM, ANY],
        out_specs=[HBM] * (2 * n),
        out_shape=[pltpu.HBM(b.shape, b.dtype) for b in both],
        input_output_aliases={i: i for i in range(2 * n)},
        compiler_params=pltpu.CompilerParams(has_side_effects=EFFECT),
    )(*both, send_sems, recv_sems, after)
    return list(outs[:n]), list(outs[n:])


_SMALL = ("mix_norm", "q_norm", "k_norm", "sinks", "sgu_ln_g", "sgu_ln_b", "w_spatial", "b_spatial", "ffn_norm")


def _pack_rows(a):
    flat = a.reshape(-1)
    pad = (-flat.shape[0]) % LANES
    if pad:
        flat = jnp.pad(flat, (0, pad))
    return flat.reshape(-1, LANES)


def _pack(values):
    rows = jnp.concatenate([_pack_rows(values[k]) for k in _SMALL], axis=0)
    pad = (-rows.shape[0]) % 8
    if pad:
        rows = jnp.pad(rows, ((0, pad), (0, 0)))
    return rows


def _pack_layers(values):
    depth = values[_SMALL[0]].shape[0]
    pieces = []
    for k in _SMALL:
        flat = values[k].reshape(depth, -1)
        pad = (-flat.shape[1]) % LANES
        if pad:
            flat = jnp.pad(flat, ((0, 0), (0, pad)))
        pieces.append(flat.reshape(depth, -1, LANES))
    rows = jnp.concatenate(pieces, axis=1)
    pad = (-rows.shape[1]) % 8
    if pad:
        rows = jnp.pad(rows, ((0, 0), (0, pad), (0, 0)))
    return rows.reshape(-1, LANES)


def _unpack_layers(rows, like, depth):
    per_layer = rows.reshape(depth, -1, LANES)
    out, at = {}, 0
    for k in _SMALL:
        size = like[k].size
        nrows = -(-size // LANES)
        out[k] = per_layer[:, at:at + nrows].reshape(depth, -1)[:, :size].reshape((depth,) + like[k].shape)
        at += nrows
    return out


def _rope_tables(t, wq, wk):
    pos = jnp.arange(t, dtype=F32)
    inv_freq = jnp.power(ROPE_THETA, -jnp.arange(0, HEAD_DIM, 2, dtype=F32) / HEAD_DIM)
    ang = pos[:, None] * inv_freq[None, :]
    cos, sin = jnp.cos(ang), jnp.sin(ang)
    cos2, sin2 = jnp.concatenate([cos, cos], axis=1), jnp.concatenate([-sin, sin], axis=1)
    return (jnp.tile(cos2, (1, wq // HEAD_DIM)), jnp.tile(sin2, (1, wq // HEAD_DIM)),
            jnp.tile(cos2, (1, wk // HEAD_DIM)), jnp.tile(sin2, (1, wk // HEAD_DIM)))


def kernel(x, mix_norm, w_in, q_norm, k_norm, sinks, sgu_ln_g, sgu_ln_b, w_spatial, b_spatial, w_attn_branch, w_sgu_branch, w_out, ffn_norm, w_gate, w_up, w_down, loss_target, m_mix_norm, m_w_in, m_q_norm, m_k_norm, m_sinks, m_sgu_ln_g, m_sgu_ln_b, m_w_spatial, m_b_spatial, m_w_attn_branch, m_w_sgu_branch, m_w_out, m_ffn_norm, m_w_gate, m_w_up, m_w_down, v_mix_norm, v_w_in, v_q_norm, v_k_norm, v_sinks, v_sgu_ln_g, v_sgu_ln_b, v_w_spatial, v_b_spatial, v_w_attn_branch, v_w_sgu_branch, v_w_out, v_ffn_norm, v_w_gate, v_w_up, v_w_down):
    names = ("mix_norm", "w_in", "q_norm", "k_norm", "sinks", "sgu_ln_g", "sgu_ln_b", "w_spatial", "b_spatial",
             "w_attn_branch", "w_sgu_branch", "w_out", "ffn_norm", "w_gate", "w_up", "w_down")
    weights = dict(zip(names, (mix_norm, w_in, q_norm, k_norm, sinks, sgu_ln_g, sgu_ln_b, w_spatial, b_spatial,
                               w_attn_branch, w_sgu_branch, w_out, ffn_norm, w_gate, w_up, w_down)))
    mom1 = dict(zip(names, (m_mix_norm, m_w_in, m_q_norm, m_k_norm, m_sinks, m_sgu_ln_g, m_sgu_ln_b, m_w_spatial,
                            m_b_spatial, m_w_attn_branch, m_w_sgu_branch, m_w_out, m_ffn_norm, m_w_gate, m_w_up,
                            m_w_down)))
    mom2 = dict(zip(names, (v_mix_norm, v_w_in, v_q_norm, v_k_norm, v_sinks, v_sgu_ln_g, v_sgu_ln_b, v_w_spatial,
                            v_b_spatial, v_w_attn_branch, v_w_sgu_branch, v_w_out, v_ffn_norm, v_w_gate, v_w_up,
                            v_w_down)))
    depth = w_in.shape[0]
    _, t, d = x.shape
    n_q_heads = sinks.shape[1]
    wq = n_q_heads * HEAD_DIM
    wk = wq // Q_PER_KV
    ws = sgu_ln_g.shape[1]
    ng = ws // LANES
    off_u = wq + 2 * wk
    off_g = off_u + 2 * ws
    tables = _rope_tables(t, wq, wk)
    px, py, pc = _place()
    core = pc.astype(jnp.int32)[None]
    chip = (2 * px + py).astype(jnp.int32)[None]
    dev = (4 * px + 2 * py + pc).astype(jnp.int32)[None]

    layers = range(depth)
    chunks = ((0,), (1, 2, 3), (4,), (5,))
    sources = [[jnp.swapaxes(w_in, 1, 2)], [jnp.swapaxes(w_attn_branch, 1, 2)], [jnp.swapaxes(w_sgu_branch, 1, 2)],
               [w_out], [jnp.swapaxes(w_gate, 1, 2), jnp.swapaxes(w_up, 1, 2)], [w_down]]
    stream = [(l, ci) for l in layers for ci in range(len(chunks))]
    placed, state, token = {}, {}, None

    def send(key, after):
        state[key] = _split_start(placed[key], GATHER_STAGES[0], "gather_send_%d_%d" % key, after)
        return state[key][3]

    def advance(key, after, stage):
        send_sems, recv_sems, bufs, _ = state[key]
        bufs = _split_wait(send_sems, recv_sems, bufs, GATHER_STAGES[stage - 1], after, "gather_wait%d_%d_%d" % (stage, *key))
        state[key] = _split_start(bufs, GATHER_STAGES[stage], "gather_pass%d_%d_%d" % (stage, *key))
        return state[key][3]

    def relay(key, after):
        tok = advance(key, after, 1)
        at = stream.index(key)
        ahead = (1, 2) if key[1] in (0, 2) else ()
        for later in [stream[at + step] for step in ahead if at + step < len(stream)]:
            tok = send(later, tok)
        return tok

    def ready(key, after):
        send_sems, recv_sems, bufs, _ = state.pop(key)
        bufs = _split_wait(send_sems, recv_sems, bufs, GATHER_STAGES[2], after, "gather_wait3_%d_%d" % key)
        return [f.reshape(f.shape[0] * f.shape[1] * f.shape[2], f.shape[3]) for f in bufs]

    for key in stream:
        l, ci = key
        placed[key] = [_place_shard(sources[a], l, dev, BF16, f"place_shard_{l}_{a}",
                                    after=token if a == chunks[ci][0] else None) for a in chunks[ci]]
        token = send(key, None) if key == stream[0] else placed[key][-1]

    saved = []
    xl = x[0]
    going = relay((0, 0), token)
    going = advance((0, 0), going, 2)
    for l in layers:
        gq = jnp.tile(q_norm[l], n_q_heads)[None]
        gk = jnp.tile(k_norm[l], n_q_heads // Q_PER_KV)[None]
        bt = b_spatial[l].T
        h = _rmsnorm_fwd(xl, mix_norm[l][None], f"mix_norm_fwd_{l}", after=going)
        (win_t,) = ready((l, 0), h)
        proj = _mm(h, win_t, "nt", F32, f"in_proj_{l}")
        going = relay((l, 1), proj)
        attn = _attn_fwd(proj, tables, gq, gk, sinks[l], wq, wk, f"attn_fwd_{l}", after=going)
        going = advance((l, 1), attn, 2)
        sgu = _sgu_fwd(proj, sgu_ln_g[l][None], sgu_ln_b[l][None], w_spatial[l], bt, off_u, ws, f"sgu_fwd_{l}",
                       after=going)
        wab_t, wsb_t, wo = ready((l, 1), sgu)
        br_a, br_b, merged = _branches_fwd(attn, sgu, wab_t, wsb_t, proj, off_g, f"branches_{l}")
        going = relay((l, 2), merged)
        x1 = _mm(merged, wo, "nn", F32, f"out_proj_{l}", residual=xl, after=going)
        going = advance((l, 2), x1, 2)
        h2 = _rmsnorm_fwd(x1, ffn_norm[l][None], f"ffn_norm_fwd_{l}", after=going)
        (wgu_t,) = ready((l, 2), h2)
        going = relay((l, 3), h2)
        gu, act = _gate_up_fwd(h2, wgu_t, f"gate_up_{l}", after=going)
        going = advance((l, 3), act, 2)
        if l + 1 < depth:
            going = relay((l + 1, 0), going)
        (wd,) = ready((l, 3), going)
        x2 = _mm(act, wd, "nn", F32, f"down_proj_{l}", residual=x1)
        if l + 1 < depth:
            going = advance((l + 1, 0), x2, 2)
        saved.append(dict(x0=xl, h=h, proj=proj, attn=attn, sgu=sgu, br_a=br_a, br_b=br_b, merged=merged, x1=x1,
                          h2=h2, gu=gu, act=act, gq=gq, gk=gk, bt=bt, win_t=win_t, wab_t=wab_t, wsb_t=wsb_t, wo=wo,
                          wgu_t=wgu_t, wd=wd))
        xl = x2

    loss_part, dx, dx16 = _loss_and_grad(xl, loss_target[0], "loss")
    loss = lax.psum(loss_part[0, 0], ("x", "y", "c"))

    def sibling_start(grads, tag):
        shaped = []
        for g, p in grads:
            rows, c = g.shape
            shaped.append(g.reshape(p, 4, 2, rows // (8 * p), c))
        send_sems, recv_sems, shaped, lands, tok = _sibling_start(shaped, f"rs_sibling_start_{tag}")
        return (send_sems, recv_sems, shaped, lands, tag), tok

    def chips_start(state, after, first=None):
        send_sems, recv_sems, shaped, lands, tag = state
        shaped, lands = _sibling_wait(send_sems, recv_sems, shaped, lands, after, f"rs_sibling_wait_{tag}")
        sums = [_sum_sibling(g, o, core, f"rs_add_sibling_{tag}_{a}") for a, (g, o) in enumerate(zip(shaped, lands))]
        gate = None if first is None else first(sums[0])
        send_sems, recv_sems, sums, lands, tok = _chips_start(sums, f"rs_chips_start_{tag}", after=gate)
        return (send_sems, recv_sems, sums, lands, tag), tok

    def scatter_finish(state, after):
        send_sems, recv_sems, sums, lands, tag = state
        sums, lands = _chips_wait(send_sems, recv_sems, sums, lands, after, f"rs_chips_wait_{tag}")
        return [[s] + o for s, o in zip(sums, lands)]

    in_flight = [dict() for _ in layers]
    small_grads = [None] * depth
    tok, swap_in = None, None
    for l in reversed(layers):
        s = saved[l]
        dgu = _gate_up_bwd(dx16, s["wd"], s["gu"], f"d_gate_up_{l}", after=tok)
        if swap_in is not None:
            in_flight[l + 1]["in"], tok = chips_start(swap_in, dgu)
        g_wd = _mm(s["act"], dx16, "tn", BF16, f"g_w_down_{l}", after=tok)
        dh2 = _mm(dgu, s["wgu_t"], "nn", F32, f"d_h2_{l}", after=g_wd)
        g_wgu_t = _mm(dgu, s["h2"], "tn", BF16, f"g_w_gate_up_{l}", after=dh2)
        swap, tok_s = sibling_start([(g_wd, 1), (g_wgu_t, 2)], f"{l}_gate_up")
        dx1, dx1_16, g_ffn = _rmsnorm_bwd(s["x1"], ffn_norm[l][None], dh2, dx, f"ffn_norm_bwd_{l}", after=tok_s)
        d_a, d_b, dla, dlb = _branches_bwd(dx1_16, s["wo"], s["br_a"], s["br_b"], s["proj"], off_g,
                                           f"d_branches_{l}")
        in_flight[l]["gate_up"], tok = chips_start(swap, d_a)
        g_wo = _mm(s["merged"], dx1_16, "tn", BF16, f"g_w_out_{l}", after=tok)
        dattn = _mm(d_a, s["wab_t"], "nn", F32, f"d_attn_{l}", after=g_wo)
        g_wab_t = _mm(d_a, s["attn"], "tn", BF16, f"g_w_attn_branch_{l}")
        dsgu = _mm(d_b, s["wsb_t"], "nn", F32, f"d_sgu_{l}")
        g_wsb_t = _mm(d_b, s["sgu"], "tn", BF16, f"g_w_sgu_branch_{l}")
        swap, tok_s = sibling_start([(g_wab_t, 1), (g_wsb_t, 1), (g_wo, 1)], f"{l}_mix")
        dq, dk, dv, g_gq, g_gk, g_sinks = _attn_bwd(s["proj"], dattn, tables, s["gq"], s["gk"], sinks[l], wq, wk,
                                                    f"attn_bwd_{l}", after=tok_s)
        du, dvv, g_lng, g_lnb, g_ws, g_bs = _sgu_bwd(s["proj"], dsgu, sgu_ln_g[l][None], sgu_ln_b[l][None],
                                                     w_spatial[l], s["bt"], off_u, ws, f"sgu_bwd_{l}")
        dproj = jnp.concatenate([dq, dk.astype(BF16), dv.astype(BF16), du, dvv, dla, dlb], axis=1)
        dh = _mm(dproj, s["win_t"], "nn", F32, f"d_h_{l}")
        in_flight[l]["mix"], tok = chips_start(swap, dh)
        g_win_t = _mm(dproj, s["h"], "tn", BF16, f"g_w_in_{l}", after=tok)
        swap_in, tok = sibling_start([(g_win_t, 1)], f"{l}_in")
        dx, dx16, g_mix = _rmsnorm_bwd(s["x0"], mix_norm[l][None], dh, dx1, f"mix_norm_bwd_{l}", after=tok)
        small_grads[l] = dict(
            mix_norm=g_mix[0], q_norm=g_gq[0].reshape(n_q_heads, HEAD_DIM).sum(0),
            k_norm=g_gk[0].reshape(n_q_heads // Q_PER_KV, HEAD_DIM).sum(0), sinks=g_sinks[0, :n_q_heads],
            sgu_ln_g=g_lng[0], sgu_ln_b=g_lnb[0], w_spatial=g_ws, b_spatial=g_bs[:, 0, :], ffn_norm=g_ffn[0])
    grad_x = dx[None]

    result = {key: {} for key in ("grad", "delta", "m", "v")}
    layer_like = {k: weights[k][0] for k in _SMALL}
    packed_g = jnp.concatenate([_pack(small_grads[l]) for l in layers], axis=0)
    small_buf = _place_shard([packed_g[None]], 0, dev, F32, "place_small_grads", after=tok)
    send_sems, recv_sems, small_bufs, tok = _split_start([small_buf], GATHER_STAGES[0], "gather_send_small")
    small_state = [(send_sems, recv_sems, small_bufs)]

    def small_stage(stage, after):
        ssem, rsem, bufs = small_state[0]
        bufs = _split_wait(ssem, rsem, bufs, GATHER_STAGES[stage - 1], after, f"gather_wait{stage}_small")
        ssem, rsem, bufs, token = _split_start(bufs, GATHER_STAGES[stage], f"gather_pass{stage}_small")
        small_state[0] = (ssem, rsem, bufs)
        return token

    in_flight[0]["in"], tok = chips_start(swap_in, tok, first=functools.partial(small_stage, 1))

    def update(k, grads, transposed, after):
        view = (lambda a: jnp.swapaxes(a, 1, 2)) if transposed else (lambda a: a)
        outs = _adam(view(weights[k]), grads, view(mom1[k]), view(mom2[k]), chip, f"adam_{k}", after=after)
        for key, val in zip(("grad", "delta", "m", "v"), outs):
            result[key][k] = view(val)
        return outs[3]

    def plain(terms, tag):
        s, lands = terms[0], terms[1:]
        g = _sum_chips(s, lands, chip, f"rs_add_chips_{tag}")
        return [jnp.swapaxes(g, 1, 2)[:, None]]

    gate_up = [scatter_finish(in_flight[l]["gate_up"], tok) for l in reversed(layers)][::-1]
    tok = update("w_down", [(gate_up[l][0], 0) for l in layers], False, None)
    tok = small_stage(2, tok)
    tok = update("w_gate", [(gate_up[l][1], 0) for l in layers], True, tok)
    tok = update("w_up", [(gate_up[l][1], 1) for l in layers], True, tok)
    mix =[scatter_finish(in_flight[l]["mix"], tok) for l in reversed(layers)][::-1]
    tok = update("w_out", [(mix[l][2], 0) for l in layers], False, None)
    tok = update("w_attn_branch", [(plain(mix[l][0], f"{l}_attn_branch"), 0) for l in layers], False, tok)
    tok = update("w_sgu_branch", [(plain(mix[l][1], f"{l}_sgu_branch"), 0) for l in layers], False, tok)

    packed = [_pack_layers(src) for src in (weights, mom1, mom2)]
    (gathered_small,) = _split_wait(*small_state[0], GATHER_STAGES[2], tok, "gather_wait3_small")
    small = _small_reduce_adam(gathered_small[0], *packed, "small_reduce_adam")
    for key, rows in zip(("grad", "delta", "m", "v"), small):
        result[key].update(_unpack_layers(rows, layer_like, depth))

    last = [scatter_finish(in_flight[l]["in"], small[0]) for l in reversed(layers)][::-1]
    update("w_in", [(last[l][0], 0) for l in layers], True, result["v"]["ffn_norm"])

    return (loss, grad_x, *[result["grad"][k] for k in names], *[result["delta"][k] for k in names],
            *[result["m"][k] for k in names], *[result["v"][k] for k in names])
```

```python
import functools
import math

import jax
import jax.numpy as jnp
from jax import lax
from jax.experimental import pallas as pl
from jax.experimental.pallas import tpu as pltpu

F32 = jnp.float32
BF16 = jnp.bfloat16
MESH = pl.DeviceIdType.MESH
ANY = pl.BlockSpec(memory_space=pl.ANY)

N_DEV = 8
HEAD_DIM = 64
Q_PER_KV = 4
BLOCK = 128
LANES = 128
ROPE_THETA = 10000.0
EPS = 1e-6
ADAM_LR = 0.001
ADAM_B1 = 0.9
ADAM_B2 = 0.999
ADAM_EPS = 1e-08
ADAM_WD = 0.01
ADAM_STEP = 10
NEG = -1e30
VMEM_LIMIT_BYTES = 56 * 1024 * 1024

NN = ((1,), (0,))
NT = ((1,), (1,))
TN = ((0,), (0,))


def _dot(a, b, dims):
    return lax.dot_general(a, b, (dims, ((), ())), preferred_element_type=F32)


def _params(*sem):
    return pltpu.CompilerParams(dimension_semantics=sem, vmem_limit_bytes=VMEM_LIMIT_BYTES)


def _divisor_tile(n, limit, unit):
    if n <= limit:
        return n
    best = unit
    for t in range(unit, limit + 1, unit):
        if n % t == 0:
            best = t
    assert n % best == 0, (n, limit, unit)
    return best


def _row_chunks(rows, size=256):
    size = min(size, rows)
    assert rows % size == 0, (rows, size)
    return [pl.ds(start, size) for start in range(0, rows, size)]


def _mm(a, b, mode, out_dtype, name, residual=None, after=None):
    parts = a.shape[0] if a.ndim == 3 else 1
    a2 = a.shape[-2:]
    if mode == "nn":
        (m, kp), (k2, n) = a2, b.shape
        k, mp = kp * parts, m
    elif mode == "nt":
        (m, kp), (n, k2) = a2, b.shape
        k, mp = kp * parts, m
    else:
        (k, mp), (k2, n) = a2, b.shape
        m, kp = mp * parts, k
    assert k == k2, (name, a.shape, b.shape)
    tk = _divisor_tile(kp, 2816, 128)
    nk = k // tk
    tm = _divisor_tile(mp, 512 if mode == "tn" else 1024, 128)
    tn = _divisor_tile(n, 2048 if mode == "tn" else 1024, 128)
    kpb, mpb = kp // tk, mp // tm
    dims = {"nn": NN, "nt": NT, "tn": TN}[mode]
    lead = (None,) if a.ndim == 3 else ()
    if mode == "tn":
        a_index = lambda i, j, kk: (i // mpb, kk, i % mpb) if lead else (kk, i)
        a_spec = pl.BlockSpec(lead + (tk, tm), a_index)
    else:
        a_index = lambda i, j, kk: (kk // kpb, i, kk % kpb) if lead else (i, kk)
        a_spec = pl.BlockSpec(lead + (tm, tk), a_index)
    if mode == "nt":
        b_spec = pl.BlockSpec((tn, tk), lambda i, j, kk: (j, kk))
    else:
        b_spec = pl.BlockSpec((tk, tn), lambda i, j, kk: (kk, j))
    o_spec = pl.BlockSpec((tm, tn), lambda i, j, kk: (i, j))
    has_res = residual is not None

    def body(*refs):
        a_ref, b_ref = refs[:2]
        r_ref = refs[2] if has_res else None
        o_ref, acc_ref = refs[-2:]
        kk = pl.program_id(2)
        p = _dot(a_ref[...], b_ref[...], dims)

        def finish(total):
            if has_res:
                total = total + r_ref[...]
            o_ref[...] = total.astype(o_ref.dtype)

        if nk == 1:
            finish(p)
        else:
            @pl.when(kk == 0)
            def _():
                acc_ref[...] = p

            @pl.when(jnp.logical_and(kk > 0, kk < nk - 1))
            def _():
                acc_ref[...] += p

            @pl.when(kk == nk - 1)
            def _():
                finish(acc_ref[...] + p)

    in_specs = [a_spec, b_spec] + ([o_spec] if has_res else []) + ([ANY] if after is not None else [])
    args = (a, b) + ((residual,) if has_res else ()) + ((after,) if after is not None else ())
    acc_shape = (tm, tn) if nk > 1 else (8, LANES)
    return pl.pallas_call(
        body,
        name=name,
        grid=(m // tm, n // tn, nk),
        in_specs=in_specs,
        out_specs=o_spec,
        out_shape=jax.ShapeDtypeStruct((m, n), out_dtype),
        scratch_shapes=[pltpu.VMEM(acc_shape, F32)],
        compiler_params=_params("parallel", "parallel", "arbitrary"),
    )(*args)


def _rmsnorm_fwd(x, g, name, after=None):
    t, d = x.shape
    tr = _divisor_tile(t, 512, 8)

    def body(x_ref, g_ref, *rest):
        h_ref = rest[-1]
        xv = x_ref[...]
        rstd = lax.rsqrt(jnp.mean(xv * xv, axis=-1, keepdims=True) + EPS)
        h_ref[...] = (xv * rstd * g_ref[...]).astype(h_ref.dtype)

    return pl.pallas_call(
        body,
        name=name,
        grid=(t // tr,),
        in_specs=[pl.BlockSpec((tr, d), lambda i: (i, 0)), pl.BlockSpec((1, d), lambda i: (0, 0))]
        + ([ANY] if after is not None else []),
        out_specs=pl.BlockSpec((tr, d), lambda i: (i, 0)),
        out_shape=jax.ShapeDtypeStruct((t, d), BF16),
        compiler_params=_params("parallel"),
    )(x, g, *(() if after is None else (after,)))


def _rmsnorm_bwd(x, g, dh, dres, name, after=None):
    t, d = x.shape
    tr = _divisor_tile(t, 256, 8)

    def body(x_ref, g_ref, dh_ref, dres_ref, *rest):
        dx_ref, dx16_ref, dg_ref = rest[-3:]
        i = pl.program_id(0)
        xv = x_ref[...]
        rstd = lax.rsqrt(jnp.mean(xv * xv, axis=-1, keepdims=True) + EPS)
        xh = xv * rstd
        dhv = dh_ref[...]
        dxh = dhv * g_ref[...]
        dx = dres_ref[...] + rstd * (dxh - xh * jnp.mean(dxh * xh, axis=-1, keepdims=True))
        dx_ref[...] = dx
        dx16_ref[...] = dx.astype(dx16_ref.dtype)
        part = jnp.broadcast_to(jnp.sum(dhv * xh, axis=0, keepdims=True), dg_ref.shape)

        @pl.when(i == 0)
        def _():
            dg_ref[...] = part

        @pl.when(i > 0)
        def _():
            dg_ref[...] += part

    row = pl.BlockSpec((tr, d), lambda i: (i, 0))
    return pl.pallas_call(
        body,
        name=name,
        grid=(t // tr,),
        in_specs=[row, pl.BlockSpec((1, d), lambda i: (0, 0)), row, row] + ([ANY] if after is not None else []),
        out_specs=[row, row, pl.BlockSpec((8, d), lambda i: (0, 0))],
        out_shape=[jax.ShapeDtypeStruct((t, d), F32), jax.ShapeDtypeStruct((t, d), BF16),
                   jax.ShapeDtypeStruct((8, d), F32)],
        compiler_params=_params("arbitrary"),
    )(x, g, dh, dres, *(() if after is None else (after,)))


def _lane(shape):
    return lax.broadcasted_iota(jnp.int32, shape, 1)


def _group_sum64(s):
    row = lax.broadcasted_iota(jnp.int32, (LANES, LANES), 0)
    col = lax.broadcasted_iota(jnp.int32, (LANES, LANES), 1)
    ones = jnp.where((row >= HEAD_DIM) == (col >= HEAD_DIM), 1.0, 0.0).astype(BF16)
    out = []
    for t in range(s.shape[1] // LANES):
        piece = s[:, LANES * t:LANES * t + LANES]
        hi = piece.astype(BF16)
        lo = (piece - hi.astype(F32)).astype(BF16)
        out.append(_dot(hi, ones, NN) + _dot(lo, ones, NN))
    return out[0] if len(out) == 1 else jnp.concatenate(out, axis=1)


def _swap32(x):
    w = x.shape[1]
    return jnp.where((_lane(x.shape) & 32) == 0, pltpu.roll(x, w - 32, axis=1), pltpu.roll(x, 32, axis=1))


def _rope(x, c, s):
    return x * c + _swap32(x) * s


def _rope_t(dy, c, s):
    return dy * c + _swap32(dy * s)


def _head_norm(x):
    rstd = lax.rsqrt(_group_sum64(x * x) * (1.0 / HEAD_DIM) + EPS)
    return x * rstd, rstd


def _head_norm_bwd(dxh, xh, rstd):
    return rstd * (dxh - xh * (_group_sum64(dxh * xh) * (1.0 / HEAD_DIM)))


def _roll64(x):
    return pltpu.roll(x, 64, axis=1)


def _attn_specs(wq, wk):
    kb = wq // wk
    prev = lambda i: jnp.maximum(i - 1, 0)
    return dict(
        q=pl.BlockSpec((BLOCK, wq), lambda i: (i, 0)),
        kc=pl.BlockSpec((BLOCK, wk), lambda i: (i, kb)),
        kp=pl.BlockSpec((BLOCK, wk), lambda i: (prev(i), kb)),
        vc=pl.BlockSpec((BLOCK, wk), lambda i: (i, kb + 1)),
        vp=pl.BlockSpec((BLOCK, wk), lambda i: (prev(i), kb + 1)),
        tq=pl.BlockSpec((BLOCK, wq), lambda i: (i, 0)),
        tkp=pl.BlockSpec((BLOCK, wk), lambda i: (prev(i), 0)),
        gq=pl.BlockSpec((1, wq), lambda i: (0, 0)),
        gk=pl.BlockSpec((1, wk), lambda i: (0, 0)),
        sinks=pl.BlockSpec(memory_space=pltpu.SMEM),
    )


def _attn_prologue(i, q_ref, kc_ref, kp_ref, cq_ref, sq_ref, ckp_ref, skp_ref, gq_ref, gk_ref):
    wk = kc_ref.shape[1]
    cq, sq = cq_ref[...], sq_ref[...]
    ck, sk = cq[:, :wk], sq[:, :wk]
    qh, q_rstd = _head_norm(q_ref[...])
    kch, kc_rstd = _head_norm(kc_ref[...])
    kph, kp_rstd = _head_norm(kp_ref[...])
    qn = _rope(qh * gq_ref[...], cq, sq)
    knc = _rope(kch * gk_ref[...], ck, sk)
    knp = _rope(kph * gk_ref[...], ckp_ref[...], skp_ref[...])
    stacked = (Q_PER_KV * BLOCK, BLOCK)
    row = lax.broadcasted_iota(jnp.int32, stacked, 0) & (BLOCK - 1)
    col = lax.broadcasted_iota(jnp.int32, stacked, 1)
    mask_c = col <= row
    valid = jnp.logical_or(mask_c, i > 0)
    half = (lax.broadcasted_iota(jnp.int32, (BLOCK, BLOCK), 1) >= HEAD_DIM).astype(jnp.int32)
    return dict(cq=cq, sq=sq, ck=ck, sk=sk, qh=qh, q_rstd=q_rstd, kch=kch, kc_rstd=kc_rstd, kph=kph,
                kp_rstd=kp_rstd, qn=qn, knc=knc, knp=knp, mask_c=mask_c, valid=valid, half=half)


def _stack_heads(x, g, half):
    kpar = g % 2
    pieces = []
    for j in range(Q_PER_KV):
        t, e = divmod(Q_PER_KV * g + j, 2)
        piece = jnp.where(half == e, x[:, LANES * t:LANES * t + LANES], 0.0)
        pieces.append(piece if e == kpar else _roll64(piece))
    return jnp.concatenate(pieces, axis=0)


def _unstack_heads(y, g, half):
    kpar = g % 2
    slabs = {}
    for j in range(Q_PER_KV):
        t, e = divmod(Q_PER_KV * g + j, 2)
        piece = jnp.where(half == kpar, y[BLOCK * j:BLOCK * j + BLOCK], 0.0)
        piece = piece if e == kpar else _roll64(piece)
        slabs[t] = piece if t not in slabs else slabs[t] + piece
    return slabs


def _group_scores(st, g, sinks_ref, scale):
    ks = g // 2
    sl = slice(LANES * ks, LANES * ks + LANES)
    q4 = _stack_heads(st["qn"], g, st["half"]).astype(BF16)
    kc, kp = st["knc"][:, sl].astype(BF16), st["knp"][:, sl].astype(BF16)
    rows = Q_PER_KV * BLOCK
    at = lax.broadcasted_iota(jnp.int32, (rows, 1), 0)
    head = jnp.zeros((rows, 1), jnp.int32)
    sink = jnp.zeros((rows, 1), F32) + sinks_ref[Q_PER_KV * g]
    for j in range(1, Q_PER_KV):
        head = jnp.where(at >= BLOCK * j, j, head)
        sink = jnp.where(at >= BLOCK * j, sinks_ref[Q_PER_KV * g + j], sink)
    cur = st["mask_c"]
    s = jnp.where(cur, _dot(q4, kc, NT), _dot(q4, kp, NT)) * scale
    s = jnp.where(st["valid"], s, NEG)
    m = jnp.maximum(jnp.max(s, axis=1, keepdims=True), sink)
    p = jnp.exp(s - m)
    p_s = jnp.exp(sink - m)
    inv = 1.0 / (jnp.sum(p, axis=1, keepdims=True) + p_s)
    return dict(sl=sl, head=head, q4=q4, kc=kc, kp=kp, cur=cur, pr=p * inv, pr_s=p_s * inv)


def _attn_fwd(proj, tables, gq, gk, sinks, wq, wk, name, after=None):
    t = proj.shape[0]
    nb = t // BLOCK
    sp = _attn_specs(wq, wk)
    scale = HEAD_DIM ** -0.5
    cos_t, sin_t, cos_k, sin_k = tables

    def body(sinks_ref, q_ref, kc_ref, kp_ref, vc_ref, vp_ref, cq_ref, sq_ref, ckp_ref, skp_ref, gq_ref, gk_ref,
             *rest):
        o_ref = rest[-1]
        i = pl.program_id(0)
        st = _attn_prologue(i, q_ref, kc_ref, kp_ref, cq_ref, sq_ref, ckp_ref, skp_ref, gq_ref, gk_ref)
        for g in range(wq // (Q_PER_KV * HEAD_DIM)):
            gs = _group_scores(st, g, sinks_ref, scale)
            own = st["half"] == g % 2
            vc = jnp.where(own, vc_ref[:, gs["sl"]], 0.0).astype(BF16)
            vp = jnp.where(own, vp_ref[:, gs["sl"]], 0.0).astype(BF16)
            pr = gs["pr"].astype(BF16)
            zero = jnp.zeros_like(pr)
            out = _dot(jnp.where(gs["cur"], pr, zero), vc, NN) + _dot(jnp.where(gs["cur"], zero, pr), vp, NN)
            for ts, slab in _unstack_heads(out, g, st["half"]).items():
                o_ref[:, LANES * ts:LANES * ts + LANES] = slab.astype(o_ref.dtype)

    return pl.pallas_call(
        body,
        name=name,
        grid=(nb,),
        in_specs=[sp["sinks"], sp["q"], sp["kc"], sp["kp"], sp["vc"], sp["vp"], sp["tq"], sp["tq"], sp["tkp"],
                  sp["tkp"], sp["gq"], sp["gk"]] + ([ANY] if after is not None else []),
        out_specs=pl.BlockSpec((BLOCK, wq), lambda i: (i, 0)),
        out_shape=jax.ShapeDtypeStruct((t, wq), BF16),
        compiler_params=_params("parallel"),
    )(sinks, proj, proj, proj, proj, proj, cos_t, sin_t, cos_k, sin_k, gq, gk, *(() if after is None else (after,)))


def _attn_bwd(proj, dout, tables, gq, gk, sinks, wq, wk, name, after=None):
    t = proj.shape[0]
    nb = t // BLOCK
    sp = _attn_specs(wq, wk)
    scale = HEAD_DIM ** -0.5
    cos_t, sin_t, cos_k, sin_k = tables

    def body(sinks_ref, q_ref, kc_ref, kp_ref, vc_ref, vp_ref, cq_ref, sq_ref, ckp_ref, skp_ref, gq_ref, gk_ref,
             do_ref, *rest):
        dq_ref, dk_ref, dv_ref, dgq_ref, dgk_ref, dsk_ref, dqn_ref, dknc_ref, dknp_ref, dvc_ref, dvp_ref = rest[-11:]
        i = pl.program_id(0)
        st = _attn_prologue(i, q_ref, kc_ref, kp_ref, cq_ref, sq_ref, ckp_ref, skp_ref, gq_ref, gk_ref)
        dknc_ref[...] = jnp.zeros_like(dknc_ref)
        dknp_ref[...] = jnp.zeros_like(dknp_ref)
        dvc_ref[...] = jnp.zeros_like(dvc_ref)
        dvp_ref[...] = jnp.zeros_like(dvp_ref)
        lane8 = _lane((8, LANES))
        dsinks = jnp.zeros((8, LANES), F32)
        for g in range(wq // (Q_PER_KV * HEAD_DIM)):
            gs = _group_scores(st, g, sinks_ref, scale)
            sl = gs["sl"]
            do4 = _stack_heads(do_ref[...], g, st["half"]).astype(BF16)
            cur, pr = gs["cur"], gs["pr"]
            dp = jnp.where(cur, _dot(do4, vc_ref[:, sl].astype(BF16), NT), _dot(do4, vp_ref[:, sl].astype(BF16), NT))
            rs = jnp.sum(pr * dp, axis=1, keepdims=True)
            ds = (pr * (dp - rs) * scale).astype(BF16)
            pr16 = pr.astype(BF16)
            zero = jnp.zeros_like(ds)
            ds_c, ds_p = jnp.where(cur, ds, zero), jnp.where(cur, zero, ds)
            pr_c, pr_p = jnp.where(cur, pr16, zero), jnp.where(cur, zero, pr16)
            dsink_rows = -gs["pr_s"] * rs
            for j in range(Q_PER_KV):
                dsink = jnp.sum(jnp.where(gs["head"] == j, dsink_rows, 0.0))
                dsinks = dsinks + jnp.where(lane8 == Q_PER_KV * g + j, dsink, 0.0)
            dq4 = _dot(ds_c, gs["kc"], NN) + _dot(ds_p, gs["kp"], NN)
            for ts, slab in _unstack_heads(dq4, g, st["half"]).items():
                dqn_ref[:, LANES * ts:LANES * ts + LANES] = slab
            dvc_ref[:, sl] += _dot(pr_c.astype(BF16), do4, TN)
            dvp_ref[:, sl] += _dot(pr_p.astype(BF16), do4, TN)
            dknc_ref[:, sl] += _dot(ds_c, gs["q4"], TN)
            dknp_ref[:, sl] += _dot(ds_p, gs["q4"], TN)

        gqv, gkv = gq_ref[...], gk_ref[...]
        dqg = _rope_t(dqn_ref[...], st["cq"], st["sq"])
        dq_ref[...] = _head_norm_bwd(dqg * gqv, st["qh"], st["q_rstd"]).astype(dq_ref.dtype)
        dkcg = _rope_t(dknc_ref[...], st["ck"], st["sk"])
        dkpg = _rope_t(dknp_ref[...], ckp_ref[...], skp_ref[...])
        dk_cur = _head_norm_bwd(dkcg * gkv, st["kch"], st["kc_rstd"])
        dk_prev = _head_norm_bwd(dkpg * gkv, st["kph"], st["kp_rstd"])
        dgq_part = jnp.broadcast_to(jnp.sum(dqg * st["qh"], axis=0, keepdims=True), dgq_ref.shape)
        dgk_part = jnp.broadcast_to(
            jnp.sum(dkcg * st["kch"] + dkpg * st["kph"], axis=0, keepdims=True), dgk_ref.shape)
        cur = pl.ds(pl.multiple_of(i * BLOCK, BLOCK), BLOCK)
        dk_ref[cur, :] = dk_cur
        dv_ref[cur, :] = dvc_ref[...]

        @pl.when(i == 0)
        def _():
            dgq_ref[...] = dgq_part
            dgk_ref[...] = dgk_part
            dsk_ref[...] = dsinks

        @pl.when(i > 0)
        def _():
            before = pl.ds(pl.multiple_of((i - 1) * BLOCK, BLOCK), BLOCK)
            dk_ref[before, :] += dk_prev
            dv_ref[before, :] += dvp_ref[...]
            dgq_ref[...] += dgq_part
            dgk_ref[...] += dgk_part
            dsk_ref[...] += dsinks

    whole = lambda shape: pl.BlockSpec(shape, lambda i: (0, 0))
    return pl.pallas_call(
        body,
        name=name,
        grid=(nb,),
        in_specs=[sp["sinks"], sp["q"], sp["kc"], sp["kp"], sp["vc"], sp["vp"], sp["tq"], sp["tq"], sp["tkp"],
                  sp["tkp"], sp["gq"], sp["gk"], pl.BlockSpec((BLOCK, wq), lambda i: (i, 0))]
        + ([ANY] if after is not None else []),
        out_specs=[pl.BlockSpec((BLOCK, wq), lambda i: (i, 0)), whole((t, wk)), whole((t, wk)), whole((8, wq)),
                   whole((8, wk)), whole((8, LANES))],
        out_shape=[jax.ShapeDtypeStruct((t, wq), BF16), jax.ShapeDtypeStruct((t, wk), F32),
                   jax.ShapeDtypeStruct((t, wk), F32), jax.ShapeDtypeStruct((8, wq), F32),
                   jax.ShapeDtypeStruct((8, wk), F32), jax.ShapeDtypeStruct((8, LANES), F32)],
        scratch_shapes=[pltpu.VMEM((BLOCK, wq), F32), pltpu.VMEM((BLOCK, wk), F32), pltpu.VMEM((BLOCK, wk), F32),
                        pltpu.VMEM((BLOCK, wk), F32), pltpu.VMEM((BLOCK, wk), F32)],
        compiler_params=_params("arbitrary"),
    )(sinks, proj, proj, proj, proj, proj, cos_t, sin_t, cos_k, sin_k, gq, gk, dout,
      *(() if after is None else (after,)))


_GELU_K = math.sqrt(2.0 / math.pi)
_GELU_A = 0.044715


def _gelu(x):
    return 0.5 * x * (1.0 + jnp.tanh(_GELU_K * (x + _GELU_A * x * x * x)))


def _gelu_and_grad(x):
    th = jnp.tanh(_GELU_K * (x + _GELU_A * x * x * x))
    return (0.5 * x * (1.0 + th),
            0.5 * (1.0 + th) + 0.5 * x * (1.0 - th * th) * (_GELU_K * (1.0 + 3.0 * _GELU_A * x * x)))


def _group_ln(v):
    mu = jnp.mean(v, axis=1, keepdims=True)
    cen = v - mu
    rstd = lax.rsqrt(jnp.mean(cen * cen, axis=1, keepdims=True) + EPS)
    return cen * rstd, rstd


def _sgu_geometry(off_u, ws):
    cw = math.gcd(off_u, ws)
    return cw, ws // cw, off_u // cw, (off_u + ws) // cw


def _sgu_fwd(proj, ln_g, ln_b, w_s, bt, off_u, ws, name, after=None):
    t = proj.shape[0]
    nb = t // BLOCK
    cw, nc, ub, vb = _sgu_geometry(off_u, ws)
    gpc = cw // LANES
    ng = ws // LANES

    def body(u_ref, v_ref, g_ref, b_ref, w_ref, bt_ref, *rest):
        o_ref = rest[-1]
        jc = pl.program_id(0)
        row = lax.broadcasted_iota(jnp.int32, (BLOCK, BLOCK), 0)
        col = lax.broadcasted_iota(jnp.int32, (BLOCK, BLOCK), 1)
        lane_g = _lane((BLOCK, ng))
        for gi in range(gpc):
            sl = slice(LANES * gi, LANES * gi + LANES)
            xh, _ = _group_ln(_gelu(v_ref[:, sl]))
            vn = xh * g_ref[:, sl] + b_ref[:, sl]
            w = jnp.where(row >= col, w_ref[gi], 0.0).astype(BF16)
            bias = jnp.sum(jnp.where(lane_g == jc * gpc + gi, bt_ref[...], 0.0), axis=1, keepdims=True)
            s = _dot(w, vn.astype(BF16), NN) + bias
            o_ref[:, sl] = (_gelu(u_ref[:, sl]) * s).astype(o_ref.dtype)

    return pl.pallas_call(
        body,
        name=name,
        grid=(nc, nb),
        in_specs=[pl.BlockSpec((BLOCK, cw), lambda jc, i: (i, ub + jc)),
                  pl.BlockSpec((BLOCK, cw), lambda jc, i: (i, vb + jc)),
                  pl.BlockSpec((1, cw), lambda jc, i: (0, jc)),
                  pl.BlockSpec((1, cw), lambda jc, i: (0, jc)),
                  pl.BlockSpec((gpc, BLOCK, BLOCK), lambda jc, i: (jc, 0, 0)),
                  pl.BlockSpec((BLOCK, ng), lambda jc, i: (0, 0))] + ([ANY] if after is not None else []),
        out_specs=pl.BlockSpec((BLOCK, cw), lambda jc, i: (i, jc)),
        out_shape=jax.ShapeDtypeStruct((t, ws), BF16),
        compiler_params=_params("parallel", "parallel"),
    )(proj, proj, ln_g, ln_b, w_s, bt, *(() if after is None else (after,)))


def _sgu_bwd(proj, dout, ln_g, ln_b, w_s, bt, off_u, ws, name):
    t = proj.shape[0]
    nb = t // BLOCK
    cw, nc, ub, vb = _sgu_geometry(off_u, ws)
    gpc = cw // LANES
    ng = ws // LANES

    def body(u_ref, v_ref, g_ref, b_ref, w_ref, bt_ref, do_ref, du_ref, dv_ref, dg_ref, db_ref, dw_ref, dbs_ref,
             bacc_ref):
        jc = pl.program_id(0)
        i = pl.program_id(1)
        row = lax.broadcasted_iota(jnp.int32, (BLOCK, BLOCK), 0)
        col = lax.broadcasted_iota(jnp.int32, (BLOCK, BLOCK), 1)
        lane_g = _lane((BLOCK, ng))
        tri = row >= col

        @pl.when(i == 0)
        def _():
            dg_ref[...] = jnp.zeros_like(dg_ref)
            db_ref[...] = jnp.zeros_like(db_ref)
            dw_ref[...] = jnp.zeros_like(dw_ref)
            bacc_ref[...] = jnp.zeros_like(bacc_ref)

        for gi in range(gpc):
            sl = slice(LANES * gi, LANES * gi + LANES)
            u_raw, v_raw = u_ref[:, sl], v_ref[:, sl]
            u_act, u_slope = _gelu_and_grad(u_raw)
            v_act, v_slope = _gelu_and_grad(v_raw)
            xh, rstd = _group_ln(v_act)
            gam = g_ref[:, sl]
            vn = (xh * gam + b_ref[:, sl]).astype(BF16)
            w = jnp.where(tri, w_ref[gi], 0.0)
            bias = jnp.sum(jnp.where(lane_g == jc * gpc + gi, bt_ref[...], 0.0), axis=1, keepdims=True)
            s = _dot(w.astype(BF16), vn, NN) + bias
            dov = do_ref[:, sl]
            du_ref[:, sl] = (dov * s * u_slope).astype(du_ref.dtype)
            ds = dov * u_act
            ds16 = ds.astype(BF16)
            dw_ref[gi] += jnp.where(tri, _dot(ds16, vn, NT), 0.0)
            bacc_ref[gi] += ds
            dvn = _dot(w.T.astype(BF16), ds16, NN)
            dg_ref[:, sl] += jnp.broadcast_to(jnp.sum(dvn * xh, axis=0, keepdims=True), (8, LANES))
            db_ref[:, sl] += jnp.broadcast_to(jnp.sum(dvn, axis=0, keepdims=True), (8, LANES))
            dxh = dvn * gam
            dvg = rstd * (dxh - jnp.mean(dxh, axis=1, keepdims=True)
                          - xh * jnp.mean(dxh * xh, axis=1, keepdims=True))
            dv_ref[:, sl] = (dvg * v_slope).astype(dv_ref.dtype)

        @pl.when(i == nb - 1)
        def _():
            for gi in range(gpc):
                dbs_ref[gi] = jnp.broadcast_to(jnp.sum(bacc_ref[gi].T, axis=0, keepdims=True), (8, LANES))

    blk = lambda base: pl.BlockSpec((BLOCK, cw), lambda jc, i: (i, base + jc))
    vec = pl.BlockSpec((1, cw), lambda jc, i: (0, jc))
    acc = pl.BlockSpec((8, cw), lambda jc, i: (0, jc))
    wsp = pl.BlockSpec((gpc, BLOCK, BLOCK), lambda jc, i: (jc, 0, 0))
    return pl.pallas_call(
        body,
        name=name,
        grid=(nc, nb),
        in_specs=[blk(ub), blk(vb), vec, vec, wsp, pl.BlockSpec((BLOCK, ng), lambda jc, i: (0, 0)), blk(0)],
        out_specs=[blk(0), blk(0), acc, acc, wsp, pl.BlockSpec((gpc, 8, LANES), lambda jc, i: (jc, 0, 0))],
        out_shape=[jax.ShapeDtypeStruct((t, ws), BF16), jax.ShapeDtypeStruct((t, ws), BF16),
                   jax.ShapeDtypeStruct((8, ws), F32), jax.ShapeDtypeStruct((8, ws), F32),
                   jax.ShapeDtypeStruct((ng, BLOCK, BLOCK), F32), jax.ShapeDtypeStruct((ng, 8, LANES), F32)],
        scratch_shapes=[pltpu.VMEM((gpc, BLOCK, BLOCK), F32)],
        compiler_params=_params("arbitrary", "arbitrary"),
    )(proj, proj, ln_g, ln_b, w_s, bt, dout)


def _sigmoid(x):
    return 1.0 / (1.0 + jnp.exp(-x))


def _merge_geometry(off_g, d):
    cw = math.gcd(off_g, d)
    return cw, d // cw, off_g // cw, (off_g + d) // cw


def _branches_fwd(attn, sgu, wab_t, wsb_t, proj, off_g, name, after=None):
    t = attn.shape[0]
    d = wab_t.shape[0]
    tn, _, ab, bb = _merge_geometry(off_g, d)
    tm = _divisor_tile(t, 1024, 128)

    def body(a1_ref, a2_ref, b1_ref, b2_ref, la_ref, lb_ref, *rest):
        bra_ref, brb_ref, o_ref = rest[-3:]
        for rows in _row_chunks(tm):
            va = _dot(a1_ref[rows, :], b1_ref[...], NT)
            vb = _dot(a2_ref[rows, :], b2_ref[...], NT)
            bra_ref[rows, :] = va
            brb_ref[rows, :] = vb
            o_ref[rows, :] = (_sigmoid(la_ref[rows, :]) * va + _sigmoid(lb_ref[rows, :]) * vb).astype(o_ref.dtype)

    rows = lambda w: pl.BlockSpec((tm, w), lambda i, j: (i, 0))
    wrow = lambda w: pl.BlockSpec((tn, w), lambda i, j: (j, 0))
    blk = lambda base: pl.BlockSpec((tm, tn), lambda i, j: (i, base + j))
    return pl.pallas_call(
        body,
        name=name,
        grid=(t // tm, d // tn),
        in_specs=[rows(attn.shape[1]), rows(sgu.shape[1]), wrow(wab_t.shape[1]), wrow(wsb_t.shape[1]), blk(ab),
                  blk(bb)] + ([ANY] if after is not None else []),
        out_specs=[blk(0)] * 3,
        out_shape=[jax.ShapeDtypeStruct((t, d), F32), jax.ShapeDtypeStruct((t, d), F32),
                   jax.ShapeDtypeStruct((t, d), BF16)],
        compiler_params=_params("parallel", "parallel"),
    )(attn, sgu, wab_t, wsb_t, proj, proj, *(() if after is None else (after,)))


def _branches_bwd(dx16, wo, br_a, br_b, proj, off_g, name, after=None):
    t, d = br_a.shape
    tn, _, ab, bb = _merge_geometry(off_g, d)
    tm = _divisor_tile(t, 1024, 128)
    k = dx16.shape[1]

    def body(a_ref, b_ref, bra_ref, brb_ref, la_ref, lb_ref, *rest):
        da_ref, db_ref, dla_ref, dlb_ref = rest[-4:]
        for rows in _row_chunks(tm):
            dmv = _dot(a_ref[rows, :], b_ref[...], NT)
            ga, gb = _sigmoid(la_ref[rows, :]), _sigmoid(lb_ref[rows, :])
            da_ref[rows, :] = (dmv * ga).astype(da_ref.dtype)
            db_ref[rows, :] = (dmv * gb).astype(db_ref.dtype)
            dla_ref[rows, :] = (dmv * bra_ref[rows, :] * ga * (1.0 - ga)).astype(dla_ref.dtype)
            dlb_ref[rows, :] = (dmv * brb_ref[rows, :] * gb * (1.0 - gb)).astype(dlb_ref.dtype)

    blk = lambda base: pl.BlockSpec((tm, tn), lambda i, j: (i, base + j))
    return pl.pallas_call(
        body,
        name=name,
        grid=(t // tm, d // tn),
        in_specs=[pl.BlockSpec((tm, k), lambda i, j: (i, 0)), pl.BlockSpec((tn, k), lambda i, j: (j, 0)), blk(0),
                  blk(0), blk(ab), blk(bb)] + ([ANY] if after is not None else []),
        out_specs=[blk(0)] * 4,
        out_shape=[jax.ShapeDtypeStruct((t, d), BF16)] * 4,
        compiler_params=_params("parallel", "parallel"),
    )(dx16, wo, br_a, br_b, proj, proj, *(() if after is None else (after,)))


def _gate_up_fwd(h2, wgu_t, name, after=None):
    t, d = h2.shape
    f = wgu_t.shape[0] // 2
    tm = _divisor_tile(t, 1024, 128)
    tn = _divisor_tile(f, 512, 128)
    nb = f // tn

    def body(a_ref, bg_ref, bu_ref, *rest):
        gu_ref, act_ref = rest[-2:]
        for rows in _row_chunks(tm):
            av = a_ref[rows, :]
            gv = _dot(av, bg_ref[...], NT)
            uv = _dot(av, bu_ref[...], NT)
            gu_ref[0, rows, :] = gv
            gu_ref[1, rows, :] = uv
            act_ref[rows, :] = (gv * _sigmoid(gv) * uv).astype(act_ref.dtype)

    return pl.pallas_call(
        body,
        name=name,
        grid=(t // tm, nb),
        in_specs=[pl.BlockSpec((tm, d), lambda i, j: (i, 0)), pl.BlockSpec((tn, d), lambda i, j: (j, 0)),
                  pl.BlockSpec((tn, d), lambda i, j: (j + nb, 0))] + ([ANY] if after is not None else []),
        out_specs=[pl.BlockSpec((2, tm, tn), lambda i, j: (0, i, j)), pl.BlockSpec((tm, tn), lambda i, j: (i, j))],
        out_shape=[jax.ShapeDtypeStruct((2, t, f), F32), jax.ShapeDtypeStruct((t, f), BF16)],
        compiler_params=_params("parallel", "parallel"),
    )(h2, wgu_t, wgu_t, *(() if after is None else (after,)))


def _gate_up_bwd(dx16, wd, gu, name, after=None):
    t, d = dx16.shape
    f = wd.shape[0]
    tm = _divisor_tile(t, 1024, 128)
    tn = _divisor_tile(f, 512, 128)

    def body(a_ref, b_ref, gu_ref, *rest):
        o_ref = rest[-1]
        for rows in _row_chunks(tm):
            dav = _dot(a_ref[rows, :], b_ref[...], NT)
            gv = gu_ref[0, rows, :]
            sg = _sigmoid(gv)
            o_ref[0, rows, :] = (dav * gu_ref[1, rows, :] * (sg + gv * sg * (1.0 - sg))).astype(o_ref.dtype)
            o_ref[1, rows, :] = (dav * gv * sg).astype(o_ref.dtype)

    pair = pl.BlockSpec((2, tm, tn), lambda i, j: (0, i, j))
    return pl.pallas_call(
        body,
        name=name,
        grid=(t // tm, f // tn),
        in_specs=[pl.BlockSpec((tm, d), lambda i, j: (i, 0)), pl.BlockSpec((tn, d), lambda i, j: (j, 0)), pair]
        + ([ANY] if after is not None else []),
        out_specs=pair,
        out_shape=jax.ShapeDtypeStruct((2, t, f), BF16),
        compiler_params=_params("parallel", "parallel"),
    )(dx16, wd, gu, *(() if after is None else (after,)))


def _loss_and_grad(y, target, name):
    t, d = y.shape
    tr = _divisor_tile(t, 512, 8)

    def body(y_ref, t_ref, l_ref, dy_ref, dy16_ref):
        i = pl.program_id(0)
        err = y_ref[...] - t_ref[...]
        dy_ref[...] = err * (1.0 / d)
        dy16_ref[...] = (err * (1.0 / d)).astype(dy16_ref.dtype)
        part = jnp.broadcast_to(0.5 * jnp.sum(err * err) * (1.0 / d), l_ref.shape)

        @pl.when(i == 0)
        def _():
            l_ref[...] = part

        @pl.when(i > 0)
        def _():
            l_ref[...] += part

    row = pl.BlockSpec((tr, d), lambda i: (i, 0))
    return pl.pallas_call(
        body,
        name=name,
        grid=(t // tr,),
        in_specs=[row, row],
        out_specs=[pl.BlockSpec((8, LANES), lambda i: (0, 0)), row, row],
        out_shape=[jax.ShapeDtypeStruct((8, LANES), F32), jax.ShapeDtypeStruct((t, d), F32),
                   jax.ShapeDtypeStruct((t, d), BF16)],
        compiler_params=_params("arbitrary"),
    )(y, target)


def _adam_math(w, g, m, v):
    m = ADAM_B1 * m + (1.0 - ADAM_B1) * g
    v = ADAM_B2 * v + (1.0 - ADAM_B2) * (g * g)
    m_hat = m / (1.0 - ADAM_B1 ** ADAM_STEP)
    v_hat = v / (1.0 - ADAM_B2 ** ADAM_STEP)
    delta = -ADAM_LR * (m_hat / (jnp.sqrt(v_hat) + ADAM_EPS) + ADAM_WD * w)
    return delta, m, v


def _row_tile(r, c, elems=512 * 1024):
    return _divisor_tile(r, max(8, elems // c // 8 * 8), 8)


def _adam(w, grads, m, v, chip, name, after=None):
    nl, r, c = w.shape
    tr = _row_tile(r, c, 384 * 1024)
    nb = r // tr
    counts = [len(terms) for terms, _ in grads]

    def body(chip_ref, *refs):
        w_ref, m_ref, v_ref = refs[:3]
        g_ref, d_ref, nm_ref, nv_ref = refs[-4:]
        layer = pl.program_id(0)
        g, at = None, 3
        for li, n in enumerate(counts):
            total = refs[at][...].astype(F32)
            for ref in refs[at + 1:at + n]:
                total = total + ref[...].astype(F32)
            g = total if g is None else jnp.where(layer == li, total, g)
            at += n
        g_ref[...] = g
        d_ref[...], nm_ref[...], nv_ref[...] = _adam_math(w_ref[...], g, m_ref[...], v_ref[...])

    def term_spec(li, p, by_owner):
        def index(l, i, chip_ref):
            rows = jnp.where(l < li, 0, jnp.where(l > li, nb - 1, i))
            return (p, chip_ref[0] if by_owner else 0, rows, 0)
        return pl.BlockSpec((None, None, tr, c), index)

    row = pl.BlockSpec((None, tr, c), lambda l, i, chip_ref: (l, i, 0))
    specs, arrays = [], []
    for li, (terms, p) in enumerate(grads):
        for term in terms:
            specs.append(term_spec(li, p, term.shape[1] == 4))
            arrays.append(term)
    return pl.pallas_call(
        body,
        name=name,
        grid_spec=pltpu.PrefetchScalarGridSpec(
            num_scalar_prefetch=1, grid=(nl, nb),
            in_specs=[row] * 3 + specs + ([ANY] if after is not None else []), out_specs=[row] * 4),
        out_shape=[jax.ShapeDtypeStruct((nl, r, c), F32)] * 4,
        compiler_params=_params("arbitrary", "arbitrary"),
    )(chip, w, m, v, *arrays, *(() if after is None else (after,)))


def _place_shard(parts, layer, dev, out_dtype, name, after=None):
    p = len(parts)
    _, r, c = parts[0].shape
    tr = _row_tile(r, c)

    def body(dev_ref, *refs):
        o_ref = refs[-1]
        x = refs[0][...]
        for pi in range(1, p):
            x = jnp.where(pl.program_id(0) == pi, refs[pi][...], x)
        o_ref[...] = x.astype(o_ref.dtype)

    return pl.pallas_call(
        body,
        name=name,
        grid_spec=pltpu.PrefetchScalarGridSpec(
            num_scalar_prefetch=1,
            grid=(p, r // tr),
            in_specs=[pl.BlockSpec((None, tr, c), lambda pi, i, dev_ref: (layer, i, 0))] * p
            + ([ANY] if after is not None else []),
            out_specs=pl.BlockSpec((None, None, tr, c), lambda pi, i, dev_ref: (pi, dev_ref[0], i, 0)),
        ),
        out_shape=jax.ShapeDtypeStruct((p, N_DEV, r, c), out_dtype),
        compiler_params=_params("parallel", "parallel"),
    )(dev, *parts, *(() if after is None else (after,)))


def _sum_sibling(g, land, core, name):
    p, _, _, r, c = g.shape
    tr = _row_tile(r, c, 1024 * 1024)

    def body(core_ref, g_ref, l_ref, o_ref):
        o_ref[...] = (g_ref[...].astype(F32) + l_ref[...].astype(F32)).astype(o_ref.dtype)

    return pl.pallas_call(
        body,
        name=name,
        grid_spec=pltpu.PrefetchScalarGridSpec(
            num_scalar_prefetch=1,
            grid=(p, 4, r // tr),
            in_specs=[pl.BlockSpec((None, None, None, tr, c), lambda pi, q, i, core_ref: (pi, q, core_ref[0], i, 0)),
                      pl.BlockSpec((None, None, None, tr, c), lambda pi, q, i, core_ref: (pi, q, 0, i, 0))],
            out_specs=pl.BlockSpec((None, None, tr, c), lambda pi, q, i, core_ref: (pi, q, i, 0)),
        ),
        out_shape=jax.ShapeDtypeStruct((p, 4, r, c), BF16),
        compiler_params=_params("parallel", "parallel", "parallel"),
    )(core, g, land)


def _sum_chips(s, lands, chip, name):
    p, _, r, c = s.shape
    tr = _row_tile(r, c)

    def body(chip_ref, s_ref, l0_ref, l1_ref, l2_ref, o_ref):
        total = s_ref[...].astype(F32) + l0_ref[...].astype(F32)
        o_ref[...] = total + l1_ref[...].astype(F32) + l2_ref[...].astype(F32)

    land_spec = pl.BlockSpec((None, None, tr, c), lambda pi, i, chip_ref: (pi, 0, i, 0))
    return pl.pallas_call(
        body,
        name=name,
        grid_spec=pltpu.PrefetchScalarGridSpec(
            num_scalar_prefetch=1,
            grid=(p, r // tr),
            in_specs=[pl.BlockSpec((None, None, tr, c), lambda pi, i, chip_ref: (pi, chip_ref[0], i, 0)),
                      land_spec, land_spec, land_spec],
            out_specs=pl.BlockSpec((None, tr, c), lambda pi, i, chip_ref: (pi, i, 0)),
        ),
        out_shape=jax.ShapeDtypeStruct((p, r, c), F32),
        compiler_params=_params("parallel", "parallel"),
    )(chip, s, *lands)


def _small_reduce_adam(gathered, w, m, v, name):
    _, r, c = gathered.shape
    tr = _row_tile(r, c)

    def body(p_ref, w_ref, m_ref, v_ref, g_ref, d_ref, nm_ref, nv_ref):
        g = p_ref[0]
        for j in range(1, N_DEV):
            g = g + p_ref[j]
        g_ref[...] = g
        d_ref[...], nm_ref[...], nv_ref[...] = _adam_math(w_ref[...], g, m_ref[...], v_ref[...])

    row = pl.BlockSpec((tr, c), lambda i: (i, 0))
    return pl.pallas_call(
        body,
        name=name,
        grid=(r // tr,),
        in_specs=[pl.BlockSpec((N_DEV, tr, c), lambda i: (0, i, 0)), row, row, row],
        out_specs=[row] * 4,
        out_shape=[jax.ShapeDtypeStruct((r, c), F32)] * 4,
        compiler_params=_params("parallel"),
    )(gathered, w, m, v)


def _place():
    return lax.axis_index("x"), lax.axis_index("y"), lax.axis_index("c")


HBM =pl.BlockSpec(memory_space=pltpu.HBM)
SEM = pl.BlockSpec(memory_space=pltpu.SEMAPHORE)
TOKEN = pl.BlockSpec(memory_space=pltpu.VMEM)
EFFECT = pltpu.SideEffectType.DATAFLOW_SIDE_EFFECTING


def _in_hbm(a):
    return pltpu.with_memory_space_constraint(a, pltpu.HBM)


_FLIPS = {"me": (0, 0, 0), "s": (0, 0, 1), "x": (1, 0, 0), "y": (0, 1, 0), "d": (1, 1, 0)}
GATHER_STAGES = (
    (("s", "me", "all"), ("x", "me", "all"), ("y", "me", "all")),
    (("s", "x", "all"), ("s", "y", "all"), ("y", "x", "first"), ("x", "y", "second")),
    (("s", "d", "all"),),
)


def _flipped(place, *names):
    out = list(place)
    for name in names:
        out = [1 - p if f else p for p, f in zip(out, _FLIPS[name])]
    return tuple(out)


def _block_part(ref, place, part):
    px, py, pc = place
    rows = ref.shape[2]
    span = {"all": pl.ds(0, rows), "first": pl.ds(0, rows // 2), "second": pl.ds(rows // 2, rows // 2)}[part]
    return ref.at[:, pl.ds(4 * px + 2 * py + pc, 1), span]


def _split_start(bufs, moves, name, after=None):
    n, nm = len(bufs), len(moves)
    extra = 0 if after is None else 1

    def body(*refs):
        ssem, rsem = refs[n + extra], refs[n + extra + 1]
        outs, token = refs[n + extra + 2:2 * n + extra + 2], refs[2 * n + extra + 2]
        me = _place()
        for a in range(n):
            for k, (to, owner, part) in enumerate(moves):
                piece = _block_part(outs[a], _flipped(me, owner), part)
                pltpu.make_async_remote_copy(
                    src_ref=piece, dst_ref=piece, send_sem=ssem.at[nm * a + k], recv_sem=rsem.at[nm * a + k],
                    device_id=_flipped(me, to), device_id_type=MESH).start()
        token[...] = jnp.zeros_like(token)

    outs = pl.pallas_call(
        body,
        name=name,
        in_specs=[HBM] * n + [ANY] * extra,
        out_specs=[SEM, SEM] + [HBM] * n + [TOKEN],
        out_shape=[pltpu.SemaphoreType.DMA((nm * n,))] * 2 + [pltpu.HBM(b.shape, b.dtype) for b in bufs]
        + [jax.ShapeDtypeStruct((8, LANES), F32)],
        input_output_aliases={i: 2 + i for i in range(n)},
        compiler_params=pltpu.CompilerParams(has_side_effects=EFFECT),
    )(*[_in_hbm(b) for b in bufs], *(() if after is None else (after,)))
    return outs[0], outs[1], list(outs[2:2 + n]), outs[-1]


def _split_wait(send_sems, recv_sems, bufs, moves, after, name):
    n, nm = len(bufs), len(moves)

    def body(*refs):
        ins, ssem, rsem = refs[:n], refs[n], refs[n + 1]
        me = _place()
        for a in range(n):
            for k, (to, owner, part) in enumerate(moves):
                landed = _block_part(ins[a], _flipped(me, owner, to), part)
                cp = pltpu.make_async_remote_copy(
                    src_ref=landed, dst_ref=landed, send_sem=ssem.at[nm * a + k], recv_sem=rsem.at[nm * a + k],
                    device_id=_flipped(me, to), device_id_type=MESH)
                cp.wait_send()
                cp.wait_recv()

    return pl.pallas_call(
        body,
        name=name,
        in_specs=[HBM] * n + [SEM, SEM, ANY],
        out_specs=[HBM] * n,
        out_shape=[pltpu.HBM(b.shape, b.dtype) for b in bufs],
        input_output_aliases={i: i for i in range(n)},
        compiler_params=pltpu.CompilerParams(has_side_effects=EFFECT),
    )(*bufs, send_sems, recv_sems, after)


def _chips_start(sums, name, after=None):
    n = len(sums)
    extra = 0 if after is None else 1

    def body(*refs):
        refs = refs[:4 * n] + refs[4 * n + extra:]
        ssem, rsem = refs[4 * n], refs[4 * n + 1]
        src, land = refs[4 * n + 2:5 * n + 2], refs[5 * n + 2:8 * n + 2]
        token = refs[8 * n + 2]
        x, y, c = _place()
        chips = [(1 - x, y), (x, 1 - y), (1 - x, 1 - y)]
        for a in range(n):
            for k, (px, py) in enumerate(chips):
                pltpu.make_async_remote_copy(
                    src_ref=src[a].at[:, pl.ds(2 * px + py, 1)], dst_ref=land[3 * a + k], send_sem=ssem.at[3 * a + k],
                    recv_sem=rsem.at[3 * a + k], device_id=(px, py, c), device_id_type=MESH).start()
        token[...] = jnp.zeros_like(token)

    lands = []
    for s in sums:
        lands += [lax.empty((s.shape[0], 1) + s.shape[2:], s.dtype) for _ in range(3)]
    outs = pl.pallas_call(
        body,
        name=name,
        in_specs=[HBM] * (4 * n) + [ANY] * extra,
        out_specs=[SEM, SEM] + [HBM] * (4 * n) + [TOKEN],
        out_shape=[pltpu.SemaphoreType.DMA((3 * n,))] * 2 + [pltpu.HBM(b.shape, b.dtype) for b in list(sums) + lands]
        + [jax.ShapeDtypeStruct((8, LANES), F32)],
        input_output_aliases={i: 2 + i for i in range(4 * n)},
        compiler_params=pltpu.CompilerParams(has_side_effects=EFFECT),
    )(*[_in_hbm(b) for b in list(sums) + lands], *(() if after is None else (after,)))
    return outs[0], outs[1], list(outs[2:2 + n]), list(outs[2 + n:2 + 4 * n]), outs[-1]


def _chips_wait(send_sems, recv_sems, sums, lands, after, name):
    n = len(sums)

    def body(*refs):
        src, land = refs[:n], refs[n:4 * n]
        ssem, rsem = refs[4 * n], refs[4 * n + 1]
        x, y, c = _place()
        chips = [(1 - x, y), (x, 1 - y), (1 - x, 1 - y)]
        for a in range(n):
            for k, (px, py) in enumerate(chips):
                cp = pltpu.make_async_remote_copy(
                    src_ref=src[a].at[:, pl.ds(2 * px + py, 1)], dst_ref=land[3 * a + k], send_sem=ssem.at[3 * a + k],
                    recv_sem=rsem.at[3 * a + k], device_id=(px, py, c), device_id_type=MESH)
                cp.wait_send()
                cp.wait_recv()

    both = list(sums) + list(lands)
    outs = pl.pallas_call(
        body,
        name=name,
        in_specs=[HBM] * (4 * n) + [SEM, SEM, ANY],
        out_specs=[HBM] * (4 * n),
        out_shape=[pltpu.HBM(b.shape, b.dtype) for b in both],
        input_output_aliases={i: i for i in range(4 * n)},
        compiler_params=pltpu.CompilerParams(has_side_effects=EFFECT),
    )(*both, send_sems, recv_sems, after)
    return list(outs[:n]), [list(outs[n + 3 * a:n + 3 * a + 3]) for a in range(n)]


def _sibling_start(grads, name):
    n = len(grads)

    def body(*refs):
        ssem, rsem = refs[2 * n], refs[2 * n + 1]
        src, land = refs[2 * n + 2:3 * n + 2], refs[3 * n + 2:4 * n + 2]
        token = refs[4 * n + 2]
        x, y, c = _place()
        for a in range(n):
            pltpu.make_async_remote_copy(
                src_ref=src[a].at[:, :, pl.ds(1 - c, 1)], dst_ref=land[a], send_sem=ssem.at[a], recv_sem=rsem.at[a],
                device_id=(x, y, 1 - c), device_id_type=MESH).start()
        token[...] = jnp.zeros_like(token)

    lands = [lax.empty(g.shape[:2] + (1,) + g.shape[3:], g.dtype) for g in grads]
    both = list(grads) + lands
    outs = pl.pallas_call(
        body,
        name=name,
        in_specs=[HBM] * (2 * n),
        out_specs=[SEM, SEM] + [HBM] * (2 * n) + [TOKEN],
        out_shape=[pltpu.SemaphoreType.DMA((n,))] * 2 + [pltpu.HBM(b.shape, b.dtype) for b in both]
        + [jax.ShapeDtypeStruct((8, LANES), F32)],
        input_output_aliases={i: 2 + i for i in range(2 * n)},
        compiler_params=pltpu.CompilerParams(has_side_effects=EFFECT),
    )(*[_in_hbm(b) for b in both])
    return outs[0], outs[1], list(outs[2:2 + n]), list(outs[2 + n:2 + 2 * n]), outs[-1]


def _sibling_wait(send_sems, recv_sems, grads, lands, after, name):
    n = len(grads)

    def body(*refs):
        src, land = refs[:n], refs[n:2 * n]
        ssem, rsem = refs[2 * n], refs[2 * n + 1]
        x, y, c = _place()
        for a in range(n):
            cp = pltpu.make_async_remote_copy(
                src_ref=src[a].at[:, :, pl.ds(1 - c, 1)], dst_ref=land[a], send_sem=ssem.at[a], recv_sem=rsem.at[a],
                device_id=(x, y, 1 - c), device_id_type=MESH)
            cp.wait_send()
            cp.wait_recv()

    both = list(grads) + list(lands)
    outs = pl.pallas_call(
        body,
        name=name,
        in_specs=[HBM] * (2 * n) + [SEM, SEM, ANY],
        out_specs=[HBM] * (2 * n),
        out_shape=[pltpu.HBM(b.shape, b.dtype) for b in both],
        input_output_aliases={i: i for i in range(2 * n)},
        compiler_params=pltpu.CompilerParams(has_side_effects=EFFECT),
    )(*both, send_sems, recv_sems, after)
    return list(outs[:n]), list(outs[n:])


_SMALL = ("mix_norm", "q_norm", "k_norm", "sinks", "sgu_ln_g", "sgu_ln_b", "w_spatial", "b_spatial", "ffn_norm")


def _pack_rows(a):
    flat = a.reshape(-1)
    pad = (-flat.shape[0]) % LANES
    if pad:
        flat = jnp.pad(flat, (0, pad))
    return flat.reshape(-1, LANES)


def _pack(values):
    rows = jnp.concatenate([_pack_rows(values[k]) for k in _SMALL], axis=0)
    pad = (-rows.shape[0]) % 8
    if pad:
        rows = jnp.pad(rows, ((0, pad), (0, 0)))
    return rows


def _pack_layers(values):
    depth = values[_SMALL[0]].shape[0]
    pieces = []
    for k in _SMALL:
        flat = values[k].reshape(depth, -1)
        pad = (-flat.shape[1]) % LANES
        if pad:
            flat = jnp.pad(flat, ((0, 0), (0, pad)))
        pieces.append(flat.reshape(depth, -1, LANES))
    rows = jnp.concatenate(pieces, axis=1)
    pad = (-rows.shape[1]) % 8
    if pad:
        rows = jnp.pad(rows, ((0, 0), (0, pad), (0, 0)))
    return rows.reshape(-1, LANES)


def _unpack_layers(rows, like, depth):
    per_layer = rows.reshape(depth, -1, LANES)
    out, at = {}, 0
    for k in _SMALL:
        size = like[k].size
        nrows = -(-size // LANES)
        out[k] = per_layer[:, at:at + nrows].reshape(depth, -1)[:, :size].reshape((depth,) + like[k].shape)
        at += nrows
    return out


def _rope_tables(t, wq, wk):
    pos = jnp.arange(t, dtype=F32)
    inv_freq = jnp.power(ROPE_THETA, -jnp.arange(0, HEAD_DIM, 2, dtype=F32) / HEAD_DIM)
    ang = pos[:, None] * inv_freq[None, :]
    cos, sin = jnp.cos(ang), jnp.sin(ang)
    cos2, sin2 = jnp.concatenate([cos, cos], axis=1), jnp.concatenate([-sin, sin], axis=1)
    return (jnp.tile(cos2, (1, wq // HEAD_DIM)), jnp.tile(sin2, (1, wq // HEAD_DIM)),
            jnp.tile(cos2, (1, wk // HEAD_DIM)), jnp.tile(sin2, (1, wk // HEAD_DIM)))


def kernel(x, mix_norm, w_in, q_norm, k_norm, sinks, sgu_ln_g, sgu_ln_b, w_spatial, b_spatial, w_attn_branch, w_sgu_branch, w_out, ffn_norm, w_gate, w_up, w_down, loss_target, m_mix_norm, m_w_in, m_q_norm, m_k_norm, m_sinks, m_sgu_ln_g, m_sgu_ln_b, m_w_spatial, m_b_spatial, m_w_attn_branch, m_w_sgu_branch, m_w_out, m_ffn_norm, m_w_gate, m_w_up, m_w_down, v_mix_norm, v_w_in, v_q_norm, v_k_norm, v_sinks, v_sgu_ln_g, v_sgu_ln_b, v_w_spatial, v_b_spatial, v_w_attn_branch, v_w_sgu_branch, v_w_out, v_ffn_norm, v_w_gate, v_w_up, v_w_down):
    names = ("mix_norm", "w_in", "q_norm", "k_norm", "sinks", "sgu_ln_g", "sgu_ln_b", "w_spatial", "b_spatial",
             "w_attn_branch", "w_sgu_branch", "w_out", "ffn_norm", "w_gate", "w_up", "w_down")
    weights = dict(zip(names, (mix_norm, w_in, q_norm, k_norm, sinks, sgu_ln_g, sgu_ln_b, w_spatial, b_spatial,
                               w_attn_branch, w_sgu_branch, w_out, ffn_norm, w_gate, w_up, w_down)))
    mom1 = dict(zip(names, (m_mix_norm, m_w_in, m_q_norm, m_k_norm, m_sinks, m_sgu_ln_g, m_sgu_ln_b, m_w_spatial,
                            m_b_spatial, m_w_attn_branch, m_w_sgu_branch, m_w_out, m_ffn_norm, m_w_gate, m_w_up,
                            m_w_down)))
    mom2 = dict(zip(names, (v_mix_norm, v_w_in, v_q_norm, v_k_norm, v_sinks, v_sgu_ln_g, v_sgu_ln_b, v_w_spatial,
                            v_b_spatial, v_w_attn_branch, v_w_sgu_branch, v_w_out, v_ffn_norm, v_w_gate, v_w_up,
                            v_w_down)))
    depth = w_in.shape[0]
    _, t, d = x.shape
    n_q_heads = sinks.shape[1]
    wq = n_q_heads * HEAD_DIM
    wk = wq // Q_PER_KV
    ws = sgu_ln_g.shape[1]
    ng = ws // LANES
    off_u = wq + 2 * wk
    off_g = off_u + 2 * ws
    tables = _rope_tables(t, wq, wk)
    px, py, pc = _place()
    core = pc.astype(jnp.int32)[None]
    chip = (2 * px + py).astype(jnp.int32)[None]
    dev = (4 * px + 2 * py + pc).astype(jnp.int32)[None]

    layers = range(depth)
    chunks = ((0,), (1, 2, 3), (4,), (5,))
    sources = [[jnp.swapaxes(w_in, 1, 2)], [jnp.swapaxes(w_attn_branch, 1, 2)], [jnp.swapaxes(w_sgu_branch, 1, 2)],
               [w_out], [jnp.swapaxes(w_gate, 1, 2), jnp.swapaxes(w_up, 1, 2)], [w_down]]
    stream = [(l, ci) for l in layers for ci in range(len(chunks))]
    placed, state, token = {}, {}, None

    def send(key, after):
        state[key] = _split_start(placed[key], GATHER_STAGES[0], "gather_send_%d_%d" % key, after)
        return state[key][3]

    def advance(key, after, stage):
        send_sems, recv_sems, bufs, _ = state[key]
        bufs = _split_wait(send_sems, recv_sems, bufs, GATHER_STAGES[stage - 1], after, "gather_wait%d_%d_%d" % (stage, *key))
        state[key] = _split_start(bufs, GATHER_STAGES[stage], "gather_pass%d_%d_%d" % (stage, *key))
        return state[key][3]

    def relay(key, after):
        tok = advance(key, after, 1)
        at = stream.index(key)
        for later in stream[at + 2:at + 3] if at else stream[1:3]:
            tok = send(later, tok)
        return tok

    def ready(key, after):
        send_sems, recv_sems, bufs, _ = state.pop(key)
        bufs = _split_wait(send_sems, recv_sems, bufs, GATHER_STAGES[2], after, "gather_wait3_%d_%d" % key)
        return [f.reshape(f.shape[0] * f.shape[1] * f.shape[2], f.shape[3]) for f in bufs]

    for key in stream:
        l, ci = key
        placed[key] = [_place_shard(sources[a], l, dev, BF16, f"place_shard_{l}_{a}",
                                    after=token if a == chunks[ci][0] else None) for a in chunks[ci]]
        token = send(key, None) if key == stream[0] else placed[key][-1]

    saved = []
    xl = x[0]
    going = relay((0, 0), token)
    going = advance((0, 0), going, 2)
    for l in layers:
        gq = jnp.tile(q_norm[l], n_q_heads)[None]
        gk = jnp.tile(k_norm[l], n_q_heads // Q_PER_KV)[None]
        bt = b_spatial[l].T
        h = _rmsnorm_fwd(xl, mix_norm[l][None], f"mix_norm_fwd_{l}", after=going)
        (win_t,) = ready((l, 0), h)
        proj = _mm(h, win_t, "nt", F32, f"in_proj_{l}")
        going = relay((l, 1), proj)
        attn = _attn_fwd(proj, tables, gq, gk, sinks[l], wq, wk, f"attn_fwd_{l}", after=going)
        going = advance((l, 1), attn, 2)
        sgu = _sgu_fwd(proj, sgu_ln_g[l][None], sgu_ln_b[l][None], w_spatial[l], bt, off_u, ws, f"sgu_fwd_{l}",
                       after=going)
        wab_t, wsb_t, wo = ready((l, 1), sgu)
        going = relay((l, 2), wo)
        br_a, br_b, merged = _branches_fwd(attn, sgu, wab_t, wsb_t, proj, off_g, f"branches_{l}", after=going)
        x1 = _mm(merged, wo, "nn", F32, f"out_proj_{l}", residual=xl)
        going = advance((l, 2), x1, 2)
        h2 = _rmsnorm_fwd(x1, ffn_norm[l][None], f"ffn_norm_fwd_{l}", after=going)
        (wgu_t,) = ready((l, 2), h2)
        going = relay((l, 3), h2)
        gu, act = _gate_up_fwd(h2, wgu_t, f"gate_up_{l}", after=going)
        going = advance((l, 3), act, 2)
        if l + 1 < depth:
            going = relay((l + 1, 0), going)
        (wd,) = ready((l, 3), going)
        x2 = _mm(act, wd, "nn", F32, f"down_proj_{l}", residual=x1)
        if l + 1 < depth:
            going = advance((l + 1, 0), x2, 2)
        saved.append(dict(x0=xl, h=h, proj=proj, attn=attn, sgu=sgu, br_a=br_a, br_b=br_b, merged=merged, x1=x1,
                          h2=h2, gu=gu, act=act, gq=gq, gk=gk, bt=bt, win_t=win_t, wab_t=wab_t, wsb_t=wsb_t, wo=wo,
                          wgu_t=wgu_t, wd=wd))
        xl = x2

    loss_part, dx, dx16 = _loss_and_grad(xl, loss_target[0], "loss")
    loss = lax.psum(loss_part[0, 0], ("x", "y", "c"))

    def sibling_start(grads, tag):
        shaped = []
        for g, p in grads:
            rows, c = g.shape
            shaped.append(g.reshape(p, 4, 2, rows // (8 * p), c))
        send_sems, recv_sems, shaped, lands, tok = _sibling_start(shaped, f"rs_sibling_start_{tag}")
        return (send_sems, recv_sems, shaped, lands, tag), tok

    def chips_start(state, after, first=None):
        send_sems, recv_sems, shaped, lands, tag = state
        shaped, lands = _sibling_wait(send_sems, recv_sems, shaped, lands, after, f"rs_sibling_wait_{tag}")
        sums = [_sum_sibling(g, o, core, f"rs_add_sibling_{tag}_{a}") for a, (g, o) in enumerate(zip(shaped, lands))]
        gate = None if first is None else first(sums[0])
        send_sems, recv_sems, sums, lands, tok = _chips_start(sums, f"rs_chips_start_{tag}", after=gate)
        return (send_sems, recv_sems, sums, lands, tag), tok

    def scatter_finish(state, after):
        send_sems, recv_sems, sums, lands, tag = state
        sums, lands = _chips_wait(send_sems, recv_sems, sums, lands, after, f"rs_chips_wait_{tag}")
        return [[s] + o for s, o in zip(sums, lands)]

    in_flight = [dict() for _ in layers]
    small_grads = [None] * depth
    tok, swap_in = None, None
    for l in reversed(layers):
        s = saved[l]
        dgu = _gate_up_bwd(dx16, s["wd"], s["gu"], f"d_gate_up_{l}", after=tok)
        if swap_in is not None:
            in_flight[l + 1]["in"], tok = chips_start(swap_in, dgu)
        g_wd = _mm(s["act"], dx16, "tn", BF16, f"g_w_down_{l}", after=tok)
        dh2 = _mm(dgu, s["wgu_t"], "nn", F32, f"d_h2_{l}", after=g_wd)
        g_wgu_t = _mm(dgu, s["h2"], "tn", BF16, f"g_w_gate_up_{l}", after=dh2)
        swap, tok_s = sibling_start([(g_wd, 1), (g_wgu_t, 2)], f"{l}_gate_up")
        dx1, dx1_16, g_ffn = _rmsnorm_bwd(s["x1"], ffn_norm[l][None], dh2, dx, f"ffn_norm_bwd_{l}", after=tok_s)
        d_a, d_b, dla, dlb = _branches_bwd(dx1_16, s["wo"], s["br_a"], s["br_b"], s["proj"], off_g,
                                           f"d_branches_{l}")
        in_flight[l]["gate_up"], tok = chips_start(swap, d_a)
        g_wo = _mm(s["merged"], dx1_16, "tn", BF16, f"g_w_out_{l}", after=tok)
        dattn = _mm(d_a, s["wab_t"], "nn", F32, f"d_attn_{l}", after=g_wo)
        g_wab_t = _mm(d_a, s["attn"], "tn", BF16, f"g_w_attn_branch_{l}")
        dsgu = _mm(d_b, s["wsb_t"], "nn", F32, f"d_sgu_{l}")
        g_wsb_t = _mm(d_b, s["sgu"], "tn", BF16, f"g_w_sgu_branch_{l}")
        swap, tok_s = sibling_start([(g_wab_t, 1), (g_wsb_t, 1), (g_wo, 1)], f"{l}_mix")
        dq, dk, dv, g_gq, g_gk, g_sinks = _attn_bwd(s["proj"], dattn, tables, s["gq"], s["gk"], sinks[l], wq, wk,
                                                    f"attn_bwd_{l}", after=tok_s)
        du, dvv, g_lng, g_lnb, g_ws, g_bs = _sgu_bwd(s["proj"], dsgu, sgu_ln_g[l][None], sgu_ln_b[l][None],
                                                     w_spatial[l], s["bt"], off_u, ws, f"sgu_bwd_{l}")
        dproj = jnp.concatenate([dq, dk.astype(BF16), dv.astype(BF16), du, dvv, dla, dlb], axis=1)
        dh = _mm(dproj, s["win_t"], "nn", F32, f"d_h_{l}")
        in_flight[l]["mix"], tok = chips_start(swap, dh)
        g_win_t = _mm(dproj, s["h"], "tn", BF16, f"g_w_in_{l}", after=tok)
        swap_in, tok = sibling_start([(g_win_t, 1)], f"{l}_in")
        dx, dx16, g_mix = _rmsnorm_bwd(s["x0"], mix_norm[l][None], dh, dx1, f"mix_norm_bwd_{l}", after=tok)
        small_grads[l] = dict(
            mix_norm=g_mix[0], q_norm=g_gq[0].reshape(n_q_heads, HEAD_DIM).sum(0),
            k_norm=g_gk[0].reshape(n_q_heads // Q_PER_KV, HEAD_DIM).sum(0), sinks=g_sinks[0, :n_q_heads],
            sgu_ln_g=g_lng[0], sgu_ln_b=g_lnb[0], w_spatial=g_ws, b_spatial=g_bs[:, 0, :], ffn_norm=g_ffn[0])
    grad_x = dx[None]

    result = {key: {} for key in ("grad", "delta", "m", "v")}
    layer_like = {k: weights[k][0] for k in _SMALL}
    packed_g = jnp.concatenate([_pack(small_grads[l]) for l in layers], axis=0)
    small_buf = _place_shard([packed_g[None]], 0, dev, F32, "place_small_grads", after=tok)
    send_sems, recv_sems, small_bufs, tok = _split_start([small_buf], GATHER_STAGES[0], "gather_send_small")
    small_state = [(send_sems, recv_sems, small_bufs)]

    def small_stage(stage, after):
        ssem, rsem, bufs = small_state[0]
        bufs = _split_wait(ssem, rsem, bufs, GATHER_STAGES[stage - 1], after, f"gather_wait{stage}_small")
        ssem, rsem, bufs, token = _split_start(bufs, GATHER_STAGES[stage], f"gather_pass{stage}_small")
        small_state[0] = (ssem, rsem, bufs)
        return token

    in_flight[0]["in"], tok = chips_start(swap_in, tok, first=functools.partial(small_stage, 1))

    def update(k, grads, transposed, after):
        view = (lambda a: jnp.swapaxes(a, 1, 2)) if transposed else (lambda a: a)
        outs = _adam(view(weights[k]), grads, view(mom1[k]), view(mom2[k]), chip, f"adam_{k}", after=after)
        for key, val in zip(("grad", "delta", "m", "v"), outs):
            result[key][k] = view(val)
        return outs[3]

    def plain(terms, tag):
        s, lands = terms[0], terms[1:]
        g = _sum_chips(s, lands, chip, f"rs_add_chips_{tag}")
        return [jnp.swapaxes(g, 1, 2)[:, None]]

    gate_up = [scatter_finish(in_flight[l]["gate_up"], tok) for l in reversed(layers)][::-1]
    tok = update("w_down", [(gate_up[l][0], 0) for l in layers], False, None)
    tok = small_stage(2, tok)
    tok = update("w_gate", [(gate_up[l][1], 0) for l in layers], True, tok)
    tok = update("w_up", [(gate_up[l][1], 1) for l in layers], True, tok)
    mix =[scatter_finish(in_flight[l]["mix"], tok) for l in reversed(layers)][::-1]
    tok = update("w_out", [(mix[l][2], 0) for l in layers], False, None)
    tok = update("w_attn_branch", [(plain(mix[l][0], f"{l}_attn_branch"), 0) for l in layers], False, tok)
    tok = update("w_sgu_branch", [(plain(mix[l][1], f"{l}_sgu_branch"), 0) for l in layers], False, tok)

    packed = [_pack_layers(src) for src in (weights, mom1, mom2)]
    (gathered_small,) = _split_wait(*small_state[0], GATHER_STAGES[2], tok, "gather_wait3_small")
    small = _small_reduce_adam(gathered_small[0], *packed, "small_reduce_adam")
    for key, rows in zip(("grad", "delta", "m", "v"), small):
        result[key].update(_unpack_layers(rows, layer_like, depth))

    last = [scatter_finish(in_flight[l]["in"], small[0]) for l in reversed(layers)][::-1]
    update("w_in", [(last[l][0], 0) for l in layers], True, result["v"]["ffn_norm"])

    return (loss, grad_x, *[result["grad"][k] for k in names], *[result["delta"][k] for k in names],
            *[result["m"][k] for k in names], *[result["v"][k] for k in names])
```

```python
import functools
import math

import jax
import jax.numpy as jnp
from jax import lax
from jax.experimental import pallas as pl
from jax.experimental.pallas import tpu as pltpu

F32 = jnp.float32
BF16 = jnp.bfloat16
MESH = pl.DeviceIdType.MESH
ANY = pl.BlockSpec(memory_space=pl.ANY)

N_DEV = 8
HEAD_DIM = 64
Q_PER_KV = 4
BLOCK = 128
LANES = 128
ROPE_THETA = 10000.0
EPS = 1e-6
ADAM_LR = 0.001
ADAM_B1 = 0.9
ADAM_B2 = 0.999
ADAM_EPS = 1e-08
ADAM_WD = 0.01
ADAM_STEP = 10
NEG = -1e30
VMEM_LIMIT_BYTES = 56 * 1024 * 1024

NN = ((1,), (0,))
NT = ((1,), (1,))
TN = ((0,), (0,))


def _dot(a, b, dims):
    return lax.dot_general(a, b, (dims, ((), ())), preferred_element_type=F32)


def _params(*sem):
    return pltpu.CompilerParams(dimension_semantics=sem, vmem_limit_bytes=VMEM_LIMIT_BYTES)


def _divisor_tile(n, limit, unit):
    if n <= limit:
        return n
    best = unit
    for t in range(unit, limit + 1, unit):
        if n % t == 0:
            best = t
    assert n % best == 0, (n, limit, unit)
    return best


def _row_chunks(rows, size=256):
    size = min(size, rows)
    assert rows % size == 0, (rows, size)
    return [pl.ds(start, size) for start in range(0, rows, size)]


def _mm(a, b, mode, out_dtype, name, residual=None, after=None):
    parts = a.shape[0] if a.ndim == 3 else 1
    a2 = a.shape[-2:]
    if mode == "nn":
        (m, kp), (k2, n) = a2, b.shape
        k, mp = kp * parts, m
    elif mode == "nt":
        (m, kp), (n, k2) = a2, b.shape
        k, mp = kp * parts, m
    else:
        (k, mp), (k2, n) = a2, b.shape
        m, kp = mp * parts, k
    assert k == k2, (name, a.shape, b.shape)
    tk = _divisor_tile(kp, 2816, 128)
    nk = k // tk
    tm = _divisor_tile(mp, 512 if mode == "tn" else 1024, 128)
    tn = _divisor_tile(n, 2048 if mode == "tn" else 1024, 128)
    kpb, mpb = kp // tk, mp // tm
    dims = {"nn": NN, "nt": NT, "tn": TN}[mode]
    lead = (None,) if a.ndim == 3 else ()
    if mode == "tn":
        a_index = lambda i, j, kk: (i // mpb, kk, i % mpb) if lead else (kk, i)
        a_spec = pl.BlockSpec(lead + (tk, tm), a_index)
    else:
        a_index = lambda i, j, kk: (kk // kpb, i, kk % kpb) if lead else (i, kk)
        a_spec = pl.BlockSpec(lead + (tm, tk), a_index)
    if mode == "nt":
        b_spec = pl.BlockSpec((tn, tk), lambda i, j, kk: (j, kk))
    else:
        b_spec = pl.BlockSpec((tk, tn), lambda i, j, kk: (kk, j))
    o_spec = pl.BlockSpec((tm, tn), lambda i, j, kk: (i, j))
    has_res = residual is not None

    def body(*refs):
        a_ref, b_ref = refs[:2]
        r_ref = refs[2] if has_res else None
        o_ref, acc_ref = refs[-2:]
        kk = pl.program_id(2)
        p = _dot(a_ref[...], b_ref[...], dims)

        def finish(total):
            if has_res:
                total = total + r_ref[...]
            o_ref[...] = total.astype(o_ref.dtype)

        if nk == 1:
            finish(p)
        else:
            @pl.when(kk == 0)
            def _():
                acc_ref[...] = p

            @pl.when(jnp.logical_and(kk > 0, kk < nk - 1))
            def _():
                acc_ref[...] += p

            @pl.when(kk == nk - 1)
            def _():
                finish(acc_ref[...] + p)

    in_specs = [a_spec, b_spec] + ([o_spec] if has_res else []) + ([ANY] if after is not None else [])
    args = (a, b) + ((residual,) if has_res else ()) + ((after,) if after is not None else ())
    acc_shape = (tm, tn) if nk > 1 else (8, LANES)
    return pl.pallas_call(
        body,
        name=name,
        grid=(m // tm, n // tn, nk),
        in_specs=in_specs,
        out_specs=o_spec,
        out_shape=jax.ShapeDtypeStruct((m, n), out_dtype),
        scratch_shapes=[pltpu.VMEM(acc_shape, F32)],
        compiler_params=_params("parallel", "parallel", "arbitrary"),
    )(*args)


def _rmsnorm_fwd(x, g, name, after=None):
    t, d = x.shape
    tr = _divisor_tile(t, 512, 8)

    def body(x_ref, g_ref, *rest):
        h_ref = rest[-1]
        xv = x_ref[...]
        rstd = lax.rsqrt(jnp.mean(xv * xv, axis=-1, keepdims=True) + EPS)
        h_ref[...] = (xv * rstd * g_ref[...]).astype(h_ref.dtype)

    return pl.pallas_call(
        body,
        name=name,
        grid=(t // tr,),
        in_specs=[pl.BlockSpec((tr, d), lambda i: (i, 0)), pl.BlockSpec((1, d), lambda i: (0, 0))]
        + ([ANY] if after is not None else []),
        out_specs=pl.BlockSpec((tr, d), lambda i: (i, 0)),
        out_shape=jax.ShapeDtypeStruct((t, d), BF16),
        compiler_params=_params("parallel"),
    )(x, g, *(() if after is None else (after,)))


def _rmsnorm_bwd(x, g, dh, dres, name, after=None):
    t, d = x.shape
    tr = _divisor_tile(t, 256, 8)

    def body(x_ref, g_ref, dh_ref, dres_ref, *rest):
        dx_ref, dx16_ref, dg_ref = rest[-3:]
        i = pl.program_id(0)
        xv = x_ref[...]
        rstd = lax.rsqrt(jnp.mean(xv * xv, axis=-1, keepdims=True) + EPS)
        xh = xv * rstd
        dhv = dh_ref[...]
        dxh = dhv * g_ref[...]
        dx = dres_ref[...] + rstd * (dxh - xh * jnp.mean(dxh * xh, axis=-1, keepdims=True))
        dx_ref[...] = dx
        dx16_ref[...] = dx.astype(dx16_ref.dtype)
        part = jnp.broadcast_to(jnp.sum(dhv * xh, axis=0, keepdims=True), dg_ref.shape)

        @pl.when(i == 0)
        def _():
            dg_ref[...] = part

        @pl.when(i > 0)
        def _():
            dg_ref[...] += part

    row = pl.BlockSpec((tr, d), lambda i: (i, 0))
    return pl.pallas_call(
        body,
        name=name,
        grid=(t // tr,),
        in_specs=[row, pl.BlockSpec((1, d), lambda i: (0, 0)), row, row] + ([ANY] if after is not None else []),
        out_specs=[row, row, pl.BlockSpec((8, d), lambda i: (0, 0))],
        out_shape=[jax.ShapeDtypeStruct((t, d), F32), jax.ShapeDtypeStruct((t, d), BF16),
                   jax.ShapeDtypeStruct((8, d), F32)],
        compiler_params=_params("arbitrary"),
    )(x, g, dh, dres, *(() if after is None else (after,)))


def _lane(shape):
    return lax.broadcasted_iota(jnp.int32, shape, 1)


def _group_sum64(s):
    row = lax.broadcasted_iota(jnp.int32, (LANES, LANES), 0)
    col = lax.broadcasted_iota(jnp.int32, (LANES, LANES), 1)
    ones = jnp.where((row >= HEAD_DIM) == (col >= HEAD_DIM), 1.0, 0.0).astype(BF16)
    out = []
    for t in range(s.shape[1] // LANES):
        piece = s[:, LANES * t:LANES * t + LANES]
        hi = piece.astype(BF16)
        lo = (piece - hi.astype(F32)).astype(BF16)
        out.append(_dot(hi, ones, NN) + _dot(lo, ones, NN))
    return out[0] if len(out) == 1 else jnp.concatenate(out, axis=1)


def _swap32(x):
    w = x.shape[1]
    return jnp.where((_lane(x.shape) & 32) == 0, pltpu.roll(x, w - 32, axis=1), pltpu.roll(x, 32, axis=1))


def _rope(x, c, s):
    return x * c + _swap32(x) * s


def _rope_t(dy, c, s):
    return dy * c + _swap32(dy * s)


def _head_norm(x):
    rstd = lax.rsqrt(_group_sum64(x * x) * (1.0 / HEAD_DIM) + EPS)
    return x * rstd, rstd


def _head_norm_bwd(dxh, xh, rstd):
    return rstd * (dxh - xh * (_group_sum64(dxh * xh) * (1.0 / HEAD_DIM)))


def _roll64(x):
    return pltpu.roll(x, 64, axis=1)


def _attn_specs(wq, wk):
    kb = wq // wk
    prev = lambda i: jnp.maximum(i - 1, 0)
    return dict(
        q=pl.BlockSpec((BLOCK, wq), lambda i: (i, 0)),
        kc=pl.BlockSpec((BLOCK, wk), lambda i: (i, kb)),
        kp=pl.BlockSpec((BLOCK, wk), lambda i: (prev(i), kb)),
        vc=pl.BlockSpec((BLOCK, wk), lambda i: (i, kb + 1)),
        vp=pl.BlockSpec((BLOCK, wk), lambda i: (prev(i), kb + 1)),
        tq=pl.BlockSpec((BLOCK, wq), lambda i: (i, 0)),
        tkp=pl.BlockSpec((BLOCK, wk), lambda i: (prev(i), 0)),
        gq=pl.BlockSpec((1, wq), lambda i: (0, 0)),
        gk=pl.BlockSpec((1, wk), lambda i: (0, 0)),
        sinks=pl.BlockSpec(memory_space=pltpu.SMEM),
    )


def _attn_prologue(i, q_ref, kc_ref, kp_ref, cq_ref, sq_ref, ckp_ref, skp_ref, gq_ref, gk_ref):
    wk = kc_ref.shape[1]
    cq, sq = cq_ref[...], sq_ref[...]
    ck, sk = cq[:, :wk], sq[:, :wk]
    qh, q_rstd = _head_norm(q_ref[...])
    kch, kc_rstd = _head_norm(kc_ref[...])
    kph, kp_rstd = _head_norm(kp_ref[...])
    qn = _rope(qh * gq_ref[...], cq, sq)
    knc = _rope(kch * gk_ref[...], ck, sk)
    knp = _rope(kph * gk_ref[...], ckp_ref[...], skp_ref[...])
    stacked = (Q_PER_KV * BLOCK, BLOCK)
    row = lax.broadcasted_iota(jnp.int32, stacked, 0) & (BLOCK - 1)
    col = lax.broadcasted_iota(jnp.int32, stacked, 1)
    mask_c = col <= row
    valid = jnp.logical_or(mask_c, i > 0)
    half = (lax.broadcasted_iota(jnp.int32, (BLOCK, BLOCK), 1) >= HEAD_DIM).astype(jnp.int32)
    return dict(cq=cq, sq=sq, ck=ck, sk=sk, qh=qh, q_rstd=q_rstd, kch=kch, kc_rstd=kc_rstd, kph=kph,
                kp_rstd=kp_rstd, qn=qn, knc=knc, knp=knp, mask_c=mask_c, valid=valid, half=half)


def _stack_heads(x, g, half):
    kpar = g % 2
    pieces = []
    for j in range(Q_PER_KV):
        t, e = divmod(Q_PER_KV * g + j, 2)
        piece = jnp.where(half == e, x[:, LANES * t:LANES * t + LANES], 0.0)
        pieces.append(piece if e == kpar else _roll64(piece))
    return jnp.concatenate(pieces, axis=0)


def _unstack_heads(y, g, half):
    kpar = g % 2
    slabs = {}
    for j in range(Q_PER_KV):
        t, e = divmod(Q_PER_KV * g + j, 2)
        piece = jnp.where(half == kpar, y[BLOCK * j:BLOCK * j + BLOCK], 0.0)
        piece = piece if e == kpar else _roll64(piece)
        slabs[t] = piece if t not in slabs else slabs[t] + piece
    return slabs


def _group_scores(st, g, sinks_ref, scale):
    ks = g // 2
    sl = slice(LANES * ks, LANES * ks + LANES)
    q4 = _stack_heads(st["qn"], g, st["half"]).astype(BF16)
    kc, kp = st["knc"][:, sl].astype(BF16), st["knp"][:, sl].astype(BF16)
    rows = Q_PER_KV * BLOCK
    at = lax.broadcasted_iota(jnp.int32, (rows, 1), 0)
    head = jnp.zeros((rows, 1), jnp.int32)
    sink = jnp.zeros((rows, 1), F32) + sinks_ref[Q_PER_KV * g]
    for j in range(1, Q_PER_KV):
        head = jnp.where(at >= BLOCK * j, j, head)
        sink = jnp.where(at >= BLOCK * j, sinks_ref[Q_PER_KV * g + j], sink)
    cur = st["mask_c"]
    s = jnp.where(cur, _dot(q4, kc, NT), _dot(q4, kp, NT)) * scale
    s = jnp.where(st["valid"], s, NEG)
    m = jnp.maximum(jnp.max(s, axis=1, keepdims=True), sink)
    p = jnp.exp(s - m)
    p_s = jnp.exp(sink - m)
    inv = 1.0 / (jnp.sum(p, axis=1, keepdims=True) + p_s)
    return dict(sl=sl, head=head, q4=q4, kc=kc, kp=kp, cur=cur, pr=p * inv, pr_s=p_s * inv)


def _attn_fwd(proj, tables, gq, gk, sinks, wq, wk, name, after=None):
    t = proj.shape[0]
    nb = t // BLOCK
    sp = _attn_specs(wq, wk)
    scale = HEAD_DIM ** -0.5
    cos_t, sin_t, cos_k, sin_k = tables

    def body(sinks_ref, q_ref, kc_ref, kp_ref, vc_ref, vp_ref, cq_ref, sq_ref, ckp_ref, skp_ref, gq_ref, gk_ref,
             *rest):
        o_ref = rest[-1]
        i = pl.program_id(0)
        st = _attn_prologue(i, q_ref, kc_ref, kp_ref, cq_ref, sq_ref, ckp_ref, skp_ref, gq_ref, gk_ref)
        for g in range(wq // (Q_PER_KV * HEAD_DIM)):
            gs = _group_scores(st, g, sinks_ref, scale)
            own = st["half"] == g % 2
            vc = jnp.where(own, vc_ref[:, gs["sl"]], 0.0).astype(BF16)
            vp = jnp.where(own, vp_ref[:, gs["sl"]], 0.0).astype(BF16)
            pr = gs["pr"].astype(BF16)
            zero = jnp.zeros_like(pr)
            out = _dot(jnp.where(gs["cur"], pr, zero), vc, NN) + _dot(jnp.where(gs["cur"], zero, pr), vp, NN)
            for ts, slab in _unstack_heads(out, g, st["half"]).items():
                o_ref[:, LANES * ts:LANES * ts + LANES] = slab.astype(o_ref.dtype)

    return pl.pallas_call(
        body,
        name=name,
        grid=(nb,),
        in_specs=[sp["sinks"], sp["q"], sp["kc"], sp["kp"], sp["vc"], sp["vp"], sp["tq"], sp["tq"], sp["tkp"],
                  sp["tkp"], sp["gq"], sp["gk"]] + ([ANY] if after is not None else []),
        out_specs=pl.BlockSpec((BLOCK, wq), lambda i: (i, 0)),
        out_shape=jax.ShapeDtypeStruct((t, wq), BF16),
        compiler_params=_params("parallel"),
    )(sinks, proj, proj, proj, proj, proj, cos_t, sin_t, cos_k, sin_k, gq, gk, *(() if after is None else (after,)))


def _attn_bwd(proj, dout, tables, gq, gk, sinks, wq, wk, name, after=None):
    t = proj.shape[0]
    nb = t // BLOCK
    sp = _attn_specs(wq, wk)
    scale = HEAD_DIM ** -0.5
    cos_t, sin_t, cos_k, sin_k = tables

    def body(sinks_ref, q_ref, kc_ref, kp_ref, vc_ref, vp_ref, cq_ref, sq_ref, ckp_ref, skp_ref, gq_ref, gk_ref,
             do_ref, *rest):
        dq_ref, dk_ref, dv_ref, dgq_ref, dgk_ref, dsk_ref, dqn_ref, dknc_ref, dknp_ref, dvc_ref, dvp_ref = rest[-11:]
        i = pl.program_id(0)
        st = _attn_prologue(i, q_ref, kc_ref, kp_ref, cq_ref, sq_ref, ckp_ref, skp_ref, gq_ref, gk_ref)
        dknc_ref[...] = jnp.zeros_like(dknc_ref)
        dknp_ref[...] = jnp.zeros_like(dknp_ref)
        dvc_ref[...] = jnp.zeros_like(dvc_ref)
        dvp_ref[...] = jnp.zeros_like(dvp_ref)
        lane8 = _lane((8, LANES))
        dsinks = jnp.zeros((8, LANES), F32)
        for g in range(wq // (Q_PER_KV * HEAD_DIM)):
            gs = _group_scores(st, g, sinks_ref, scale)
            sl = gs["sl"]
            do4 = _stack_heads(do_ref[...], g, st["half"]).astype(BF16)
            cur, pr = gs["cur"], gs["pr"]
            dp = jnp.where(cur, _dot(do4, vc_ref[:, sl].astype(BF16), NT), _dot(do4, vp_ref[:, sl].astype(BF16), NT))
            rs = jnp.sum(pr * dp, axis=1, keepdims=True)
            ds = (pr * (dp - rs) * scale).astype(BF16)
            pr16 = pr.astype(BF16)
            zero = jnp.zeros_like(ds)
            ds_c, ds_p = jnp.where(cur, ds, zero), jnp.where(cur, zero, ds)
            pr_c, pr_p = jnp.where(cur, pr16, zero), jnp.where(cur, zero, pr16)
            dsink_rows = -gs["pr_s"] * rs
            for j in range(Q_PER_KV):
                dsink = jnp.sum(jnp.where(gs["head"] == j, dsink_rows, 0.0))
                dsinks = dsinks + jnp.where(lane8 == Q_PER_KV * g + j, dsink, 0.0)
            dq4 = _dot(ds_c, gs["kc"], NN) + _dot(ds_p, gs["kp"], NN)
            for ts, slab in _unstack_heads(dq4, g, st["half"]).items():
                dqn_ref[:, LANES * ts:LANES * ts + LANES] = slab
            dvc_ref[:, sl] += _dot(pr_c.astype(BF16), do4, TN)
            dvp_ref[:, sl] += _dot(pr_p.astype(BF16), do4, TN)
            dknc_ref[:, sl] += _dot(ds_c, gs["q4"], TN)
            dknp_ref[:, sl] += _dot(ds_p, gs["q4"], TN)

        gqv, gkv = gq_ref[...], gk_ref[...]
        dqg = _rope_t(dqn_ref[...], st["cq"], st["sq"])
        dq_ref[...] = _head_norm_bwd(dqg * gqv, st["qh"], st["q_rstd"]).astype(dq_ref.dtype)
        dkcg = _rope_t(dknc_ref[...], st["ck"], st["sk"])
        dkpg = _rope_t(dknp_ref[...], ckp_ref[...], skp_ref[...])
        dk_cur = _head_norm_bwd(dkcg * gkv, st["kch"], st["kc_rstd"])
        dk_prev = _head_norm_bwd(dkpg * gkv, st["kph"], st["kp_rstd"])
        dgq_part = jnp.broadcast_to(jnp.sum(dqg * st["qh"], axis=0, keepdims=True), dgq_ref.shape)
        dgk_part = jnp.broadcast_to(
            jnp.sum(dkcg * st["kch"] + dkpg * st["kph"], axis=0, keepdims=True), dgk_ref.shape)
        cur = pl.ds(pl.multiple_of(i * BLOCK, BLOCK), BLOCK)
        dk_ref[cur, :] = dk_cur
        dv_ref[cur, :] = dvc_ref[...]

        @pl.when(i == 0)
        def _():
            dgq_ref[...] = dgq_part
            dgk_ref[...] = dgk_part
            dsk_ref[...] = dsinks

        @pl.when(i > 0)
        def _():
            before = pl.ds(pl.multiple_of((i - 1) * BLOCK, BLOCK), BLOCK)
            dk_ref[before, :] += dk_prev
            dv_ref[before, :] += dvp_ref[...]
            dgq_ref[...] += dgq_part
            dgk_ref[...] += dgk_part
            dsk_ref[...] += dsinks

    whole = lambda shape: pl.BlockSpec(shape, lambda i: (0, 0))
    return pl.pallas_call(
        body,
        name=name,
        grid=(nb,),
        in_specs=[sp["sinks"], sp["q"], sp["kc"], sp["kp"], sp["vc"], sp["vp"], sp["tq"], sp["tq"], sp["tkp"],
                  sp["tkp"], sp["gq"], sp["gk"], pl.BlockSpec((BLOCK, wq), lambda i: (i, 0))]
        + ([ANY] if after is not None else []),
        out_specs=[pl.BlockSpec((BLOCK, wq), lambda i: (i, 0)), whole((t, wk)), whole((t, wk)), whole((8, wq)),
                   whole((8, wk)), whole((8, LANES))],
        out_shape=[jax.ShapeDtypeStruct((t, wq), BF16), jax.ShapeDtypeStruct((t, wk), F32),
                   jax.ShapeDtypeStruct((t, wk), F32), jax.ShapeDtypeStruct((8, wq), F32),
                   jax.ShapeDtypeStruct((8, wk), F32), jax.ShapeDtypeStruct((8, LANES), F32)],
        scratch_shapes=[pltpu.VMEM((BLOCK, wq), F32), pltpu.VMEM((BLOCK, wk), F32), pltpu.VMEM((BLOCK, wk), F32),
                        pltpu.VMEM((BLOCK, wk), F32), pltpu.VMEM((BLOCK, wk), F32)],
        compiler_params=_params("arbitrary"),
    )(sinks, proj, proj, proj, proj, proj, cos_t, sin_t, cos_k, sin_k, gq, gk, dout,
      *(() if after is None else (after,)))


_GELU_K = math.sqrt(2.0 / math.pi)
_GELU_A = 0.044715


def _gelu(x):
    return 0.5 * x * (1.0 + jnp.tanh(_GELU_K * (x + _GELU_A * x * x * x)))


def _gelu_and_grad(x):
    th = jnp.tanh(_GELU_K * (x + _GELU_A * x * x * x))
    return (0.5 * x * (1.0 + th),
            0.5 * (1.0 + th) + 0.5 * x * (1.0 - th * th) * (_GELU_K * (1.0 + 3.0 * _GELU_A * x * x)))


def _group_ln(v):
    mu = jnp.mean(v, axis=1, keepdims=True)
    cen = v - mu
    rstd = lax.rsqrt(jnp.mean(cen * cen, axis=1, keepdims=True) + EPS)
    return cen * rstd, rstd


def _sgu_geometry(off_u, ws):
    cw = math.gcd(off_u, ws)
    return cw, ws // cw, off_u // cw, (off_u + ws) // cw


def _sgu_fwd(proj, ln_g, ln_b, w_s, bt, off_u, ws, name, after=None):
    t = proj.shape[0]
    nb = t // BLOCK
    cw, nc, ub, vb = _sgu_geometry(off_u, ws)
    gpc = cw // LANES
    ng = ws // LANES

    def body(u_ref, v_ref, g_ref, b_ref, w_ref, bt_ref, *rest):
        o_ref = rest[-1]
        jc = pl.program_id(0)
        row = lax.broadcasted_iota(jnp.int32, (BLOCK, BLOCK), 0)
        col = lax.broadcasted_iota(jnp.int32, (BLOCK, BLOCK), 1)
        lane_g = _lane((BLOCK, ng))
        for gi in range(gpc):
            sl = slice(LANES * gi, LANES * gi + LANES)
            xh, _ = _group_ln(_gelu(v_ref[:, sl]))
            vn = xh * g_ref[:, sl] + b_ref[:, sl]
            w = jnp.where(row >= col, w_ref[gi], 0.0).astype(BF16)
            bias = jnp.sum(jnp.where(lane_g == jc * gpc + gi, bt_ref[...], 0.0), axis=1, keepdims=True)
            s = _dot(w, vn.astype(BF16), NN) + bias
            o_ref[:, sl] = (_gelu(u_ref[:, sl]) * s).astype(o_ref.dtype)

    return pl.pallas_call(
        body,
        name=name,
        grid=(nc, nb),
        in_specs=[pl.BlockSpec((BLOCK, cw), lambda jc, i: (i, ub + jc)),
                  pl.BlockSpec((BLOCK, cw), lambda jc, i: (i, vb + jc)),
                  pl.BlockSpec((1, cw), lambda jc, i: (0, jc)),
                  pl.BlockSpec((1, cw), lambda jc, i: (0, jc)),
                  pl.BlockSpec((gpc, BLOCK, BLOCK), lambda jc, i: (jc, 0, 0)),
                  pl.BlockSpec((BLOCK, ng), lambda jc, i: (0, 0))] + ([ANY] if after is not None else []),
        out_specs=pl.BlockSpec((BLOCK, cw), lambda jc, i: (i, jc)),
        out_shape=jax.ShapeDtypeStruct((t, ws), BF16),
        compiler_params=_params("parallel", "parallel"),
    )(proj, proj, ln_g, ln_b, w_s, bt, *(() if after is None else (after,)))


def _sgu_bwd(proj, dout, ln_g, ln_b, w_s, bt, off_u, ws, name):
    t = proj.shape[0]
    nb = t // BLOCK
    cw, nc, ub, vb = _sgu_geometry(off_u, ws)
    gpc = cw // LANES
    ng = ws // LANES

    def body(u_ref, v_ref, g_ref, b_ref, w_ref, bt_ref, do_ref, du_ref, dv_ref, dg_ref, db_ref, dw_ref, dbs_ref,
             bacc_ref):
        jc = pl.program_id(0)
        i = pl.program_id(1)
        row = lax.broadcasted_iota(jnp.int32, (BLOCK, BLOCK), 0)
        col = lax.broadcasted_iota(jnp.int32, (BLOCK, BLOCK), 1)
        lane_g = _lane((BLOCK, ng))
        tri = row >= col

        @pl.when(i == 0)
        def _():
            dg_ref[...] = jnp.zeros_like(dg_ref)
            db_ref[...] = jnp.zeros_like(db_ref)
            dw_ref[...] = jnp.zeros_like(dw_ref)
            bacc_ref[...] = jnp.zeros_like(bacc_ref)

        for gi in range(gpc):
            sl = slice(LANES * gi, LANES * gi + LANES)
            u_raw, v_raw = u_ref[:, sl], v_ref[:, sl]
            u_act, u_slope = _gelu_and_grad(u_raw)
            v_act, v_slope = _gelu_and_grad(v_raw)
            xh, rstd = _group_ln(v_act)
            gam = g_ref[:, sl]
            vn = (xh * gam + b_ref[:, sl]).astype(BF16)
            w = jnp.where(tri, w_ref[gi], 0.0)
            bias = jnp.sum(jnp.where(lane_g == jc * gpc + gi, bt_ref[...], 0.0), axis=1, keepdims=True)
            s = _dot(w.astype(BF16), vn, NN) + bias
            dov = do_ref[:, sl]
            du_ref[:, sl] = (dov * s * u_slope).astype(du_ref.dtype)
            ds = dov * u_act
            ds16 = ds.astype(BF16)
            dw_ref[gi] += jnp.where(tri, _dot(ds16, vn, NT), 0.0)
            bacc_ref[gi] += ds
            dvn = _dot(w.T.astype(BF16), ds16, NN)
            dg_ref[:, sl] += jnp.broadcast_to(jnp.sum(dvn * xh, axis=0, keepdims=True), (8, LANES))
            db_ref[:, sl] += jnp.broadcast_to(jnp.sum(dvn, axis=0, keepdims=True), (8, LANES))
            dxh = dvn * gam
            dvg = rstd * (dxh - jnp.mean(dxh, axis=1, keepdims=True)
                          - xh * jnp.mean(dxh * xh, axis=1, keepdims=True))
            dv_ref[:, sl] = (dvg * v_slope).astype(dv_ref.dtype)

        @pl.when(i == nb - 1)
        def _():
            for gi in range(gpc):
                dbs_ref[gi] = jnp.broadcast_to(jnp.sum(bacc_ref[gi].T, axis=0, keepdims=True), (8, LANES))

    blk = lambda base: pl.BlockSpec((BLOCK, cw), lambda jc, i: (i, base + jc))
    vec = pl.BlockSpec((1, cw), lambda jc, i: (0, jc))
    acc = pl.BlockSpec((8, cw), lambda jc, i: (0, jc))
    wsp = pl.BlockSpec((gpc, BLOCK, BLOCK), lambda jc, i: (jc, 0, 0))
    return pl.pallas_call(
        body,
        name=name,
        grid=(nc, nb),
        in_specs=[blk(ub), blk(vb), vec, vec, wsp, pl.BlockSpec((BLOCK, ng), lambda jc, i: (0, 0)), blk(0)],
        out_specs=[blk(0), blk(0), acc, acc, wsp, pl.BlockSpec((gpc, 8, LANES), lambda jc, i: (jc, 0, 0))],
        out_shape=[jax.ShapeDtypeStruct((t, ws), BF16), jax.ShapeDtypeStruct((t, ws), BF16),
                   jax.ShapeDtypeStruct((8, ws), F32), jax.ShapeDtypeStruct((8, ws), F32),
                   jax.ShapeDtypeStruct((ng, BLOCK, BLOCK), F32), jax.ShapeDtypeStruct((ng, 8, LANES), F32)],
        scratch_shapes=[pltpu.VMEM((gpc, BLOCK, BLOCK), F32)],
        compiler_params=_params("arbitrary", "arbitrary"),
    )(proj, proj, ln_g, ln_b, w_s, bt, dout)


def _sigmoid(x):
    return 1.0 / (1.0 + jnp.exp(-x))


def _merge_geometry(off_g, d):
    cw = math.gcd(off_g, d)
    return cw, d // cw, off_g // cw, (off_g + d) // cw


def _branches_fwd(attn, sgu, wab_t, wsb_t, proj, off_g, name, after=None):
    t = attn.shape[0]
    d = wab_t.shape[0]
    tn, _, ab, bb = _merge_geometry(off_g, d)
    tm = _divisor_tile(t, 1024, 128)

    def body(a1_ref, a2_ref, b1_ref, b2_ref, la_ref, lb_ref, *rest):
        bra_ref, brb_ref, o_ref = rest[-3:]
        for rows in _row_chunks(tm):
            va = _dot(a1_ref[rows, :], b1_ref[...], NT)
            vb = _dot(a2_ref[rows, :], b2_ref[...], NT)
            bra_ref[rows, :] = va
            brb_ref[rows, :] = vb
            o_ref[rows, :] = (_sigmoid(la_ref[rows, :]) * va + _sigmoid(lb_ref[rows, :]) * vb).astype(o_ref.dtype)

    rows = lambda w: pl.BlockSpec((tm, w), lambda i, j: (i, 0))
    wrow = lambda w: pl.BlockSpec((tn, w), lambda i, j: (j, 0))
    blk = lambda base: pl.BlockSpec((tm, tn), lambda i, j: (i, base + j))
    return pl.pallas_call(
        body,
        name=name,
        grid=(t // tm, d // tn),
        in_specs=[rows(attn.shape[1]), rows(sgu.shape[1]), wrow(wab_t.shape[1]), wrow(wsb_t.shape[1]), blk(ab),
                  blk(bb)] + ([ANY] if after is not None else []),
        out_specs=[blk(0)] * 3,
        out_shape=[jax.ShapeDtypeStruct((t, d), F32), jax.ShapeDtypeStruct((t, d), F32),
                   jax.ShapeDtypeStruct((t, d), BF16)],
        compiler_params=_params("parallel", "parallel"),
    )(attn, sgu, wab_t, wsb_t, proj, proj, *(() if after is None else (after,)))


def _branches_bwd(dx16, wo, br_a, br_b, proj, off_g, name, after=None):
    t, d = br_a.shape
    tn, _, ab, bb = _merge_geometry(off_g, d)
    tm = _divisor_tile(t, 1024, 128)
    k = dx16.shape[1]

    def body(a_ref, b_ref, bra_ref, brb_ref, la_ref, lb_ref, *rest):
        da_ref, db_ref, dla_ref, dlb_ref = rest[-4:]
        for rows in _row_chunks(tm):
            dmv = _dot(a_ref[rows, :], b_ref[...], NT)
            ga, gb = _sigmoid(la_ref[rows, :]), _sigmoid(lb_ref[rows, :])
            da_ref[rows, :] = (dmv * ga).astype(da_ref.dtype)
            db_ref[rows, :] = (dmv * gb).astype(db_ref.dtype)
            dla_ref[rows, :] = (dmv * bra_ref[rows, :] * ga * (1.0 - ga)).astype(dla_ref.dtype)
            dlb_ref[rows, :] = (dmv * brb_ref[rows, :] * gb * (1.0 - gb)).astype(dlb_ref.dtype)

    blk = lambda base: pl.BlockSpec((tm, tn), lambda i, j: (i, base + j))
    return pl.pallas_call(
        body,
        name=name,
        grid=(t // tm, d // tn),
        in_specs=[pl.BlockSpec((tm, k), lambda i, j: (i, 0)), pl.BlockSpec((tn, k), lambda i, j: (j, 0)), blk(0),
                  blk(0), blk(ab), blk(bb)] + ([ANY] if after is not None else []),
        out_specs=[blk(0)] * 4,
        out_shape=[jax.ShapeDtypeStruct((t, d), BF16)] * 4,
        compiler_params=_params("parallel", "parallel"),
    )(dx16, wo, br_a, br_b, proj, proj, *(() if after is None else (after,)))


def _gate_up_fwd(h2, wgu_t, name, after=None):
    t, d = h2.shape
    f = wgu_t.shape[0] // 2
    tm = _divisor_tile(t, 1024, 128)
    tn = _divisor_tile(f, 512, 128)
    nb = f // tn

    def body(a_ref, bg_ref, bu_ref, *rest):
        gu_ref, act_ref = rest[-2:]
        for rows in _row_chunks(tm):
            av = a_ref[rows, :]
            gv = _dot(av, bg_ref[...], NT)
            uv = _dot(av, bu_ref[...], NT)
            gu_ref[0, rows, :] = gv
            gu_ref[1, rows, :] = uv
            act_ref[rows, :] = (gv * _sigmoid(gv) * uv).astype(act_ref.dtype)

    return pl.pallas_call(
        body,
        name=name,
        grid=(t // tm, nb),
        in_specs=[pl.BlockSpec((tm, d), lambda i, j: (i, 0)), pl.BlockSpec((tn, d), lambda i, j: (j, 0)),
                  pl.BlockSpec((tn, d), lambda i, j: (j + nb, 0))] + ([ANY] if after is not None else []),
        out_specs=[pl.BlockSpec((2, tm, tn), lambda i, j: (0, i, j)), pl.BlockSpec((tm, tn), lambda i, j: (i, j))],
        out_shape=[jax.ShapeDtypeStruct((2, t, f), F32), jax.ShapeDtypeStruct((t, f), BF16)],
        compiler_params=_params("parallel", "parallel"),
    )(h2, wgu_t, wgu_t, *(() if after is None else (after,)))


def _gate_up_bwd(dx16, wd, gu, name, after=None):
    t, d = dx16.shape
    f = wd.shape[0]
    tm = _divisor_tile(t, 1024, 128)
    tn = _divisor_tile(f, 512, 128)

    def body(a_ref, b_ref, gu_ref, *rest):
        o_ref = rest[-1]
        for rows in _row_chunks(tm):
            dav = _dot(a_ref[rows, :], b_ref[...], NT)
            gv = gu_ref[0, rows, :]
            sg = _sigmoid(gv)
            o_ref[0, rows, :] = (dav * gu_ref[1, rows, :] * (sg + gv * sg * (1.0 - sg))).astype(o_ref.dtype)
            o_ref[1, rows, :] = (dav * gv * sg).astype(o_ref.dtype)

    pair = pl.BlockSpec((2, tm, tn), lambda i, j: (0, i, j))
    return pl.pallas_call(
        body,
        name=name,
        grid=(t // tm, f // tn),
        in_specs=[pl.BlockSpec((tm, d), lambda i, j: (i, 0)), pl.BlockSpec((tn, d), lambda i, j: (j, 0)), pair]
        + ([ANY] if after is not None else []),
        out_specs=pair,
        out_shape=jax.ShapeDtypeStruct((2, t, f), BF16),
        compiler_params=_params("parallel", "parallel"),
    )(dx16, wd, gu, *(() if after is None else (after,)))


def _loss_and_grad(y, target, name):
    t, d = y.shape
    tr = _divisor_tile(t, 512, 8)

    def body(y_ref, t_ref, l_ref, dy_ref, dy16_ref):
        i = pl.program_id(0)
        err = y_ref[...] - t_ref[...]
        dy_ref[...] = err * (1.0 / d)
        dy16_ref[...] = (err * (1.0 / d)).astype(dy16_ref.dtype)
        part = jnp.broadcast_to(0.5 * jnp.sum(err * err) * (1.0 / d), l_ref.shape)

        @pl.when(i == 0)
        def _():
            l_ref[...] = part

        @pl.when(i > 0)
        def _():
            l_ref[...] += part

    row = pl.BlockSpec((tr, d), lambda i: (i, 0))
    return pl.pallas_call(
        body,
        name=name,
        grid=(t // tr,),
        in_specs=[row, row],
        out_specs=[pl.BlockSpec((8, LANES), lambda i: (0, 0)), row, row],
        out_shape=[jax.ShapeDtypeStruct((8, LANES), F32), jax.ShapeDtypeStruct((t, d), F32),
                   jax.ShapeDtypeStruct((t, d), BF16)],
        compiler_params=_params("arbitrary"),
    )(y, target)


def _adam_math(w, g, m, v):
    m = ADAM_B1 * m + (1.0 - ADAM_B1) * g
    v = ADAM_B2 * v + (1.0 - ADAM_B2) * (g * g)
    m_hat = m / (1.0 - ADAM_B1 ** ADAM_STEP)
    v_hat = v / (1.0 - ADAM_B2 ** ADAM_STEP)
    delta = -ADAM_LR * (m_hat / (jnp.sqrt(v_hat) + ADAM_EPS) + ADAM_WD * w)
    return delta, m, v


def _row_tile(r, c, elems=512 * 1024):
    return _divisor_tile(r, max(8, elems // c // 8 * 8), 8)


def _adam(w, grads, m, v, chip, name, after=None):
    nl, r, c = w.shape
    tr = _row_tile(r, c, 384 * 1024)
    nb = r // tr
    counts = [len(terms) for terms, _ in grads]

    def body(chip_ref, *refs):
        w_ref, m_ref, v_ref = refs[:3]
        g_ref, d_ref, nm_ref, nv_ref = refs[-4:]
        layer = pl.program_id(0)
        g, at = None, 3
        for li, n in enumerate(counts):
            total = refs[at][...].astype(F32)
            for ref in refs[at + 1:at + n]:
                total = total + ref[...].astype(F32)
            g = total if g is None else jnp.where(layer == li, total, g)
            at += n
        g_ref[...] = g
        d_ref[...], nm_ref[...], nv_ref[...] = _adam_math(w_ref[...], g, m_ref[...], v_ref[...])

    def term_spec(li, p, by_owner):
        def index(l, i, chip_ref):
            rows = jnp.where(l < li, 0, jnp.where(l > li, nb - 1, i))
            return (p, chip_ref[0] if by_owner else 0, rows, 0)
        return pl.BlockSpec((None, None, tr, c), index)

    row = pl.BlockSpec((None, tr, c), lambda l, i, chip_ref: (l, i, 0))
    specs, arrays = [], []
    for li, (terms, p) in enumerate(grads):
        for term in terms:
            specs.append(term_spec(li, p, term.shape[1] == 4))
            arrays.append(term)
    return pl.pallas_call(
        body,
        name=name,
        grid_spec=pltpu.PrefetchScalarGridSpec(
            num_scalar_prefetch=1, grid=(nl, nb),
            in_specs=[row] * 3 + specs + ([ANY] if after is not None else []), out_specs=[row] * 4),
        out_shape=[jax.ShapeDtypeStruct((nl, r, c), F32)] * 4,
        compiler_params=_params("arbitrary", "arbitrary"),
    )(chip, w, m, v, *arrays, *(() if after is None else (after,)))


def _place_shard(parts, layer, dev, out_dtype, name, after=None):
    p = len(parts)
    _, r, c = parts[0].shape
    tr = _row_tile(r, c)

    def body(dev_ref, *refs):
        o_ref = refs[-1]
        x = refs[0][...]
        for pi in range(1, p):
            x = jnp.where(pl.program_id(0) == pi, refs[pi][...], x)
        o_ref[...] = x.astype(o_ref.dtype)

    return pl.pallas_call(
        body,
        name=name,
        grid_spec=pltpu.PrefetchScalarGridSpec(
            num_scalar_prefetch=1,
            grid=(p, r // tr),
            in_specs=[pl.BlockSpec((None, tr, c), lambda pi, i, dev_ref: (layer, i, 0))] * p
            + ([ANY] if after is not None else []),
            out_specs=pl.BlockSpec((None, None, tr, c), lambda pi, i, dev_ref: (pi, dev_ref[0], i, 0)),
        ),
        out_shape=jax.ShapeDtypeStruct((p, N_DEV, r, c), out_dtype),
        compiler_params=_params("parallel", "parallel"),
    )(dev, *parts, *(() if after is None else (after,)))


def _sum_sibling(g, land, core, name):
    p, _, _, r, c = g.shape
    tr = _row_tile(r, c, 1024 * 1024)

    def body(core_ref, g_ref, l_ref, o_ref):
        o_ref[...] = (g_ref[...].astype(F32) + l_ref[...].astype(F32)).astype(o_ref.dtype)

    return pl.pallas_call(
        body,
        name=name,
        grid_spec=pltpu.PrefetchScalarGridSpec(
            num_scalar_prefetch=1,
            grid=(p, 4, r // tr),
            in_specs=[pl.BlockSpec((None, None, None, tr, c), lambda pi, q, i, core_ref: (pi, q, core_ref[0], i, 0)),
                      pl.BlockSpec((None, None, None, tr, c), lambda pi, q, i, core_ref: (pi, q, 0, i, 0))],
            out_specs=pl.BlockSpec((None, None, tr, c), lambda pi, q, i, core_ref: (pi, q, i, 0)),
        ),
        out_shape=jax.ShapeDtypeStruct((p, 4, r, c), BF16),
        compiler_params=_params("parallel", "parallel", "parallel"),
    )(core, g, land)


def _sum_chips(s, lands, chip, name):
    p, _, r, c = s.shape
    tr = _row_tile(r, c)

    def body(chip_ref, s_ref, l0_ref, l1_ref, l2_ref, o_ref):
        total = s_ref[...].astype(F32) + l0_ref[...].astype(F32)
        o_ref[...] = total + l1_ref[...].astype(F32) + l2_ref[...].astype(F32)

    land_spec = pl.BlockSpec((None, None, tr, c), lambda pi, i, chip_ref: (pi, 0, i, 0))
    return pl.pallas_call(
        body,
        name=name,
        grid_spec=pltpu.PrefetchScalarGridSpec(
            num_scalar_prefetch=1,
            grid=(p, r // tr),
            in_specs=[pl.BlockSpec((None, None, tr, c), lambda pi, i, chip_ref: (pi, chip_ref[0], i, 0)),
                      land_spec, land_spec, land_spec],
            out_specs=pl.BlockSpec((None, tr, c), lambda pi, i, chip_ref: (pi, i, 0)),
        ),
        out_shape=jax.ShapeDtypeStruct((p, r, c), F32),
        compiler_params=_params("parallel", "parallel"),
    )(chip, s, *lands)


def _small_reduce_adam(gathered, w, m, v, name):
    _, r, c = gathered.shape
    tr = _row_tile(r, c)

    def body(p_ref, w_ref, m_ref, v_ref, g_ref, d_ref, nm_ref, nv_ref):
        g = p_ref[0]
        for j in range(1, N_DEV):
            g = g + p_ref[j]
        g_ref[...] = g
        d_ref[...], nm_ref[...], nv_ref[...] = _adam_math(w_ref[...], g, m_ref[...], v_ref[...])

    row = pl.BlockSpec((tr, c), lambda i: (i, 0))
    return pl.pallas_call(
        body,
        name=name,
        grid=(r // tr,),
        in_specs=[pl.BlockSpec((N_DEV, tr, c), lambda i: (0, i, 0)), row, row, row],
        out_specs=[row] * 4,
        out_shape=[jax.ShapeDtypeStruct((r, c), F32)] * 4,
        compiler_params=_params("parallel"),
    )(gathered, w, m, v)


def _place():
    return lax.axis_index("x"), lax.axis_index("y"), lax.axis_index("c")


HBM =pl.BlockSpec(memory_space=pltpu.HBM)
SEM = pl.BlockSpec(memory_space=pltpu.SEMAPHORE)
TOKEN = pl.BlockSpec(memory_space=pltpu.VMEM)
EFFECT = pltpu.SideEffectType.DATAFLOW_SIDE_EFFECTING


def _in_hbm(a):
    return pltpu.with_memory_space_constraint(a, pltpu.HBM)


_FLIPS = {"me": (0, 0, 0), "s": (0, 0, 1), "x": (1, 0, 0), "y": (0, 1, 0), "d": (1, 1, 0)}
GATHER_STAGES = (
    (("s", "me", "all"), ("x", "me", "all"), ("y", "me", "all")),
    (("s", "x", "all"), ("s", "y", "all"), ("y", "x", "first"), ("x", "y", "second")),
    (("s", "d", "all"),),
)


def _flipped(place, *names):
    out = list(place)
    for name in names:
        out = [1 - p if f else p for p, f in zip(out, _FLIPS[name])]
    return tuple(out)


def _block_part(ref, place, part):
    px, py, pc = place
    rows = ref.shape[2]
    span = {"all": pl.ds(0, rows), "first": pl.ds(0, rows // 2), "second": pl.ds(rows // 2, rows // 2)}[part]
    return ref.at[:, pl.ds(4 * px + 2 * py + pc, 1), span]


def _split_start(bufs, moves, name, after=None):
    n, nm = len(bufs), len(moves)
    extra = 0 if after is None else 1

    def body(*refs):
        ssem, rsem = refs[n + extra], refs[n + extra + 1]
        outs, token = refs[n + extra + 2:2 * n + extra + 2], refs[2 * n + extra + 2]
        me = _place()
        for a in range(n):
            for k, (to, owner, part) in enumerate(moves):
                piece = _block_part(outs[a], _flipped(me, owner), part)
                pltpu.make_async_remote_copy(
                    src_ref=piece, dst_ref=piece, send_sem=ssem.at[nm * a + k], recv_sem=rsem.at[nm * a + k],
                    device_id=_flipped(me, to), device_id_type=MESH).start()
        token[...] = jnp.zeros_like(token)

    outs = pl.pallas_call(
        body,
        name=name,
        in_specs=[HBM] * n + [ANY] * extra,
        out_specs=[SEM, SEM] + [HBM] * n + [TOKEN],
        out_shape=[pltpu.SemaphoreType.DMA((nm * n,))] * 2 + [pltpu.HBM(b.shape, b.dtype) for b in bufs]
        + [jax.ShapeDtypeStruct((8, LANES), F32)],
        input_output_aliases={i: 2 + i for i in range(n)},
        compiler_params=pltpu.CompilerParams(has_side_effects=EFFECT),
    )(*[_in_hbm(b) for b in bufs], *(() if after is None else (after,)))
    return outs[0], outs[1], list(outs[2:2 + n]), outs[-1]


def _split_wait(send_sems, recv_sems, bufs, moves, after, name):
    n, nm = len(bufs), len(moves)

    def body(*refs):
        ins, ssem, rsem = refs[:n], refs[n], refs[n + 1]
        me = _place()
        for a in range(n):
            for k, (to, owner, part) in enumerate(moves):
                landed = _block_part(ins[a], _flipped(me, owner, to), part)
                cp = pltpu.make_async_remote_copy(
                    src_ref=landed, dst_ref=landed, send_sem=ssem.at[nm * a + k], recv_sem=rsem.at[nm * a + k],
                    device_id=_flipped(me, to), device_id_type=MESH)
                cp.wait_send()
                cp.wait_recv()

    return pl.pallas_call(
        body,
        name=name,
        in_specs=[HBM] * n + [SEM, SEM, ANY],
        out_specs=[HBM] * n,
        out_shape=[pltpu.HBM(b.shape, b.dtype) for b in bufs],
        input_output_aliases={i: i for i in range(n)},
        compiler_params=pltpu.CompilerParams(has_side_effects=EFFECT),
    )(*bufs, send_sems, recv_sems, after)


def _chips_start(sums, name, after=None):
    n = len(sums)
    extra = 0 if after is None else 1

    def body(*refs):
        refs = refs[:4 * n] + refs[4 * n + extra:]
        ssem, rsem = refs[4 * n], refs[4 * n + 1]
        src, land = refs[4 * n + 2:5 * n + 2], refs[5 * n + 2:8 * n + 2]
        token = refs[8 * n + 2]
        x, y, c = _place()
        chips = [(1 - x, y), (x, 1 - y), (1 - x, 1 - y)]
        for a in range(n):
            for k, (px, py) in enumerate(chips):
                pltpu.make_async_remote_copy(
                    src_ref=src[a].at[:, pl.ds(2 * px + py, 1)], dst_ref=land[3 * a + k], send_sem=ssem.at[3 * a + k],
                    recv_sem=rsem.at[3 * a + k], device_id=(px, py, c), device_id_type=MESH).start()
        token[...] = jnp.zeros_like(token)

    lands = []
    for s in sums:
        lands += [lax.empty((s.shape[0], 1) + s.shape[2:], s.dtype) for _ in range(3)]
    outs = pl.pallas_call(
        body,
        name=name,
        in_specs=[HBM] * (4 * n) + [ANY] * extra,
        out_specs=[SEM, SEM] + [HBM] * (4 * n) + [TOKEN],
        out_shape=[pltpu.SemaphoreType.DMA((3 * n,))] * 2 + [pltpu.HBM(b.shape, b.dtype) for b in list(sums) + lands]
        + [jax.ShapeDtypeStruct((8, LANES), F32)],
        input_output_aliases={i: 2 + i for i in range(4 * n)},
        compiler_params=pltpu.CompilerParams(has_side_effects=EFFECT),
    )(*[_in_hbm(b) for b in list(sums) + lands], *(() if after is None else (after,)))
    return outs[0], outs[1], list(outs[2:2 + n]), list(outs[2 + n:2 + 4 * n]), outs[-1]


def _chips_wait(send_sems, recv_sems, sums, lands, after, name):
    n = len(sums)

    def body(*refs):
        src, land = refs[:n], refs[n:4 * n]
        ssem, rsem = refs[4 * n], refs[4 * n + 1]
        x, y, c = _place()
        chips = [(1 - x, y), (x, 1 - y), (1 - x, 1 - y)]
        for a in range(n):
            for k, (px, py) in enumerate(chips):
                cp = pltpu.make_async_remote_copy(
                    src_ref=src[a].at[:, pl.ds(2 * px + py, 1)], dst_ref=land[3 * a + k], send_sem=ssem.at[3 * a + k],
                    recv_sem=rsem.at[3 * a + k], device_id=(px, py, c), device_id_type=MESH)
                cp.wait_send()
                cp.wait_recv()

    both = list(sums) + list(lands)
    outs = pl.pallas_call(
        body,
        name=name,
        in_specs=[HBM] * (4 * n) + [SEM, SEM, ANY],
        out_specs=[HBM] * (4 * n),
        out_shape=[pltpu.HBM(b.shape, b.dtype) for b in both],
        input_output_aliases={i: i for i in range(4 * n)},
        compiler_params=pltpu.CompilerParams(has_side_effects=EFFECT),
    )(*both, send_sems, recv_sems, after)
    return list(outs[:n]), [list(outs[n + 3 * a:n + 3 * a + 3]) for a in range(n)]


def _sibling_start(grads, name):
    n = len(grads)

    def body(*refs):
        ssem, rsem = refs[2 * n], refs[2 * n + 1]
        src, land = refs[2 * n + 2:3 * n + 2], refs[3 * n + 2:4 * n + 2]
        token = refs[4 * n + 2]
        x, y, c = _place()
        for a in range(n):
            pltpu.make_async_remote_copy(
                src_ref=src[a].at[:, :, pl.ds(1 - c, 1)], dst_ref=land[a], send_sem=ssem.at[a], recv_sem=rsem.at[a],
                device_id=(x, y, 1 - c), device_id_type=MESH).start()
        token[...] = jnp.zeros_like(token)

    lands = [lax.empty(g.shape[:2] + (1,) + g.shape[3:], g.dtype) for g in grads]
    both = list(grads) + lands
    outs = pl.pallas_call(
        body,
        name=name,
        in_specs=[HBM] * (2 * n),
        out_specs=[SEM, SEM] + [HBM] * (2 * n) + [TOKEN],
        out_shape=[pltpu.SemaphoreType.DMA((n,))] * 2 + [pltpu.HBM(b.shape, b.dtype) for b in both]
        + [jax.ShapeDtypeStruct((8, LANES), F32)],
        input_output_aliases={i: 2 + i for i in range(2 * n)},
        compiler_params=pltpu.CompilerParams(has_side_effects=EFFECT),
    )(*[_in_hbm(b) for b in both])
    return outs[0], outs[1], list(outs[2:2 + n]), list(outs[2 + n:2 + 2 * n]), outs[-1]


def _sibling_wait(send_sems, recv_sems, grads, lands, after, name):
    n = len(grads)

    def body(*refs):
        src, land = refs[:n], refs[n:2 * n]
        ssem, rsem = refs[2 * n], refs[2 * n + 1]
        x, y, c = _place()
        for a in range(n):
            cp = pltpu.make_async_remote_copy(
                src_ref=src[a].at[:, :, pl.ds(1 - c, 1)], dst_ref=land[a], send_sem=ssem.at[a], recv_sem=rsem.at[a],
                device_id=(x, y, 1 - c), device_id_type=MESH)
            cp.wait_send()
            cp.wait_recv()

    both = list(grads) + list(lands)
    outs = pl.pallas_call(
        body,
        name=name,
        in_specs=[HBM] * (2 * n) + [SEM, SEM, ANY],
        out_specs=[HBM] * (2 * n),
        out_shape=[pltpu.HBM(b.shape, b.dtype) for b in both],
        input_output_aliases={i: i for i in range(2 * n)},
        compiler_params=pltpu.CompilerParams(has_side_effects=EFFECT),
    )(*both, send_sems, recv_sems, after)
    return list(outs[:n]), list(outs[n:])


_SMALL = ("mix_norm", "q_norm", "k_norm", "sinks", "sgu_ln_g", "sgu_ln_b", "w_spatial", "b_spatial", "ffn_norm")


def _pack_rows(a):
    flat = a.reshape(-1)
    pad = (-flat.shape[0]) % LANES
    if pad:
        flat = jnp.pad(flat, (0, pad))
    return flat.reshape(-1, LANES)


def _pack(values):
    rows = jnp.concatenate([_pack_rows(values[k]) for k in _SMALL], axis=0)
    pad = (-rows.shape[0]) % 8
    if pad:
        rows = jnp.pad(rows, ((0, pad), (0, 0)))
    return rows


def _pack_layers(values):
    depth = values[_SMALL[0]].shape[0]
    pieces = []
    for k in _SMALL:
        flat = values[k].reshape(depth, -1)
        pad = (-flat.shape[1]) % LANES
        if pad:
            flat = jnp.pad(flat, ((0, 0), (0, pad)))
        pieces.append(flat.reshape(depth, -1, LANES))
    rows = jnp.concatenate(pieces, axis=1)
    pad = (-rows.shape[1]) % 8
    if pad:
        rows = jnp.pad(rows, ((0, 0), (0, pad), (0, 0)))
    return rows.reshape(-1, LANES)


def _unpack_layers(rows, like, depth):
    per_layer = rows.reshape(depth, -1, LANES)
    out, at = {}, 0
    for k in _SMALL:
        size = like[k].size
        nrows = -(-size // LANES)
        out[k] = per_layer[:, at:at + nrows].reshape(depth, -1)[:, :size].reshape((depth,) + like[k].shape)
        at += nrows
    return out


def _rope_tables(t, wq, wk, after):
    pos = jnp.arange(t, dtype=F32)
    inv_freq = jnp.power(ROPE_THETA, -jnp.arange(0, HEAD_DIM, 2, dtype=F32) / HEAD_DIM)
    ang = pos[:, None] * inv_freq[None, :] + after[0, 0]
    cos, sin = jnp.cos(ang), jnp.sin(ang)
    cos2, sin2 = jnp.concatenate([cos, cos], axis=1), jnp.concatenate([-sin, sin], axis=1)
    return (jnp.tile(cos2, (1, wq // HEAD_DIM)), jnp.tile(sin2, (1, wq // HEAD_DIM)),
            jnp.tile(cos2, (1, wk // HEAD_DIM)), jnp.tile(sin2, (1, wk // HEAD_DIM)))


def kernel(x, mix_norm, w_in, q_norm, k_norm, sinks, sgu_ln_g, sgu_ln_b, w_spatial, b_spatial, w_attn_branch, w_sgu_branch, w_out, ffn_norm, w_gate, w_up, w_down, loss_target, m_mix_norm, m_w_in, m_q_norm, m_k_norm, m_sinks, m_sgu_ln_g, m_sgu_ln_b, m_w_spatial, m_b_spatial, m_w_attn_branch, m_w_sgu_branch, m_w_out, m_ffn_norm, m_w_gate, m_w_up, m_w_down, v_mix_norm, v_w_in, v_q_norm, v_k_norm, v_sinks, v_sgu_ln_g, v_sgu_ln_b, v_w_spatial, v_b_spatial, v_w_attn_branch, v_w_sgu_branch, v_w_out, v_ffn_norm, v_w_gate, v_w_up, v_w_down):
    names = ("mix_norm", "w_in", "q_norm", "k_norm", "sinks", "sgu_ln_g", "sgu_ln_b", "w_spatial", "b_spatial",
             "w_attn_branch", "w_sgu_branch", "w_out", "ffn_norm", "w_gate", "w_up", "w_down")
    weights = dict(zip(names, (mix_norm, w_in, q_norm, k_norm, sinks, sgu_ln_g, sgu_ln_b, w_spatial, b_spatial,
                               w_attn_branch, w_sgu_branch, w_out, ffn_norm, w_gate, w_up, w_down)))
    mom1 = dict(zip(names, (m_mix_norm, m_w_in, m_q_norm, m_k_norm, m_sinks, m_sgu_ln_g, m_sgu_ln_b, m_w_spatial,
                            m_b_spatial, m_w_attn_branch, m_w_sgu_branch, m_w_out, m_ffn_norm, m_w_gate, m_w_up,
                            m_w_down)))
    mom2 = dict(zip(names, (v_mix_norm, v_w_in, v_q_norm, v_k_norm, v_sinks, v_sgu_ln_g, v_sgu_ln_b, v_w_spatial,
                            v_b_spatial, v_w_attn_branch, v_w_sgu_branch, v_w_out, v_ffn_norm, v_w_gate, v_w_up,
                            v_w_down)))
    depth = w_in.shape[0]
    _, t, d = x.shape
    n_q_heads = sinks.shape[1]
    wq = n_q_heads * HEAD_DIM
    wk = wq // Q_PER_KV
    ws = sgu_ln_g.shape[1]
    ng = ws // LANES
    off_u = wq + 2 * wk
    off_g = off_u + 2 * ws
    px, py, pc = _place()
    core = pc.astype(jnp.int32)[None]
    chip = (2 * px + py).astype(jnp.int32)[None]
    dev = (4 * px + 2 * py + pc).astype(jnp.int32)[None]

    layers = range(depth)
    chunks = ((0,), (1, 2, 3), (4,), (5,))
    sources = [[jnp.swapaxes(w_in, 1, 2)], [jnp.swapaxes(w_attn_branch, 1, 2)], [jnp.swapaxes(w_sgu_branch, 1, 2)],
               [w_out], [jnp.swapaxes(w_gate, 1, 2), jnp.swapaxes(w_up, 1, 2)], [w_down]]
    stream = [(l, ci) for l in layers for ci in range(len(chunks))]
    placed, state, token = {}, {}, None

    def send(key, after):
        state[key] = _split_start(placed[key], GATHER_STAGES[0], "gather_send_%d_%d" % key, after)
        return state[key][3]

    def advance(key, after, stage):
        send_sems, recv_sems, bufs, _ = state[key]
        bufs = _split_wait(send_sems, recv_sems, bufs, GATHER_STAGES[stage - 1], after, "gather_wait%d_%d_%d" % (stage, *key))
        state[key] = _split_start(bufs, GATHER_STAGES[stage], "gather_pass%d_%d_%d" % (stage, *key))
        return state[key][3]

    def relay(key, after):
        tok = advance(key, after, 1)
        at = stream.index(key)
        for later in stream[at + 2:at + 3] if at else stream[1:3]:
            tok = send(later, tok)
        return tok

    def ready(key, after):
        send_sems, recv_sems, bufs, _ = state.pop(key)
        bufs = _split_wait(send_sems, recv_sems, bufs, GATHER_STAGES[2], after, "gather_wait3_%d_%d" % key)
        return [f.reshape(f.shape[0] * f.shape[1] * f.shape[2], f.shape[3]) for f in bufs]

    for key in stream:
        l, ci = key
        placed[key] = [_place_shard(sources[a], l, dev, BF16, f"place_shard_{l}_{a}",
                                    after=token if a == chunks[ci][0] else None) for a in chunks[ci]]
        token = send(key, None) if key == stream[0] else placed[key][-1]

    tables = _rope_tables(t, wq, wk, state[stream[0]][3])
    tables_built = sum(table[:8, :LANES] for table in tables)

    saved = []
    xl = x[0]
    going = relay((0, 0), tables_built)
    going = advance((0, 0), going, 2)
    for l in layers:
        gq = jnp.tile(q_norm[l], n_q_heads)[None]
        gk = jnp.tile(k_norm[l], n_q_heads // Q_PER_KV)[None]
        bt = b_spatial[l].T
        h = _rmsnorm_fwd(xl, mix_norm[l][None], f"mix_norm_fwd_{l}", after=going)
        (win_t,) = ready((l, 0), h)
        proj = _mm(h, win_t, "nt", F32, f"in_proj_{l}")
        going = relay((l, 1), proj)
        attn = _attn_fwd(proj, tables, gq, gk, sinks[l], wq, wk, f"attn_fwd_{l}", after=going)
        going = advance((l, 1), attn, 2)
        sgu = _sgu_fwd(proj, sgu_ln_g[l][None], sgu_ln_b[l][None], w_spatial[l], bt, off_u, ws, f"sgu_fwd_{l}",
                       after=going)
        wab_t, wsb_t, wo = ready((l, 1), sgu)
        going = relay((l, 2), wo)
        br_a, br_b, merged = _branches_fwd(attn, sgu, wab_t, wsb_t, proj, off_g, f"branches_{l}", after=going)
        x1 = _mm(merged, wo, "nn", F32, f"out_proj_{l}", residual=xl)
        going = advance((l, 2), x1, 2)
        h2 = _rmsnorm_fwd(x1, ffn_norm[l][None], f"ffn_norm_fwd_{l}", after=going)
        (wgu_t,) = ready((l, 2), h2)
        going = relay((l, 3), h2)
        gu, act = _gate_up_fwd(h2, wgu_t, f"gate_up_{l}", after=going)
        going = advance((l, 3), act, 2)
        if l + 1 < depth:
            going = relay((l + 1, 0), going)
        (wd,) = ready((l, 3), going)
        x2 = _mm(act, wd, "nn", F32, f"down_proj_{l}", residual=x1)
        if l + 1 < depth:
            going = advance((l + 1, 0), x2, 2)
        saved.append(dict(x0=xl, h=h, proj=proj, attn=attn, sgu=sgu, br_a=br_a, br_b=br_b, merged=merged, x1=x1,
                          h2=h2, gu=gu, act=act, gq=gq, gk=gk, bt=bt, win_t=win_t, wab_t=wab_t, wsb_t=wsb_t, wo=wo,
                          wgu_t=wgu_t, wd=wd))
        xl = x2

    loss_part, dx, dx16 = _loss_and_grad(xl, loss_target[0], "loss")
    loss = lax.psum(loss_part[0, 0], ("x", "y", "c"))

    def sibling_start(grads, tag):
        shaped = []
        for g, p in grads:
            rows, c = g.shape
            shaped.append(g.reshape(p, 4, 2, rows // (8 * p), c))
        send_sems, recv_sems, shaped, lands, tok = _sibling_start(shaped, f"rs_sibling_start_{tag}")
        return (send_sems, recv_sems, shaped, lands, tag), tok

    def chips_start(state, after, first=None):
        send_sems, recv_sems, shaped, lands, tag = state
        shaped, lands = _sibling_wait(send_sems, recv_sems, shaped, lands, after, f"rs_sibling_wait_{tag}")
        sums = [_sum_sibling(g, o, core, f"rs_add_sibling_{tag}_{a}") for a, (g, o) in enumerate(zip(shaped, lands))]
        gate = None if first is None else first(sums[0])
        send_sems, recv_sems, sums, lands, tok = _chips_start(sums, f"rs_chips_start_{tag}", after=gate)
        return (send_sems, recv_sems, sums, lands, tag), tok

    def scatter_finish(state, after):
        send_sems, recv_sems, sums, lands, tag = state
        sums, lands = _chips_wait(send_sems, recv_sems, sums, lands, after, f"rs_chips_wait_{tag}")
        return [[s] + o for s, o in zip(sums, lands)]

    in_flight = [dict() for _ in layers]
    small_grads = [None] * depth
    tok, swap_in = None, None
    for l in reversed(layers):
        s = saved[l]
        dgu = _gate_up_bwd(dx16, s["wd"], s["gu"], f"d_gate_up_{l}", after=tok)
        if swap_in is not None:
            in_flight[l + 1]["in"], tok = chips_start(swap_in, dgu)
        g_wd = _mm(s["act"], dx16, "tn", BF16, f"g_w_down_{l}", after=tok)
        dh2 = _mm(dgu, s["wgu_t"], "nn", F32, f"d_h2_{l}", after=g_wd)
        g_wgu_t = _mm(dgu, s["h2"], "tn", BF16, f"g_w_gate_up_{l}", after=dh2)
        swap, tok_s = sibling_start([(g_wd, 1), (g_wgu_t, 2)], f"{l}_gate_up")
        dx1, dx1_16, g_ffn = _rmsnorm_bwd(s["x1"], ffn_norm[l][None], dh2, dx, f"ffn_norm_bwd_{l}", after=tok_s)
        d_a, d_b, dla, dlb = _branches_bwd(dx1_16, s["wo"], s["br_a"], s["br_b"], s["proj"], off_g,
                                           f"d_branches_{l}")
        in_flight[l]["gate_up"], tok = chips_start(swap, d_a)
        g_wo = _mm(s["merged"], dx1_16, "tn", BF16, f"g_w_out_{l}", after=tok)
        dattn = _mm(d_a, s["wab_t"], "nn", F32, f"d_attn_{l}", after=g_wo)
        g_wab_t = _mm(d_a, s["attn"], "tn", BF16, f"g_w_attn_branch_{l}")
        dsgu = _mm(d_b, s["wsb_t"], "nn", F32, f"d_sgu_{l}")
        g_wsb_t = _mm(d_b, s["sgu"], "tn", BF16, f"g_w_sgu_branch_{l}")
        swap, tok_s = sibling_start([(g_wab_t, 1), (g_wsb_t, 1), (g_wo, 1)], f"{l}_mix")
        dq, dk, dv, g_gq, g_gk, g_sinks = _attn_bwd(s["proj"], dattn, tables, s["gq"], s["gk"], sinks[l], wq, wk,
                                                    f"attn_bwd_{l}", after=tok_s)
        du, dvv, g_lng, g_lnb, g_ws, g_bs = _sgu_bwd(s["proj"], dsgu, sgu_ln_g[l][None], sgu_ln_b[l][None],
                                                     w_spatial[l], s["bt"], off_u, ws, f"sgu_bwd_{l}")
        dproj = jnp.concatenate([dq, dk.astype(BF16), dv.astype(BF16), du, dvv, dla, dlb], axis=1)
        dh = _mm(dproj, s["win_t"], "nn", F32, f"d_h_{l}")
        in_flight[l]["mix"], tok = chips_start(swap, dh)
        g_win_t = _mm(dproj, s["h"], "tn", BF16, f"g_w_in_{l}", after=tok)
        swap_in, tok = sibling_start([(g_win_t, 1)], f"{l}_in")
        dx, dx16, g_mix = _rmsnorm_bwd(s["x0"], mix_norm[l][None], dh, dx1, f"mix_norm_bwd_{l}", after=tok)
        small_grads[l] = dict(
            mix_norm=g_mix[0], q_norm=g_gq[0].reshape(n_q_heads, HEAD_DIM).sum(0),
            k_norm=g_gk[0].reshape(n_q_heads // Q_PER_KV, HEAD_DIM).sum(0), sinks=g_sinks[0, :n_q_heads],
            sgu_ln_g=g_lng[0], sgu_ln_b=g_lnb[0], w_spatial=g_ws, b_spatial=g_bs[:, 0, :], ffn_norm=g_ffn[0])
    grad_x = dx[None]

    result = {key: {} for key in ("grad", "delta", "m", "v")}
    layer_like = {k: weights[k][0] for k in _SMALL}
    packed_g = jnp.concatenate([_pack(small_grads[l]) for l in layers], axis=0)
    small_buf = _place_shard([packed_g[None]], 0, dev, F32, "place_small_grads", after=tok)
    send_sems, recv_sems, small_bufs, tok = _split_start([small_buf], GATHER_STAGES[0], "gather_send_small")
    small_state = [(send_sems, recv_sems, small_bufs)]

    def small_stage(stage, after):
        ssem, rsem, bufs = small_state[0]
        bufs = _split_wait(ssem, rsem, bufs, GATHER_STAGES[stage - 1], after, f"gather_wait{stage}_small")
        ssem, rsem, bufs, token = _split_start(bufs, GATHER_STAGES[stage], f"gather_pass{stage}_small")
        small_state[0] = (ssem, rsem, bufs)
        return token

    in_flight[0]["in"], tok = chips_start(swap_in, tok, first=functools.partial(small_stage, 1))

    def update(k, grads, transposed, after):
        view = (lambda a: jnp.swapaxes(a, 1, 2)) if transposed else (lambda a: a)
        outs = _adam(view(weights[k]), grads, view(mom1[k]), view(mom2[k]), chip, f"adam_{k}", after=after)
        for key, val in zip(("grad", "delta", "m", "v"), outs):
            result[key][k] = view(val)
        return outs[3]

    def plain(terms, tag):
        s, lands = terms[0], terms[1:]
        g = _sum_chips(s, lands, chip, f"rs_add_chips_{tag}")
        return [jnp.swapaxes(g, 1, 2)[:, None]]

    gate_up = [scatter_finish(in_flight[l]["gate_up"], tok) for l in reversed(layers)][::-1]
    tok = update("w_down", [(gate_up[l][0], 0) for l in layers], False, None)
    tok = small_stage(2, tok)
    tok = update("w_gate", [(gate_up[l][1], 0) for l in layers], True, tok)
    tok = update("w_up", [(gate_up[l][1], 1) for l in layers], True, tok)
    mix =[scatter_finish(in_flight[l]["mix"], tok) for l in reversed(layers)][::-1]
    tok = update("w_out", [(mix[l][2], 0) for l in layers], False, None)
    tok = update("w_attn_branch", [(plain(mix[l][0], f"{l}_attn_branch"), 0) for l in layers], False, tok)
    tok = update("w_sgu_branch", [(plain(mix[l][1], f"{l}_sgu_branch"), 0) for l in layers], False, tok)

    packed = [_pack_layers(src) for src in (weights, mom1, mom2)]
    (gathered_small,) = _split_wait(*small_state[0], GATHER_STAGES[2], tok, "gather_wait3_small")
    small = _small_reduce_adam(gathered_small[0], *packed, "small_reduce_adam")
    for key, rows in zip(("grad", "delta", "m", "v"), small):
        result[key].update(_unpack_layers(rows, layer_like, depth))

    last = [scatter_finish(in_flight[l]["in"], small[0]) for l in reversed(layers)][::-1]
    update("w_in", [(last[l][0], 0) for l in layers], True, result["v"]["ffn_norm"])

    return (loss, grad_x, *[result["grad"][k] for k in names], *[result["delta"][k] for k in names],
            *[result["m"][k] for k in names], *[result["v"][k] for k in names])
```
